```python
import math
import jax, jax.numpy as jnp
from jax import lax
import numpy as np

D_MODEL = 1024
BATCH = 8
SEQ = 2048
DEPTH = 4

N_BRANCH = 4
MIX_WIDTH = D_MODEL // N_BRANCH
CONV_WIDTH = 3
SSM_GROUP = 16
SSM_GROUPS = MIX_WIDTH // SSM_GROUP
SSM_STATE = 64
DT_MIN = 1e-3
DT_MAX = 1e-1
LAMBDA_RE_MAX = -1e-4
POOL_WINDOWS = (2, 4, 8, 16)
POOL_GROUP = MIX_WIDTH // len(POOL_WINDOWS)
SB_HEAD_DIM = 64
SB_HEADS = MIX_WIDTH // SB_HEAD_DIM
Q_BLOCK = 128
D_FF = 2816
FFN_RES_WEIGHT = 0.5
N_SUB = 3
EPS = 1e-6

CONV_COLS = 3 * MIX_WIDTH
SSM_COLS = MIX_WIDTH
POOL_COLS = MIX_WIDTH
SB_COLS = 3 * MIX_WIDTH
GATE_COLS = N_BRANCH * D_MODEL
IN_COLS = CONV_COLS + SSM_COLS + POOL_COLS + SB_COLS + GATE_COLS
IN_SPLITS = (CONV_COLS, CONV_COLS + SSM_COLS, CONV_COLS + SSM_COLS + POOL_COLS,
             CONV_COLS + SSM_COLS + POOL_COLS + SB_COLS)

kernel_name = "hybrid_parallel_gated_mixer_trunk"


def rmsnorm(x, g):
    xf = x.astype(jnp.float32)
    y = xf * lax.rsqrt(jnp.mean(xf * xf, axis=-1, keepdims=True) + EPS)
    return (y * g.astype(jnp.float32)).astype(x.dtype)


def modulate(h, shift, scale):
    return h * (1.0 + scale[:, None, :]) + shift[:, None, :]


def swiglu_ffn(h, w_in, w_out):
    a, b = jnp.split(h @ w_in, 2, axis=-1)
    return (jax.nn.silu(a) * b) @ w_out


def short_conv_mixer(p, conv_w, w_out):
    b_g, c_g, v = jnp.split(p, 3, axis=-1)
    u = c_g * v
    L = u.shape[1]
    up = jnp.pad(u, ((0, 0), (CONV_WIDTH - 1, 0), (0, 0)))
    y = conv_w[0] * up[:, 0:L]
    for k in range(1, CONV_WIDTH):
        y = y + conv_w[k] * up[:, k:k + L]
    return (b_g * y) @ w_out


def s5_mixer(u, lam_re, lam_im, log_dt, b_re, b_im, c_re, c_im, d_skip, w_glu):
    f32 = jnp.float32
    Bsz, L, W = u.shape
    uf = u.astype(f32).reshape(Bsz, L, SSM_GROUPS, SSM_GROUP)
    lr = jnp.minimum(lam_re.astype(f32), LAMBDA_RE_MAX)
    li = lam_im.astype(f32)
    dt = jnp.exp(log_dt.astype(f32))[:, None]
    mag = jnp.exp(lr * dt)
    ab_re = mag * jnp.cos(li * dt)
    ab_im = mag * jnp.sin(li * dt)
    den = lr * lr + li * li
    nr = ab_re - 1.0
    f_re = (nr * lr + ab_im * li) / den
    f_im = (ab_im * lr - nr * li) / den
    br = b_re.astype(f32)
    bi = b_im.astype(f32)
    bb_re = f_re[..., None] * br - f_im[..., None] * bi
    bb_im = f_re[..., None] * bi + f_im[..., None] * br
    bu_re = jnp.einsum('blgh,gph->blgp', uf, bb_re)
    bu_im = jnp.einsum('blgh,gph->blgp', uf, bb_im)
    a_re = jnp.broadcast_to(ab_re, bu_re.shape)
    a_im = jnp.broadcast_to(ab_im, bu_im.shape)

    def combine(e1, e2):
        a1r, a1i, b1r, b1i = e1
        a2r, a2i, b2r, b2i = e2
        return (a2r * a1r - a2i * a1i,
                a2r * a1i + a2i * a1r,
                a2r * b1r - a2i * b1i + b2r,
                a2r * b1i + a2i * b1r + b2i)

    _, _, s_re, s_im = lax.associative_scan(combine, (a_re, a_im, bu_re, bu_im), axis=1)
    y = (jnp.einsum('blgp,ghp->blgh', s_re, c_re.astype(f32))
         - jnp.einsum('blgp,ghp->blgh', s_im, c_im.astype(f32)))
    y = y.reshape(Bsz, L, W) + d_skip.astype(f32) * uf.reshape(Bsz, L, W)
    y = jax.nn.gelu(y).astype(u.dtype)
    a, g = jnp.split(y @ w_glu, 2, axis=-1)
    return a * jax.nn.sigmoid(g)


def pool_mixer(u, w_pool, pool_scale, w_out):
    f32 = jnp.float32
    Bsz, L, W = u.shape
    uf = u.astype(f32).reshape(Bsz, L, len(POOL_WINDOWS), POOL_GROUP)
    cs = jnp.cumsum(uf, axis=1)
    pos = jnp.arange(L)
    outs = []
    for gi, w in enumerate(POOL_WINDOWS):
        cg = cs[:, :, gi]
        lag = jnp.pad(cg, ((0, 0), (w, 0), (0, 0)))[:, :L]
        cnt = jnp.minimum(pos + 1, w).astype(f32)[None, :, None]
        outs.append((cg - lag) / cnt - uf[:, :, gi])
    pooled = jnp.stack(outs, axis=2)
    mixed = jnp.einsum('blgc,gcd->blgd', pooled, w_pool.astype(f32)).reshape(Bsz, L, W)
    return (mixed * pool_scale.astype(f32)).astype(u.dtype) @ w_out


def stick_breaking_attention(p, w_out):
    f32 = jnp.float32
    Bsz, L, _ = p.shape
    q, k, v = jnp.split(p, 3, axis=-1)
    q = q.astype(f32).reshape(Bsz, L, SB_HEADS, SB_HEAD_DIM) * (SB_HEAD_DIM ** -0.5)
    k = k.astype(f32).reshape(Bsz, L, SB_HEADS, SB_HEAD_DIM)
    v = v.astype(f32).reshape(Bsz, L, SB_HEADS, SB_HEAD_DIM)
    outs = []
    for start in range(0, L, Q_BLOCK):
        end = start + Q_BLOCK
        z = jnp.einsum('bqhd,bkhd->bhqk', q[:, start:end], k[:, :end])
        t_idx = jnp.arange(start, end)[:, None]
        s_idx = jnp.arange(end)[None, :]
        mask = s_idx < t_idx
        log_keep = jnp.where(mask, jax.nn.log_sigmoid(-z), 0.0)
        log_w = jax.nn.log_sigmoid(z) + lax.cumsum(log_keep, axis=3, reverse=True) - log_keep
        a = jnp.where(mask, jnp.exp(log_w), 0.0)
        outs.append(jnp.einsum('bhqk,bkhd->bqhd', a, v[:, :end]))
    o = jnp.concatenate(outs, axis=1).reshape(Bsz, L, MIX_WIDTH).astype(p.dtype)
    return o @ w_out


def _fwd_setup_inputs(seed: int = 0) -> dict:
    key = jax.random.key(seed)
    ks = jax.random.split(key, 32)
    f32 = jnp.float32

    def nrm(k, shape, fan_in, gain=1.0):
        return jax.random.normal(k, shape, f32) * (gain * fan_in ** -0.5)

    W = MIX_WIDTH
    x = jax.random.normal(ks[0], (BATCH, SEQ, D_MODEL), f32)
    c = jax.random.normal(ks[1], (BATCH, D_MODEL), f32)
    w_ada = nrm(ks[2], (DEPTH, D_MODEL, N_SUB * 3 * D_MODEL), D_MODEL, 0.1)
    b_ada = 0.01 * jax.random.normal(ks[3], (DEPTH, N_SUB * 3 * D_MODEL), f32)
    g_pre = 1.0 + 0.02 * jax.random.normal(ks[4], (DEPTH, N_SUB, D_MODEL), f32)
    g_post = 1.0 + 0.02 * jax.random.normal(ks[5], (DEPTH, N_SUB, D_MODEL), f32)
    w_ff_in = nrm(ks[6], (DEPTH, 2, D_MODEL, 2 * D_FF), D_MODEL)
    w_ff_out = nrm(ks[7], (DEPTH, 2, D_FF, D_MODEL), D_FF)
    w_in = nrm(ks[8], (DEPTH, D_MODEL, IN_COLS), D_MODEL)
    conv_w = nrm(ks[9], (DEPTH, CONV_WIDTH, W), CONV_WIDTH)
    w_conv_out = nrm(ks[10], (DEPTH, W, D_MODEL), W)
    lam_re = -0.5 + 0.01 * jax.random.normal(ks[11], (DEPTH, SSM_GROUPS, SSM_STATE), f32)
    lam_im = (math.pi * jnp.arange(SSM_STATE, dtype=f32))[None, None, :] \
        + 0.01 * jax.random.normal(ks[12], (DEPTH, SSM_GROUPS, SSM_STATE), f32)
    log_dt = jax.random.uniform(ks[13], (DEPTH, SSM_GROUPS), f32,
                                math.log(DT_MIN), math.log(DT_MAX))
    ssm_b_re = nrm(ks[14], (DEPTH, SSM_GROUPS, SSM_STATE, SSM_GROUP), 2 * SSM_GROUP)
    ssm_b_im = nrm(ks[15], (DEPTH, SSM_GROUPS, SSM_STATE, SSM_GROUP), 2 * SSM_GROUP)
    ssm_c_re = nrm(ks[16], (DEPTH, SSM_GROUPS, SSM_GROUP, SSM_STATE), 2 * SSM_STATE)
    ssm_c_im = nrm(ks[17], (DEPTH, SSM_GROUPS, SSM_GROUP, SSM_STATE), 2 * SSM_STATE)
    ssm_d = jax.random.normal(ks[18], (DEPTH, W), f32)
    w_glu = nrm(ks[19], (DEPTH, W, 2 * D_MODEL), W)
    w_pool = nrm(ks[20], (DEPTH, len(POOL_WINDOWS), POOL_GROUP, POOL_GROUP), POOL_GROUP)
    pool_scale = 1.0 + 0.1 * jax.random.normal(ks[21], (DEPTH, W), f32)
    w_pool_out = nrm(ks[22], (DEPTH, W, D_MODEL), W)
    w_sb_out = nrm(ks[23], (DEPTH, W, D_MODEL), W)
    w_out = nrm(ks[24], (DEPTH, D_MODEL, D_MODEL), D_MODEL)
    return {"x": x, "c": c, "w_ada": w_ada, "b_ada": b_ada, "g_pre": g_pre, "g_post": g_post,
            "w_ff_in": w_ff_in, "w_ff_out": w_ff_out, "w_in": w_in, "conv_w": conv_w,
            "w_conv_out": w_conv_out, "lam_re": lam_re, "lam_im": lam_im, "log_dt": log_dt,
            "ssm_b_re": ssm_b_re, "ssm_b_im": ssm_b_im, "ssm_c_re": ssm_c_re, "ssm_c_im": ssm_c_im,
            "ssm_d": ssm_d, "w_glu": w_glu, "w_pool": w_pool, "pool_scale": pool_scale,
            "w_pool_out": w_pool_out, "w_sb_out": w_sb_out, "w_out": w_out}


def _fwd_reference(x, c, w_ada, b_ada, g_pre, g_post, w_ff_in, w_ff_out, w_in, conv_w,
              w_conv_out, lam_re, lam_im, log_dt, ssm_b_re, ssm_b_im, ssm_c_re, ssm_c_im,
              ssm_d, w_glu, w_pool, pool_scale, w_pool_out, w_sb_out, w_out):
    Bsz, L, D = x.shape
    c_act = jax.nn.silu(c)
    for l in range(DEPTH):
        ada = (c_act @ w_ada[l] + b_ada[l]).reshape(Bsz, N_SUB, 3, D)

        h = modulate(rmsnorm(x, g_pre[l, 0]), ada[:, 0, 0], ada[:, 0, 1])
        f = swiglu_ffn(h, w_ff_in[l, 0], w_ff_out[l, 0])
        x = x + FFN_RES_WEIGHT * (1.0 + ada[:, 0, 2])[:, None, :] * rmsnorm(f, g_post[l, 0])

        h = modulate(rmsnorm(x, g_pre[l, 1]), ada[:, 1, 0], ada[:, 1, 1])
        p = h @ w_in[l]
        p_conv, p_ssm, p_pool, p_sb, p_gate = jnp.split(p, IN_SPLITS, axis=-1)
        y_a = short_conv_mixer(p_conv, conv_w[l], w_conv_out[l])
        y_b = s5_mixer(p_ssm, lam_re[l], lam_im[l], log_dt[l], ssm_b_re[l], ssm_b_im[l],
                       ssm_c_re[l], ssm_c_im[l], ssm_d[l], w_glu[l])
        y_c = pool_mixer(p_pool, w_pool[l], pool_scale[l], w_pool_out[l])
        y_d = stick_breaking_attention(p_sb, w_sb_out[l])
        gates = jax.nn.sigmoid(p_gate).reshape(Bsz, L, N_BRANCH, D)
        merged = (gates[:, :, 0] * y_a + gates[:, :, 1] * y_b
                  + gates[:, :, 2] * y_c + gates[:, :, 3] * y_d)
        m = merged @ w_out[l]
        x = x + (1.0 + ada[:, 1, 2])[:, None, :] * rmsnorm(m, g_post[l, 1])

        h = modulate(rmsnorm(x, g_pre[l, 2]), ada[:, 2, 0], ada[:, 2, 1])
        f = swiglu_ffn(h, w_ff_in[l, 1], w_ff_out[l, 1])
        x = x + FFN_RES_WEIGHT * (1.0 + ada[:, 2, 2])[:, None, :] * rmsnorm(f, g_post[l, 2])
    return x


import jax as _jax
import jax.numpy as _jnp

TWIN_FORMAT = 'train_step'
FWD_PARAMS = ['x', 'c', 'w_ada', 'b_ada', 'g_pre', 'g_post', 'w_ff_in', 'w_ff_out', 'w_in', 'conv_w', 'w_conv_out', 'lam_re', 'lam_im', 'log_dt', 'ssm_b_re', 'ssm_b_im', 'ssm_c_re', 'ssm_c_im', 'ssm_d', 'w_glu', 'w_pool', 'pool_scale', 'w_pool_out', 'w_sb_out', 'w_out']
TWIN_WEIGHTS = ['w_ada', 'b_ada', 'g_pre', 'g_post', 'w_ff_in', 'w_ff_out', 'w_in', 'conv_w', 'w_conv_out', 'lam_re', 'lam_im', 'log_dt', 'ssm_b_re', 'ssm_b_im', 'ssm_c_re', 'ssm_c_im', 'ssm_d', 'w_glu', 'w_pool', 'pool_scale', 'w_pool_out', 'w_sb_out', 'w_out']
TWIN_DIFF_INPUT = 'x'
TWIN_INPUTS = ['x', 'c', 'w_ada', 'b_ada', 'g_pre', 'g_post', 'w_ff_in', 'w_ff_out', 'w_in', 'conv_w', 'w_conv_out', 'lam_re', 'lam_im', 'log_dt', 'ssm_b_re', 'ssm_b_im', 'ssm_c_re', 'ssm_c_im', 'ssm_d', 'w_glu', 'w_pool', 'pool_scale', 'w_pool_out', 'w_sb_out', 'w_out', 'loss_target', 'm_w_ada', 'm_b_ada', 'm_g_pre', 'm_g_post', 'm_w_ff_in', 'm_w_ff_out', 'm_w_in', 'm_conv_w', 'm_w_conv_out', 'm_lam_re', 'm_lam_im', 'm_log_dt', 'm_ssm_b_re', 'm_ssm_b_im', 'm_ssm_c_re', 'm_ssm_c_im', 'm_ssm_d', 'm_w_glu', 'm_w_pool', 'm_pool_scale', 'm_w_pool_out', 'm_w_sb_out', 'm_w_out', 'v_w_ada', 'v_b_ada', 'v_g_pre', 'v_g_post', 'v_w_ff_in', 'v_w_ff_out', 'v_w_in', 'v_conv_w', 'v_w_conv_out', 'v_lam_re', 'v_lam_im', 'v_log_dt', 'v_ssm_b_re', 'v_ssm_b_im', 'v_ssm_c_re', 'v_ssm_c_im', 'v_ssm_d', 'v_w_glu', 'v_w_pool', 'v_pool_scale', 'v_w_pool_out', 'v_w_sb_out', 'v_w_out']
TWIN_OUTPUTS = ['loss', 'grad_x', 'grad_w_ada', 'grad_b_ada', 'grad_g_pre', 'grad_g_post', 'grad_w_ff_in', 'grad_w_ff_out', 'grad_w_in', 'grad_conv_w', 'grad_w_conv_out', 'grad_lam_re', 'grad_lam_im', 'grad_log_dt', 'grad_ssm_b_re', 'grad_ssm_b_im', 'grad_ssm_c_re', 'grad_ssm_c_im', 'grad_ssm_d', 'grad_w_glu', 'grad_w_pool', 'grad_pool_scale', 'grad_w_pool_out', 'grad_w_sb_out', 'grad_w_out', 'delta_w_ada', 'delta_b_ada', 'delta_g_pre', 'delta_g_post', 'delta_w_ff_in', 'delta_w_ff_out', 'delta_w_in', 'delta_conv_w', 'delta_w_conv_out', 'delta_lam_re', 'delta_lam_im', 'delta_log_dt', 'delta_ssm_b_re', 'delta_ssm_b_im', 'delta_ssm_c_re', 'delta_ssm_c_im', 'delta_ssm_d', 'delta_w_glu', 'delta_w_pool', 'delta_pool_scale', 'delta_w_pool_out', 'delta_w_sb_out', 'delta_w_out', 'new_m_w_ada', 'new_m_b_ada', 'new_m_g_pre', 'new_m_g_post', 'new_m_w_ff_in', 'new_m_w_ff_out', 'new_m_w_in', 'new_m_conv_w', 'new_m_w_conv_out', 'new_m_lam_re', 'new_m_lam_im', 'new_m_log_dt', 'new_m_ssm_b_re', 'new_m_ssm_b_im', 'new_m_ssm_c_re', 'new_m_ssm_c_im', 'new_m_ssm_d', 'new_m_w_glu', 'new_m_w_pool', 'new_m_pool_scale', 'new_m_w_pool_out', 'new_m_w_sb_out', 'new_m_w_out', 'new_v_w_ada', 'new_v_b_ada', 'new_v_g_pre', 'new_v_g_post', 'new_v_w_ff_in', 'new_v_w_ff_out', 'new_v_w_in', 'new_v_conv_w', 'new_v_w_conv_out', 'new_v_lam_re', 'new_v_lam_im', 'new_v_log_dt', 'new_v_ssm_b_re', 'new_v_ssm_b_im', 'new_v_ssm_c_re', 'new_v_ssm_c_im', 'new_v_ssm_d', 'new_v_w_glu', 'new_v_w_pool', 'new_v_pool_scale', 'new_v_w_pool_out', 'new_v_w_sb_out', 'new_v_w_out']
TWIN_LEAF_KINDS = {'loss': 'loss', 'grad_x': 'grad_x', 'grad_w_ada': 'grad_w', 'grad_b_ada': 'grad_w', 'grad_g_pre': 'grad_w', 'grad_g_post': 'grad_w', 'grad_w_ff_in': 'grad_w', 'grad_w_ff_out': 'grad_w', 'grad_w_in': 'grad_w', 'grad_conv_w': 'grad_w', 'grad_w_conv_out': 'grad_w', 'grad_lam_re': 'grad_w', 'grad_lam_im': 'grad_w', 'grad_log_dt': 'grad_w', 'grad_ssm_b_re': 'grad_w', 'grad_ssm_b_im': 'grad_w', 'grad_ssm_c_re': 'grad_w', 'grad_ssm_c_im': 'grad_w', 'grad_ssm_d': 'grad_w', 'grad_w_glu': 'grad_w', 'grad_w_pool': 'grad_w', 'grad_pool_scale': 'grad_w', 'grad_w_pool_out': 'grad_w', 'grad_w_sb_out': 'grad_w', 'grad_w_out': 'grad_w', 'delta_w_ada': 'delta_w', 'delta_b_ada': 'delta_w', 'delta_g_pre': 'delta_w', 'delta_g_post': 'delta_w', 'delta_w_ff_in': 'delta_w', 'delta_w_ff_out': 'delta_w', 'delta_w_in': 'delta_w', 'delta_conv_w': 'delta_w', 'delta_w_conv_out': 'delta_w', 'delta_lam_re': 'delta_w', 'delta_lam_im': 'delta_w', 'delta_log_dt': 'delta_w', 'delta_ssm_b_re': 'delta_w', 'delta_ssm_b_im': 'delta_w', 'delta_ssm_c_re': 'delta_w', 'delta_ssm_c_im': 'delta_w', 'delta_ssm_d': 'delta_w', 'delta_w_glu': 'delta_w', 'delta_w_pool': 'delta_w', 'delta_pool_scale': 'delta_w', 'delta_w_pool_out': 'delta_w', 'delta_w_sb_out': 'delta_w', 'delta_w_out': 'delta_w', 'new_m_w_ada': 'new_m', 'new_m_b_ada': 'new_m', 'new_m_g_pre': 'new_m', 'new_m_g_post': 'new_m', 'new_m_w_ff_in': 'new_m', 'new_m_w_ff_out': 'new_m', 'new_m_w_in': 'new_m', 'new_m_conv_w': 'new_m', 'new_m_w_conv_out': 'new_m', 'new_m_lam_re': 'new_m', 'new_m_lam_im': 'new_m', 'new_m_log_dt': 'new_m', 'new_m_ssm_b_re': 'new_m', 'new_m_ssm_b_im': 'new_m', 'new_m_ssm_c_re': 'new_m', 'new_m_ssm_c_im': 'new_m', 'new_m_ssm_d': 'new_m', 'new_m_w_glu': 'new_m', 'new_m_w_pool': 'new_m', 'new_m_pool_scale': 'new_m', 'new_m_w_pool_out': 'new_m', 'new_m_w_sb_out': 'new_m', 'new_m_w_out': 'new_m', 'new_v_w_ada': 'new_v', 'new_v_b_ada': 'new_v', 'new_v_g_pre': 'new_v', 'new_v_g_post': 'new_v', 'new_v_w_ff_in': 'new_v', 'new_v_w_ff_out': 'new_v', 'new_v_w_in': 'new_v', 'new_v_conv_w': 'new_v', 'new_v_w_conv_out': 'new_v', 'new_v_lam_re': 'new_v', 'new_v_lam_im': 'new_v', 'new_v_log_dt': 'new_v', 'new_v_ssm_b_re': 'new_v', 'new_v_ssm_b_im': 'new_v', 'new_v_ssm_c_re': 'new_v', 'new_v_ssm_c_im': 'new_v', 'new_v_ssm_d': 'new_v', 'new_v_w_glu': 'new_v', 'new_v_w_pool': 'new_v', 'new_v_pool_scale': 'new_v', 'new_v_w_pool_out': 'new_v', 'new_v_w_sb_out': 'new_v', 'new_v_w_out': 'new_v'}


def _forward(args):
    return _fwd_reference(*[args[k] for k in FWD_PARAMS])


def _output_shape():
    out = _jax.eval_shape(lambda: _forward(_fwd_setup_inputs(0)))
    return out.shape, out.dtype

N_MICROBATCH = 1
ADAM_LR = 0.001
ADAM_B1 = 0.9
ADAM_B2 = 0.999
ADAM_EPS = 1e-08
ADAM_WD = 0.01
ADAM_STEP = 10
PER_EXAMPLE_BATCH_AXIS = {'x': 0, 'c': 0, 'loss_target': 0}
SHARED_INPUTS = []
_WEIGHT_DTYPES = {'w_ada': _jnp.float32, 'b_ada': _jnp.float32, 'g_pre': _jnp.float32, 'g_post': _jnp.float32, 'w_ff_in': _jnp.float32, 'w_ff_out': _jnp.float32, 'w_in': _jnp.float32, 'conv_w': _jnp.float32, 'w_conv_out': _jnp.float32, 'lam_re': _jnp.float32, 'lam_im': _jnp.float32, 'log_dt': _jnp.float32, 'ssm_b_re': _jnp.float32, 'ssm_b_im': _jnp.float32, 'ssm_c_re': _jnp.float32, 'ssm_c_im': _jnp.float32, 'ssm_d': _jnp.float32, 'w_glu': _jnp.float32, 'w_pool': _jnp.float32, 'pool_scale': _jnp.float32, 'w_pool_out': _jnp.float32, 'w_sb_out': _jnp.float32, 'w_out': _jnp.float32}
MOMENT_SCALE = {'w_ada': 1.891761e+00, 'b_ada': 5.510548e+00, 'g_pre': 7.925778e-01, 'g_post': 9.798372e+00, 'w_ff_in': 2.659264e-01, 'w_ff_out': 4.506296e-01, 'w_in': 4.327890e-01, 'conv_w': 8.919480e-01, 'w_conv_out': 4.595548e-01, 'lam_re': 2.117273e-02, 'lam_im': 1.630345e-02, 'log_dt': 9.759558e+00, 'ssm_b_re': 1.285794e-02, 'ssm_b_im': 1.236326e-02, 'ssm_c_re': 2.373911e-02, 'ssm_c_im': 2.499620e-02, 'ssm_d': 7.733736e-01, 'w_glu': 2.561899e-01, 'w_pool': 1.180987e+00, 'pool_scale': 1.326811e+00, 'w_pool_out': 6.060931e-01, 'w_sb_out': 3.229429e-01, 'w_out': 9.036741e-01}


def _to_microbatches(a, axis):
    t = _jnp.moveaxis(a, axis, 0)
    t = t.reshape((N_MICROBATCH, t.shape[0] // N_MICROBATCH) + t.shape[1:])
    return _jnp.moveaxis(t, 1, axis + 1)


def setup_inputs(seed: int = 0) -> dict:
    inp = _fwd_setup_inputs(seed)
    key = _jax.random.fold_in(_jax.random.key(seed), 7919)
    shape, _ = _output_shape()
    out = dict(inp)
    out["loss_target"] = _jax.random.normal(_jax.random.fold_in(key, 0), shape, _jnp.float32)
    for i, name in enumerate(TWIN_WEIGHTS):
        w = inp[name].astype(_jnp.float32)
        if MOMENT_SCALE is None:
            s = _jnp.sqrt(_jnp.mean(_jnp.square(w)) + 1e-30)
        else:
            s = MOMENT_SCALE[name]
        km, kv = _jax.random.split(_jax.random.fold_in(key, i + 1))
        out[name] = w
        out["m_" + name] = s * _jax.random.normal(km, w.shape, _jnp.float32)
        out["v_" + name] = (s * s) * _jax.random.uniform(kv, w.shape, _jnp.float32, 0.5, 1.5)
    if N_MICROBATCH > 1:
        for name, axis in PER_EXAMPLE_BATCH_AXIS.items():
            out[name] = _to_microbatches(out[name], axis)
    return {'x': out['x'], 'c': out['c'], 'w_ada': out['w_ada'], 'b_ada': out['b_ada'], 'g_pre': out['g_pre'], 'g_post': out['g_post'], 'w_ff_in': out['w_ff_in'], 'w_ff_out': out['w_ff_out'], 'w_in': out['w_in'], 'conv_w': out['conv_w'], 'w_conv_out': out['w_conv_out'], 'lam_re': out['lam_re'], 'lam_im': out['lam_im'], 'log_dt': out['log_dt'], 'ssm_b_re': out['ssm_b_re'], 'ssm_b_im': out['ssm_b_im'], 'ssm_c_re': out['ssm_c_re'], 'ssm_c_im': out['ssm_c_im'], 'ssm_d': out['ssm_d'], 'w_glu': out['w_glu'], 'w_pool': out['w_pool'], 'pool_scale': out['pool_scale'], 'w_pool_out': out['w_pool_out'], 'w_sb_out': out['w_sb_out'], 'w_out': out['w_out'], 'loss_target': out['loss_target'], 'm_w_ada': out['m_w_ada'], 'm_b_ada': out['m_b_ada'], 'm_g_pre': out['m_g_pre'], 'm_g_post': out['m_g_post'], 'm_w_ff_in': out['m_w_ff_in'], 'm_w_ff_out': out['m_w_ff_out'], 'm_w_in': out['m_w_in'], 'm_conv_w': out['m_conv_w'], 'm_w_conv_out': out['m_w_conv_out'], 'm_lam_re': out['m_lam_re'], 'm_lam_im': out['m_lam_im'], 'm_log_dt': out['m_log_dt'], 'm_ssm_b_re': out['m_ssm_b_re'], 'm_ssm_b_im': out['m_ssm_b_im'], 'm_ssm_c_re': out['m_ssm_c_re'], 'm_ssm_c_im': out['m_ssm_c_im'], 'm_ssm_d': out['m_ssm_d'], 'm_w_glu': out['m_w_glu'], 'm_w_pool': out['m_w_pool'], 'm_pool_scale': out['m_pool_scale'], 'm_w_pool_out': out['m_w_pool_out'], 'm_w_sb_out': out['m_w_sb_out'], 'm_w_out': out['m_w_out'], 'v_w_ada': out['v_w_ada'], 'v_b_ada': out['v_b_ada'], 'v_g_pre': out['v_g_pre'], 'v_g_post': out['v_g_post'], 'v_w_ff_in': out['v_w_ff_in'], 'v_w_ff_out': out['v_w_ff_out'], 'v_w_in': out['v_w_in'], 'v_conv_w': out['v_conv_w'], 'v_w_conv_out': out['v_w_conv_out'], 'v_lam_re': out['v_lam_re'], 'v_lam_im': out['v_lam_im'], 'v_log_dt': out['v_log_dt'], 'v_ssm_b_re': out['v_ssm_b_re'], 'v_ssm_b_im': out['v_ssm_b_im'], 'v_ssm_c_re': out['v_ssm_c_re'], 'v_ssm_c_im': out['v_ssm_c_im'], 'v_ssm_d': out['v_ssm_d'], 'v_w_glu': out['v_w_glu'], 'v_w_pool': out['v_w_pool'], 'v_pool_scale': out['v_pool_scale'], 'v_w_pool_out': out['v_w_pool_out'], 'v_w_sb_out': out['v_w_sb_out'], 'v_w_out': out['v_w_out']}


def _loss(weights, diff, rest, loss_target):
    with _jax.named_scope("forward"):
        args = {**rest, TWIN_DIFF_INPUT: diff, **{k: w.astype(_WEIGHT_DTYPES[k]) for k, w in weights.items()}}
        y = _forward(args)
    with _jax.named_scope("loss_head"):
        err = _jnp.square(y.astype(_jnp.float32) - loss_target)
        return 0.5 * _jnp.sum(_jnp.mean(err, axis=-1)) if err.ndim else 0.5 * err


def _adamw(w, g, m, v):
    m = ADAM_B1 * m + (1.0 - ADAM_B1) * g
    v = ADAM_B2 * v + (1.0 - ADAM_B2) * _jnp.square(g)
    m_hat = m / (1.0 - ADAM_B1 ** ADAM_STEP)
    v_hat = v / (1.0 - ADAM_B2 ** ADAM_STEP)
    delta = -ADAM_LR * (m_hat / (_jnp.sqrt(v_hat) + ADAM_EPS) + ADAM_WD * w)
    return delta, m, v


def reference(x, c, w_ada, b_ada, g_pre, g_post, w_ff_in, w_ff_out, w_in, conv_w, w_conv_out, lam_re, lam_im, log_dt, ssm_b_re, ssm_b_im, ssm_c_re, ssm_c_im, ssm_d, w_glu, w_pool, pool_scale, w_pool_out, w_sb_out, w_out, loss_target, m_w_ada, m_b_ada, m_g_pre, m_g_post, m_w_ff_in, m_w_ff_out, m_w_in, m_conv_w, m_w_conv_out, m_lam_re, m_lam_im, m_log_dt, m_ssm_b_re, m_ssm_b_im, m_ssm_c_re, m_ssm_c_im, m_ssm_d, m_w_glu, m_w_pool, m_pool_scale, m_w_pool_out, m_w_sb_out, m_w_out, v_w_ada, v_b_ada, v_g_pre, v_g_post, v_w_ff_in, v_w_ff_out, v_w_in, v_conv_w, v_w_conv_out, v_lam_re, v_lam_im, v_log_dt, v_ssm_b_re, v_ssm_b_im, v_ssm_c_re, v_ssm_c_im, v_ssm_d, v_w_glu, v_w_pool, v_pool_scale, v_w_pool_out, v_w_sb_out, v_w_out):
    given = dict(x=x, c=c, w_ada=w_ada, b_ada=b_ada, g_pre=g_pre, g_post=g_post, w_ff_in=w_ff_in, w_ff_out=w_ff_out, w_in=w_in, conv_w=conv_w, w_conv_out=w_conv_out, lam_re=lam_re, lam_im=lam_im, log_dt=log_dt, ssm_b_re=ssm_b_re, ssm_b_im=ssm_b_im, ssm_c_re=ssm_c_re, ssm_c_im=ssm_c_im, ssm_d=ssm_d, w_glu=w_glu, w_pool=w_pool, pool_scale=pool_scale, w_pool_out=w_pool_out, w_sb_out=w_sb_out, w_out=w_out, loss_target=loss_target, m_w_ada=m_w_ada, m_b_ada=m_b_ada, m_g_pre=m_g_pre, m_g_post=m_g_post, m_w_ff_in=m_w_ff_in, m_w_ff_out=m_w_ff_out, m_w_in=m_w_in, m_conv_w=m_conv_w, m_w_conv_out=m_w_conv_out, m_lam_re=m_lam_re, m_lam_im=m_lam_im, m_log_dt=m_log_dt, m_ssm_b_re=m_ssm_b_re, m_ssm_b_im=m_ssm_b_im, m_ssm_c_re=m_ssm_c_re, m_ssm_c_im=m_ssm_c_im, m_ssm_d=m_ssm_d, m_w_glu=m_w_glu, m_w_pool=m_w_pool, m_pool_scale=m_pool_scale, m_w_pool_out=m_w_pool_out, m_w_sb_out=m_w_sb_out, m_w_out=m_w_out, v_w_ada=v_w_ada, v_b_ada=v_b_ada, v_g_pre=v_g_pre, v_g_post=v_g_post, v_w_ff_in=v_w_ff_in, v_w_ff_out=v_w_ff_out, v_w_in=v_w_in, v_conv_w=v_conv_w, v_w_conv_out=v_w_conv_out, v_lam_re=v_lam_re, v_lam_im=v_lam_im, v_log_dt=v_log_dt, v_ssm_b_re=v_ssm_b_re, v_ssm_b_im=v_ssm_b_im, v_ssm_c_re=v_ssm_c_re, v_ssm_c_im=v_ssm_c_im, v_ssm_d=v_ssm_d, v_w_glu=v_w_glu, v_w_pool=v_w_pool, v_pool_scale=v_pool_scale, v_w_pool_out=v_w_pool_out, v_w_sb_out=v_w_sb_out, v_w_out=v_w_out)
    weights = {n: given[n] for n in TWIN_WEIGHTS}
    shared = {n: given[n] for n in SHARED_INPUTS}
    per_example = {n: given[n] for n in ['x', 'c']}
    grad_fn = _jax.value_and_grad(_loss, argnums=(0, 1))

    def one_microbatch(ex, loss_target):
        ex = dict(ex)
        diff = ex.pop(TWIN_DIFF_INPUT)
        return grad_fn(weights, diff, {**shared, **ex}, loss_target)

    if N_MICROBATCH == 1:
        loss, (grad_w, grad_x) = one_microbatch(per_example, given["loss_target"])
    else:
        def body(carry, xs):
            loss_sum, grad_sum = carry
            l_k, (gw_k, gx_k) = one_microbatch(xs[0], xs[1])
            with _jax.named_scope("update"):
                return (loss_sum + l_k, _jax.tree.map(_jnp.add, grad_sum, gw_k)), gx_k

        init = (_jnp.zeros((), _jnp.float32), _jax.tree.map(_jnp.zeros_like, weights))
        (loss, grad_w), grad_x = _jax.lax.scan(body, init, (per_example, given["loss_target"]))
    with _jax.named_scope("update"):
        delta_w, new_m, new_v = {}, {}, {}
        for n in TWIN_WEIGHTS:
            delta_w[n], new_m[n], new_v[n] = _adamw(weights[n], grad_w[n], given["m_" + n], given["v_" + n])
    return (loss, grad_x, *[grad_w[n] for n in TWIN_WEIGHTS], *[delta_w[n] for n in TWIN_WEIGHTS],
            *[new_m[n] for n in TWIN_WEIGHTS], *[new_v[n] for n in TWIN_WEIGHTS])
```

```python
import functools
import math

import jax
import jax.numpy as jnp
import numpy as np
from jax import lax
from jax.experimental import pallas as pl
from jax.experimental.pallas import tpu as pltpu

F32 = jnp.float32
BF16 = jnp.bfloat16
MESH = pl.DeviceIdType.MESH

D = 1024
W = 256
FF = 2816
IN_COLS = 6144
G = 16
GH = 16
GP = 64
NST = G * GP
QB = 128
HEADS = 4
HD = 64
EPS = 1e-6
LAMBDA_RE_MAX = -1e-4
POOL_WINDOWS = (2, 4, 8, 16)
N_CHIP = 4
N_DEV = 8
VMEM_LIMIT = 56 * 1024 * 1024
HIGH = lax.Precision.HIGHEST

ADAM_LR, ADAM_B1, ADAM_B2, ADAM_EPS, ADAM_WD, ADAM_STEP = 0.001, 0.9, 0.999, 1e-08, 0.01, 10


def _call(body, **kw):
    return pl.pallas_call(body, **kw)


def _params(dims=None, **kw):
    return pltpu.CompilerParams(dimension_semantics=dims, vmem_limit_bytes=VMEM_LIMIT, **kw)


def _sds(shape, dtype):
    return jax.ShapeDtypeStruct(shape, dtype)


def _dot(a, b, ca=1, cb=0, precision=None):
    return lax.dot_general(a, b, (((ca,), (cb,)), ((), ())), preferred_element_type=F32, precision=precision)


def _bdot(a, b, ca=1, cb=0):
    return _dot(a.astype(BF16), b.astype(BF16), ca, cb)


def _sigmoid(x):
    return 1.0 / (1.0 + jnp.exp(-x))


def _mm(a, b, *, M, N, K, tm, tn, tk=None, ta=False, tb=False, out_dtype=F32, a_off=(0, 0), b_off=(0, 0), name):
    tk = K if tk is None else tk
    nk = K // tk
    assert M % tm == 0 and N % tn == 0 and K % tk == 0

    def body(a_ref, b_ref, o_ref, *acc):
        part = _bdot(a_ref[...], b_ref[...], 0 if ta else 1, 1 if tb else 0)
        if nk == 1:
            o_ref[...] = part.astype(out_dtype)
            return
        acc_ref = acc[0]
        k = pl.program_id(2)

        @pl.when(k == 0)
        def _():
            acc_ref[...] = part

        @pl.when(k > 0)
        def _():
            acc_ref[...] += part

        @pl.when(k == nk - 1)
        def _():
            o_ref[...] = acc_ref[...].astype(out_dtype)

    if ta:
        a_spec = pl.BlockSpec((tk, tm), lambda i, j, k: (k + a_off[0], i + a_off[1]))
    else:
        a_spec = pl.BlockSpec((tm, tk), lambda i, j, k: (i + a_off[0], k + a_off[1]))
    if tb:
        b_spec = pl.BlockSpec((tn, tk), lambda i, j, k: (j + b_off[0], k + b_off[1]))
    else:
        b_spec = pl.BlockSpec((tk, tn), lambda i, j, k: (k + b_off[0], j + b_off[1]))
    return _call(
        body, name=name, grid=(M // tm, N // tn, nk),
        in_specs=[a_spec, b_spec],
        out_specs=pl.BlockSpec((tm, tn), lambda i, j, k: (i, j)),
        out_shape=_sds((M, N), out_dtype),
        scratch_shapes=[] if nk == 1 else [pltpu.VMEM((tm, tn), F32)],
        compiler_params=_params(("parallel", "parallel", "arbitrary")),
    )(a, b)


def _row_tile(L):
    return min(L, 256)


def _norm_mod(x, g, shift, scale):
    L = x.shape[0]
    tr = _row_tile(L)

    def body(x_ref, g_ref, sh_ref, sc_ref, h_ref):
        xv = x_ref[...]
        r = lax.rsqrt(jnp.mean(xv * xv, axis=-1, keepdims=True) + EPS)
        h_ref[...] = (xv * r * g_ref[...] * (1.0 + sc_ref[...]) + sh_ref[...]).astype(BF16)

    row = pl.BlockSpec((tr, D), lambda i: (i, 0))
    vec = pl.BlockSpec((1, D), lambda i: (0, 0))
    return _call(body, name="norm_mod", grid=(L // tr,), in_specs=[row, vec, vec, vec], out_specs=row,
                 out_shape=_sds((L, D), BF16), compiler_params=_params(("parallel",)))(x, g, shift, scale)


def _norm_mod_bwd(dh, x, g, scale, dx_res):
    L = x.shape[0]
    tr = _row_tile(L)

    def body(dh_ref, x_ref, g_ref, sc_ref, dxr_ref, dx_ref, dsh_ref, dsc_ref, dg_ref):
        i = pl.program_id(0)
        xv = x_ref[...]
        dhv = dh_ref[...]
        r = lax.rsqrt(jnp.mean(xv * xv, axis=-1, keepdims=True) + EPS)
        y = xv * r
        n = y * g_ref[...]
        dn = dhv * (1.0 + sc_ref[...])
        dy = dn * g_ref[...]
        dx_ref[...] = dxr_ref[...] + r * (dy - y * jnp.mean(dy * y, axis=-1, keepdims=True))

        @pl.when(i == 0)
        def _():
            dsh_ref[...] = jnp.zeros_like(dsh_ref)
            dsc_ref[...] = jnp.zeros_like(dsc_ref)
            dg_ref[...] = jnp.zeros_like(dg_ref)

        dsh_ref[...] += jnp.sum(dhv, axis=0, keepdims=True)
        dsc_ref[...] += jnp.sum(dhv * n, axis=0, keepdims=True)
        dg_ref[...] += jnp.sum(dn * y, axis=0, keepdims=True)

    row = pl.BlockSpec((tr, D), lambda i: (i, 0))
    vec = pl.BlockSpec((1, D), lambda i: (0, 0))
    return _call(body, name="norm_mod_bwd", grid=(L // tr,), in_specs=[row, row, vec, vec, row],
                 out_specs=[row, vec, vec, vec],
                 out_shape=[_sds((L, D), F32), _sds((1, D), F32), _sds((1, D), F32), _sds((1, D), F32)],
                 compiler_params=_params(("arbitrary",)))(dh, x, g, scale, dx_res)


def _post(x, f, g, gate, res_weight):
    L = x.shape[0]
    tr = _row_tile(L)

    def body(x_ref, f_ref, g_ref, gt_ref, o_ref):
        fv = f_ref[...]
        r = lax.rsqrt(jnp.mean(fv * fv, axis=-1, keepdims=True) + EPS)
        o_ref[...] = x_ref[...] + (res_weight * (1.0 + gt_ref[...])) * (fv * r * g_ref[...])

    row = pl.BlockSpec((tr, D), lambda i: (i, 0))
    vec = pl.BlockSpec((1, D), lambda i: (0, 0))
    return _call(body, name="post", grid=(L // tr,), in_specs=[row, row, vec, vec], out_specs=row,
                 out_shape=_sds((L, D), F32), compiler_params=_params(("parallel",)))(x, f, g, gate)


def _post_bwd(dx, f, g, gate, res_weight):
    L = dx.shape[0]
    tr = _row_tile(L)

    def body(dx_ref, f_ref, g_ref, gt_ref, df_ref, dgt_ref, dg_ref):
        i = pl.program_id(0)
        fv = f_ref[...]
        dxv = dx_ref[...]
        r = lax.rsqrt(jnp.mean(fv * fv, axis=-1, keepdims=True) + EPS)
        y = fv * r
        dn = dxv * (res_weight * (1.0 + gt_ref[...]))
        dy = dn * g_ref[...]
        df_ref[...] = (r * (dy - y * jnp.mean(dy * y, axis=-1, keepdims=True))).astype(BF16)

        @pl.when(i == 0)
        def _():
            dgt_ref[...] = jnp.zeros_like(dgt_ref)
            dg_ref[...] = jnp.zeros_like(dg_ref)

        dgt_ref[...] += res_weight * jnp.sum(dxv * (y * g_ref[...]), axis=0, keepdims=True)
        dg_ref[...] += jnp.sum(dn * y, axis=0, keepdims=True)

    row = pl.BlockSpec((tr, D), lambda i: (i, 0))
    vec = pl.BlockSpec((1, D), lambda i: (0, 0))
    return _call(body, name="post_bwd", grid=(L // tr,), in_specs=[row, row, vec, vec],
                 out_specs=[row, vec, vec],
                 out_shape=[_sds((L, D), BF16), _sds((1, D), F32), _sds((1, D), F32)],
                 compiler_params=_params(("arbitrary",)))(dx, f, g, gate)


def _loss_head(x, target):
    L = x.shape[0]
    tr = _row_tile(L)

    def body(x_ref, t_ref, dx_ref, loss_ref):
        i = pl.program_id(0)
        err = x_ref[...] - t_ref[...]
        dx_ref[...] = err * (1.0 / D)

        @pl.when(i == 0)
        def _():
            loss_ref[...] = jnp.zeros_like(loss_ref)

        loss_ref[...] += 0.5 * jnp.sum(jnp.mean(err * err, axis=-1, keepdims=True), axis=0, keepdims=True)

    row = pl.BlockSpec((tr, D), lambda i: (i, 0))
    return _call(body, name="loss_head", grid=(L // tr,), in_specs=[row, row],
                 out_specs=[row, pl.BlockSpec((1, 1), lambda i: (0, 0))],
                 out_shape=[_sds((L, D), F32), _sds((1, 1), F32)],
                 compiler_params=_params(("arbitrary",)))(x, target)


def _ffn_in(h, w_in):
    L = h.shape[0]
    tm, tn = min(L, 512), 256
    nf = FF // tn

    def body(h_ref, wa_ref, wb_ref, a_ref, b_ref, act_ref):
        hv = h_ref[...]
        a = _dot(hv, wa_ref[...])
        b = _dot(hv, wb_ref[...])
        a_ref[...] = a
        b_ref[...] = b
        act_ref[...] = (a * _sigmoid(a) * b).astype(BF16)

    tile = pl.BlockSpec((tm, tn), lambda i, j: (i, j))
    return _call(body, name="ffn_in", grid=(L // tm, nf),
                 in_specs=[pl.BlockSpec((tm, D), lambda i, j: (i, 0)),
                           pl.BlockSpec((D, tn), lambda i, j: (0, j)),
                           pl.BlockSpec((D, tn), lambda i, j: (0, j + nf))],
                 out_specs=[tile, tile, tile],
                 out_shape=[_sds((L, FF), F32), _sds((L, FF), F32), _sds((L, FF), BF16)],
                 compiler_params=_params(("parallel", "parallel")))(h, w_in, w_in)


def _ffn_mid_bwd(df, w_out, a, b):
    L = df.shape[0]
    tm, tn = min(L, 512), 256

    def body(df_ref, w_ref, a_ref, b_ref, da_ref, db_ref):
        dact = _dot(df_ref[...], w_ref[...], 1, 1)
        av = a_ref[...]
        sg = _sigmoid(av)
        da_ref[...] = (dact * b_ref[...] * (sg * (1.0 + av * (1.0 - sg)))).astype(BF16)
        db_ref[...] = (dact * (av * sg)).astype(BF16)

    tile = pl.BlockSpec((tm, tn), lambda i, j: (i, j))
    return _call(body, name="ffn_mid_bwd", grid=(L // tm, FF // tn),
                 in_specs=[pl.BlockSpec((tm, D), lambda i, j: (i, 0)),
                           pl.BlockSpec((tn, D), lambda i, j: (j, 0)), tile, tile],
                 out_specs=[tile, tile],
                 out_shape=[_sds((L, FF), BF16), _sds((L, FF), BF16)],
                 compiler_params=_params(("parallel", "parallel")))(df, w_out, a, b)


def _rows_before(ref, i, tr, halo):
    start = pl.multiple_of(jnp.maximum(i * tr - halo, 0), 8)
    return jnp.where(i > 0, ref[pl.ds(start, halo), :], 0.0)


def _rows_after(ref, i, n, tr, halo):
    start = pl.multiple_of(jnp.minimum((i + 1) * tr, (n - 1) * tr), 8)
    return jnp.where(i < n - 1, ref[pl.ds(start, halo), :], 0.0)


def _conv_fwd(p, conv_w):
    L = p.shape[0]
    tr = _row_tile(L)
    n = L // tr

    def body(bg_ref, cg_ref, v_ref, w_ref, za_ref, u_scr):
        i = pl.program_id(0)

        @pl.when(i == 0)
        def _():
            u_scr[...] = cg_ref[...] * v_ref[...]

        r0 = pl.multiple_of(i * tr, 8)
        ext = jnp.concatenate([_rows_before(u_scr, i, tr, 8), u_scr[pl.ds(r0, tr), :]], axis=0)
        w = w_ref[...]
        y = (w[0:1] * pltpu.roll(ext, 2, axis=0) + w[1:2] * pltpu.roll(ext, 1, axis=0) + w[2:3] * ext)[8:, :]
        za_ref[...] = (bg_ref[pl.ds(r0, tr), :] * y).astype(BF16)

    col = lambda c: pl.BlockSpec((L, W), lambda i: (0, c))
    return _call(body, name="conv_fwd", grid=(n,),
                 in_specs=[col(0), col(1), col(2), pl.BlockSpec((3, W), lambda i: (0, 0))],
                 out_specs=pl.BlockSpec((tr, W), lambda i: (i, 0)),
                 out_shape=_sds((L, W), BF16),
                 scratch_shapes=[pltpu.VMEM((L, W), F32)],
                 compiler_params=_params(("arbitrary",)))(p, p, p, conv_w)


def _conv_bwd(p, conv_w, dza):
    L = p.shape[0]
    tr = _row_tile(L)
    n = L // tr

    def body(bg_ref, cg_ref, v_ref, w_ref, dza_ref, dp_ref, dw_ref, u_scr, dy_scr):
        i = pl.program_id(0)

        @pl.when(i == 0)
        def _():
            u_scr[...] = cg_ref[...] * v_ref[...]
            dy_scr[...] = dza_ref[...] * bg_ref[...]
            dw_ref[...] = jnp.zeros_like(dw_ref)

        r0 = pl.multiple_of(i * tr, 8)
        w = w_ref[...]
        ext = jnp.concatenate([_rows_before(u_scr, i, tr, 8), u_scr[pl.ds(r0, tr), :]], axis=0)
        u2 = pltpu.roll(ext, 2, axis=0)[8:, :]
        u1 = pltpu.roll(ext, 1, axis=0)[8:, :]
        u0 = ext[8:, :]
        y = w[0:1] * u2 + w[1:2] * u1 + w[2:3] * u0
        dy = dy_scr[pl.ds(r0, tr), :]
        dext = jnp.concatenate([dy, _rows_after(dy_scr, i, n, tr, 8)], axis=0)
        m = tr + 8
        du = (w[2:3] * dext + w[1:2] * pltpu.roll(dext, m - 1, axis=0) + w[0:1] * pltpu.roll(dext, m - 2, axis=0))[:tr, :]
        dp_ref[:, 0:W] = (dza_ref[pl.ds(r0, tr), :] * y).astype(BF16)
        dp_ref[:, W:2 * W] = (du * v_ref[pl.ds(r0, tr), :]).astype(BF16)
        dp_ref[:, 2 * W:3 * W] = (du * cg_ref[pl.ds(r0, tr), :]).astype(BF16)
        dw_ref[...] += jnp.concatenate([jnp.sum(dy * u2, axis=0, keepdims=True),
                                        jnp.sum(dy * u1, axis=0, keepdims=True),
                                        jnp.sum(dy * u0, axis=0, keepdims=True)], axis=0)

    col = lambda c: pl.BlockSpec((L, W), lambda i: (0, c))
    return _call(body, name="conv_bwd", grid=(n,),
                 in_specs=[col(0), col(1), col(2), pl.BlockSpec((3, W), lambda i: (0, 0)),
                           pl.BlockSpec((L, W), lambda i: (0, 0))],
                 out_specs=[pl.BlockSpec((tr, 3 * W), lambda i: (i, 0)), pl.BlockSpec((3, W), lambda i: (0, 0))],
                 out_shape=[_sds((L, 3 * W), BF16), _sds((3, W), F32)],
                 scratch_shapes=[pltpu.VMEM((L, W), F32), pltpu.VMEM((L, W), F32)],
                 compiler_params=_params(("arbitrary",)))(p, p, p, conv_w, dza)


def _pool_windows(lane):
    wins = jnp.zeros(lane.shape, jnp.int32)
    for gi, w in enumerate(POOL_WINDOWS):
        wins = jnp.where(lane // (W // len(POOL_WINDOWS)) == gi, w, wins)
    return wins


def _pooled_block(u_ref, i, tr):
    r0 = pl.multiple_of(i * tr, 8)
    cur = u_ref[pl.ds(r0, tr), :]
    ext = jnp.concatenate([_rows_before(u_ref, i, tr, 16), cur], axis=0)
    s2 = ext + pltpu.roll(ext, 1, axis=0)
    s4 = s2 + pltpu.roll(s2, 2, axis=0)
    s8 = s4 + pltpu.roll(s4, 4, axis=0)
    s16 = s8 + pltpu.roll(s8, 8, axis=0)
    lane = lax.broadcasted_iota(jnp.int32, (tr, W), 1)
    wins = _pool_windows(lane)
    win_sum = jnp.where(wins == 2, s2[16:], jnp.where(wins == 4, s4[16:], jnp.where(wins == 8, s8[16:], s16[16:])))
    t = lax.broadcasted_iota(jnp.int32, (tr, W), 0) + i * tr
    cnt = jnp.minimum(t + 1, wins).astype(F32)
    return win_sum / cnt - cur, cnt


def _pool_fwd(p, w_pool_bd, pool_scale):
    L = p.shape[0]
    tr = _row_tile(L)

    def body(u_ref, w_ref, sc_ref, zc_ref):
        pooled, _ = _pooled_block(u_ref, pl.program_id(0), tr)
        zc_ref[...] = (_bdot(pooled, w_ref[...]) * sc_ref[...]).astype(BF16)

    return _call(body, name="pool_fwd", grid=(L // tr,),
                 in_specs=[pl.BlockSpec((L, W), lambda i: (0, 4)), pl.BlockSpec((W, W), lambda i: (0, 0)),
                           pl.BlockSpec((1, W), lambda i: (0, 0))],
                 out_specs=pl.BlockSpec((tr, W), lambda i: (i, 0)), out_shape=_sds((L, W), BF16),
                 compiler_params=_params(("arbitrary",)))(p, w_pool_bd, pool_scale)


def _pool_bwd(p, w_pool_bd, pool_scale, dzc):
    L = p.shape[0]
    tr = _row_tile(L)
    n = L // tr

    def body(u_ref, w_ref, sc_ref, dzc_ref, du_ref, dw_ref, dsc_ref, g_scr):
        i = pl.program_id(0)

        @pl.when(i == 0)
        def _():
            dw_ref[...] = jnp.zeros_like(dw_ref)
            dsc_ref[...] = jnp.zeros_like(dsc_ref)

            def rows(k, carry):
                r = pl.multiple_of(k * tr, 8)
                dmix = (dzc_ref[pl.ds(r, tr), :] * sc_ref[...]).astype(BF16)
                dpool = _dot(dmix, w_ref[...].astype(BF16), 1, 1)
                lane = lax.broadcasted_iota(jnp.int32, (tr, W), 1)
                t = lax.broadcasted_iota(jnp.int32, (tr, W), 0) + k * tr
                cnt = jnp.minimum(t + 1, _pool_windows(lane)).astype(F32)
                g_scr[pl.ds(r, tr), :] = dpool / cnt
                return carry

            lax.fori_loop(0, n, rows, 0)

        r0 = pl.multiple_of(i * tr, 8)
        pooled, cnt = _pooled_block(u_ref, i, tr)
        dzc = dzc_ref[pl.ds(r0, tr), :]
        mixed = _bdot(pooled, w_ref[...])
        dsc_ref[...] += jnp.sum(dzc * mixed, axis=0, keepdims=True)
        dmix = (dzc * sc_ref[...]).astype(BF16)
        dw_ref[...] += _dot(pooled.astype(BF16), dmix, 0, 0)
        gcur = g_scr[pl.ds(r0, tr), :]
        ext = jnp.concatenate([gcur, _rows_after(g_scr, i, n, tr, 16)], axis=0)
        m = tr + 16
        s2 = ext + pltpu.roll(ext, m - 1, axis=0)
        s4 = s2 + pltpu.roll(s2, m - 2, axis=0)
        s8 = s4 + pltpu.roll(s4, m - 4, axis=0)
        s16 = s8 + pltpu.roll(s8, m - 8, axis=0)
        lane = lax.broadcasted_iota(jnp.int32, (tr, W), 1)
        wins = _pool_windows(lane)
        ahead = jnp.where(wins == 2, s2[:tr], jnp.where(wins == 4, s4[:tr], jnp.where(wins == 8, s8[:tr], s16[:tr])))
        du_ref[...] = (ahead - gcur * cnt).astype(BF16)

    return _call(body, name="pool_bwd", grid=(n,),
                 in_specs=[pl.BlockSpec((L, W), lambda i: (0, 4)), pl.BlockSpec((W, W), lambda i: (0, 0)),
                           pl.BlockSpec((1, W), lambda i: (0, 0)), pl.BlockSpec((L, W), lambda i: (0, 0))],
                 out_specs=[pl.BlockSpec((tr, W), lambda i: (i, 0)), pl.BlockSpec((W, W), lambda i: (0, 0)),
                            pl.BlockSpec((1, W), lambda i: (0, 0))],
                 out_shape=[_sds((L, W), BF16), _sds((W, W), F32), _sds((1, W), F32)],
                 scratch_shapes=[pltpu.VMEM((L, W), F32)],
                 compiler_params=_params(("arbitrary",)))(p, w_pool_bd, pool_scale, dzc)


SSM_SLAB = 512


def _ssm_prep(lam_re, lam_im, log_dt_x):
    def body(lr_ref, li_ref, ldt_ref, abr_ref, abi_ref, fr_ref, fi_ref):
        lr = jnp.minimum(lr_ref[...], LAMBDA_RE_MAX)
        li = li_ref[...]
        dt = jnp.exp(ldt_ref[...])
        mag = jnp.exp(lr * dt)
        abr = mag * jnp.cos(li * dt)
        abi = mag * jnp.sin(li * dt)
        den = lr * lr + li * li
        nr = abr - 1.0
        abr_ref[...] = abr
        abi_ref[...] = abi
        fr_ref[...] = (nr * lr + abi * li) / den
        fi_ref[...] = (abi * lr - nr * li) / den

    shp = _sds(lam_re.shape, F32)
    return _call(body, name="ssm_prep", out_shape=[shp, shp, shp, shp], compiler_params=_params())(lam_re, lam_im, log_dt_x)


def _ssm_prep_bwd(lam_re, lam_im, log_dt_x, g_abr, g_abi, g_fr, g_fi, group_sum):
    def body(lr_ref, li_ref, ldt_ref, gar_ref, gai_ref, gfr_ref, gfi_ref, gs_ref, dlr_ref, dli_ref, dldt_ref):
        lam = lr_ref[...]
        lr = jnp.minimum(lam, LAMBDA_RE_MAX)
        li = li_ref[...]
        dt = jnp.exp(ldt_ref[...])
        mag = jnp.exp(lr * dt)
        abr = mag * jnp.cos(li * dt)
        abi = mag * jnp.sin(li * dt)
        den = lr * lr + li * li
        nr = abr - 1.0
        fr = (nr * lr + abi * li) / den
        fi = (abi * lr - nr * li) / den
        d_nre = gfr_ref[...] / den
        d_nim = gfi_ref[...] / den
        d_den = -(gfr_ref[...] * fr + gfi_ref[...] * fi) / den
        d_abr = gar_ref[...] + d_nre * lr - d_nim * li
        d_abi = gai_ref[...] + d_nre * li + d_nim * lr
        d_lr = d_nre * nr + d_nim * abi + d_den * 2.0 * lr
        d_li = d_nre * abi - d_nim * nr + d_den * 2.0 * li
        d_mag = d_abr * jnp.cos(li * dt) + d_abi * jnp.sin(li * dt)
        d_th = -d_abr * abi + d_abi * abr
        d_lr = d_lr + d_mag * mag * dt
        d_li = d_li + d_th * dt
        d_dt = d_mag * mag * lr + d_th * li
        passes = jnp.where(lam < LAMBDA_RE_MAX, 1.0, jnp.where(lam == LAMBDA_RE_MAX, 0.5, 0.0))
        dlr_ref[...] = d_lr * passes
        dli_ref[...] = d_li
        dldt_ref[...] = _dot(d_dt * dt, gs_ref[...], precision=HIGH)

    shp = _sds(lam_re.shape, F32)
    return _call(body, name="ssm_prep_bwd", out_shape=[shp, shp, _sds((lam_re.shape[0], 128), F32)],
                 compiler_params=_params())(lam_re, lam_im, log_dt_x, g_abr, g_abi, g_fr, g_fi, group_sum)


def _cmul(ar, ai, br, bi):
    return ar * br - ai * bi, ar * bi + ai * br


def _powers(ar, ai):
    out = [(ar, ai)]
    for _ in range(7):
        out.append(_cmul(out[-1][0], out[-1][1], ar, ai))
    return out


def _scan_rows(s_re, s_im, ar, ai, L, reverse=False, visit=None, visit_init=None):
    n = s_re.shape[1]
    pw = _powers(ar, ai)
    row = lax.broadcasted_iota(jnp.int32, (8, n), 0)
    dist = (8 - row) if reverse else (row + 1)
    pr = jnp.zeros((8, n), F32)
    pi = jnp.zeros((8, n), F32)
    for k in range(8):
        pr = jnp.where(dist == k + 1, pw[k][0], pr)
        pi = jnp.where(dist == k + 1, pw[k][1], pi)
    nb = L // 8

    def blk(t, carry):
        cr, ci, acc = carry
        b = (nb - 1 - t) if reverse else t
        r0 = pl.multiple_of(b * 8, 8)
        xr = s_re[pl.ds(r0, 8), :]
        xi = s_im[pl.ds(r0, 8), :]
        for d in (1, 2, 4):
            if reverse:
                keep = row < 8 - d
                sr, si = pltpu.roll(xr, 8 - d, axis=0), pltpu.roll(xi, 8 - d, axis=0)
            else:
                keep = row >= d
                sr, si = pltpu.roll(xr, d, axis=0), pltpu.roll(xi, d, axis=0)
            sr = jnp.where(keep, sr, 0.0)
            si = jnp.where(keep, si, 0.0)
            mr, mi = _cmul(pw[d - 1][0], pw[d - 1][1], sr, si)
            xr, xi = xr + mr, xi + mi
        mr, mi = _cmul(pr, pi, cr, ci)
        xr, xi = xr + mr, xi + mi
        s_re[pl.ds(r0, 8), :] = xr
        s_im[pl.ds(r0, 8), :] = xi
        if visit is not None:
            acc = visit(b, xr, xi, acc)
        if reverse:
            return xr[0:1, :], xi[0:1, :], acc
        return xr[7:8, :], xi[7:8, :], acc

    zero = jnp.zeros((1, n), F32)
    return lax.fori_loop(0, nb, blk, (zero, zero, visit_init if visit is not None else 0))[2]


def _ssm_project(u_ref, wbr, wbi, s_re, s_im, L):
    ch = min(L, 256)

    def rows(k, carry):
        r = pl.multiple_of(k * ch, 8)
        ub = u_ref[pl.ds(r, ch), :].astype(BF16)
        s_re[pl.ds(r, ch), :] = _dot(ub, wbr)
        s_im[pl.ds(r, ch), :] = _dot(ub, wbi)
        return carry

    lax.fori_loop(0, L // ch, rows, 0)


def _gelu(y):
    c = math.sqrt(2.0 / math.pi)
    return 0.5 * y * (1.0 + jnp.tanh(c * (y + 0.044715 * y * y * y)))


def _gelu_grad(y):
    c = math.sqrt(2.0 / math.pi)
    th = jnp.tanh(c * (y + 0.044715 * y * y * y))
    return 0.5 * (1.0 + th) + 0.5 * y * (1.0 - th * th) * c * (1.0 + 3.0 * 0.044715 * y * y)


def _ssm_fwd(p, b_re_bd, b_im_bd, c_re_bd, c_im_bd, abr, abi, fr, fi, d_skip):
    L = p.shape[0]
    ns = NST // SSM_SLAB
    ch = min(L, 256)

    def body(u_ref, br_ref, bi_ref, cr_ref, ci_ref, abr_ref, abi_ref, fr_ref, fi_ref, d_ref,
             y_ref, zb_ref, s_re, s_im):
        j = pl.program_id(0)
        f_re, f_im = fr_ref[...], fi_ref[...]
        wbr = (f_re * br_ref[...] - f_im * bi_ref[...]).astype(BF16)
        wbi = (f_re * bi_ref[...] + f_im * br_ref[...]).astype(BF16)
        _ssm_project(u_ref, wbr, wbi, s_re, s_im, L)
        _scan_rows(s_re, s_im, abr_ref[...], abi_ref[...], L)
        crb = cr_ref[...].astype(BF16)
        cib = ci_ref[...].astype(BF16)

        def rows(k, carry):
            r = pl.multiple_of(k * ch, 8)
            part = _dot(s_re[pl.ds(r, ch), :].astype(BF16), crb) - _dot(s_im[pl.ds(r, ch), :].astype(BF16), cib)

            @pl.when(j == 0)
            def _():
                y_ref[pl.ds(r, ch), :] = part + d_ref[...] * u_ref[pl.ds(r, ch), :]

            @pl.when(j > 0)
            def _():
                y_ref[pl.ds(r, ch), :] += part

            @pl.when(j == ns - 1)
            def _():
                zb_ref[pl.ds(r, ch), :] = _gelu(y_ref[pl.ds(r, ch), :]).astype(BF16)

            return carry

        lax.fori_loop(0, L // ch, rows, 0)

    full = lambda shape: pl.BlockSpec(shape, lambda j: (0, 0))
    lanes = pl.BlockSpec((1, SSM_SLAB), lambda j: (0, j))
    return _call(body, name="ssm_fwd", grid=(ns,),
                 in_specs=[pl.BlockSpec((L, W), lambda j: (0, 3)),
                           pl.BlockSpec((W, SSM_SLAB), lambda j: (0, j)), pl.BlockSpec((W, SSM_SLAB), lambda j: (0, j)),
                           pl.BlockSpec((SSM_SLAB, W), lambda j: (j, 0)), pl.BlockSpec((SSM_SLAB, W), lambda j: (j, 0)),
                           lanes, lanes, lanes, lanes, full((1, W))],
                 out_specs=[full((L, W)), full((L, W))],
                 out_shape=[_sds((L, W), F32), _sds((L, W), BF16)],
                 scratch_shapes=[pltpu.VMEM((L, SSM_SLAB), F32), pltpu.VMEM((L, SSM_SLAB), F32)],
                 compiler_params=_params(("arbitrary",)))(p, b_re_bd, b_im_bd, c_re_bd, c_im_bd, abr, abi, fr, fi, d_skip)


def _ssm_bwd(p, y, dzb, b_re_bd, b_im_bd, c_re_bd, c_im_bd, abr, abi, fr, fi, d_skip):
    L = p.shape[0]
    ns = NST // SSM_SLAB
    ch = min(L, 256)
    n_ch = L // ch

    def body(u_ref, y_ref, dzb_ref, br_ref, bi_ref, cr_ref, ci_ref, abr_ref, abi_ref, fr_ref, fi_ref, d_ref,
             du_ref, dd_ref, dbr_ref, dbi_ref, dcr_ref, dci_ref, gar_ref, gai_ref, gfr_ref, gfi_ref,
             s_re, s_im, l_re, l_im, dy_scr, du_scr):
        j = pl.program_id(0)
        f_re, f_im = fr_ref[...], fi_ref[...]
        b_re, b_im = br_ref[...], bi_ref[...]
        wbr = (f_re * b_re - f_im * b_im).astype(BF16)
        wbi = (f_re * b_im + f_im * b_re).astype(BF16)
        a_re, a_im = abr_ref[...], abi_ref[...]

        @pl.when(j == 0)
        def _():
            def rows(k, acc):
                r = pl.multiple_of(k * ch, 8)
                dy = dzb_ref[pl.ds(r, ch), :] * _gelu_grad(y_ref[pl.ds(r, ch), :])
                dy_scr[pl.ds(r, ch), :] = dy
                du_scr[pl.ds(r, ch), :] = d_ref[...] * dy
                return acc + jnp.sum(dy * u_ref[pl.ds(r, ch), :], axis=0, keepdims=True)

            dd_ref[...] = lax.fori_loop(0, n_ch, rows, jnp.zeros((1, W), F32))

        _ssm_project(u_ref, wbr, wbi, s_re, s_im, L)
        _scan_rows(s_re, s_im, a_re, a_im, L)
        crb = cr_ref[...].astype(BF16)
        cib = ci_ref[...].astype(BF16)

        def rows_c(k, acc):
            dcr, dci = acc
            r = pl.multiple_of(k * ch, 8)
            dyb = dy_scr[pl.ds(r, ch), :].astype(BF16)
            dcr = dcr + _dot(s_re[pl.ds(r, ch), :].astype(BF16), dyb, 0, 0)
            dci = dci - _dot(s_im[pl.ds(r, ch), :].astype(BF16), dyb, 0, 0)
            l_re[pl.ds(r, ch), :] = _dot(dyb, crb, 1, 1)
            l_im[pl.ds(r, ch), :] = -_dot(dyb, cib, 1, 1)
            return dcr, dci

        zc = jnp.zeros((SSM_SLAB, W), F32)
        dcr, dci = lax.fori_loop(0, n_ch, rows_c, (zc, zc))
        dcr_ref[...] = dcr
        dci_ref[...] = dci

        row8 = lax.broadcasted_iota(jnp.int32, (8, SSM_SLAB), 0)

        def visit(b, lr, li, acc):
            ar_acc, ai_acc = acc
            r0 = pl.multiple_of(b * 8, 8)
            rp = pl.multiple_of(jnp.maximum(b * 8 - 8, 0), 8)
            has_prev = b > 0
            pr = jnp.where(has_prev, s_re[pl.ds(rp, 8), :][7:8, :], 0.0)
            pi = jnp.where(has_prev, s_im[pl.ds(rp, 8), :][7:8, :], 0.0)
            sr = jnp.where(row8 >= 1, pltpu.roll(s_re[pl.ds(r0, 8), :], 1, axis=0), pr)
            si = jnp.where(row8 >= 1, pltpu.roll(s_im[pl.ds(r0, 8), :], 1, axis=0), pi)
            return ar_acc + lr * sr + li * si, ai_acc - lr * si + li * sr

        z8 = jnp.zeros((8, SSM_SLAB), F32)
        ar_acc, ai_acc = _scan_rows(l_re, l_im, a_re, -a_im, L, reverse=True, visit=visit, visit_init=(z8, z8))
        gar_ref[...] = jnp.sum(ar_acc, axis=0, keepdims=True)
        gai_ref[...] = jnp.sum(ai_acc, axis=0, keepdims=True)

        def rows_b(k, acc):
            dwr, dwi = acc
            r = pl.multiple_of(k * ch, 8)
            ub = u_ref[pl.ds(r, ch), :].astype(BF16)
            lrb = l_re[pl.ds(r, ch), :].astype(BF16)
            lib = l_im[pl.ds(r, ch), :].astype(BF16)
            du_scr[pl.ds(r, ch), :] += _dot(lrb, wbr, 1, 1) + _dot(lib, wbi, 1, 1)
            return dwr + _dot(ub, lrb, 0, 0), dwi + _dot(ub, lib, 0, 0)

        zb = jnp.zeros((W, SSM_SLAB), F32)
        dwr, dwi = lax.fori_loop(0, n_ch, rows_b, (zb, zb))
        dbr_ref[...] = dwr * f_re + dwi * f_im
        dbi_ref[...] = -dwr * f_im + dwi * f_re
        gfr_ref[...] = jnp.sum(dwr * b_re + dwi * b_im, axis=0, keepdims=True)
        gfi_ref[...] = jnp.sum(-dwr * b_im + dwi * b_re, axis=0, keepdims=True)

        @pl.when(j == ns - 1)
        def _():
            du_ref[...] = du_scr[...].astype(BF16)

    full = lambda shape: pl.BlockSpec(shape, lambda j: (0, 0))
    lanes = pl.BlockSpec((1, SSM_SLAB), lambda j: (0, j))
    bspec = pl.BlockSpec((W, SSM_SLAB), lambda j: (0, j))
    cspec = pl.BlockSpec((SSM_SLAB, W), lambda j: (j, 0))
    slab = lambda: pltpu.VMEM((L, SSM_SLAB), F32)
    return _call(body, name="ssm_bwd", grid=(ns,),
                 in_specs=[pl.BlockSpec((L, W), lambda j: (0, 3)), full((L, W)), full((L, W)),
                           bspec, bspec, cspec, cspec, lanes, lanes, lanes, lanes, full((1, W))],
                 out_specs=[full((L, W)), full((1, W)), bspec, bspec, cspec, cspec, lanes, lanes, lanes, lanes],
                 out_shape=[_sds((L, W), BF16), _sds((1, W), F32), _sds((W, NST), F32), _sds((W, NST), F32),
                            _sds((NST, W), F32), _sds((NST, W), F32)] + [_sds((1, NST), F32)] * 4,
                 scratch_shapes=[slab(), slab(), slab(), slab(), pltpu.VMEM((L, W), F32), pltpu.VMEM((L, W), F32)],
                 compiler_params=_params(("arbitrary",)))(p, y, dzb, b_re_bd, b_im_bd, c_re_bd, c_im_bd,
                                                          abr, abi, fr, fi, d_skip)


def _log_sigmoid(z):
    return jnp.minimum(z, 0.0) - jnp.log1p(jnp.exp(-jnp.abs(z)))


def _sb_block(q, kj, i, j, carry, tri):
    z = _bdot(q, kj, 1, 1)
    t_idx = lax.broadcasted_iota(jnp.int32, (QB, QB), 0) + i * QB
    s_idx = lax.broadcasted_iota(jnp.int32, (QB, QB), 1) + j * QB
    mask = s_idx < t_idx
    lk = jnp.where(mask, _log_sigmoid(-z), 0.0)
    suf = _dot(lk, tri, precision=HIGH)
    a = jnp.where(mask, jnp.exp(_log_sigmoid(z) + (suf - lk) + carry), 0.0)
    return z, mask, lk, a


def _suffix_ones():
    r = lax.broadcasted_iota(jnp.int32, (QB, QB), 0)
    c = lax.broadcasted_iota(jnp.int32, (QB, QB), 1)
    return (r >= c).astype(F32), (r < c).astype(F32)


def _sb_fwd(q, k, v):
    L = q.shape[1]

    def body(q_ref, k_ref, v_ref, o_ref):
        i = pl.program_id(1)
        qv = q_ref[...]
        tri, _ = _suffix_ones()

        def step(t, carry):
            acc, right = carry
            j = i - t
            r = pl.multiple_of(j * QB, QB)
            _, _, lk, a = _sb_block(qv, k_ref[pl.ds(r, QB), :], i, j, right, tri)
            acc = acc + _bdot(a, v_ref[pl.ds(r, QB), :])
            return acc, right + jnp.sum(lk, axis=1, keepdims=True)

        acc, _ = lax.fori_loop(0, i + 1, step, (jnp.zeros((QB, HD), F32), jnp.zeros((QB, 1), F32)))
        o_ref[...] = acc

    head = pl.BlockSpec((None, L, HD), lambda h, i: (h, 0, 0))
    blk = pl.BlockSpec((None, QB, HD), lambda h, i: (h, i, 0))
    return _call(body, name="sb_fwd", grid=(HEADS, L // QB), in_specs=[blk, head, head], out_specs=blk,
                 out_shape=_sds((HEADS, L, HD), F32), compiler_params=_params(("parallel", "arbitrary")))(q, k, v)


def _sb_bwd(q, k, v, do):
    L = q.shape[1]
    nq = L // QB

    def body(q_ref, k_ref, v_ref, do_ref, dq_ref, dk_ref, dv_ref, right_scr):
        i = pl.program_id(1)
        qv = q_ref[...]
        dov = do_ref[...]
        tri, tri_strict = _suffix_ones()

        @pl.when(i == 0)
        def _():
            dk_ref[...] = jnp.zeros_like(dk_ref)
            dv_ref[...] = jnp.zeros_like(dv_ref)

        def carries(t, right):
            j = i - t
            r = pl.multiple_of(j * QB, QB)
            right_scr[pl.ds(r, QB), :] = jnp.broadcast_to(right, (QB, 128))
            z = _bdot(qv, k_ref[pl.ds(r, QB), :], 1, 1)
            t_idx = lax.broadcasted_iota(jnp.int32, (QB, QB), 0) + i * QB
            s_idx = lax.broadcasted_iota(jnp.int32, (QB, QB), 1) + j * QB
            lk = jnp.where(s_idx < t_idx, _log_sigmoid(-z), 0.0)
            return right + jnp.sum(lk, axis=1, keepdims=True)

        lax.fori_loop(0, i + 1, carries, jnp.zeros((QB, 1), F32))

        def step(j, carry):
            dq, left = carry
            r = pl.multiple_of(j * QB, QB)
            kj = k_ref[pl.ds(r, QB), :]
            vj = v_ref[pl.ds(r, QB), :]
            right = right_scr[pl.ds(r, QB), :][:, 0:1]
            z, mask, _, a = _sb_block(qv, kj, i, j, right, tri)
            ab = a.astype(BF16)
            e = a * _bdot(dov, vj, 1, 1)
            dv_ref[pl.ds(r, QB), :] += _dot(ab, dov.astype(BF16), 0, 0)
            before = left + _dot(e, tri_strict, precision=HIGH)
            sg = _sigmoid(z)
            dz = jnp.where(mask, e * (1.0 - sg) - sg * before, 0.0).astype(BF16)
            dk_ref[pl.ds(r, QB), :] += _dot(dz, qv.astype(BF16), 0, 0)
            return dq + _dot(dz, kj.astype(BF16)), left + jnp.sum(e, axis=1, keepdims=True)

        dq, _ = lax.fori_loop(0, i + 1, step, (jnp.zeros((QB, HD), F32), jnp.zeros((QB, 1), F32)))
        dq_ref[...] = dq

    head = pl.BlockSpec((None, L, HD), lambda h, i: (h, 0, 0))
    blk = pl.BlockSpec((None, QB, HD), lambda h, i: (h, i, 0))
    shp = _sds((HEADS, L, HD), F32)
    return _call(body, name="sb_bwd", grid=(HEADS, nq), in_specs=[blk, head, head, blk], out_specs=[blk, head, head],
                 out_shape=[shp, shp, shp], scratch_shapes=[pltpu.VMEM((L, 128), F32)],
                 compiler_params=_params(("parallel", "arbitrary")))(q, k, v, do)


def _merge_fwd(za, zb, zc, zd, p, w_conv_out, w_glu, w_pool_out, w_sb_out):
    L = za.shape[0]
    tm = _row_tile(L)

    def body(za_ref, zb_ref, zc_ref, zd_ref, g0, g1, g2, g3, wc_ref, wg_ref, wp_ref, ws_ref, o_ref):
        glu = _dot(zb_ref[...], wg_ref[...])
        yb = glu[:, :D] * _sigmoid(glu[:, D:])
        m = _sigmoid(g0[...]) * _dot(za_ref[...], wc_ref[...])
        m = m + _sigmoid(g1[...]) * yb
        m = m + _sigmoid(g2[...]) * _dot(zc_ref[...], wp_ref[...])
        m = m + _sigmoid(g3[...]) * _dot(zd_ref[...], ws_ref[...])
        o_ref[...] = m.astype(BF16)

    zt = pl.BlockSpec((tm, W), lambda i: (i, 0))
    gate = lambda b: pl.BlockSpec((tm, D), lambda i: (i, 2 + b))
    wfull = lambda n: pl.BlockSpec((W, n), lambda i: (0, 0))
    return _call(body, name="merge_fwd", grid=(L // tm,),
                 in_specs=[zt, zt, zt, zt, gate(0), gate(1), gate(2), gate(3), wfull(D), wfull(2 * D), wfull(D), wfull(D)],
                 out_specs=pl.BlockSpec((tm, D), lambda i: (i, 0)), out_shape=_sds((L, D), BF16),
                 compiler_params=_params(("parallel",)))(za, zb, zc, zd, p, p, p, p, w_conv_out, w_glu, w_pool_out, w_sb_out)


def _merge_bwd(dm, za, zb, zc, zd, p, w_conv_out, w_glu, w_pool_out, w_sb_out):
    L = za.shape[0]
    tm = _row_tile(L)
    n = L // tm

    def body(dm_ref, za_ref, zb_ref, zc_ref, zd_ref, g0, g1, g2, g3, wc_ref, wg_ref, wp_ref, ws_ref,
             dza_ref, dzb_ref, dzc_ref, dzd_ref, dg_ref, dwc_ref, dwg_ref, dwp_ref, dws_ref,
             awc, awg, awp, aws):
        i = pl.program_id(0)

        @pl.when(i == 0)
        def _():
            awc[...] = jnp.zeros_like(awc)
            awg[...] = jnp.zeros_like(awg)
            awp[...] = jnp.zeros_like(awp)
            aws[...] = jnp.zeros_like(aws)

        dmv = dm_ref[...]

        def gated(g_ref, y, col):
            s = _sigmoid(g_ref[...])
            dg_ref[:, col * D:(col + 1) * D] = (dmv * y * s * (1.0 - s)).astype(BF16)
            return (dmv * s)

        def linear(z_ref, w_ref, acc, dz_ref, col, g_ref):
            zv = z_ref[...]
            dy = gated(g_ref, _dot(zv, w_ref[...]), col).astype(BF16)
            dz_ref[...] = _dot(dy, w_ref[...], 1, 1)
            acc[...] += _dot(zv, dy, 0, 0)

        linear(za_ref, wc_ref, awc, dza_ref, 0, g0)
        linear(zc_ref, wp_ref, awp, dzc_ref, 2, g2)
        linear(zd_ref, ws_ref, aws, dzd_ref, 3, g3)
        zbv = zb_ref[...]
        glu = _dot(zbv, wg_ref[...])
        ga = glu[:, :D]
        sg = _sigmoid(glu[:, D:])
        dyb = gated(g1, ga * sg, 1)
        dga = (dyb * sg).astype(BF16)
        dgg = (dyb * ga * sg * (1.0 - sg)).astype(BF16)
        dzb_ref[...] = _dot(dga, wg_ref[:, :D], 1, 1) + _dot(dgg, wg_ref[:, D:], 1, 1)
        awg[:, :D] += _dot(zbv, dga, 0, 0)
        awg[:, D:] += _dot(zbv, dgg, 0, 0)

        @pl.when(i == n - 1)
        def _():
            dwc_ref[...] = awc[...].astype(BF16)
            dwg_ref[...] = awg[...].astype(BF16)
            dwp_ref[...] = awp[...].astype(BF16)
            dws_ref[...] = aws[...].astype(BF16)

    zt = pl.BlockSpec((tm, W), lambda i: (i, 0))
    gate = lambda b: pl.BlockSpec((tm, D), lambda i: (i, 2 + b))
    wfull = lambda n_: pl.BlockSpec((W, n_), lambda i: (0, 0))
    zs = _sds((L, W), F32)
    return _call(body, name="merge_bwd", grid=(n,),
                 in_specs=[pl.BlockSpec((tm, D), lambda i: (i, 0)), zt, zt, zt, zt, gate(0), gate(1), gate(2), gate(3),
                           wfull(D), wfull(2 * D), wfull(D), wfull(D)],
                 out_specs=[zt, zt, zt, zt, pl.BlockSpec((tm, 4 * D), lambda i: (i, 0)),
                            wfull(D), wfull(2 * D), wfull(D), wfull(D)],
                 out_shape=[zs, zs, zs, zs, _sds((L, 4 * D), BF16),
                            _sds((W, D), BF16), _sds((W, 2 * D), BF16), _sds((W, D), BF16), _sds((W, D), BF16)],
                 scratch_shapes=[pltpu.VMEM((W, D), F32), pltpu.VMEM((W, 2 * D), F32), pltpu.VMEM((W, D), F32),
                                 pltpu.VMEM((W, D), F32)],
                 compiler_params=_params(("arbitrary",)))(dm, za, zb, zc, zd, p, p, p, p,
                                                          w_conv_out, w_glu, w_pool_out, w_sb_out)


def _adam_math(w, g, m, v):
    m2 = ADAM_B1 * m + (1.0 - ADAM_B1) * g
    v2 = ADAM_B2 * v + (1.0 - ADAM_B2) * (g * g)
    m_hat = m2 / (1.0 - ADAM_B1 ** ADAM_STEP)
    v_hat = v2 / (1.0 - ADAM_B2 ** ADAM_STEP)
    return -ADAM_LR * (m_hat / (jnp.sqrt(v_hat) + ADAM_EPS) + ADAM_WD * w), m2, v2


def _as_rows(a):
    return a.reshape(-1, a.shape[-1])


def _adamw(w, g, m, v):
    shape = w.shape
    w2, g2, m2, v2 = _as_rows(w), _as_rows(g), _as_rows(m), _as_rows(v)
    R, C = w2.shape
    tr = R
    for cand in (1024, 512, 256, 128, 64, 32, 16, 8):
        if R % cand == 0 and cand * C * 4 <= 2 * 1024 * 1024:
            tr = cand
            break

    def body(w_ref, g_ref, m_ref, v_ref, d_ref, m_out, v_out):
        d, mn, vn = _adam_math(w_ref[...], g_ref[...], m_ref[...], v_ref[...])
        d_ref[...] = d
        m_out[...] = mn
        v_out[...] = vn

    blk = pl.BlockSpec((tr, C), lambda i: (i, 0))
    shp = _sds((R, C), F32)
    outs = _call(body, name="adamw", grid=(R // tr,), in_specs=[blk] * 4, out_specs=[blk] * 3, out_shape=[shp] * 3,
                 compiler_params=_params(("parallel",)))(w2, g2, m2, v2)
    return tuple(o.reshape(shape) for o in outs)


def _sum_parts(parts, out_dtype, name):
    shape = parts[0].shape
    flat = [_as_rows(a) for a in parts]
    R, C = flat[0].shape
    tr = R
    for cand in (1024, 512, 256, 128, 64, 32, 16):
        if R % cand == 0 and cand * C * 4 <= 2 * 1024 * 1024:
            tr = cand
            break
    k = len(parts)

    def body(*refs):
        acc = refs[0][...].astype(F32)
        for r in refs[1:k]:
            acc = acc + r[...].astype(F32)
        refs[k][...] = acc.astype(out_dtype)

    blk = pl.BlockSpec((tr, C), lambda i: (i, 0))
    out = _call(body, name=name, grid=(R // tr,), in_specs=[blk] * k, out_specs=blk, out_shape=_sds((R, C), out_dtype),
                compiler_params=_params(("parallel",)))(*flat)
    return out.reshape(shape)


ADA_SHARD = 9 * D // N_CHIP
ADA_TN = 768


def _ada_fwd(c_pad, w_ada, b_ada_cols):
    depth = w_ada.shape[0]

    def body(c_ref, w_ref, b_ref, o_ref):
        cv = c_ref[...]
        o_ref[...] = _bdot(cv * _sigmoid(cv), w_ref[...]) + b_ref[...]

    return _call(body, name="ada_fwd", grid=(depth, ADA_SHARD // ADA_TN),
                 in_specs=[pl.BlockSpec((16, D), lambda l, j: (0, 0)),
                           pl.BlockSpec((None, D, ADA_TN), lambda l, j: (l, 0, j)),
                           pl.BlockSpec((None, 1, ADA_TN), lambda l, j: (l, 0, j))],
                 out_specs=pl.BlockSpec((None, 16, ADA_TN), lambda l, j: (l, 0, j)),
                 out_shape=_sds((depth, 16, ADA_SHARD), F32),
                 compiler_params=_params(("parallel", "parallel")))(c_pad, w_ada, b_ada_cols)


def _ada_wgrad(c_pad, d_ada):
    depth = d_ada.shape[0]

    def body(c_ref, d_ref, o_ref):
        cv = c_ref[...]
        o_ref[...] = _bdot(cv * _sigmoid(cv), d_ref[...], 0, 0)

    return _call(body, name="ada_wgrad", grid=(depth, ADA_SHARD // ADA_TN),
                 in_specs=[pl.BlockSpec((16, D), lambda l, j: (0, 0)),
                           pl.BlockSpec((None, 16, ADA_TN), lambda l, j: (l, 0, j))],
                 out_specs=pl.BlockSpec((None, D, ADA_TN), lambda l, j: (l, 0, j)),
                 out_shape=_sds((depth, D, ADA_SHARD), F32),
                 compiler_params=_params(("parallel", "parallel")))(c_pad, d_ada)


HBM_SPEC = pl.BlockSpec(memory_space=pltpu.HBM)


def _place():
    x, y, c = lax.axis_index("x"), lax.axis_index("y"), lax.axis_index("c")
    peers = [(1 - x, y), (x, 1 - y), (1 - x, 1 - y)]
    return x, y, c, peers


def _chip(px, py):
    return 2 * px + py


def _allgather8(block, name):
    m_per, n = block.shape

    def body(x_ref, out_ref, send_sems, recv_sems, local_sem):
        x, y, c, chips = _place()
        me, sibling = (x, y, c), (x, y, 1 - c)

        def rows(px, py, pc):
            return out_ref.at[pl.ds(pl.multiple_of((4 * px + 2 * py + pc) * m_per, 8), m_per), :]

        def copy(k, blk, to, src=None):
            return pltpu.make_async_remote_copy(
                src_ref=rows(*blk) if src is None else src, dst_ref=rows(*blk),
                send_sem=send_sems.at[k], recv_sem=recv_sems.at[k], device_id=to, device_id_type=MESH)

        mine = pltpu.make_async_copy(x_ref, rows(*me), local_sem)
        mine.start()
        first = [copy(0, me, sibling, src=x_ref)]
        first += [copy(1 + j, me, (*chip, c), src=x_ref) for j, chip in enumerate(chips)]
        for cp in first:
            cp.start()
        passed = [copy(4 + j, (*chip, c), sibling) for j, chip in enumerate(chips)]
        for j, chip in enumerate(chips):
            copy(1 + j, (*chip, c), me).wait_recv()
            passed[j].start()
        copy(0, sibling, me).wait_recv()
        for j, chip in enumerate(chips):
            copy(4 + j, (*chip, 1 - c), me).wait_recv()
        for cp in first + passed:
            cp.wait_send()
        mine.wait()

    return _call(body, name=name, out_shape=_sds((N_DEV * m_per, n), block.dtype),
                 in_specs=[pl.BlockSpec(memory_space=pltpu.VMEM)], out_specs=pl.BlockSpec(memory_space=pltpu.VMEM),
                 scratch_shapes=[pltpu.SemaphoreType.DMA((7,)), pltpu.SemaphoreType.DMA((7,)), pltpu.SemaphoreType.DMA],
                 compiler_params=_params())(block)


GATHERED = (("w_ff_in", -1), ("w_ff_out", -2), ("w_in", -1), ("w_conv_out", -1), ("w_glu", -1),
            ("w_pool_out", -1), ("w_sb_out", -1), ("w_out", -2))


def _lead(ref):
    return (slice(None),) * (len(ref.shape) - 2)


def _mo(v, m):
    return v if isinstance(v, int) else pl.multiple_of(v, m)


def _full_region(ref, axis, j, half, shard_shape):
    rs, cs = shard_shape[-2], shard_shape[-1]
    if axis == -1:
        r0, nr = (0, rs) if half is None else (half * (rs // 2), rs // 2)
        return ref.at[_lead(ref) + (pl.ds(_mo(r0, 16), nr), pl.ds(_mo(j * cs, 128), cs))]
    r0, nr = (j * rs, rs) if half is None else (j * rs + half * (rs // 2), rs // 2)
    return ref.at[_lead(ref) + (pl.ds(_mo(r0, 16), nr), slice(None))]


def _shard_half(ref, half):
    rs = ref.shape[-2]
    return ref.at[_lead(ref) + (pl.ds(_mo(half * (rs // 2), 16), rs // 2), slice(None))]


def _full_shape(shard_shape, axis):
    s = list(shard_shape)
    s[axis] *= N_CHIP
    return tuple(s)


def _gather_layer(shards):
    n = len(shards)
    axes = [a for _, a in GATHERED]
    shapes = [s.shape for s in shards]

    def body(*refs):
        ins, outs = refs[:n], refs[n:2 * n]
        send_sems, recv_sems, local_sems = refs[2 * n:]
        x, y, c, chips = _place()
        my = _chip(x, y)
        sibling = (x, y, 1 - c)
        local = []
        for a in range(n):
            cp = pltpu.make_async_copy(ins[a], _full_region(outs[a], axes[a], my, None, shapes[a]), local_sems.at[a])
            cp.start()
            local.append(cp)
        sends = []
        for a in range(n):
            for k, chip in enumerate(chips):
                cp = pltpu.make_async_remote_copy(
                    src_ref=_shard_half(ins[a], c), dst_ref=_full_region(outs[a], axes[a], my, c, shapes[a]),
                    send_sem=send_sems.at[a * 3 + k], recv_sem=recv_sems.at[a * 3 + k],
                    device_id=(*chip, c), device_id_type=MESH)
                cp.start()
                sends.append(cp)
        for a in range(n):
            for k, chip in enumerate(chips):
                landed = _full_region(outs[a], axes[a], _chip(*chip), c, shapes[a])
                pltpu.make_async_remote_copy(
                    src_ref=landed, dst_ref=landed, send_sem=send_sems.at[a * 3 + k], recv_sem=recv_sems.at[a * 3 + k],
                    device_id=(*chip, c), device_id_type=MESH).wait_recv()
                cp = pltpu.make_async_remote_copy(
                    src_ref=landed, dst_ref=landed, send_sem=send_sems.at[3 * n + a * 3 + k],
                    recv_sem=recv_sems.at[3 * n + a * 3 + k], device_id=sibling, device_id_type=MESH)
                cp.start()
                sends.append(cp)
        for a in range(n):
            for k, chip in enumerate(chips):
                passed = _full_region(outs[a], axes[a], _chip(*chip), 1 - c, shapes[a])
                pltpu.make_async_remote_copy(
                    src_ref=passed, dst_ref=passed, send_sem=send_sems.at[3 * n + a * 3 + k],
                    recv_sem=recv_sems.at[3 * n + a * 3 + k], device_id=sibling, device_id_type=MESH).wait_recv()
        for cp in sends:
            cp.wait_send()
        for cp in local:
            cp.wait()

    return _call(body, name="gather_layer",
                 out_shape=[_sds(_full_shape(s.shape, ax), s.dtype) for s, ax in zip(shards, axes)],
                 in_specs=[HBM_SPEC] * n, out_specs=[HBM_SPEC] * n,
                 scratch_shapes=[pltpu.SemaphoreType.DMA((6 * n,)), pltpu.SemaphoreType.DMA((6 * n,)),
                                 pltpu.SemaphoreType.DMA((n,))],
                 compiler_params=_params())(*shards)


def _half_rows_shape(full_shape):
    s = list(full_shape)
    s[-2] //= 2
    return tuple(s)


def _half_shard_shape(shard_shape):
    s = list(shard_shape)
    s[-2] //= 2
    return tuple(s)


def _reduce_sibling(grads):
    n = len(grads)
    axes = [a for _, a in GATHERED]
    shard_shapes = [tuple(d // (N_CHIP if i == len(g.shape) + ax else 1) for i, d in enumerate(g.shape))
                    for g, ax in zip(grads, axes)]

    def pieces(a, ref_full, ref_half, half):
        if axes[a] == -1:
            rs = shard_shapes[a][-2]
            src = ref_full.at[_lead(ref_full) + (pl.ds(_mo(half * (rs // 2), 16), rs // 2), slice(None))]
            return [(src, ref_half)]
        rs = shard_shapes[a][-2]
        out = []
        for j in range(N_CHIP):
            src = _full_region(ref_full, -2, j, half, shard_shapes[a])
            dst = ref_half.at[_lead(ref_half) + (pl.ds(j * (rs // 2), rs // 2), slice(None))]
            out.append((src, dst))
        return out

    n_cp = sum(1 if ax == -1 else N_CHIP for ax in axes)

    def body(*refs):
        ins, own, got = refs[:n], refs[n:2 * n], refs[2 * n:3 * n]
        send_sems, recv_sems, local_sems = refs[3 * n:]
        x, y, c, _ = _place()
        sibling = (x, y, 1 - c)
        started, idx = [], 0
        for a in range(n):
            mine = pieces(a, ins[a], own[a], c)
            theirs = pieces(a, ins[a], got[a], 1 - c)
            for (src_l, dst_l), (src_r, dst_r) in zip(mine, theirs):
                lc = pltpu.make_async_copy(src_l, dst_l, local_sems.at[idx])
                lc.start()
                rc = pltpu.make_async_remote_copy(src_ref=src_r, dst_ref=dst_r, send_sem=send_sems.at[idx],
                                                  recv_sem=recv_sems.at[idx], device_id=sibling, device_id_type=MESH)
                rc.start()
                started.append((lc, rc))
                idx += 1
        for lc, rc in started:
            rc.wait_recv()
            rc.wait_send()
            lc.wait()

    half_shapes = [_half_rows_shape(g.shape) for g in grads]
    outs = _call(body, name="reduce_sibling",
                 out_shape=[_sds(s, BF16) for s in half_shapes] * 2,
                 in_specs=[HBM_SPEC] * n, out_specs=[HBM_SPEC] * (2 * n),
                 scratch_shapes=[pltpu.SemaphoreType.DMA((n_cp,)), pltpu.SemaphoreType.DMA((n_cp,)),
                                 pltpu.SemaphoreType.DMA((n_cp,))],
                 compiler_params=_params())(*grads)
    return outs[:n], outs[n:], shard_shapes


def _reduce_chips(parts, shard_shapes):
    n = len(parts)
    axes = [a for _, a in GATHERED]

    def region(a, ref, j):
        hs = _half_shard_shape(shard_shapes[a])
        if axes[a] == -1:
            return ref.at[_lead(ref) + (slice(None), pl.ds(_mo(j * hs[-1], 128), hs[-1]))]
        return ref.at[_lead(ref) + (pl.ds(_mo(j * hs[-2], 16), hs[-2]), slice(None))]

    def body(*refs):
        ins, mine, got = refs[:n], refs[n:2 * n], refs[2 * n:5 * n]
        send_sems, recv_sems, local_sems = refs[5 * n:]
        x, y, c, chips = _place()
        my = _chip(x, y)
        started = []
        for a in range(n):
            lc = pltpu.make_async_copy(region(a, ins[a], my), mine[a], local_sems.at[a])
            lc.start()
            started.append(lc)
        sends = []
        for a in range(n):
            for k, chip in enumerate(chips):
                cp = pltpu.make_async_remote_copy(
                    src_ref=region(a, ins[a], _chip(*chip)), dst_ref=got[a * 3 + k],
                    send_sem=send_sems.at[a * 3 + k], recv_sem=recv_sems.at[a * 3 + k],
                    device_id=(*chip, c), device_id_type=MESH)
                cp.start()
                sends.append(cp)
        for cp in sends:
            cp.wait_recv()
            cp.wait_send()
        for lc in started:
            lc.wait()

    hs = [_half_shard_shape(s) for s in shard_shapes]
    outs = _call(body, name="reduce_chips",
                 out_shape=[_sds(s, BF16) for s in hs] + [_sds(s, BF16) for s in hs for _ in range(3)],
                 in_specs=[HBM_SPEC] * n, out_specs=[HBM_SPEC] * (4 * n),
                 scratch_shapes=[pltpu.SemaphoreType.DMA((3 * n,)), pltpu.SemaphoreType.DMA((3 * n,)),
                                 pltpu.SemaphoreType.DMA((n,))],
                 compiler_params=_params())(*parts)
    return outs[:n], [outs[n + 3 * a:n + 3 * a + 3] for a in range(n)]


def _share_halves(halves):
    n = len(halves)

    def body(*refs):
        ins, outs = refs[:n], refs[n:2 * n]
        send_sems, recv_sems, local_sems = refs[2 * n:]
        x, y, c, _ = _place()
        sibling = (x, y, 1 - c)
        started = []
        for a in range(n):
            lc = pltpu.make_async_copy(ins[a], _shard_half(outs[a], c), local_sems.at[a])
            lc.start()
            rc = pltpu.make_async_remote_copy(src_ref=ins[a], dst_ref=_shard_half(outs[a], c), send_sem=send_sems.at[a],
                                              recv_sem=recv_sems.at[a], device_id=sibling, device_id_type=MESH)
            rc.start()
            started.append((lc, rc))
        for lc, rc in started:
            rc.wait_recv()
            rc.wait_send()
            lc.wait()

    def whole(s):
        s = list(s)
        s[-2] *= 2
        return tuple(s)

    return _call(body, name="share_halves", out_shape=[_sds(whole(h.shape), F32) for h in halves],
                 in_specs=[HBM_SPEC] * n, out_specs=[HBM_SPEC] * n,
                 scratch_shapes=[pltpu.SemaphoreType.DMA((n,)), pltpu.SemaphoreType.DMA((n,)), pltpu.SemaphoreType.DMA((n,))],
                 compiler_params=_params())(*halves)


def _reduce_layer(grads):
    own, got, shard_shapes = _reduce_sibling(grads)
    chip_parts = [_sum_parts([o, g], BF16, "chip_partial") for o, g in zip(own, got)]
    mine, landed = _reduce_chips(chip_parts, shard_shapes)
    halves = [_sum_parts([m, l[0], l[1], l[2]], F32, "shard_half_sum") for m, l in zip(mine, landed)]
    return _share_halves(halves)


def _embed(blocks):
    n, r, c = blocks.shape
    eye = jnp.eye(n, dtype=blocks.dtype)
    return (blocks[:, :, None, :] * eye[:, None, :, None]).reshape(n * r, n * c)


def _unembed(mat, n):
    r, c = mat.shape[0] // n, mat.shape[1] // n
    return jnp.transpose(jnp.diagonal(mat.reshape(n, r, n, c), axis1=0, axis2=2), (2, 0, 1))


def _to_heads(a):
    return jnp.transpose(a.reshape(a.shape[0], HEADS, HD), (1, 0, 2))


def _from_heads(a):
    return jnp.transpose(a, (1, 0, 2)).reshape(a.shape[1], W)


def _row(v):
    return v.reshape(1, -1)


def _ffn_fwd(x, ada, gp, gq, w_in, w_out, s):
    L = x.shape[0]
    h = _norm_mod(x, _row(gp[s]), _row(ada[3 * s]), _row(ada[3 * s + 1]))
    a, b, act = _ffn_in(h, w_in)
    f = _mm(act, w_out, M=L, N=D, K=FF, tm=min(L, 512), tn=512, name="ffn_out")
    x2 = _post(x, f, _row(gq[s]), _row(ada[3 * s + 2]), 0.5)
    return x2, (x, h, a, b, act, f)


def _ffn_bwd(dx, saved, ada, gp, gq, w_in, w_out, s):
    x, h, a, b, act, f = saved
    L = x.shape[0]
    df, dgate, dgq = _post_bwd(dx, f, _row(gq[s]), _row(ada[3 * s + 2]), 0.5)
    dw_out = _mm(act, df, M=FF, N=D, K=L, tm=256, tn=512, ta=True, out_dtype=BF16, name="ffn_dw_out")
    da, db = _ffn_mid_bwd(df, w_out, a, b)
    du = jnp.concatenate([da, db], axis=1)
    dw_in = _mm(h, du, M=D, N=2 * FF, K=L, tm=512, tn=512, ta=True, out_dtype=BF16, name="ffn_dw_in")
    dh = _mm(du, w_in, M=L, N=D, K=2 * FF, tm=min(L, 512), tn=512, tk=1408, tb=True, name="ffn_dh")
    dx2, dshift, dscale, dgp = _norm_mod_bwd(dh, x, _row(gp[s]), _row(ada[3 * s + 1]), dx)
    return dx2, dw_in, dw_out, (dshift, dscale, dgate), dgp, dgq


def _mixer_fwd(x, ada, gp, gq, wf, sm):
    L = x.shape[0]
    h = _norm_mod(x, _row(gp[1]), _row(ada[3]), _row(ada[4]))
    p = _mm(h, wf["w_in"], M=L, N=IN_COLS, K=D, tm=min(L, 512), tn=512, name="mixer_in")
    za = _conv_fwd(p, sm["conv_w"])
    y, zb = _ssm_fwd(p, sm["b_re"], sm["b_im"], sm["c_re"], sm["c_im"], sm["abr"], sm["abi"], sm["fr"], sm["fi"], sm["ssm_d"])
    zc = _pool_fwd(p, sm["w_pool"], sm["pool_scale"])
    q = _to_heads(p[:, 5 * W:6 * W]) * (HD ** -0.5)
    k = _to_heads(p[:, 6 * W:7 * W])
    v = _to_heads(p[:, 7 * W:8 * W])
    zd = _from_heads(_sb_fwd(q, k, v)).astype(BF16)
    merged = _merge_fwd(za, zb, zc, zd, p, wf["w_conv_out"], wf["w_glu"], wf["w_pool_out"], wf["w_sb_out"])
    m = _mm(merged, wf["w_out"], M=L, N=D, K=D, tm=min(L, 512), tn=512, name="mixer_out")
    x2 = _post(x, m, _row(gq[1]), _row(ada[5]), 1.0)
    return x2, (x, h, p, za, y, zb, zc, zd, q, k, v, merged, m)


def _mixer_bwd(dx, saved, ada, gp, gq, wf, sm):
    x, h, p, za, y, zb, zc, zd, q, k, v, merged, m = saved
    L = x.shape[0]
    dmf, dgate, dgq = _post_bwd(dx, m, _row(gq[1]), _row(ada[5]), 1.0)
    dw_out = _mm(merged, dmf, M=D, N=D, K=L, tm=512, tn=512, ta=True, out_dtype=BF16, name="mixer_dw_out")
    dmerged = _mm(dmf, wf["w_out"], M=L, N=D, K=D, tm=min(L, 512), tn=512, tb=True, name="mixer_dmerged")
    dza, dzb, dzc, dzd, dgates, dwc, dwg, dwp, dws = _merge_bwd(
        dmerged, za, zb, zc, zd, p, wf["w_conv_out"], wf["w_glu"], wf["w_pool_out"], wf["w_sb_out"])
    dconv, dconv_w = _conv_bwd(p, sm["conv_w"], dza)
    (du_ssm, dd, dbr, dbi, dcr, dci, gar, gai, gfr, gfi) = _ssm_bwd(
        p, y, dzb, sm["b_re"], sm["b_im"], sm["c_re"], sm["c_im"], sm["abr"], sm["abi"], sm["fr"], sm["fi"], sm["ssm_d"])
    du_pool, dwpool, dpscale = _pool_bwd(p, sm["w_pool"], sm["pool_scale"], dzc)
    dq, dk, dv = _sb_bwd(q, k, v, _to_heads(dzd))
    dqkv = [_from_heads(t).astype(BF16) for t in (dq * (HD ** -0.5), dk, dv)]
    dp = jnp.concatenate([dconv, du_ssm, du_pool] + dqkv + [dgates], axis=1)
    dw_in = _mm(h, dp, M=D, N=IN_COLS, K=L, tm=512, tn=512, ta=True, out_dtype=BF16, name="mixer_dw_in")
    dh = _mm(dp, wf["w_in"], M=L, N=D, K=IN_COLS, tm=min(L, 512), tn=512, tk=1536, tb=True, name="mixer_dh")
    dx2, dshift, dscale, dgp = _norm_mod_bwd(dh, x, _row(gp[1]), _row(ada[4]), dx)
    wgrads = {"w_in": dw_in, "w_conv_out": dwc, "w_glu": dwg, "w_pool_out": dwp, "w_sb_out": dws, "w_out": dw_out}
    small = {"conv_w": dconv_w, "ssm_d": dd, "b_re": dbr, "b_im": dbi, "c_re": dcr, "c_im": dci,
             "abr": gar, "abi": gai, "fr": gfr, "fi": gfi, "w_pool": dwpool, "pool_scale": dpscale}
    return dx2, wgrads, small, (dshift, dscale, dgate), dgp, dgq


def _layer_fwd(x, ada, gp, gq, wf, sm):
    x, s0 = _ffn_fwd(x, ada, gp, gq, wf["w_ff_in"][0], wf["w_ff_out"][0], 0)
    x, s1 = _mixer_fwd(x, ada, gp, gq, wf, sm)
    x, s2 = _ffn_fwd(x, ada, gp, gq, wf["w_ff_in"][1], wf["w_ff_out"][1], 2)
    return x, (s0, s1, s2)


def _layer_bwd(dx, saved, ada, gp, gq, wf, sm):
    s0, s1, s2 = saved
    dx, dwi2, dwo2, dada2, dgp2, dgq2 = _ffn_bwd(dx, s2, ada, gp, gq, wf["w_ff_in"][1], wf["w_ff_out"][1], 2)
    dx, wgrads, small, dada1, dgp1, dgq1 = _mixer_bwd(dx, s1, ada, gp, gq, wf, sm)
    dx, dwi0, dwo0, dada0, dgp0, dgq0 = _ffn_bwd(dx, s0, ada, gp, gq, wf["w_ff_in"][0], wf["w_ff_out"][0], 0)
    wgrads["w_ff_in"] = jnp.stack([dwi0, dwi2])
    wgrads["w_ff_out"] = jnp.stack([dwo0, dwo2])
    small["d_ada"] = jnp.concatenate(list(dada0) + list(dada1) + list(dada2), axis=1).reshape(-1)
    small["g_pre"] = jnp.concatenate([dgp0, dgp1, dgp2], axis=0)
    small["g_post"] = jnp.concatenate([dgq0, dgq1, dgq2], axis=0)
    return dx, wgrads, small


def _pack(arrays):
    flat = jnp.concatenate([a.reshape(-1) for a in arrays])
    rows = -(-flat.shape[0] // 128)
    rows = -(-rows // 64) * 64
    return jnp.pad(flat, (0, rows * 128 - flat.shape[0])).reshape(rows, 128)


def _unpack(block, shapes):
    flat = block.reshape(-1)
    out, off = [], 0
    for s in shapes:
        n = int(np.prod(s))
        out.append(flat[off:off + n].reshape(s))
        off += n
    return out


def _pad_rows(a, mult):
    rows = -(-a.shape[0] // mult) * mult
    return jnp.concatenate([a] * (-(-rows // a.shape[0])), axis=0)[:rows]


SMALL_ORDER = ("d_ada", "g_pre", "g_post", "conv_w", "lam_re", "lam_im", "log_dt", "ssm_b_re", "ssm_b_im",
               "ssm_c_re", "ssm_c_im", "ssm_d", "w_pool", "pool_scale")
WEIGHTS = ('w_ada', 'b_ada', 'g_pre', 'g_post', 'w_ff_in', 'w_ff_out', 'w_in', 'conv_w', 'w_conv_out', 'lam_re', 'lam_im',
           'log_dt', 'ssm_b_re', 'ssm_b_im', 'ssm_c_re', 'ssm_c_im', 'ssm_d', 'w_glu', 'w_pool', 'pool_scale', 'w_pool_out',
           'w_sb_out', 'w_out')


def _step(a):
    depth = a["w_ada"].shape[0]
    x = a["x"][0]
    target = a["loss_target"][0]
    L = x.shape[0]
    ix, iy, ic = lax.axis_index("x"), lax.axis_index("y"), lax.axis_index("c")
    chip = 2 * ix + iy
    me = 4 * ix + 2 * iy + ic

    first_shapes = [(D,), (depth, 3, W), (depth, 3, W), (depth, 3, W // N_CHIP)]
    gathered = _allgather8(_pack([a["c"], a["g_pre"], a["g_post"], a["conv_w"]]), "gather_small_inputs")
    per_dev = [_unpack(blk, first_shapes) for blk in gathered.reshape(N_DEV, -1, 128)]
    c_all = jnp.stack([d[0] for d in per_dev])
    c_pad = jnp.concatenate([c_all, jnp.zeros_like(c_all)], axis=0)
    g_pre = jnp.concatenate([per_dev[2 * j][1] for j in range(N_CHIP)], axis=-1)
    g_post = jnp.concatenate([per_dev[2 * j][2] for j in range(N_CHIP)], axis=-1)
    conv_w = jnp.concatenate([per_dev[2 * j][3] for j in range(N_CHIP)], axis=-1)

    b_cols = lax.dynamic_slice(a["b_ada"], (0, chip * ADA_SHARD), (depth, ADA_SHARD)).reshape(depth, 1, ADA_SHARD)
    ada_part = _ada_fwd(c_pad, a["w_ada"], b_cols)
    ada_all = _allgather8(ada_part.reshape(depth * 16, ADA_SHARD), "gather_ada").reshape(N_DEV, depth, 16, ADA_SHARD)
    ada_rows = lax.dynamic_slice(ada_all, (0, 0, me, 0), (N_DEV, depth, 1, ADA_SHARD))[:, :, 0]
    ada = jnp.concatenate([ada_rows[2 * j] for j in range(N_CHIP)], axis=-1).reshape(depth, 9, D)

    lam_re = _pad_rows(a["lam_re"].reshape(depth, NST), 8)
    lam_im = _pad_rows(a["lam_im"].reshape(depth, NST), 8)
    log_dt_x = _pad_rows(jnp.repeat(a["log_dt"], GP, axis=1), 8)
    abr, abi, fr, fi = _ssm_prep(lam_re, lam_im, log_dt_x)

    def small_of(l):
        return {"conv_w": conv_w[l], "ssm_d": _row(a["ssm_d"][l]), "pool_scale": _row(a["pool_scale"][l]),
                "b_re": _embed(jnp.transpose(a["ssm_b_re"][l], (0, 2, 1))), "b_im": _embed(jnp.transpose(a["ssm_b_im"][l], (0, 2, 1))),
                "c_re": _embed(jnp.transpose(a["ssm_c_re"][l], (0, 2, 1))), "c_im": _embed(jnp.transpose(a["ssm_c_im"][l], (0, 2, 1))),
                "w_pool": _embed(a["w_pool"][l]),
                "abr": abr[l:l + 1], "abi": abi[l:l + 1], "fr": fr[l:l + 1], "fi": fi[l:l + 1]}

    full_w, smalls, saved = [], [], []
    for l in range(depth):
        full = _gather_layer([a[name][l].astype(BF16) for name, _ in GATHERED])
        full_w.append({name: w for (name, _), w in zip(GATHERED, full)})
        smalls.append(small_of(l))
        x, s = _layer_fwd(x, ada[l], g_pre[l], g_post[l], full_w[l], smalls[l])
        saved.append(s)
    dx, loss_part = _loss_head(x, target)
    loss = lax.psum(loss_part[0, 0], ("x", "y", "c"))

    shard_grads = {name: [None] * depth for name, _ in GATHERED}
    small_grads = [None] * depth
    for l in reversed(range(depth)):
        dx, wgrads, small = _layer_bwd(dx, saved[l], ada[l], g_pre[l], g_post[l], full_w[l], smalls[l])
        for (name, _), g in zip(GATHERED, _reduce_layer([wgrads[name] for name, _ in GATHERED])):
            shard_grads[name][l] = g
        small_grads[l] = small

    stack = lambda key: _pad_rows(jnp.concatenate([small_grads[l][key] for l in range(depth)], axis=0), 8)
    gs = np.zeros((NST, 128), np.float32)
    gs[np.arange(NST), np.arange(NST) // GP] = 1.0
    dlr, dli, dldt = _ssm_prep_bwd(lam_re, lam_im, log_dt_x, stack("abr"), stack("abi"), stack("fr"), stack("fi"), jnp.asarray(gs))
    part = {
        "d_ada": jnp.stack([small_grads[l]["d_ada"] for l in range(depth)]),
        "g_pre": jnp.stack([small_grads[l]["g_pre"] for l in range(depth)]),
        "g_post": jnp.stack([small_grads[l]["g_post"] for l in range(depth)]),
        "conv_w": jnp.stack([small_grads[l]["conv_w"] for l in range(depth)]),
        "lam_re": dlr[:depth].reshape(depth, G, GP), "lam_im": dli[:depth].reshape(depth, G, GP), "log_dt": dldt[:depth, :G],
        "ssm_b_re": jnp.stack([jnp.transpose(_unembed(small_grads[l]["b_re"], G), (0, 2, 1)) for l in range(depth)]),
        "ssm_b_im": jnp.stack([jnp.transpose(_unembed(small_grads[l]["b_im"], G), (0, 2, 1)) for l in range(depth)]),
        "ssm_c_re": jnp.stack([jnp.transpose(_unembed(small_grads[l]["c_re"], G), (0, 2, 1)) for l in range(depth)]),
        "ssm_c_im": jnp.stack([jnp.transpose(_unembed(small_grads[l]["c_im"], G), (0, 2, 1)) for l in range(depth)]),
        "ssm_d": jnp.stack([small_grads[l]["ssm_d"][0] for l in range(depth)]),
        "w_pool": jnp.stack([_unembed(small_grads[l]["w_pool"], len(POOL_WINDOWS)) for l in range(depth)]),
        "pool_scale": jnp.stack([small_grads[l]["pool_scale"][0] for l in range(depth)]),
    }
    small_shapes = [part[k].shape for k in SMALL_ORDER]
    blocks = _allgather8(_pack([part[k] for k in SMALL_ORDER]), "gather_small_grads").reshape(N_DEV, -1, 128)
    total = dict(zip(SMALL_ORDER, _unpack(_sum_parts([blocks[i] for i in range(N_DEV)], F32, "small_grad_sum"), small_shapes)))

    grads = {name: jnp.stack(shard_grads[name]) for name, _ in GATHERED}
    d_ada_all = jnp.stack([_unpack(blocks[i], small_shapes[:1])[0] for i in range(N_DEV)])
    d_cols = lax.dynamic_slice(d_ada_all, (0, 0, chip * ADA_SHARD), (N_DEV, depth, ADA_SHARD))
    d_cols = jnp.transpose(d_cols, (1, 0, 2))
    grads["w_ada"] = _ada_wgrad(c_pad, jnp.concatenate([d_cols, jnp.zeros_like(d_cols)], axis=1))
    grads["b_ada"] = total["d_ada"]
    grads["g_pre"] = lax.dynamic_slice(total["g_pre"], (0, 0, chip * W), (depth, 3, W))
    grads["g_post"] = lax.dynamic_slice(total["g_post"], (0, 0, chip * W), (depth, 3, W))
    grads["conv_w"] = lax.dynamic_slice(total["conv_w"], (0, 0, chip * (W // N_CHIP)), (depth, 3, W // N_CHIP))
    for k in SMALL_ORDER[4:]:
        grads[k] = total[k]

    out = {"loss": loss, "grad_x": dx[None]}
    for name in WEIGHTS:
        out["grad_" + name] = grads[name]
        out["delta_" + name], out["new_m_" + name], out["new_v_" + name] = _adamw(a[name], grads[name], a["m_" + name], a["v_" + name])
    return out


def kernel(x, c, w_ada, b_ada, g_pre, g_post, w_ff_in, w_ff_out, w_in, conv_w, w_conv_out, lam_re, lam_im, log_dt, ssm_b_re, ssm_b_im, ssm_c_re, ssm_c_im, ssm_d, w_glu, w_pool, pool_scale, w_pool_out, w_sb_out, w_out, loss_target, m_w_ada, m_b_ada, m_g_pre, m_g_post, m_w_ff_in, m_w_ff_out, m_w_in, m_conv_w, m_w_conv_out, m_lam_re, m_lam_im, m_log_dt, m_ssm_b_re, m_ssm_b_im, m_ssm_c_re, m_ssm_c_im, m_ssm_d, m_w_glu, m_w_pool, m_pool_scale, m_w_pool_out, m_w_sb_out, m_w_out, v_w_ada, v_b_ada, v_g_pre, v_g_post, v_w_ff_in, v_w_ff_out, v_w_in, v_conv_w, v_w_conv_out, v_lam_re, v_lam_im, v_log_dt, v_ssm_b_re, v_ssm_b_im, v_ssm_c_re, v_ssm_c_im, v_ssm_d, v_w_glu, v_w_pool, v_pool_scale, v_w_pool_out, v_w_sb_out, v_w_out):
    out = _step(dict(locals()))
    names = ["loss", "grad_x"] + [p + n for p in ("grad_", "delta_", "new_m_", "new_v_") for n in WEIGHTS]
    return tuple(out[n] for n in names)
```

```python
import functools
import math

import jax
import jax.numpy as jnp
import numpy as np
from jax import lax
from jax.experimental import pallas as pl
from jax.experimental.pallas import tpu as pltpu

F32 = jnp.float32
BF16 = jnp.bfloat16
MESH = pl.DeviceIdType.MESH

D = 1024
W = 256
FF = 2816
IN_COLS = 6144
G = 16
GH = 16
GP = 64
NST = G * GP
QB = 128
HEADS = 4
HD = 64
EPS = 1e-6
LAMBDA_RE_MAX = -1e-4
POOL_WINDOWS = (2, 4, 8, 16)
N_CHIP = 4
N_DEV = 8
VMEM_LIMIT = 56 * 1024 * 1024
HIGH = lax.Precision.HIGHEST

ADAM_LR, ADAM_B1, ADAM_B2, ADAM_EPS, ADAM_WD, ADAM_STEP = 0.001, 0.9, 0.999, 1e-08, 0.01, 10


def _call(body, **kw):
    return pl.pallas_call(body, **kw)


def _params(dims=None, **kw):
    return pltpu.CompilerParams(dimension_semantics=dims, vmem_limit_bytes=VMEM_LIMIT, **kw)


def _sds(shape, dtype):
    return jax.ShapeDtypeStruct(shape, dtype)


def _dot(a, b, ca=1, cb=0, precision=None):
    return lax.dot_general(a, b, (((ca,), (cb,)), ((), ())), preferred_element_type=F32, precision=precision)


def _bdot(a, b, ca=1, cb=0):
    return _dot(a.astype(BF16), b.astype(BF16), ca, cb)


def _sigmoid(x):
    return 1.0 / (1.0 + jnp.exp(-x))


def _mm(a, b, *, M, N, K, tm, tn, tk=None, ta=False, tb=False, out_dtype=F32, a_off=(0, 0), b_off=(0, 0), name):
    tk = K if tk is None else tk
    nk = K // tk
    assert M % tm == 0 and N % tn == 0 and K % tk == 0

    def body(a_ref, b_ref, o_ref, *acc):
        part = _bdot(a_ref[...], b_ref[...], 0 if ta else 1, 1 if tb else 0)
        if nk == 1:
            o_ref[...] = part.astype(out_dtype)
            return
        acc_ref = acc[0]
        k = pl.program_id(2)

        @pl.when(k == 0)
        def _():
            acc_ref[...] = part

        @pl.when(k > 0)
        def _():
            acc_ref[...] += part

        @pl.when(k == nk - 1)
        def _():
            o_ref[...] = acc_ref[...].astype(out_dtype)

    if ta:
        a_spec = pl.BlockSpec((tk, tm), lambda i, j, k: (k + a_off[0], i + a_off[1]))
    else:
        a_spec = pl.BlockSpec((tm, tk), lambda i, j, k: (i + a_off[0], k + a_off[1]))
    if tb:
        b_spec = pl.BlockSpec((tn, tk), lambda i, j, k: (j + b_off[0], k + b_off[1]))
    else:
        b_spec = pl.BlockSpec((tk, tn), lambda i, j, k: (k + b_off[0], j + b_off[1]))
    return _call(
        body, name=name, grid=(M // tm, N // tn, nk),
        in_specs=[a_spec, b_spec],
        out_specs=pl.BlockSpec((tm, tn), lambda i, j, k: (i, j)),
        out_shape=_sds((M, N), out_dtype),
        scratch_shapes=[] if nk == 1 else [pltpu.VMEM((tm, tn), F32)],
        compiler_params=_params(("parallel", "parallel", "arbitrary")),
    )(a, b)


def _row_tile(L):
    return min(L, 256)


def _norm_mod(x, g, shift, scale):
    L = x.shape[0]
    tr = _row_tile(L)

    def body(x_ref, g_ref, sh_ref, sc_ref, h_ref):
        xv = x_ref[...]
        r = lax.rsqrt(jnp.mean(xv * xv, axis=-1, keepdims=True) + EPS)
        h_ref[...] = (xv * r * g_ref[...] * (1.0 + sc_ref[...]) + sh_ref[...]).astype(BF16)

    row = pl.BlockSpec((tr, D), lambda i: (i, 0))
    vec = pl.BlockSpec((1, D), lambda i: (0, 0))
    return _call(body, name="norm_mod", grid=(L // tr,), in_specs=[row, vec, vec, vec], out_specs=row,
                 out_shape=_sds((L, D), BF16), compiler_params=_params(("parallel",)))(x, g, shift, scale)


def _norm_mod_bwd(dh, x, g, scale, dx_res):
    L = x.shape[0]
    tr = _row_tile(L)

    def body(dh_ref, x_ref, g_ref, sc_ref, dxr_ref, dx_ref, dsh_ref, dsc_ref, dg_ref):
        i = pl.program_id(0)
        xv = x_ref[...]
        dhv = dh_ref[...]
        r = lax.rsqrt(jnp.mean(xv * xv, axis=-1, keepdims=True) + EPS)
        y = xv * r
        n = y * g_ref[...]
        dn = dhv * (1.0 + sc_ref[...])
        dy = dn * g_ref[...]
        dx_ref[...] = dxr_ref[...] + r * (dy - y * jnp.mean(dy * y, axis=-1, keepdims=True))

        @pl.when(i == 0)
        def _():
            dsh_ref[...] = jnp.zeros_like(dsh_ref)
            dsc_ref[...] = jnp.zeros_like(dsc_ref)
            dg_ref[...] = jnp.zeros_like(dg_ref)

        dsh_ref[...] += jnp.sum(dhv, axis=0, keepdims=True)
        dsc_ref[...] += jnp.sum(dhv * n, axis=0, keepdims=True)
        dg_ref[...] += jnp.sum(dn * y, axis=0, keepdims=True)

    row = pl.BlockSpec((tr, D), lambda i: (i, 0))
    vec = pl.BlockSpec((1, D), lambda i: (0, 0))
    return _call(body, name="norm_mod_bwd", grid=(L // tr,), in_specs=[row, row, vec, vec, row],
                 out_specs=[row, vec, vec, vec],
                 out_shape=[_sds((L, D), F32), _sds((1, D), F32), _sds((1, D), F32), _sds((1, D), F32)],
                 compiler_params=_params(("arbitrary",)))(dh, x, g, scale, dx_res)


def _post(x, f, g, gate, res_weight):
    L = x.shape[0]
    tr = _row_tile(L)

    def body(x_ref, f_ref, g_ref, gt_ref, o_ref):
        fv = f_ref[...]
        r = lax.rsqrt(jnp.mean(fv * fv, axis=-1, keepdims=True) + EPS)
        o_ref[...] = x_ref[...] + (res_weight * (1.0 + gt_ref[...])) * (fv * r * g_ref[...])

    row = pl.BlockSpec((tr, D), lambda i: (i, 0))
    vec = pl.BlockSpec((1, D), lambda i: (0, 0))
    return _call(body, name="post", grid=(L // tr,), in_specs=[row, row, vec, vec], out_specs=row,
                 out_shape=_sds((L, D), F32), compiler_params=_params(("parallel",)))(x, f, g, gate)


def _post_bwd(dx, f, g, gate, res_weight):
    L = dx.shape[0]
    tr = _row_tile(L)

    def body(dx_ref, f_ref, g_ref, gt_ref, df_ref, dgt_ref, dg_ref):
        i = pl.program_id(0)
        fv = f_ref[...]
        dxv = dx_ref[...]
        r = lax.rsqrt(jnp.mean(fv * fv, axis=-1, keepdims=True) + EPS)
        y = fv * r
        dn = dxv * (res_weight * (1.0 + gt_ref[...]))
        dy = dn * g_ref[...]
        df_ref[...] = (r * (dy - y * jnp.mean(dy * y, axis=-1, keepdims=True))).astype(BF16)

        @pl.when(i == 0)
        def _():
            dgt_ref[...] = jnp.zeros_like(dgt_ref)
            dg_ref[...] = jnp.zeros_like(dg_ref)

        dgt_ref[...] += res_weight * jnp.sum(dxv * (y * g_ref[...]), axis=0, keepdims=True)
        dg_ref[...] += jnp.sum(dn * y, axis=0, keepdims=True)

    row = pl.BlockSpec((tr, D), lambda i: (i, 0))
    vec = pl.BlockSpec((1, D), lambda i: (0, 0))
    return _call(body, name="post_bwd", grid=(L // tr,), in_specs=[row, row, vec, vec],
                 out_specs=[row, vec, vec],
                 out_shape=[_sds((L, D), BF16), _sds((1, D), F32), _sds((1, D), F32)],
                 compiler_params=_params(("arbitrary",)))(dx, f, g, gate)


def _loss_head(x, target):
    L = x.shape[0]
    tr = _row_tile(L)

    def body(x_ref, t_ref, dx_ref, loss_ref):
        i = pl.program_id(0)
        err = x_ref[...] - t_ref[...]
        dx_ref[...] = err * (1.0 / D)

        @pl.when(i == 0)
        def _():
            loss_ref[...] = jnp.zeros_like(loss_ref)

        loss_ref[...] += 0.5 * jnp.sum(jnp.mean(err * err, axis=-1, keepdims=True), axis=0, keepdims=True)

    row = pl.BlockSpec((tr, D), lambda i: (i, 0))
    return _call(body, name="loss_head", grid=(L // tr,), in_specs=[row, row],
                 out_specs=[row, pl.BlockSpec((1, 1), lambda i: (0, 0))],
                 out_shape=[_sds((L, D), F32), _sds((1, 1), F32)],
                 compiler_params=_params(("arbitrary",)))(x, target)


def _ffn_in(h, w_in):
    L = h.shape[0]
    tm, tn = min(L, 512), 256
    nf = FF // tn

    def body(h_ref, wa_ref, wb_ref, a_ref, b_ref, act_ref):
        hv = h_ref[...]
        a = _dot(hv, wa_ref[...])
        b = _dot(hv, wb_ref[...])
        a_ref[...] = a
        b_ref[...] = b
        act_ref[...] = (a * _sigmoid(a) * b).astype(BF16)

    tile = pl.BlockSpec((tm, tn), lambda i, j: (i, j))
    return _call(body, name="ffn_in", grid=(L // tm, nf),
                 in_specs=[pl.BlockSpec((tm, D), lambda i, j: (i, 0)),
                           pl.BlockSpec((D, tn), lambda i, j: (0, j)),
                           pl.BlockSpec((D, tn), lambda i, j: (0, j + nf))],
                 out_specs=[tile, tile, tile],
                 out_shape=[_sds((L, FF), F32), _sds((L, FF), F32), _sds((L, FF), BF16)],
                 compiler_params=_params(("parallel", "parallel")))(h, w_in, w_in)


def _ffn_mid_bwd(df, w_out, a, b):
    L = df.shape[0]
    tm, tn = min(L, 512), 256

    def body(df_ref, w_ref, a_ref, b_ref, da_ref, db_ref):
        dact = _dot(df_ref[...], w_ref[...], 1, 1)
        av = a_ref[...]
        sg = _sigmoid(av)
        da_ref[...] = (dact * b_ref[...] * (sg * (1.0 + av * (1.0 - sg)))).astype(BF16)
        db_ref[...] = (dact * (av * sg)).astype(BF16)

    tile = pl.BlockSpec((tm, tn), lambda i, j: (i, j))
    return _call(body, name="ffn_mid_bwd", grid=(L // tm, FF // tn),
                 in_specs=[pl.BlockSpec((tm, D), lambda i, j: (i, 0)),
                           pl.BlockSpec((tn, D), lambda i, j: (j, 0)), tile, tile],
                 out_specs=[tile, tile],
                 out_shape=[_sds((L, FF), BF16), _sds((L, FF), BF16)],
                 compiler_params=_params(("parallel", "parallel")))(df, w_out, a, b)


def _rows_before(ref, i, tr, halo):
    start = pl.multiple_of(jnp.maximum(i * tr - halo, 0), 8)
    return jnp.where(i > 0, ref[pl.ds(start, halo), :], 0.0)


def _rows_after(ref, i, n, tr, halo):
    start = pl.multiple_of(jnp.minimum((i + 1) * tr, (n - 1) * tr), 8)
    return jnp.where(i < n - 1, ref[pl.ds(start, halo), :], 0.0)


def _conv_fwd(p, conv_w):
    L = p.shape[0]
    tr = _row_tile(L)
    n = L // tr

    def body(bg_ref, cg_ref, v_ref, w_ref, za_ref, u_scr):
        i = pl.program_id(0)

        @pl.when(i == 0)
        def _():
            u_scr[...] = cg_ref[...] * v_ref[...]

        r0 = pl.multiple_of(i * tr, 8)
        ext = jnp.concatenate([_rows_before(u_scr, i, tr, 8), u_scr[pl.ds(r0, tr), :]], axis=0)
        w = w_ref[...]
        y = (w[0:1] * pltpu.roll(ext, 2, axis=0) + w[1:2] * pltpu.roll(ext, 1, axis=0) + w[2:3] * ext)[8:, :]
        za_ref[...] = (bg_ref[pl.ds(r0, tr), :] * y).astype(BF16)

    col = lambda c: pl.BlockSpec((L, W), lambda i: (0, c))
    return _call(body, name="conv_fwd", grid=(n,),
                 in_specs=[col(0), col(1), col(2), pl.BlockSpec((3, W), lambda i: (0, 0))],
                 out_specs=pl.BlockSpec((tr, W), lambda i: (i, 0)),
                 out_shape=_sds((L, W), BF16),
                 scratch_shapes=[pltpu.VMEM((L, W), F32)],
                 compiler_params=_params(("arbitrary",)))(p, p, p, conv_w)


def _conv_bwd(p, conv_w, dza):
    L = p.shape[0]
    tr = _row_tile(L)
    n = L // tr

    def body(bg_ref, cg_ref, v_ref, w_ref, dza_ref, dp_ref, dw_ref, u_scr, dy_scr):
        i = pl.program_id(0)

        @pl.when(i == 0)
        def _():
            u_scr[...] = cg_ref[...] * v_ref[...]
            dy_scr[...] = dza_ref[...] * bg_ref[...]
            dw_ref[...] = jnp.zeros_like(dw_ref)

        r0 = pl.multiple_of(i * tr, 8)
        w = w_ref[...]
        ext = jnp.concatenate([_rows_before(u_scr, i, tr, 8), u_scr[pl.ds(r0, tr), :]], axis=0)
        u2 = pltpu.roll(ext, 2, axis=0)[8:, :]
        u1 = pltpu.roll(ext, 1, axis=0)[8:, :]
        u0 = ext[8:, :]
        y = w[0:1] * u2 + w[1:2] * u1 + w[2:3] * u0
        dy = dy_scr[pl.ds(r0, tr), :]
        dext = jnp.concatenate([dy, _rows_after(dy_scr, i, n, tr, 8)], axis=0)
        m = tr + 8
        du = (w[2:3] * dext + w[1:2] * pltpu.roll(dext, m - 1, axis=0) + w[0:1] * pltpu.roll(dext, m - 2, axis=0))[:tr, :]
        dp_ref[:, 0:W] = (dza_ref[pl.ds(r0, tr), :] * y).astype(BF16)
        dp_ref[:, W:2 * W] = (du * v_ref[pl.ds(r0, tr), :]).astype(BF16)
        dp_ref[:, 2 * W:3 * W] = (du * cg_ref[pl.ds(r0, tr), :]).astype(BF16)
        dw_ref[...] += jnp.concatenate([jnp.sum(dy * u2, axis=0, keepdims=True),
                                        jnp.sum(dy * u1, axis=0, keepdims=True),
                                        jnp.sum(dy * u0, axis=0, keepdims=True)], axis=0)

    col = lambda c: pl.BlockSpec((L, W), lambda i: (0, c))
    return _call(body, name="conv_bwd", grid=(n,),
                 in_specs=[col(0), col(1), col(2), pl.BlockSpec((3, W), lambda i: (0, 0)),
                           pl.BlockSpec((L, W), lambda i: (0, 0))],
                 out_specs=[pl.BlockSpec((tr, 3 * W), lambda i: (i, 0)), pl.BlockSpec((3, W), lambda i: (0, 0))],
                 out_shape=[_sds((L, 3 * W), BF16), _sds((3, W), F32)],
                 scratch_shapes=[pltpu.VMEM((L, W), F32), pltpu.VMEM((L, W), F32)],
                 compiler_params=_params(("arbitrary",)))(p, p, p, conv_w, dza)


def _pool_windows(lane):
    wins = jnp.zeros(lane.shape, jnp.int32)
    for gi, w in enumerate(POOL_WINDOWS):
        wins = jnp.where(lane // (W // len(POOL_WINDOWS)) == gi, w, wins)
    return wins


def _pooled_block(u_ref, i, tr):
    r0 = pl.multiple_of(i * tr, 8)
    cur = u_ref[pl.ds(r0, tr), :]
    ext = jnp.concatenate([_rows_before(u_ref, i, tr, 16), cur], axis=0)
    s2 = ext + pltpu.roll(ext, 1, axis=0)
    s4 = s2 + pltpu.roll(s2, 2, axis=0)
    s8 = s4 + pltpu.roll(s4, 4, axis=0)
    s16 = s8 + pltpu.roll(s8, 8, axis=0)
    lane = lax.broadcasted_iota(jnp.int32, (tr, W), 1)
    wins = _pool_windows(lane)
    win_sum = jnp.where(wins == 2, s2[16:], jnp.where(wins == 4, s4[16:], jnp.where(wins == 8, s8[16:], s16[16:])))
    t = lax.broadcasted_iota(jnp.int32, (tr, W), 0) + i * tr
    cnt = jnp.minimum(t + 1, wins).astype(F32)
    return win_sum / cnt - cur, cnt


def _pool_fwd(p, w_pool_bd, pool_scale):
    L = p.shape[0]
    tr = _row_tile(L)

    def body(u_ref, w_ref, sc_ref, zc_ref):
        pooled, _ = _pooled_block(u_ref, pl.program_id(0), tr)
        zc_ref[...] = (_bdot(pooled, w_ref[...]) * sc_ref[...]).astype(BF16)

    return _call(body, name="pool_fwd", grid=(L // tr,),
                 in_specs=[pl.BlockSpec((L, W), lambda i: (0, 4)), pl.BlockSpec((W, W), lambda i: (0, 0)),
                           pl.BlockSpec((1, W), lambda i: (0, 0))],
                 out_specs=pl.BlockSpec((tr, W), lambda i: (i, 0)), out_shape=_sds((L, W), BF16),
                 compiler_params=_params(("arbitrary",)))(p, w_pool_bd, pool_scale)


def _pool_bwd(p, w_pool_bd, pool_scale, dzc):
    L = p.shape[0]
    tr = _row_tile(L)
    n = L // tr

    def body(u_ref, w_ref, sc_ref, dzc_ref, du_ref, dw_ref, dsc_ref, g_scr):
        i = pl.program_id(0)

        @pl.when(i == 0)
        def _():
            dw_ref[...] = jnp.zeros_like(dw_ref)
            dsc_ref[...] = jnp.zeros_like(dsc_ref)

            def rows(k, carry):
                r = pl.multiple_of(k * tr, 8)
                dmix = (dzc_ref[pl.ds(r, tr), :] * sc_ref[...]).astype(BF16)
                dpool = _dot(dmix, w_ref[...].astype(BF16), 1, 1)
                lane = lax.broadcasted_iota(jnp.int32, (tr, W), 1)
                t = lax.broadcasted_iota(jnp.int32, (tr, W), 0) + k * tr
                cnt = jnp.minimum(t + 1, _pool_windows(lane)).astype(F32)
                g_scr[pl.ds(r, tr), :] = dpool / cnt
                return carry

            lax.fori_loop(0, n, rows, 0)

        r0 = pl.multiple_of(i * tr, 8)
        pooled, cnt = _pooled_block(u_ref, i, tr)
        dzc = dzc_ref[pl.ds(r0, tr), :]
        mixed = _bdot(pooled, w_ref[...])
        dsc_ref[...] += jnp.sum(dzc * mixed, axis=0, keepdims=True)
        dmix = (dzc * sc_ref[...]).astype(BF16)
        dw_ref[...] += _dot(pooled.astype(BF16), dmix, 0, 0)
        gcur = g_scr[pl.ds(r0, tr), :]
        ext = jnp.concatenate([gcur, _rows_after(g_scr, i, n, tr, 16)], axis=0)
        m = tr + 16
        s2 = ext + pltpu.roll(ext, m - 1, axis=0)
        s4 = s2 + pltpu.roll(s2, m - 2, axis=0)
        s8 = s4 + pltpu.roll(s4, m - 4, axis=0)
        s16 = s8 + pltpu.roll(s8, m - 8, axis=0)
        lane = lax.broadcasted_iota(jnp.int32, (tr, W), 1)
        wins = _pool_windows(lane)
        ahead = jnp.where(wins == 2, s2[:tr], jnp.where(wins == 4, s4[:tr], jnp.where(wins == 8, s8[:tr], s16[:tr])))
        du_ref[...] = (ahead - gcur * cnt).astype(BF16)

    return _call(body, name="pool_bwd", grid=(n,),
                 in_specs=[pl.BlockSpec((L, W), lambda i: (0, 4)), pl.BlockSpec((W, W), lambda i: (0, 0)),
                           pl.BlockSpec((1, W), lambda i: (0, 0)), pl.BlockSpec((L, W), lambda i: (0, 0))],
                 out_specs=[pl.BlockSpec((tr, W), lambda i: (i, 0)), pl.BlockSpec((W, W), lambda i: (0, 0)),
                            pl.BlockSpec((1, W), lambda i: (0, 0))],
                 out_shape=[_sds((L, W), BF16), _sds((W, W), F32), _sds((1, W), F32)],
                 scratch_shapes=[pltpu.VMEM((L, W), F32)],
                 compiler_params=_params(("arbitrary",)))(p, w_pool_bd, pool_scale, dzc)


SSM_SLAB = 512


def _ssm_prep(lam_re, lam_im, log_dt_x):
    def body(lr_ref, li_ref, ldt_ref, abr_ref, abi_ref, fr_ref, fi_ref):
        lr = jnp.minimum(lr_ref[...], LAMBDA_RE_MAX)
        li = li_ref[...]
        dt = jnp.exp(ldt_ref[...])
        mag = jnp.exp(lr * dt)
        abr = mag * jnp.cos(li * dt)
        abi = mag * jnp.sin(li * dt)
        den = lr * lr + li * li
        nr = abr - 1.0
        abr_ref[...] = abr
        abi_ref[...] = abi
        fr_ref[...] = (nr * lr + abi * li) / den
        fi_ref[...] = (abi * lr - nr * li) / den

    shp = _sds(lam_re.shape, F32)
    return _call(body, name="ssm_prep", out_shape=[shp, shp, shp, shp], compiler_params=_params())(lam_re, lam_im, log_dt_x)


def _ssm_prep_bwd(lam_re, lam_im, log_dt_x, g_abr, g_abi, g_fr, g_fi, group_sum):
    def body(lr_ref, li_ref, ldt_ref, gar_ref, gai_ref, gfr_ref, gfi_ref, gs_ref, dlr_ref, dli_ref, dldt_ref):
        lam = lr_ref[...]
        lr = jnp.minimum(lam, LAMBDA_RE_MAX)
        li = li_ref[...]
        dt = jnp.exp(ldt_ref[...])
        mag = jnp.exp(lr * dt)
        abr = mag * jnp.cos(li * dt)
        abi = mag * jnp.sin(li * dt)
        den = lr * lr + li * li
        nr = abr - 1.0
        fr = (nr * lr + abi * li) / den
        fi = (abi * lr - nr * li) / den
        d_nre = gfr_ref[...] / den
        d_nim = gfi_ref[...] / den
        d_den = -(gfr_ref[...] * fr + gfi_ref[...] * fi) / den
        d_abr = gar_ref[...] + d_nre * lr - d_nim * li
        d_abi = gai_ref[...] + d_nre * li + d_nim * lr
        d_lr = d_nre * nr + d_nim * abi + d_den * 2.0 * lr
        d_li = d_nre * abi - d_nim * nr + d_den * 2.0 * li
        d_mag = d_abr * jnp.cos(li * dt) + d_abi * jnp.sin(li * dt)
        d_th = -d_abr * abi + d_abi * abr
        d_lr = d_lr + d_mag * mag * dt
        d_li = d_li + d_th * dt
        d_dt = d_mag * mag * lr + d_th * li
        passes = jnp.where(lam < LAMBDA_RE_MAX, 1.0, jnp.where(lam == LAMBDA_RE_MAX, 0.5, 0.0))
        dlr_ref[...] = d_lr * passes
        dli_ref[...] = d_li
        dldt_ref[...] = _dot(d_dt * dt, gs_ref[...], precision=HIGH)

    shp = _sds(lam_re.shape, F32)
    return _call(body, name="ssm_prep_bwd", out_shape=[shp, shp, _sds((lam_re.shape[0], 128), F32)],
                 compiler_params=_params())(lam_re, lam_im, log_dt_x, g_abr, g_abi, g_fr, g_fi, group_sum)


def _cmul(ar, ai, br, bi):
    return ar * br - ai * bi, ar * bi + ai * br


def _powers(ar, ai):
    out = [(ar, ai)]
    for _ in range(7):
        out.append(_cmul(out[-1][0], out[-1][1], ar, ai))
    return out


def _scan_rows(s_re, s_im, ar, ai, L, reverse=False, visit=None, visit_init=None):
    n = s_re.shape[1]
    pw = _powers(ar, ai)
    row = lax.broadcasted_iota(jnp.int32, (8, n), 0)
    dist = (8 - row) if reverse else (row + 1)
    pr = jnp.zeros((8, n), F32)
    pi = jnp.zeros((8, n), F32)
    for k in range(8):
        pr = jnp.where(dist == k + 1, pw[k][0], pr)
        pi = jnp.where(dist == k + 1, pw[k][1], pi)
    nb = L // 8

    def blk(t, carry):
        cr, ci, acc = carry
        b = (nb - 1 - t) if reverse else t
        r0 = pl.multiple_of(b * 8, 8)
        xr = s_re[pl.ds(r0, 8), :]
        xi = s_im[pl.ds(r0, 8), :]
        for d in (1, 2, 4):
            if reverse:
                keep = row < 8 - d
                sr, si = pltpu.roll(xr, 8 - d, axis=0), pltpu.roll(xi, 8 - d, axis=0)
            else:
                keep = row >= d
                sr, si = pltpu.roll(xr, d, axis=0), pltpu.roll(xi, d, axis=0)
            sr = jnp.where(keep, sr, 0.0)
            si = jnp.where(keep, si, 0.0)
            mr, mi = _cmul(pw[d - 1][0], pw[d - 1][1], sr, si)
            xr, xi = xr + mr, xi + mi
        mr, mi = _cmul(pr, pi, cr, ci)
        xr, xi = xr + mr, xi + mi
        s_re[pl.ds(r0, 8), :] = xr
        s_im[pl.ds(r0, 8), :] = xi
        if visit is not None:
            acc = visit(b, xr, xi, acc)
        if reverse:
            return xr[0:1, :], xi[0:1, :], acc
        return xr[7:8, :], xi[7:8, :], acc

    zero = jnp.zeros((1, n), F32)
    return lax.fori_loop(0, nb, blk, (zero, zero, visit_init if visit is not None else 0))[2]


def _ssm_project(u_ref, wbr, wbi, s_re, s_im, L):
    ch = min(L, 256)

    def rows(k, carry):
        r = pl.multiple_of(k * ch, 8)
        ub = u_ref[pl.ds(r, ch), :].astype(BF16)
        s_re[pl.ds(r, ch), :] = _dot(ub, wbr)
        s_im[pl.ds(r, ch), :] = _dot(ub, wbi)
        return carry

    lax.fori_loop(0, L // ch, rows, 0)


def _gelu(y):
    c = math.sqrt(2.0 / math.pi)
    return 0.5 * y * (1.0 + jnp.tanh(c * (y + 0.044715 * y * y * y)))


def _gelu_grad(y):
    c = math.sqrt(2.0 / math.pi)
    th = jnp.tanh(c * (y + 0.044715 * y * y * y))
    return 0.5 * (1.0 + th) + 0.5 * y * (1.0 - th * th) * c * (1.0 + 3.0 * 0.044715 * y * y)


def _ssm_fwd(p, b_re_bd, b_im_bd, c_re_bd, c_im_bd, abr, abi, fr, fi, d_skip):
    L = p.shape[0]
    ns = NST // SSM_SLAB
    ch = min(L, 256)

    def body(u_ref, br_ref, bi_ref, cr_ref, ci_ref, abr_ref, abi_ref, fr_ref, fi_ref, d_ref,
             y_ref, zb_ref, s_re, s_im):
        j = pl.program_id(0)
        f_re, f_im = fr_ref[...], fi_ref[...]
        wbr = (f_re * br_ref[...] - f_im * bi_ref[...]).astype(BF16)
        wbi = (f_re * bi_ref[...] + f_im * br_ref[...]).astype(BF16)
        _ssm_project(u_ref, wbr, wbi, s_re, s_im, L)
        _scan_rows(s_re, s_im, abr_ref[...], abi_ref[...], L)
        crb = cr_ref[...].astype(BF16)
        cib = ci_ref[...].astype(BF16)

        def rows(k, carry):
            r = pl.multiple_of(k * ch, 8)
            part = _dot(s_re[pl.ds(r, ch), :].astype(BF16), crb) - _dot(s_im[pl.ds(r, ch), :].astype(BF16), cib)

            @pl.when(j == 0)
            def _():
                y_ref[pl.ds(r, ch), :] = part + d_ref[...] * u_ref[pl.ds(r, ch), :]

            @pl.when(j > 0)
            def _():
                y_ref[pl.ds(r, ch), :] += part

            @pl.when(j == ns - 1)
            def _():
                zb_ref[pl.ds(r, ch), :] = _gelu(y_ref[pl.ds(r, ch), :]).astype(BF16)

            return carry

        lax.fori_loop(0, L // ch, rows, 0)

    full = lambda shape: pl.BlockSpec(shape, lambda j: (0, 0))
    lanes = pl.BlockSpec((1, SSM_SLAB), lambda j: (0, j))
    return _call(body, name="ssm_fwd", grid=(ns,),
                 in_specs=[pl.BlockSpec((L, W), lambda j: (0, 3)),
                           pl.BlockSpec((W, SSM_SLAB), lambda j: (0, j)), pl.BlockSpec((W, SSM_SLAB), lambda j: (0, j)),
                           pl.BlockSpec((SSM_SLAB, W), lambda j: (j, 0)), pl.BlockSpec((SSM_SLAB, W), lambda j: (j, 0)),
                           lanes, lanes, lanes, lanes, full((1, W))],
                 out_specs=[full((L, W)), full((L, W))],
                 out_shape=[_sds((L, W), F32), _sds((L, W), BF16)],
                 scratch_shapes=[pltpu.VMEM((L, SSM_SLAB), F32), pltpu.VMEM((L, SSM_SLAB), F32)],
                 compiler_params=_params(("arbitrary",)))(p, b_re_bd, b_im_bd, c_re_bd, c_im_bd, abr, abi, fr, fi, d_skip)


def _ssm_bwd(p, y, dzb, b_re_bd, b_im_bd, c_re_bd, c_im_bd, abr, abi, fr, fi, d_skip):
    L = p.shape[0]
    ns = NST // SSM_SLAB
    ch = min(L, 256)
    n_ch = L // ch

    def body(u_ref, y_ref, dzb_ref, br_ref, bi_ref, cr_ref, ci_ref, abr_ref, abi_ref, fr_ref, fi_ref, d_ref,
             du_ref, dd_ref, dbr_ref, dbi_ref, dcr_ref, dci_ref, gar_ref, gai_ref, gfr_ref, gfi_ref,
             s_re, s_im, l_re, l_im, dy_scr, du_scr):
        j = pl.program_id(0)
        f_re, f_im = fr_ref[...], fi_ref[...]
        b_re, b_im = br_ref[...], bi_ref[...]
        wbr = (f_re * b_re - f_im * b_im).astype(BF16)
        wbi = (f_re * b_im + f_im * b_re).astype(BF16)
        a_re, a_im = abr_ref[...], abi_ref[...]

        @pl.when(j == 0)
        def _():
            def rows(k, acc):
                r = pl.multiple_of(k * ch, 8)
                dy = dzb_ref[pl.ds(r, ch), :] * _gelu_grad(y_ref[pl.ds(r, ch), :])
                dy_scr[pl.ds(r, ch), :] = dy
                du_scr[pl.ds(r, ch), :] = d_ref[...] * dy
                return acc + jnp.sum(dy * u_ref[pl.ds(r, ch), :], axis=0, keepdims=True)

            dd_ref[...] = lax.fori_loop(0, n_ch, rows, jnp.zeros((1, W), F32))

        _ssm_project(u_ref, wbr, wbi, s_re, s_im, L)
        _scan_rows(s_re, s_im, a_re, a_im, L)
        crb = cr_ref[...].astype(BF16)
        cib = ci_ref[...].astype(BF16)

        def rows_c(k, acc):
            dcr, dci = acc
            r = pl.multiple_of(k * ch, 8)
            dyb = dy_scr[pl.ds(r, ch), :].astype(BF16)
            dcr = dcr + _dot(s_re[pl.ds(r, ch), :].astype(BF16), dyb, 0, 0)
            dci = dci - _dot(s_im[pl.ds(r, ch), :].astype(BF16), dyb, 0, 0)
            l_re[pl.ds(r, ch), :] = _dot(dyb, crb, 1, 1)
            l_im[pl.ds(r, ch), :] = -_dot(dyb, cib, 1, 1)
            return dcr, dci

        zc = jnp.zeros((SSM_SLAB, W), F32)
        dcr, dci = lax.fori_loop(0, n_ch, rows_c, (zc, zc))
        dcr_ref[...] = dcr
        dci_ref[...] = dci

        row8 = lax.broadcasted_iota(jnp.int32, (8, SSM_SLAB), 0)

        def visit(b, lr, li, acc):
            ar_acc, ai_acc = acc
            r0 = pl.multiple_of(b * 8, 8)
            rp = pl.multiple_of(jnp.maximum(b * 8 - 8, 0), 8)
            has_prev = b > 0
            pr = jnp.where(has_prev, s_re[pl.ds(rp, 8), :][7:8, :], 0.0)
            pi = jnp.where(has_prev, s_im[pl.ds(rp, 8), :][7:8, :], 0.0)
            sr = jnp.where(row8 >= 1, pltpu.roll(s_re[pl.ds(r0, 8), :], 1, axis=0), pr)
            si = jnp.where(row8 >= 1, pltpu.roll(s_im[pl.ds(r0, 8), :], 1, axis=0), pi)
            return ar_acc + lr * sr + li * si, ai_acc - lr * si + li * sr

        z8 = jnp.zeros((8, SSM_SLAB), F32)
        ar_acc, ai_acc = _scan_rows(l_re, l_im, a_re, -a_im, L, reverse=True, visit=visit, visit_init=(z8, z8))
        gar_ref[...] = jnp.sum(ar_acc, axis=0, keepdims=True)
        gai_ref[...] = jnp.sum(ai_acc, axis=0, keepdims=True)

        def rows_b(k, acc):
            dwr, dwi = acc
            r = pl.multiple_of(k * ch, 8)
            ub = u_ref[pl.ds(r, ch), :].astype(BF16)
            lrb = l_re[pl.ds(r, ch), :].astype(BF16)
            lib = l_im[pl.ds(r, ch), :].astype(BF16)
            du_scr[pl.ds(r, ch), :] += _dot(lrb, wbr, 1, 1) + _dot(lib, wbi, 1, 1)
            return dwr + _dot(ub, lrb, 0, 0), dwi + _dot(ub, lib, 0, 0)

        zb = jnp.zeros((W, SSM_SLAB), F32)
        dwr, dwi = lax.fori_loop(0, n_ch, rows_b, (zb, zb))
        dbr_ref[...] = dwr * f_re + dwi * f_im
        dbi_ref[...] = -dwr * f_im + dwi * f_re
        gfr_ref[...] = jnp.sum(dwr * b_re + dwi * b_im, axis=0, keepdims=True)
        gfi_ref[...] = jnp.sum(-dwr * b_im + dwi * b_re, axis=0, keepdims=True)

        @pl.when(j == ns - 1)
        def _():
            du_ref[...] = du_scr[...].astype(BF16)

    full = lambda shape: pl.BlockSpec(shape, lambda j: (0, 0))
    lanes = pl.BlockSpec((1, SSM_SLAB), lambda j: (0, j))
    bspec = pl.BlockSpec((W, SSM_SLAB), lambda j: (0, j))
    cspec = pl.BlockSpec((SSM_SLAB, W), lambda j: (j, 0))
    slab = lambda: pltpu.VMEM((L, SSM_SLAB), F32)
    return _call(body, name="ssm_bwd", grid=(ns,),
                 in_specs=[pl.BlockSpec((L, W), lambda j: (0, 3)), full((L, W)), full((L, W)),
                           bspec, bspec, cspec, cspec, lanes, lanes, lanes, lanes, full((1, W))],
                 out_specs=[full((L, W)), full((1, W)), bspec, bspec, cspec, cspec, lanes, lanes, lanes, lanes],
                 out_shape=[_sds((L, W), BF16), _sds((1, W), F32), _sds((W, NST), F32), _sds((W, NST), F32),
                            _sds((NST, W), F32), _sds((NST, W), F32)] + [_sds((1, NST), F32)] * 4,
                 scratch_shapes=[slab(), slab(), slab(), slab(), pltpu.VMEM((L, W), F32), pltpu.VMEM((L, W), F32)],
                 compiler_params=_params(("arbitrary",)))(p, y, dzb, b_re_bd, b_im_bd, c_re_bd, c_im_bd,
                                                          abr, abi, fr, fi, d_skip)


SB_KB = 512


def _split3(x):
    hi = x.astype(BF16)
    r1 = x - hi.astype(F32)
    mid = r1.astype(BF16)
    return hi, mid, (r1 - mid.astype(F32)).astype(BF16)


def _ones_dot(x, ones):
    n = x.shape[0]
    r = _dot(jnp.concatenate(_split3(x), axis=0), ones)
    return r[:n] + r[n:2 * n] + r[2 * n:]


def _sb_block(q, kj, i, jb, kb, right, tri):
    z = _bdot(q, kj, 1, 1)
    t_idx = lax.broadcasted_iota(jnp.int32, (QB, kb), 0) + i * QB
    s_idx = lax.broadcasted_iota(jnp.int32, (QB, kb), 1) + jb * kb
    mask = s_idx < t_idx
    lk_all = jnp.minimum(-z, 0.0) - jnp.log1p(jnp.exp(-jnp.abs(z)))
    lk = jnp.where(mask, lk_all, 0.0)
    suf = _ones_dot(lk, tri)
    a = jnp.where(mask, jnp.exp((lk_all + z) + (suf - lk) + right), 0.0)
    return z, mask, suf, a


def _sb_ones(kb):
    r = lax.broadcasted_iota(jnp.int32, (kb, kb), 0)
    c = lax.broadcasted_iota(jnp.int32, (kb, kb), 1)
    return (r >= c).astype(BF16), (r < c).astype(BF16)


def _sb_fwd(q, k, v):
    L = q.shape[1]
    kb = min(SB_KB, L)
    per = kb // QB

    def body(q_ref, k_ref, v_ref, o_ref, rs_ref):
        i = pl.program_id(0)
        tri, _ = _sb_ones(kb)
        lane = lax.broadcasted_iota(jnp.int32, (QB, 128), 1)
        qs = [q_ref[h] for h in range(HEADS)]

        def step(t, carry):
            accs, rights, sums = carry
            jb = i // per - t
            r = pl.multiple_of(jb * kb, kb)
            out = []
            for h in range(HEADS):
                _, _, suf, a = _sb_block(qs[h], k_ref[h, pl.ds(r, kb), :], i, jb, kb, rights[h], tri)
                tot = suf[:, 0:1]
                out.append((accs[h] + _bdot(a, v_ref[h, pl.ds(r, kb), :]), rights[h] + tot,
                            sums[h] + jnp.where(lane == jb, tot, 0.0)))
            return tuple(o[0] for o in out), tuple(o[1] for o in out), tuple(o[2] for o in out)

        init = (tuple(jnp.zeros((QB, HD), F32) for _ in range(HEADS)), tuple(jnp.zeros((QB, 1), F32) for _ in range(HEADS)),
                tuple(jnp.zeros((QB, 128), F32) for _ in range(HEADS)))
        accs, _, sums = lax.fori_loop(0, i // per + 1, step, init)
        for h in range(HEADS):
            o_ref[h] = accs[h]
            rs_ref[h] = sums[h]

    heads = pl.BlockSpec((HEADS, L, HD), lambda i: (0, 0, 0))
    blk = pl.BlockSpec((HEADS, QB, HD), lambda i: (0, i, 0))
    return _call(body, name="sb_fwd", grid=(L // QB,), in_specs=[blk, heads, heads],
                 out_specs=[blk, pl.BlockSpec((HEADS, QB, 128), lambda i: (0, i, 0))],
                 out_shape=[_sds((HEADS, L, HD), F32), _sds((HEADS, L, 128), F32)],
                 compiler_params=_params(("parallel",)))(q, k, v)


def _sb_bwd(q, k, v, do, block_sums):
    L = q.shape[1]
    kb = min(SB_KB, L)
    per = kb // QB

    def body(q_ref, k_ref, v_ref, do_ref, rs_ref, dq_ref, dk_ref, dv_ref):
        i = pl.program_id(0)
        tri, tri_strict = _sb_ones(kb)
        lane = lax.broadcasted_iota(jnp.int32, (QB, 128), 1)

        @pl.when(i == 0)
        def _():
            dk_ref[...] = jnp.zeros_like(dk_ref)
            dv_ref[...] = jnp.zeros_like(dv_ref)

        qs = [q_ref[h] for h in range(HEADS)]
        dos = [do_ref[h] for h in range(HEADS)]
        sums = [rs_ref[h] for h in range(HEADS)]

        def step(jb, carry):
            dqs, lefts = carry
            r = pl.multiple_of(jb * kb, kb)
            out = []
            for h in range(HEADS):
                kj = k_ref[h, pl.ds(r, kb), :]
                vj = v_ref[h, pl.ds(r, kb), :]
                right = jnp.sum(jnp.where(lane > jb, sums[h], 0.0), axis=1, keepdims=True)
                z, mask, _, a = _sb_block(qs[h], kj, i, jb, kb, right, tri)
                e = a * _bdot(dos[h], vj, 1, 1)
                dv_ref[h, pl.ds(r, kb), :] += _dot(a.astype(BF16), dos[h].astype(BF16), 0, 0)
                before = lefts[h] + _ones_dot(e, tri_strict)
                sg = _sigmoid(z)
                dz = jnp.where(mask, e * (1.0 - sg) - sg * before, 0.0).astype(BF16)
                dk_ref[h, pl.ds(r, kb), :] += _dot(dz, qs[h].astype(BF16), 0, 0)
                out.append((dqs[h] + _dot(dz, kj.astype(BF16)), lefts[h] + jnp.sum(e, axis=1, keepdims=True)))
            return tuple(o[0] for o in out), tuple(o[1] for o in out)

        init = (tuple(jnp.zeros((QB, HD), F32) for _ in range(HEADS)), tuple(jnp.zeros((QB, 1), F32) for _ in range(HEADS)))
        dqs, _ = lax.fori_loop(0, i // per + 1, step, init)
        for h in range(HEADS):
            dq_ref[h] = dqs[h]

    heads = pl.BlockSpec((HEADS, L, HD), lambda i: (0, 0, 0))
    blk = pl.BlockSpec((HEADS, QB, HD), lambda i: (0, i, 0))
    shp = _sds((HEADS, L, HD), F32)
    return _call(body, name="sb_bwd", grid=(L // QB,),
                 in_specs=[blk, heads, heads, blk, pl.BlockSpec((HEADS, QB, 128), lambda i: (0, i, 0))],
                 out_specs=[blk, heads, heads], out_shape=[shp, shp, shp],
                 compiler_params=_params(("arbitrary",)))(q, k, v, do, block_sums)


def _merge_fwd(za, zb, zc, zd, p, w_conv_out, w_glu, w_pool_out, w_sb_out):
    L = za.shape[0]
    tm = _row_tile(L)

    def body(za_ref, zb_ref, zc_ref, zd_ref, g0, g1, g2, g3, wc_ref, wg_ref, wp_ref, ws_ref, o_ref):
        glu = _dot(zb_ref[...], wg_ref[...])
        yb = glu[:, :D] * _sigmoid(glu[:, D:])
        m = _sigmoid(g0[...]) * _dot(za_ref[...], wc_ref[...])
        m = m + _sigmoid(g1[...]) * yb
        m = m + _sigmoid(g2[...]) * _dot(zc_ref[...], wp_ref[...])
        m = m + _sigmoid(g3[...]) * _dot(zd_ref[...], ws_ref[...])
        o_ref[...] = m.astype(BF16)

    zt = pl.BlockSpec((tm, W), lambda i: (i, 0))
    gate = lambda b: pl.BlockSpec((tm, D), lambda i: (i, 2 + b))
    wfull = lambda n: pl.BlockSpec((W, n), lambda i: (0, 0))
    return _call(body, name="merge_fwd", grid=(L // tm,),
                 in_specs=[zt, zt, zt, zt, gate(0), gate(1), gate(2), gate(3), wfull(D), wfull(2 * D), wfull(D), wfull(D)],
                 out_specs=pl.BlockSpec((tm, D), lambda i: (i, 0)), out_shape=_sds((L, D), BF16),
                 compiler_params=_params(("parallel",)))(za, zb, zc, zd, p, p, p, p, w_conv_out, w_glu, w_pool_out, w_sb_out)


def _merge_bwd(dm, za, zb, zc, zd, p, w_conv_out, w_glu, w_pool_out, w_sb_out):
    L = za.shape[0]
    tm = _row_tile(L)
    n = L // tm

    def body(dm_ref, za_ref, zb_ref, zc_ref, zd_ref, g0, g1, g2, g3, wc_ref, wg_ref, wp_ref, ws_ref,
             dza_ref, dzb_ref, dzc_ref, dzd_ref, dg_ref, dwc_ref, dwg_ref, dwp_ref, dws_ref,
             awc, awg, awp, aws):
        i = pl.program_id(0)

        @pl.when(i == 0)
        def _():
            awc[...] = jnp.zeros_like(awc)
            awg[...] = jnp.zeros_like(awg)
            awp[...] = jnp.zeros_like(awp)
            aws[...] = jnp.zeros_like(aws)

        dmv = dm_ref[...]

        def gated(g_ref, y, col):
            s = _sigmoid(g_ref[...])
            dg_ref[:, col * D:(col + 1) * D] = (dmv * y * s * (1.0 - s)).astype(BF16)
            return (dmv * s)

        def linear(z_ref, w_ref, acc, dz_ref, col, g_ref):
            zv = z_ref[...]
            dy = gated(g_ref, _dot(zv, w_ref[...]), col).astype(BF16)
            dz_ref[...] = _dot(dy, w_ref[...], 1, 1)
            acc[...] += _dot(zv, dy, 0, 0)

        linear(za_ref, wc_ref, awc, dza_ref, 0, g0)
        linear(zc_ref, wp_ref, awp, dzc_ref, 2, g2)
        linear(zd_ref, ws_ref, aws, dzd_ref, 3, g3)
        zbv = zb_ref[...]
        glu = _dot(zbv, wg_ref[...])
        ga = glu[:, :D]
        sg = _sigmoid(glu[:, D:])
        dyb = gated(g1, ga * sg, 1)
        dga = (dyb * sg).astype(BF16)
        dgg = (dyb * ga * sg * (1.0 - sg)).astype(BF16)
        dzb_ref[...] = _dot(dga, wg_ref[:, :D], 1, 1) + _dot(dgg, wg_ref[:, D:], 1, 1)
        awg[:, :D] += _dot(zbv, dga, 0, 0)
        awg[:, D:] += _dot(zbv, dgg, 0, 0)

        @pl.when(i == n - 1)
        def _():
            dwc_ref[...] = awc[...].astype(BF16)
            dwg_ref[...] = awg[...].astype(BF16)
            dwp_ref[...] = awp[...].astype(BF16)
            dws_ref[...] = aws[...].astype(BF16)

    zt = pl.BlockSpec((tm, W), lambda i: (i, 0))
    gate = lambda b: pl.BlockSpec((tm, D), lambda i: (i, 2 + b))
    wfull = lambda n_: pl.BlockSpec((W, n_), lambda i: (0, 0))
    zs = _sds((L, W), F32)
    return _call(body, name="merge_bwd", grid=(n,),
                 in_specs=[pl.BlockSpec((tm, D), lambda i: (i, 0)), zt, zt, zt, zt, gate(0), gate(1), gate(2), gate(3),
                           wfull(D), wfull(2 * D), wfull(D), wfull(D)],
                 out_specs=[zt, zt, zt, zt, pl.BlockSpec((tm, 4 * D), lambda i: (i, 0)),
                            wfull(D), wfull(2 * D), wfull(D), wfull(D)],
                 out_shape=[zs, zs, zs, zs, _sds((L, 4 * D), BF16),
                            _sds((W, D), BF16), _sds((W, 2 * D), BF16), _sds((W, D), BF16), _sds((W, D), BF16)],
                 scratch_shapes=[pltpu.VMEM((W, D), F32), pltpu.VMEM((W, 2 * D), F32), pltpu.VMEM((W, D), F32),
                                 pltpu.VMEM((W, D), F32)],
                 compiler_params=_params(("arbitrary",)))(dm, za, zb, zc, zd, p, p, p, p,
                                                          w_conv_out, w_glu, w_pool_out, w_sb_out)


def _adam_math(w, g, m, v):
    m2 = ADAM_B1 * m + (1.0 - ADAM_B1) * g
    v2 = ADAM_B2 * v + (1.0 - ADAM_B2) * (g * g)
    m_hat = m2 / (1.0 - ADAM_B1 ** ADAM_STEP)
    v_hat = v2 / (1.0 - ADAM_B2 ** ADAM_STEP)
    return -ADAM_LR * (m_hat / (jnp.sqrt(v_hat) + ADAM_EPS) + ADAM_WD * w), m2, v2


def _as_rows(a):
    return a.reshape(-1, a.shape[-1])


def _adamw(w, g, m, v):
    shape = w.shape
    w2, g2, m2, v2 = _as_rows(w), _as_rows(g), _as_rows(m), _as_rows(v)
    R, C = w2.shape
    tr = R
    for cand in (1024, 512, 256, 128, 64, 32, 16, 8):
        if R % cand == 0 and cand * C * 4 <= 2 * 1024 * 1024:
            tr = cand
            break

    def body(w_ref, g_ref, m_ref, v_ref, d_ref, m_out, v_out):
        d, mn, vn = _adam_math(w_ref[...], g_ref[...], m_ref[...], v_ref[...])
        d_ref[...] = d
        m_out[...] = mn
        v_out[...] = vn

    blk = pl.BlockSpec((tr, C), lambda i: (i, 0))
    shp = _sds((R, C), F32)
    outs = _call(body, name="adamw", grid=(R // tr,), in_specs=[blk] * 4, out_specs=[blk] * 3, out_shape=[shp] * 3,
                 compiler_params=_params(("parallel",)))(w2, g2, m2, v2)
    return tuple(o.reshape(shape) for o in outs)


def _sum_parts(parts, out_dtype, name):
    shape = parts[0].shape
    flat = [_as_rows(a) for a in parts]
    R, C = flat[0].shape
    tr = R
    for cand in (1024, 512, 256, 128, 64, 32, 16):
        if R % cand == 0 and cand * C * 4 <= 2 * 1024 * 1024:
            tr = cand
            break
    k = len(parts)

    def body(*refs):
        acc = refs[0][...].astype(F32)
        for r in refs[1:k]:
            acc = acc + r[...].astype(F32)
        refs[k][...] = acc.astype(out_dtype)

    blk = pl.BlockSpec((tr, C), lambda i: (i, 0))
    out = _call(body, name=name, grid=(R // tr,), in_specs=[blk] * k, out_specs=blk, out_shape=_sds((R, C), out_dtype),
                compiler_params=_params(("parallel",)))(*flat)
    return out.reshape(shape)


ADA_SHARD = 9 * D // N_CHIP
ADA_TN = 768


def _ada_fwd(c_pad, w_ada, b_ada_cols):
    depth = w_ada.shape[0]

    def body(c_ref, w_ref, b_ref, o_ref):
        cv = c_ref[...]
        o_ref[...] = _bdot(cv * _sigmoid(cv), w_ref[...]) + b_ref[...]

    return _call(body, name="ada_fwd", grid=(depth, ADA_SHARD // ADA_TN),
                 in_specs=[pl.BlockSpec((16, D), lambda l, j: (0, 0)),
                           pl.BlockSpec((None, D, ADA_TN), lambda l, j: (l, 0, j)),
                           pl.BlockSpec((None, 1, ADA_TN), lambda l, j: (l, 0, j))],
                 out_specs=pl.BlockSpec((None, 16, ADA_TN), lambda l, j: (l, 0, j)),
                 out_shape=_sds((depth, 16, ADA_SHARD), F32),
                 compiler_params=_params(("parallel", "parallel")))(c_pad, w_ada, b_ada_cols)


def _ada_wgrad(c_pad, d_ada):
    depth = d_ada.shape[0]

    def body(c_ref, d_ref, o_ref):
        cv = c_ref[...]
        o_ref[...] = _bdot(cv * _sigmoid(cv), d_ref[...], 0, 0)

    return _call(body, name="ada_wgrad", grid=(depth, ADA_SHARD // ADA_TN),
                 in_specs=[pl.BlockSpec((16, D), lambda l, j: (0, 0)),
                           pl.BlockSpec((None, 16, ADA_TN), lambda l, j: (l, 0, j))],
                 out_specs=pl.BlockSpec((None, D, ADA_TN), lambda l, j: (l, 0, j)),
                 out_shape=_sds((depth, D, ADA_SHARD), F32),
                 compiler_params=_params(("parallel", "parallel")))(c_pad, d_ada)


HBM_SPEC = pl.BlockSpec(memory_space=pltpu.HBM)


def _place():
    x, y, c = lax.axis_index("x"), lax.axis_index("y"), lax.axis_index("c")
    peers = [(1 - x, y), (x, 1 - y), (1 - x, 1 - y)]
    return x, y, c, peers


def _chip(px, py):
    return 2 * px + py


def _allgather8(block, name):
    m_per, n = block.shape

    def body(x_ref, out_ref, send_sems, recv_sems, local_sem):
        x, y, c, chips = _place()
        me, sibling = (x, y, c), (x, y, 1 - c)

        def rows(px, py, pc):
            return out_ref.at[pl.ds(pl.multiple_of((4 * px + 2 * py + pc) * m_per, 8), m_per), :]

        def copy(k, blk, to, src=None):
            return pltpu.make_async_remote_copy(
                src_ref=rows(*blk) if src is None else src, dst_ref=rows(*blk),
                send_sem=send_sems.at[k], recv_sem=recv_sems.at[k], device_id=to, device_id_type=MESH)

        mine = pltpu.make_async_copy(x_ref, rows(*me), local_sem)
        mine.start()
        first = [copy(0, me, sibling, src=x_ref)]
        first += [copy(1 + j, me, (*chip, c), src=x_ref) for j, chip in enumerate(chips)]
        for cp in first:
            cp.start()
        passed = [copy(4 + j, (*chip, c), sibling) for j, chip in enumerate(chips)]
        for j, chip in enumerate(chips):
            copy(1 + j, (*chip, c), me).wait_recv()
            passed[j].start()
        copy(0, sibling, me).wait_recv()
        for j, chip in enumerate(chips):
            copy(4 + j, (*chip, 1 - c), me).wait_recv()
        for cp in first + passed:
            cp.wait_send()
        mine.wait()

    return _call(body, name=name, out_shape=_sds((N_DEV * m_per, n), block.dtype),
                 in_specs=[pl.BlockSpec(memory_space=pltpu.VMEM)], out_specs=pl.BlockSpec(memory_space=pltpu.VMEM),
                 scratch_shapes=[pltpu.SemaphoreType.DMA((7,)), pltpu.SemaphoreType.DMA((7,)), pltpu.SemaphoreType.DMA],
                 compiler_params=_params())(block)


GATHERED = (("w_ff_in", -1), ("w_ff_out", -2), ("w_in", -1), ("w_conv_out", -1), ("w_glu", -1),
            ("w_pool_out", -1), ("w_sb_out", -1), ("w_out", -2))


def _lead(ref):
    return (slice(None),) * (len(ref.shape) - 2)


def _mo(v, m):
    return v if isinstance(v, int) else pl.multiple_of(v, m)


def _full_region(ref, axis, j, half, shard_shape):
    rs, cs = shard_shape[-2], shard_shape[-1]
    if axis == -1:
        r0, nr = (0, rs) if half is None else (half * (rs // 2), rs // 2)
        return ref.at[_lead(ref) + (pl.ds(_mo(r0, 16), nr), pl.ds(_mo(j * cs, 128), cs))]
    r0, nr = (j * rs, rs) if half is None else (j * rs + half * (rs // 2), rs // 2)
    return ref.at[_lead(ref) + (pl.ds(_mo(r0, 16), nr), slice(None))]


def _shard_half(ref, half):
    rs = ref.shape[-2]
    return ref.at[_lead(ref) + (pl.ds(_mo(half * (rs // 2), 16), rs // 2), slice(None))]


def _full_shape(shard_shape, axis):
    s = list(shard_shape)
    s[axis] *= N_CHIP
    return tuple(s)


class _Lay:
    def __init__(self, shard_shape, axis):
        self.axis = axis
        self.shard_shape = tuple(shard_shape)
        self.full_shape = _full_shape(shard_shape, axis)
        self.lead = int(np.prod(shard_shape[:-2]))
        self.rs, self.cs = shard_shape[-2], shard_shape[-1]
        self.hr = self.rs // 2
        self.tr = next(t for t in (256, 128, 64, 32, 16) if self.hr % t == 0 and t * self.cs * 4 <= (1 << 20))
        self.half_rows_shape = _half_rows_shape(self.full_shape)
        self.half_shard_shape = _half_rows_shape(self.shard_shape)

    def full(self, jf, hf):
        if self.axis == -1:
            return ((self.lead, 2, self.hr, N_CHIP * self.cs),
                    pl.BlockSpec((None, None, self.tr, self.cs), lambda b, j, i, s: (b, hf(j, s), i, jf(j, s))))
        return ((self.lead, N_CHIP, 2, self.hr, self.cs),
                pl.BlockSpec((None, None, None, self.tr, self.cs), lambda b, j, i, s: (b, jf(j, s), hf(j, s), i, 0)))

    def half_rows(self, jf):
        if self.axis == -1:
            return ((self.lead, self.hr, N_CHIP * self.cs),
                    pl.BlockSpec((None, self.tr, self.cs), lambda b, j, i, s: (b, i, jf(j, s))))
        return ((self.lead, N_CHIP, self.hr, self.cs),
                pl.BlockSpec((None, None, self.tr, self.cs), lambda b, j, i, s: (b, jf(j, s), i, 0)))

    def half_shard(self):
        return (self.lead, self.hr, self.cs), pl.BlockSpec((None, self.tr, self.cs), lambda b, j, i, s: (b, i, 0))

    def shard(self, hf):
        return ((self.lead, 2, self.hr, self.cs),
                pl.BlockSpec((None, None, self.tr, self.cs), lambda b, j, i, s: (b, hf(j, s), i, 0)))


def _view_sum(sel, operands, out_view, out_shape, out_dtype, grid, name):
    k = len(operands)

    def body(sel_ref, *refs):
        acc = refs[0][...].astype(F32)
        for r in refs[1:k]:
            acc = acc + r[...].astype(F32)
        refs[k][...] = acc.astype(out_dtype)

    spec = pltpu.PrefetchScalarGridSpec(num_scalar_prefetch=1, grid=grid, in_specs=[v[1] for _, v in operands],
                                        out_specs=out_view[1])
    out = _call(body, name=name, grid_spec=spec, out_shape=_sds(out_view[0], out_dtype),
                compiler_params=_params(("parallel", "parallel", "parallel")))(
                    sel, *[a.reshape(v[0]) for a, v in operands])
    return out.reshape(out_shape)


def _sel_core(j, s):
    return s[0]


def _sel_chip(j, s):
    return s[1]


def _grid_j(j, s):
    return j


def _place_shard(lay, sel, w):
    return _view_sum(sel, [(w, lay.shard(_grid_j))], lay.full(_sel_chip, _grid_j), lay.full_shape, BF16,
                     (lay.lead, 2, lay.hr // lay.tr), "place_shard")


def _gather_layer(fulls, lays):
    n = len(fulls)
    axes = [l.axis for l in lays]
    shapes = [l.shard_shape for l in lays]

    def body(*refs):
        outs = refs[n:2 * n]
        send_sems, recv_sems = refs[2 * n:]
        x, y, c, chips = _place()
        my = _chip(x, y)
        sibling = (x, y, 1 - c)
        sends = []
        for a in range(n):
            for k, chip in enumerate(chips):
                own = _full_region(outs[a], axes[a], my, c, shapes[a])
                cp = pltpu.make_async_remote_copy(
                    src_ref=own, dst_ref=own,
                    send_sem=send_sems.at[a * 3 + k], recv_sem=recv_sems.at[a * 3 + k],
                    device_id=(*chip, c), device_id_type=MESH)
                cp.start()
                sends.append(cp)
        for a in range(n):
            for k, chip in enumerate(chips):
                landed = _full_region(outs[a], axes[a], _chip(*chip), c, shapes[a])
                pltpu.make_async_remote_copy(
                    src_ref=landed, dst_ref=landed, send_sem=send_sems.at[a * 3 + k], recv_sem=recv_sems.at[a * 3 + k],
                    device_id=(*chip, c), device_id_type=MESH).wait_recv()
                cp = pltpu.make_async_remote_copy(
                    src_ref=landed, dst_ref=landed, send_sem=send_sems.at[3 * n + a * 3 + k],
                    recv_sem=recv_sems.at[3 * n + a * 3 + k], device_id=sibling, device_id_type=MESH)
                cp.start()
                sends.append(cp)
        for a in range(n):
            for k, chip in enumerate(chips):
                passed = _full_region(outs[a], axes[a], _chip(*chip), 1 - c, shapes[a])
                pltpu.make_async_remote_copy(
                    src_ref=passed, dst_ref=passed, send_sem=send_sems.at[3 * n + a * 3 + k],
                    recv_sem=recv_sems.at[3 * n + a * 3 + k], device_id=sibling, device_id_type=MESH).wait_recv()
        for cp in sends:
            cp.wait_send()

    return _call(body, name="gather_layer",
                 out_shape=[_sds(f.shape, f.dtype) for f in fulls],
                 in_specs=[HBM_SPEC] * n, out_specs=[HBM_SPEC] * n,
                 input_output_aliases={a: a for a in range(n)},
                 scratch_shapes=[pltpu.SemaphoreType.DMA((6 * n,)), pltpu.SemaphoreType.DMA((6 * n,))],
                 compiler_params=_params())(*fulls)


def _half_rows_shape(full_shape):
    s = list(full_shape)
    s[-2] //= 2
    return tuple(s)


def _reduce_sibling(grads, lays):
    n = len(grads)

    def pieces(lay, ref_full, ref_half, half):
        if lay.axis == -1:
            src = ref_full.at[_lead(ref_full) + (pl.ds(_mo(half * lay.hr, 16), lay.hr), slice(None))]
            return [(src, ref_half)]
        out = []
        for j in range(N_CHIP):
            src = _full_region(ref_full, -2, j, half, lay.shard_shape)
            dst = ref_half.at[_lead(ref_half) + (pl.ds(j * lay.hr, lay.hr), slice(None))]
            out.append((src, dst))
        return out

    n_cp = sum(1 if lay.axis == -1 else N_CHIP for lay in lays)

    def body(*refs):
        ins, got = refs[:n], refs[n:2 * n]
        send_sems, recv_sems = refs[2 * n:]
        x, y, c, _ = _place()
        sibling = (x, y, 1 - c)
        started, idx = [], 0
        for a in range(n):
            for src, dst in pieces(lays[a], ins[a], got[a], 1 - c):
                rc = pltpu.make_async_remote_copy(src_ref=src, dst_ref=dst, send_sem=send_sems.at[idx],
                                                  recv_sem=recv_sems.at[idx], device_id=sibling, device_id_type=MESH)
                rc.start()
                started.append(rc)
                idx += 1
        for rc in started:
            rc.wait_recv()
            rc.wait_send()

    return _call(body, name="reduce_sibling",
                 out_shape=[_sds(lay.half_rows_shape, BF16) for lay in lays],
                 in_specs=[HBM_SPEC] * n, out_specs=[HBM_SPEC] * n,
                 scratch_shapes=[pltpu.SemaphoreType.DMA((n_cp,)), pltpu.SemaphoreType.DMA((n_cp,))],
                 compiler_params=_params())(*grads)


def _reduce_chips(parts, lays):
    n = len(parts)

    def region(lay, ref, j):
        if lay.axis == -1:
            return ref.at[_lead(ref) + (slice(None), pl.ds(_mo(j * lay.cs, 128), lay.cs))]
        return ref.at[_lead(ref) + (pl.ds(_mo(j * lay.hr, 16), lay.hr), slice(None))]

    def body(*refs):
        ins, got = refs[:n], refs[n:4 * n]
        send_sems, recv_sems = refs[4 * n:]
        x, y, c, chips = _place()
        sends = []
        for a in range(n):
            for k, chip in enumerate(chips):
                cp = pltpu.make_async_remote_copy(
                    src_ref=region(lays[a], ins[a], _chip(*chip)), dst_ref=got[a * 3 + k],
                    send_sem=send_sems.at[a * 3 + k], recv_sem=recv_sems.at[a * 3 + k],
                    device_id=(*chip, c), device_id_type=MESH)
                cp.start()
                sends.append(cp)
        for cp in sends:
            cp.wait_recv()
            cp.wait_send()

    outs = _call(body, name="reduce_chips",
                 out_shape=[_sds(lay.half_shard_shape, BF16) for lay in lays for _ in range(3)],
                 in_specs=[HBM_SPEC] * n, out_specs=[HBM_SPEC] * (3 * n),
                 scratch_shapes=[pltpu.SemaphoreType.DMA((3 * n,)), pltpu.SemaphoreType.DMA((3 * n,))],
                 compiler_params=_params())(*parts)
    return [outs[3 * a:3 * a + 3] for a in range(n)]


def _share_halves(shards):
    n = len(shards)

    def body(*refs):
        outs = refs[n:2 * n]
        send_sems, recv_sems = refs[2 * n:]
        x, y, c, _ = _place()
        sibling = (x, y, 1 - c)
        started = []
        for a in range(n):
            mine = _shard_half(outs[a], c)
            rc = pltpu.make_async_remote_copy(src_ref=mine, dst_ref=mine, send_sem=send_sems.at[a],
                                              recv_sem=recv_sems.at[a], device_id=sibling, device_id_type=MESH)
            rc.start()
            started.append(rc)
        for rc in started:
            rc.wait_recv()
            rc.wait_send()

    return _call(body, name="share_halves", out_shape=[_sds(s.shape, F32) for s in shards],
                 in_specs=[HBM_SPEC] * n, out_specs=[HBM_SPEC] * n, input_output_aliases={a: a for a in range(n)},
                 scratch_shapes=[pltpu.SemaphoreType.DMA((n,)), pltpu.SemaphoreType.DMA((n,))],
                 compiler_params=_params())(*shards)


def _reduce_layer(grads, lays, sel):
    got = _reduce_sibling(grads, lays)
    chip_parts = [
        _view_sum(sel, [(g, lay.full(_grid_j, _sel_core)), (o, lay.half_rows(_grid_j))], lay.half_rows(_grid_j),
                  lay.half_rows_shape, BF16, (lay.lead, N_CHIP, lay.hr // lay.tr), "chip_partial")
        for g, o, lay in zip(grads, got, lays)]
    landed = _reduce_chips(chip_parts, lays)
    halves = [
        _view_sum(sel, [(t, lay.half_rows(_sel_chip))] + [(l, lay.half_shard()) for l in ls], lay.shard(_sel_core),
                  lay.shard_shape, F32, (lay.lead, 1, lay.hr // lay.tr), "shard_half_sum")
        for t, ls, lay in zip(chip_parts, landed, lays)]
    return _share_halves(halves)


def _embed(blocks):
    n, r, c = blocks.shape
    eye = jnp.eye(n, dtype=blocks.dtype)
    return (blocks[:, :, None, :] * eye[:, None, :, None]).reshape(n * r, n * c)


def _unembed(mat, n):
    r, c = mat.shape[0] // n, mat.shape[1] // n
    return jnp.transpose(jnp.diagonal(mat.reshape(n, r, n, c), axis1=0, axis2=2), (2, 0, 1))


def _to_heads(a):
    return jnp.transpose(a.reshape(a.shape[0], HEADS, HD), (1, 0, 2))


def _from_heads(a):
    return jnp.transpose(a, (1, 0, 2)).reshape(a.shape[1], W)


def _row(v):
    return v.reshape(1, -1)


def _ffn_fwd(x, ada, gp, gq, w_in, w_out, s):
    L = x.shape[0]
    h = _norm_mod(x, _row(gp[s]), _row(ada[3 * s]), _row(ada[3 * s + 1]))
    a, b, act = _ffn_in(h, w_in)
    f = _mm(act, w_out, M=L, N=D, K=FF, tm=min(L, 512), tn=512, name="ffn_out")
    x2 = _post(x, f, _row(gq[s]), _row(ada[3 * s + 2]), 0.5)
    return x2, (x, h, a, b, act, f)


def _ffn_bwd(dx, saved, ada, gp, gq, w_in, w_out, s):
    x, h, a, b, act, f = saved
    L = x.shape[0]
    df, dgate, dgq = _post_bwd(dx, f, _row(gq[s]), _row(ada[3 * s + 2]), 0.5)
    dw_out = _mm(act, df, M=FF, N=D, K=L, tm=256, tn=512, ta=True, out_dtype=BF16, name="ffn_dw_out")
    da, db = _ffn_mid_bwd(df, w_out, a, b)
    du = jnp.concatenate([da, db], axis=1)
    dw_in = _mm(h, du, M=D, N=2 * FF, K=L, tm=512, tn=512, ta=True, out_dtype=BF16, name="ffn_dw_in")
    dh = _mm(du, w_in, M=L, N=D, K=2 * FF, tm=min(L, 512), tn=512, tk=1408, tb=True, name="ffn_dh")
    dx2, dshift, dscale, dgp = _norm_mod_bwd(dh, x, _row(gp[s]), _row(ada[3 * s + 1]), dx)
    return dx2, dw_in, dw_out, (dshift, dscale, dgate), dgp, dgq


def _mixer_fwd(x, ada, gp, gq, wf, sm):
    L = x.shape[0]
    h = _norm_mod(x, _row(gp[1]), _row(ada[3]), _row(ada[4]))
    p = _mm(h, wf["w_in"], M=L, N=IN_COLS, K=D, tm=min(L, 512), tn=512, name="mixer_in")
    za = _conv_fwd(p, sm["conv_w"])
    y, zb = _ssm_fwd(p, sm["b_re"], sm["b_im"], sm["c_re"], sm["c_im"], sm["abr"], sm["abi"], sm["fr"], sm["fi"], sm["ssm_d"])
    zc = _pool_fwd(p, sm["w_pool"], sm["pool_scale"])
    q = _to_heads(p[:, 5 * W:6 * W]) * (HD ** -0.5)
    k = _to_heads(p[:, 6 * W:7 * W])
    v = _to_heads(p[:, 7 * W:8 * W])
    o_heads, block_sums = _sb_fwd(q, k, v)
    zd = _from_heads(o_heads).astype(BF16)
    merged = _merge_fwd(za, zb, zc, zd, p, wf["w_conv_out"], wf["w_glu"], wf["w_pool_out"], wf["w_sb_out"])
    m = _mm(merged, wf["w_out"], M=L, N=D, K=D, tm=min(L, 512), tn=512, name="mixer_out")
    x2 = _post(x, m, _row(gq[1]), _row(ada[5]), 1.0)
    return x2, (x, h, p, za, y, zb, zc, zd, q, k, v, block_sums, merged, m)


def _mixer_bwd(dx, saved, ada, gp, gq, wf, sm):
    x, h, p, za, y, zb, zc, zd, q, k, v, block_sums, merged, m = saved
    L = x.shape[0]
    dmf, dgate, dgq = _post_bwd(dx, m, _row(gq[1]), _row(ada[5]), 1.0)
    dw_out = _mm(merged, dmf, M=D, N=D, K=L, tm=512, tn=512, ta=True, out_dtype=BF16, name="mixer_dw_out")
    dmerged = _mm(dmf, wf["w_out"], M=L, N=D, K=D, tm=min(L, 512), tn=512, tb=True, name="mixer_dmerged")
    dza, dzb, dzc, dzd, dgates, dwc, dwg, dwp, dws = _merge_bwd(
        dmerged, za, zb, zc, zd, p, wf["w_conv_out"], wf["w_glu"], wf["w_pool_out"], wf["w_sb_out"])
    dconv, dconv_w = _conv_bwd(p, sm["conv_w"], dza)
    (du_ssm, dd, dbr, dbi, dcr, dci, gar, gai, gfr, gfi) = _ssm_bwd(
        p, y, dzb, sm["b_re"], sm["b_im"], sm["c_re"], sm["c_im"], sm["abr"], sm["abi"], sm["fr"], sm["fi"], sm["ssm_d"])
    du_pool, dwpool, dpscale = _pool_bwd(p, sm["w_pool"], sm["pool_scale"], dzc)
    dq, dk, dv = _sb_bwd(q, k, v, _to_heads(dzd), block_sums)
    dqkv = [_from_heads(t).astype(BF16) for t in (dq * (HD ** -0.5), dk, dv)]
    dp = jnp.concatenate([dconv, du_ssm, du_pool] + dqkv + [dgates], axis=1)
    dw_in = _mm(h, dp, M=D, N=IN_COLS, K=L, tm=512, tn=512, ta=True, out_dtype=BF16, name="mixer_dw_in")
    dh = _mm(dp, wf["w_in"], M=L, N=D, K=IN_COLS, tm=min(L, 512), tn=512, tk=1536, tb=True, name="mixer_dh")
    dx2, dshift, dscale, dgp = _norm_mod_bwd(dh, x, _row(gp[1]), _row(ada[4]), dx)
    wgrads = {"w_in": dw_in, "w_conv_out": dwc, "w_glu": dwg, "w_pool_out": dwp, "w_sb_out": dws, "w_out": dw_out}
    small = {"conv_w": dconv_w, "ssm_d": dd, "b_re": dbr, "b_im": dbi, "c_re": dcr, "c_im": dci,
             "abr": gar, "abi": gai, "fr": gfr, "fi": gfi, "w_pool": dwpool, "pool_scale": dpscale}
    return dx2, wgrads, small, (dshift, dscale, dgate), dgp, dgq


def _layer_fwd(x, ada, gp, gq, wf, sm):
    x, s0 = _ffn_fwd(x, ada, gp, gq, wf["w_ff_in"][0], wf["w_ff_out"][0], 0)
    x, s1 = _mixer_fwd(x, ada, gp, gq, wf, sm)
    x, s2 = _ffn_fwd(x, ada, gp, gq, wf["w_ff_in"][1], wf["w_ff_out"][1], 2)
    return x, (s0, s1, s2)


def _layer_bwd(dx, saved, ada, gp, gq, wf, sm):
    s0, s1, s2 = saved
    dx, dwi2, dwo2, dada2, dgp2, dgq2 = _ffn_bwd(dx, s2, ada, gp, gq, wf["w_ff_in"][1], wf["w_ff_out"][1], 2)
    dx, wgrads, small, dada1, dgp1, dgq1 = _mixer_bwd(dx, s1, ada, gp, gq, wf, sm)
    dx, dwi0, dwo0, dada0, dgp0, dgq0 = _ffn_bwd(dx, s0, ada, gp, gq, wf["w_ff_in"][0], wf["w_ff_out"][0], 0)
    wgrads["w_ff_in"] = jnp.stack([dwi0, dwi2])
    wgrads["w_ff_out"] = jnp.stack([dwo0, dwo2])
    small["d_ada"] = jnp.concatenate(list(dada0) + list(dada1) + list(dada2), axis=1).reshape(-1)
    small["g_pre"] = jnp.concatenate([dgp0, dgp1, dgp2], axis=0)
    small["g_post"] = jnp.concatenate([dgq0, dgq1, dgq2], axis=0)
    return dx, wgrads, small


def _pack(arrays):
    flat = jnp.concatenate([a.reshape(-1) for a in arrays])
    rows = -(-flat.shape[0] // 128)
    rows = -(-rows // 64) * 64
    return jnp.pad(flat, (0, rows * 128 - flat.shape[0])).reshape(rows, 128)


def _unpack(block, shapes):
    flat = block.reshape(-1)
    out, off = [], 0
    for s in shapes:
        n = int(np.prod(s))
        out.append(flat[off:off + n].reshape(s))
        off += n
    return out


def _pad_rows(a, mult):
    rows = -(-a.shape[0] // mult) * mult
    return jnp.concatenate([a] * (-(-rows // a.shape[0])), axis=0)[:rows]


SMALL_ORDER = ("d_ada", "g_pre", "g_post", "conv_w", "lam_re", "lam_im", "log_dt", "ssm_b_re", "ssm_b_im",
               "ssm_c_re", "ssm_c_im", "ssm_d", "w_pool", "pool_scale")
WEIGHTS = ('w_ada', 'b_ada', 'g_pre', 'g_post', 'w_ff_in', 'w_ff_out', 'w_in', 'conv_w', 'w_conv_out', 'lam_re', 'lam_im',
           'log_dt', 'ssm_b_re', 'ssm_b_im', 'ssm_c_re', 'ssm_c_im', 'ssm_d', 'w_glu', 'w_pool', 'pool_scale', 'w_pool_out',
           'w_sb_out', 'w_out')


def _step(a):
    depth = a["w_ada"].shape[0]
    x = a["x"][0]
    target = a["loss_target"][0]
    L = x.shape[0]
    ix, iy, ic = lax.axis_index("x"), lax.axis_index("y"), lax.axis_index("c")
    chip = 2 * ix + iy
    me = 4 * ix + 2 * iy + ic
    sel = jnp.stack([ic, chip]).astype(jnp.int32)
    lays = [_Lay(a[name].shape[1:], ax) for name, ax in GATHERED]

    first_shapes = [(D,), (depth, 3, W), (depth, 3, W), (depth, 3, W // N_CHIP)]
    gathered = _allgather8(_pack([a["c"], a["g_pre"], a["g_post"], a["conv_w"]]), "gather_small_inputs")
    per_dev = [_unpack(blk, first_shapes) for blk in gathered.reshape(N_DEV, -1, 128)]
    c_all = jnp.stack([d[0] for d in per_dev])
    c_pad = jnp.concatenate([c_all, jnp.zeros_like(c_all)], axis=0)
    g_pre = jnp.concatenate([per_dev[2 * j][1] for j in range(N_CHIP)], axis=-1)
    g_post = jnp.concatenate([per_dev[2 * j][2] for j in range(N_CHIP)], axis=-1)
    conv_w = jnp.concatenate([per_dev[2 * j][3] for j in range(N_CHIP)], axis=-1)

    b_cols = lax.dynamic_slice(a["b_ada"], (0, chip * ADA_SHARD), (depth, ADA_SHARD)).reshape(depth, 1, ADA_SHARD)
    ada_part = _ada_fwd(c_pad, a["w_ada"], b_cols)
    ada_all = _allgather8(ada_part.reshape(depth * 16, ADA_SHARD), "gather_ada").reshape(N_DEV, depth, 16, ADA_SHARD)
    ada_rows = lax.dynamic_slice(ada_all, (0, 0, me, 0), (N_DEV, depth, 1, ADA_SHARD))[:, :, 0]
    ada = jnp.concatenate([ada_rows[2 * j] for j in range(N_CHIP)], axis=-1).reshape(depth, 9, D)

    lam_re = _pad_rows(a["lam_re"].reshape(depth, NST), 8)
    lam_im = _pad_rows(a["lam_im"].reshape(depth, NST), 8)
    log_dt_x = _pad_rows(jnp.repeat(a["log_dt"], GP, axis=1), 8)
    abr, abi, fr, fi = _ssm_prep(lam_re, lam_im, log_dt_x)

    def small_of(l):
        return {"conv_w": conv_w[l], "ssm_d": _row(a["ssm_d"][l]), "pool_scale": _row(a["pool_scale"][l]),
                "b_re": _embed(jnp.transpose(a["ssm_b_re"][l], (0, 2, 1))), "b_im": _embed(jnp.transpose(a["ssm_b_im"][l], (0, 2, 1))),
                "c_re": _embed(jnp.transpose(a["ssm_c_re"][l], (0, 2, 1))), "c_im": _embed(jnp.transpose(a["ssm_c_im"][l], (0, 2, 1))),
                "w_pool": _embed(a["w_pool"][l]),
                "abr": abr[l:l + 1], "abi": abi[l:l + 1], "fr": fr[l:l + 1], "fi": fi[l:l + 1]}

    full_w, smalls, saved = [], [], []
    for l in range(depth):
        full = _gather_layer([_place_shard(lay, sel, a[name][l]) for (name, _), lay in zip(GATHERED, lays)], lays)
        full_w.append({name: w for (name, _), w in zip(GATHERED, full)})
        smalls.append(small_of(l))
        x, s = _layer_fwd(x, ada[l], g_pre[l], g_post[l], full_w[l], smalls[l])
        saved.append(s)
    dx, loss_part = _loss_head(x, target)
    loss = lax.psum(loss_part[0, 0], ("x", "y", "c"))

    shard_grads = {name: [None] * depth for name, _ in GATHERED}
    small_grads = [None] * depth
    for l in reversed(range(depth)):
        dx, wgrads, small = _layer_bwd(dx, saved[l], ada[l], g_pre[l], g_post[l], full_w[l], smalls[l])
        for (name, _), g in zip(GATHERED, _reduce_layer([wgrads[name] for name, _ in GATHERED], lays, sel)):
            shard_grads[name][l] = g
        small_grads[l] = small

    stack = lambda key: _pad_rows(jnp.concatenate([small_grads[l][key] for l in range(depth)], axis=0), 8)
    gs = np.zeros((NST, 128), np.float32)
    gs[np.arange(NST), np.arange(NST) // GP] = 1.0
    dlr, dli, dldt = _ssm_prep_bwd(lam_re, lam_im, log_dt_x, stack("abr"), stack("abi"), stack("fr"), stack("fi"), jnp.asarray(gs))
    part = {
        "d_ada": jnp.stack([small_grads[l]["d_ada"] for l in range(depth)]),
        "g_pre": jnp.stack([small_grads[l]["g_pre"] for l in range(depth)]),
        "g_post": jnp.stack([small_grads[l]["g_post"] for l in range(depth)]),
        "conv_w": jnp.stack([small_grads[l]["conv_w"] for l in range(depth)]),
        "lam_re": dlr[:depth].reshape(depth, G, GP), "lam_im": dli[:depth].reshape(depth, G, GP), "log_dt": dldt[:depth, :G],
        "ssm_b_re": jnp.stack([jnp.transpose(_unembed(small_grads[l]["b_re"], G), (0, 2, 1)) for l in range(depth)]),
        "ssm_b_im": jnp.stack([jnp.transpose(_unembed(small_grads[l]["b_im"], G), (0, 2, 1)) for l in range(depth)]),
        "ssm_c_re": jnp.stack([jnp.transpose(_unembed(small_grads[l]["c_re"], G), (0, 2, 1)) for l in range(depth)]),
        "ssm_c_im": jnp.stack([jnp.transpose(_unembed(small_grads[l]["c_im"], G), (0, 2, 1)) for l in range(depth)]),
        "ssm_d": jnp.stack([small_grads[l]["ssm_d"][0] for l in range(depth)]),
        "w_pool": jnp.stack([_unembed(small_grads[l]["w_pool"], len(POOL_WINDOWS)) for l in range(depth)]),
        "pool_scale": jnp.stack([small_grads[l]["pool_scale"][0] for l in range(depth)]),
    }
    small_shapes = [part[k].shape for k in SMALL_ORDER]
    blocks = _allgather8(_pack([part[k] for k in SMALL_ORDER]), "gather_small_grads").reshape(N_DEV, -1, 128)
    total = dict(zip(SMALL_ORDER, _unpack(_sum_parts([blocks[i] for i in range(N_DEV)], F32, "small_grad_sum"), small_shapes)))

    grads = {name: jnp.stack(shard_grads[name]) for name, _ in GATHERED}
    d_ada_all = jnp.stack([_unpack(blocks[i], small_shapes[:1])[0] for i in range(N_DEV)])
    d_cols = lax.dynamic_slice(d_ada_all, (0, 0, chip * ADA_SHARD), (N_DEV, depth, ADA_SHARD))
    d_cols = jnp.transpose(d_cols, (1, 0, 2))
    grads["w_ada"] = _ada_wgrad(c_pad, jnp.concatenate([d_cols, jnp.zeros_like(d_cols)], axis=1))
    grads["b_ada"] = total["d_ada"]
    grads["g_pre"] = lax.dynamic_slice(total["g_pre"], (0, 0, chip * W), (depth, 3, W))
    grads["g_post"] = lax.dynamic_slice(total["g_post"], (0, 0, chip * W), (depth, 3, W))
    grads["conv_w"] = lax.dynamic_slice(total["conv_w"], (0, 0, chip * (W // N_CHIP)), (depth, 3, W // N_CHIP))
    for k in SMALL_ORDER[4:]:
        grads[k] = total[k]

    out = {"loss": loss, "grad_x": dx[None]}
    for name in WEIGHTS:
        out["grad_" + name] = grads[name]
        out["delta_" + name], out["new_m_" + name], out["new_v_" + name] = _adamw(a[name], grads[name], a["m_" + name], a["v_" + name])
    return out


def kernel(x, c, w_ada, b_ada, g_pre, g_post, w_ff_in, w_ff_out, w_in, conv_w, w_conv_out, lam_re, lam_im, log_dt, ssm_b_re, ssm_b_im, ssm_c_re, ssm_c_im, ssm_d, w_glu, w_pool, pool_scale, w_pool_out, w_sb_out, w_out, loss_target, m_w_ada, m_b_ada, m_g_pre, m_g_post, m_w_ff_in, m_w_ff_out, m_w_in, m_conv_w, m_w_conv_out, m_lam_re, m_lam_im, m_log_dt, m_ssm_b_re, m_ssm_b_im, m_ssm_c_re, m_ssm_c_im, m_ssm_d, m_w_glu, m_w_pool, m_pool_scale, m_w_pool_out, m_w_sb_out, m_w_out, v_w_ada, v_b_ada, v_g_pre, v_g_post, v_w_ff_in, v_w_ff_out, v_w_in, v_conv_w, v_w_conv_out, v_lam_re, v_lam_im, v_log_dt, v_ssm_b_re, v_ssm_b_im, v_ssm_c_re, v_ssm_c_im, v_ssm_d, v_w_glu, v_w_pool, v_pool_scale, v_w_pool_out, v_w_sb_out, v_w_out):
    out = _step(dict(locals()))
    names = ["loss", "grad_x"] + [p + n for p in ("grad_", "delta_", "new_m_", "new_v_") for n in WEIGHTS]
    return tuple(out[n] for n in names)
```

```python
import functools
import math

import jax
import jax.numpy as jnp
import numpy as np
from jax import lax
from jax.experimental import pallas as pl
from jax.experimental.pallas import tpu as pltpu

F32 = jnp.float32
BF16 = jnp.bfloat16
MESH = pl.DeviceIdType.MESH

D = 1024
W = 256
FF = 2816
IN_COLS = 6144
G = 16
GH = 16
GP = 64
NST = G * GP
QB = 128
HEADS = 4
HD = 64
EPS = 1e-6
LAMBDA_RE_MAX = -1e-4
POOL_WINDOWS = (2, 4, 8, 16)
N_CHIP = 4
N_DEV = 8
VMEM_LIMIT = 56 * 1024 * 1024
HIGH = lax.Precision.HIGHEST

ADAM_LR, ADAM_B1, ADAM_B2, ADAM_EPS, ADAM_WD, ADAM_STEP = 0.001, 0.9, 0.999, 1e-08, 0.01, 10


def _call(body, **kw):
    return pl.pallas_call(body, **kw)


def _params(dims=None, **kw):
    return pltpu.CompilerParams(dimension_semantics=dims, vmem_limit_bytes=VMEM_LIMIT, **kw)


def _sds(shape, dtype):
    return jax.ShapeDtypeStruct(shape, dtype)


def _dot(a, b, ca=1, cb=0, precision=None):
    return lax.dot_general(a, b, (((ca,), (cb,)), ((), ())), preferred_element_type=F32, precision=precision)


def _bdot(a, b, ca=1, cb=0):
    return _dot(a.astype(BF16), b.astype(BF16), ca, cb)


def _sigmoid(x):
    return 1.0 / (1.0 + jnp.exp(-x))


def _mm(a, b, *, M, N, K, tm, tn, tk=None, ta=False, tb=False, out_dtype=F32, a_off=(0, 0), b_off=(0, 0), name):
    tk = K if tk is None else tk
    nk = K // tk
    assert M % tm == 0 and N % tn == 0 and K % tk == 0

    def body(a_ref, b_ref, o_ref, *acc):
        part = _bdot(a_ref[...], b_ref[...], 0 if ta else 1, 1 if tb else 0)
        if nk == 1:
            o_ref[...] = part.astype(out_dtype)
            return
        acc_ref = acc[0]
        k = pl.program_id(2)

        @pl.when(k == 0)
        def _():
            acc_ref[...] = part

        @pl.when(k > 0)
        def _():
            acc_ref[...] += part

        @pl.when(k == nk - 1)
        def _():
            o_ref[...] = acc_ref[...].astype(out_dtype)

    if ta:
        a_spec = pl.BlockSpec((tk, tm), lambda i, j, k: (k + a_off[0], i + a_off[1]))
    else:
        a_spec = pl.BlockSpec((tm, tk), lambda i, j, k: (i + a_off[0], k + a_off[1]))
    if tb:
        b_spec = pl.BlockSpec((tn, tk), lambda i, j, k: (j + b_off[0], k + b_off[1]))
    else:
        b_spec = pl.BlockSpec((tk, tn), lambda i, j, k: (k + b_off[0], j + b_off[1]))
    return _call(
        body, name=name, grid=(M // tm, N // tn, nk),
        in_specs=[a_spec, b_spec],
        out_specs=pl.BlockSpec((tm, tn), lambda i, j, k: (i, j)),
        out_shape=_sds((M, N), out_dtype),
        scratch_shapes=[] if nk == 1 else [pltpu.VMEM((tm, tn), F32)],
        compiler_params=_params(("parallel", "parallel", "arbitrary")),
    )(a, b)


def _row_tile(L):
    return min(L, 256)


def _norm_mod(x, g, shift, scale):
    L = x.shape[0]
    tr = _row_tile(L)

    def body(x_ref, g_ref, sh_ref, sc_ref, h_ref):
        xv = x_ref[...]
        r = lax.rsqrt(jnp.mean(xv * xv, axis=-1, keepdims=True) + EPS)
        h_ref[...] = (xv * r * g_ref[...] * (1.0 + sc_ref[...]) + sh_ref[...]).astype(BF16)

    row = pl.BlockSpec((tr, D), lambda i: (i, 0))
    vec = pl.BlockSpec((1, D), lambda i: (0, 0))
    return _call(body, name="norm_mod", grid=(L // tr,), in_specs=[row, vec, vec, vec], out_specs=row,
                 out_shape=_sds((L, D), BF16), compiler_params=_params(("parallel",)))(x, g, shift, scale)


def _norm_mod_bwd(dh, x, g, scale, dx_res):
    L = x.shape[0]
    tr = _row_tile(L)

    def body(dh_ref, x_ref, g_ref, sc_ref, dxr_ref, dx_ref, dsh_ref, dsc_ref, dg_ref):
        i = pl.program_id(0)
        xv = x_ref[...]
        dhv = dh_ref[...]
        r = lax.rsqrt(jnp.mean(xv * xv, axis=-1, keepdims=True) + EPS)
        y = xv * r
        n = y * g_ref[...]
        dn = dhv * (1.0 + sc_ref[...])
        dy = dn * g_ref[...]
        dx_ref[...] = dxr_ref[...] + r * (dy - y * jnp.mean(dy * y, axis=-1, keepdims=True))

        @pl.when(i == 0)
        def _():
            dsh_ref[...] = jnp.zeros_like(dsh_ref)
            dsc_ref[...] = jnp.zeros_like(dsc_ref)
            dg_ref[...] = jnp.zeros_like(dg_ref)

        dsh_ref[...] += jnp.sum(dhv, axis=0, keepdims=True)
        dsc_ref[...] += jnp.sum(dhv * n, axis=0, keepdims=True)
        dg_ref[...] += jnp.sum(dn * y, axis=0, keepdims=True)

    row = pl.BlockSpec((tr, D), lambda i: (i, 0))
    vec = pl.BlockSpec((1, D), lambda i: (0, 0))
    return _call(body, name="norm_mod_bwd", grid=(L // tr,), in_specs=[row, row, vec, vec, row],
                 out_specs=[row, vec, vec, vec],
                 out_shape=[_sds((L, D), F32), _sds((1, D), F32), _sds((1, D), F32), _sds((1, D), F32)],
                 compiler_params=_params(("arbitrary",)))(dh, x, g, scale, dx_res)


def _post(x, f, g, gate, res_weight):
    L = x.shape[0]
    tr = _row_tile(L)

    def body(x_ref, f_ref, g_ref, gt_ref, o_ref):
        fv = f_ref[...]
        r = lax.rsqrt(jnp.mean(fv * fv, axis=-1, keepdims=True) + EPS)
        o_ref[...] = x_ref[...] + (res_weight * (1.0 + gt_ref[...])) * (fv * r * g_ref[...])

    row = pl.BlockSpec((tr, D), lambda i: (i, 0))
    vec = pl.BlockSpec((1, D), lambda i: (0, 0))
    return _call(body, name="post", grid=(L // tr,), in_specs=[row, row, vec, vec], out_specs=row,
                 out_shape=_sds((L, D), F32), compiler_params=_params(("parallel",)))(x, f, g, gate)


def _post_bwd(dx, f, g, gate, res_weight):
    L = dx.shape[0]
    tr = _row_tile(L)

    def body(dx_ref, f_ref, g_ref, gt_ref, df_ref, dgt_ref, dg_ref):
        i = pl.program_id(0)
        fv = f_ref[...]
        dxv = dx_ref[...]
        r = lax.rsqrt(jnp.mean(fv * fv, axis=-1, keepdims=True) + EPS)
        y = fv * r
        dn = dxv * (res_weight * (1.0 + gt_ref[...]))
        dy = dn * g_ref[...]
        df_ref[...] = (r * (dy - y * jnp.mean(dy * y, axis=-1, keepdims=True))).astype(BF16)

        @pl.when(i == 0)
        def _():
            dgt_ref[...] = jnp.zeros_like(dgt_ref)
            dg_ref[...] = jnp.zeros_like(dg_ref)

        dgt_ref[...] += res_weight * jnp.sum(dxv * (y * g_ref[...]), axis=0, keepdims=True)
        dg_ref[...] += jnp.sum(dn * y, axis=0, keepdims=True)

    row = pl.BlockSpec((tr, D), lambda i: (i, 0))
    vec = pl.BlockSpec((1, D), lambda i: (0, 0))
    return _call(body, name="post_bwd", grid=(L // tr,), in_specs=[row, row, vec, vec],
                 out_specs=[row, vec, vec],
                 out_shape=[_sds((L, D), BF16), _sds((1, D), F32), _sds((1, D), F32)],
                 compiler_params=_params(("arbitrary",)))(dx, f, g, gate)


def _loss_head(x, target):
    L = x.shape[0]
    tr = _row_tile(L)

    def body(x_ref, t_ref, dx_ref, loss_ref):
        i = pl.program_id(0)
        err = x_ref[...] - t_ref[...]
        dx_ref[...] = err * (1.0 / D)

        @pl.when(i == 0)
        def _():
            loss_ref[...] = jnp.zeros_like(loss_ref)

        loss_ref[...] += 0.5 * jnp.sum(jnp.mean(err * err, axis=-1, keepdims=True), axis=0, keepdims=True)

    row = pl.BlockSpec((tr, D), lambda i: (i, 0))
    return _call(body, name="loss_head", grid=(L // tr,), in_specs=[row, row],
                 out_specs=[row, pl.BlockSpec((1, 1), lambda i: (0, 0))],
                 out_shape=[_sds((L, D), F32), _sds((1, 1), F32)],
                 compiler_params=_params(("arbitrary",)))(x, target)


def _ffn_in(h, w_in):
    L = h.shape[0]
    tm, tn = min(L, 512), 256
    nf = FF // tn

    def body(h_ref, wa_ref, wb_ref, a_ref, b_ref, act_ref):
        hv = h_ref[...]
        a = _dot(hv, wa_ref[...])
        b = _dot(hv, wb_ref[...])
        a_ref[...] = a
        b_ref[...] = b
        act_ref[...] = (a * _sigmoid(a) * b).astype(BF16)

    tile = pl.BlockSpec((tm, tn), lambda i, j: (i, j))
    return _call(body, name="ffn_in", grid=(L // tm, nf),
                 in_specs=[pl.BlockSpec((tm, D), lambda i, j: (i, 0)),
                           pl.BlockSpec((D, tn), lambda i, j: (0, j)),
                           pl.BlockSpec((D, tn), lambda i, j: (0, j + nf))],
                 out_specs=[tile, tile, tile],
                 out_shape=[_sds((L, FF), F32), _sds((L, FF), F32), _sds((L, FF), BF16)],
                 compiler_params=_params(("parallel", "parallel")))(h, w_in, w_in)


def _ffn_mid_bwd(df, w_out, a, b):
    L = df.shape[0]
    tm, tn = min(L, 512), 256

    def body(df_ref, w_ref, a_ref, b_ref, da_ref, db_ref):
        dact = _dot(df_ref[...], w_ref[...], 1, 1)
        av = a_ref[...]
        sg = _sigmoid(av)
        da_ref[...] = (dact * b_ref[...] * (sg * (1.0 + av * (1.0 - sg)))).astype(BF16)
        db_ref[...] = (dact * (av * sg)).astype(BF16)

    tile = pl.BlockSpec((tm, tn), lambda i, j: (i, j))
    return _call(body, name="ffn_mid_bwd", grid=(L // tm, FF // tn),
                 in_specs=[pl.BlockSpec((tm, D), lambda i, j: (i, 0)),
                           pl.BlockSpec((tn, D), lambda i, j: (j, 0)), tile, tile],
                 out_specs=[tile, tile],
                 out_shape=[_sds((L, FF), BF16), _sds((L, FF), BF16)],
                 compiler_params=_params(("parallel", "parallel")))(df, w_out, a, b)


def _rows_before(ref, i, tr, halo):
    start = pl.multiple_of(jnp.maximum(i * tr - halo, 0), 8)
    return jnp.where(i > 0, ref[pl.ds(start, halo), :], 0.0)


def _rows_after(ref, i, n, tr, halo):
    start = pl.multiple_of(jnp.minimum((i + 1) * tr, (n - 1) * tr), 8)
    return jnp.where(i < n - 1, ref[pl.ds(start, halo), :], 0.0)


def _conv_fwd(p, conv_w):
    L = p.shape[0]
    tr = _row_tile(L)
    n = L // tr

    def body(bg_ref, cg_ref, v_ref, w_ref, za_ref, u_scr):
        i = pl.program_id(0)

        @pl.when(i == 0)
        def _():
            u_scr[...] = cg_ref[...] * v_ref[...]

        r0 = pl.multiple_of(i * tr, 8)
        ext = jnp.concatenate([_rows_before(u_scr, i, tr, 8), u_scr[pl.ds(r0, tr), :]], axis=0)
        w = w_ref[...]
        y = (w[0:1] * pltpu.roll(ext, 2, axis=0) + w[1:2] * pltpu.roll(ext, 1, axis=0) + w[2:3] * ext)[8:, :]
        za_ref[...] = (bg_ref[pl.ds(r0, tr), :] * y).astype(BF16)

    col = lambda c: pl.BlockSpec((L, W), lambda i: (0, c))
    return _call(body, name="conv_fwd", grid=(n,),
                 in_specs=[col(0), col(1), col(2), pl.BlockSpec((3, W), lambda i: (0, 0))],
                 out_specs=pl.BlockSpec((tr, W), lambda i: (i, 0)),
                 out_shape=_sds((L, W), BF16),
                 scratch_shapes=[pltpu.VMEM((L, W), F32)],
                 compiler_params=_params(("arbitrary",)))(p, p, p, conv_w)


def _conv_bwd(p, conv_w, dza):
    L = p.shape[0]
    tr = _row_tile(L)
    n = L // tr

    def body(bg_ref, cg_ref, v_ref, w_ref, dza_ref, dp_ref, dw_ref, u_scr, dy_scr):
        i = pl.program_id(0)

        @pl.when(i == 0)
        def _():
            u_scr[...] = cg_ref[...] * v_ref[...]
            dy_scr[...] = dza_ref[...] * bg_ref[...]
            dw_ref[...] = jnp.zeros_like(dw_ref)

        r0 = pl.multiple_of(i * tr, 8)
        w = w_ref[...]
        ext = jnp.concatenate([_rows_before(u_scr, i, tr, 8), u_scr[pl.ds(r0, tr), :]], axis=0)
        u2 = pltpu.roll(ext, 2, axis=0)[8:, :]
        u1 = pltpu.roll(ext, 1, axis=0)[8:, :]
        u0 = ext[8:, :]
        y = w[0:1] * u2 + w[1:2] * u1 + w[2:3] * u0
        dy = dy_scr[pl.ds(r0, tr), :]
        dext = jnp.concatenate([dy, _rows_after(dy_scr, i, n, tr, 8)], axis=0)
        m = tr + 8
        du = (w[2:3] * dext + w[1:2] * pltpu.roll(dext, m - 1, axis=0) + w[0:1] * pltpu.roll(dext, m - 2, axis=0))[:tr, :]
        dp_ref[:, 0:W] = (dza_ref[pl.ds(r0, tr), :] * y).astype(BF16)
        dp_ref[:, W:2 * W] = (du * v_ref[pl.ds(r0, tr), :]).astype(BF16)
        dp_ref[:, 2 * W:3 * W] = (du * cg_ref[pl.ds(r0, tr), :]).astype(BF16)
        dw_ref[...] += jnp.concatenate([jnp.sum(dy * u2, axis=0, keepdims=True),
                                        jnp.sum(dy * u1, axis=0, keepdims=True),
                                        jnp.sum(dy * u0, axis=0, keepdims=True)], axis=0)

    col = lambda c: pl.BlockSpec((L, W), lambda i: (0, c))
    return _call(body, name="conv_bwd", grid=(n,),
                 in_specs=[col(0), col(1), col(2), pl.BlockSpec((3, W), lambda i: (0, 0)),
                           pl.BlockSpec((L, W), lambda i: (0, 0))],
                 out_specs=[pl.BlockSpec((tr, 3 * W), lambda i: (i, 0)), pl.BlockSpec((3, W), lambda i: (0, 0))],
                 out_shape=[_sds((L, 3 * W), BF16), _sds((3, W), F32)],
                 scratch_shapes=[pltpu.VMEM((L, W), F32), pltpu.VMEM((L, W), F32)],
                 compiler_params=_params(("arbitrary",)))(p, p, p, conv_w, dza)


def _pool_windows(lane):
    wins = jnp.zeros(lane.shape, jnp.int32)
    for gi, w in enumerate(POOL_WINDOWS):
        wins = jnp.where(lane // (W // len(POOL_WINDOWS)) == gi, w, wins)
    return wins


def _pooled_block(u_ref, i, tr):
    r0 = pl.multiple_of(i * tr, 8)
    cur = u_ref[pl.ds(r0, tr), :]
    ext = jnp.concatenate([_rows_before(u_ref, i, tr, 16), cur], axis=0)
    s2 = ext + pltpu.roll(ext, 1, axis=0)
    s4 = s2 + pltpu.roll(s2, 2, axis=0)
    s8 = s4 + pltpu.roll(s4, 4, axis=0)
    s16 = s8 + pltpu.roll(s8, 8, axis=0)
    lane = lax.broadcasted_iota(jnp.int32, (tr, W), 1)
    wins = _pool_windows(lane)
    win_sum = jnp.where(wins == 2, s2[16:], jnp.where(wins == 4, s4[16:], jnp.where(wins == 8, s8[16:], s16[16:])))
    t = lax.broadcasted_iota(jnp.int32, (tr, W), 0) + i * tr
    cnt = jnp.minimum(t + 1, wins).astype(F32)
    return win_sum / cnt - cur, cnt


def _pool_fwd(p, w_pool_bd, pool_scale):
    L = p.shape[0]
    tr = _row_tile(L)

    def body(u_ref, w_ref, sc_ref, zc_ref):
        pooled, _ = _pooled_block(u_ref, pl.program_id(0), tr)
        zc_ref[...] = (_bdot(pooled, w_ref[...]) * sc_ref[...]).astype(BF16)

    return _call(body, name="pool_fwd", grid=(L // tr,),
                 in_specs=[pl.BlockSpec((L, W), lambda i: (0, 4)), pl.BlockSpec((W, W), lambda i: (0, 0)),
                           pl.BlockSpec((1, W), lambda i: (0, 0))],
                 out_specs=pl.BlockSpec((tr, W), lambda i: (i, 0)), out_shape=_sds((L, W), BF16),
                 compiler_params=_params(("arbitrary",)))(p, w_pool_bd, pool_scale)


def _pool_bwd(p, w_pool_bd, pool_scale, dzc):
    L = p.shape[0]
    tr = _row_tile(L)
    n = L // tr

    def body(u_ref, w_ref, sc_ref, dzc_ref, du_ref, dw_ref, dsc_ref, g_scr):
        i = pl.program_id(0)

        @pl.when(i == 0)
        def _():
            dw_ref[...] = jnp.zeros_like(dw_ref)
            dsc_ref[...] = jnp.zeros_like(dsc_ref)

            def rows(k, carry):
                r = pl.multiple_of(k * tr, 8)
                dmix = (dzc_ref[pl.ds(r, tr), :] * sc_ref[...]).astype(BF16)
                dpool = _dot(dmix, w_ref[...].astype(BF16), 1, 1)
                lane = lax.broadcasted_iota(jnp.int32, (tr, W), 1)
                t = lax.broadcasted_iota(jnp.int32, (tr, W), 0) + k * tr
                cnt = jnp.minimum(t + 1, _pool_windows(lane)).astype(F32)
                g_scr[pl.ds(r, tr), :] = dpool / cnt
                return carry

            lax.fori_loop(0, n, rows, 0)

        r0 = pl.multiple_of(i * tr, 8)
        pooled, cnt = _pooled_block(u_ref, i, tr)
        dzc = dzc_ref[pl.ds(r0, tr), :]
        mixed = _bdot(pooled, w_ref[...])
        dsc_ref[...] += jnp.sum(dzc * mixed, axis=0, keepdims=True)
        dmix = (dzc * sc_ref[...]).astype(BF16)
        dw_ref[...] += _dot(pooled.astype(BF16), dmix, 0, 0)
        gcur = g_scr[pl.ds(r0, tr), :]
        ext = jnp.concatenate([gcur, _rows_after(g_scr, i, n, tr, 16)], axis=0)
        m = tr + 16
        s2 = ext + pltpu.roll(ext, m - 1, axis=0)
        s4 = s2 + pltpu.roll(s2, m - 2, axis=0)
        s8 = s4 + pltpu.roll(s4, m - 4, axis=0)
        s16 = s8 + pltpu.roll(s8, m - 8, axis=0)
        lane = lax.broadcasted_iota(jnp.int32, (tr, W), 1)
        wins = _pool_windows(lane)
        ahead = jnp.where(wins == 2, s2[:tr], jnp.where(wins == 4, s4[:tr], jnp.where(wins == 8, s8[:tr], s16[:tr])))
        du_ref[...] = (ahead - gcur * cnt).astype(BF16)

    return _call(body, name="pool_bwd", grid=(n,),
                 in_specs=[pl.BlockSpec((L, W), lambda i: (0, 4)), pl.BlockSpec((W, W), lambda i: (0, 0)),
                           pl.BlockSpec((1, W), lambda i: (0, 0)), pl.BlockSpec((L, W), lambda i: (0, 0))],
                 out_specs=[pl.BlockSpec((tr, W), lambda i: (i, 0)), pl.BlockSpec((W, W), lambda i: (0, 0)),
                            pl.BlockSpec((1, W), lambda i: (0, 0))],
                 out_shape=[_sds((L, W), BF16), _sds((W, W), F32), _sds((1, W), F32)],
                 scratch_shapes=[pltpu.VMEM((L, W), F32)],
                 compiler_params=_params(("arbitrary",)))(p, w_pool_bd, pool_scale, dzc)


SSM_SLAB = 512


def _ssm_prep(lam_re, lam_im, log_dt_x):
    def body(lr_ref, li_ref, ldt_ref, abr_ref, abi_ref, fr_ref, fi_ref):
        lr = jnp.minimum(lr_ref[...], LAMBDA_RE_MAX)
        li = li_ref[...]
        dt = jnp.exp(ldt_ref[...])
        mag = jnp.exp(lr * dt)
        abr = mag * jnp.cos(li * dt)
        abi = mag * jnp.sin(li * dt)
        den = lr * lr + li * li
        nr = abr - 1.0
        abr_ref[...] = abr
        abi_ref[...] = abi
        fr_ref[...] = (nr * lr + abi * li) / den
        fi_ref[...] = (abi * lr - nr * li) / den

    shp = _sds(lam_re.shape, F32)
    return _call(body, name="ssm_prep", out_shape=[shp, shp, shp, shp], compiler_params=_params())(lam_re, lam_im, log_dt_x)


def _ssm_prep_bwd(lam_re, lam_im, log_dt_x, g_abr, g_abi, g_fr, g_fi, group_sum):
    def body(lr_ref, li_ref, ldt_ref, gar_ref, gai_ref, gfr_ref, gfi_ref, gs_ref, dlr_ref, dli_ref, dldt_ref):
        lam = lr_ref[...]
        lr = jnp.minimum(lam, LAMBDA_RE_MAX)
        li = li_ref[...]
        dt = jnp.exp(ldt_ref[...])
        mag = jnp.exp(lr * dt)
        abr = mag * jnp.cos(li * dt)
        abi = mag * jnp.sin(li * dt)
        den = lr * lr + li * li
        nr = abr - 1.0
        fr = (nr * lr + abi * li) / den
        fi = (abi * lr - nr * li) / den
        d_nre = gfr_ref[...] / den
        d_nim = gfi_ref[...] / den
        d_den = -(gfr_ref[...] * fr + gfi_ref[...] * fi) / den
        d_abr = gar_ref[...] + d_nre * lr - d_nim * li
        d_abi = gai_ref[...] + d_nre * li + d_nim * lr
        d_lr = d_nre * nr + d_nim * abi + d_den * 2.0 * lr
        d_li = d_nre * abi - d_nim * nr + d_den * 2.0 * li
        d_mag = d_abr * jnp.cos(li * dt) + d_abi * jnp.sin(li * dt)
        d_th = -d_abr * abi + d_abi * abr
        d_lr = d_lr + d_mag * mag * dt
        d_li = d_li + d_th * dt
        d_dt = d_mag * mag * lr + d_th * li
        passes = jnp.where(lam < LAMBDA_RE_MAX, 1.0, jnp.where(lam == LAMBDA_RE_MAX, 0.5, 0.0))
        dlr_ref[...] = d_lr * passes
        dli_ref[...] = d_li
        dldt_ref[...] = _dot(d_dt * dt, gs_ref[...], precision=HIGH)

    shp = _sds(lam_re.shape, F32)
    return _call(body, name="ssm_prep_bwd", out_shape=[shp, shp, _sds((lam_re.shape[0], 128), F32)],
                 compiler_params=_params())(lam_re, lam_im, log_dt_x, g_abr, g_abi, g_fr, g_fi, group_sum)


def _cmul(ar, ai, br, bi):
    return ar * br - ai * bi, ar * bi + ai * br


def _powers(ar, ai):
    out = [(ar, ai)]
    for _ in range(7):
        out.append(_cmul(out[-1][0], out[-1][1], ar, ai))
    return out


def _scan_rows(s_re, s_im, ar, ai, L, reverse=False, visit=None, visit_init=None):
    n = s_re.shape[1]
    pw = _powers(ar, ai)
    row = lax.broadcasted_iota(jnp.int32, (8, n), 0)
    dist = (8 - row) if reverse else (row + 1)
    pr = jnp.zeros((8, n), F32)
    pi = jnp.zeros((8, n), F32)
    for k in range(8):
        pr = jnp.where(dist == k + 1, pw[k][0], pr)
        pi = jnp.where(dist == k + 1, pw[k][1], pi)
    nb = L // 8

    def blk(t, carry):
        cr, ci, acc = carry
        b = (nb - 1 - t) if reverse else t
        r0 = pl.multiple_of(b * 8, 8)
        xr = s_re[pl.ds(r0, 8), :]
        xi = s_im[pl.ds(r0, 8), :]
        for d in (1, 2, 4):
            if reverse:
                keep = row < 8 - d
                sr, si = pltpu.roll(xr, 8 - d, axis=0), pltpu.roll(xi, 8 - d, axis=0)
            else:
                keep = row >= d
                sr, si = pltpu.roll(xr, d, axis=0), pltpu.roll(xi, d, axis=0)
            sr = jnp.where(keep, sr, 0.0)
            si = jnp.where(keep, si, 0.0)
            mr, mi = _cmul(pw[d - 1][0], pw[d - 1][1], sr, si)
            xr, xi = xr + mr, xi + mi
        mr, mi = _cmul(pr, pi, cr, ci)
        xr, xi = xr + mr, xi + mi
        s_re[pl.ds(r0, 8), :] = xr
        s_im[pl.ds(r0, 8), :] = xi
        if visit is not None:
            acc = visit(b, xr, xi, acc)
        if reverse:
            return xr[0:1, :], xi[0:1, :], acc
        return xr[7:8, :], xi[7:8, :], acc

    zero = jnp.zeros((1, n), F32)
    return lax.fori_loop(0, nb, blk, (zero, zero, visit_init if visit is not None else 0))[2]


def _ssm_project(u_ref, wbr, wbi, s_re, s_im, L):
    ch = min(L, 256)

    def rows(k, carry):
        r = pl.multiple_of(k * ch, 8)
        ub = u_ref[pl.ds(r, ch), :].astype(BF16)
        s_re[pl.ds(r, ch), :] = _dot(ub, wbr)
        s_im[pl.ds(r, ch), :] = _dot(ub, wbi)
        return carry

    lax.fori_loop(0, L // ch, rows, 0)


def _gelu(y):
    c = math.sqrt(2.0 / math.pi)
    return 0.5 * y * (1.0 + jnp.tanh(c * (y + 0.044715 * y * y * y)))


def _gelu_grad(y):
    c = math.sqrt(2.0 / math.pi)
    th = jnp.tanh(c * (y + 0.044715 * y * y * y))
    return 0.5 * (1.0 + th) + 0.5 * y * (1.0 - th * th) * c * (1.0 + 3.0 * 0.044715 * y * y)


def _ssm_fwd(p, b_re_bd, b_im_bd, c_re_bd, c_im_bd, abr, abi, fr, fi, d_skip):
    L = p.shape[0]
    ns = NST // SSM_SLAB
    ch = min(L, 256)

    def body(u_ref, br_ref, bi_ref, cr_ref, ci_ref, abr_ref, abi_ref, fr_ref, fi_ref, d_ref,
             y_ref, zb_ref, s_re, s_im):
        j = pl.program_id(0)
        f_re, f_im = fr_ref[...], fi_ref[...]
        wbr = (f_re * br_ref[...] - f_im * bi_ref[...]).astype(BF16)
        wbi = (f_re * bi_ref[...] + f_im * br_ref[...]).astype(BF16)
        _ssm_project(u_ref, wbr, wbi, s_re, s_im, L)
        _scan_rows(s_re, s_im, abr_ref[...], abi_ref[...], L)
        crb = cr_ref[...].astype(BF16)
        cib = ci_ref[...].astype(BF16)

        def rows(k, carry):
            r = pl.multiple_of(k * ch, 8)
            part = _dot(s_re[pl.ds(r, ch), :].astype(BF16), crb) - _dot(s_im[pl.ds(r, ch), :].astype(BF16), cib)

            @pl.when(j == 0)
            def _():
                y_ref[pl.ds(r, ch), :] = part + d_ref[...] * u_ref[pl.ds(r, ch), :]

            @pl.when(j > 0)
            def _():
                y_ref[pl.ds(r, ch), :] += part

            @pl.when(j == ns - 1)
            def _():
                zb_ref[pl.ds(r, ch), :] = _gelu(y_ref[pl.ds(r, ch), :]).astype(BF16)

            return carry

        lax.fori_loop(0, L // ch, rows, 0)

    full = lambda shape: pl.BlockSpec(shape, lambda j: (0, 0))
    lanes = pl.BlockSpec((1, SSM_SLAB), lambda j: (0, j))
    return _call(body, name="ssm_fwd", grid=(ns,),
                 in_specs=[pl.BlockSpec((L, W), lambda j: (0, 3)),
                           pl.BlockSpec((W, SSM_SLAB), lambda j: (0, j)), pl.BlockSpec((W, SSM_SLAB), lambda j: (0, j)),
                           pl.BlockSpec((SSM_SLAB, W), lambda j: (j, 0)), pl.BlockSpec((SSM_SLAB, W), lambda j: (j, 0)),
                           lanes, lanes, lanes, lanes, full((1, W))],
                 out_specs=[full((L, W)), full((L, W))],
                 out_shape=[_sds((L, W), F32), _sds((L, W), BF16)],
                 scratch_shapes=[pltpu.VMEM((L, SSM_SLAB), F32), pltpu.VMEM((L, SSM_SLAB), F32)],
                 compiler_params=_params(("arbitrary",)))(p, b_re_bd, b_im_bd, c_re_bd, c_im_bd, abr, abi, fr, fi, d_skip)


def _ssm_bwd(p, y, dzb, b_re_bd, b_im_bd, c_re_bd, c_im_bd, abr, abi, fr, fi, d_skip):
    L = p.shape[0]
    ns = NST // SSM_SLAB
    ch = min(L, 256)
    n_ch = L // ch

    def body(u_ref, y_ref, dzb_ref, br_ref, bi_ref, cr_ref, ci_ref, abr_ref, abi_ref, fr_ref, fi_ref, d_ref,
             du_ref, dd_ref, dbr_ref, dbi_ref, dcr_ref, dci_ref, gar_ref, gai_ref, gfr_ref, gfi_ref,
             s_re, s_im, l_re, l_im, dy_scr, du_scr):
        j = pl.program_id(0)
        f_re, f_im = fr_ref[...], fi_ref[...]
        b_re, b_im = br_ref[...], bi_ref[...]
        wbr = (f_re * b_re - f_im * b_im).astype(BF16)
        wbi = (f_re * b_im + f_im * b_re).astype(BF16)
        a_re, a_im = abr_ref[...], abi_ref[...]

        @pl.when(j == 0)
        def _():
            def rows(k, acc):
                r = pl.multiple_of(k * ch, 8)
                dy = dzb_ref[pl.ds(r, ch), :] * _gelu_grad(y_ref[pl.ds(r, ch), :])
                dy_scr[pl.ds(r, ch), :] = dy
                du_scr[pl.ds(r, ch), :] = d_ref[...] * dy
                return acc + jnp.sum(dy * u_ref[pl.ds(r, ch), :], axis=0, keepdims=True)

            dd_ref[...] = lax.fori_loop(0, n_ch, rows, jnp.zeros((1, W), F32))

        _ssm_project(u_ref, wbr, wbi, s_re, s_im, L)
        _scan_rows(s_re, s_im, a_re, a_im, L)
        crb = cr_ref[...].astype(BF16)
        cib = ci_ref[...].astype(BF16)

        def rows_c(k, acc):
            dcr, dci = acc
            r = pl.multiple_of(k * ch, 8)
            dyb = dy_scr[pl.ds(r, ch), :].astype(BF16)
            dcr = dcr + _dot(s_re[pl.ds(r, ch), :].astype(BF16), dyb, 0, 0)
            dci = dci - _dot(s_im[pl.ds(r, ch), :].astype(BF16), dyb, 0, 0)
            l_re[pl.ds(r, ch), :] = _dot(dyb, crb, 1, 1)
            l_im[pl.ds(r, ch), :] = -_dot(dyb, cib, 1, 1)
            return dcr, dci

        zc = jnp.zeros((SSM_SLAB, W), F32)
        dcr, dci = lax.fori_loop(0, n_ch, rows_c, (zc, zc))
        dcr_ref[...] = dcr
        dci_ref[...] = dci

        row8 = lax.broadcasted_iota(jnp.int32, (8, SSM_SLAB), 0)

        def visit(b, lr, li, acc):
            ar_acc, ai_acc = acc
            r0 = pl.multiple_of(b * 8, 8)
            rp = pl.multiple_of(jnp.maximum(b * 8 - 8, 0), 8)
            has_prev = b > 0
            pr = jnp.where(has_prev, s_re[pl.ds(rp, 8), :][7:8, :], 0.0)
            pi = jnp.where(has_prev, s_im[pl.ds(rp, 8), :][7:8, :], 0.0)
            sr = jnp.where(row8 >= 1, pltpu.roll(s_re[pl.ds(r0, 8), :], 1, axis=0), pr)
            si = jnp.where(row8 >= 1, pltpu.roll(s_im[pl.ds(r0, 8), :], 1, axis=0), pi)
            return ar_acc + lr * sr + li * si, ai_acc - lr * si + li * sr

        z8 = jnp.zeros((8, SSM_SLAB), F32)
        ar_acc, ai_acc = _scan_rows(l_re, l_im, a_re, -a_im, L, reverse=True, visit=visit, visit_init=(z8, z8))
        gar_ref[...] = jnp.sum(ar_acc, axis=0, keepdims=True)
        gai_ref[...] = jnp.sum(ai_acc, axis=0, keepdims=True)

        def rows_b(k, acc):
            dwr, dwi = acc
            r = pl.multiple_of(k * ch, 8)
            ub = u_ref[pl.ds(r, ch), :].astype(BF16)
            lrb = l_re[pl.ds(r, ch), :].astype(BF16)
            lib = l_im[pl.ds(r, ch), :].astype(BF16)
            du_scr[pl.ds(r, ch), :] += _dot(lrb, wbr, 1, 1) + _dot(lib, wbi, 1, 1)
            return dwr + _dot(ub, lrb, 0, 0), dwi + _dot(ub, lib, 0, 0)

        zb = jnp.zeros((W, SSM_SLAB), F32)
        dwr, dwi = lax.fori_loop(0, n_ch, rows_b, (zb, zb))
        dbr_ref[...] = dwr * f_re + dwi * f_im
        dbi_ref[...] = -dwr * f_im + dwi * f_re
        gfr_ref[...] = jnp.sum(dwr * b_re + dwi * b_im, axis=0, keepdims=True)
        gfi_ref[...] = jnp.sum(-dwr * b_im + dwi * b_re, axis=0, keepdims=True)

        @pl.when(j == ns - 1)
        def _():
            du_ref[...] = du_scr[...].astype(BF16)

    full = lambda shape: pl.BlockSpec(shape, lambda j: (0, 0))
    lanes = pl.BlockSpec((1, SSM_SLAB), lambda j: (0, j))
    bspec = pl.BlockSpec((W, SSM_SLAB), lambda j: (0, j))
    cspec = pl.BlockSpec((SSM_SLAB, W), lambda j: (j, 0))
    slab = lambda: pltpu.VMEM((L, SSM_SLAB), F32)
    return _call(body, name="ssm_bwd", grid=(ns,),
                 in_specs=[pl.BlockSpec((L, W), lambda j: (0, 3)), full((L, W)), full((L, W)),
                           bspec, bspec, cspec, cspec, lanes, lanes, lanes, lanes, full((1, W))],
                 out_specs=[full((L, W)), full((1, W)), bspec, bspec, cspec, cspec, lanes, lanes, lanes, lanes],
                 out_shape=[_sds((L, W), BF16), _sds((1, W), F32), _sds((W, NST), F32), _sds((W, NST), F32),
                            _sds((NST, W), F32), _sds((NST, W), F32)] + [_sds((1, NST), F32)] * 4,
                 scratch_shapes=[slab(), slab(), slab(), slab(), pltpu.VMEM((L, W), F32), pltpu.VMEM((L, W), F32)],
                 compiler_params=_params(("arbitrary",)))(p, y, dzb, b_re_bd, b_im_bd, c_re_bd, c_im_bd,
                                                          abr, abi, fr, fi, d_skip)


SB_KB = 512


def _split3(x):
    hi = x.astype(BF16)
    r1 = x - hi.astype(F32)
    mid = r1.astype(BF16)
    return hi, mid, (r1 - mid.astype(F32)).astype(BF16)


def _ones_dot(x, ones):
    n = x.shape[0]
    r = _dot(jnp.concatenate(_split3(x), axis=0), ones)
    return r[:n] + r[n:2 * n] + r[2 * n:]


def _sb_block(q, kj, i, jb, kb, right, tri):
    z = _bdot(q, kj, 1, 1)
    t_idx = lax.broadcasted_iota(jnp.int32, (QB, kb), 0) + i * QB
    s_idx = lax.broadcasted_iota(jnp.int32, (QB, kb), 1) + jb * kb
    mask = s_idx < t_idx
    lk_all = jnp.minimum(-z, 0.0) - jnp.log1p(jnp.exp(-jnp.abs(z)))
    lk = jnp.where(mask, lk_all, 0.0)
    suf = _ones_dot(lk, tri)
    a = jnp.where(mask, jnp.exp((lk_all + z) + (suf - lk) + right), 0.0)
    return z, mask, suf, a


def _sb_ones(kb):
    r = lax.broadcasted_iota(jnp.int32, (kb, kb), 0)
    c = lax.broadcasted_iota(jnp.int32, (kb, kb), 1)
    return (r >= c).astype(BF16), (r < c).astype(BF16)


def _sb_fwd(q, k, v):
    L = q.shape[1]
    kb = min(SB_KB, L)
    per = kb // QB

    def body(q_ref, k_ref, v_ref, o_ref, rs_ref):
        i = pl.program_id(0)
        tri, _ = _sb_ones(kb)
        lane = lax.broadcasted_iota(jnp.int32, (QB, 128), 1)
        qs = [q_ref[h] for h in range(HEADS)]

        def step(t, carry):
            accs, rights, sums = carry
            jb = i // per - t
            r = pl.multiple_of(jb * kb, kb)
            out = []
            for h in range(HEADS):
                _, _, suf, a = _sb_block(qs[h], k_ref[h, pl.ds(r, kb), :], i, jb, kb, rights[h], tri)
                tot = suf[:, 0:1]
                out.append((accs[h] + _bdot(a, v_ref[h, pl.ds(r, kb), :]), rights[h] + tot,
                            sums[h] + jnp.where(lane == jb, tot, 0.0)))
            return tuple(o[0] for o in out), tuple(o[1] for o in out), tuple(o[2] for o in out)

        init = (tuple(jnp.zeros((QB, HD), F32) for _ in range(HEADS)), tuple(jnp.zeros((QB, 1), F32) for _ in range(HEADS)),
                tuple(jnp.zeros((QB, 128), F32) for _ in range(HEADS)))
        accs, _, sums = lax.fori_loop(0, i // per + 1, step, init)
        for h in range(HEADS):
            o_ref[h] = accs[h]
            rs_ref[h] = sums[h]

    heads = pl.BlockSpec((HEADS, L, HD), lambda i: (0, 0, 0))
    blk = pl.BlockSpec((HEADS, QB, HD), lambda i: (0, i, 0))
    return _call(body, name="sb_fwd", grid=(L // QB,), in_specs=[blk, heads, heads],
                 out_specs=[blk, pl.BlockSpec((HEADS, QB, 128), lambda i: (0, i, 0))],
                 out_shape=[_sds((HEADS, L, HD), F32), _sds((HEADS, L, 128), F32)],
                 compiler_params=_params(("parallel",)))(q, k, v)


def _sb_bwd(q, k, v, do, block_sums):
    L = q.shape[1]
    kb = min(SB_KB, L)
    per = kb // QB

    def body(q_ref, k_ref, v_ref, do_ref, rs_ref, dq_ref, dk_ref, dv_ref):
        i = pl.program_id(0)
        tri, tri_strict = _sb_ones(kb)
        lane = lax.broadcasted_iota(jnp.int32, (QB, 128), 1)

        @pl.when(i == 0)
        def _():
            dk_ref[...] = jnp.zeros_like(dk_ref)
            dv_ref[...] = jnp.zeros_like(dv_ref)

        qs = [q_ref[h] for h in range(HEADS)]
        dos = [do_ref[h] for h in range(HEADS)]
        sums = [rs_ref[h] for h in range(HEADS)]

        def step(jb, carry):
            dqs, lefts = carry
            r = pl.multiple_of(jb * kb, kb)
            out = []
            for h in range(HEADS):
                kj = k_ref[h, pl.ds(r, kb), :]
                vj = v_ref[h, pl.ds(r, kb), :]
                right = jnp.sum(jnp.where(lane > jb, sums[h], 0.0), axis=1, keepdims=True)
                z, mask, _, a = _sb_block(qs[h], kj, i, jb, kb, right, tri)
                e = a * _bdot(dos[h], vj, 1, 1)
                dv_ref[h, pl.ds(r, kb), :] += _dot(a.astype(BF16), dos[h].astype(BF16), 0, 0)
                before = lefts[h] + _ones_dot(e, tri_strict)
                sg = _sigmoid(z)
                dz = jnp.where(mask, e * (1.0 - sg) - sg * before, 0.0).astype(BF16)
                dk_ref[h, pl.ds(r, kb), :] += _dot(dz, qs[h].astype(BF16), 0, 0)
                out.append((dqs[h] + _dot(dz, kj.astype(BF16)), lefts[h] + jnp.sum(e, axis=1, keepdims=True)))
            return tuple(o[0] for o in out), tuple(o[1] for o in out)

        init = (tuple(jnp.zeros((QB, HD), F32) for _ in range(HEADS)), tuple(jnp.zeros((QB, 1), F32) for _ in range(HEADS)))
        dqs, _ = lax.fori_loop(0, i // per + 1, step, init)
        for h in range(HEADS):
            dq_ref[h] = dqs[h]

    heads = pl.BlockSpec((HEADS, L, HD), lambda i: (0, 0, 0))
    blk = pl.BlockSpec((HEADS, QB, HD), lambda i: (0, i, 0))
    shp = _sds((HEADS, L, HD), F32)
    return _call(body, name="sb_bwd", grid=(L // QB,),
                 in_specs=[blk, heads, heads, blk, pl.BlockSpec((HEADS, QB, 128), lambda i: (0, i, 0))],
                 out_specs=[blk, heads, heads], out_shape=[shp, shp, shp],
                 compiler_params=_params(("arbitrary",)))(q, k, v, do, block_sums)


def _merge_fwd(za, zb, zc, zd, p, w_conv_out, w_glu, w_pool_out, w_sb_out):
    L = za.shape[0]
    tm = _row_tile(L)

    def body(za_ref, zb_ref, zc_ref, zd_ref, g0, g1, g2, g3, wc_ref, wg_ref, wp_ref, ws_ref, o_ref):
        glu = _dot(zb_ref[...], wg_ref[...])
        yb = glu[:, :D] * _sigmoid(glu[:, D:])
        m = _sigmoid(g0[...]) * _dot(za_ref[...], wc_ref[...])
        m = m + _sigmoid(g1[...]) * yb
        m = m + _sigmoid(g2[...]) * _dot(zc_ref[...], wp_ref[...])
        m = m + _sigmoid(g3[...]) * _dot(zd_ref[...], ws_ref[...])
        o_ref[...] = m.astype(BF16)

    zt = pl.BlockSpec((tm, W), lambda i: (i, 0))
    gate = lambda b: pl.BlockSpec((tm, D), lambda i: (i, 2 + b))
    wfull = lambda n: pl.BlockSpec((W, n), lambda i: (0, 0))
    return _call(body, name="merge_fwd", grid=(L // tm,),
                 in_specs=[zt, zt, zt, zt, gate(0), gate(1), gate(2), gate(3), wfull(D), wfull(2 * D), wfull(D), wfull(D)],
                 out_specs=pl.BlockSpec((tm, D), lambda i: (i, 0)), out_shape=_sds((L, D), BF16),
                 compiler_params=_params(("parallel",)))(za, zb, zc, zd, p, p, p, p, w_conv_out, w_glu, w_pool_out, w_sb_out)


def _merge_bwd(dm, za, zb, zc, zd, p, w_conv_out, w_glu, w_pool_out, w_sb_out):
    L = za.shape[0]
    tm = _row_tile(L)
    n = L // tm

    def body(dm_ref, za_ref, zb_ref, zc_ref, zd_ref, g0, g1, g2, g3, wc_ref, wg_ref, wp_ref, ws_ref,
             dza_ref, dzb_ref, dzc_ref, dzd_ref, dg_ref, dwc_ref, dwg_ref, dwp_ref, dws_ref,
             awc, awg, awp, aws):
        i = pl.program_id(0)

        @pl.when(i == 0)
        def _():
            awc[...] = jnp.zeros_like(awc)
            awg[...] = jnp.zeros_like(awg)
            awp[...] = jnp.zeros_like(awp)
            aws[...] = jnp.zeros_like(aws)

        dmv = dm_ref[...]

        def gated(g_ref, y, col):
            s = _sigmoid(g_ref[...])
            dg_ref[:, col * D:(col + 1) * D] = (dmv * y * s * (1.0 - s)).astype(BF16)
            return (dmv * s)

        def linear(z_ref, w_ref, acc, dz_ref, col, g_ref):
            zv = z_ref[...]
            dy = gated(g_ref, _dot(zv, w_ref[...]), col).astype(BF16)
            dz_ref[...] = _dot(dy, w_ref[...], 1, 1)
            acc[...] += _dot(zv, dy, 0, 0)

        linear(za_ref, wc_ref, awc, dza_ref, 0, g0)
        linear(zc_ref, wp_ref, awp, dzc_ref, 2, g2)
        linear(zd_ref, ws_ref, aws, dzd_ref, 3, g3)
        zbv = zb_ref[...]
        glu = _dot(zbv, wg_ref[...])
        ga = glu[:, :D]
        sg = _sigmoid(glu[:, D:])
        dyb = gated(g1, ga * sg, 1)
        dga = (dyb * sg).astype(BF16)
        dgg = (dyb * ga * sg * (1.0 - sg)).astype(BF16)
        dzb_ref[...] = _dot(dga, wg_ref[:, :D], 1, 1) + _dot(dgg, wg_ref[:, D:], 1, 1)
        awg[:, :D] += _dot(zbv, dga, 0, 0)
        awg[:, D:] += _dot(zbv, dgg, 0, 0)

        @pl.when(i == n - 1)
        def _():
            dwc_ref[...] = awc[...].astype(BF16)
            dwg_ref[...] = awg[...].astype(BF16)
            dwp_ref[...] = awp[...].astype(BF16)
            dws_ref[...] = aws[...].astype(BF16)

    zt = pl.BlockSpec((tm, W), lambda i: (i, 0))
    gate = lambda b: pl.BlockSpec((tm, D), lambda i: (i, 2 + b))
    wfull = lambda n_: pl.BlockSpec((W, n_), lambda i: (0, 0))
    zs = _sds((L, W), F32)
    return _call(body, name="merge_bwd", grid=(n,),
                 in_specs=[pl.BlockSpec((tm, D), lambda i: (i, 0)), zt, zt, zt, zt, gate(0), gate(1), gate(2), gate(3),
                           wfull(D), wfull(2 * D), wfull(D), wfull(D)],
                 out_specs=[zt, zt, zt, zt, pl.BlockSpec((tm, 4 * D), lambda i: (i, 0)),
                            wfull(D), wfull(2 * D), wfull(D), wfull(D)],
                 out_shape=[zs, zs, zs, zs, _sds((L, 4 * D), BF16),
                            _sds((W, D), BF16), _sds((W, 2 * D), BF16), _sds((W, D), BF16), _sds((W, D), BF16)],
                 scratch_shapes=[pltpu.VMEM((W, D), F32), pltpu.VMEM((W, 2 * D), F32), pltpu.VMEM((W, D), F32),
                                 pltpu.VMEM((W, D), F32)],
                 compiler_params=_params(("arbitrary",)))(dm, za, zb, zc, zd, p, p, p, p,
                                                          w_conv_out, w_glu, w_pool_out, w_sb_out)


def _adam_math(w, g, m, v):
    m2 = ADAM_B1 * m + (1.0 - ADAM_B1) * g
    v2 = ADAM_B2 * v + (1.0 - ADAM_B2) * (g * g)
    m_hat = m2 / (1.0 - ADAM_B1 ** ADAM_STEP)
    v_hat = v2 / (1.0 - ADAM_B2 ** ADAM_STEP)
    return -ADAM_LR * (m_hat / (jnp.sqrt(v_hat) + ADAM_EPS) + ADAM_WD * w), m2, v2


def _as_rows(a):
    return a.reshape(-1, a.shape[-1])


def _adamw(w, g, m, v):
    shape = w.shape
    w2, g2, m2, v2 = _as_rows(w), _as_rows(g), _as_rows(m), _as_rows(v)
    R, C = w2.shape
    tr = R
    for cand in (1024, 512, 256, 128, 64, 32, 16, 8):
        if R % cand == 0 and cand * C * 4 <= 2 * 1024 * 1024:
            tr = cand
            break

    def body(w_ref, g_ref, m_ref, v_ref, d_ref, m_out, v_out):
        d, mn, vn = _adam_math(w_ref[...], g_ref[...], m_ref[...], v_ref[...])
        d_ref[...] = d
        m_out[...] = mn
        v_out[...] = vn

    blk = pl.BlockSpec((tr, C), lambda i: (i, 0))
    shp = _sds((R, C), F32)
    outs = _call(body, name="adamw", grid=(R // tr,), in_specs=[blk] * 4, out_specs=[blk] * 3, out_shape=[shp] * 3,
                 compiler_params=_params(("parallel",)))(w2, g2, m2, v2)
    return tuple(o.reshape(shape) for o in outs)


def _sum_parts(parts, out_dtype, name):
    shape = parts[0].shape
    flat = [_as_rows(a) for a in parts]
    R, C = flat[0].shape
    tr = R
    for cand in (1024, 512, 256, 128, 64, 32, 16):
        if R % cand == 0 and cand * C * 4 <= 2 * 1024 * 1024:
            tr = cand
            break
    k = len(parts)

    def body(*refs):
        acc = refs[0][...].astype(F32)
        for r in refs[1:k]:
            acc = acc + r[...].astype(F32)
        refs[k][...] = acc.astype(out_dtype)

    blk = pl.BlockSpec((tr, C), lambda i: (i, 0))
    out = _call(body, name=name, grid=(R // tr,), in_specs=[blk] * k, out_specs=blk, out_shape=_sds((R, C), out_dtype),
                compiler_params=_params(("parallel",)))(*flat)
    return out.reshape(shape)


ADA_SHARD = 9 * D // N_CHIP
ADA_TN = 768


def _ada_fwd(c_pad, w_ada, b_ada_cols):
    depth = w_ada.shape[0]

    def body(c_ref, w_ref, b_ref, o_ref):
        cv = c_ref[...]
        o_ref[...] = _bdot(cv * _sigmoid(cv), w_ref[...]) + b_ref[...]

    return _call(body, name="ada_fwd", grid=(depth, ADA_SHARD // ADA_TN),
                 in_specs=[pl.BlockSpec((16, D), lambda l, j: (0, 0)),
                           pl.BlockSpec((None, D, ADA_TN), lambda l, j: (l, 0, j)),
                           pl.BlockSpec((None, 1, ADA_TN), lambda l, j: (l, 0, j))],
                 out_specs=pl.BlockSpec((None, 16, ADA_TN), lambda l, j: (l, 0, j)),
                 out_shape=_sds((depth, 16, ADA_SHARD), F32),
                 compiler_params=_params(("parallel", "parallel")))(c_pad, w_ada, b_ada_cols)


def _ada_wgrad(c_pad, d_ada):
    depth = d_ada.shape[0]

    def body(c_ref, d_ref, o_ref):
        cv = c_ref[...]
        o_ref[...] = _bdot(cv * _sigmoid(cv), d_ref[...], 0, 0)

    return _call(body, name="ada_wgrad", grid=(depth, ADA_SHARD // ADA_TN),
                 in_specs=[pl.BlockSpec((16, D), lambda l, j: (0, 0)),
                           pl.BlockSpec((None, 16, ADA_TN), lambda l, j: (l, 0, j))],
                 out_specs=pl.BlockSpec((None, D, ADA_TN), lambda l, j: (l, 0, j)),
                 out_shape=_sds((depth, D, ADA_SHARD), F32),
                 compiler_params=_params(("parallel", "parallel")))(c_pad, d_ada)


HBM_SPEC = pl.BlockSpec(memory_space=pltpu.HBM)


def _place():
    x, y, c = lax.axis_index("x"), lax.axis_index("y"), lax.axis_index("c")
    peers = [(1 - x, y), (x, 1 - y), (1 - x, 1 - y)]
    return x, y, c, peers


def _chip(px, py):
    return 2 * px + py


def _allgather8(block, name):
    m_per, n = block.shape

    def body(x_ref, out_ref, send_sems, recv_sems, local_sem):
        x, y, c, chips = _place()
        me, sibling = (x, y, c), (x, y, 1 - c)

        def rows(px, py, pc):
            return out_ref.at[pl.ds(pl.multiple_of((4 * px + 2 * py + pc) * m_per, 8), m_per), :]

        def copy(k, blk, to, src=None):
            return pltpu.make_async_remote_copy(
                src_ref=rows(*blk) if src is None else src, dst_ref=rows(*blk),
                send_sem=send_sems.at[k], recv_sem=recv_sems.at[k], device_id=to, device_id_type=MESH)

        mine = pltpu.make_async_copy(x_ref, rows(*me), local_sem)
        mine.start()
        first = [copy(0, me, sibling, src=x_ref)]
        first += [copy(1 + j, me, (*chip, c), src=x_ref) for j, chip in enumerate(chips)]
        for cp in first:
            cp.start()
        passed = [copy(4 + j, (*chip, c), sibling) for j, chip in enumerate(chips)]
        for j, chip in enumerate(chips):
            copy(1 + j, (*chip, c), me).wait_recv()
            passed[j].start()
        copy(0, sibling, me).wait_recv()
        for j, chip in enumerate(chips):
            copy(4 + j, (*chip, 1 - c), me).wait_recv()
        for cp in first + passed:
            cp.wait_send()
        mine.wait()

    return _call(body, name=name, out_shape=_sds((N_DEV * m_per, n), block.dtype),
                 in_specs=[pl.BlockSpec(memory_space=pltpu.VMEM)], out_specs=pl.BlockSpec(memory_space=pltpu.VMEM),
                 scratch_shapes=[pltpu.SemaphoreType.DMA((7,)), pltpu.SemaphoreType.DMA((7,)), pltpu.SemaphoreType.DMA],
                 compiler_params=_params())(block)


GATHERED = (("w_ff_in", -1), ("w_ff_out", -2), ("w_in", -1), ("w_conv_out", -1), ("w_glu", -1),
            ("w_pool_out", -1), ("w_sb_out", -1), ("w_out", -2))


def _lead(ref):
    return (slice(None),) * (len(ref.shape) - 2)


def _mo(v, m):
    return v if isinstance(v, int) else pl.multiple_of(v, m)


def _full_region(ref, axis, j, half, shard_shape):
    rs, cs = shard_shape[-2], shard_shape[-1]
    if axis == -1:
        r0, nr = (0, rs) if half is None else (half * (rs // 2), rs // 2)
        return ref.at[_lead(ref) + (pl.ds(_mo(r0, 16), nr), pl.ds(_mo(j * cs, 128), cs))]
    r0, nr = (j * rs, rs) if half is None else (j * rs + half * (rs // 2), rs // 2)
    return ref.at[_lead(ref) + (pl.ds(_mo(r0, 16), nr), slice(None))]


def _shard_half(ref, half):
    rs = ref.shape[-2]
    return ref.at[_lead(ref) + (pl.ds(_mo(half * (rs // 2), 16), rs // 2), slice(None))]


def _full_shape(shard_shape, axis):
    s = list(shard_shape)
    s[axis] *= N_CHIP
    return tuple(s)


class _Lay:
    def __init__(self, shard_shape, axis):
        self.axis = axis
        self.shard_shape = tuple(shard_shape)
        self.full_shape = _full_shape(shard_shape, axis)
        self.lead = int(np.prod(shard_shape[:-2]))
        self.rs, self.cs = shard_shape[-2], shard_shape[-1]
        self.hr = self.rs // 2
        self.tr = next(t for t in (256, 128, 64, 32, 16) if self.hr % t == 0 and t * self.cs * 4 <= (1 << 20))
        self.half_rows_shape = _half_rows_shape(self.full_shape)
        self.half_shard_shape = _half_rows_shape(self.shard_shape)

    def full(self, jf, hf):
        if self.axis == -1:
            return ((self.lead, 2, self.hr, N_CHIP * self.cs),
                    pl.BlockSpec((None, None, self.tr, self.cs), lambda b, j, i, s: (b, hf(j, s), i, jf(j, s))))
        return ((self.lead, N_CHIP, 2, self.hr, self.cs),
                pl.BlockSpec((None, None, None, self.tr, self.cs), lambda b, j, i, s: (b, jf(j, s), hf(j, s), i, 0)))

    def half_rows(self, jf):
        if self.axis == -1:
            return ((self.lead, self.hr, N_CHIP * self.cs),
                    pl.BlockSpec((None, self.tr, self.cs), lambda b, j, i, s: (b, i, jf(j, s))))
        return ((self.lead, N_CHIP, self.hr, self.cs),
                pl.BlockSpec((None, None, self.tr, self.cs), lambda b, j, i, s: (b, jf(j, s), i, 0)))

    def half_shard(self):
        return (self.lead, self.hr, self.cs), pl.BlockSpec((None, self.tr, self.cs), lambda b, j, i, s: (b, i, 0))

    def shard(self, hf):
        return ((self.lead, 2, self.hr, self.cs),
                pl.BlockSpec((None, None, self.tr, self.cs), lambda b, j, i, s: (b, hf(j, s), i, 0)))


def _view_sum(sel, operands, out_view, out_shape, out_dtype, grid, name):
    k = len(operands)

    def body(sel_ref, *refs):
        acc = refs[0][...].astype(F32)
        for r in refs[1:k]:
            acc = acc + r[...].astype(F32)
        refs[k][...] = acc.astype(out_dtype)

    spec = pltpu.PrefetchScalarGridSpec(num_scalar_prefetch=1, grid=grid, in_specs=[v[1] for _, v in operands],
                                        out_specs=out_view[1])
    out = _call(body, name=name, grid_spec=spec, out_shape=_sds(out_view[0], out_dtype),
                compiler_params=_params(("parallel", "parallel", "parallel")))(
                    sel, *[a.reshape(v[0]) for a, v in operands])
    return out.reshape(out_shape)


def _sel_core(j, s):
    return s[0]


def _sel_chip(j, s):
    return s[1]


def _grid_j(j, s):
    return j


def _place_shard(lay, sel, w):
    return _view_sum(sel, [(w, lay.shard(_grid_j))], lay.full(_sel_chip, _grid_j), lay.full_shape, BF16,
                     (lay.lead, 2, lay.hr // lay.tr), "place_shard")


SEM_SPEC = pl.BlockSpec(memory_space=pltpu.SEMAPHORE)
ANY_SPEC = pl.BlockSpec(memory_space=pl.ANY)
SPLIT_COPY = pltpu.SideEffectType.DATAFLOW_SIDE_EFFECTING


def _in_hbm(a):
    return pltpu.with_memory_space_constraint(a, pltpu.HBM)


def _gather_start(fulls, lays, tag):
    n = len(fulls)

    def body(*refs):
        send_sems, recv_sems = refs[n], refs[n + 1]
        bufs, token = refs[n + 2:2 * n + 2], refs[2 * n + 2]
        x, y, c, chips = _place()
        my = _chip(x, y)
        for a in range(n):
            own = _full_region(bufs[a], lays[a].axis, my, c, lays[a].shard_shape)
            for k, chip in enumerate(chips):
                pltpu.make_async_remote_copy(
                    src_ref=own, dst_ref=own, send_sem=send_sems.at[a * 3 + k], recv_sem=recv_sems.at[a * 3 + k],
                    device_id=(*chip, c), device_id_type=MESH).start()
        token[...] = jnp.zeros_like(token)

    outs = _call(body, name="gather_start_" + tag,
                 out_shape=[pltpu.SemaphoreType.DMA((3 * n,)), pltpu.SemaphoreType.DMA((3 * n,))]
                 + [pltpu.HBM(f.shape, f.dtype) for f in fulls] + [_sds((8, 128), F32)],
                 in_specs=[HBM_SPEC] * n,
                 out_specs=[SEM_SPEC, SEM_SPEC] + [HBM_SPEC] * n + [pl.BlockSpec(memory_space=pltpu.VMEM)],
                 input_output_aliases={a: a + 2 for a in range(n)},
                 compiler_params=pltpu.CompilerParams(has_side_effects=SPLIT_COPY))(*[_in_hbm(f) for f in fulls])
    return outs[0], outs[1], outs[2:2 + n], outs[2 + n]


def _gather_wait(send_sems, recv_sems, bufs, after, lays, tag):
    n = len(bufs)

    def body(*refs):
        ss, rs = refs[n], refs[n + 1]
        outs = refs[n + 3:]
        x, y, c, chips = _place()
        my = _chip(x, y)
        for a in range(n):
            own = _full_region(outs[a], lays[a].axis, my, c, lays[a].shard_shape)
            for k, chip in enumerate(chips):
                landed = _full_region(outs[a], lays[a].axis, _chip(*chip), c, lays[a].shard_shape)
                cp = pltpu.make_async_remote_copy(
                    src_ref=own, dst_ref=landed, send_sem=ss.at[a * 3 + k], recv_sem=rs.at[a * 3 + k],
                    device_id=(*chip, c), device_id_type=MESH)
                cp.wait_send()
                cp.wait_recv()

    return _call(body, name="gather_wait_" + tag,
                 out_shape=[pltpu.HBM(b.shape, b.dtype) for b in bufs],
                 in_specs=[HBM_SPEC] * n + [SEM_SPEC, SEM_SPEC, ANY_SPEC], out_specs=[HBM_SPEC] * n,
                 input_output_aliases={a: a for a in range(n)},
                 compiler_params=pltpu.CompilerParams(has_side_effects=SPLIT_COPY))(*bufs, send_sems, recv_sems, after)


def _gather_forward(bufs, lays):
    n = len(bufs)

    def body(*refs):
        outs = refs[n:2 * n]
        send_sems, recv_sems = refs[2 * n:]
        x, y, c, chips = _place()
        sibling = (x, y, 1 - c)
        sends = []
        for a in range(n):
            for k, chip in enumerate(chips):
                landed = _full_region(outs[a], lays[a].axis, _chip(*chip), c, lays[a].shard_shape)
                cp = pltpu.make_async_remote_copy(
                    src_ref=landed, dst_ref=landed, send_sem=send_sems.at[a * 3 + k], recv_sem=recv_sems.at[a * 3 + k],
                    device_id=sibling, device_id_type=MESH)
                cp.start()
                sends.append(cp)
        for a in range(n):
            for k, chip in enumerate(chips):
                passed = _full_region(outs[a], lays[a].axis, _chip(*chip), 1 - c, lays[a].shard_shape)
                pltpu.make_async_remote_copy(
                    src_ref=passed, dst_ref=passed, send_sem=send_sems.at[a * 3 + k], recv_sem=recv_sems.at[a * 3 + k],
                    device_id=sibling, device_id_type=MESH).wait_recv()
        for cp in sends:
            cp.wait_send()

    return _call(body, name="gather_forward",
                 out_shape=[_sds(b.shape, b.dtype) for b in bufs],
                 in_specs=[HBM_SPEC] * n, out_specs=[HBM_SPEC] * n,
                 input_output_aliases={a: a for a in range(n)},
                 scratch_shapes=[pltpu.SemaphoreType.DMA((3 * n,)), pltpu.SemaphoreType.DMA((3 * n,))],
                 compiler_params=_params())(*bufs)


def _half_rows_shape(full_shape):
    s = list(full_shape)
    s[-2] //= 2
    return tuple(s)


def _reduce_sibling(grads, lays):
    n = len(grads)

    def pieces(lay, ref_full, ref_half, half):
        if lay.axis == -1:
            src = ref_full.at[_lead(ref_full) + (pl.ds(_mo(half * lay.hr, 16), lay.hr), slice(None))]
            return [(src, ref_half)]
        out = []
        for j in range(N_CHIP):
            src = _full_region(ref_full, -2, j, half, lay.shard_shape)
            dst = ref_half.at[_lead(ref_half) + (pl.ds(j * lay.hr, lay.hr), slice(None))]
            out.append((src, dst))
        return out

    n_cp = sum(1 if lay.axis == -1 else N_CHIP for lay in lays)

    def body(*refs):
        ins, got = refs[:n], refs[n:2 * n]
        send_sems, recv_sems = refs[2 * n:]
        x, y, c, _ = _place()
        sibling = (x, y, 1 - c)
        started, idx = [], 0
        for a in range(n):
            for src, dst in pieces(lays[a], ins[a], got[a], 1 - c):
                rc = pltpu.make_async_remote_copy(src_ref=src, dst_ref=dst, send_sem=send_sems.at[idx],
                                                  recv_sem=recv_sems.at[idx], device_id=sibling, device_id_type=MESH)
                rc.start()
                started.append(rc)
                idx += 1
        for rc in started:
            rc.wait_recv()
            rc.wait_send()

    return _call(body, name="reduce_sibling",
                 out_shape=[_sds(lay.half_rows_shape, BF16) for lay in lays],
                 in_specs=[HBM_SPEC] * n, out_specs=[HBM_SPEC] * n,
                 scratch_shapes=[pltpu.SemaphoreType.DMA((n_cp,)), pltpu.SemaphoreType.DMA((n_cp,))],
                 compiler_params=_params())(*grads)


def _chip_region(lay, ref, j):
    if lay.axis == -1:
        return ref.at[_lead(ref) + (slice(None), pl.ds(_mo(j * lay.cs, 128), lay.cs))]
    return ref.at[_lead(ref) + (pl.ds(_mo(j * lay.hr, 16), lay.hr), slice(None))]


def _reduce_start(parts, lays, tag):
    n = len(parts)
    landing = [_in_hbm(lax.empty(lay.half_shard_shape, BF16)) for lay in lays for _ in range(3)]

    def body(*refs):
        send_sems, recv_sems = refs[4 * n], refs[4 * n + 1]
        src, land, token = refs[4 * n + 2:5 * n + 2], refs[5 * n + 2:8 * n + 2], refs[8 * n + 2]
        x, y, c, chips = _place()
        for a in range(n):
            for k, chip in enumerate(chips):
                pltpu.make_async_remote_copy(
                    src_ref=_chip_region(lays[a], src[a], _chip(*chip)), dst_ref=land[a * 3 + k],
                    send_sem=send_sems.at[a * 3 + k], recv_sem=recv_sems.at[a * 3 + k],
                    device_id=(*chip, c), device_id_type=MESH).start()
        token[...] = jnp.zeros_like(token)

    ops = [_in_hbm(p) for p in parts] + landing
    outs = _call(body, name="reduce_start_" + tag,
                 out_shape=[pltpu.SemaphoreType.DMA((3 * n,)), pltpu.SemaphoreType.DMA((3 * n,))]
                 + [pltpu.HBM(o.shape, o.dtype) for o in ops] + [_sds((8, 128), F32)],
                 in_specs=[HBM_SPEC] * (4 * n),
                 out_specs=[SEM_SPEC, SEM_SPEC] + [HBM_SPEC] * (4 * n) + [pl.BlockSpec(memory_space=pltpu.VMEM)],
                 input_output_aliases={a: a + 2 for a in range(4 * n)},
                 compiler_params=pltpu.CompilerParams(has_side_effects=SPLIT_COPY))(*ops)
    return outs[0], outs[1], outs[2:2 + n], outs[2 + n:2 + 4 * n], outs[2 + 4 * n]


def _reduce_wait(send_sems, recv_sems, parts, landing, after, lays, tag):
    n = len(parts)

    def body(*refs):
        ss, rs = refs[4 * n], refs[4 * n + 1]
        src, land = refs[4 * n + 3:5 * n + 3], refs[5 * n + 3:]
        x, y, c, chips = _place()
        for a in range(n):
            for k, chip in enumerate(chips):
                cp = pltpu.make_async_remote_copy(
                    src_ref=_chip_region(lays[a], src[a], _chip(*chip)), dst_ref=land[a * 3 + k],
                    send_sem=ss.at[a * 3 + k], recv_sem=rs.at[a * 3 + k], device_id=(*chip, c), device_id_type=MESH)
                cp.wait_send()
                cp.wait_recv()

    ops = list(parts) + list(landing)
    outs = _call(body, name="reduce_wait_" + tag,
                 out_shape=[pltpu.HBM(o.shape, o.dtype) for o in ops],
                 in_specs=[HBM_SPEC] * (4 * n) + [SEM_SPEC, SEM_SPEC, ANY_SPEC], out_specs=[HBM_SPEC] * (4 * n),
                 input_output_aliases={a: a for a in range(4 * n)},
                 compiler_params=pltpu.CompilerParams(has_side_effects=SPLIT_COPY))(*ops, send_sems, recv_sems, after)
    return outs[:n], [outs[n + 3 * a:n + 3 * a + 3] for a in range(n)]


def _share_halves(shards):
    n = len(shards)

    def body(*refs):
        outs = refs[n:2 * n]
        send_sems, recv_sems = refs[2 * n:]
        x, y, c, _ = _place()
        sibling = (x, y, 1 - c)
        started = []
        for a in range(n):
            mine = _shard_half(outs[a], c)
            rc = pltpu.make_async_remote_copy(src_ref=mine, dst_ref=mine, send_sem=send_sems.at[a],
                                              recv_sem=recv_sems.at[a], device_id=sibling, device_id_type=MESH)
            rc.start()
            started.append(rc)
        for rc in started:
            rc.wait_recv()
            rc.wait_send()

    return _call(body, name="share_halves", out_shape=[_sds(s.shape, F32) for s in shards],
                 in_specs=[HBM_SPEC] * n, out_specs=[HBM_SPEC] * n, input_output_aliases={a: a for a in range(n)},
                 scratch_shapes=[pltpu.SemaphoreType.DMA((n,)), pltpu.SemaphoreType.DMA((n,))],
                 compiler_params=_params())(*shards)


def _reduce_begin(grads, lays, sel, tag):
    got = _reduce_sibling(grads, lays)
    chip_parts = [
        _view_sum(sel, [(g, lay.full(_grid_j, _sel_core)), (o, lay.half_rows(_grid_j))], lay.half_rows(_grid_j),
                  lay.half_rows_shape, BF16, (lay.lead, N_CHIP, lay.hr // lay.tr), "chip_partial")
        for g, o, lay in zip(grads, got, lays)]
    return _reduce_start(chip_parts, lays, tag)


def _reduce_end(state, after, lays, sel, tag):
    send_sems, recv_sems, chip_parts, landing, _ = state
    chip_parts, landed = _reduce_wait(send_sems, recv_sems, chip_parts, landing, after, lays, tag)
    halves = [
        _view_sum(sel, [(t, lay.half_rows(_sel_chip))] + [(l, lay.half_shard()) for l in ls], lay.shard(_sel_core),
                  lay.shard_shape, F32, (lay.lead, 1, lay.hr // lay.tr), "shard_half_sum")
        for t, ls, lay in zip(chip_parts, landed, lays)]
    return _share_halves(halves)


def _embed(blocks):
    n, r, c = blocks.shape
    eye = jnp.eye(n, dtype=blocks.dtype)
    return (blocks[:, :, None, :] * eye[:, None, :, None]).reshape(n * r, n * c)


def _unembed(mat, n):
    r, c = mat.shape[0] // n, mat.shape[1] // n
    return jnp.transpose(jnp.diagonal(mat.reshape(n, r, n, c), axis1=0, axis2=2), (2, 0, 1))


def _to_heads(a):
    return jnp.transpose(a.reshape(a.shape[0], HEADS, HD), (1, 0, 2))


def _from_heads(a):
    return jnp.transpose(a, (1, 0, 2)).reshape(a.shape[1], W)


def _row(v):
    return v.reshape(1, -1)


def _ffn_fwd(x, ada, gp, gq, w_in, w_out, s):
    L = x.shape[0]
    h = _norm_mod(x, _row(gp[s]), _row(ada[3 * s]), _row(ada[3 * s + 1]))
    a, b, act = _ffn_in(h, w_in)
    f = _mm(act, w_out, M=L, N=D, K=FF, tm=min(L, 512), tn=512, name="ffn_out")
    x2 = _post(x, f, _row(gq[s]), _row(ada[3 * s + 2]), 0.5)
    return x2, (x, h, a, b, act, f)


def _ffn_bwd(dx, saved, ada, gp, gq, w_in, w_out, s):
    x, h, a, b, act, f = saved
    L = x.shape[0]
    df, dgate, dgq = _post_bwd(dx, f, _row(gq[s]), _row(ada[3 * s + 2]), 0.5)
    dw_out = _mm(act, df, M=FF, N=D, K=L, tm=256, tn=512, ta=True, out_dtype=BF16, name="ffn_dw_out")
    da, db = _ffn_mid_bwd(df, w_out, a, b)
    du = jnp.concatenate([da, db], axis=1)
    dw_in = _mm(h, du, M=D, N=2 * FF, K=L, tm=512, tn=512, ta=True, out_dtype=BF16, name="ffn_dw_in")
    dh = _mm(du, w_in, M=L, N=D, K=2 * FF, tm=min(L, 512), tn=512, tk=1408, tb=True, name="ffn_dh")
    dx2, dshift, dscale, dgp = _norm_mod_bwd(dh, x, _row(gp[s]), _row(ada[3 * s + 1]), dx)
    return dx2, dw_in, dw_out, (dshift, dscale, dgate), dgp, dgq


def _mixer_fwd(x, ada, gp, gq, wf, sm):
    L = x.shape[0]
    h = _norm_mod(x, _row(gp[1]), _row(ada[3]), _row(ada[4]))
    p = _mm(h, wf["w_in"], M=L, N=IN_COLS, K=D, tm=min(L, 512), tn=512, name="mixer_in")
    za = _conv_fwd(p, sm["conv_w"])
    y, zb = _ssm_fwd(p, sm["b_re"], sm["b_im"], sm["c_re"], sm["c_im"], sm["abr"], sm["abi"], sm["fr"], sm["fi"], sm["ssm_d"])
    zc = _pool_fwd(p, sm["w_pool"], sm["pool_scale"])
    q = _to_heads(p[:, 5 * W:6 * W]) * (HD ** -0.5)
    k = _to_heads(p[:, 6 * W:7 * W])
    v = _to_heads(p[:, 7 * W:8 * W])
    o_heads, block_sums = _sb_fwd(q, k, v)
    zd = _from_heads(o_heads).astype(BF16)
    merged = _merge_fwd(za, zb, zc, zd, p, wf["w_conv_out"], wf["w_glu"], wf["w_pool_out"], wf["w_sb_out"])
    m = _mm(merged, wf["w_out"], M=L, N=D, K=D, tm=min(L, 512), tn=512, name="mixer_out")
    x2 = _post(x, m, _row(gq[1]), _row(ada[5]), 1.0)
    return x2, (x, h, p, za, y, zb, zc, zd, q, k, v, block_sums, merged, m)


def _mixer_bwd(dx, saved, ada, gp, gq, wf, sm):
    x, h, p, za, y, zb, zc, zd, q, k, v, block_sums, merged, m = saved
    L = x.shape[0]
    dmf, dgate, dgq = _post_bwd(dx, m, _row(gq[1]), _row(ada[5]), 1.0)
    dw_out = _mm(merged, dmf, M=D, N=D, K=L, tm=512, tn=512, ta=True, out_dtype=BF16, name="mixer_dw_out")
    dmerged = _mm(dmf, wf["w_out"], M=L, N=D, K=D, tm=min(L, 512), tn=512, tb=True, name="mixer_dmerged")
    dza, dzb, dzc, dzd, dgates, dwc, dwg, dwp, dws = _merge_bwd(
        dmerged, za, zb, zc, zd, p, wf["w_conv_out"], wf["w_glu"], wf["w_pool_out"], wf["w_sb_out"])
    dconv, dconv_w = _conv_bwd(p, sm["conv_w"], dza)
    (du_ssm, dd, dbr, dbi, dcr, dci, gar, gai, gfr, gfi) = _ssm_bwd(
        p, y, dzb, sm["b_re"], sm["b_im"], sm["c_re"], sm["c_im"], sm["abr"], sm["abi"], sm["fr"], sm["fi"], sm["ssm_d"])
    du_pool, dwpool, dpscale = _pool_bwd(p, sm["w_pool"], sm["pool_scale"], dzc)
    dq, dk, dv = _sb_bwd(q, k, v, _to_heads(dzd), block_sums)
    dqkv = [_from_heads(t).astype(BF16) for t in (dq * (HD ** -0.5), dk, dv)]
    dp = jnp.concatenate([dconv, du_ssm, du_pool] + dqkv + [dgates], axis=1)
    dw_in = _mm(h, dp, M=D, N=IN_COLS, K=L, tm=512, tn=512, ta=True, out_dtype=BF16, name="mixer_dw_in")
    dh = _mm(dp, wf["w_in"], M=L, N=D, K=IN_COLS, tm=min(L, 512), tn=512, tk=1536, tb=True, name="mixer_dh")
    dx2, dshift, dscale, dgp = _norm_mod_bwd(dh, x, _row(gp[1]), _row(ada[4]), dx)
    wgrads = {"w_in": dw_in, "w_conv_out": dwc, "w_glu": dwg, "w_pool_out": dwp, "w_sb_out": dws, "w_out": dw_out}
    small = {"conv_w": dconv_w, "ssm_d": dd, "b_re": dbr, "b_im": dbi, "c_re": dcr, "c_im": dci,
             "abr": gar, "abi": gai, "fr": gfr, "fi": gfi, "w_pool": dwpool, "pool_scale": dpscale}
    return dx2, wgrads, small, (dshift, dscale, dgate), dgp, dgq


def _layer_fwd(x, ada, gp, gq, wf, sm):
    x, s0 = _ffn_fwd(x, ada, gp, gq, wf["w_ff_in"][0], wf["w_ff_out"][0], 0)
    x, s1 = _mixer_fwd(x, ada, gp, gq, wf, sm)
    x, s2 = _ffn_fwd(x, ada, gp, gq, wf["w_ff_in"][1], wf["w_ff_out"][1], 2)
    return x, (s0, s1, s2)


def _layer_bwd(dx, saved, ada, gp, gq, wf, sm):
    s0, s1, s2 = saved
    dx, dwi2, dwo2, dada2, dgp2, dgq2 = _ffn_bwd(dx, s2, ada, gp, gq, wf["w_ff_in"][1], wf["w_ff_out"][1], 2)
    dx, wgrads, small, dada1, dgp1, dgq1 = _mixer_bwd(dx, s1, ada, gp, gq, wf, sm)
    dx, dwi0, dwo0, dada0, dgp0, dgq0 = _ffn_bwd(dx, s0, ada, gp, gq, wf["w_ff_in"][0], wf["w_ff_out"][0], 0)
    wgrads["w_ff_in"] = jnp.stack([dwi0, dwi2])
    wgrads["w_ff_out"] = jnp.stack([dwo0, dwo2])
    small["d_ada"] = jnp.concatenate(list(dada0) + list(dada1) + list(dada2), axis=1).reshape(-1)
    small["g_pre"] = jnp.concatenate([dgp0, dgp1, dgp2], axis=0)
    small["g_post"] = jnp.concatenate([dgq0, dgq1, dgq2], axis=0)
    return dx, wgrads, small


def _pack(arrays):
    flat = jnp.concatenate([a.reshape(-1) for a in arrays])
    rows = -(-flat.shape[0] // 128)
    rows = -(-rows // 64) * 64
    return jnp.pad(flat, (0, rows * 128 - flat.shape[0])).reshape(rows, 128)


def _unpack(block, shapes):
    flat = block.reshape(-1)
    out, off = [], 0
    for s in shapes:
        n = int(np.prod(s))
        out.append(flat[off:off + n].reshape(s))
        off += n
    return out


def _pad_rows(a, mult):
    rows = -(-a.shape[0] // mult) * mult
    return jnp.concatenate([a] * (-(-rows // a.shape[0])), axis=0)[:rows]


SMALL_ORDER = ("d_ada", "g_pre", "g_post", "conv_w", "lam_re", "lam_im", "log_dt", "ssm_b_re", "ssm_b_im",
               "ssm_c_re", "ssm_c_im", "ssm_d", "w_pool", "pool_scale")
WEIGHTS = ('w_ada', 'b_ada', 'g_pre', 'g_post', 'w_ff_in', 'w_ff_out', 'w_in', 'conv_w', 'w_conv_out', 'lam_re', 'lam_im',
           'log_dt', 'ssm_b_re', 'ssm_b_im', 'ssm_c_re', 'ssm_c_im', 'ssm_d', 'w_glu', 'w_pool', 'pool_scale', 'w_pool_out',
           'w_sb_out', 'w_out')


def _step(a):
    depth = a["w_ada"].shape[0]
    x = a["x"][0]
    target = a["loss_target"][0]
    L = x.shape[0]
    ix, iy, ic = lax.axis_index("x"), lax.axis_index("y"), lax.axis_index("c")
    chip = 2 * ix + iy
    me = 4 * ix + 2 * iy + ic
    sel = jnp.stack([ic, chip]).astype(jnp.int32)
    lays = [_Lay(a[name].shape[1:], ax) for name, ax in GATHERED]

    first_shapes = [(D,), (depth, 3, W), (depth, 3, W), (depth, 3, W // N_CHIP)]
    gathered = _allgather8(_pack([a["c"], a["g_pre"], a["g_post"], a["conv_w"]]), "gather_small_inputs")
    per_dev = [_unpack(blk, first_shapes) for blk in gathered.reshape(N_DEV, -1, 128)]
    c_all = jnp.stack([d[0] for d in per_dev])
    c_pad = jnp.concatenate([c_all, jnp.zeros_like(c_all)], axis=0)
    g_pre = jnp.concatenate([per_dev[2 * j][1] for j in range(N_CHIP)], axis=-1)
    g_post = jnp.concatenate([per_dev[2 * j][2] for j in range(N_CHIP)], axis=-1)
    conv_w = jnp.concatenate([per_dev[2 * j][3] for j in range(N_CHIP)], axis=-1)

    b_cols = lax.dynamic_slice(a["b_ada"], (0, chip * ADA_SHARD), (depth, ADA_SHARD)).reshape(depth, 1, ADA_SHARD)
    ada_part = _ada_fwd(c_pad, a["w_ada"], b_cols)
    ada_all = _allgather8(ada_part.reshape(depth * 16, ADA_SHARD), "gather_ada").reshape(N_DEV, depth, 16, ADA_SHARD)
    ada_rows = lax.dynamic_slice(ada_all, (0, 0, me, 0), (N_DEV, depth, 1, ADA_SHARD))[:, :, 0]
    ada = jnp.concatenate([ada_rows[2 * j] for j in range(N_CHIP)], axis=-1).reshape(depth, 9, D)

    lam_re = _pad_rows(a["lam_re"].reshape(depth, NST), 8)
    lam_im = _pad_rows(a["lam_im"].reshape(depth, NST), 8)
    log_dt_x = _pad_rows(jnp.repeat(a["log_dt"], GP, axis=1), 8)
    abr, abi, fr, fi = _ssm_prep(lam_re, lam_im, log_dt_x)

    def small_of(l):
        return {"conv_w": conv_w[l], "ssm_d": _row(a["ssm_d"][l]), "pool_scale": _row(a["pool_scale"][l]),
                "b_re": _embed(jnp.transpose(a["ssm_b_re"][l], (0, 2, 1))), "b_im": _embed(jnp.transpose(a["ssm_b_im"][l], (0, 2, 1))),
                "c_re": _embed(jnp.transpose(a["ssm_c_re"][l], (0, 2, 1))), "c_im": _embed(jnp.transpose(a["ssm_c_im"][l], (0, 2, 1))),
                "w_pool": _embed(a["w_pool"][l]),
                "abr": abr[l:l + 1], "abi": abi[l:l + 1], "fr": fr[l:l + 1], "fi": fi[l:l + 1]}

    def gather_begin(l):
        placed = [_place_shard(lay, sel, a[name][l]) for (name, _), lay in zip(GATHERED, lays)]
        return _gather_start(placed, lays, str(l))

    full_w, smalls, saved = [], [], []
    pending = gather_begin(0)
    after = ada
    for l in range(depth):
        send_sems, recv_sems, bufs, _ = pending
        full = _gather_forward(_gather_wait(send_sems, recv_sems, bufs, after, lays, str(l)), lays)
        full_w.append({name: w for (name, _), w in zip(GATHERED, full)})
        smalls.append(small_of(l))
        ada_l = ada[l]
        if l + 1 < depth:
            pending = gather_begin(l + 1)
            ada_l = ada_l + pending[3][0, 0]
        x, s = _layer_fwd(x, ada_l, g_pre[l], g_post[l], full_w[l], smalls[l])
        after = x
        saved.append(s)
    dx, loss_part = _loss_head(x, target)
    loss = lax.psum(loss_part[0, 0], ("x", "y", "c"))

    shard_grads = {name: [None] * depth for name, _ in GATHERED}
    small_grads = [None] * depth
    pending = None
    for l in reversed(range(depth)):
        ada_l = ada[l] if pending is None else ada[l] + pending[1][4][0, 0]
        dx, wgrads, small = _layer_bwd(dx, saved[l], ada_l, g_pre[l], g_post[l], full_w[l], smalls[l])
        small_grads[l] = small
        if pending is not None:
            for (name, _), g in zip(GATHERED, _reduce_end(pending[1], dx, lays, sel, str(pending[0]))):
                shard_grads[name][pending[0]] = g
        pending = (l, _reduce_begin([wgrads[name] for name, _ in GATHERED], lays, sel, str(l)))

    stack = lambda key: _pad_rows(jnp.concatenate([small_grads[l][key] for l in range(depth)], axis=0), 8)
    gs = np.zeros((NST, 128), np.float32)
    gs[np.arange(NST), np.arange(NST) // GP] = 1.0
    dlr, dli, dldt = _ssm_prep_bwd(lam_re, lam_im, log_dt_x, stack("abr"), stack("abi"), stack("fr"), stack("fi"), jnp.asarray(gs))
    part = {
        "d_ada": jnp.stack([small_grads[l]["d_ada"] for l in range(depth)]),
        "g_pre": jnp.stack([small_grads[l]["g_pre"] for l in range(depth)]),
        "g_post": jnp.stack([small_grads[l]["g_post"] for l in range(depth)]),
        "conv_w": jnp.stack([small_grads[l]["conv_w"] for l in range(depth)]),
        "lam_re": dlr[:depth].reshape(depth, G, GP), "lam_im": dli[:depth].reshape(depth, G, GP), "log_dt": dldt[:depth, :G],
        "ssm_b_re": jnp.stack([jnp.transpose(_unembed(small_grads[l]["b_re"], G), (0, 2, 1)) for l in range(depth)]),
        "ssm_b_im": jnp.stack([jnp.transpose(_unembed(small_grads[l]["b_im"], G), (0, 2, 1)) for l in range(depth)]),
        "ssm_c_re": jnp.stack([jnp.transpose(_unembed(small_grads[l]["c_re"], G), (0, 2, 1)) for l in range(depth)]),
        "ssm_c_im": jnp.stack([jnp.transpose(_unembed(small_grads[l]["c_im"], G), (0, 2, 1)) for l in range(depth)]),
        "ssm_d": jnp.stack([small_grads[l]["ssm_d"][0] for l in range(depth)]),
        "w_pool": jnp.stack([_unembed(small_grads[l]["w_pool"], len(POOL_WINDOWS)) for l in range(depth)]),
        "pool_scale": jnp.stack([small_grads[l]["pool_scale"][0] for l in range(depth)]),
    }
    small_shapes = [part[k].shape for k in SMALL_ORDER]
    blocks = _allgather8(_pack([part[k] for k in SMALL_ORDER]), "gather_small_grads").reshape(N_DEV, -1, 128)
    small_sum = _sum_parts([blocks[i] for i in range(N_DEV)], F32, "small_grad_sum")
    total = dict(zip(SMALL_ORDER, _unpack(small_sum, small_shapes)))
    for (name, _), g in zip(GATHERED, _reduce_end(pending[1], small_sum, lays, sel, str(pending[0]))):
        shard_grads[name][pending[0]] = g

    grads = {name: jnp.stack(shard_grads[name]) for name, _ in GATHERED}
    d_ada_all = jnp.stack([_unpack(blocks[i], small_shapes[:1])[0] for i in range(N_DEV)])
    d_cols = lax.dynamic_slice(d_ada_all, (0, 0, chip * ADA_SHARD), (N_DEV, depth, ADA_SHARD))
    d_cols = jnp.transpose(d_cols, (1, 0, 2))
    grads["w_ada"] = _ada_wgrad(c_pad, jnp.concatenate([d_cols, jnp.zeros_like(d_cols)], axis=1))
    grads["b_ada"] = total["d_ada"]
    grads["g_pre"] = lax.dynamic_slice(total["g_pre"], (0, 0, chip * W), (depth, 3, W))
    grads["g_post"] = lax.dynamic_slice(total["g_post"], (0, 0, chip * W), (depth, 3, W))
    grads["conv_w"] = lax.dynamic_slice(total["conv_w"], (0, 0, chip * (W // N_CHIP)), (depth, 3, W // N_CHIP))
    for k in SMALL_ORDER[4:]:
        grads[k] = total[k]

    out = {"loss": loss, "grad_x": dx[None]}
    for name in WEIGHTS:
        out["grad_" + name] = grads[name]
        out["delta_" + name], out["new_m_" + name], out["new_v_" + name] = _adamw(a[name], grads[name], a["m_" + name], a["v_" + name])
    return out


def kernel(x, c, w_ada, b_ada, g_pre, g_post, w_ff_in, w_ff_out, w_in, conv_w, w_conv_out, lam_re, lam_im, log_dt, ssm_b_re, ssm_b_im, ssm_c_re, ssm_c_im, ssm_d, w_glu, w_pool, pool_scale, w_pool_out, w_sb_out, w_out, loss_target, m_w_ada, m_b_ada, m_g_pre, m_g_post, m_w_ff_in, m_w_ff_out, m_w_in, m_conv_w, m_w_conv_out, m_lam_re, m_lam_im, m_log_dt, m_ssm_b_re, m_ssm_b_im, m_ssm_c_re, m_ssm_c_im, m_ssm_d, m_w_glu, m_w_pool, m_pool_scale, m_w_pool_out, m_w_sb_out, m_w_out, v_w_ada, v_b_ada, v_g_pre, v_g_post, v_w_ff_in, v_w_ff_out, v_w_in, v_conv_w, v_w_conv_out, v_lam_re, v_lam_im, v_log_dt, v_ssm_b_re, v_ssm_b_im, v_ssm_c_re, v_ssm_c_im, v_ssm_d, v_w_glu, v_w_pool, v_pool_scale, v_w_pool_out, v_w_sb_out, v_w_out):
    out = _step(dict(locals()))
    names = ["loss", "grad_x"] + [p + n for p in ("grad_", "delta_", "new_m_", "new_v_") for n in WEIGHTS]
    return tuple(out[n] for n in names)
```

```python
import functools
import math

import jax
import jax.numpy as jnp
import numpy as np
from jax import lax
from jax.experimental import pallas as pl
from jax.experimental.pallas import tpu as pltpu

F32 = jnp.float32
BF16 = jnp.bfloat16
MESH = pl.DeviceIdType.MESH

D = 1024
W = 256
FF = 2816
IN_COLS = 6144
G = 16
GH = 16
GP = 64
NST = G * GP
QB = 128
HEADS = 4
HD = 64
EPS = 1e-6
LAMBDA_RE_MAX = -1e-4
POOL_WINDOWS = (2, 4, 8, 16)
N_CHIP = 4
N_DEV = 8
VMEM_LIMIT = 56 * 1024 * 1024
HIGH = lax.Precision.HIGHEST

ADAM_LR, ADAM_B1, ADAM_B2, ADAM_EPS, ADAM_WD, ADAM_STEP = 0.001, 0.9, 0.999, 1e-08, 0.01, 10


def _call(body, **kw):
    return pl.pallas_call(body, **kw)


def _params(dims=None, **kw):
    return pltpu.CompilerParams(dimension_semantics=dims, vmem_limit_bytes=VMEM_LIMIT, **kw)


def _sds(shape, dtype):
    return jax.ShapeDtypeStruct(shape, dtype)


def _dot(a, b, ca=1, cb=0, precision=None):
    return lax.dot_general(a, b, (((ca,), (cb,)), ((), ())), preferred_element_type=F32, precision=precision)


def _bdot(a, b, ca=1, cb=0):
    return _dot(a.astype(BF16), b.astype(BF16), ca, cb)


def _sigmoid(x):
    return 1.0 / (1.0 + jnp.exp(-x))


def _mm(a, b, *, M, N, K, tm, tn, tk=None, ta=False, tb=False, out_dtype=F32, a_off=(0, 0), b_off=(0, 0), name):
    tk = K if tk is None else tk
    nk = K // tk
    assert M % tm == 0 and N % tn == 0 and K % tk == 0

    def body(a_ref, b_ref, o_ref, *acc):
        part = _bdot(a_ref[...], b_ref[...], 0 if ta else 1, 1 if tb else 0)
        if nk == 1:
            o_ref[...] = part.astype(out_dtype)
            return
        acc_ref = acc[0]
        k = pl.program_id(2)

        @pl.when(k == 0)
        def _():
            acc_ref[...] = part

        @pl.when(k > 0)
        def _():
            acc_ref[...] += part

        @pl.when(k == nk - 1)
        def _():
            o_ref[...] = acc_ref[...].astype(out_dtype)

    if ta:
        a_spec = pl.BlockSpec((tk, tm), lambda i, j, k: (k + a_off[0], i + a_off[1]))
    else:
        a_spec = pl.BlockSpec((tm, tk), lambda i, j, k: (i + a_off[0], k + a_off[1]))
    if tb:
        b_spec = pl.BlockSpec((tn, tk), lambda i, j, k: (j + b_off[0], k + b_off[1]))
    else:
        b_spec = pl.BlockSpec((tk, tn), lambda i, j, k: (k + b_off[0], j + b_off[1]))
    return _call(
        body, name=name, grid=(M // tm, N // tn, nk),
        in_specs=[a_spec, b_spec],
        out_specs=pl.BlockSpec((tm, tn), lambda i, j, k: (i, j)),
        out_shape=_sds((M, N), out_dtype),
        scratch_shapes=[] if nk == 1 else [pltpu.VMEM((tm, tn), F32)],
        compiler_params=_params(("parallel", "parallel", "arbitrary")),
    )(a, b)


def _row_tile(L):
    return min(L, 256)


def _norm_mod(x, g, shift, scale):
    L = x.shape[0]
    tr = _row_tile(L)

    def body(x_ref, g_ref, sh_ref, sc_ref, h_ref):
        xv = x_ref[...]
        r = lax.rsqrt(jnp.mean(xv * xv, axis=-1, keepdims=True) + EPS)
        h_ref[...] = (xv * r * g_ref[...] * (1.0 + sc_ref[...]) + sh_ref[...]).astype(BF16)

    row = pl.BlockSpec((tr, D), lambda i: (i, 0))
    vec = pl.BlockSpec((1, D), lambda i: (0, 0))
    return _call(body, name="norm_mod", grid=(L // tr,), in_specs=[row, vec, vec, vec], out_specs=row,
                 out_shape=_sds((L, D), BF16), compiler_params=_params(("parallel",)))(x, g, shift, scale)


def _norm_mod_bwd(dh, x, g, scale, dx_res):
    L = x.shape[0]
    tr = _row_tile(L)

    def body(dh_ref, x_ref, g_ref, sc_ref, dxr_ref, dx_ref, dsh_ref, dsc_ref, dg_ref):
        i = pl.program_id(0)
        xv = x_ref[...]
        dhv = dh_ref[...]
        r = lax.rsqrt(jnp.mean(xv * xv, axis=-1, keepdims=True) + EPS)
        y = xv * r
        n = y * g_ref[...]
        dn = dhv * (1.0 + sc_ref[...])
        dy = dn * g_ref[...]
        dx_ref[...] = dxr_ref[...] + r * (dy - y * jnp.mean(dy * y, axis=-1, keepdims=True))

        @pl.when(i == 0)
        def _():
            dsh_ref[...] = jnp.zeros_like(dsh_ref)
            dsc_ref[...] = jnp.zeros_like(dsc_ref)
            dg_ref[...] = jnp.zeros_like(dg_ref)

        dsh_ref[...] += jnp.sum(dhv, axis=0, keepdims=True)
        dsc_ref[...] += jnp.sum(dhv * n, axis=0, keepdims=True)
        dg_ref[...] += jnp.sum(dn * y, axis=0, keepdims=True)

    row = pl.BlockSpec((tr, D), lambda i: (i, 0))
    vec = pl.BlockSpec((1, D), lambda i: (0, 0))
    return _call(body, name="norm_mod_bwd", grid=(L // tr,), in_specs=[row, row, vec, vec, row],
                 out_specs=[row, vec, vec, vec],
                 out_shape=[_sds((L, D), F32), _sds((1, D), F32), _sds((1, D), F32), _sds((1, D), F32)],
                 compiler_params=_params(("arbitrary",)))(dh, x, g, scale, dx_res)


def _post(x, f, g, gate, res_weight):
    L = x.shape[0]
    tr = _row_tile(L)

    def body(x_ref, f_ref, g_ref, gt_ref, o_ref):
        fv = f_ref[...]
        r = lax.rsqrt(jnp.mean(fv * fv, axis=-1, keepdims=True) + EPS)
        o_ref[...] = x_ref[...] + (res_weight * (1.0 + gt_ref[...])) * (fv * r * g_ref[...])

    row = pl.BlockSpec((tr, D), lambda i: (i, 0))
    vec = pl.BlockSpec((1, D), lambda i: (0, 0))
    return _call(body, name="post", grid=(L // tr,), in_specs=[row, row, vec, vec], out_specs=row,
                 out_shape=_sds((L, D), F32), compiler_params=_params(("parallel",)))(x, f, g, gate)


def _post_bwd(dx, f, g, gate, res_weight):
    L = dx.shape[0]
    tr = _row_tile(L)

    def body(dx_ref, f_ref, g_ref, gt_ref, df_ref, dgt_ref, dg_ref):
        i = pl.program_id(0)
        fv = f_ref[...]
        dxv = dx_ref[...]
        r = lax.rsqrt(jnp.mean(fv * fv, axis=-1, keepdims=True) + EPS)
        y = fv * r
        dn = dxv * (res_weight * (1.0 + gt_ref[...]))
        dy = dn * g_ref[...]
        df_ref[...] = (r * (dy - y * jnp.mean(dy * y, axis=-1, keepdims=True))).astype(BF16)

        @pl.when(i == 0)
        def _():
            dgt_ref[...] = jnp.zeros_like(dgt_ref)
            dg_ref[...] = jnp.zeros_like(dg_ref)

        dgt_ref[...] += res_weight * jnp.sum(dxv * (y * g_ref[...]), axis=0, keepdims=True)
        dg_ref[...] += jnp.sum(dn * y, axis=0, keepdims=True)

    row = pl.BlockSpec((tr, D), lambda i: (i, 0))
    vec = pl.BlockSpec((1, D), lambda i: (0, 0))
    return _call(body, name="post_bwd", grid=(L // tr,), in_specs=[row, row, vec, vec],
                 out_specs=[row, vec, vec],
                 out_shape=[_sds((L, D), BF16), _sds((1, D), F32), _sds((1, D), F32)],
                 compiler_params=_params(("arbitrary",)))(dx, f, g, gate)


def _loss_head(x, target):
    L = x.shape[0]
    tr = _row_tile(L)

    def body(x_ref, t_ref, dx_ref, loss_ref):
        i = pl.program_id(0)
        err = x_ref[...] - t_ref[...]
        dx_ref[...] = err * (1.0 / D)

        @pl.when(i == 0)
        def _():
            loss_ref[...] = jnp.zeros_like(loss_ref)

        loss_ref[...] += 0.5 * jnp.sum(jnp.mean(err * err, axis=-1, keepdims=True), axis=0, keepdims=True)

    row = pl.BlockSpec((tr, D), lambda i: (i, 0))
    return _call(body, name="loss_head", grid=(L // tr,), in_specs=[row, row],
                 out_specs=[row, pl.BlockSpec((1, 1), lambda i: (0, 0))],
                 out_shape=[_sds((L, D), F32), _sds((1, 1), F32)],
                 compiler_params=_params(("arbitrary",)))(x, target)


def _ffn_in(h, w_in):
    L = h.shape[0]
    tm, tn = min(L, 512), 256
    nf = FF // tn

    def body(h_ref, wa_ref, wb_ref, a_ref, b_ref, act_ref):
        hv = h_ref[...]
        a = _dot(hv, wa_ref[...])
        b = _dot(hv, wb_ref[...])
        a_ref[...] = a
        b_ref[...] = b
        act_ref[...] = (a * _sigmoid(a) * b).astype(BF16)

    tile = pl.BlockSpec((tm, tn), lambda i, j: (i, j))
    return _call(body, name="ffn_in", grid=(L // tm, nf),
                 in_specs=[pl.BlockSpec((tm, D), lambda i, j: (i, 0)),
                           pl.BlockSpec((D, tn), lambda i, j: (0, j)),
                           pl.BlockSpec((D, tn), lambda i, j: (0, j + nf))],
                 out_specs=[tile, tile, tile],
                 out_shape=[_sds((L, FF), F32), _sds((L, FF), F32), _sds((L, FF), BF16)],
                 compiler_params=_params(("parallel", "parallel")))(h, w_in, w_in)


def _ffn_mid_bwd(df, w_out, a, b):
    L = df.shape[0]
    tm, tn = min(L, 512), 256

    def body(df_ref, w_ref, a_ref, b_ref, da_ref, db_ref):
        dact = _dot(df_ref[...], w_ref[...], 1, 1)
        av = a_ref[...]
        sg = _sigmoid(av)
        da_ref[...] = (dact * b_ref[...] * (sg * (1.0 + av * (1.0 - sg)))).astype(BF16)
        db_ref[...] = (dact * (av * sg)).astype(BF16)

    tile = pl.BlockSpec((tm, tn), lambda i, j: (i, j))
    return _call(body, name="ffn_mid_bwd", grid=(L // tm, FF // tn),
                 in_specs=[pl.BlockSpec((tm, D), lambda i, j: (i, 0)),
                           pl.BlockSpec((tn, D), lambda i, j: (j, 0)), tile, tile],
                 out_specs=[tile, tile],
                 out_shape=[_sds((L, FF), BF16), _sds((L, FF), BF16)],
                 compiler_params=_params(("parallel", "parallel")))(df, w_out, a, b)


def _rows_before(ref, i, tr, halo):
    start = pl.multiple_of(jnp.maximum(i * tr - halo, 0), 8)
    return jnp.where(i > 0, ref[pl.ds(start, halo), :], 0.0)


def _rows_after(ref, i, n, tr, halo):
    start = pl.multiple_of(jnp.minimum((i + 1) * tr, (n - 1) * tr), 8)
    return jnp.where(i < n - 1, ref[pl.ds(start, halo), :], 0.0)


def _conv_fwd(p, conv_w):
    L = p.shape[0]
    tr = _row_tile(L)
    n = L // tr

    def body(bg_ref, cg_ref, v_ref, w_ref, za_ref, u_scr):
        i = pl.program_id(0)

        @pl.when(i == 0)
        def _():
            u_scr[...] = cg_ref[...] * v_ref[...]

        r0 = pl.multiple_of(i * tr, 8)
        ext = jnp.concatenate([_rows_before(u_scr, i, tr, 8), u_scr[pl.ds(r0, tr), :]], axis=0)
        w = w_ref[...]
        y = (w[0:1] * pltpu.roll(ext, 2, axis=0) + w[1:2] * pltpu.roll(ext, 1, axis=0) + w[2:3] * ext)[8:, :]
        za_ref[...] = (bg_ref[pl.ds(r0, tr), :] * y).astype(BF16)

    col = lambda c: pl.BlockSpec((L, W), lambda i: (0, c))
    return _call(body, name="conv_fwd", grid=(n,),
                 in_specs=[col(0), col(1), col(2), pl.BlockSpec((3, W), lambda i: (0, 0))],
                 out_specs=pl.BlockSpec((tr, W), lambda i: (i, 0)),
                 out_shape=_sds((L, W), BF16),
                 scratch_shapes=[pltpu.VMEM((L, W), F32)],
                 compiler_params=_params(("arbitrary",)))(p, p, p, conv_w)


def _conv_bwd(p, conv_w, dza):
    L = p.shape[0]
    tr = _row_tile(L)
    n = L // tr

    def body(bg_ref, cg_ref, v_ref, w_ref, dza_ref, dp_ref, dw_ref, u_scr, dy_scr):
        i = pl.program_id(0)

        @pl.when(i == 0)
        def _():
            u_scr[...] = cg_ref[...] * v_ref[...]
            dy_scr[...] = dza_ref[...] * bg_ref[...]
            dw_ref[...] = jnp.zeros_like(dw_ref)

        r0 = pl.multiple_of(i * tr, 8)
        w = w_ref[...]
        ext = jnp.concatenate([_rows_before(u_scr, i, tr, 8), u_scr[pl.ds(r0, tr), :]], axis=0)
        u2 = pltpu.roll(ext, 2, axis=0)[8:, :]
        u1 = pltpu.roll(ext, 1, axis=0)[8:, :]
        u0 = ext[8:, :]
        y = w[0:1] * u2 + w[1:2] * u1 + w[2:3] * u0
        dy = dy_scr[pl.ds(r0, tr), :]
        dext = jnp.concatenate([dy, _rows_after(dy_scr, i, n, tr, 8)], axis=0)
        m = tr + 8
        du = (w[2:3] * dext + w[1:2] * pltpu.roll(dext, m - 1, axis=0) + w[0:1] * pltpu.roll(dext, m - 2, axis=0))[:tr, :]
        dp_ref[:, 0:W] = (dza_ref[pl.ds(r0, tr), :] * y).astype(BF16)
        dp_ref[:, W:2 * W] = (du * v_ref[pl.ds(r0, tr), :]).astype(BF16)
        dp_ref[:, 2 * W:3 * W] = (du * cg_ref[pl.ds(r0, tr), :]).astype(BF16)
        dw_ref[...] += jnp.concatenate([jnp.sum(dy * u2, axis=0, keepdims=True),
                                        jnp.sum(dy * u1, axis=0, keepdims=True),
                                        jnp.sum(dy * u0, axis=0, keepdims=True)], axis=0)

    col = lambda c: pl.BlockSpec((L, W), lambda i: (0, c))
    return _call(body, name="conv_bwd", grid=(n,),
                 in_specs=[col(0), col(1), col(2), pl.BlockSpec((3, W), lambda i: (0, 0)),
                           pl.BlockSpec((L, W), lambda i: (0, 0))],
                 out_specs=[pl.BlockSpec((tr, 3 * W), lambda i: (i, 0)), pl.BlockSpec((3, W), lambda i: (0, 0))],
                 out_shape=[_sds((L, 3 * W), BF16), _sds((3, W), F32)],
                 scratch_shapes=[pltpu.VMEM((L, W), F32), pltpu.VMEM((L, W), F32)],
                 compiler_params=_params(("arbitrary",)))(p, p, p, conv_w, dza)


def _pool_windows(lane):
    wins = jnp.zeros(lane.shape, jnp.int32)
    for gi, w in enumerate(POOL_WINDOWS):
        wins = jnp.where(lane // (W // len(POOL_WINDOWS)) == gi, w, wins)
    return wins


def _pooled_block(u_ref, i, tr):
    r0 = pl.multiple_of(i * tr, 8)
    cur = u_ref[pl.ds(r0, tr), :]
    ext = jnp.concatenate([_rows_before(u_ref, i, tr, 16), cur], axis=0)
    s2 = ext + pltpu.roll(ext, 1, axis=0)
    s4 = s2 + pltpu.roll(s2, 2, axis=0)
    s8 = s4 + pltpu.roll(s4, 4, axis=0)
    s16 = s8 + pltpu.roll(s8, 8, axis=0)
    lane = lax.broadcasted_iota(jnp.int32, (tr, W), 1)
    wins = _pool_windows(lane)
    win_sum = jnp.where(wins == 2, s2[16:], jnp.where(wins == 4, s4[16:], jnp.where(wins == 8, s8[16:], s16[16:])))
    t = lax.broadcasted_iota(jnp.int32, (tr, W), 0) + i * tr
    cnt = jnp.minimum(t + 1, wins).astype(F32)
    return win_sum / cnt - cur, cnt


def _pool_fwd(p, w_pool_bd, pool_scale):
    L = p.shape[0]
    tr = _row_tile(L)

    def body(u_ref, w_ref, sc_ref, zc_ref):
        pooled, _ = _pooled_block(u_ref, pl.program_id(0), tr)
        zc_ref[...] = (_bdot(pooled, w_ref[...]) * sc_ref[...]).astype(BF16)

    return _call(body, name="pool_fwd", grid=(L // tr,),
                 in_specs=[pl.BlockSpec((L, W), lambda i: (0, 4)), pl.BlockSpec((W, W), lambda i: (0, 0)),
                           pl.BlockSpec((1, W), lambda i: (0, 0))],
                 out_specs=pl.BlockSpec((tr, W), lambda i: (i, 0)), out_shape=_sds((L, W), BF16),
                 compiler_params=_params(("arbitrary",)))(p, w_pool_bd, pool_scale)


def _pool_bwd(p, w_pool_bd, pool_scale, dzc):
    L = p.shape[0]
    tr = _row_tile(L)
    n = L // tr

    def body(u_ref, w_ref, sc_ref, dzc_ref, du_ref, dw_ref, dsc_ref, g_scr):
        i = pl.program_id(0)

        @pl.when(i == 0)
        def _():
            dw_ref[...] = jnp.zeros_like(dw_ref)
            dsc_ref[...] = jnp.zeros_like(dsc_ref)

            def rows(k, carry):
                r = pl.multiple_of(k * tr, 8)
                dmix = (dzc_ref[pl.ds(r, tr), :] * sc_ref[...]).astype(BF16)
                dpool = _dot(dmix, w_ref[...].astype(BF16), 1, 1)
                lane = lax.broadcasted_iota(jnp.int32, (tr, W), 1)
                t = lax.broadcasted_iota(jnp.int32, (tr, W), 0) + k * tr
                cnt = jnp.minimum(t + 1, _pool_windows(lane)).astype(F32)
                g_scr[pl.ds(r, tr), :] = dpool / cnt
                return carry

            lax.fori_loop(0, n, rows, 0)

        r0 = pl.multiple_of(i * tr, 8)
        pooled, cnt = _pooled_block(u_ref, i, tr)
        dzc = dzc_ref[pl.ds(r0, tr), :]
        mixed = _bdot(pooled, w_ref[...])
        dsc_ref[...] += jnp.sum(dzc * mixed, axis=0, keepdims=True)
        dmix = (dzc * sc_ref[...]).astype(BF16)
        dw_ref[...] += _dot(pooled.astype(BF16), dmix, 0, 0)
        gcur = g_scr[pl.ds(r0, tr), :]
        ext = jnp.concatenate([gcur, _rows_after(g_scr, i, n, tr, 16)], axis=0)
        m = tr + 16
        s2 = ext + pltpu.roll(ext, m - 1, axis=0)
        s4 = s2 + pltpu.roll(s2, m - 2, axis=0)
        s8 = s4 + pltpu.roll(s4, m - 4, axis=0)
        s16 = s8 + pltpu.roll(s8, m - 8, axis=0)
        lane = lax.broadcasted_iota(jnp.int32, (tr, W), 1)
        wins = _pool_windows(lane)
        ahead = jnp.where(wins == 2, s2[:tr], jnp.where(wins == 4, s4[:tr], jnp.where(wins == 8, s8[:tr], s16[:tr])))
        du_ref[...] = (ahead - gcur * cnt).astype(BF16)

    return _call(body, name="pool_bwd", grid=(n,),
                 in_specs=[pl.BlockSpec((L, W), lambda i: (0, 4)), pl.BlockSpec((W, W), lambda i: (0, 0)),
                           pl.BlockSpec((1, W), lambda i: (0, 0)), pl.BlockSpec((L, W), lambda i: (0, 0))],
                 out_specs=[pl.BlockSpec((tr, W), lambda i: (i, 0)), pl.BlockSpec((W, W), lambda i: (0, 0)),
                            pl.BlockSpec((1, W), lambda i: (0, 0))],
                 out_shape=[_sds((L, W), BF16), _sds((W, W), F32), _sds((1, W), F32)],
                 scratch_shapes=[pltpu.VMEM((L, W), F32)],
                 compiler_params=_params(("arbitrary",)))(p, w_pool_bd, pool_scale, dzc)


SSM_SLAB = 512


def _ssm_prep(lam_re, lam_im, log_dt_x):
    def body(lr_ref, li_ref, ldt_ref, abr_ref, abi_ref, fr_ref, fi_ref):
        lr = jnp.minimum(lr_ref[...], LAMBDA_RE_MAX)
        li = li_ref[...]
        dt = jnp.exp(ldt_ref[...])
        mag = jnp.exp(lr * dt)
        abr = mag * jnp.cos(li * dt)
        abi = mag * jnp.sin(li * dt)
        den = lr * lr + li * li
        nr = abr - 1.0
        abr_ref[...] = abr
        abi_ref[...] = abi
        fr_ref[...] = (nr * lr + abi * li) / den
        fi_ref[...] = (abi * lr - nr * li) / den

    shp = _sds(lam_re.shape, F32)
    return _call(body, name="ssm_prep", out_shape=[shp, shp, shp, shp], compiler_params=_params())(lam_re, lam_im, log_dt_x)


def _ssm_prep_bwd(lam_re, lam_im, log_dt_x, g_abr, g_abi, g_fr, g_fi, group_sum):
    def body(lr_ref, li_ref, ldt_ref, gar_ref, gai_ref, gfr_ref, gfi_ref, gs_ref, dlr_ref, dli_ref, dldt_ref):
        lam = lr_ref[...]
        lr = jnp.minimum(lam, LAMBDA_RE_MAX)
        li = li_ref[...]
        dt = jnp.exp(ldt_ref[...])
        mag = jnp.exp(lr * dt)
        abr = mag * jnp.cos(li * dt)
        abi = mag * jnp.sin(li * dt)
        den = lr * lr + li * li
        nr = abr - 1.0
        fr = (nr * lr + abi * li) / den
        fi = (abi * lr - nr * li) / den
        d_nre = gfr_ref[...] / den
        d_nim = gfi_ref[...] / den
        d_den = -(gfr_ref[...] * fr + gfi_ref[...] * fi) / den
        d_abr = gar_ref[...] + d_nre * lr - d_nim * li
        d_abi = gai_ref[...] + d_nre * li + d_nim * lr
        d_lr = d_nre * nr + d_nim * abi + d_den * 2.0 * lr
        d_li = d_nre * abi - d_nim * nr + d_den * 2.0 * li
        d_mag = d_abr * jnp.cos(li * dt) + d_abi * jnp.sin(li * dt)
        d_th = -d_abr * abi + d_abi * abr
        d_lr = d_lr + d_mag * mag * dt
        d_li = d_li + d_th * dt
        d_dt = d_mag * mag * lr + d_th * li
        passes = jnp.where(lam < LAMBDA_RE_MAX, 1.0, jnp.where(lam == LAMBDA_RE_MAX, 0.5, 0.0))
        dlr_ref[...] = d_lr * passes
        dli_ref[...] = d_li
        dldt_ref[...] = _dot(d_dt * dt, gs_ref[...], precision=HIGH)

    shp = _sds(lam_re.shape, F32)
    return _call(body, name="ssm_prep_bwd", out_shape=[shp, shp, _sds((lam_re.shape[0], 128), F32)],
                 compiler_params=_params())(lam_re, lam_im, log_dt_x, g_abr, g_abi, g_fr, g_fi, group_sum)


def _cmul(ar, ai, br, bi):
    return ar * br - ai * bi, ar * bi + ai * br


def _powers(ar, ai):
    out = [(ar, ai)]
    for _ in range(7):
        out.append(_cmul(out[-1][0], out[-1][1], ar, ai))
    return out


def _scan_rows(s_re, s_im, ar, ai, L, reverse=False, visit=None, visit_init=None):
    n = s_re.shape[1]
    pw = _powers(ar, ai)
    row = lax.broadcasted_iota(jnp.int32, (8, n), 0)
    dist = (8 - row) if reverse else (row + 1)
    pr = jnp.zeros((8, n), F32)
    pi = jnp.zeros((8, n), F32)
    for k in range(8):
        pr = jnp.where(dist == k + 1, pw[k][0], pr)
        pi = jnp.where(dist == k + 1, pw[k][1], pi)
    nb = L // 8

    def blk(t, carry):
        cr, ci, acc = carry
        b = (nb - 1 - t) if reverse else t
        r0 = pl.multiple_of(b * 8, 8)
        xr = s_re[pl.ds(r0, 8), :]
        xi = s_im[pl.ds(r0, 8), :]
        for d in (1, 2, 4):
            if reverse:
                keep = row < 8 - d
                sr, si = pltpu.roll(xr, 8 - d, axis=0), pltpu.roll(xi, 8 - d, axis=0)
            else:
                keep = row >= d
                sr, si = pltpu.roll(xr, d, axis=0), pltpu.roll(xi, d, axis=0)
            sr = jnp.where(keep, sr, 0.0)
            si = jnp.where(keep, si, 0.0)
            mr, mi = _cmul(pw[d - 1][0], pw[d - 1][1], sr, si)
            xr, xi = xr + mr, xi + mi
        mr, mi = _cmul(pr, pi, cr, ci)
        xr, xi = xr + mr, xi + mi
        s_re[pl.ds(r0, 8), :] = xr
        s_im[pl.ds(r0, 8), :] = xi
        if visit is not None:
            acc = visit(b, xr, xi, acc)
        if reverse:
            return xr[0:1, :], xi[0:1, :], acc
        return xr[7:8, :], xi[7:8, :], acc

    zero = jnp.zeros((1, n), F32)
    return lax.fori_loop(0, nb, blk, (zero, zero, visit_init if visit is not None else 0))[2]


def _ssm_project(u_ref, wbr, wbi, s_re, s_im, L):
    ch = min(L, 256)

    def rows(k, carry):
        r = pl.multiple_of(k * ch, 8)
        ub = u_ref[pl.ds(r, ch), :].astype(BF16)
        s_re[pl.ds(r, ch), :] = _dot(ub, wbr)
        s_im[pl.ds(r, ch), :] = _dot(ub, wbi)
        return carry

    lax.fori_loop(0, L // ch, rows, 0)


def _gelu(y):
    c = math.sqrt(2.0 / math.pi)
    return 0.5 * y * (1.0 + jnp.tanh(c * (y + 0.044715 * y * y * y)))


def _gelu_grad(y):
    c = math.sqrt(2.0 / math.pi)
    th = jnp.tanh(c * (y + 0.044715 * y * y * y))
    return 0.5 * (1.0 + th) + 0.5 * y * (1.0 - th * th) * c * (1.0 + 3.0 * 0.044715 * y * y)


def _ssm_fwd(p, b_re_bd, b_im_bd, c_re_bd, c_im_bd, abr, abi, fr, fi, d_skip):
    L = p.shape[0]
    ns = NST // SSM_SLAB
    ch = min(L, 256)

    def body(u_ref, br_ref, bi_ref, cr_ref, ci_ref, abr_ref, abi_ref, fr_ref, fi_ref, d_ref,
             y_ref, zb_ref, s_re, s_im):
        j = pl.program_id(0)
        f_re, f_im = fr_ref[...], fi_ref[...]
        wbr = (f_re * br_ref[...] - f_im * bi_ref[...]).astype(BF16)
        wbi = (f_re * bi_ref[...] + f_im * br_ref[...]).astype(BF16)
        _ssm_project(u_ref, wbr, wbi, s_re, s_im, L)
        _scan_rows(s_re, s_im, abr_ref[...], abi_ref[...], L)
        crb = cr_ref[...].astype(BF16)
        cib = ci_ref[...].astype(BF16)

        def rows(k, carry):
            r = pl.multiple_of(k * ch, 8)
            part = _dot(s_re[pl.ds(r, ch), :].astype(BF16), crb) - _dot(s_im[pl.ds(r, ch), :].astype(BF16), cib)

            @pl.when(j == 0)
            def _():
                y_ref[pl.ds(r, ch), :] = part + d_ref[...] * u_ref[pl.ds(r, ch), :]

            @pl.when(j > 0)
            def _():
                y_ref[pl.ds(r, ch), :] += part

            @pl.when(j == ns - 1)
            def _():
                zb_ref[pl.ds(r, ch), :] = _gelu(y_ref[pl.ds(r, ch), :]).astype(BF16)

            return carry

        lax.fori_loop(0, L // ch, rows, 0)

    full = lambda shape: pl.BlockSpec(shape, lambda j: (0, 0))
    lanes = pl.BlockSpec((1, SSM_SLAB), lambda j: (0, j))
    return _call(body, name="ssm_fwd", grid=(ns,),
                 in_specs=[pl.BlockSpec((L, W), lambda j: (0, 3)),
                           pl.BlockSpec((W, SSM_SLAB), lambda j: (0, j)), pl.BlockSpec((W, SSM_SLAB), lambda j: (0, j)),
                           pl.BlockSpec((SSM_SLAB, W), lambda j: (j, 0)), pl.BlockSpec((SSM_SLAB, W), lambda j: (j, 0)),
                           lanes, lanes, lanes, lanes, full((1, W))],
                 out_specs=[full((L, W)), full((L, W))],
                 out_shape=[_sds((L, W), F32), _sds((L, W), BF16)],
                 scratch_shapes=[pltpu.VMEM((L, SSM_SLAB), F32), pltpu.VMEM((L, SSM_SLAB), F32)],
                 compiler_params=_params(("arbitrary",)))(p, b_re_bd, b_im_bd, c_re_bd, c_im_bd, abr, abi, fr, fi, d_skip)


def _ssm_bwd(p, y, dzb, b_re_bd, b_im_bd, c_re_bd, c_im_bd, abr, abi, fr, fi, d_skip):
    L = p.shape[0]
    ns = NST // SSM_SLAB
    ch = min(L, 256)
    n_ch = L // ch

    def body(u_ref, y_ref, dzb_ref, br_ref, bi_ref, cr_ref, ci_ref, abr_ref, abi_ref, fr_ref, fi_ref, d_ref,
             du_ref, dd_ref, dbr_ref, dbi_ref, dcr_ref, dci_ref, gar_ref, gai_ref, gfr_ref, gfi_ref,
             s_re, s_im, l_re, l_im, dy_scr, du_scr):
        j = pl.program_id(0)
        f_re, f_im = fr_ref[...], fi_ref[...]
        b_re, b_im = br_ref[...], bi_ref[...]
        wbr = (f_re * b_re - f_im * b_im).astype(BF16)
        wbi = (f_re * b_im + f_im * b_re).astype(BF16)
        a_re, a_im = abr_ref[...], abi_ref[...]

        @pl.when(j == 0)
        def _():
            def rows(k, acc):
                r = pl.multiple_of(k * ch, 8)
                dy = dzb_ref[pl.ds(r, ch), :] * _gelu_grad(y_ref[pl.ds(r, ch), :])
                dy_scr[pl.ds(r, ch), :] = dy
                du_scr[pl.ds(r, ch), :] = d_ref[...] * dy
                return acc + jnp.sum(dy * u_ref[pl.ds(r, ch), :], axis=0, keepdims=True)

            dd_ref[...] = lax.fori_loop(0, n_ch, rows, jnp.zeros((1, W), F32))

        _ssm_project(u_ref, wbr, wbi, s_re, s_im, L)
        _scan_rows(s_re, s_im, a_re, a_im, L)
        crb = cr_ref[...].astype(BF16)
        cib = ci_ref[...].astype(BF16)

        def rows_c(k, acc):
            dcr, dci = acc
            r = pl.multiple_of(k * ch, 8)
            dyb = dy_scr[pl.ds(r, ch), :].astype(BF16)
            dcr = dcr + _dot(s_re[pl.ds(r, ch), :].astype(BF16), dyb, 0, 0)
            dci = dci - _dot(s_im[pl.ds(r, ch), :].astype(BF16), dyb, 0, 0)
            l_re[pl.ds(r, ch), :] = _dot(dyb, crb, 1, 1)
            l_im[pl.ds(r, ch), :] = -_dot(dyb, cib, 1, 1)
            return dcr, dci

        zc = jnp.zeros((SSM_SLAB, W), F32)
        dcr, dci = lax.fori_loop(0, n_ch, rows_c, (zc, zc))
        dcr_ref[...] = dcr
        dci_ref[...] = dci

        row8 = lax.broadcasted_iota(jnp.int32, (8, SSM_SLAB), 0)

        def visit(b, lr, li, acc):
            ar_acc, ai_acc = acc
            r0 = pl.multiple_of(b * 8, 8)
            rp = pl.multiple_of(jnp.maximum(b * 8 - 8, 0), 8)
            has_prev = b > 0
            pr = jnp.where(has_prev, s_re[pl.ds(rp, 8), :][7:8, :], 0.0)
            pi = jnp.where(has_prev, s_im[pl.ds(rp, 8), :][7:8, :], 0.0)
            sr = jnp.where(row8 >= 1, pltpu.roll(s_re[pl.ds(r0, 8), :], 1, axis=0), pr)
            si = jnp.where(row8 >= 1, pltpu.roll(s_im[pl.ds(r0, 8), :], 1, axis=0), pi)
            return ar_acc + lr * sr + li * si, ai_acc - lr * si + li * sr

        z8 = jnp.zeros((8, SSM_SLAB), F32)
        ar_acc, ai_acc = _scan_rows(l_re, l_im, a_re, -a_im, L, reverse=True, visit=visit, visit_init=(z8, z8))
        gar_ref[...] = jnp.sum(ar_acc, axis=0, keepdims=True)
        gai_ref[...] = jnp.sum(ai_acc, axis=0, keepdims=True)

        def rows_b(k, acc):
            dwr, dwi = acc
            r = pl.multiple_of(k * ch, 8)
            ub = u_ref[pl.ds(r, ch), :].astype(BF16)
            lrb = l_re[pl.ds(r, ch), :].astype(BF16)
            lib = l_im[pl.ds(r, ch), :].astype(BF16)
            du_scr[pl.ds(r, ch), :] += _dot(lrb, wbr, 1, 1) + _dot(lib, wbi, 1, 1)
            return dwr + _dot(ub, lrb, 0, 0), dwi + _dot(ub, lib, 0, 0)

        zb = jnp.zeros((W, SSM_SLAB), F32)
        dwr, dwi = lax.fori_loop(0, n_ch, rows_b, (zb, zb))
        dbr_ref[...] = dwr * f_re + dwi * f_im
        dbi_ref[...] = -dwr * f_im + dwi * f_re
        gfr_ref[...] = jnp.sum(dwr * b_re + dwi * b_im, axis=0, keepdims=True)
        gfi_ref[...] = jnp.sum(-dwr * b_im + dwi * b_re, axis=0, keepdims=True)

        @pl.when(j == ns - 1)
        def _():
            du_ref[...] = du_scr[...].astype(BF16)

    full = lambda shape: pl.BlockSpec(shape, lambda j: (0, 0))
    lanes = pl.BlockSpec((1, SSM_SLAB), lambda j: (0, j))
    bspec = pl.BlockSpec((W, SSM_SLAB), lambda j: (0, j))
    cspec = pl.BlockSpec((SSM_SLAB, W), lambda j: (j, 0))
    slab = lambda: pltpu.VMEM((L, SSM_SLAB), F32)
    return _call(body, name="ssm_bwd", grid=(ns,),
                 in_specs=[pl.BlockSpec((L, W), lambda j: (0, 3)), full((L, W)), full((L, W)),
                           bspec, bspec, cspec, cspec, lanes, lanes, lanes, lanes, full((1, W))],
                 out_specs=[full((L, W)), full((1, W)), bspec, bspec, cspec, cspec, lanes, lanes, lanes, lanes],
                 out_shape=[_sds((L, W), BF16), _sds((1, W), F32), _sds((W, NST), F32), _sds((W, NST), F32),
                            _sds((NST, W), F32), _sds((NST, W), F32)] + [_sds((1, NST), F32)] * 4,
                 scratch_shapes=[slab(), slab(), slab(), slab(), pltpu.VMEM((L, W), F32), pltpu.VMEM((L, W), F32)],
                 compiler_params=_params(("arbitrary",)))(p, y, dzb, b_re_bd, b_im_bd, c_re_bd, c_im_bd,
                                                          abr, abi, fr, fi, d_skip)


SB_KB = 512


def _split3(x):
    hi = x.astype(BF16)
    r1 = x - hi.astype(F32)
    mid = r1.astype(BF16)
    return hi, mid, (r1 - mid.astype(F32)).astype(BF16)


def _ones_dot(x, ones):
    n = x.shape[0]
    r = _dot(jnp.concatenate(_split3(x), axis=0), ones)
    return r[:n] + r[n:2 * n] + r[2 * n:]


def _sb_block(q, kj, i, jb, kb, right, tri):
    z = _bdot(q, kj, 1, 1)
    t_idx = lax.broadcasted_iota(jnp.int32, (QB, kb), 0) + i * QB
    s_idx = lax.broadcasted_iota(jnp.int32, (QB, kb), 1) + jb * kb
    mask = s_idx < t_idx
    lk_all = jnp.minimum(-z, 0.0) - jnp.log1p(jnp.exp(-jnp.abs(z)))
    lk = jnp.where(mask, lk_all, 0.0)
    suf = _ones_dot(lk, tri)
    a = jnp.where(mask, jnp.exp((lk_all + z) + (suf - lk) + right), 0.0)
    return z, mask, suf, a


def _sb_ones(kb):
    r = lax.broadcasted_iota(jnp.int32, (kb, kb), 0)
    c = lax.broadcasted_iota(jnp.int32, (kb, kb), 1)
    return (r >= c).astype(BF16), (r < c).astype(BF16)


def _sb_fwd(q, k, v):
    L = q.shape[1]
    kb = min(SB_KB, L)
    per = kb // QB

    def body(q_ref, k_ref, v_ref, o_ref, rs_ref):
        i = pl.program_id(0)
        tri, _ = _sb_ones(kb)
        lane = lax.broadcasted_iota(jnp.int32, (QB, 128), 1)
        qs = [q_ref[h] for h in range(HEADS)]

        def step(t, carry):
            accs, rights, sums = carry
            jb = i // per - t
            r = pl.multiple_of(jb * kb, kb)
            out = []
            for h in range(HEADS):
                _, _, suf, a = _sb_block(qs[h], k_ref[h, pl.ds(r, kb), :], i, jb, kb, rights[h], tri)
                tot = suf[:, 0:1]
                out.append((accs[h] + _bdot(a, v_ref[h, pl.ds(r, kb), :]), rights[h] + tot,
                            sums[h] + jnp.where(lane == jb, tot, 0.0)))
            return tuple(o[0] for o in out), tuple(o[1] for o in out), tuple(o[2] for o in out)

        init = (tuple(jnp.zeros((QB, HD), F32) for _ in range(HEADS)), tuple(jnp.zeros((QB, 1), F32) for _ in range(HEADS)),
                tuple(jnp.zeros((QB, 128), F32) for _ in range(HEADS)))
        accs, _, sums = lax.fori_loop(0, i // per + 1, step, init)
        for h in range(HEADS):
            o_ref[h] = accs[h]
            rs_ref[h] = sums[h]

    heads = pl.BlockSpec((HEADS, L, HD), lambda i: (0, 0, 0))
    blk = pl.BlockSpec((HEADS, QB, HD), lambda i: (0, i, 0))
    return _call(body, name="sb_fwd", grid=(L // QB,), in_specs=[blk, heads, heads],
                 out_specs=[blk, pl.BlockSpec((HEADS, QB, 128), lambda i: (0, i, 0))],
                 out_shape=[_sds((HEADS, L, HD), F32), _sds((HEADS, L, 128), F32)],
                 compiler_params=_params(("parallel",)))(q, k, v)


def _sb_bwd(q, k, v, do, block_sums):
    L = q.shape[1]
    kb = min(SB_KB, L)
    per = kb // QB

    def body(q_ref, k_ref, v_ref, do_ref, rs_ref, dq_ref, dk_ref, dv_ref):
        i = pl.program_id(0)
        tri, tri_strict = _sb_ones(kb)
        lane = lax.broadcasted_iota(jnp.int32, (QB, 128), 1)

        @pl.when(i == 0)
        def _():
            dk_ref[...] = jnp.zeros_like(dk_ref)
            dv_ref[...] = jnp.zeros_like(dv_ref)

        qs = [q_ref[h] for h in range(HEADS)]
        dos = [do_ref[h] for h in range(HEADS)]
        sums = [rs_ref[h] for h in range(HEADS)]

        def step(jb, carry):
            dqs, lefts = carry
            r = pl.multiple_of(jb * kb, kb)
            out = []
            for h in range(HEADS):
                kj = k_ref[h, pl.ds(r, kb), :]
                vj = v_ref[h, pl.ds(r, kb), :]
                right = jnp.sum(jnp.where(lane > jb, sums[h], 0.0), axis=1, keepdims=True)
                z, mask, _, a = _sb_block(qs[h], kj, i, jb, kb, right, tri)
                e = a * _bdot(dos[h], vj, 1, 1)
                dv_ref[h, pl.ds(r, kb), :] += _dot(a.astype(BF16), dos[h].astype(BF16), 0, 0)
                before = lefts[h] + _ones_dot(e, tri_strict)
                sg = _sigmoid(z)
                dz = jnp.where(mask, e * (1.0 - sg) - sg * before, 0.0).astype(BF16)
                dk_ref[h, pl.ds(r, kb), :] += _dot(dz, qs[h].astype(BF16), 0, 0)
                out.append((dqs[h] + _dot(dz, kj.astype(BF16)), lefts[h] + jnp.sum(e, axis=1, keepdims=True)))
            return tuple(o[0] for o in out), tuple(o[1] for o in out)

        init = (tuple(jnp.zeros((QB, HD), F32) for _ in range(HEADS)), tuple(jnp.zeros((QB, 1), F32) for _ in range(HEADS)))
        dqs, _ = lax.fori_loop(0, i // per + 1, step, init)
        for h in range(HEADS):
            dq_ref[h] = dqs[h]

    heads = pl.BlockSpec((HEADS, L, HD), lambda i: (0, 0, 0))
    blk = pl.BlockSpec((HEADS, QB, HD), lambda i: (0, i, 0))
    shp = _sds((HEADS, L, HD), F32)
    return _call(body, name="sb_bwd", grid=(L // QB,),
                 in_specs=[blk, heads, heads, blk, pl.BlockSpec((HEADS, QB, 128), lambda i: (0, i, 0))],
                 out_specs=[blk, heads, heads], out_shape=[shp, shp, shp],
                 compiler_params=_params(("arbitrary",)))(q, k, v, do, block_sums)


def _merge_fwd(za, zb, zc, zd, p, w_conv_out, w_glu, w_pool_out, w_sb_out):
    L = za.shape[0]
    tm = _row_tile(L)

    def body(za_ref, zb_ref, zc_ref, zd_ref, g0, g1, g2, g3, wc_ref, wg_ref, wp_ref, ws_ref, o_ref):
        glu = _dot(zb_ref[...], wg_ref[...])
        yb = glu[:, :D] * _sigmoid(glu[:, D:])
        m = _sigmoid(g0[...]) * _dot(za_ref[...], wc_ref[...])
        m = m + _sigmoid(g1[...]) * yb
        m = m + _sigmoid(g2[...]) * _dot(zc_ref[...], wp_ref[...])
        m = m + _sigmoid(g3[...]) * _dot(zd_ref[...], ws_ref[...])
        o_ref[...] = m.astype(BF16)

    zt = pl.BlockSpec((tm, W), lambda i: (i, 0))
    gate = lambda b: pl.BlockSpec((tm, D), lambda i: (i, 2 + b))
    wfull = lambda n: pl.BlockSpec((W, n), lambda i: (0, 0))
    return _call(body, name="merge_fwd", grid=(L // tm,),
                 in_specs=[zt, zt, zt, zt, gate(0), gate(1), gate(2), gate(3), wfull(D), wfull(2 * D), wfull(D), wfull(D)],
                 out_specs=pl.BlockSpec((tm, D), lambda i: (i, 0)), out_shape=_sds((L, D), BF16),
                 compiler_params=_params(("parallel",)))(za, zb, zc, zd, p, p, p, p, w_conv_out, w_glu, w_pool_out, w_sb_out)


def _merge_bwd(dm, za, zb, zc, zd, p, w_conv_out, w_glu, w_pool_out, w_sb_out):
    L = za.shape[0]
    tm = _row_tile(L)
    n = L // tm

    def body(dm_ref, za_ref, zb_ref, zc_ref, zd_ref, g0, g1, g2, g3, wc_ref, wg_ref, wp_ref, ws_ref,
             dza_ref, dzb_ref, dzc_ref, dzd_ref, dg_ref, dwc_ref, dwg_ref, dwp_ref, dws_ref,
             awc, awg, awp, aws):
        i = pl.program_id(0)

        @pl.when(i == 0)
        def _():
            awc[...] = jnp.zeros_like(awc)
            awg[...] = jnp.zeros_like(awg)
            awp[...] = jnp.zeros_like(awp)
            aws[...] = jnp.zeros_like(aws)

        dmv = dm_ref[...]

        def gated(g_ref, y, col):
            s = _sigmoid(g_ref[...])
            dg_ref[:, col * D:(col + 1) * D] = (dmv * y * s * (1.0 - s)).astype(BF16)
            return (dmv * s)

        def linear(z_ref, w_ref, acc, dz_ref, col, g_ref):
            zv = z_ref[...]
            dy = gated(g_ref, _dot(zv, w_ref[...]), col).astype(BF16)
            dz_ref[...] = _dot(dy, w_ref[...], 1, 1)
            acc[...] += _dot(zv, dy, 0, 0)

        linear(za_ref, wc_ref, awc, dza_ref, 0, g0)
        linear(zc_ref, wp_ref, awp, dzc_ref, 2, g2)
        linear(zd_ref, ws_ref, aws, dzd_ref, 3, g3)
        zbv = zb_ref[...]
        glu = _dot(zbv, wg_ref[...])
        ga = glu[:, :D]
        sg = _sigmoid(glu[:, D:])
        dyb = gated(g1, ga * sg, 1)
        dga = (dyb * sg).astype(BF16)
        dgg = (dyb * ga * sg * (1.0 - sg)).astype(BF16)
        dzb_ref[...] = _dot(dga, wg_ref[:, :D], 1, 1) + _dot(dgg, wg_ref[:, D:], 1, 1)
        awg[:, :D] += _dot(zbv, dga, 0, 0)
        awg[:, D:] += _dot(zbv, dgg, 0, 0)

        @pl.when(i == n - 1)
        def _():
            dwc_ref[...] = awc[...].astype(BF16)
            dwg_ref[...] = awg[...].astype(BF16)
            dwp_ref[...] = awp[...].astype(BF16)
            dws_ref[...] = aws[...].astype(BF16)

    zt = pl.BlockSpec((tm, W), lambda i: (i, 0))
    gate = lambda b: pl.BlockSpec((tm, D), lambda i: (i, 2 + b))
    wfull = lambda n_: pl.BlockSpec((W, n_), lambda i: (0, 0))
    zs = _sds((L, W), F32)
    return _call(body, name="merge_bwd", grid=(n,),
                 in_specs=[pl.BlockSpec((tm, D), lambda i: (i, 0)), zt, zt, zt, zt, gate(0), gate(1), gate(2), gate(3),
                           wfull(D), wfull(2 * D), wfull(D), wfull(D)],
                 out_specs=[zt, zt, zt, zt, pl.BlockSpec((tm, 4 * D), lambda i: (i, 0)),
                            wfull(D), wfull(2 * D), wfull(D), wfull(D)],
                 out_shape=[zs, zs, zs, zs, _sds((L, 4 * D), BF16),
                            _sds((W, D), BF16), _sds((W, 2 * D), BF16), _sds((W, D), BF16), _sds((W, D), BF16)],
                 scratch_shapes=[pltpu.VMEM((W, D), F32), pltpu.VMEM((W, 2 * D), F32), pltpu.VMEM((W, D), F32),
                                 pltpu.VMEM((W, D), F32)],
                 compiler_params=_params(("arbitrary",)))(dm, za, zb, zc, zd, p, p, p, p,
                                                          w_conv_out, w_glu, w_pool_out, w_sb_out)


def _adam_math(w, g, m, v):
    m2 = ADAM_B1 * m + (1.0 - ADAM_B1) * g
    v2 = ADAM_B2 * v + (1.0 - ADAM_B2) * (g * g)
    m_hat = m2 / (1.0 - ADAM_B1 ** ADAM_STEP)
    v_hat = v2 / (1.0 - ADAM_B2 ** ADAM_STEP)
    return -ADAM_LR * (m_hat / (jnp.sqrt(v_hat) + ADAM_EPS) + ADAM_WD * w), m2, v2


def _as_rows(a):
    return a.reshape(-1, a.shape[-1])


def _adamw(w, g, m, v):
    shape = w.shape
    w2, g2, m2, v2 = _as_rows(w), _as_rows(g), _as_rows(m), _as_rows(v)
    R, C = w2.shape
    tr = R
    for cand in (1024, 512, 256, 128, 64, 32, 16, 8):
        if R % cand == 0 and cand * C * 4 <= 2 * 1024 * 1024:
            tr = cand
            break

    def body(w_ref, g_ref, m_ref, v_ref, d_ref, m_out, v_out):
        d, mn, vn = _adam_math(w_ref[...], g_ref[...], m_ref[...], v_ref[...])
        d_ref[...] = d
        m_out[...] = mn
        v_out[...] = vn

    blk = pl.BlockSpec((tr, C), lambda i: (i, 0))
    shp = _sds((R, C), F32)
    outs = _call(body, name="adamw", grid=(R // tr,), in_specs=[blk] * 4, out_specs=[blk] * 3, out_shape=[shp] * 3,
                 compiler_params=_params(("parallel",)))(w2, g2, m2, v2)
    return tuple(o.reshape(shape) for o in outs)


def _sum_parts(parts, out_dtype, name):
    shape = parts[0].shape
    flat = [_as_rows(a) for a in parts]
    R, C = flat[0].shape
    tr = R
    for cand in (1024, 512, 256, 128, 64, 32, 16):
        if R % cand == 0 and cand * C * 4 <= 2 * 1024 * 1024:
            tr = cand
            break
    k = len(parts)

    def body(*refs):
        acc = refs[0][...].astype(F32)
        for r in refs[1:k]:
            acc = acc + r[...].astype(F32)
        refs[k][...] = acc.astype(out_dtype)

    blk = pl.BlockSpec((tr, C), lambda i: (i, 0))
    out = _call(body, name=name, grid=(R // tr,), in_specs=[blk] * k, out_specs=blk, out_shape=_sds((R, C), out_dtype),
                compiler_params=_params(("parallel",)))(*flat)
    return out.reshape(shape)


ADA_SHARD = 9 * D // N_CHIP
ADA_TN = 768


def _ada_fwd(c_pad, w_ada, b_ada_cols):
    depth = w_ada.shape[0]

    def body(c_ref, w_ref, b_ref, o_ref):
        cv = c_ref[...]
        o_ref[...] = _bdot(cv * _sigmoid(cv), w_ref[...]) + b_ref[...]

    return _call(body, name="ada_fwd", grid=(depth, ADA_SHARD // ADA_TN),
                 in_specs=[pl.BlockSpec((16, D), lambda l, j: (0, 0)),
                           pl.BlockSpec((None, D, ADA_TN), lambda l, j: (l, 0, j)),
                           pl.BlockSpec((None, 1, ADA_TN), lambda l, j: (l, 0, j))],
                 out_specs=pl.BlockSpec((None, 16, ADA_TN), lambda l, j: (l, 0, j)),
                 out_shape=_sds((depth, 16, ADA_SHARD), F32),
                 compiler_params=_params(("parallel", "parallel")))(c_pad, w_ada, b_ada_cols)


def _ada_wgrad(c_pad, d_ada):
    depth = d_ada.shape[0]

    def body(c_ref, d_ref, o_ref):
        cv = c_ref[...]
        o_ref[...] = _bdot(cv * _sigmoid(cv), d_ref[...], 0, 0)

    return _call(body, name="ada_wgrad", grid=(depth, ADA_SHARD // ADA_TN),
                 in_specs=[pl.BlockSpec((16, D), lambda l, j: (0, 0)),
                           pl.BlockSpec((None, 16, ADA_TN), lambda l, j: (l, 0, j))],
                 out_specs=pl.BlockSpec((None, D, ADA_TN), lambda l, j: (l, 0, j)),
                 out_shape=_sds((depth, D, ADA_SHARD), F32),
                 compiler_params=_params(("parallel", "parallel")))(c_pad, d_ada)


HBM_SPEC = pl.BlockSpec(memory_space=pltpu.HBM)


def _place():
    x, y, c = lax.axis_index("x"), lax.axis_index("y"), lax.axis_index("c")
    peers = [(1 - x, y), (x, 1 - y), (1 - x, 1 - y)]
    return x, y, c, peers


def _chip(px, py):
    return 2 * px + py


def _allgather8(block, name):
    m_per, n = block.shape

    def body(x_ref, out_ref, send_sems, recv_sems, local_sem):
        x, y, c, chips = _place()
        me, sibling = (x, y, c), (x, y, 1 - c)

        def rows(px, py, pc):
            return out_ref.at[pl.ds(pl.multiple_of((4 * px + 2 * py + pc) * m_per, 8), m_per), :]

        def copy(k, blk, to, src=None):
            return pltpu.make_async_remote_copy(
                src_ref=rows(*blk) if src is None else src, dst_ref=rows(*blk),
                send_sem=send_sems.at[k], recv_sem=recv_sems.at[k], device_id=to, device_id_type=MESH)

        mine = pltpu.make_async_copy(x_ref, rows(*me), local_sem)
        mine.start()
        first = [copy(0, me, sibling, src=x_ref)]
        first += [copy(1 + j, me, (*chip, c), src=x_ref) for j, chip in enumerate(chips)]
        for cp in first:
            cp.start()
        passed = [copy(4 + j, (*chip, c), sibling) for j, chip in enumerate(chips)]
        for j, chip in enumerate(chips):
            copy(1 + j, (*chip, c), me).wait_recv()
            passed[j].start()
        copy(0, sibling, me).wait_recv()
        for j, chip in enumerate(chips):
            copy(4 + j, (*chip, 1 - c), me).wait_recv()
        for cp in first + passed:
            cp.wait_send()
        mine.wait()

    return _call(body, name=name, out_shape=_sds((N_DEV * m_per, n), block.dtype),
                 in_specs=[pl.BlockSpec(memory_space=pltpu.VMEM)], out_specs=pl.BlockSpec(memory_space=pltpu.VMEM),
                 scratch_shapes=[pltpu.SemaphoreType.DMA((7,)), pltpu.SemaphoreType.DMA((7,)), pltpu.SemaphoreType.DMA],
                 compiler_params=_params())(block)


GATHERED = (("w_ff_in", 0, -1, 0), ("w_ff_out", 0, -2, 0),
            ("w_in", None, -1, 1), ("w_conv_out", None, -1, 1), ("w_glu", None, -1, 1), ("w_pool_out", None, -1, 1),
            ("w_sb_out", None, -1, 1), ("w_out", None, -2, 1),
            ("w_ff_in", 1, -1, 2), ("w_ff_out", 1, -2, 2))
N_SUB = 3


def _lead(ref):
    return (slice(None),) * (len(ref.shape) - 2)


def _mo(v, m):
    return v if isinstance(v, int) else pl.multiple_of(v, m)


def _full_region(ref, axis, j, half, shard_shape):
    rs, cs = shard_shape[-2], shard_shape[-1]
    if axis == -1:
        r0, nr = (0, rs) if half is None else (half * (rs // 2), rs // 2)
        return ref.at[_lead(ref) + (pl.ds(_mo(r0, 16), nr), pl.ds(_mo(j * cs, 128), cs))]
    r0, nr = (j * rs, rs) if half is None else (j * rs + half * (rs // 2), rs // 2)
    return ref.at[_lead(ref) + (pl.ds(_mo(r0, 16), nr), slice(None))]


def _shard_half(ref, half):
    rs = ref.shape[-2]
    return ref.at[_lead(ref) + (pl.ds(_mo(half * (rs // 2), 16), rs // 2), slice(None))]


def _full_shape(shard_shape, axis):
    s = list(shard_shape)
    s[axis] *= N_CHIP
    return tuple(s)


class _Lay:
    def __init__(self, shard_shape, axis):
        self.axis = axis
        self.shard_shape = tuple(shard_shape)
        self.full_shape = _full_shape(shard_shape, axis)
        self.lead = int(np.prod(shard_shape[:-2]))
        self.rs, self.cs = shard_shape[-2], shard_shape[-1]
        self.hr = self.rs // 2
        self.tr = next(t for t in (256, 128, 64, 32, 16) if self.hr % t == 0 and t * self.cs * 4 <= (1 << 20))
        self.half_rows_shape = _half_rows_shape(self.full_shape)
        self.half_shard_shape = _half_rows_shape(self.shard_shape)

    def full(self, jf, hf):
        if self.axis == -1:
            return ((self.lead, 2, self.hr, N_CHIP * self.cs),
                    pl.BlockSpec((None, None, self.tr, self.cs), lambda b, j, i, s: (b, hf(j, s), i, jf(j, s))))
        return ((self.lead, N_CHIP, 2, self.hr, self.cs),
                pl.BlockSpec((None, None, None, self.tr, self.cs), lambda b, j, i, s: (b, jf(j, s), hf(j, s), i, 0)))

    def half_rows(self, jf):
        if self.axis == -1:
            return ((self.lead, self.hr, N_CHIP * self.cs),
                    pl.BlockSpec((None, self.tr, self.cs), lambda b, j, i, s: (b, i, jf(j, s))))
        return ((self.lead, N_CHIP, self.hr, self.cs),
                pl.BlockSpec((None, None, self.tr, self.cs), lambda b, j, i, s: (b, jf(j, s), i, 0)))

    def half_shard(self):
        return (self.lead, self.hr, self.cs), pl.BlockSpec((None, self.tr, self.cs), lambda b, j, i, s: (b, i, 0))

    def shard(self, hf):
        return ((self.lead, 2, self.hr, self.cs),
                pl.BlockSpec((None, None, self.tr, self.cs), lambda b, j, i, s: (b, hf(j, s), i, 0)))


def _view_sum(sel, operands, out_view, out_shape, out_dtype, grid, name):
    k = len(operands)

    def body(sel_ref, *refs):
        acc = refs[0][...].astype(F32)
        for r in refs[1:k]:
            acc = acc + r[...].astype(F32)
        refs[k][...] = acc.astype(out_dtype)

    spec = pltpu.PrefetchScalarGridSpec(num_scalar_prefetch=1, grid=grid, in_specs=[v[1] for _, v in operands],
                                        out_specs=out_view[1])
    out = _call(body, name=name, grid_spec=spec, out_shape=_sds(out_view[0], out_dtype),
                compiler_params=_params(("parallel", "parallel", "parallel")))(
                    sel, *[a.reshape(v[0]) for a, v in operands])
    return out.reshape(out_shape)


def _sel_core(j, s):
    return s[0]


def _sel_chip(j, s):
    return s[1]


def _grid_j(j, s):
    return j


def _place_shard(lay, sel, w):
    return _view_sum(sel, [(w, lay.shard(_grid_j))], lay.full(_sel_chip, _grid_j), lay.full_shape, BF16,
                     (lay.lead, 2, lay.hr // lay.tr), "place_shard")


SEM_SPEC = pl.BlockSpec(memory_space=pltpu.SEMAPHORE)
ANY_SPEC = pl.BlockSpec(memory_space=pl.ANY)
SPLIT_COPY = pltpu.SideEffectType.DATAFLOW_SIDE_EFFECTING


def _in_hbm(a):
    return pltpu.with_memory_space_constraint(a, pltpu.HBM)


def _gather_start(fulls, after, lays, tag):
    n = len(fulls)

    def body(*refs):
        send_sems, recv_sems = refs[n + 1], refs[n + 2]
        bufs, token = refs[n + 3:2 * n + 3], refs[2 * n + 3]
        x, y, c, chips = _place()
        my = _chip(x, y)
        for a in range(n):
            own = _full_region(bufs[a], lays[a].axis, my, c, lays[a].shard_shape)
            for k, chip in enumerate(chips):
                pltpu.make_async_remote_copy(
                    src_ref=own, dst_ref=own, send_sem=send_sems.at[a * 3 + k], recv_sem=recv_sems.at[a * 3 + k],
                    device_id=(*chip, c), device_id_type=MESH).start()
        token[...] = jnp.zeros_like(token)

    outs = _call(body, name="gather_start_" + tag,
                 out_shape=[pltpu.SemaphoreType.DMA((3 * n,)), pltpu.SemaphoreType.DMA((3 * n,))]
                 + [pltpu.HBM(f.shape, f.dtype) for f in fulls] + [_sds((8, 128), F32)],
                 in_specs=[HBM_SPEC] * n + [ANY_SPEC],
                 out_specs=[SEM_SPEC, SEM_SPEC] + [HBM_SPEC] * n + [pl.BlockSpec(memory_space=pltpu.VMEM)],
                 input_output_aliases={a: a + 2 for a in range(n)},
                 compiler_params=pltpu.CompilerParams(has_side_effects=SPLIT_COPY))(*[_in_hbm(f) for f in fulls], after)
    return outs[0], outs[1], outs[2:2 + n], outs[2 + n]


def _gather_wait(send_sems, recv_sems, bufs, after, lays, tag):
    n = len(bufs)

    def body(*refs):
        ss, rs = refs[n], refs[n + 1]
        outs = refs[n + 3:]
        x, y, c, chips = _place()
        my = _chip(x, y)
        for a in range(n):
            own = _full_region(outs[a], lays[a].axis, my, c, lays[a].shard_shape)
            for k, chip in enumerate(chips):
                landed = _full_region(outs[a], lays[a].axis, _chip(*chip), c, lays[a].shard_shape)
                cp = pltpu.make_async_remote_copy(
                    src_ref=own, dst_ref=landed, send_sem=ss.at[a * 3 + k], recv_sem=rs.at[a * 3 + k],
                    device_id=(*chip, c), device_id_type=MESH)
                cp.wait_send()
                cp.wait_recv()

    return _call(body, name="gather_wait_" + tag,
                 out_shape=[pltpu.HBM(b.shape, b.dtype) for b in bufs],
                 in_specs=[HBM_SPEC] * n + [SEM_SPEC, SEM_SPEC, ANY_SPEC], out_specs=[HBM_SPEC] * n,
                 input_output_aliases={a: a for a in range(n)},
                 compiler_params=pltpu.CompilerParams(has_side_effects=SPLIT_COPY))(*bufs, send_sems, recv_sems, after)


def _gather_forward(bufs, lays):
    n = len(bufs)

    def body(*refs):
        outs = refs[n:2 * n]
        send_sems, recv_sems = refs[2 * n:]
        x, y, c, chips = _place()
        sibling = (x, y, 1 - c)
        sends = []
        for a in range(n):
            for k, chip in enumerate(chips):
                landed = _full_region(outs[a], lays[a].axis, _chip(*chip), c, lays[a].shard_shape)
                cp = pltpu.make_async_remote_copy(
                    src_ref=landed, dst_ref=landed, send_sem=send_sems.at[a * 3 + k], recv_sem=recv_sems.at[a * 3 + k],
                    device_id=sibling, device_id_type=MESH)
                cp.start()
                sends.append(cp)
        for a in range(n):
            for k, chip in enumerate(chips):
                passed = _full_region(outs[a], lays[a].axis, _chip(*chip), 1 - c, lays[a].shard_shape)
                pltpu.make_async_remote_copy(
                    src_ref=passed, dst_ref=passed, send_sem=send_sems.at[a * 3 + k], recv_sem=recv_sems.at[a * 3 + k],
                    device_id=sibling, device_id_type=MESH).wait_recv()
        for cp in sends:
            cp.wait_send()

    return _call(body, name="gather_forward",
                 out_shape=[_sds(b.shape, b.dtype) for b in bufs],
                 in_specs=[HBM_SPEC] * n, out_specs=[HBM_SPEC] * n,
                 input_output_aliases={a: a for a in range(n)},
                 scratch_shapes=[pltpu.SemaphoreType.DMA((3 * n,)), pltpu.SemaphoreType.DMA((3 * n,))],
                 compiler_params=_params())(*bufs)


def _half_rows_shape(full_shape):
    s = list(full_shape)
    s[-2] //= 2
    return tuple(s)


def _reduce_sibling(grads, lays):
    n = len(grads)

    def pieces(lay, ref_full, ref_half, half):
        if lay.axis == -1:
            src = ref_full.at[_lead(ref_full) + (pl.ds(_mo(half * lay.hr, 16), lay.hr), slice(None))]
            return [(src, ref_half)]
        out = []
        for j in range(N_CHIP):
            src = _full_region(ref_full, -2, j, half, lay.shard_shape)
            dst = ref_half.at[_lead(ref_half) + (pl.ds(j * lay.hr, lay.hr), slice(None))]
            out.append((src, dst))
        return out

    n_cp = sum(1 if lay.axis == -1 else N_CHIP for lay in lays)

    def body(*refs):
        ins, got = refs[:n], refs[n:2 * n]
        send_sems, recv_sems = refs[2 * n:]
        x, y, c, _ = _place()
        sibling = (x, y, 1 - c)
        started, idx = [], 0
        for a in range(n):
            for src, dst in pieces(lays[a], ins[a], got[a], 1 - c):
                rc = pltpu.make_async_remote_copy(src_ref=src, dst_ref=dst, send_sem=send_sems.at[idx],
                                                  recv_sem=recv_sems.at[idx], device_id=sibling, device_id_type=MESH)
                rc.start()
                started.append(rc)
                idx += 1
        for rc in started:
            rc.wait_recv()
            rc.wait_send()

    return _call(body, name="reduce_sibling",
                 out_shape=[_sds(lay.half_rows_shape, BF16) for lay in lays],
                 in_specs=[HBM_SPEC] * n, out_specs=[HBM_SPEC] * n,
                 scratch_shapes=[pltpu.SemaphoreType.DMA((n_cp,)), pltpu.SemaphoreType.DMA((n_cp,))],
                 compiler_params=_params())(*grads)


def _chip_region(lay, ref, j):
    if lay.axis == -1:
        return ref.at[_lead(ref) + (slice(None), pl.ds(_mo(j * lay.cs, 128), lay.cs))]
    return ref.at[_lead(ref) + (pl.ds(_mo(j * lay.hr, 16), lay.hr), slice(None))]


def _reduce_start(parts, lays, tag):
    n = len(parts)
    landing = [_in_hbm(lax.empty(lay.half_shard_shape, BF16)) for lay in lays for _ in range(3)]

    def body(*refs):
        send_sems, recv_sems = refs[4 * n], refs[4 * n + 1]
        src, land, token = refs[4 * n + 2:5 * n + 2], refs[5 * n + 2:8 * n + 2], refs[8 * n + 2]
        x, y, c, chips = _place()
        for a in range(n):
            for k, chip in enumerate(chips):
                pltpu.make_async_remote_copy(
                    src_ref=_chip_region(lays[a], src[a], _chip(*chip)), dst_ref=land[a * 3 + k],
                    send_sem=send_sems.at[a * 3 + k], recv_sem=recv_sems.at[a * 3 + k],
                    device_id=(*chip, c), device_id_type=MESH).start()
        token[...] = jnp.zeros_like(token)

    ops = [_in_hbm(p) for p in parts] + landing
    outs = _call(body, name="reduce_start_" + tag,
                 out_shape=[pltpu.SemaphoreType.DMA((3 * n,)), pltpu.SemaphoreType.DMA((3 * n,))]
                 + [pltpu.HBM(o.shape, o.dtype) for o in ops] + [_sds((8, 128), F32)],
                 in_specs=[HBM_SPEC] * (4 * n),
                 out_specs=[SEM_SPEC, SEM_SPEC] + [HBM_SPEC] * (4 * n) + [pl.BlockSpec(memory_space=pltpu.VMEM)],
                 input_output_aliases={a: a + 2 for a in range(4 * n)},
                 compiler_params=pltpu.CompilerParams(has_side_effects=SPLIT_COPY))(*ops)
    return outs[0], outs[1], outs[2:2 + n], outs[2 + n:2 + 4 * n], outs[2 + 4 * n]


def _reduce_wait(send_sems, recv_sems, parts, landing, after, lays, tag):
    n = len(parts)

    def body(*refs):
        ss, rs = refs[4 * n], refs[4 * n + 1]
        src, land = refs[4 * n + 3:5 * n + 3], refs[5 * n + 3:]
        x, y, c, chips = _place()
        for a in range(n):
            for k, chip in enumerate(chips):
                cp = pltpu.make_async_remote_copy(
                    src_ref=_chip_region(lays[a], src[a], _chip(*chip)), dst_ref=land[a * 3 + k],
                    send_sem=ss.at[a * 3 + k], recv_sem=rs.at[a * 3 + k], device_id=(*chip, c), device_id_type=MESH)
                cp.wait_send()
                cp.wait_recv()

    ops = list(parts) + list(landing)
    outs = _call(body, name="reduce_wait_" + tag,
                 out_shape=[pltpu.HBM(o.shape, o.dtype) for o in ops],
                 in_specs=[HBM_SPEC] * (4 * n) + [SEM_SPEC, SEM_SPEC, ANY_SPEC], out_specs=[HBM_SPEC] * (4 * n),
                 input_output_aliases={a: a for a in range(4 * n)},
                 compiler_params=pltpu.CompilerParams(has_side_effects=SPLIT_COPY))(*ops, send_sems, recv_sems, after)
    return outs[:n], [outs[n + 3 * a:n + 3 * a + 3] for a in range(n)]


def _share_halves(shards):
    n = len(shards)

    def body(*refs):
        outs = refs[n:2 * n]
        send_sems, recv_sems = refs[2 * n:]
        x, y, c, _ = _place()
        sibling = (x, y, 1 - c)
        started = []
        for a in range(n):
            mine = _shard_half(outs[a], c)
            rc = pltpu.make_async_remote_copy(src_ref=mine, dst_ref=mine, send_sem=send_sems.at[a],
                                              recv_sem=recv_sems.at[a], device_id=sibling, device_id_type=MESH)
            rc.start()
            started.append(rc)
        for rc in started:
            rc.wait_recv()
            rc.wait_send()

    return _call(body, name="share_halves", out_shape=[_sds(s.shape, F32) for s in shards],
                 in_specs=[HBM_SPEC] * n, out_specs=[HBM_SPEC] * n, input_output_aliases={a: a for a in range(n)},
                 scratch_shapes=[pltpu.SemaphoreType.DMA((n,)), pltpu.SemaphoreType.DMA((n,))],
                 compiler_params=_params())(*shards)


def _reduce_begin(grads, lays, sel, tag):
    got = _reduce_sibling(grads, lays)
    chip_parts = [
        _view_sum(sel, [(g, lay.full(_grid_j, _sel_core)), (o, lay.half_rows(_grid_j))], lay.half_rows(_grid_j),
                  lay.half_rows_shape, BF16, (lay.lead, N_CHIP, lay.hr // lay.tr), "chip_partial")
        for g, o, lay in zip(grads, got, lays)]
    return _reduce_start(chip_parts, lays, tag)


def _reduce_end(state, after, lays, sel, tag):
    send_sems, recv_sems, chip_parts, landing, _ = state
    chip_parts, landed = _reduce_wait(send_sems, recv_sems, chip_parts, landing, after, lays, tag)
    halves = [
        _view_sum(sel, [(t, lay.half_rows(_sel_chip))] + [(l, lay.half_shard()) for l in ls], lay.shard(_sel_core),
                  lay.shard_shape, F32, (lay.lead, 1, lay.hr // lay.tr), "shard_half_sum")
        for t, ls, lay in zip(chip_parts, landed, lays)]
    return _share_halves(halves)


def _embed(blocks):
    n, r, c = blocks.shape
    eye = jnp.eye(n, dtype=blocks.dtype)
    return (blocks[:, :, None, :] * eye[:, None, :, None]).reshape(n * r, n * c)


def _unembed(mat, n):
    r, c = mat.shape[0] // n, mat.shape[1] // n
    return jnp.transpose(jnp.diagonal(mat.reshape(n, r, n, c), axis1=0, axis2=2), (2, 0, 1))


def _to_heads(a):
    return jnp.transpose(a.reshape(a.shape[0], HEADS, HD), (1, 0, 2))


def _from_heads(a):
    return jnp.transpose(a, (1, 0, 2)).reshape(a.shape[1], W)


def _row(v):
    return v.reshape(1, -1)


def _ffn_fwd(x, ada, gp, gq, w_in, w_out, s):
    L = x.shape[0]
    h = _norm_mod(x, _row(gp[s]), _row(ada[3 * s]), _row(ada[3 * s + 1]))
    a, b, act = _ffn_in(h, w_in)
    f = _mm(act, w_out, M=L, N=D, K=FF, tm=min(L, 512), tn=512, name="ffn_out")
    x2 = _post(x, f, _row(gq[s]), _row(ada[3 * s + 2]), 0.5)
    return x2, (x, h, a, b, act, f)


def _ffn_bwd(dx, saved, ada, gp, gq, w_in, w_out, s):
    x, h, a, b, act, f = saved
    L = x.shape[0]
    df, dgate, dgq = _post_bwd(dx, f, _row(gq[s]), _row(ada[3 * s + 2]), 0.5)
    dw_out = _mm(act, df, M=FF, N=D, K=L, tm=256, tn=512, ta=True, out_dtype=BF16, name="ffn_dw_out")
    da, db = _ffn_mid_bwd(df, w_out, a, b)
    du = jnp.concatenate([da, db], axis=1)
    dw_in = _mm(h, du, M=D, N=2 * FF, K=L, tm=512, tn=512, ta=True, out_dtype=BF16, name="ffn_dw_in")
    dh = _mm(du, w_in, M=L, N=D, K=2 * FF, tm=min(L, 512), tn=512, tk=1408, tb=True, name="ffn_dh")
    dx2, dshift, dscale, dgp = _norm_mod_bwd(dh, x, _row(gp[s]), _row(ada[3 * s + 1]), dx)
    return dx2, dw_in, dw_out, (dshift, dscale, dgate), dgp, dgq


def _mixer_fwd(x, ada, gp, gq, wf, sm):
    L = x.shape[0]
    h = _norm_mod(x, _row(gp[1]), _row(ada[3]), _row(ada[4]))
    p = _mm(h, wf["w_in"], M=L, N=IN_COLS, K=D, tm=min(L, 512), tn=512, name="mixer_in")
    za = _conv_fwd(p, sm["conv_w"])
    y, zb = _ssm_fwd(p, sm["b_re"], sm["b_im"], sm["c_re"], sm["c_im"], sm["abr"], sm["abi"], sm["fr"], sm["fi"], sm["ssm_d"])
    zc = _pool_fwd(p, sm["w_pool"], sm["pool_scale"])
    q = _to_heads(p[:, 5 * W:6 * W]) * (HD ** -0.5)
    k = _to_heads(p[:, 6 * W:7 * W])
    v = _to_heads(p[:, 7 * W:8 * W])
    o_heads, block_sums = _sb_fwd(q, k, v)
    zd = _from_heads(o_heads).astype(BF16)
    merged = _merge_fwd(za, zb, zc, zd, p, wf["w_conv_out"], wf["w_glu"], wf["w_pool_out"], wf["w_sb_out"])
    m = _mm(merged, wf["w_out"], M=L, N=D, K=D, tm=min(L, 512), tn=512, name="mixer_out")
    x2 = _post(x, m, _row(gq[1]), _row(ada[5]), 1.0)
    return x2, (x, h, p, za, y, zb, zc, zd, q, k, v, block_sums, merged, m)


def _mixer_bwd(dx, saved, ada, gp, gq, wf, sm):
    x, h, p, za, y, zb, zc, zd, q, k, v, block_sums, merged, m = saved
    L = x.shape[0]
    dmf, dgate, dgq = _post_bwd(dx, m, _row(gq[1]), _row(ada[5]), 1.0)
    dw_out = _mm(merged, dmf, M=D, N=D, K=L, tm=512, tn=512, ta=True, out_dtype=BF16, name="mixer_dw_out")
    dmerged = _mm(dmf, wf["w_out"], M=L, N=D, K=D, tm=min(L, 512), tn=512, tb=True, name="mixer_dmerged")
    dza, dzb, dzc, dzd, dgates, dwc, dwg, dwp, dws = _merge_bwd(
        dmerged, za, zb, zc, zd, p, wf["w_conv_out"], wf["w_glu"], wf["w_pool_out"], wf["w_sb_out"])
    dconv, dconv_w = _conv_bwd(p, sm["conv_w"], dza)
    (du_ssm, dd, dbr, dbi, dcr, dci, gar, gai, gfr, gfi) = _ssm_bwd(
        p, y, dzb, sm["b_re"], sm["b_im"], sm["c_re"], sm["c_im"], sm["abr"], sm["abi"], sm["fr"], sm["fi"], sm["ssm_d"])
    du_pool, dwpool, dpscale = _pool_bwd(p, sm["w_pool"], sm["pool_scale"], dzc)
    dq, dk, dv = _sb_bwd(q, k, v, _to_heads(dzd), block_sums)
    dqkv = [_from_heads(t).astype(BF16) for t in (dq * (HD ** -0.5), dk, dv)]
    dp = jnp.concatenate([dconv, du_ssm, du_pool] + dqkv + [dgates], axis=1)
    dw_in = _mm(h, dp, M=D, N=IN_COLS, K=L, tm=512, tn=512, ta=True, out_dtype=BF16, name="mixer_dw_in")
    dh = _mm(dp, wf["w_in"], M=L, N=D, K=IN_COLS, tm=min(L, 512), tn=512, tk=1536, tb=True, name="mixer_dh")
    dx2, dshift, dscale, dgp = _norm_mod_bwd(dh, x, _row(gp[1]), _row(ada[4]), dx)
    wgrads = [dw_in, dwc, dwg, dwp, dws, dw_out]
    small = {"conv_w": dconv_w, "ssm_d": dd, "b_re": dbr, "b_im": dbi, "c_re": dcr, "c_im": dci,
             "abr": gar, "abi": gai, "fr": gfr, "fi": gfi, "w_pool": dwpool, "pool_scale": dpscale}
    return dx2, wgrads, small, (dshift, dscale, dgate), dgp, dgq


def _pack(arrays):
    flat = jnp.concatenate([a.reshape(-1) for a in arrays])
    rows = -(-flat.shape[0] // 128)
    rows = -(-rows // 64) * 64
    return jnp.pad(flat, (0, rows * 128 - flat.shape[0])).reshape(rows, 128)


def _unpack(block, shapes):
    flat = block.reshape(-1)
    out, off = [], 0
    for s in shapes:
        n = int(np.prod(s))
        out.append(flat[off:off + n].reshape(s))
        off += n
    return out


def _pad_rows(a, mult):
    rows = -(-a.shape[0] // mult) * mult
    return jnp.concatenate([a] * (-(-rows // a.shape[0])), axis=0)[:rows]


SMALL_ORDER = ("d_ada", "g_pre", "g_post", "conv_w", "lam_re", "lam_im", "log_dt", "ssm_b_re", "ssm_b_im",
               "ssm_c_re", "ssm_c_im", "ssm_d", "w_pool", "pool_scale")
WEIGHTS = ('w_ada', 'b_ada', 'g_pre', 'g_post', 'w_ff_in', 'w_ff_out', 'w_in', 'conv_w', 'w_conv_out', 'lam_re', 'lam_im',
           'log_dt', 'ssm_b_re', 'ssm_b_im', 'ssm_c_re', 'ssm_c_im', 'ssm_d', 'w_glu', 'w_pool', 'pool_scale', 'w_pool_out',
           'w_sb_out', 'w_out')


def _step(a):
    depth = a["w_ada"].shape[0]
    x = a["x"][0]
    target = a["loss_target"][0]
    L = x.shape[0]
    ix, iy, ic = lax.axis_index("x"), lax.axis_index("y"), lax.axis_index("c")
    chip = 2 * ix + iy
    me = 4 * ix + 2 * iy + ic
    sel = jnp.stack([ic, chip]).astype(jnp.int32)
    lays = [_Lay(a[name].shape[(1 if idx is None else 2):], ax) for name, idx, ax, _ in GATHERED]

    first_shapes = [(D,), (depth, 3, W), (depth, 3, W), (depth, 3, W // N_CHIP)]
    gathered = _allgather8(_pack([a["c"], a["g_pre"], a["g_post"], a["conv_w"]]), "gather_small_inputs")
    per_dev = [_unpack(blk, first_shapes) for blk in gathered.reshape(N_DEV, -1, 128)]
    c_all = jnp.stack([d[0] for d in per_dev])
    c_pad = jnp.concatenate([c_all, jnp.zeros_like(c_all)], axis=0)
    g_pre = jnp.concatenate([per_dev[2 * j][1] for j in range(N_CHIP)], axis=-1)
    g_post = jnp.concatenate([per_dev[2 * j][2] for j in range(N_CHIP)], axis=-1)
    conv_w = jnp.concatenate([per_dev[2 * j][3] for j in range(N_CHIP)], axis=-1)

    b_cols = lax.dynamic_slice(a["b_ada"], (0, chip * ADA_SHARD), (depth, ADA_SHARD)).reshape(depth, 1, ADA_SHARD)
    ada_part = _ada_fwd(c_pad, a["w_ada"], b_cols)
    ada_all = _allgather8(ada_part.reshape(depth * 16, ADA_SHARD), "gather_ada").reshape(N_DEV, depth, 16, ADA_SHARD)
    ada_rows = lax.dynamic_slice(ada_all, (0, 0, me, 0), (N_DEV, depth, 1, ADA_SHARD))[:, :, 0]
    ada = jnp.concatenate([ada_rows[2 * j] for j in range(N_CHIP)], axis=-1).reshape(depth, 9, D)

    lam_re = _pad_rows(a["lam_re"].reshape(depth, NST), 8)
    lam_im = _pad_rows(a["lam_im"].reshape(depth, NST), 8)
    log_dt_x = _pad_rows(jnp.repeat(a["log_dt"], GP, axis=1), 8)
    abr, abi, fr, fi = _ssm_prep(lam_re, lam_im, log_dt_x)

    def small_of(l):
        return {"conv_w": conv_w[l], "ssm_d": _row(a["ssm_d"][l]), "pool_scale": _row(a["pool_scale"][l]),
                "b_re": _embed(jnp.transpose(a["ssm_b_re"][l], (0, 2, 1))), "b_im": _embed(jnp.transpose(a["ssm_b_im"][l], (0, 2, 1))),
                "c_re": _embed(jnp.transpose(a["ssm_c_re"][l], (0, 2, 1))), "c_im": _embed(jnp.transpose(a["ssm_c_im"][l], (0, 2, 1))),
                "w_pool": _embed(a["w_pool"][l]),
                "abr": abr[l:l + 1], "abi": abi[l:l + 1], "fr": fr[l:l + 1], "fi": fi[l:l + 1]}

    def entries(g):
        return [i for i, e in enumerate(GATHERED) if e[3] == g]

    def shard_of(i, l):
        name, idx = GATHERED[i][0], GATHERED[i][1]
        return a[name][l] if idx is None else a[name][l, idx]

    stages = [(l, g) for l in range(depth) for g in range(N_SUB)]

    def gather_begin(t, after):
        l, g = stages[t]
        placed = [_place_shard(lays[i], sel, shard_of(i, l)) for i in entries(g)]
        return _gather_start(placed, after, [lays[i] for i in entries(g)], str(t))

    saved, weights, smalls = [], [], [small_of(l) for l in range(depth)]
    pending = {t: gather_begin(t, ada) for t in range(min(2, len(stages)))}
    for t, (l, g) in enumerate(stages):
        glays = [lays[i] for i in entries(g)]
        send_sems, recv_sems, bufs, _ = pending.pop(t)
        w = _gather_forward(_gather_wait(send_sems, recv_sems, bufs, x, glays, str(t)), glays)
        weights.append(w)
        ada_l = ada[l]
        if t + 2 < len(stages):
            pending[t + 2] = gather_begin(t + 2, x)
            ada_l = ada_l + pending[t + 2][3][0, 0]
        if g == 1:
            wf = {GATHERED[i][0]: wi for i, wi in zip(entries(1), w)}
            x, sv = _mixer_fwd(x, ada_l, g_pre[l], g_post[l], wf, smalls[l])
        else:
            x, sv = _ffn_fwd(x, ada_l, g_pre[l], g_post[l], w[0], w[1], g)
        saved.append(sv)
    dx, loss_part = _loss_head(x, target)
    loss = lax.psum(loss_part[0, 0], ("x", "y", "c"))

    shard_grads = [[None] * depth for _ in GATHERED]
    small_grads = [{} for _ in range(depth)]
    ada_grads = [[None] * N_SUB for _ in range(depth)]
    gpre_grads = [[None] * N_SUB for _ in range(depth)]
    gpost_grads = [[None] * N_SUB for _ in range(depth)]
    states = {}

    def reduce_finish(t, after):
        l, g = stages[t]
        glays = [lays[i] for i in entries(g)]
        for i, grad in zip(entries(g), _reduce_end(states.pop(t), after, glays, sel, str(t))):
            shard_grads[i][l] = grad

    token = None
    for t in reversed(range(len(stages))):
        l, g = stages[t]
        ada_l = ada[l] if token is None else ada[l] + token[0, 0]
        if g == 1:
            wf = {GATHERED[i][0]: wi for i, wi in zip(entries(1), weights[t])}
            dx, wgrads, small, dada, dgp, dgq = _mixer_bwd(dx, saved[t], ada_l, g_pre[l], g_post[l], wf, smalls[l])
            small_grads[l].update(small)
        else:
            dx, dw_in, dw_out, dada, dgp, dgq = _ffn_bwd(dx, saved[t], ada_l, g_pre[l], g_post[l], weights[t][0], weights[t][1], g)
            wgrads = [dw_in, dw_out]
        ada_grads[l][g], gpre_grads[l][g], gpost_grads[l][g] = dada, dgp, dgq
        states[t] = _reduce_begin(wgrads, [lays[i] for i in entries(g)], sel, str(t))
        token = states[t][4]
        if t + 2 in states:
            reduce_finish(t + 2, dx)
    for l in range(depth):
        small_grads[l]["d_ada"] = jnp.concatenate([p for g in range(N_SUB) for p in ada_grads[l][g]], axis=1).reshape(-1)
        small_grads[l]["g_pre"] = jnp.concatenate(gpre_grads[l], axis=0)
        small_grads[l]["g_post"] = jnp.concatenate(gpost_grads[l], axis=0)

    stack = lambda key: _pad_rows(jnp.concatenate([small_grads[l][key] for l in range(depth)], axis=0), 8)
    gs = np.zeros((NST, 128), np.float32)
    gs[np.arange(NST), np.arange(NST) // GP] = 1.0
    dlr, dli, dldt = _ssm_prep_bwd(lam_re, lam_im, log_dt_x, stack("abr"), stack("abi"), stack("fr"), stack("fi"), jnp.asarray(gs))
    part = {
        "d_ada": jnp.stack([small_grads[l]["d_ada"] for l in range(depth)]),
        "g_pre": jnp.stack([small_grads[l]["g_pre"] for l in range(depth)]),
        "g_post": jnp.stack([small_grads[l]["g_post"] for l in range(depth)]),
        "conv_w": jnp.stack([small_grads[l]["conv_w"] for l in range(depth)]),
        "lam_re": dlr[:depth].reshape(depth, G, GP), "lam_im": dli[:depth].reshape(depth, G, GP), "log_dt": dldt[:depth, :G],
        "ssm_b_re": jnp.stack([jnp.transpose(_unembed(small_grads[l]["b_re"], G), (0, 2, 1)) for l in range(depth)]),
        "ssm_b_im": jnp.stack([jnp.transpose(_unembed(small_grads[l]["b_im"], G), (0, 2, 1)) for l in range(depth)]),
        "ssm_c_re": jnp.stack([jnp.transpose(_unembed(small_grads[l]["c_re"], G), (0, 2, 1)) for l in range(depth)]),
        "ssm_c_im": jnp.stack([jnp.transpose(_unembed(small_grads[l]["c_im"], G), (0, 2, 1)) for l in range(depth)]),
        "ssm_d": jnp.stack([small_grads[l]["ssm_d"][0] for l in range(depth)]),
        "w_pool": jnp.stack([_unembed(small_grads[l]["w_pool"], len(POOL_WINDOWS)) for l in range(depth)]),
        "pool_scale": jnp.stack([small_grads[l]["pool_scale"][0] for l in range(depth)]),
    }
    small_shapes = [part[k].shape for k in SMALL_ORDER]
    blocks = _allgather8(_pack([part[k] for k in SMALL_ORDER]), "gather_small_grads").reshape(N_DEV, -1, 128)
    small_sum = _sum_parts([blocks[i] for i in range(N_DEV)], F32, "small_grad_sum")
    total = dict(zip(SMALL_ORDER, _unpack(small_sum, small_shapes)))
    for t in sorted(states, reverse=True):
        reduce_finish(t, small_sum)

    grads = {}
    for name in sorted({e[0] for e in GATHERED}):
        cols = [shard_grads[i] for i, e in enumerate(GATHERED) if e[0] == name]
        grads[name] = jnp.stack(cols[0]) if len(cols) == 1 else jnp.stack([jnp.stack(pair) for pair in zip(*cols)])
    d_ada_all = jnp.stack([_unpack(blocks[i], small_shapes[:1])[0] for i in range(N_DEV)])
    d_cols = lax.dynamic_slice(d_ada_all, (0, 0, chip * ADA_SHARD), (N_DEV, depth, ADA_SHARD))
    d_cols = jnp.transpose(d_cols, (1, 0, 2))
    grads["w_ada"] = _ada_wgrad(c_pad, jnp.concatenate([d_cols, jnp.zeros_like(d_cols)], axis=1))
    grads["b_ada"] = total["d_ada"]
    grads["g_pre"] = lax.dynamic_slice(total["g_pre"], (0, 0, chip * W), (depth, 3, W))
    grads["g_post"] = lax.dynamic_slice(total["g_post"], (0, 0, chip * W), (depth, 3, W))
    grads["conv_w"] = lax.dynamic_slice(total["conv_w"], (0, 0, chip * (W // N_CHIP)), (depth, 3, W // N_CHIP))
    for k in SMALL_ORDER[4:]:
        grads[k] = total[k]

    out = {"loss": loss, "grad_x": dx[None]}
    for name in WEIGHTS:
        out["grad_" + name] = grads[name]
        out["delta_" + name], out["new_m_" + name], out["new_v_" + name] = _adamw(a[name], grads[name], a["m_" + name], a["v_" + name])
    return out


def kernel(x, c, w_ada, b_ada, g_pre, g_post, w_ff_in, w_ff_out, w_in, conv_w, w_conv_out, lam_re, lam_im, log_dt, ssm_b_re, ssm_b_im, ssm_c_re, ssm_c_im, ssm_d, w_glu, w_pool, pool_scale, w_pool_out, w_sb_out, w_out, loss_target, m_w_ada, m_b_ada, m_g_pre, m_g_post, m_w_ff_in, m_w_ff_out, m_w_in, m_conv_w, m_w_conv_out, m_lam_re, m_lam_im, m_log_dt, m_ssm_b_re, m_ssm_b_im, m_ssm_c_re, m_ssm_c_im, m_ssm_d, m_w_glu, m_w_pool, m_pool_scale, m_w_pool_out, m_w_sb_out, m_w_out, v_w_ada, v_b_ada, v_g_pre, v_g_post, v_w_ff_in, v_w_ff_out, v_w_in, v_conv_w, v_w_conv_out, v_lam_re, v_lam_im, v_log_dt, v_ssm_b_re, v_ssm_b_im, v_ssm_c_re, v_ssm_c_im, v_ssm_d, v_w_glu, v_w_pool, v_pool_scale, v_w_pool_out, v_w_sb_out, v_w_out):
    out = _step(dict(locals()))
    names = ["loss", "grad_x"] + [p + n for p in ("grad_", "delta_", "new_m_", "new_v_") for n in WEIGHTS]
    return tuple(out[n] for n in names)
```

```python
import functools
import math

import jax
import jax.numpy as jnp
import numpy as np
from jax import lax
from jax.experimental import pallas as pl
from jax.experimental.pallas import tpu as pltpu

F32 = jnp.float32
BF16 = jnp.bfloat16
MESH = pl.DeviceIdType.MESH

D = 1024
W = 256
FF = 2816
IN_COLS = 6144
G = 16
GH = 16
GP = 64
NST = G * GP
QB = 128
HEADS = 4
HD = 64
EPS = 1e-6
LAMBDA_RE_MAX = -1e-4
POOL_WINDOWS = (2, 4, 8, 16)
N_CHIP = 4
N_DEV = 8
VMEM_LIMIT = 56 * 1024 * 1024
HIGH = lax.Precision.HIGHEST

ADAM_LR, ADAM_B1, ADAM_B2, ADAM_EPS, ADAM_WD, ADAM_STEP = 0.001, 0.9, 0.999, 1e-08, 0.01, 10


def _call(body, **kw):
    return pl.pallas_call(body, **kw)


def _params(dims=None, **kw):
    return pltpu.CompilerParams(dimension_semantics=dims, vmem_limit_bytes=VMEM_LIMIT, **kw)


def _sds(shape, dtype):
    return jax.ShapeDtypeStruct(shape, dtype)


def _dot(a, b, ca=1, cb=0, precision=None):
    return lax.dot_general(a, b, (((ca,), (cb,)), ((), ())), preferred_element_type=F32, precision=precision)


def _bdot(a, b, ca=1, cb=0):
    return _dot(a.astype(BF16), b.astype(BF16), ca, cb)


def _sigmoid(x):
    return 1.0 / (1.0 + jnp.exp(-x))


def _mm(a, b, *, M, N, K, tm, tn, tk=None, ta=False, tb=False, out_dtype=F32, a_off=(0, 0), b_off=(0, 0), name):
    tk = K if tk is None else tk
    nk = K // tk
    assert M % tm == 0 and N % tn == 0 and K % tk == 0

    def body(a_ref, b_ref, o_ref, *acc):
        part = _bdot(a_ref[...], b_ref[...], 0 if ta else 1, 1 if tb else 0)
        if nk == 1:
            o_ref[...] = part.astype(out_dtype)
            return
        acc_ref = acc[0]
        k = pl.program_id(2)

        @pl.when(k == 0)
        def _():
            acc_ref[...] = part

        @pl.when(k > 0)
        def _():
            acc_ref[...] += part

        @pl.when(k == nk - 1)
        def _():
            o_ref[...] = acc_ref[...].astype(out_dtype)

    if ta:
        a_spec = pl.BlockSpec((tk, tm), lambda i, j, k: (k + a_off[0], i + a_off[1]))
    else:
        a_spec = pl.BlockSpec((tm, tk), lambda i, j, k: (i + a_off[0], k + a_off[1]))
    if tb:
        b_spec = pl.BlockSpec((tn, tk), lambda i, j, k: (j + b_off[0], k + b_off[1]))
    else:
        b_spec = pl.BlockSpec((tk, tn), lambda i, j, k: (k + b_off[0], j + b_off[1]))
    return _call(
        body, name=name, grid=(M // tm, N // tn, nk),
        in_specs=[a_spec, b_spec],
        out_specs=pl.BlockSpec((tm, tn), lambda i, j, k: (i, j)),
        out_shape=_sds((M, N), out_dtype),
        scratch_shapes=[] if nk == 1 else [pltpu.VMEM((tm, tn), F32)],
        compiler_params=_params(("parallel", "parallel", "arbitrary")),
    )(a, b)


def _row_tile(L):
    return min(L, 256)


def _norm_mod(x, g, shift, scale):
    L = x.shape[0]
    tr = _row_tile(L)

    def body(x_ref, g_ref, sh_ref, sc_ref, h_ref):
        xv = x_ref[...]
        r = lax.rsqrt(jnp.mean(xv * xv, axis=-1, keepdims=True) + EPS)
        h_ref[...] = (xv * r * g_ref[...] * (1.0 + sc_ref[...]) + sh_ref[...]).astype(BF16)

    row = pl.BlockSpec((tr, D), lambda i: (i, 0))
    vec = pl.BlockSpec((1, D), lambda i: (0, 0))
    return _call(body, name="norm_mod", grid=(L // tr,), in_specs=[row, vec, vec, vec], out_specs=row,
                 out_shape=_sds((L, D), BF16), compiler_params=_params(("parallel",)))(x, g, shift, scale)


def _norm_mod_bwd(dh, x, g, scale, dx_res):
    L = x.shape[0]
    tr = _row_tile(L)

    def body(dh_ref, x_ref, g_ref, sc_ref, dxr_ref, dx_ref, dsh_ref, dsc_ref, dg_ref):
        i = pl.program_id(0)
        xv = x_ref[...]
        dhv = dh_ref[...]
        r = lax.rsqrt(jnp.mean(xv * xv, axis=-1, keepdims=True) + EPS)
        y = xv * r
        n = y * g_ref[...]
        dn = dhv * (1.0 + sc_ref[...])
        dy = dn * g_ref[...]
        dx_ref[...] = dxr_ref[...] + r * (dy - y * jnp.mean(dy * y, axis=-1, keepdims=True))

        @pl.when(i == 0)
        def _():
            dsh_ref[...] = jnp.zeros_like(dsh_ref)
            dsc_ref[...] = jnp.zeros_like(dsc_ref)
            dg_ref[...] = jnp.zeros_like(dg_ref)

        dsh_ref[...] += jnp.sum(dhv, axis=0, keepdims=True)
        dsc_ref[...] += jnp.sum(dhv * n, axis=0, keepdims=True)
        dg_ref[...] += jnp.sum(dn * y, axis=0, keepdims=True)

    row = pl.BlockSpec((tr, D), lambda i: (i, 0))
    vec = pl.BlockSpec((1, D), lambda i: (0, 0))
    return _call(body, name="norm_mod_bwd", grid=(L // tr,), in_specs=[row, row, vec, vec, row],
                 out_specs=[row, vec, vec, vec],
                 out_shape=[_sds((L, D), F32), _sds((1, D), F32), _sds((1, D), F32), _sds((1, D), F32)],
                 compiler_params=_params(("arbitrary",)))(dh, x, g, scale, dx_res)


def _post(x, f, g, gate, res_weight):
    L = x.shape[0]
    tr = _row_tile(L)

    def body(x_ref, f_ref, g_ref, gt_ref, o_ref):
        fv = f_ref[...]
        r = lax.rsqrt(jnp.mean(fv * fv, axis=-1, keepdims=True) + EPS)
        o_ref[...] = x_ref[...] + (res_weight * (1.0 + gt_ref[...])) * (fv * r * g_ref[...])

    row = pl.BlockSpec((tr, D), lambda i: (i, 0))
    vec = pl.BlockSpec((1, D), lambda i: (0, 0))
    return _call(body, name="post", grid=(L // tr,), in_specs=[row, row, vec, vec], out_specs=row,
                 out_shape=_sds((L, D), F32), compiler_params=_params(("parallel",)))(x, f, g, gate)


def _post_bwd(dx, f, g, gate, res_weight):
    L = dx.shape[0]
    tr = _row_tile(L)

    def body(dx_ref, f_ref, g_ref, gt_ref, df_ref, dgt_ref, dg_ref):
        i = pl.program_id(0)
        fv = f_ref[...]
        dxv = dx_ref[...]
        r = lax.rsqrt(jnp.mean(fv * fv, axis=-1, keepdims=True) + EPS)
        y = fv * r
        dn = dxv * (res_weight * (1.0 + gt_ref[...]))
        dy = dn * g_ref[...]
        df_ref[...] = (r * (dy - y * jnp.mean(dy * y, axis=-1, keepdims=True))).astype(BF16)

        @pl.when(i == 0)
        def _():
            dgt_ref[...] = jnp.zeros_like(dgt_ref)
            dg_ref[...] = jnp.zeros_like(dg_ref)

        dgt_ref[...] += res_weight * jnp.sum(dxv * (y * g_ref[...]), axis=0, keepdims=True)
        dg_ref[...] += jnp.sum(dn * y, axis=0, keepdims=True)

    row = pl.BlockSpec((tr, D), lambda i: (i, 0))
    vec = pl.BlockSpec((1, D), lambda i: (0, 0))
    return _call(body, name="post_bwd", grid=(L // tr,), in_specs=[row, row, vec, vec],
                 out_specs=[row, vec, vec],
                 out_shape=[_sds((L, D), BF16), _sds((1, D), F32), _sds((1, D), F32)],
                 compiler_params=_params(("arbitrary",)))(dx, f, g, gate)


def _loss_head(x, target):
    L = x.shape[0]
    tr = _row_tile(L)

    def body(x_ref, t_ref, dx_ref, loss_ref):
        i = pl.program_id(0)
        err = x_ref[...] - t_ref[...]
        dx_ref[...] = err * (1.0 / D)

        @pl.when(i == 0)
        def _():
            loss_ref[...] = jnp.zeros_like(loss_ref)

        loss_ref[...] += 0.5 * jnp.sum(jnp.mean(err * err, axis=-1, keepdims=True), axis=0, keepdims=True)

    row = pl.BlockSpec((tr, D), lambda i: (i, 0))
    return _call(body, name="loss_head", grid=(L // tr,), in_specs=[row, row],
                 out_specs=[row, pl.BlockSpec((1, 1), lambda i: (0, 0))],
                 out_shape=[_sds((L, D), F32), _sds((1, 1), F32)],
                 compiler_params=_params(("arbitrary",)))(x, target)


def _ffn_in(h, w_in):
    L = h.shape[0]
    tm, tn = min(L, 1024), 256
    nf = FF // tn

    def body(h_ref, wa_ref, wb_ref, a_ref, b_ref, act_ref):
        hv = h_ref[...]
        a = _dot(hv, wa_ref[...])
        b = _dot(hv, wb_ref[...])
        a_ref[...] = a
        b_ref[...] = b
        act_ref[...] = (a * _sigmoid(a) * b).astype(BF16)

    tile = pl.BlockSpec((tm, tn), lambda i, j: (i, j))
    return _call(body, name="ffn_in", grid=(L // tm, nf),
                 in_specs=[pl.BlockSpec((tm, D), lambda i, j: (i, 0)),
                           pl.BlockSpec((D, tn), lambda i, j: (0, j)),
                           pl.BlockSpec((D, tn), lambda i, j: (0, j + nf))],
                 out_specs=[tile, tile, tile],
                 out_shape=[_sds((L, FF), F32), _sds((L, FF), F32), _sds((L, FF), BF16)],
                 compiler_params=_params(("parallel", "parallel")))(h, w_in, w_in)


def _ffn_mid_bwd(df, w_out, a, b):
    L = df.shape[0]
    tm, tn = min(L, 1024), 256

    def body(df_ref, w_ref, a_ref, b_ref, da_ref, db_ref):
        dact = _dot(df_ref[...], w_ref[...], 1, 1)
        av = a_ref[...]
        sg = _sigmoid(av)
        da_ref[...] = (dact * b_ref[...] * (sg * (1.0 + av * (1.0 - sg)))).astype(BF16)
        db_ref[...] = (dact * (av * sg)).astype(BF16)

    tile = pl.BlockSpec((tm, tn), lambda i, j: (i, j))
    return _call(body, name="ffn_mid_bwd", grid=(L // tm, FF // tn),
                 in_specs=[pl.BlockSpec((tm, D), lambda i, j: (i, 0)),
                           pl.BlockSpec((tn, D), lambda i, j: (j, 0)), tile, tile],
                 out_specs=[tile, tile],
                 out_shape=[_sds((L, FF), BF16), _sds((L, FF), BF16)],
                 compiler_params=_params(("parallel", "parallel")))(df, w_out, a, b)


def _rows_before(ref, i, tr, halo):
    start = pl.multiple_of(jnp.maximum(i * tr - halo, 0), 8)
    return jnp.where(i > 0, ref[pl.ds(start, halo), :], 0.0)


def _rows_after(ref, i, n, tr, halo):
    start = pl.multiple_of(jnp.minimum((i + 1) * tr, (n - 1) * tr), 8)
    return jnp.where(i < n - 1, ref[pl.ds(start, halo), :], 0.0)


def _conv_fwd(p, conv_w):
    L = p.shape[0]
    tr = _row_tile(L)
    n = L // tr

    def body(bg_ref, cg_ref, v_ref, w_ref, za_ref, u_scr):
        i = pl.program_id(0)

        @pl.when(i == 0)
        def _():
            u_scr[...] = cg_ref[...] * v_ref[...]

        r0 = pl.multiple_of(i * tr, 8)
        ext = jnp.concatenate([_rows_before(u_scr, i, tr, 8), u_scr[pl.ds(r0, tr), :]], axis=0)
        w = w_ref[...]
        y = (w[0:1] * pltpu.roll(ext, 2, axis=0) + w[1:2] * pltpu.roll(ext, 1, axis=0) + w[2:3] * ext)[8:, :]
        za_ref[...] = (bg_ref[pl.ds(r0, tr), :] * y).astype(BF16)

    col = lambda c: pl.BlockSpec((L, W), lambda i: (0, c))
    return _call(body, name="conv_fwd", grid=(n,),
                 in_specs=[col(0), col(1), col(2), pl.BlockSpec((3, W), lambda i: (0, 0))],
                 out_specs=pl.BlockSpec((tr, W), lambda i: (i, 0)),
                 out_shape=_sds((L, W), BF16),
                 scratch_shapes=[pltpu.VMEM((L, W), F32)],
                 compiler_params=_params(("arbitrary",)))(p, p, p, conv_w)


def _conv_bwd(p, conv_w, dza):
    L = p.shape[0]
    tr = _row_tile(L)
    n = L // tr

    def body(bg_ref, cg_ref, v_ref, w_ref, dza_ref, dp_ref, dw_ref, u_scr, dy_scr):
        i = pl.program_id(0)

        @pl.when(i == 0)
        def _():
            u_scr[...] = cg_ref[...] * v_ref[...]
            dy_scr[...] = dza_ref[...] * bg_ref[...]
            dw_ref[...] = jnp.zeros_like(dw_ref)

        r0 = pl.multiple_of(i * tr, 8)
        w = w_ref[...]
        ext = jnp.concatenate([_rows_before(u_scr, i, tr, 8), u_scr[pl.ds(r0, tr), :]], axis=0)
        u2 = pltpu.roll(ext, 2, axis=0)[8:, :]
        u1 = pltpu.roll(ext, 1, axis=0)[8:, :]
        u0 = ext[8:, :]
        y = w[0:1] * u2 + w[1:2] * u1 + w[2:3] * u0
        dy = dy_scr[pl.ds(r0, tr), :]
        dext = jnp.concatenate([dy, _rows_after(dy_scr, i, n, tr, 8)], axis=0)
        m = tr + 8
        du = (w[2:3] * dext + w[1:2] * pltpu.roll(dext, m - 1, axis=0) + w[0:1] * pltpu.roll(dext, m - 2, axis=0))[:tr, :]
        dp_ref[:, 0:W] = (dza_ref[pl.ds(r0, tr), :] * y).astype(BF16)
        dp_ref[:, W:2 * W] = (du * v_ref[pl.ds(r0, tr), :]).astype(BF16)
        dp_ref[:, 2 * W:3 * W] = (du * cg_ref[pl.ds(r0, tr), :]).astype(BF16)
        dw_ref[...] += jnp.concatenate([jnp.sum(dy * u2, axis=0, keepdims=True),
                                        jnp.sum(dy * u1, axis=0, keepdims=True),
                                        jnp.sum(dy * u0, axis=0, keepdims=True)], axis=0)

    col = lambda c: pl.BlockSpec((L, W), lambda i: (0, c))
    return _call(body, name="conv_bwd", grid=(n,),
                 in_specs=[col(0), col(1), col(2), pl.BlockSpec((3, W), lambda i: (0, 0)),
                           pl.BlockSpec((L, W), lambda i: (0, 0))],
                 out_specs=[pl.BlockSpec((tr, 3 * W), lambda i: (i, 0)), pl.BlockSpec((3, W), lambda i: (0, 0))],
                 out_shape=[_sds((L, 3 * W), BF16), _sds((3, W), F32)],
                 scratch_shapes=[pltpu.VMEM((L, W), F32), pltpu.VMEM((L, W), F32)],
                 compiler_params=_params(("arbitrary",)))(p, p, p, conv_w, dza)


def _pool_windows(lane):
    wins = jnp.zeros(lane.shape, jnp.int32)
    for gi, w in enumerate(POOL_WINDOWS):
        wins = jnp.where(lane // (W // len(POOL_WINDOWS)) == gi, w, wins)
    return wins


def _pooled_block(u_ref, i, tr):
    r0 = pl.multiple_of(i * tr, 8)
    cur = u_ref[pl.ds(r0, tr), :]
    ext = jnp.concatenate([_rows_before(u_ref, i, tr, 16), cur], axis=0)
    s2 = ext + pltpu.roll(ext, 1, axis=0)
    s4 = s2 + pltpu.roll(s2, 2, axis=0)
    s8 = s4 + pltpu.roll(s4, 4, axis=0)
    s16 = s8 + pltpu.roll(s8, 8, axis=0)
    lane = lax.broadcasted_iota(jnp.int32, (tr, W), 1)
    wins = _pool_windows(lane)
    win_sum = jnp.where(wins == 2, s2[16:], jnp.where(wins == 4, s4[16:], jnp.where(wins == 8, s8[16:], s16[16:])))
    t = lax.broadcasted_iota(jnp.int32, (tr, W), 0) + i * tr
    cnt = jnp.minimum(t + 1, wins).astype(F32)
    return win_sum / cnt - cur, cnt


def _pool_fwd(p, w_pool_bd, pool_scale):
    L = p.shape[0]
    tr = _row_tile(L)

    def body(u_ref, w_ref, sc_ref, zc_ref):
        pooled, _ = _pooled_block(u_ref, pl.program_id(0), tr)
        zc_ref[...] = (_bdot(pooled, w_ref[...]) * sc_ref[...]).astype(BF16)

    return _call(body, name="pool_fwd", grid=(L // tr,),
                 in_specs=[pl.BlockSpec((L, W), lambda i: (0, 4)), pl.BlockSpec((W, W), lambda i: (0, 0)),
                           pl.BlockSpec((1, W), lambda i: (0, 0))],
                 out_specs=pl.BlockSpec((tr, W), lambda i: (i, 0)), out_shape=_sds((L, W), BF16),
                 compiler_params=_params(("arbitrary",)))(p, w_pool_bd, pool_scale)


def _pool_bwd(p, w_pool_bd, pool_scale, dzc):
    L = p.shape[0]
    tr = _row_tile(L)
    n = L // tr

    def body(u_ref, w_ref, sc_ref, dzc_ref, du_ref, dw_ref, dsc_ref, g_scr):
        i = pl.program_id(0)

        @pl.when(i == 0)
        def _():
            dw_ref[...] = jnp.zeros_like(dw_ref)
            dsc_ref[...] = jnp.zeros_like(dsc_ref)

            def rows(k, carry):
                r = pl.multiple_of(k * tr, 8)
                dmix = (dzc_ref[pl.ds(r, tr), :] * sc_ref[...]).astype(BF16)
                dpool = _dot(dmix, w_ref[...].astype(BF16), 1, 1)
                lane = lax.broadcasted_iota(jnp.int32, (tr, W), 1)
                t = lax.broadcasted_iota(jnp.int32, (tr, W), 0) + k * tr
                cnt = jnp.minimum(t + 1, _pool_windows(lane)).astype(F32)
                g_scr[pl.ds(r, tr), :] = dpool / cnt
                return carry

            lax.fori_loop(0, n, rows, 0)

        r0 = pl.multiple_of(i * tr, 8)
        pooled, cnt = _pooled_block(u_ref, i, tr)
        dzc = dzc_ref[pl.ds(r0, tr), :]
        mixed = _bdot(pooled, w_ref[...])
        dsc_ref[...] += jnp.sum(dzc * mixed, axis=0, keepdims=True)
        dmix = (dzc * sc_ref[...]).astype(BF16)
        dw_ref[...] += _dot(pooled.astype(BF16), dmix, 0, 0)
        gcur = g_scr[pl.ds(r0, tr), :]
        ext = jnp.concatenate([gcur, _rows_after(g_scr, i, n, tr, 16)], axis=0)
        m = tr + 16
        s2 = ext + pltpu.roll(ext, m - 1, axis=0)
        s4 = s2 + pltpu.roll(s2, m - 2, axis=0)
        s8 = s4 + pltpu.roll(s4, m - 4, axis=0)
        s16 = s8 + pltpu.roll(s8, m - 8, axis=0)
        lane = lax.broadcasted_iota(jnp.int32, (tr, W), 1)
        wins = _pool_windows(lane)
        ahead = jnp.where(wins == 2, s2[:tr], jnp.where(wins == 4, s4[:tr], jnp.where(wins == 8, s8[:tr], s16[:tr])))
        du_ref[...] = (ahead - gcur * cnt).astype(BF16)

    return _call(body, name="pool_bwd", grid=(n,),
                 in_specs=[pl.BlockSpec((L, W), lambda i: (0, 4)), pl.BlockSpec((W, W), lambda i: (0, 0)),
                           pl.BlockSpec((1, W), lambda i: (0, 0)), pl.BlockSpec((L, W), lambda i: (0, 0))],
                 out_specs=[pl.BlockSpec((tr, W), lambda i: (i, 0)), pl.BlockSpec((W, W), lambda i: (0, 0)),
                            pl.BlockSpec((1, W), lambda i: (0, 0))],
                 out_shape=[_sds((L, W), BF16), _sds((W, W), F32), _sds((1, W), F32)],
                 scratch_shapes=[pltpu.VMEM((L, W), F32)],
                 compiler_params=_params(("arbitrary",)))(p, w_pool_bd, pool_scale, dzc)


SSM_SLAB = 512


def _ssm_prep(lam_re, lam_im, log_dt_x):
    def body(lr_ref, li_ref, ldt_ref, abr_ref, abi_ref, fr_ref, fi_ref):
        lr = jnp.minimum(lr_ref[...], LAMBDA_RE_MAX)
        li = li_ref[...]
        dt = jnp.exp(ldt_ref[...])
        mag = jnp.exp(lr * dt)
        abr = mag * jnp.cos(li * dt)
        abi = mag * jnp.sin(li * dt)
        den = lr * lr + li * li
        nr = abr - 1.0
        abr_ref[...] = abr
        abi_ref[...] = abi
        fr_ref[...] = (nr * lr + abi * li) / den
        fi_ref[...] = (abi * lr - nr * li) / den

    shp = _sds(lam_re.shape, F32)
    return _call(body, name="ssm_prep", out_shape=[shp, shp, shp, shp], compiler_params=_params())(lam_re, lam_im, log_dt_x)


def _ssm_prep_bwd(lam_re, lam_im, log_dt_x, g_abr, g_abi, g_fr, g_fi, group_sum):
    def body(lr_ref, li_ref, ldt_ref, gar_ref, gai_ref, gfr_ref, gfi_ref, gs_ref, dlr_ref, dli_ref, dldt_ref):
        lam = lr_ref[...]
        lr = jnp.minimum(lam, LAMBDA_RE_MAX)
        li = li_ref[...]
        dt = jnp.exp(ldt_ref[...])
        mag = jnp.exp(lr * dt)
        abr = mag * jnp.cos(li * dt)
        abi = mag * jnp.sin(li * dt)
        den = lr * lr + li * li
        nr = abr - 1.0
        fr = (nr * lr + abi * li) / den
        fi = (abi * lr - nr * li) / den
        d_nre = gfr_ref[...] / den
        d_nim = gfi_ref[...] / den
        d_den = -(gfr_ref[...] * fr + gfi_ref[...] * fi) / den
        d_abr = gar_ref[...] + d_nre * lr - d_nim * li
        d_abi = gai_ref[...] + d_nre * li + d_nim * lr
        d_lr = d_nre * nr + d_nim * abi + d_den * 2.0 * lr
        d_li = d_nre * abi - d_nim * nr + d_den * 2.0 * li
        d_mag = d_abr * jnp.cos(li * dt) + d_abi * jnp.sin(li * dt)
        d_th = -d_abr * abi + d_abi * abr
        d_lr = d_lr + d_mag * mag * dt
        d_li = d_li + d_th * dt
        d_dt = d_mag * mag * lr + d_th * li
        passes = jnp.where(lam < LAMBDA_RE_MAX, 1.0, jnp.where(lam == LAMBDA_RE_MAX, 0.5, 0.0))
        dlr_ref[...] = d_lr * passes
        dli_ref[...] = d_li
        dldt_ref[...] = _dot(d_dt * dt, gs_ref[...], precision=HIGH)

    shp = _sds(lam_re.shape, F32)
    return _call(body, name="ssm_prep_bwd", out_shape=[shp, shp, _sds((lam_re.shape[0], 128), F32)],
                 compiler_params=_params())(lam_re, lam_im, log_dt_x, g_abr, g_abi, g_fr, g_fi, group_sum)


def _cmul(ar, ai, br, bi):
    return ar * br - ai * bi, ar * bi + ai * br


def _powers(ar, ai):
    out = [(ar, ai)]
    for _ in range(7):
        out.append(_cmul(out[-1][0], out[-1][1], ar, ai))
    return out


def _scan_rows(s_re, s_im, ar, ai, L, reverse=False, visit=None, visit_init=None):
    n = s_re.shape[1]
    pw = _powers(ar, ai)
    row = lax.broadcasted_iota(jnp.int32, (8, n), 0)
    dist = (8 - row) if reverse else (row + 1)
    pr = jnp.zeros((8, n), F32)
    pi = jnp.zeros((8, n), F32)
    for k in range(8):
        pr = jnp.where(dist == k + 1, pw[k][0], pr)
        pi = jnp.where(dist == k + 1, pw[k][1], pi)
    nb = L // 8

    def blk(t, carry):
        cr, ci, acc = carry
        b = (nb - 1 - t) if reverse else t
        r0 = pl.multiple_of(b * 8, 8)
        xr = s_re[pl.ds(r0, 8), :]
        xi = s_im[pl.ds(r0, 8), :]
        for d in (1, 2, 4):
            if reverse:
                keep = row < 8 - d
                sr, si = pltpu.roll(xr, 8 - d, axis=0), pltpu.roll(xi, 8 - d, axis=0)
            else:
                keep = row >= d
                sr, si = pltpu.roll(xr, d, axis=0), pltpu.roll(xi, d, axis=0)
            sr = jnp.where(keep, sr, 0.0)
            si = jnp.where(keep, si, 0.0)
            mr, mi = _cmul(pw[d - 1][0], pw[d - 1][1], sr, si)
            xr, xi = xr + mr, xi + mi
        mr, mi = _cmul(pr, pi, cr, ci)
        xr, xi = xr + mr, xi + mi
        s_re[pl.ds(r0, 8), :] = xr
        s_im[pl.ds(r0, 8), :] = xi
        if visit is not None:
            acc = visit(b, xr, xi, acc)
        if reverse:
            return xr[0:1, :], xi[0:1, :], acc
        return xr[7:8, :], xi[7:8, :], acc

    zero = jnp.zeros((1, n), F32)
    return lax.fori_loop(0, nb, blk, (zero, zero, visit_init if visit is not None else 0))[2]


def _ssm_project(u_ref, wbr, wbi, s_re, s_im, L):
    ch = min(L, 256)

    def rows(k, carry):
        r = pl.multiple_of(k * ch, 8)
        ub = u_ref[pl.ds(r, ch), :].astype(BF16)
        s_re[pl.ds(r, ch), :] = _dot(ub, wbr)
        s_im[pl.ds(r, ch), :] = _dot(ub, wbi)
        return carry

    lax.fori_loop(0, L // ch, rows, 0)


def _gelu(y):
    c = math.sqrt(2.0 / math.pi)
    return 0.5 * y * (1.0 + jnp.tanh(c * (y + 0.044715 * y * y * y)))


def _gelu_grad(y):
    c = math.sqrt(2.0 / math.pi)
    th = jnp.tanh(c * (y + 0.044715 * y * y * y))
    return 0.5 * (1.0 + th) + 0.5 * y * (1.0 - th * th) * c * (1.0 + 3.0 * 0.044715 * y * y)


def _ssm_fwd(p, b_re_bd, b_im_bd, c_re_bd, c_im_bd, abr, abi, fr, fi, d_skip):
    L = p.shape[0]
    ns = NST // SSM_SLAB
    ch = min(L, 256)

    def body(u_ref, br_ref, bi_ref, cr_ref, ci_ref, abr_ref, abi_ref, fr_ref, fi_ref, d_ref,
             y_ref, zb_ref, s_re, s_im):
        j = pl.program_id(0)
        f_re, f_im = fr_ref[...], fi_ref[...]
        wbr = (f_re * br_ref[...] - f_im * bi_ref[...]).astype(BF16)
        wbi = (f_re * bi_ref[...] + f_im * br_ref[...]).astype(BF16)
        _ssm_project(u_ref, wbr, wbi, s_re, s_im, L)
        _scan_rows(s_re, s_im, abr_ref[...], abi_ref[...], L)
        crb = cr_ref[...].astype(BF16)
        cib = ci_ref[...].astype(BF16)

        def rows(k, carry):
            r = pl.multiple_of(k * ch, 8)
            part = _dot(s_re[pl.ds(r, ch), :].astype(BF16), crb) - _dot(s_im[pl.ds(r, ch), :].astype(BF16), cib)

            @pl.when(j == 0)
            def _():
                y_ref[pl.ds(r, ch), :] = part + d_ref[...] * u_ref[pl.ds(r, ch), :]

            @pl.when(j > 0)
            def _():
                y_ref[pl.ds(r, ch), :] += part

            @pl.when(j == ns - 1)
            def _():
                zb_ref[pl.ds(r, ch), :] = _gelu(y_ref[pl.ds(r, ch), :]).astype(BF16)

            return carry

        lax.fori_loop(0, L // ch, rows, 0)

    full = lambda shape: pl.BlockSpec(shape, lambda j: (0, 0))
    lanes = pl.BlockSpec((1, SSM_SLAB), lambda j: (0, j))
    return _call(body, name="ssm_fwd", grid=(ns,),
                 in_specs=[pl.BlockSpec((L, W), lambda j: (0, 3)),
                           pl.BlockSpec((W, SSM_SLAB), lambda j: (0, j)), pl.BlockSpec((W, SSM_SLAB), lambda j: (0, j)),
                           pl.BlockSpec((SSM_SLAB, W), lambda j: (j, 0)), pl.BlockSpec((SSM_SLAB, W), lambda j: (j, 0)),
                           lanes, lanes, lanes, lanes, full((1, W))],
                 out_specs=[full((L, W)), full((L, W))],
                 out_shape=[_sds((L, W), F32), _sds((L, W), BF16)],
                 scratch_shapes=[pltpu.VMEM((L, SSM_SLAB), F32), pltpu.VMEM((L, SSM_SLAB), F32)],
                 compiler_params=_params(("arbitrary",)))(p, b_re_bd, b_im_bd, c_re_bd, c_im_bd, abr, abi, fr, fi, d_skip)


def _ssm_bwd(p, y, dzb, b_re_bd, b_im_bd, c_re_bd, c_im_bd, abr, abi, fr, fi, d_skip):
    L = p.shape[0]
    ns = NST // SSM_SLAB
    ch = min(L, 256)
    n_ch = L // ch

    def body(u_ref, y_ref, dzb_ref, br_ref, bi_ref, cr_ref, ci_ref, abr_ref, abi_ref, fr_ref, fi_ref, d_ref,
             du_ref, dd_ref, dbr_ref, dbi_ref, dcr_ref, dci_ref, gar_ref, gai_ref, gfr_ref, gfi_ref,
             s_re, s_im, l_re, l_im, dy_scr, du_scr):
        j = pl.program_id(0)
        f_re, f_im = fr_ref[...], fi_ref[...]
        b_re, b_im = br_ref[...], bi_ref[...]
        wbr = (f_re * b_re - f_im * b_im).astype(BF16)
        wbi = (f_re * b_im + f_im * b_re).astype(BF16)
        a_re, a_im = abr_ref[...], abi_ref[...]

        @pl.when(j == 0)
        def _():
            def rows(k, acc):
                r = pl.multiple_of(k * ch, 8)
                dy = dzb_ref[pl.ds(r, ch), :] * _gelu_grad(y_ref[pl.ds(r, ch), :])
                dy_scr[pl.ds(r, ch), :] = dy
                du_scr[pl.ds(r, ch), :] = d_ref[...] * dy
                return acc + jnp.sum(dy * u_ref[pl.ds(r, ch), :], axis=0, keepdims=True)

            dd_ref[...] = lax.fori_loop(0, n_ch, rows, jnp.zeros((1, W), F32))

        _ssm_project(u_ref, wbr, wbi, s_re, s_im, L)
        _scan_rows(s_re, s_im, a_re, a_im, L)
        crb = cr_ref[...].astype(BF16)
        cib = ci_ref[...].astype(BF16)

        def rows_c(k, acc):
            dcr, dci = acc
            r = pl.multiple_of(k * ch, 8)
            dyb = dy_scr[pl.ds(r, ch), :].astype(BF16)
            dcr = dcr + _dot(s_re[pl.ds(r, ch), :].astype(BF16), dyb, 0, 0)
            dci = dci - _dot(s_im[pl.ds(r, ch), :].astype(BF16), dyb, 0, 0)
            l_re[pl.ds(r, ch), :] = _dot(dyb, crb, 1, 1)
            l_im[pl.ds(r, ch), :] = -_dot(dyb, cib, 1, 1)
            return dcr, dci

        zc = jnp.zeros((SSM_SLAB, W), F32)
        dcr, dci = lax.fori_loop(0, n_ch, rows_c, (zc, zc))
        dcr_ref[...] = dcr
        dci_ref[...] = dci

        row8 = lax.broadcasted_iota(jnp.int32, (8, SSM_SLAB), 0)

        def visit(b, lr, li, acc):
            ar_acc, ai_acc = acc
            r0 = pl.multiple_of(b * 8, 8)
            rp = pl.multiple_of(jnp.maximum(b * 8 - 8, 0), 8)
            has_prev = b > 0
            pr = jnp.where(has_prev, s_re[pl.ds(rp, 8), :][7:8, :], 0.0)
            pi = jnp.where(has_prev, s_im[pl.ds(rp, 8), :][7:8, :], 0.0)
            sr = jnp.where(row8 >= 1, pltpu.roll(s_re[pl.ds(r0, 8), :], 1, axis=0), pr)
            si = jnp.where(row8 >= 1, pltpu.roll(s_im[pl.ds(r0, 8), :], 1, axis=0), pi)
            return ar_acc + lr * sr + li * si, ai_acc - lr * si + li * sr

        z8 = jnp.zeros((8, SSM_SLAB), F32)
        ar_acc, ai_acc = _scan_rows(l_re, l_im, a_re, -a_im, L, reverse=True, visit=visit, visit_init=(z8, z8))
        gar_ref[...] = jnp.sum(ar_acc, axis=0, keepdims=True)
        gai_ref[...] = jnp.sum(ai_acc, axis=0, keepdims=True)

        def rows_b(k, acc):
            dwr, dwi = acc
            r = pl.multiple_of(k * ch, 8)
            ub = u_ref[pl.ds(r, ch), :].astype(BF16)
            lrb = l_re[pl.ds(r, ch), :].astype(BF16)
            lib = l_im[pl.ds(r, ch), :].astype(BF16)
            du_scr[pl.ds(r, ch), :] += _dot(lrb, wbr, 1, 1) + _dot(lib, wbi, 1, 1)
            return dwr + _dot(ub, lrb, 0, 0), dwi + _dot(ub, lib, 0, 0)

        zb = jnp.zeros((W, SSM_SLAB), F32)
        dwr, dwi = lax.fori_loop(0, n_ch, rows_b, (zb, zb))
        dbr_ref[...] = dwr * f_re + dwi * f_im
        dbi_ref[...] = -dwr * f_im + dwi * f_re
        gfr_ref[...] = jnp.sum(dwr * b_re + dwi * b_im, axis=0, keepdims=True)
        gfi_ref[...] = jnp.sum(-dwr * b_im + dwi * b_re, axis=0, keepdims=True)

        @pl.when(j == ns - 1)
        def _():
            du_ref[...] = du_scr[...].astype(BF16)

    full = lambda shape: pl.BlockSpec(shape, lambda j: (0, 0))
    lanes = pl.BlockSpec((1, SSM_SLAB), lambda j: (0, j))
    bspec = pl.BlockSpec((W, SSM_SLAB), lambda j: (0, j))
    cspec = pl.BlockSpec((SSM_SLAB, W), lambda j: (j, 0))
    slab = lambda: pltpu.VMEM((L, SSM_SLAB), F32)
    return _call(body, name="ssm_bwd", grid=(ns,),
                 in_specs=[pl.BlockSpec((L, W), lambda j: (0, 3)), full((L, W)), full((L, W)),
                           bspec, bspec, cspec, cspec, lanes, lanes, lanes, lanes, full((1, W))],
                 out_specs=[full((L, W)), full((1, W)), bspec, bspec, cspec, cspec, lanes, lanes, lanes, lanes],
                 out_shape=[_sds((L, W), BF16), _sds((1, W), F32), _sds((W, NST), F32), _sds((W, NST), F32),
                            _sds((NST, W), F32), _sds((NST, W), F32)] + [_sds((1, NST), F32)] * 4,
                 scratch_shapes=[slab(), slab(), slab(), slab(), pltpu.VMEM((L, W), F32), pltpu.VMEM((L, W), F32)],
                 compiler_params=_params(("arbitrary",)))(p, y, dzb, b_re_bd, b_im_bd, c_re_bd, c_im_bd,
                                                          abr, abi, fr, fi, d_skip)


SB_KB = 512


SB_SUB = 256


def _split2(x):
    hi = x.astype(BF16)
    return hi, (x - hi.astype(F32)).astype(BF16)


def _ones_dot(x, ones):
    n = x.shape[0]
    r = _dot(jnp.concatenate(_split2(x), axis=0), ones)
    return r[:n] + r[n:]


def _suffix_sums(x, tri):
    sub = tri.shape[0]
    parts = [_ones_dot(x[:, i:i + sub], tri) for i in range(0, x.shape[1], sub)]
    out, after = [], None
    for p in reversed(parts):
        out.append(p if after is None else p + after)
        after = p[:, 0:1] if after is None else after + p[:, 0:1]
    return jnp.concatenate(out[::-1], axis=1)


def _prefix_sums_exclusive(x, tri_le):
    sub = tri_le.shape[0]
    out, before = [], None
    for i in range(0, x.shape[1], sub):
        xi = x[:, i:i + sub]
        inc = _ones_dot(xi, tri_le)
        out.append(inc - xi if before is None else inc - xi + before)
        before = inc[:, sub - 1:sub] if before is None else before + inc[:, sub - 1:sub]
    return jnp.concatenate(out, axis=1)


def _sb_block(q, kj, i, jb, kb, right, tri):
    z = _bdot(q, kj, 1, 1)
    t_idx = lax.broadcasted_iota(jnp.int32, (QB, kb), 0) + i * QB
    s_idx = lax.broadcasted_iota(jnp.int32, (QB, kb), 1) + jb * kb
    mask = s_idx < t_idx
    lk_all = jnp.minimum(-z, 0.0) - jnp.log1p(jnp.exp(-jnp.abs(z)))
    lk = jnp.where(mask, lk_all, 0.0)
    suf = _suffix_sums(lk, tri)
    a = jnp.where(mask, jnp.exp((lk_all + z) + (suf - lk) + right), 0.0)
    return z, mask, suf, a


def _sb_ones(kb):
    sub = min(SB_SUB, kb)
    r = lax.broadcasted_iota(jnp.int32, (sub, sub), 0)
    c = lax.broadcasted_iota(jnp.int32, (sub, sub), 1)
    return (r >= c).astype(BF16), (r <= c).astype(BF16)


def _sb_fwd(q, k, v):
    L = q.shape[1]
    kb = min(SB_KB, L)
    per = kb // QB

    def body(q_ref, k_ref, v_ref, o_ref, rs_ref):
        i = pl.program_id(0)
        tri, _ = _sb_ones(kb)
        lane = lax.broadcasted_iota(jnp.int32, (QB, 128), 1)
        qs = [q_ref[h] for h in range(HEADS)]

        def step(t, carry):
            accs, rights, sums = carry
            jb = i // per - t
            r = pl.multiple_of(jb * kb, kb)
            out = []
            for h in range(HEADS):
                _, _, suf, a = _sb_block(qs[h], k_ref[h, pl.ds(r, kb), :], i, jb, kb, rights[h], tri)
                tot = suf[:, 0:1]
                out.append((accs[h] + _bdot(a, v_ref[h, pl.ds(r, kb), :]), rights[h] + tot,
                            sums[h] + jnp.where(lane == jb, tot, 0.0)))
            return tuple(o[0] for o in out), tuple(o[1] for o in out), tuple(o[2] for o in out)

        init = (tuple(jnp.zeros((QB, HD), F32) for _ in range(HEADS)), tuple(jnp.zeros((QB, 1), F32) for _ in range(HEADS)),
                tuple(jnp.zeros((QB, 128), F32) for _ in range(HEADS)))
        accs, _, sums = lax.fori_loop(0, i // per + 1, step, init)
        for h in range(HEADS):
            o_ref[h] = accs[h]
            rs_ref[h] = sums[h]

    heads = pl.BlockSpec((HEADS, L, HD), lambda i: (0, 0, 0))
    blk = pl.BlockSpec((HEADS, QB, HD), lambda i: (0, i, 0))
    return _call(body, name="sb_fwd", grid=(L // QB,), in_specs=[blk, heads, heads],
                 out_specs=[blk, pl.BlockSpec((HEADS, QB, 128), lambda i: (0, i, 0))],
                 out_shape=[_sds((HEADS, L, HD), F32), _sds((HEADS, L, 128), F32)],
                 compiler_params=_params(("parallel",)))(q, k, v)


def _sb_bwd(q, k, v, do, block_sums):
    L = q.shape[1]
    kb = min(SB_KB, L)
    per = kb // QB

    def body(q_ref, k_ref, v_ref, do_ref, rs_ref, dq_ref, dk_ref, dv_ref):
        i = pl.program_id(0)
        tri, tri_le = _sb_ones(kb)
        lane = lax.broadcasted_iota(jnp.int32, (QB, 128), 1)

        @pl.when(i == 0)
        def _():
            dk_ref[...] = jnp.zeros_like(dk_ref)
            dv_ref[...] = jnp.zeros_like(dv_ref)

        qs = [q_ref[h] for h in range(HEADS)]
        dos = [do_ref[h] for h in range(HEADS)]
        sums = [rs_ref[h] for h in range(HEADS)]

        def step(jb, carry):
            dqs, lefts = carry
            r = pl.multiple_of(jb * kb, kb)
            out = []
            for h in range(HEADS):
                kj = k_ref[h, pl.ds(r, kb), :]
                vj = v_ref[h, pl.ds(r, kb), :]
                right = jnp.sum(jnp.where(lane > jb, sums[h], 0.0), axis=1, keepdims=True)
                z, mask, _, a = _sb_block(qs[h], kj, i, jb, kb, right, tri)
                e = a * _bdot(dos[h], vj, 1, 1)
                dv_ref[h, pl.ds(r, kb), :] += _dot(a.astype(BF16), dos[h].astype(BF16), 0, 0)
                before = lefts[h] + _prefix_sums_exclusive(e, tri_le)
                sg = _sigmoid(z)
                dz = jnp.where(mask, e * (1.0 - sg) - sg * before, 0.0).astype(BF16)
                dk_ref[h, pl.ds(r, kb), :] += _dot(dz, qs[h].astype(BF16), 0, 0)
                out.append((dqs[h] + _dot(dz, kj.astype(BF16)), lefts[h] + jnp.sum(e, axis=1, keepdims=True)))
            return tuple(o[0] for o in out), tuple(o[1] for o in out)

        init = (tuple(jnp.zeros((QB, HD), F32) for _ in range(HEADS)), tuple(jnp.zeros((QB, 1), F32) for _ in range(HEADS)))
        dqs, _ = lax.fori_loop(0, i // per + 1, step, init)
        for h in range(HEADS):
            dq_ref[h] = dqs[h]

    heads = pl.BlockSpec((HEADS, L, HD), lambda i: (0, 0, 0))
    blk = pl.BlockSpec((HEADS, QB, HD), lambda i: (0, i, 0))
    shp = _sds((HEADS, L, HD), F32)
    return _call(body, name="sb_bwd", grid=(L // QB,),
                 in_specs=[blk, heads, heads, blk, pl.BlockSpec((HEADS, QB, 128), lambda i: (0, i, 0))],
                 out_specs=[blk, heads, heads], out_shape=[shp, shp, shp],
                 compiler_params=_params(("arbitrary",)))(q, k, v, do, block_sums)


def _merge_fwd(za, zb, zc, zd, p, w_conv_out, w_glu, w_pool_out, w_sb_out):
    L = za.shape[0]
    tm = _row_tile(L)

    def body(za_ref, zb_ref, zc_ref, zd_ref, g0, g1, g2, g3, wc_ref, wg_ref, wp_ref, ws_ref, o_ref):
        glu = _dot(zb_ref[...], wg_ref[...])
        yb = glu[:, :D] * _sigmoid(glu[:, D:])
        m = _sigmoid(g0[...]) * _dot(za_ref[...], wc_ref[...])
        m = m + _sigmoid(g1[...]) * yb
        m = m + _sigmoid(g2[...]) * _dot(zc_ref[...], wp_ref[...])
        m = m + _sigmoid(g3[...]) * _dot(zd_ref[...], ws_ref[...])
        o_ref[...] = m.astype(BF16)

    zt = pl.BlockSpec((tm, W), lambda i: (i, 0))
    gate = lambda b: pl.BlockSpec((tm, D), lambda i: (i, 2 + b))
    wfull = lambda n: pl.BlockSpec((W, n), lambda i: (0, 0))
    return _call(body, name="merge_fwd", grid=(L // tm,),
                 in_specs=[zt, zt, zt, zt, gate(0), gate(1), gate(2), gate(3), wfull(D), wfull(2 * D), wfull(D), wfull(D)],
                 out_specs=pl.BlockSpec((tm, D), lambda i: (i, 0)), out_shape=_sds((L, D), BF16),
                 compiler_params=_params(("parallel",)))(za, zb, zc, zd, p, p, p, p, w_conv_out, w_glu, w_pool_out, w_sb_out)


def _merge_bwd(dm, za, zb, zc, zd, p, w_conv_out, w_glu, w_pool_out, w_sb_out):
    L = za.shape[0]
    tm = _row_tile(L)
    n = L // tm

    def body(dm_ref, za_ref, zb_ref, zc_ref, zd_ref, g0, g1, g2, g3, wc_ref, wg_ref, wp_ref, ws_ref,
             dza_ref, dzb_ref, dzc_ref, dzd_ref, dg_ref, dwc_ref, dwg_ref, dwp_ref, dws_ref,
             awc, awg, awp, aws):
        i = pl.program_id(0)

        @pl.when(i == 0)
        def _():
            awc[...] = jnp.zeros_like(awc)
            awg[...] = jnp.zeros_like(awg)
            awp[...] = jnp.zeros_like(awp)
            aws[...] = jnp.zeros_like(aws)

        dmv = dm_ref[...]

        def gated(g_ref, y, col):
            s = _sigmoid(g_ref[...])
            dg_ref[:, col * D:(col + 1) * D] = (dmv * y * s * (1.0 - s)).astype(BF16)
            return (dmv * s)

        def linear(z_ref, w_ref, acc, dz_ref, col, g_ref):
            zv = z_ref[...]
            dy = gated(g_ref, _dot(zv, w_ref[...]), col).astype(BF16)
            dz_ref[...] = _dot(dy, w_ref[...], 1, 1)
            acc[...] += _dot(zv, dy, 0, 0)

        linear(za_ref, wc_ref, awc, dza_ref, 0, g0)
        linear(zc_ref, wp_ref, awp, dzc_ref, 2, g2)
        linear(zd_ref, ws_ref, aws, dzd_ref, 3, g3)
        zbv = zb_ref[...]
        glu = _dot(zbv, wg_ref[...])
        ga = glu[:, :D]
        sg = _sigmoid(glu[:, D:])
        dyb = gated(g1, ga * sg, 1)
        dga = (dyb * sg).astype(BF16)
        dgg = (dyb * ga * sg * (1.0 - sg)).astype(BF16)
        dzb_ref[...] = _dot(dga, wg_ref[:, :D], 1, 1) + _dot(dgg, wg_ref[:, D:], 1, 1)
        awg[:, :D] += _dot(zbv, dga, 0, 0)
        awg[:, D:] += _dot(zbv, dgg, 0, 0)

        @pl.when(i == n - 1)
        def _():
            dwc_ref[...] = awc[...].astype(BF16)
            dwg_ref[...] = awg[...].astype(BF16)
            dwp_ref[...] = awp[...].astype(BF16)
            dws_ref[...] = aws[...].astype(BF16)

    zt = pl.BlockSpec((tm, W), lambda i: (i, 0))
    gate = lambda b: pl.BlockSpec((tm, D), lambda i: (i, 2 + b))
    wfull = lambda n_: pl.BlockSpec((W, n_), lambda i: (0, 0))
    zs = _sds((L, W), F32)
    return _call(body, name="merge_bwd", grid=(n,),
                 in_specs=[pl.BlockSpec((tm, D), lambda i: (i, 0)), zt, zt, zt, zt, gate(0), gate(1), gate(2), gate(3),
                           wfull(D), wfull(2 * D), wfull(D), wfull(D)],
                 out_specs=[zt, zt, zt, zt, pl.BlockSpec((tm, 4 * D), lambda i: (i, 0)),
                            wfull(D), wfull(2 * D), wfull(D), wfull(D)],
                 out_shape=[zs, zs, zs, zs, _sds((L, 4 * D), BF16),
                            _sds((W, D), BF16), _sds((W, 2 * D), BF16), _sds((W, D), BF16), _sds((W, D), BF16)],
                 scratch_shapes=[pltpu.VMEM((W, D), F32), pltpu.VMEM((W, 2 * D), F32), pltpu.VMEM((W, D), F32),
                                 pltpu.VMEM((W, D), F32)],
                 compiler_params=_params(("arbitrary",)))(dm, za, zb, zc, zd, p, p, p, p,
                                                          w_conv_out, w_glu, w_pool_out, w_sb_out)


def _adam_math(w, g, m, v):
    m2 = ADAM_B1 * m + (1.0 - ADAM_B1) * g
    v2 = ADAM_B2 * v + (1.0 - ADAM_B2) * (g * g)
    m_hat = m2 / (1.0 - ADAM_B1 ** ADAM_STEP)
    v_hat = v2 / (1.0 - ADAM_B2 ** ADAM_STEP)
    return -ADAM_LR * (m_hat / (jnp.sqrt(v_hat) + ADAM_EPS) + ADAM_WD * w), m2, v2


def _as_rows(a):
    return a.reshape(-1, a.shape[-1])


def _adamw(w, g, m, v):
    shape = w.shape
    w2, g2, m2, v2 = _as_rows(w), _as_rows(g), _as_rows(m), _as_rows(v)
    R, C = w2.shape
    tr = R
    for cand in (1024, 512, 256, 128, 64, 32, 16, 8):
        if R % cand == 0 and cand * C * 4 <= 2 * 1024 * 1024:
            tr = cand
            break

    def body(w_ref, g_ref, m_ref, v_ref, d_ref, m_out, v_out):
        d, mn, vn = _adam_math(w_ref[...], g_ref[...], m_ref[...], v_ref[...])
        d_ref[...] = d
        m_out[...] = mn
        v_out[...] = vn

    blk = pl.BlockSpec((tr, C), lambda i: (i, 0))
    shp = _sds((R, C), F32)
    outs = _call(body, name="adamw", grid=(R // tr,), in_specs=[blk] * 4, out_specs=[blk] * 3, out_shape=[shp] * 3,
                 compiler_params=_params(("parallel",)))(w2, g2, m2, v2)
    return tuple(o.reshape(shape) for o in outs)


def _sum_parts(parts, out_dtype, name):
    shape = parts[0].shape
    flat = [_as_rows(a) for a in parts]
    R, C = flat[0].shape
    tr = R
    for cand in (1024, 512, 256, 128, 64, 32, 16):
        if R % cand == 0 and cand * C * 4 <= 2 * 1024 * 1024:
            tr = cand
            break
    k = len(parts)

    def body(*refs):
        acc = refs[0][...].astype(F32)
        for r in refs[1:k]:
            acc = acc + r[...].astype(F32)
        refs[k][...] = acc.astype(out_dtype)

    blk = pl.BlockSpec((tr, C), lambda i: (i, 0))
    out = _call(body, name=name, grid=(R // tr,), in_specs=[blk] * k, out_specs=blk, out_shape=_sds((R, C), out_dtype),
                compiler_params=_params(("parallel",)))(*flat)
    return out.reshape(shape)


ADA_SHARD = 9 * D // N_CHIP
ADA_TN = 768


def _ada_fwd(c_pad, w_ada, b_ada_cols):
    depth = w_ada.shape[0]

    def body(c_ref, w_ref, b_ref, o_ref):
        cv = c_ref[...]
        o_ref[...] = _bdot(cv * _sigmoid(cv), w_ref[...]) + b_ref[...]

    return _call(body, name="ada_fwd", grid=(depth, ADA_SHARD // ADA_TN),
                 in_specs=[pl.BlockSpec((16, D), lambda l, j: (0, 0)),
                           pl.BlockSpec((None, D, ADA_TN), lambda l, j: (l, 0, j)),
                           pl.BlockSpec((None, 1, ADA_TN), lambda l, j: (l, 0, j))],
                 out_specs=pl.BlockSpec((None, 16, ADA_TN), lambda l, j: (l, 0, j)),
                 out_shape=_sds((depth, 16, ADA_SHARD), F32),
                 compiler_params=_params(("parallel", "parallel")))(c_pad, w_ada, b_ada_cols)


def _ada_wgrad(c_pad, d_ada):
    depth = d_ada.shape[0]

    def body(c_ref, d_ref, o_ref):
        cv = c_ref[...]
        o_ref[...] = _bdot(cv * _sigmoid(cv), d_ref[...], 0, 0)

    return _call(body, name="ada_wgrad", grid=(depth, ADA_SHARD // ADA_TN),
                 in_specs=[pl.BlockSpec((16, D), lambda l, j: (0, 0)),
                           pl.BlockSpec((None, 16, ADA_TN), lambda l, j: (l, 0, j))],
                 out_specs=pl.BlockSpec((None, D, ADA_TN), lambda l, j: (l, 0, j)),
                 out_shape=_sds((depth, D, ADA_SHARD), F32),
                 compiler_params=_params(("parallel", "parallel")))(c_pad, d_ada)


HBM_SPEC = pl.BlockSpec(memory_space=pltpu.HBM)


def _place():
    x, y, c = lax.axis_index("x"), lax.axis_index("y"), lax.axis_index("c")
    peers = [(1 - x, y), (x, 1 - y), (1 - x, 1 - y)]
    return x, y, c, peers


def _chip(px, py):
    return 2 * px + py


def _allgather8(block, name):
    m_per, n = block.shape

    def body(x_ref, out_ref, send_sems, recv_sems, local_sem):
        x, y, c, chips = _place()
        me, sibling = (x, y, c), (x, y, 1 - c)

        def rows(px, py, pc):
            return out_ref.at[pl.ds(pl.multiple_of((4 * px + 2 * py + pc) * m_per, 8), m_per), :]

        def copy(k, blk, to, src=None):
            return pltpu.make_async_remote_copy(
                src_ref=rows(*blk) if src is None else src, dst_ref=rows(*blk),
                send_sem=send_sems.at[k], recv_sem=recv_sems.at[k], device_id=to, device_id_type=MESH)

        mine = pltpu.make_async_copy(x_ref, rows(*me), local_sem)
        mine.start()
        first = [copy(0, me, sibling, src=x_ref)]
        first += [copy(1 + j, me, (*chip, c), src=x_ref) for j, chip in enumerate(chips)]
        for cp in first:
            cp.start()
        passed = [copy(4 + j, (*chip, c), sibling) for j, chip in enumerate(chips)]
        for j, chip in enumerate(chips):
            copy(1 + j, (*chip, c), me).wait_recv()
            passed[j].start()
        copy(0, sibling, me).wait_recv()
        for j, chip in enumerate(chips):
            copy(4 + j, (*chip, 1 - c), me).wait_recv()
        for cp in first + passed:
            cp.wait_send()
        mine.wait()

    return _call(body, name=name, out_shape=_sds((N_DEV * m_per, n), block.dtype),
                 in_specs=[pl.BlockSpec(memory_space=pltpu.VMEM)], out_specs=pl.BlockSpec(memory_space=pltpu.VMEM),
                 scratch_shapes=[pltpu.SemaphoreType.DMA((7,)), pltpu.SemaphoreType.DMA((7,)), pltpu.SemaphoreType.DMA],
                 compiler_params=_params())(block)


GATHERED = (("w_ff_in", 0, -1, 0), ("w_ff_out", 0, -2, 0),
            ("w_in", None, -1, 1), ("w_conv_out", None, -1, 1), ("w_glu", None, -1, 1), ("w_pool_out", None, -1, 1),
            ("w_sb_out", None, -1, 1), ("w_out", None, -2, 1),
            ("w_ff_in", 1, -1, 2), ("w_ff_out", 1, -2, 2))
N_SUB = 3


def _lead(ref):
    return (slice(None),) * (len(ref.shape) - 2)


def _mo(v, m):
    return v if isinstance(v, int) else pl.multiple_of(v, m)


def _full_region(ref, axis, j, half, shard_shape):
    rs, cs = shard_shape[-2], shard_shape[-1]
    if axis == -1:
        r0, nr = (0, rs) if half is None else (half * (rs // 2), rs // 2)
        return ref.at[_lead(ref) + (pl.ds(_mo(r0, 16), nr), pl.ds(_mo(j * cs, 128), cs))]
    r0, nr = (j * rs, rs) if half is None else (j * rs + half * (rs // 2), rs // 2)
    return ref.at[_lead(ref) + (pl.ds(_mo(r0, 16), nr), slice(None))]


def _shard_half(ref, half):
    rs = ref.shape[-2]
    return ref.at[_lead(ref) + (pl.ds(_mo(half * (rs // 2), 16), rs // 2), slice(None))]


def _full_shape(shard_shape, axis):
    s = list(shard_shape)
    s[axis] *= N_CHIP
    return tuple(s)


class _Lay:
    def __init__(self, shard_shape, axis):
        self.axis = axis
        self.shard_shape = tuple(shard_shape)
        self.full_shape = _full_shape(shard_shape, axis)
        self.lead = int(np.prod(shard_shape[:-2]))
        self.rs, self.cs = shard_shape[-2], shard_shape[-1]
        self.hr = self.rs // 2
        self.tr = next(t for t in (256, 128, 64, 32, 16) if self.hr % t == 0 and t * self.cs * 4 <= (1 << 20))
        self.half_rows_shape = _half_rows_shape(self.full_shape)
        self.half_shard_shape = _half_rows_shape(self.shard_shape)

    def full(self, jf, hf):
        if self.axis == -1:
            return ((self.lead, 2, self.hr, N_CHIP * self.cs),
                    pl.BlockSpec((None, None, self.tr, self.cs), lambda b, j, i, s: (b, hf(j, s), i, jf(j, s))))
        return ((self.lead, N_CHIP, 2, self.hr, self.cs),
                pl.BlockSpec((None, None, None, self.tr, self.cs), lambda b, j, i, s: (b, jf(j, s), hf(j, s), i, 0)))

    def half_rows(self, jf):
        if self.axis == -1:
            return ((self.lead, self.hr, N_CHIP * self.cs),
                    pl.BlockSpec((None, self.tr, self.cs), lambda b, j, i, s: (b, i, jf(j, s))))
        return ((self.lead, N_CHIP, self.hr, self.cs),
                pl.BlockSpec((None, None, self.tr, self.cs), lambda b, j, i, s: (b, jf(j, s), i, 0)))

    def half_shard(self):
        return (self.lead, self.hr, self.cs), pl.BlockSpec((None, self.tr, self.cs), lambda b, j, i, s: (b, i, 0))

    def shard(self, hf):
        return ((self.lead, 2, self.hr, self.cs),
                pl.BlockSpec((None, None, self.tr, self.cs), lambda b, j, i, s: (b, hf(j, s), i, 0)))


def _view_sum(sel, operands, out_view, out_shape, out_dtype, grid, name):
    k = len(operands)

    def body(sel_ref, *refs):
        acc = refs[0][...].astype(F32)
        for r in refs[1:k]:
            acc = acc + r[...].astype(F32)
        refs[k][...] = acc.astype(out_dtype)

    spec = pltpu.PrefetchScalarGridSpec(num_scalar_prefetch=1, grid=grid, in_specs=[v[1] for _, v in operands],
                                        out_specs=out_view[1])
    out = _call(body, name=name, grid_spec=spec, out_shape=_sds(out_view[0], out_dtype),
                compiler_params=_params(("parallel", "parallel", "parallel")))(
                    sel, *[a.reshape(v[0]) for a, v in operands])
    return out.reshape(out_shape)


def _sel_core(j, s):
    return s[0]


def _sel_chip(j, s):
    return s[1]


def _grid_j(j, s):
    return j


def _place_shard(lay, sel, w):
    return _view_sum(sel, [(w, lay.shard(_grid_j))], lay.full(_sel_chip, _grid_j), lay.full_shape, BF16,
                     (lay.lead, 2, lay.hr // lay.tr), "place_shard")


SEM_SPEC = pl.BlockSpec(memory_space=pltpu.SEMAPHORE)
ANY_SPEC = pl.BlockSpec(memory_space=pl.ANY)
SPLIT_COPY = pltpu.SideEffectType.DATAFLOW_SIDE_EFFECTING


def _in_hbm(a):
    return pltpu.with_memory_space_constraint(a, pltpu.HBM)


def _gather_start(fulls, after, lays, tag):
    n = len(fulls)

    def body(*refs):
        send_sems, recv_sems = refs[n + 1], refs[n + 2]
        bufs, token = refs[n + 3:2 * n + 3], refs[2 * n + 3]
        x, y, c, chips = _place()
        my = _chip(x, y)
        for a in range(n):
            own = _full_region(bufs[a], lays[a].axis, my, c, lays[a].shard_shape)
            for k, chip in enumerate(chips):
                pltpu.make_async_remote_copy(
                    src_ref=own, dst_ref=own, send_sem=send_sems.at[a * 3 + k], recv_sem=recv_sems.at[a * 3 + k],
                    device_id=(*chip, c), device_id_type=MESH).start()
        token[...] = jnp.zeros_like(token)

    outs = _call(body, name="gather_start_" + tag,
                 out_shape=[pltpu.SemaphoreType.DMA((3 * n,)), pltpu.SemaphoreType.DMA((3 * n,))]
                 + [pltpu.HBM(f.shape, f.dtype) for f in fulls] + [_sds((8, 128), F32)],
                 in_specs=[HBM_SPEC] * n + [ANY_SPEC],
                 out_specs=[SEM_SPEC, SEM_SPEC] + [HBM_SPEC] * n + [pl.BlockSpec(memory_space=pltpu.VMEM)],
                 input_output_aliases={a: a + 2 for a in range(n)},
                 compiler_params=pltpu.CompilerParams(has_side_effects=SPLIT_COPY))(*[_in_hbm(f) for f in fulls], after)
    return outs[0], outs[1], outs[2:2 + n], outs[2 + n]


def _gather_wait(send_sems, recv_sems, bufs, after, lays, tag):
    n = len(bufs)

    def body(*refs):
        ss, rs = refs[n], refs[n + 1]
        outs = refs[n + 3:]
        x, y, c, chips = _place()
        my = _chip(x, y)
        for a in range(n):
            own = _full_region(outs[a], lays[a].axis, my, c, lays[a].shard_shape)
            for k, chip in enumerate(chips):
                landed = _full_region(outs[a], lays[a].axis, _chip(*chip), c, lays[a].shard_shape)
                cp = pltpu.make_async_remote_copy(
                    src_ref=own, dst_ref=landed, send_sem=ss.at[a * 3 + k], recv_sem=rs.at[a * 3 + k],
                    device_id=(*chip, c), device_id_type=MESH)
                cp.wait_send()
                cp.wait_recv()

    return _call(body, name="gather_wait_" + tag,
                 out_shape=[pltpu.HBM(b.shape, b.dtype) for b in bufs],
                 in_specs=[HBM_SPEC] * n + [SEM_SPEC, SEM_SPEC, ANY_SPEC], out_specs=[HBM_SPEC] * n,
                 input_output_aliases={a: a for a in range(n)},
                 compiler_params=pltpu.CompilerParams(has_side_effects=SPLIT_COPY))(*bufs, send_sems, recv_sems, after)


def _gather_forward(bufs, lays):
    n = len(bufs)

    def body(*refs):
        outs = refs[n:2 * n]
        send_sems, recv_sems = refs[2 * n:]
        x, y, c, chips = _place()
        sibling = (x, y, 1 - c)
        sends = []
        for a in range(n):
            for k, chip in enumerate(chips):
                landed = _full_region(outs[a], lays[a].axis, _chip(*chip), c, lays[a].shard_shape)
                cp = pltpu.make_async_remote_copy(
                    src_ref=landed, dst_ref=landed, send_sem=send_sems.at[a * 3 + k], recv_sem=recv_sems.at[a * 3 + k],
                    device_id=sibling, device_id_type=MESH)
                cp.start()
                sends.append(cp)
        for a in range(n):
            for k, chip in enumerate(chips):
                passed = _full_region(outs[a], lays[a].axis, _chip(*chip), 1 - c, lays[a].shard_shape)
                pltpu.make_async_remote_copy(
                    src_ref=passed, dst_ref=passed, send_sem=send_sems.at[a * 3 + k], recv_sem=recv_sems.at[a * 3 + k],
                    device_id=sibling, device_id_type=MESH).wait_recv()
        for cp in sends:
            cp.wait_send()

    return _call(body, name="gather_forward",
                 out_shape=[_sds(b.shape, b.dtype) for b in bufs],
                 in_specs=[HBM_SPEC] * n, out_specs=[HBM_SPEC] * n,
                 input_output_aliases={a: a for a in range(n)},
                 scratch_shapes=[pltpu.SemaphoreType.DMA((3 * n,)), pltpu.SemaphoreType.DMA((3 * n,))],
                 compiler_params=_params())(*bufs)


def _half_rows_shape(full_shape):
    s = list(full_shape)
    s[-2] //= 2
    return tuple(s)


def _reduce_sibling(grads, lays):
    n = len(grads)

    def pieces(lay, ref_full, ref_half, half):
        if lay.axis == -1:
            src = ref_full.at[_lead(ref_full) + (pl.ds(_mo(half * lay.hr, 16), lay.hr), slice(None))]
            return [(src, ref_half)]
        out = []
        for j in range(N_CHIP):
            src = _full_region(ref_full, -2, j, half, lay.shard_shape)
            dst = ref_half.at[_lead(ref_half) + (pl.ds(j * lay.hr, lay.hr), slice(None))]
            out.append((src, dst))
        return out

    n_cp = sum(1 if lay.axis == -1 else N_CHIP for lay in lays)

    def body(*refs):
        ins, got = refs[:n], refs[n:2 * n]
        send_sems, recv_sems = refs[2 * n:]
        x, y, c, _ = _place()
        sibling = (x, y, 1 - c)
        started, idx = [], 0
        for a in range(n):
            for src, dst in pieces(lays[a], ins[a], got[a], 1 - c):
                rc = pltpu.make_async_remote_copy(src_ref=src, dst_ref=dst, send_sem=send_sems.at[idx],
                                                  recv_sem=recv_sems.at[idx], device_id=sibling, device_id_type=MESH)
                rc.start()
                started.append(rc)
                idx += 1
        for rc in started:
            rc.wait_recv()
            rc.wait_send()

    return _call(body, name="reduce_sibling",
                 out_shape=[_sds(lay.half_rows_shape, BF16) for lay in lays],
                 in_specs=[HBM_SPEC] * n, out_specs=[HBM_SPEC] * n,
                 scratch_shapes=[pltpu.SemaphoreType.DMA((n_cp,)), pltpu.SemaphoreType.DMA((n_cp,))],
                 compiler_params=_params())(*grads)


def _chip_region(lay, ref, j):
    if lay.axis == -1:
        return ref.at[_lead(ref) + (slice(None), pl.ds(_mo(j * lay.cs, 128), lay.cs))]
    return ref.at[_lead(ref) + (pl.ds(_mo(j * lay.hr, 16), lay.hr), slice(None))]


def _reduce_start(parts, lays, tag):
    n = len(parts)
    landing = [_in_hbm(lax.empty(lay.half_shard_shape, BF16)) for lay in lays for _ in range(3)]

    def body(*refs):
        send_sems, recv_sems = refs[4 * n], refs[4 * n + 1]
        src, land, token = refs[4 * n + 2:5 * n + 2], refs[5 * n + 2:8 * n + 2], refs[8 * n + 2]
        x, y, c, chips = _place()
        for a in range(n):
            for k, chip in enumerate(chips):
                pltpu.make_async_remote_copy(
                    src_ref=_chip_region(lays[a], src[a], _chip(*chip)), dst_ref=land[a * 3 + k],
                    send_sem=send_sems.at[a * 3 + k], recv_sem=recv_sems.at[a * 3 + k],
                    device_id=(*chip, c), device_id_type=MESH).start()
        token[...] = jnp.zeros_like(token)

    ops = [_in_hbm(p) for p in parts] + landing
    outs = _call(body, name="reduce_start_" + tag,
                 out_shape=[pltpu.SemaphoreType.DMA((3 * n,)), pltpu.SemaphoreType.DMA((3 * n,))]
                 + [pltpu.HBM(o.shape, o.dtype) for o in ops] + [_sds((8, 128), F32)],
                 in_specs=[HBM_SPEC] * (4 * n),
                 out_specs=[SEM_SPEC, SEM_SPEC] + [HBM_SPEC] * (4 * n) + [pl.BlockSpec(memory_space=pltpu.VMEM)],
                 input_output_aliases={a: a + 2 for a in range(4 * n)},
                 compiler_params=pltpu.CompilerParams(has_side_effects=SPLIT_COPY))(*ops)
    return outs[0], outs[1], outs[2:2 + n], outs[2 + n:2 + 4 * n], outs[2 + 4 * n]


def _reduce_wait(send_sems, recv_sems, parts, landing, after, lays, tag):
    n = len(parts)

    def body(*refs):
        ss, rs = refs[4 * n], refs[4 * n + 1]
        src, land = refs[4 * n + 3:5 * n + 3], refs[5 * n + 3:]
        x, y, c, chips = _place()
        for a in range(n):
            for k, chip in enumerate(chips):
                cp = pltpu.make_async_remote_copy(
                    src_ref=_chip_region(lays[a], src[a], _chip(*chip)), dst_ref=land[a * 3 + k],
                    send_sem=ss.at[a * 3 + k], recv_sem=rs.at[a * 3 + k], device_id=(*chip, c), device_id_type=MESH)
                cp.wait_send()
                cp.wait_recv()

    ops = list(parts) + list(landing)
    outs = _call(body, name="reduce_wait_" + tag,
                 out_shape=[pltpu.HBM(o.shape, o.dtype) for o in ops],
                 in_specs=[HBM_SPEC] * (4 * n) + [SEM_SPEC, SEM_SPEC, ANY_SPEC], out_specs=[HBM_SPEC] * (4 * n),
                 input_output_aliases={a: a for a in range(4 * n)},
                 compiler_params=pltpu.CompilerParams(has_side_effects=SPLIT_COPY))(*ops, send_sems, recv_sems, after)
    return outs[:n], [outs[n + 3 * a:n + 3 * a + 3] for a in range(n)]


def _share_halves(shards):
    n = len(shards)

    def body(*refs):
        outs = refs[n:2 * n]
        send_sems, recv_sems = refs[2 * n:]
        x, y, c, _ = _place()
        sibling = (x, y, 1 - c)
        started = []
        for a in range(n):
            mine = _shard_half(outs[a], c)
            rc = pltpu.make_async_remote_copy(src_ref=mine, dst_ref=mine, send_sem=send_sems.at[a],
                                              recv_sem=recv_sems.at[a], device_id=sibling, device_id_type=MESH)
            rc.start()
            started.append(rc)
        for rc in started:
            rc.wait_recv()
            rc.wait_send()

    return _call(body, name="share_halves", out_shape=[_sds(s.shape, F32) for s in shards],
                 in_specs=[HBM_SPEC] * n, out_specs=[HBM_SPEC] * n, input_output_aliases={a: a for a in range(n)},
                 scratch_shapes=[pltpu.SemaphoreType.DMA((n,)), pltpu.SemaphoreType.DMA((n,))],
                 compiler_params=_params())(*shards)


def _reduce_begin(grads, lays, sel, tag):
    got = _reduce_sibling(grads, lays)
    chip_parts = [
        _view_sum(sel, [(g, lay.full(_grid_j, _sel_core)), (o, lay.half_rows(_grid_j))], lay.half_rows(_grid_j),
                  lay.half_rows_shape, BF16, (lay.lead, N_CHIP, lay.hr // lay.tr), "chip_partial")
        for g, o, lay in zip(grads, got, lays)]
    return _reduce_start(chip_parts, lays, tag)


def _reduce_end(state, after, lays, sel, tag):
    send_sems, recv_sems, chip_parts, landing, _ = state
    chip_parts, landed = _reduce_wait(send_sems, recv_sems, chip_parts, landing, after, lays, tag)
    halves = [
        _view_sum(sel, [(t, lay.half_rows(_sel_chip))] + [(l, lay.half_shard()) for l in ls], lay.shard(_sel_core),
                  lay.shard_shape, F32, (lay.lead, 1, lay.hr // lay.tr), "shard_half_sum")
        for t, ls, lay in zip(chip_parts, landed, lays)]
    return _share_halves(halves)


def _embed(blocks):
    n, r, c = blocks.shape
    eye = jnp.eye(n, dtype=blocks.dtype)
    return (blocks[:, :, None, :] * eye[:, None, :, None]).reshape(n * r, n * c)


def _unembed(mat, n):
    r, c = mat.shape[0] // n, mat.shape[1] // n
    return jnp.transpose(jnp.diagonal(mat.reshape(n, r, n, c), axis1=0, axis2=2), (2, 0, 1))


def _to_heads(a):
    return jnp.transpose(a.reshape(a.shape[0], HEADS, HD), (1, 0, 2))


def _from_heads(a):
    return jnp.transpose(a, (1, 0, 2)).reshape(a.shape[1], W)


def _row(v):
    return v.reshape(1, -1)


def _ffn_fwd(x, ada, gp, gq, w_in, w_out, s):
    L = x.shape[0]
    h = _norm_mod(x, _row(gp[s]), _row(ada[3 * s]), _row(ada[3 * s + 1]))
    a, b, act = _ffn_in(h, w_in)
    f = _mm(act, w_out, M=L, N=D, K=FF, tm=min(L, 1024), tn=512, name="ffn_out")
    x2 = _post(x, f, _row(gq[s]), _row(ada[3 * s + 2]), 0.5)
    return x2, (x, h, a, b, act, f)


def _ffn_bwd(dx, saved, ada, gp, gq, w_in, w_out, s):
    x, h, a, b, act, f = saved
    L = x.shape[0]
    df, dgate, dgq = _post_bwd(dx, f, _row(gq[s]), _row(ada[3 * s + 2]), 0.5)
    dw_out = _mm(act, df, M=FF, N=D, K=L, tm=256, tn=1024, ta=True, out_dtype=BF16, name="ffn_dw_out")
    da, db = _ffn_mid_bwd(df, w_out, a, b)
    du = jnp.concatenate([da, db], axis=1)
    dw_in = _mm(h, du, M=D, N=2 * FF, K=L, tm=1024, tn=512, ta=True, out_dtype=BF16, name="ffn_dw_in")
    dh = _mm(du, w_in, M=L, N=D, K=2 * FF, tm=min(L, 1024), tn=1024, tk=1408, tb=True, name="ffn_dh")
    dx2, dshift, dscale, dgp = _norm_mod_bwd(dh, x, _row(gp[s]), _row(ada[3 * s + 1]), dx)
    return dx2, dw_in, dw_out, (dshift, dscale, dgate), dgp, dgq


def _mixer_fwd(x, ada, gp, gq, wf, sm):
    L = x.shape[0]
    h = _norm_mod(x, _row(gp[1]), _row(ada[3]), _row(ada[4]))
    p = _mm(h, wf["w_in"], M=L, N=IN_COLS, K=D, tm=min(L, 1024), tn=512, name="mixer_in")
    za = _conv_fwd(p, sm["conv_w"])
    y, zb = _ssm_fwd(p, sm["b_re"], sm["b_im"], sm["c_re"], sm["c_im"], sm["abr"], sm["abi"], sm["fr"], sm["fi"], sm["ssm_d"])
    zc = _pool_fwd(p, sm["w_pool"], sm["pool_scale"])
    q = _to_heads(p[:, 5 * W:6 * W]) * (HD ** -0.5)
    k = _to_heads(p[:, 6 * W:7 * W])
    v = _to_heads(p[:, 7 * W:8 * W])
    o_heads, block_sums = _sb_fwd(q, k, v)
    zd = _from_heads(o_heads).astype(BF16)
    merged = _merge_fwd(za, zb, zc, zd, p, wf["w_conv_out"], wf["w_glu"], wf["w_pool_out"], wf["w_sb_out"])
    m = _mm(merged, wf["w_out"], M=L, N=D, K=D, tm=min(L, 1024), tn=512, name="mixer_out")
    x2 = _post(x, m, _row(gq[1]), _row(ada[5]), 1.0)
    return x2, (x, h, p, za, y, zb, zc, zd, q, k, v, block_sums, merged, m)


def _mixer_bwd(dx, saved, ada, gp, gq, wf, sm):
    x, h, p, za, y, zb, zc, zd, q, k, v, block_sums, merged, m = saved
    L = x.shape[0]
    dmf, dgate, dgq = _post_bwd(dx, m, _row(gq[1]), _row(ada[5]), 1.0)
    dw_out = _mm(merged, dmf, M=D, N=D, K=L, tm=512, tn=512, ta=True, out_dtype=BF16, name="mixer_dw_out")
    dmerged = _mm(dmf, wf["w_out"], M=L, N=D, K=D, tm=min(L, 1024), tn=512, tb=True, name="mixer_dmerged")
    dza, dzb, dzc, dzd, dgates, dwc, dwg, dwp, dws = _merge_bwd(
        dmerged, za, zb, zc, zd, p, wf["w_conv_out"], wf["w_glu"], wf["w_pool_out"], wf["w_sb_out"])
    dconv, dconv_w = _conv_bwd(p, sm["conv_w"], dza)
    (du_ssm, dd, dbr, dbi, dcr, dci, gar, gai, gfr, gfi) = _ssm_bwd(
        p, y, dzb, sm["b_re"], sm["b_im"], sm["c_re"], sm["c_im"], sm["abr"], sm["abi"], sm["fr"], sm["fi"], sm["ssm_d"])
    du_pool, dwpool, dpscale = _pool_bwd(p, sm["w_pool"], sm["pool_scale"], dzc)
    dq, dk, dv = _sb_bwd(q, k, v, _to_heads(dzd), block_sums)
    dqkv = [_from_heads(t).astype(BF16) for t in (dq * (HD ** -0.5), dk, dv)]
    dp = jnp.concatenate([dconv, du_ssm, du_pool] + dqkv + [dgates], axis=1)
    dw_in = _mm(h, dp, M=D, N=IN_COLS, K=L, tm=1024, tn=512, ta=True, out_dtype=BF16, name="mixer_dw_in")
    dh = _mm(dp, wf["w_in"], M=L, N=D, K=IN_COLS, tm=min(L, 1024), tn=1024, tk=1536, tb=True, name="mixer_dh")
    dx2, dshift, dscale, dgp = _norm_mod_bwd(dh, x, _row(gp[1]), _row(ada[4]), dx)
    wgrads = [dw_in, dwc, dwg, dwp, dws, dw_out]
    small = {"conv_w": dconv_w, "ssm_d": dd, "b_re": dbr, "b_im": dbi, "c_re": dcr, "c_im": dci,
             "abr": gar, "abi": gai, "fr": gfr, "fi": gfi, "w_pool": dwpool, "pool_scale": dpscale}
    return dx2, wgrads, small, (dshift, dscale, dgate), dgp, dgq


def _pack(arrays):
    flat = jnp.concatenate([a.reshape(-1) for a in arrays])
    rows = -(-flat.shape[0] // 128)
    rows = -(-rows // 64) * 64
    return jnp.pad(flat, (0, rows * 128 - flat.shape[0])).reshape(rows, 128)


def _unpack(block, shapes):
    flat = block.reshape(-1)
    out, off = [], 0
    for s in shapes:
        n = int(np.prod(s))
        out.append(flat[off:off + n].reshape(s))
        off += n
    return out


def _pad_rows(a, mult):
    rows = -(-a.shape[0] // mult) * mult
    return jnp.concatenate([a] * (-(-rows // a.shape[0])), axis=0)[:rows]


SMALL_ORDER = ("d_ada", "g_pre", "g_post", "conv_w", "lam_re", "lam_im", "log_dt", "ssm_b_re", "ssm_b_im",
               "ssm_c_re", "ssm_c_im", "ssm_d", "w_pool", "pool_scale")
WEIGHTS = ('w_ada', 'b_ada', 'g_pre', 'g_post', 'w_ff_in', 'w_ff_out', 'w_in', 'conv_w', 'w_conv_out', 'lam_re', 'lam_im',
           'log_dt', 'ssm_b_re', 'ssm_b_im', 'ssm_c_re', 'ssm_c_im', 'ssm_d', 'w_glu', 'w_pool', 'pool_scale', 'w_pool_out',
           'w_sb_out', 'w_out')


def _step(a):
    depth = a["w_ada"].shape[0]
    x = a["x"][0]
    target = a["loss_target"][0]
    L = x.shape[0]
    ix, iy, ic = lax.axis_index("x"), lax.axis_index("y"), lax.axis_index("c")
    chip = 2 * ix + iy
    me = 4 * ix + 2 * iy + ic
    sel = jnp.stack([ic, chip]).astype(jnp.int32)
    lays = [_Lay(a[name].shape[(1 if idx is None else 2):], ax) for name, idx, ax, _ in GATHERED]

    first_shapes = [(D,), (depth, 3, W), (depth, 3, W), (depth, 3, W // N_CHIP)]
    gathered = _allgather8(_pack([a["c"], a["g_pre"], a["g_post"], a["conv_w"]]), "gather_small_inputs")
    per_dev = [_unpack(blk, first_shapes) for blk in gathered.reshape(N_DEV, -1, 128)]
    c_all = jnp.stack([d[0] for d in per_dev])
    c_pad = jnp.concatenate([c_all, jnp.zeros_like(c_all)], axis=0)
    g_pre = jnp.concatenate([per_dev[2 * j][1] for j in range(N_CHIP)], axis=-1)
    g_post = jnp.concatenate([per_dev[2 * j][2] for j in range(N_CHIP)], axis=-1)
    conv_w = jnp.concatenate([per_dev[2 * j][3] for j in range(N_CHIP)], axis=-1)

    b_cols = lax.dynamic_slice(a["b_ada"], (0, chip * ADA_SHARD), (depth, ADA_SHARD)).reshape(depth, 1, ADA_SHARD)
    ada_part = _ada_fwd(c_pad, a["w_ada"], b_cols)
    ada_all = _allgather8(ada_part.reshape(depth * 16, ADA_SHARD), "gather_ada").reshape(N_DEV, depth, 16, ADA_SHARD)
    ada_rows = lax.dynamic_slice(ada_all, (0, 0, me, 0), (N_DEV, depth, 1, ADA_SHARD))[:, :, 0]
    ada = jnp.concatenate([ada_rows[2 * j] for j in range(N_CHIP)], axis=-1).reshape(depth, 9, D)

    lam_re = _pad_rows(a["lam_re"].reshape(depth, NST), 8)
    lam_im = _pad_rows(a["lam_im"].reshape(depth, NST), 8)
    log_dt_x = _pad_rows(jnp.repeat(a["log_dt"], GP, axis=1), 8)
    abr, abi, fr, fi = _ssm_prep(lam_re, lam_im, log_dt_x)

    def small_of(l):
        return {"conv_w": conv_w[l], "ssm_d": _row(a["ssm_d"][l]), "pool_scale": _row(a["pool_scale"][l]),
                "b_re": _embed(jnp.transpose(a["ssm_b_re"][l], (0, 2, 1))), "b_im": _embed(jnp.transpose(a["ssm_b_im"][l], (0, 2, 1))),
                "c_re": _embed(jnp.transpose(a["ssm_c_re"][l], (0, 2, 1))), "c_im": _embed(jnp.transpose(a["ssm_c_im"][l], (0, 2, 1))),
                "w_pool": _embed(a["w_pool"][l]),
                "abr": abr[l:l + 1], "abi": abi[l:l + 1], "fr": fr[l:l + 1], "fi": fi[l:l + 1]}

    def entries(g):
        return [i for i, e in enumerate(GATHERED) if e[3] == g]

    def shard_of(i, l):
        name, idx = GATHERED[i][0], GATHERED[i][1]
        return a[name][l] if idx is None else a[name][l, idx]

    stages = [(l, g) for l in range(depth) for g in range(N_SUB)]

    def gather_begin(t, after):
        l, g = stages[t]
        placed = [_place_shard(lays[i], sel, shard_of(i, l)) for i in entries(g)]
        return _gather_start(placed, after, [lays[i] for i in entries(g)], str(t))

    saved, weights, smalls = [], [], [small_of(l) for l in range(depth)]
    pending = {t: gather_begin(t, ada) for t in range(min(2, len(stages)))}
    for t, (l, g) in enumerate(stages):
        glays = [lays[i] for i in entries(g)]
        send_sems, recv_sems, bufs, _ = pending.pop(t)
        w = _gather_forward(_gather_wait(send_sems, recv_sems, bufs, x, glays, str(t)), glays)
        weights.append(w)
        ada_l = ada[l]
        if t + 2 < len(stages):
            pending[t + 2] = gather_begin(t + 2, x)
            ada_l = ada_l + pending[t + 2][3][0, 0]
        if g == 1:
            wf = {GATHERED[i][0]: wi for i, wi in zip(entries(1), w)}
            x, sv = _mixer_fwd(x, ada_l, g_pre[l], g_post[l], wf, smalls[l])
        else:
            x, sv = _ffn_fwd(x, ada_l, g_pre[l], g_post[l], w[0], w[1], g)
        saved.append(sv)
    dx, loss_part = _loss_head(x, target)
    loss = lax.psum(loss_part[0, 0], ("x", "y", "c"))

    shard_grads = [[None] * depth for _ in GATHERED]
    small_grads = [{} for _ in range(depth)]
    ada_grads = [[None] * N_SUB for _ in range(depth)]
    gpre_grads = [[None] * N_SUB for _ in range(depth)]
    gpost_grads = [[None] * N_SUB for _ in range(depth)]
    states = {}

    def reduce_finish(t, after):
        l, g = stages[t]
        glays = [lays[i] for i in entries(g)]
        for i, grad in zip(entries(g), _reduce_end(states.pop(t), after, glays, sel, str(t))):
            shard_grads[i][l] = grad

    token = None
    for t in reversed(range(len(stages))):
        l, g = stages[t]
        ada_l = ada[l] if token is None else ada[l] + token[0, 0]
        if g == 1:
            wf = {GATHERED[i][0]: wi for i, wi in zip(entries(1), weights[t])}
            dx, wgrads, small, dada, dgp, dgq = _mixer_bwd(dx, saved[t], ada_l, g_pre[l], g_post[l], wf, smalls[l])
            small_grads[l].update(small)
        else:
            dx, dw_in, dw_out, dada, dgp, dgq = _ffn_bwd(dx, saved[t], ada_l, g_pre[l], g_post[l], weights[t][0], weights[t][1], g)
            wgrads = [dw_in, dw_out]
        ada_grads[l][g], gpre_grads[l][g], gpost_grads[l][g] = dada, dgp, dgq
        states[t] = _reduce_begin(wgrads, [lays[i] for i in entries(g)], sel, str(t))
        token = states[t][4]
        if t + 2 in states:
            reduce_finish(t + 2, dx)
    for l in range(depth):
        small_grads[l]["d_ada"] = jnp.concatenate([p for g in range(N_SUB) for p in ada_grads[l][g]], axis=1).reshape(-1)
        small_grads[l]["g_pre"] = jnp.concatenate(gpre_grads[l], axis=0)
        small_grads[l]["g_post"] = jnp.concatenate(gpost_grads[l], axis=0)

    stack = lambda key: _pad_rows(jnp.concatenate([small_grads[l][key] for l in range(depth)], axis=0), 8)
    gs = np.zeros((NST, 128), np.float32)
    gs[np.arange(NST), np.arange(NST) // GP] = 1.0
    dlr, dli, dldt = _ssm_prep_bwd(lam_re, lam_im, log_dt_x, stack("abr"), stack("abi"), stack("fr"), stack("fi"), jnp.asarray(gs))
    part = {
        "d_ada": jnp.stack([small_grads[l]["d_ada"] for l in range(depth)]),
        "g_pre": jnp.stack([small_grads[l]["g_pre"] for l in range(depth)]),
        "g_post": jnp.stack([small_grads[l]["g_post"] for l in range(depth)]),
        "conv_w": jnp.stack([small_grads[l]["conv_w"] for l in range(depth)]),
        "lam_re": dlr[:depth].reshape(depth, G, GP), "lam_im": dli[:depth].reshape(depth, G, GP), "log_dt": dldt[:depth, :G],
        "ssm_b_re": jnp.stack([jnp.transpose(_unembed(small_grads[l]["b_re"], G), (0, 2, 1)) for l in range(depth)]),
        "ssm_b_im": jnp.stack([jnp.transpose(_unembed(small_grads[l]["b_im"], G), (0, 2, 1)) for l in range(depth)]),
        "ssm_c_re": jnp.stack([jnp.transpose(_unembed(small_grads[l]["c_re"], G), (0, 2, 1)) for l in range(depth)]),
        "ssm_c_im": jnp.stack([jnp.transpose(_unembed(small_grads[l]["c_im"], G), (0, 2, 1)) for l in range(depth)]),
        "ssm_d": jnp.stack([small_grads[l]["ssm_d"][0] for l in range(depth)]),
        "w_pool": jnp.stack([_unembed(small_grads[l]["w_pool"], len(POOL_WINDOWS)) for l in range(depth)]),
        "pool_scale": jnp.stack([small_grads[l]["pool_scale"][0] for l in range(depth)]),
    }
    small_shapes = [part[k].shape for k in SMALL_ORDER]
    blocks = _allgather8(_pack([part[k] for k in SMALL_ORDER]), "gather_small_grads").reshape(N_DEV, -1, 128)
    small_sum = _sum_parts([blocks[i] for i in range(N_DEV)], F32, "small_grad_sum")
    total = dict(zip(SMALL_ORDER, _unpack(small_sum, small_shapes)))
    for t in sorted(states, reverse=True):
        reduce_finish(t, small_sum)

    grads = {}
    for name in sorted({e[0] for e in GATHERED}):
        cols = [shard_grads[i] for i, e in enumerate(GATHERED) if e[0] == name]
        grads[name] = jnp.stack(cols[0]) if len(cols) == 1 else jnp.stack([jnp.stack(pair) for pair in zip(*cols)])
    d_ada_all = jnp.stack([_unpack(blocks[i], small_shapes[:1])[0] for i in range(N_DEV)])
    d_cols = lax.dynamic_slice(d_ada_all, (0, 0, chip * ADA_SHARD), (N_DEV, depth, ADA_SHARD))
    d_cols = jnp.transpose(d_cols, (1, 0, 2))
    grads["w_ada"] = _ada_wgrad(c_pad, jnp.concatenate([d_cols, jnp.zeros_like(d_cols)], axis=1))
    grads["b_ada"] = total["d_ada"]
    grads["g_pre"] = lax.dynamic_slice(total["g_pre"], (0, 0, chip * W), (depth, 3, W))
    grads["g_post"] = lax.dynamic_slice(total["g_post"], (0, 0, chip * W), (depth, 3, W))
    grads["conv_w"] = lax.dynamic_slice(total["conv_w"], (0, 0, chip * (W // N_CHIP)), (depth, 3, W // N_CHIP))
    for k in SMALL_ORDER[4:]:
        grads[k] = total[k]

    out = {"loss": loss, "grad_x": dx[None]}
    for name in WEIGHTS:
        out["grad_" + name] = grads[name]
        out["delta_" + name], out["new_m_" + name], out["new_v_" + name] = _adamw(a[name], grads[name], a["m_" + name], a["v_" + name])
    return out


def kernel(x, c, w_ada, b_ada, g_pre, g_post, w_ff_in, w_ff_out, w_in, conv_w, w_conv_out, lam_re, lam_im, log_dt, ssm_b_re, ssm_b_im, ssm_c_re, ssm_c_im, ssm_d, w_glu, w_pool, pool_scale, w_pool_out, w_sb_out, w_out, loss_target, m_w_ada, m_b_ada, m_g_pre, m_g_post, m_w_ff_in, m_w_ff_out, m_w_in, m_conv_w, m_w_conv_out, m_lam_re, m_lam_im, m_log_dt, m_ssm_b_re, m_ssm_b_im, m_ssm_c_re, m_ssm_c_im, m_ssm_d, m_w_glu, m_w_pool, m_pool_scale, m_w_pool_out, m_w_sb_out, m_w_out, v_w_ada, v_b_ada, v_g_pre, v_g_post, v_w_ff_in, v_w_ff_out, v_w_in, v_conv_w, v_w_conv_out, v_lam_re, v_lam_im, v_log_dt, v_ssm_b_re, v_ssm_b_im, v_ssm_c_re, v_ssm_c_im, v_ssm_d, v_w_glu, v_w_pool, v_pool_scale, v_w_pool_out, v_w_sb_out, v_w_out):
    out = _step(dict(locals()))
    names = ["loss", "grad_x"] + [p + n for p in ("grad_", "delta_", "new_m_", "new_v_") for n in WEIGHTS]
    return tuple(out[n] for n in names)
```

```python
import functools
import math

import jax
import jax.numpy as jnp
import numpy as np
from jax import lax
from jax.experimental import pallas as pl
from jax.experimental.pallas import tpu as pltpu

F32 = jnp.float32
BF16 = jnp.bfloat16
MESH = pl.DeviceIdType.MESH

D = 1024
W = 256
FF = 2816
IN_COLS = 6144
G = 16
GH = 16
GP = 64
NST = G * GP
QB = 128
HEADS = 4
HD = 64
EPS = 1e-6
LAMBDA_RE_MAX = -1e-4
POOL_WINDOWS = (2, 4, 8, 16)
N_CHIP = 4
N_DEV = 8
VMEM_LIMIT = 56 * 1024 * 1024
HIGH = lax.Precision.HIGHEST

ADAM_LR, ADAM_B1, ADAM_B2, ADAM_EPS, ADAM_WD, ADAM_STEP = 0.001, 0.9, 0.999, 1e-08, 0.01, 10


def _call(body, **kw):
    return pl.pallas_call(body, **kw)


def _params(dims=None, **kw):
    return pltpu.CompilerParams(dimension_semantics=dims, vmem_limit_bytes=VMEM_LIMIT, **kw)


def _sds(shape, dtype):
    return jax.ShapeDtypeStruct(shape, dtype)


def _dot(a, b, ca=1, cb=0, precision=None):
    return lax.dot_general(a, b, (((ca,), (cb,)), ((), ())), preferred_element_type=F32, precision=precision)


def _bdot(a, b, ca=1, cb=0):
    return _dot(a.astype(BF16), b.astype(BF16), ca, cb)


def _sigmoid(x):
    return 1.0 / (1.0 + jnp.exp(-x))


def _mm(a, b, *, M, N, K, tm, tn, tk=None, ta=False, tb=False, out_dtype=F32, a_off=(0, 0), b_off=(0, 0), name):
    tk = K if tk is None else tk
    nk = K // tk
    assert M % tm == 0 and N % tn == 0 and K % tk == 0

    def body(a_ref, b_ref, o_ref, *acc):
        part = _bdot(a_ref[...], b_ref[...], 0 if ta else 1, 1 if tb else 0)
        if nk == 1:
            o_ref[...] = part.astype(out_dtype)
            return
        acc_ref = acc[0]
        k = pl.program_id(2)

        @pl.when(k == 0)
        def _():
            acc_ref[...] = part

        @pl.when(k > 0)
        def _():
            acc_ref[...] += part

        @pl.when(k == nk - 1)
        def _():
            o_ref[...] = acc_ref[...].astype(out_dtype)

    if ta:
        a_spec = pl.BlockSpec((tk, tm), lambda i, j, k: (k + a_off[0], i + a_off[1]))
    else:
        a_spec = pl.BlockSpec((tm, tk), lambda i, j, k: (i + a_off[0], k + a_off[1]))
    if tb:
        b_spec = pl.BlockSpec((tn, tk), lambda i, j, k: (j + b_off[0], k + b_off[1]))
    else:
        b_spec = pl.BlockSpec((tk, tn), lambda i, j, k: (k + b_off[0], j + b_off[1]))
    return _call(
        body, name=name, grid=(M // tm, N // tn, nk),
        in_specs=[a_spec, b_spec],
        out_specs=pl.BlockSpec((tm, tn), lambda i, j, k: (i, j)),
        out_shape=_sds((M, N), out_dtype),
        scratch_shapes=[] if nk == 1 else [pltpu.VMEM((tm, tn), F32)],
        compiler_params=_params(("parallel", "parallel", "arbitrary")),
    )(a, b)


def _row_tile(L):
    return min(L, 256)


def _norm_mod(x, g, shift, scale):
    L = x.shape[0]
    tr = _row_tile(L)

    def body(x_ref, g_ref, sh_ref, sc_ref, h_ref):
        xv = x_ref[...]
        r = lax.rsqrt(jnp.mean(xv * xv, axis=-1, keepdims=True) + EPS)
        h_ref[...] = (xv * r * g_ref[...] * (1.0 + sc_ref[...]) + sh_ref[...]).astype(BF16)

    row = pl.BlockSpec((tr, D), lambda i: (i, 0))
    vec = pl.BlockSpec((1, D), lambda i: (0, 0))
    return _call(body, name="norm_mod", grid=(L // tr,), in_specs=[row, vec, vec, vec], out_specs=row,
                 out_shape=_sds((L, D), BF16), compiler_params=_params(("parallel",)))(x, g, shift, scale)


def _norm_mod_bwd(dh, x, g, scale, dx_res):
    L = x.shape[0]
    tr = _row_tile(L)

    def body(dh_ref, x_ref, g_ref, sc_ref, dxr_ref, dx_ref, dsh_ref, dsc_ref, dg_ref):
        i = pl.program_id(0)
        xv = x_ref[...]
        dhv = dh_ref[...]
        r = lax.rsqrt(jnp.mean(xv * xv, axis=-1, keepdims=True) + EPS)
        y = xv * r
        n = y * g_ref[...]
        dn = dhv * (1.0 + sc_ref[...])
        dy = dn * g_ref[...]
        dx_ref[...] = dxr_ref[...] + r * (dy - y * jnp.mean(dy * y, axis=-1, keepdims=True))

        @pl.when(i == 0)
        def _():
            dsh_ref[...] = jnp.zeros_like(dsh_ref)
            dsc_ref[...] = jnp.zeros_like(dsc_ref)
            dg_ref[...] = jnp.zeros_like(dg_ref)

        dsh_ref[...] += jnp.sum(dhv, axis=0, keepdims=True)
        dsc_ref[...] += jnp.sum(dhv * n, axis=0, keepdims=True)
        dg_ref[...] += jnp.sum(dn * y, axis=0, keepdims=True)

    row = pl.BlockSpec((tr, D), lambda i: (i, 0))
    vec = pl.BlockSpec((1, D), lambda i: (0, 0))
    return _call(body, name="norm_mod_bwd", grid=(L // tr,), in_specs=[row, row, vec, vec, row],
                 out_specs=[row, vec, vec, vec],
                 out_shape=[_sds((L, D), F32), _sds((1, D), F32), _sds((1, D), F32), _sds((1, D), F32)],
                 compiler_params=_params(("arbitrary",)))(dh, x, g, scale, dx_res)


def _post(x, f, g, gate, res_weight):
    L = x.shape[0]
    tr = _row_tile(L)

    def body(x_ref, f_ref, g_ref, gt_ref, o_ref):
        fv = f_ref[...]
        r = lax.rsqrt(jnp.mean(fv * fv, axis=-1, keepdims=True) + EPS)
        o_ref[...] = x_ref[...] + (res_weight * (1.0 + gt_ref[...])) * (fv * r * g_ref[...])

    row = pl.BlockSpec((tr, D), lambda i: (i, 0))
    vec = pl.BlockSpec((1, D), lambda i: (0, 0))
    return _call(body, name="post", grid=(L // tr,), in_specs=[row, row, vec, vec], out_specs=row,
                 out_shape=_sds((L, D), F32), compiler_params=_params(("parallel",)))(x, f, g, gate)


def _post_bwd(dx, f, g, gate, res_weight):
    L = dx.shape[0]
    tr = _row_tile(L)

    def body(dx_ref, f_ref, g_ref, gt_ref, df_ref, dgt_ref, dg_ref):
        i = pl.program_id(0)
        fv = f_ref[...]
        dxv = dx_ref[...]
        r = lax.rsqrt(jnp.mean(fv * fv, axis=-1, keepdims=True) + EPS)
        y = fv * r
        dn = dxv * (res_weight * (1.0 + gt_ref[...]))
        dy = dn * g_ref[...]
        df_ref[...] = (r * (dy - y * jnp.mean(dy * y, axis=-1, keepdims=True))).astype(BF16)

        @pl.when(i == 0)
        def _():
            dgt_ref[...] = jnp.zeros_like(dgt_ref)
            dg_ref[...] = jnp.zeros_like(dg_ref)

        dgt_ref[...] += res_weight * jnp.sum(dxv * (y * g_ref[...]), axis=0, keepdims=True)
        dg_ref[...] += jnp.sum(dn * y, axis=0, keepdims=True)

    row = pl.BlockSpec((tr, D), lambda i: (i, 0))
    vec = pl.BlockSpec((1, D), lambda i: (0, 0))
    return _call(body, name="post_bwd", grid=(L // tr,), in_specs=[row, row, vec, vec],
                 out_specs=[row, vec, vec],
                 out_shape=[_sds((L, D), BF16), _sds((1, D), F32), _sds((1, D), F32)],
                 compiler_params=_params(("arbitrary",)))(dx, f, g, gate)


def _loss_head(x, target):
    L = x.shape[0]
    tr = _row_tile(L)

    def body(x_ref, t_ref, dx_ref, loss_ref):
        i = pl.program_id(0)
        err = x_ref[...] - t_ref[...]
        dx_ref[...] = err * (1.0 / D)

        @pl.when(i == 0)
        def _():
            loss_ref[...] = jnp.zeros_like(loss_ref)

        loss_ref[...] += 0.5 * jnp.sum(jnp.mean(err * err, axis=-1, keepdims=True), axis=0, keepdims=True)

    row = pl.BlockSpec((tr, D), lambda i: (i, 0))
    return _call(body, name="loss_head", grid=(L // tr,), in_specs=[row, row],
                 out_specs=[row, pl.BlockSpec((1, 1), lambda i: (0, 0))],
                 out_shape=[_sds((L, D), F32), _sds((1, 1), F32)],
                 compiler_params=_params(("arbitrary",)))(x, target)


def _ffn_in(h, w_in):
    L = h.shape[0]
    tm, tn = min(L, 1024), 256
    nf = FF // tn

    def body(h_ref, wa_ref, wb_ref, a_ref, b_ref, act_ref):
        hv = h_ref[...]
        a = _dot(hv, wa_ref[...])
        b = _dot(hv, wb_ref[...])
        a_ref[...] = a
        b_ref[...] = b
        act_ref[...] = (a * _sigmoid(a) * b).astype(BF16)

    tile = pl.BlockSpec((tm, tn), lambda i, j: (i, j))
    return _call(body, name="ffn_in", grid=(L // tm, nf),
                 in_specs=[pl.BlockSpec((tm, D), lambda i, j: (i, 0)),
                           pl.BlockSpec((D, tn), lambda i, j: (0, j)),
                           pl.BlockSpec((D, tn), lambda i, j: (0, j + nf))],
                 out_specs=[tile, tile, tile],
                 out_shape=[_sds((L, FF), F32), _sds((L, FF), F32), _sds((L, FF), BF16)],
                 compiler_params=_params(("parallel", "parallel")))(h, w_in, w_in)


def _ffn_mid_bwd(df, w_out, a, b):
    L = df.shape[0]
    tm, tn = min(L, 1024), 256

    def body(df_ref, w_ref, a_ref, b_ref, da_ref, db_ref):
        dact = _dot(df_ref[...], w_ref[...], 1, 1)
        av = a_ref[...]
        sg = _sigmoid(av)
        da_ref[...] = (dact * b_ref[...] * (sg * (1.0 + av * (1.0 - sg)))).astype(BF16)
        db_ref[...] = (dact * (av * sg)).astype(BF16)

    tile = pl.BlockSpec((tm, tn), lambda i, j: (i, j))
    return _call(body, name="ffn_mid_bwd", grid=(L // tm, FF // tn),
                 in_specs=[pl.BlockSpec((tm, D), lambda i, j: (i, 0)),
                           pl.BlockSpec((tn, D), lambda i, j: (j, 0)), tile, tile],
                 out_specs=[tile, tile],
                 out_shape=[_sds((L, FF), BF16), _sds((L, FF), BF16)],
                 compiler_params=_params(("parallel", "parallel")))(df, w_out, a, b)


def _rows_before(ref, i, tr, halo):
    start = pl.multiple_of(jnp.maximum(i * tr - halo, 0), 8)
    return jnp.where(i > 0, ref[pl.ds(start, halo), :], 0.0)


def _rows_after(ref, i, n, tr, halo):
    start = pl.multiple_of(jnp.minimum((i + 1) * tr, (n - 1) * tr), 8)
    return jnp.where(i < n - 1, ref[pl.ds(start, halo), :], 0.0)


def _conv_fwd(p, conv_w):
    L = p.shape[0]
    tr = _row_tile(L)
    n = L // tr

    def body(bg_ref, cg_ref, v_ref, w_ref, za_ref, u_scr):
        i = pl.program_id(0)

        @pl.when(i == 0)
        def _():
            u_scr[...] = cg_ref[...] * v_ref[...]

        r0 = pl.multiple_of(i * tr, 8)
        ext = jnp.concatenate([_rows_before(u_scr, i, tr, 8), u_scr[pl.ds(r0, tr), :]], axis=0)
        w = w_ref[...]
        y = (w[0:1] * pltpu.roll(ext, 2, axis=0) + w[1:2] * pltpu.roll(ext, 1, axis=0) + w[2:3] * ext)[8:, :]
        za_ref[...] = (bg_ref[pl.ds(r0, tr), :] * y).astype(BF16)

    col = lambda c: pl.BlockSpec((L, W), lambda i: (0, c))
    return _call(body, name="conv_fwd", grid=(n,),
                 in_specs=[col(0), col(1), col(2), pl.BlockSpec((3, W), lambda i: (0, 0))],
                 out_specs=pl.BlockSpec((tr, W), lambda i: (i, 0)),
                 out_shape=_sds((L, W), BF16),
                 scratch_shapes=[pltpu.VMEM((L, W), F32)],
                 compiler_params=_params(("arbitrary",)))(p, p, p, conv_w)


def _conv_bwd(p, conv_w, dza):
    L = p.shape[0]
    tr = _row_tile(L)
    n = L // tr

    def body(bg_ref, cg_ref, v_ref, w_ref, dza_ref, dp_ref, dw_ref, u_scr, dy_scr):
        i = pl.program_id(0)

        @pl.when(i == 0)
        def _():
            u_scr[...] = cg_ref[...] * v_ref[...]
            dy_scr[...] = dza_ref[...] * bg_ref[...]
            dw_ref[...] = jnp.zeros_like(dw_ref)

        r0 = pl.multiple_of(i * tr, 8)
        w = w_ref[...]
        ext = jnp.concatenate([_rows_before(u_scr, i, tr, 8), u_scr[pl.ds(r0, tr), :]], axis=0)
        u2 = pltpu.roll(ext, 2, axis=0)[8:, :]
        u1 = pltpu.roll(ext, 1, axis=0)[8:, :]
        u0 = ext[8:, :]
        y = w[0:1] * u2 + w[1:2] * u1 + w[2:3] * u0
        dy = dy_scr[pl.ds(r0, tr), :]
        dext = jnp.concatenate([dy, _rows_after(dy_scr, i, n, tr, 8)], axis=0)
        m = tr + 8
        du = (w[2:3] * dext + w[1:2] * pltpu.roll(dext, m - 1, axis=0) + w[0:1] * pltpu.roll(dext, m - 2, axis=0))[:tr, :]
        dp_ref[:, 0:W] = (dza_ref[pl.ds(r0, tr), :] * y).astype(BF16)
        dp_ref[:, W:2 * W] = (du * v_ref[pl.ds(r0, tr), :]).astype(BF16)
        dp_ref[:, 2 * W:3 * W] = (du * cg_ref[pl.ds(r0, tr), :]).astype(BF16)
        dw_ref[...] += jnp.concatenate([jnp.sum(dy * u2, axis=0, keepdims=True),
                                        jnp.sum(dy * u1, axis=0, keepdims=True),
                                        jnp.sum(dy * u0, axis=0, keepdims=True)], axis=0)

    col = lambda c: pl.BlockSpec((L, W), lambda i: (0, c))
    return _call(body, name="conv_bwd", grid=(n,),
                 in_specs=[col(0), col(1), col(2), pl.BlockSpec((3, W), lambda i: (0, 0)),
                           pl.BlockSpec((L, W), lambda i: (0, 0))],
                 out_specs=[pl.BlockSpec((tr, 3 * W), lambda i: (i, 0)), pl.BlockSpec((3, W), lambda i: (0, 0))],
                 out_shape=[_sds((L, 3 * W), BF16), _sds((3, W), F32)],
                 scratch_shapes=[pltpu.VMEM((L, W), F32), pltpu.VMEM((L, W), F32)],
                 compiler_params=_params(("arbitrary",)))(p, p, p, conv_w, dza)


def _pool_windows(lane):
    wins = jnp.zeros(lane.shape, jnp.int32)
    for gi, w in enumerate(POOL_WINDOWS):
        wins = jnp.where(lane // (W // len(POOL_WINDOWS)) == gi, w, wins)
    return wins


def _pooled_block(u_ref, i, tr):
    r0 = pl.multiple_of(i * tr, 8)
    cur = u_ref[pl.ds(r0, tr), :]
    ext = jnp.concatenate([_rows_before(u_ref, i, tr, 16), cur], axis=0)
    s2 = ext + pltpu.roll(ext, 1, axis=0)
    s4 = s2 + pltpu.roll(s2, 2, axis=0)
    s8 = s4 + pltpu.roll(s4, 4, axis=0)
    s16 = s8 + pltpu.roll(s8, 8, axis=0)
    lane = lax.broadcasted_iota(jnp.int32, (tr, W), 1)
    wins = _pool_windows(lane)
    win_sum = jnp.where(wins == 2, s2[16:], jnp.where(wins == 4, s4[16:], jnp.where(wins == 8, s8[16:], s16[16:])))
    t = lax.broadcasted_iota(jnp.int32, (tr, W), 0) + i * tr
    cnt = jnp.minimum(t + 1, wins).astype(F32)
    return win_sum / cnt - cur, cnt


def _pool_fwd(p, w_pool_bd, pool_scale):
    L = p.shape[0]
    tr = _row_tile(L)

    def body(u_ref, w_ref, sc_ref, zc_ref):
        pooled, _ = _pooled_block(u_ref, pl.program_id(0), tr)
        zc_ref[...] = (_bdot(pooled, w_ref[...]) * sc_ref[...]).astype(BF16)

    return _call(body, name="pool_fwd", grid=(L // tr,),
                 in_specs=[pl.BlockSpec((L, W), lambda i: (0, 4)), pl.BlockSpec((W, W), lambda i: (0, 0)),
                           pl.BlockSpec((1, W), lambda i: (0, 0))],
                 out_specs=pl.BlockSpec((tr, W), lambda i: (i, 0)), out_shape=_sds((L, W), BF16),
                 compiler_params=_params(("arbitrary",)))(p, w_pool_bd, pool_scale)


def _pool_bwd(p, w_pool_bd, pool_scale, dzc):
    L = p.shape[0]
    tr = _row_tile(L)
    n = L // tr

    def body(u_ref, w_ref, sc_ref, dzc_ref, du_ref, dw_ref, dsc_ref, g_scr):
        i = pl.program_id(0)

        @pl.when(i == 0)
        def _():
            dw_ref[...] = jnp.zeros_like(dw_ref)
            dsc_ref[...] = jnp.zeros_like(dsc_ref)

            def rows(k, carry):
                r = pl.multiple_of(k * tr, 8)
                dmix = (dzc_ref[pl.ds(r, tr), :] * sc_ref[...]).astype(BF16)
                dpool = _dot(dmix, w_ref[...].astype(BF16), 1, 1)
                lane = lax.broadcasted_iota(jnp.int32, (tr, W), 1)
                t = lax.broadcasted_iota(jnp.int32, (tr, W), 0) + k * tr
                cnt = jnp.minimum(t + 1, _pool_windows(lane)).astype(F32)
                g_scr[pl.ds(r, tr), :] = dpool / cnt
                return carry

            lax.fori_loop(0, n, rows, 0)

        r0 = pl.multiple_of(i * tr, 8)
        pooled, cnt = _pooled_block(u_ref, i, tr)
        dzc = dzc_ref[pl.ds(r0, tr), :]
        mixed = _bdot(pooled, w_ref[...])
        dsc_ref[...] += jnp.sum(dzc * mixed, axis=0, keepdims=True)
        dmix = (dzc * sc_ref[...]).astype(BF16)
        dw_ref[...] += _dot(pooled.astype(BF16), dmix, 0, 0)
        gcur = g_scr[pl.ds(r0, tr), :]
        ext = jnp.concatenate([gcur, _rows_after(g_scr, i, n, tr, 16)], axis=0)
        m = tr + 16
        s2 = ext + pltpu.roll(ext, m - 1, axis=0)
        s4 = s2 + pltpu.roll(s2, m - 2, axis=0)
        s8 = s4 + pltpu.roll(s4, m - 4, axis=0)
        s16 = s8 + pltpu.roll(s8, m - 8, axis=0)
        lane = lax.broadcasted_iota(jnp.int32, (tr, W), 1)
        wins = _pool_windows(lane)
        ahead = jnp.where(wins == 2, s2[:tr], jnp.where(wins == 4, s4[:tr], jnp.where(wins == 8, s8[:tr], s16[:tr])))
        du_ref[...] = (ahead - gcur * cnt).astype(BF16)

    return _call(body, name="pool_bwd", grid=(n,),
                 in_specs=[pl.BlockSpec((L, W), lambda i: (0, 4)), pl.BlockSpec((W, W), lambda i: (0, 0)),
                           pl.BlockSpec((1, W), lambda i: (0, 0)), pl.BlockSpec((L, W), lambda i: (0, 0))],
                 out_specs=[pl.BlockSpec((tr, W), lambda i: (i, 0)), pl.BlockSpec((W, W), lambda i: (0, 0)),
                            pl.BlockSpec((1, W), lambda i: (0, 0))],
                 out_shape=[_sds((L, W), BF16), _sds((W, W), F32), _sds((1, W), F32)],
                 scratch_shapes=[pltpu.VMEM((L, W), F32)],
                 compiler_params=_params(("arbitrary",)))(p, w_pool_bd, pool_scale, dzc)


SSM_SLAB = 512


def _ssm_prep(lam_re, lam_im, log_dt_x):
    def body(lr_ref, li_ref, ldt_ref, abr_ref, abi_ref, fr_ref, fi_ref):
        lr = jnp.minimum(lr_ref[...], LAMBDA_RE_MAX)
        li = li_ref[...]
        dt = jnp.exp(ldt_ref[...])
        mag = jnp.exp(lr * dt)
        abr = mag * jnp.cos(li * dt)
        abi = mag * jnp.sin(li * dt)
        den = lr * lr + li * li
        nr = abr - 1.0
        abr_ref[...] = abr
        abi_ref[...] = abi
        fr_ref[...] = (nr * lr + abi * li) / den
        fi_ref[...] = (abi * lr - nr * li) / den

    shp = _sds(lam_re.shape, F32)
    return _call(body, name="ssm_prep", out_shape=[shp, shp, shp, shp], compiler_params=_params())(lam_re, lam_im, log_dt_x)


def _ssm_prep_bwd(lam_re, lam_im, log_dt_x, g_abr, g_abi, g_fr, g_fi, group_sum):
    def body(lr_ref, li_ref, ldt_ref, gar_ref, gai_ref, gfr_ref, gfi_ref, gs_ref, dlr_ref, dli_ref, dldt_ref):
        lam = lr_ref[...]
        lr = jnp.minimum(lam, LAMBDA_RE_MAX)
        li = li_ref[...]
        dt = jnp.exp(ldt_ref[...])
        mag = jnp.exp(lr * dt)
        abr = mag * jnp.cos(li * dt)
        abi = mag * jnp.sin(li * dt)
        den = lr * lr + li * li
        nr = abr - 1.0
        fr = (nr * lr + abi * li) / den
        fi = (abi * lr - nr * li) / den
        d_nre = gfr_ref[...] / den
        d_nim = gfi_ref[...] / den
        d_den = -(gfr_ref[...] * fr + gfi_ref[...] * fi) / den
        d_abr = gar_ref[...] + d_nre * lr - d_nim * li
        d_abi = gai_ref[...] + d_nre * li + d_nim * lr
        d_lr = d_nre * nr + d_nim * abi + d_den * 2.0 * lr
        d_li = d_nre * abi - d_nim * nr + d_den * 2.0 * li
        d_mag = d_abr * jnp.cos(li * dt) + d_abi * jnp.sin(li * dt)
        d_th = -d_abr * abi + d_abi * abr
        d_lr = d_lr + d_mag * mag * dt
        d_li = d_li + d_th * dt
        d_dt = d_mag * mag * lr + d_th * li
        passes = jnp.where(lam < LAMBDA_RE_MAX, 1.0, jnp.where(lam == LAMBDA_RE_MAX, 0.5, 0.0))
        dlr_ref[...] = d_lr * passes
        dli_ref[...] = d_li
        dldt_ref[...] = _dot(d_dt * dt, gs_ref[...], precision=HIGH)

    shp = _sds(lam_re.shape, F32)
    return _call(body, name="ssm_prep_bwd", out_shape=[shp, shp, _sds((lam_re.shape[0], 128), F32)],
                 compiler_params=_params())(lam_re, lam_im, log_dt_x, g_abr, g_abi, g_fr, g_fi, group_sum)


def _cmul(ar, ai, br, bi):
    return ar * br - ai * bi, ar * bi + ai * br


def _powers(ar, ai):
    out = [(ar, ai)]
    for _ in range(7):
        out.append(_cmul(out[-1][0], out[-1][1], ar, ai))
    return out


def _scan_rows(s_re, s_im, ar, ai, L, reverse=False, visit=None, visit_init=None):
    n = s_re.shape[1]
    pw = _powers(ar, ai)
    row = lax.broadcasted_iota(jnp.int32, (8, n), 0)
    dist = (8 - row) if reverse else (row + 1)
    pr = jnp.zeros((8, n), F32)
    pi = jnp.zeros((8, n), F32)
    for k in range(8):
        pr = jnp.where(dist == k + 1, pw[k][0], pr)
        pi = jnp.where(dist == k + 1, pw[k][1], pi)
    nb = L // 8

    def blk(t, carry):
        cr, ci, acc = carry
        b = (nb - 1 - t) if reverse else t
        r0 = pl.multiple_of(b * 8, 8)
        xr = s_re[pl.ds(r0, 8), :]
        xi = s_im[pl.ds(r0, 8), :]
        for d in (1, 2, 4):
            if reverse:
                keep = row < 8 - d
                sr, si = pltpu.roll(xr, 8 - d, axis=0), pltpu.roll(xi, 8 - d, axis=0)
            else:
                keep = row >= d
                sr, si = pltpu.roll(xr, d, axis=0), pltpu.roll(xi, d, axis=0)
            sr = jnp.where(keep, sr, 0.0)
            si = jnp.where(keep, si, 0.0)
            mr, mi = _cmul(pw[d - 1][0], pw[d - 1][1], sr, si)
            xr, xi = xr + mr, xi + mi
        mr, mi = _cmul(pr, pi, cr, ci)
        xr, xi = xr + mr, xi + mi
        s_re[pl.ds(r0, 8), :] = xr
        s_im[pl.ds(r0, 8), :] = xi
        if visit is not None:
            acc = visit(b, xr, xi, acc)
        if reverse:
            return xr[0:1, :], xi[0:1, :], acc
        return xr[7:8, :], xi[7:8, :], acc

    zero = jnp.zeros((1, n), F32)
    return lax.fori_loop(0, nb, blk, (zero, zero, visit_init if visit is not None else 0))[2]


def _ssm_project(u_ref, wbr, wbi, s_re, s_im, L):
    ch = min(L, 256)

    def rows(k, carry):
        r = pl.multiple_of(k * ch, 8)
        ub = u_ref[pl.ds(r, ch), :].astype(BF16)
        s_re[pl.ds(r, ch), :] = _dot(ub, wbr)
        s_im[pl.ds(r, ch), :] = _dot(ub, wbi)
        return carry

    lax.fori_loop(0, L // ch, rows, 0)


def _gelu(y):
    c = math.sqrt(2.0 / math.pi)
    return 0.5 * y * (1.0 + jnp.tanh(c * (y + 0.044715 * y * y * y)))


def _gelu_grad(y):
    c = math.sqrt(2.0 / math.pi)
    th = jnp.tanh(c * (y + 0.044715 * y * y * y))
    return 0.5 * (1.0 + th) + 0.5 * y * (1.0 - th * th) * c * (1.0 + 3.0 * 0.044715 * y * y)


def _ssm_fwd(p, b_re_bd, b_im_bd, c_re_bd, c_im_bd, abr, abi, fr, fi, d_skip):
    L = p.shape[0]
    ns = NST // SSM_SLAB
    ch = min(L, 256)

    def body(u_ref, br_ref, bi_ref, cr_ref, ci_ref, abr_ref, abi_ref, fr_ref, fi_ref, d_ref,
             y_ref, zb_ref, s_re, s_im):
        j = pl.program_id(0)
        f_re, f_im = fr_ref[...], fi_ref[...]
        wbr = (f_re * br_ref[...] - f_im * bi_ref[...]).astype(BF16)
        wbi = (f_re * bi_ref[...] + f_im * br_ref[...]).astype(BF16)
        _ssm_project(u_ref, wbr, wbi, s_re, s_im, L)
        _scan_rows(s_re, s_im, abr_ref[...], abi_ref[...], L)
        crb = cr_ref[...].astype(BF16)
        cib = ci_ref[...].astype(BF16)

        def rows(k, carry):
            r = pl.multiple_of(k * ch, 8)
            part = _dot(s_re[pl.ds(r, ch), :].astype(BF16), crb) - _dot(s_im[pl.ds(r, ch), :].astype(BF16), cib)

            @pl.when(j == 0)
            def _():
                y_ref[pl.ds(r, ch), :] = part + d_ref[...] * u_ref[pl.ds(r, ch), :]

            @pl.when(j > 0)
            def _():
                y_ref[pl.ds(r, ch), :] += part

            @pl.when(j == ns - 1)
            def _():
                zb_ref[pl.ds(r, ch), :] = _gelu(y_ref[pl.ds(r, ch), :]).astype(BF16)

            return carry

        lax.fori_loop(0, L // ch, rows, 0)

    full = lambda shape: pl.BlockSpec(shape, lambda j: (0, 0))
    lanes = pl.BlockSpec((1, SSM_SLAB), lambda j: (0, j))
    return _call(body, name="ssm_fwd", grid=(ns,),
                 in_specs=[pl.BlockSpec((L, W), lambda j: (0, 3)),
                           pl.BlockSpec((W, SSM_SLAB), lambda j: (0, j)), pl.BlockSpec((W, SSM_SLAB), lambda j: (0, j)),
                           pl.BlockSpec((SSM_SLAB, W), lambda j: (j, 0)), pl.BlockSpec((SSM_SLAB, W), lambda j: (j, 0)),
                           lanes, lanes, lanes, lanes, full((1, W))],
                 out_specs=[full((L, W)), full((L, W))],
                 out_shape=[_sds((L, W), F32), _sds((L, W), BF16)],
                 scratch_shapes=[pltpu.VMEM((L, SSM_SLAB), F32), pltpu.VMEM((L, SSM_SLAB), F32)],
                 compiler_params=_params(("arbitrary",)))(p, b_re_bd, b_im_bd, c_re_bd, c_im_bd, abr, abi, fr, fi, d_skip)


def _ssm_bwd(p, y, dzb, b_re_bd, b_im_bd, c_re_bd, c_im_bd, abr, abi, fr, fi, d_skip):
    L = p.shape[0]
    ns = NST // SSM_SLAB
    ch = min(L, 256)
    n_ch = L // ch

    def body(u_ref, y_ref, dzb_ref, br_ref, bi_ref, cr_ref, ci_ref, abr_ref, abi_ref, fr_ref, fi_ref, d_ref,
             du_ref, dd_ref, dbr_ref, dbi_ref, dcr_ref, dci_ref, gar_ref, gai_ref, gfr_ref, gfi_ref,
             s_re, s_im, l_re, l_im, dy_scr, du_scr):
        j = pl.program_id(0)
        f_re, f_im = fr_ref[...], fi_ref[...]
        b_re, b_im = br_ref[...], bi_ref[...]
        wbr = (f_re * b_re - f_im * b_im).astype(BF16)
        wbi = (f_re * b_im + f_im * b_re).astype(BF16)
        a_re, a_im = abr_ref[...], abi_ref[...]

        @pl.when(j == 0)
        def _():
            def rows(k, acc):
                r = pl.multiple_of(k * ch, 8)
                dy = dzb_ref[pl.ds(r, ch), :] * _gelu_grad(y_ref[pl.ds(r, ch), :])
                dy_scr[pl.ds(r, ch), :] = dy
                du_scr[pl.ds(r, ch), :] = d_ref[...] * dy
                return acc + jnp.sum(dy * u_ref[pl.ds(r, ch), :], axis=0, keepdims=True)

            dd_ref[...] = lax.fori_loop(0, n_ch, rows, jnp.zeros((1, W), F32))

        _ssm_project(u_ref, wbr, wbi, s_re, s_im, L)
        _scan_rows(s_re, s_im, a_re, a_im, L)
        crb = cr_ref[...].astype(BF16)
        cib = ci_ref[...].astype(BF16)

        def rows_c(k, acc):
            dcr, dci = acc
            r = pl.multiple_of(k * ch, 8)
            dyb = dy_scr[pl.ds(r, ch), :].astype(BF16)
            dcr = dcr + _dot(s_re[pl.ds(r, ch), :].astype(BF16), dyb, 0, 0)
            dci = dci - _dot(s_im[pl.ds(r, ch), :].astype(BF16), dyb, 0, 0)
            l_re[pl.ds(r, ch), :] = _dot(dyb, crb, 1, 1)
            l_im[pl.ds(r, ch), :] = -_dot(dyb, cib, 1, 1)
            return dcr, dci

        zc = jnp.zeros((SSM_SLAB, W), F32)
        dcr, dci = lax.fori_loop(0, n_ch, rows_c, (zc, zc))
        dcr_ref[...] = dcr
        dci_ref[...] = dci

        row8 = lax.broadcasted_iota(jnp.int32, (8, SSM_SLAB), 0)

        def visit(b, lr, li, acc):
            ar_acc, ai_acc = acc
            r0 = pl.multiple_of(b * 8, 8)
            rp = pl.multiple_of(jnp.maximum(b * 8 - 8, 0), 8)
            has_prev = b > 0
            pr = jnp.where(has_prev, s_re[pl.ds(rp, 8), :][7:8, :], 0.0)
            pi = jnp.where(has_prev, s_im[pl.ds(rp, 8), :][7:8, :], 0.0)
            sr = jnp.where(row8 >= 1, pltpu.roll(s_re[pl.ds(r0, 8), :], 1, axis=0), pr)
            si = jnp.where(row8 >= 1, pltpu.roll(s_im[pl.ds(r0, 8), :], 1, axis=0), pi)
            return ar_acc + lr * sr + li * si, ai_acc - lr * si + li * sr

        z8 = jnp.zeros((8, SSM_SLAB), F32)
        ar_acc, ai_acc = _scan_rows(l_re, l_im, a_re, -a_im, L, reverse=True, visit=visit, visit_init=(z8, z8))
        gar_ref[...] = jnp.sum(ar_acc, axis=0, keepdims=True)
        gai_ref[...] = jnp.sum(ai_acc, axis=0, keepdims=True)

        def rows_b(k, acc):
            dwr, dwi = acc
            r = pl.multiple_of(k * ch, 8)
            ub = u_ref[pl.ds(r, ch), :].astype(BF16)
            lrb = l_re[pl.ds(r, ch), :].astype(BF16)
            lib = l_im[pl.ds(r, ch), :].astype(BF16)
            du_scr[pl.ds(r, ch), :] += _dot(lrb, wbr, 1, 1) + _dot(lib, wbi, 1, 1)
            return dwr + _dot(ub, lrb, 0, 0), dwi + _dot(ub, lib, 0, 0)

        zb = jnp.zeros((W, SSM_SLAB), F32)
        dwr, dwi = lax.fori_loop(0, n_ch, rows_b, (zb, zb))
        dbr_ref[...] = dwr * f_re + dwi * f_im
        dbi_ref[...] = -dwr * f_im + dwi * f_re
        gfr_ref[...] = jnp.sum(dwr * b_re + dwi * b_im, axis=0, keepdims=True)
        gfi_ref[...] = jnp.sum(-dwr * b_im + dwi * b_re, axis=0, keepdims=True)

        @pl.when(j == ns - 1)
        def _():
            du_ref[...] = du_scr[...].astype(BF16)

    full = lambda shape: pl.BlockSpec(shape, lambda j: (0, 0))
    lanes = pl.BlockSpec((1, SSM_SLAB), lambda j: (0, j))
    bspec = pl.BlockSpec((W, SSM_SLAB), lambda j: (0, j))
    cspec = pl.BlockSpec((SSM_SLAB, W), lambda j: (j, 0))
    slab = lambda: pltpu.VMEM((L, SSM_SLAB), F32)
    return _call(body, name="ssm_bwd", grid=(ns,),
                 in_specs=[pl.BlockSpec((L, W), lambda j: (0, 3)), full((L, W)), full((L, W)),
                           bspec, bspec, cspec, cspec, lanes, lanes, lanes, lanes, full((1, W))],
                 out_specs=[full((L, W)), full((1, W)), bspec, bspec, cspec, cspec, lanes, lanes, lanes, lanes],
                 out_shape=[_sds((L, W), BF16), _sds((1, W), F32), _sds((W, NST), F32), _sds((W, NST), F32),
                            _sds((NST, W), F32), _sds((NST, W), F32)] + [_sds((1, NST), F32)] * 4,
                 scratch_shapes=[slab(), slab(), slab(), slab(), pltpu.VMEM((L, W), F32), pltpu.VMEM((L, W), F32)],
                 compiler_params=_params(("arbitrary",)))(p, y, dzb, b_re_bd, b_im_bd, c_re_bd, c_im_bd,
                                                          abr, abi, fr, fi, d_skip)


SB_KB = 512


SB_SUB = 256


def _split2(x):
    hi = x.astype(BF16)
    return hi, (x - hi.astype(F32)).astype(BF16)


def _ones_dot(x, ones):
    n = x.shape[0]
    r = _dot(jnp.concatenate(_split2(x), axis=0), ones)
    return r[:n] + r[n:]


def _suffix_sums(x, tri):
    sub = tri.shape[0]
    parts = [_ones_dot(x[:, i:i + sub], tri) for i in range(0, x.shape[1], sub)]
    out, after = [], None
    for p in reversed(parts):
        out.append(p if after is None else p + after)
        after = p[:, 0:1] if after is None else after + p[:, 0:1]
    return jnp.concatenate(out[::-1], axis=1)


def _prefix_sums_exclusive(x, tri_le):
    sub = tri_le.shape[0]
    out, before = [], None
    for i in range(0, x.shape[1], sub):
        xi = x[:, i:i + sub]
        inc = _ones_dot(xi, tri_le)
        out.append(inc - xi if before is None else inc - xi + before)
        before = inc[:, sub - 1:sub] if before is None else before + inc[:, sub - 1:sub]
    return jnp.concatenate(out, axis=1)


def _sb_block(q, kj, i, jb, kb, right, tri):
    z = _bdot(q, kj, 1, 1)
    t_idx = lax.broadcasted_iota(jnp.int32, (QB, kb), 0) + i * QB
    s_idx = lax.broadcasted_iota(jnp.int32, (QB, kb), 1) + jb * kb
    mask = s_idx < t_idx
    lk_all = jnp.minimum(-z, 0.0) - jnp.log1p(jnp.exp(-jnp.abs(z)))
    lk = jnp.where(mask, lk_all, 0.0)
    suf = _suffix_sums(lk, tri)
    a = jnp.where(mask, jnp.exp((lk_all + z) + (suf - lk) + right), 0.0)
    return z, mask, suf, a


def _sb_ones(kb):
    sub = min(SB_SUB, kb)
    r = lax.broadcasted_iota(jnp.int32, (sub, sub), 0)
    c = lax.broadcasted_iota(jnp.int32, (sub, sub), 1)
    return (r >= c).astype(BF16), (r <= c).astype(BF16)


def _sb_fwd(q, k, v):
    L = q.shape[1]
    kb = min(SB_KB, L)
    per = kb // QB

    def body(q_ref, k_ref, v_ref, o_ref, rs_ref):
        i = pl.program_id(0)
        tri, _ = _sb_ones(kb)
        lane = lax.broadcasted_iota(jnp.int32, (QB, 128), 1)
        qs = [q_ref[h] for h in range(HEADS)]

        def step(t, carry):
            accs, rights, sums = carry
            jb = i // per - t
            r = pl.multiple_of(jb * kb, kb)
            out = []
            for h in range(HEADS):
                _, _, suf, a = _sb_block(qs[h], k_ref[h, pl.ds(r, kb), :], i, jb, kb, rights[h], tri)
                tot = suf[:, 0:1]
                out.append((accs[h] + _bdot(a, v_ref[h, pl.ds(r, kb), :]), rights[h] + tot,
                            sums[h] + jnp.where(lane == jb, tot, 0.0)))
            return tuple(o[0] for o in out), tuple(o[1] for o in out), tuple(o[2] for o in out)

        init = (tuple(jnp.zeros((QB, HD), F32) for _ in range(HEADS)), tuple(jnp.zeros((QB, 1), F32) for _ in range(HEADS)),
                tuple(jnp.zeros((QB, 128), F32) for _ in range(HEADS)))
        accs, _, sums = lax.fori_loop(0, i // per + 1, step, init)
        for h in range(HEADS):
            o_ref[h] = accs[h]
            rs_ref[h] = sums[h]

    heads = pl.BlockSpec((HEADS, L, HD), lambda i: (0, 0, 0))
    blk = pl.BlockSpec((HEADS, QB, HD), lambda i: (0, i, 0))
    return _call(body, name="sb_fwd", grid=(L // QB,), in_specs=[blk, heads, heads],
                 out_specs=[blk, pl.BlockSpec((HEADS, QB, 128), lambda i: (0, i, 0))],
                 out_shape=[_sds((HEADS, L, HD), F32), _sds((HEADS, L, 128), F32)],
                 compiler_params=_params(("parallel",)))(q, k, v)


def _sb_bwd(q, k, v, do, block_sums):
    L = q.shape[1]
    kb = min(SB_KB, L)
    per = kb // QB

    def body(q_ref, k_ref, v_ref, do_ref, rs_ref, dq_ref, dk_ref, dv_ref):
        i = pl.program_id(0)
        tri, tri_le = _sb_ones(kb)
        lane = lax.broadcasted_iota(jnp.int32, (QB, 128), 1)

        @pl.when(i == 0)
        def _():
            dk_ref[...] = jnp.zeros_like(dk_ref)
            dv_ref[...] = jnp.zeros_like(dv_ref)

        qs = [q_ref[h] for h in range(HEADS)]
        dos = [do_ref[h] for h in range(HEADS)]
        sums = [rs_ref[h] for h in range(HEADS)]

        def step(jb, carry):
            dqs, lefts = carry
            r = pl.multiple_of(jb * kb, kb)
            out = []
            for h in range(HEADS):
                kj = k_ref[h, pl.ds(r, kb), :]
                vj = v_ref[h, pl.ds(r, kb), :]
                right = jnp.sum(jnp.where(lane > jb, sums[h], 0.0), axis=1, keepdims=True)
                z, mask, _, a = _sb_block(qs[h], kj, i, jb, kb, right, tri)
                e = a * _bdot(dos[h], vj, 1, 1)
                dv_ref[h, pl.ds(r, kb), :] += _dot(a.astype(BF16), dos[h].astype(BF16), 0, 0)
                before = lefts[h] + _prefix_sums_exclusive(e, tri_le)
                sg = _sigmoid(z)
                dz = jnp.where(mask, e * (1.0 - sg) - sg * before, 0.0).astype(BF16)
                dk_ref[h, pl.ds(r, kb), :] += _dot(dz, qs[h].astype(BF16), 0, 0)
                out.append((dqs[h] + _dot(dz, kj.astype(BF16)), lefts[h] + jnp.sum(e, axis=1, keepdims=True)))
            return tuple(o[0] for o in out), tuple(o[1] for o in out)

        init = (tuple(jnp.zeros((QB, HD), F32) for _ in range(HEADS)), tuple(jnp.zeros((QB, 1), F32) for _ in range(HEADS)))
        dqs, _ = lax.fori_loop(0, i // per + 1, step, init)
        for h in range(HEADS):
            dq_ref[h] = dqs[h]

    heads = pl.BlockSpec((HEADS, L, HD), lambda i: (0, 0, 0))
    blk = pl.BlockSpec((HEADS, QB, HD), lambda i: (0, i, 0))
    shp = _sds((HEADS, L, HD), F32)
    return _call(body, name="sb_bwd", grid=(L // QB,),
                 in_specs=[blk, heads, heads, blk, pl.BlockSpec((HEADS, QB, 128), lambda i: (0, i, 0))],
                 out_specs=[blk, heads, heads], out_shape=[shp, shp, shp],
                 compiler_params=_params(("arbitrary",)))(q, k, v, do, block_sums)


def _merge_fwd(za, zb, zc, zd, p, w_conv_out, w_glu, w_pool_out, w_sb_out):
    L = za.shape[0]
    tm = _row_tile(L)

    def body(za_ref, zb_ref, zc_ref, zd_ref, g0, g1, g2, g3, wc_ref, wg_ref, wp_ref, ws_ref, o_ref):
        glu = _dot(zb_ref[...], wg_ref[...])
        yb = glu[:, :D] * _sigmoid(glu[:, D:])
        m = _sigmoid(g0[...]) * _dot(za_ref[...], wc_ref[...])
        m = m + _sigmoid(g1[...]) * yb
        m = m + _sigmoid(g2[...]) * _dot(zc_ref[...], wp_ref[...])
        m = m + _sigmoid(g3[...]) * _dot(zd_ref[...], ws_ref[...])
        o_ref[...] = m.astype(BF16)

    zt = pl.BlockSpec((tm, W), lambda i: (i, 0))
    gate = lambda b: pl.BlockSpec((tm, D), lambda i: (i, 2 + b))
    wfull = lambda n: pl.BlockSpec((W, n), lambda i: (0, 0))
    return _call(body, name="merge_fwd", grid=(L // tm,),
                 in_specs=[zt, zt, zt, zt, gate(0), gate(1), gate(2), gate(3), wfull(D), wfull(2 * D), wfull(D), wfull(D)],
                 out_specs=pl.BlockSpec((tm, D), lambda i: (i, 0)), out_shape=_sds((L, D), BF16),
                 compiler_params=_params(("parallel",)))(za, zb, zc, zd, p, p, p, p, w_conv_out, w_glu, w_pool_out, w_sb_out)


def _merge_bwd(dm, za, zb, zc, zd, p, w_conv_out, w_glu, w_pool_out, w_sb_out):
    L = za.shape[0]
    tm = _row_tile(L)
    n = L // tm

    def body(dm_ref, za_ref, zb_ref, zc_ref, zd_ref, g0, g1, g2, g3, wc_ref, wg_ref, wp_ref, ws_ref,
             dza_ref, dzb_ref, dzc_ref, dzd_ref, dg_ref, dwc_ref, dwg_ref, dwp_ref, dws_ref,
             awc, awg, awp, aws):
        i = pl.program_id(0)

        @pl.when(i == 0)
        def _():
            awc[...] = jnp.zeros_like(awc)
            awg[...] = jnp.zeros_like(awg)
            awp[...] = jnp.zeros_like(awp)
            aws[...] = jnp.zeros_like(aws)

        dmv = dm_ref[...]

        def gated(g_ref, y, col):
            s = _sigmoid(g_ref[...])
            dg_ref[:, col * D:(col + 1) * D] = (dmv * y * s * (1.0 - s)).astype(BF16)
            return (dmv * s)

        def linear(z_ref, w_ref, acc, dz_ref, col, g_ref):
            zv = z_ref[...]
            dy = gated(g_ref, _dot(zv, w_ref[...]), col).astype(BF16)
            dz_ref[...] = _dot(dy, w_ref[...], 1, 1)
            acc[...] += _dot(zv, dy, 0, 0)

        linear(za_ref, wc_ref, awc, dza_ref, 0, g0)
        linear(zc_ref, wp_ref, awp, dzc_ref, 2, g2)
        linear(zd_ref, ws_ref, aws, dzd_ref, 3, g3)
        zbv = zb_ref[...]
        glu = _dot(zbv, wg_ref[...])
        ga = glu[:, :D]
        sg = _sigmoid(glu[:, D:])
        dyb = gated(g1, ga * sg, 1)
        dga = (dyb * sg).astype(BF16)
        dgg = (dyb * ga * sg * (1.0 - sg)).astype(BF16)
        dzb_ref[...] = _dot(dga, wg_ref[:, :D], 1, 1) + _dot(dgg, wg_ref[:, D:], 1, 1)
        awg[:, :D] += _dot(zbv, dga, 0, 0)
        awg[:, D:] += _dot(zbv, dgg, 0, 0)

        @pl.when(i == n - 1)
        def _():
            dwc_ref[...] = awc[...].astype(BF16)
            dwg_ref[...] = awg[...].astype(BF16)
            dwp_ref[...] = awp[...].astype(BF16)
            dws_ref[...] = aws[...].astype(BF16)

    zt = pl.BlockSpec((tm, W), lambda i: (i, 0))
    gate = lambda b: pl.BlockSpec((tm, D), lambda i: (i, 2 + b))
    wfull = lambda n_: pl.BlockSpec((W, n_), lambda i: (0, 0))
    zs = _sds((L, W), F32)
    return _call(body, name="merge_bwd", grid=(n,),
                 in_specs=[pl.BlockSpec((tm, D), lambda i: (i, 0)), zt, zt, zt, zt, gate(0), gate(1), gate(2), gate(3),
                           wfull(D), wfull(2 * D), wfull(D), wfull(D)],
                 out_specs=[zt, zt, zt, zt, pl.BlockSpec((tm, 4 * D), lambda i: (i, 0)),
                            wfull(D), wfull(2 * D), wfull(D), wfull(D)],
                 out_shape=[zs, zs, zs, zs, _sds((L, 4 * D), BF16),
                            _sds((W, D), BF16), _sds((W, 2 * D), BF16), _sds((W, D), BF16), _sds((W, D), BF16)],
                 scratch_shapes=[pltpu.VMEM((W, D), F32), pltpu.VMEM((W, 2 * D), F32), pltpu.VMEM((W, D), F32),
                                 pltpu.VMEM((W, D), F32)],
                 compiler_params=_params(("arbitrary",)))(dm, za, zb, zc, zd, p, p, p, p,
                                                          w_conv_out, w_glu, w_pool_out, w_sb_out)


def _adam_math(w, g, m, v):
    m2 = ADAM_B1 * m + (1.0 - ADAM_B1) * g
    v2 = ADAM_B2 * v + (1.0 - ADAM_B2) * (g * g)
    m_hat = m2 / (1.0 - ADAM_B1 ** ADAM_STEP)
    v_hat = v2 / (1.0 - ADAM_B2 ** ADAM_STEP)
    return -ADAM_LR * (m_hat / (jnp.sqrt(v_hat) + ADAM_EPS) + ADAM_WD * w), m2, v2


def _as_rows(a):
    return a.reshape(-1, a.shape[-1])


def _adamw(w, g, m, v):
    shape = w.shape
    w2, g2, m2, v2 = _as_rows(w), _as_rows(g), _as_rows(m), _as_rows(v)
    R, C = w2.shape
    tr = R
    for cand in (1024, 512, 256, 128, 64, 32, 16, 8):
        if R % cand == 0 and cand * C * 4 <= 2 * 1024 * 1024:
            tr = cand
            break

    def body(w_ref, g_ref, m_ref, v_ref, d_ref, m_out, v_out):
        d, mn, vn = _adam_math(w_ref[...], g_ref[...], m_ref[...], v_ref[...])
        d_ref[...] = d
        m_out[...] = mn
        v_out[...] = vn

    blk = pl.BlockSpec((tr, C), lambda i: (i, 0))
    shp = _sds((R, C), F32)
    outs = _call(body, name="adamw", grid=(R // tr,), in_specs=[blk] * 4, out_specs=[blk] * 3, out_shape=[shp] * 3,
                 compiler_params=_params(("parallel",)))(w2, g2, m2, v2)
    return tuple(o.reshape(shape) for o in outs)


def _sum_parts(parts, out_dtype, name):
    shape = parts[0].shape
    flat = [_as_rows(a) for a in parts]
    R, C = flat[0].shape
    tr = R
    for cand in (1024, 512, 256, 128, 64, 32, 16):
        if R % cand == 0 and cand * C * 4 <= 2 * 1024 * 1024:
            tr = cand
            break
    k = len(parts)

    def body(*refs):
        acc = refs[0][...].astype(F32)
        for r in refs[1:k]:
            acc = acc + r[...].astype(F32)
        refs[k][...] = acc.astype(out_dtype)

    blk = pl.BlockSpec((tr, C), lambda i: (i, 0))
    out = _call(body, name=name, grid=(R // tr,), in_specs=[blk] * k, out_specs=blk, out_shape=_sds((R, C), out_dtype),
                compiler_params=_params(("parallel",)))(*flat)
    return out.reshape(shape)


ADA_SHARD = 9 * D // N_CHIP
ADA_TN = 768


def _ada_fwd(c_pad, w_ada, b_ada_cols):
    depth = w_ada.shape[0]

    def body(c_ref, w_ref, b_ref, o_ref):
        cv = c_ref[...]
        o_ref[...] = _bdot(cv * _sigmoid(cv), w_ref[...]) + b_ref[...]

    return _call(body, name="ada_fwd", grid=(depth, ADA_SHARD // ADA_TN),
                 in_specs=[pl.BlockSpec((16, D), lambda l, j: (0, 0)),
                           pl.BlockSpec((None, D, ADA_TN), lambda l, j: (l, 0, j)),
                           pl.BlockSpec((None, 1, ADA_TN), lambda l, j: (l, 0, j))],
                 out_specs=pl.BlockSpec((None, 16, ADA_TN), lambda l, j: (l, 0, j)),
                 out_shape=_sds((depth, 16, ADA_SHARD), F32),
                 compiler_params=_params(("parallel", "parallel")))(c_pad, w_ada, b_ada_cols)


def _ada_wgrad(c_pad, d_ada):
    depth = d_ada.shape[0]

    def body(c_ref, d_ref, o_ref):
        cv = c_ref[...]
        o_ref[...] = _bdot(cv * _sigmoid(cv), d_ref[...], 0, 0)

    return _call(body, name="ada_wgrad", grid=(depth, ADA_SHARD // ADA_TN),
                 in_specs=[pl.BlockSpec((16, D), lambda l, j: (0, 0)),
                           pl.BlockSpec((None, 16, ADA_TN), lambda l, j: (l, 0, j))],
                 out_specs=pl.BlockSpec((None, D, ADA_TN), lambda l, j: (l, 0, j)),
                 out_shape=_sds((depth, D, ADA_SHARD), F32),
                 compiler_params=_params(("parallel", "parallel")))(c_pad, d_ada)


HBM_SPEC = pl.BlockSpec(memory_space=pltpu.HBM)


def _place():
    x, y, c = lax.axis_index("x"), lax.axis_index("y"), lax.axis_index("c")
    peers = [(1 - x, y), (x, 1 - y), (1 - x, 1 - y)]
    return x, y, c, peers


def _chip(px, py):
    return 2 * px + py


def _allgather8(block, name):
    m_per, n = block.shape

    def body(x_ref, out_ref, send_sems, recv_sems, local_sem):
        x, y, c, chips = _place()
        me, sibling = (x, y, c), (x, y, 1 - c)

        def rows(px, py, pc):
            return out_ref.at[pl.ds(pl.multiple_of((4 * px + 2 * py + pc) * m_per, 8), m_per), :]

        def copy(k, blk, to, src=None):
            return pltpu.make_async_remote_copy(
                src_ref=rows(*blk) if src is None else src, dst_ref=rows(*blk),
                send_sem=send_sems.at[k], recv_sem=recv_sems.at[k], device_id=to, device_id_type=MESH)

        mine = pltpu.make_async_copy(x_ref, rows(*me), local_sem)
        mine.start()
        first = [copy(0, me, sibling, src=x_ref)]
        first += [copy(1 + j, me, (*chip, c), src=x_ref) for j, chip in enumerate(chips)]
        for cp in first:
            cp.start()
        passed = [copy(4 + j, (*chip, c), sibling) for j, chip in enumerate(chips)]
        for j, chip in enumerate(chips):
            copy(1 + j, (*chip, c), me).wait_recv()
            passed[j].start()
        copy(0, sibling, me).wait_recv()
        for j, chip in enumerate(chips):
            copy(4 + j, (*chip, 1 - c), me).wait_recv()
        for cp in first + passed:
            cp.wait_send()
        mine.wait()

    return _call(body, name=name, out_shape=_sds((N_DEV * m_per, n), block.dtype),
                 in_specs=[pl.BlockSpec(memory_space=pltpu.VMEM)], out_specs=pl.BlockSpec(memory_space=pltpu.VMEM),
                 scratch_shapes=[pltpu.SemaphoreType.DMA((7,)), pltpu.SemaphoreType.DMA((7,)), pltpu.SemaphoreType.DMA],
                 compiler_params=_params())(block)


GATHERED = (("w_ff_in", 0, -1, 0), ("w_ff_out", 0, -2, 0),
            ("w_in", None, -1, 1), ("w_conv_out", None, -1, 1), ("w_glu", None, -1, 1), ("w_pool_out", None, -1, 1),
            ("w_sb_out", None, -1, 1), ("w_out", None, -2, 1),
            ("w_ff_in", 1, -1, 2), ("w_ff_out", 1, -2, 2))
N_SUB = 3


def _lead(ref):
    return (slice(None),) * (len(ref.shape) - 2)


def _mo(v, m):
    return v if isinstance(v, int) else pl.multiple_of(v, m)


def _full_region(ref, axis, j, half, shard_shape):
    rs, cs = shard_shape[-2], shard_shape[-1]
    if axis == -1:
        r0, nr = (0, rs) if half is None else (half * (rs // 2), rs // 2)
        return ref.at[_lead(ref) + (pl.ds(_mo(r0, 16), nr), pl.ds(_mo(j * cs, 128), cs))]
    r0, nr = (j * rs, rs) if half is None else (j * rs + half * (rs // 2), rs // 2)
    return ref.at[_lead(ref) + (pl.ds(_mo(r0, 16), nr), slice(None))]


def _shard_half(ref, half):
    rs = ref.shape[-2]
    return ref.at[_lead(ref) + (pl.ds(_mo(half * (rs // 2), 16), rs // 2), slice(None))]


def _full_shape(shard_shape, axis):
    s = list(shard_shape)
    s[axis] *= N_CHIP
    return tuple(s)


class _Lay:
    def __init__(self, shard_shape, axis):
        self.axis = axis
        self.shard_shape = tuple(shard_shape)
        self.full_shape = _full_shape(shard_shape, axis)
        self.lead = int(np.prod(shard_shape[:-2]))
        self.rs, self.cs = shard_shape[-2], shard_shape[-1]
        self.hr = self.rs // 2
        self.tr = next(t for t in (512, 256, 128, 64, 32, 16) if self.hr % t == 0 and t * self.cs * 4 <= (1 << 21))
        self.half_rows_shape = _half_rows_shape(self.full_shape)
        self.half_shard_shape = _half_rows_shape(self.shard_shape)

    def full(self, jf, hf):
        if self.axis == -1:
            return ((self.lead, 2, self.hr, N_CHIP * self.cs),
                    pl.BlockSpec((None, None, self.tr, self.cs), lambda b, j, i, s: (b, hf(j, s), i, jf(j, s))))
        return ((self.lead, N_CHIP, 2, self.hr, self.cs),
                pl.BlockSpec((None, None, None, self.tr, self.cs), lambda b, j, i, s: (b, jf(j, s), hf(j, s), i, 0)))

    def half_shard(self):
        return (self.lead, self.hr, self.cs), pl.BlockSpec((None, self.tr, self.cs), lambda b, j, i, s: (b, i, 0))

    def shard(self, hf):
        return ((self.lead, 2, self.hr, self.cs),
                pl.BlockSpec((None, None, self.tr, self.cs), lambda b, j, i, s: (b, hf(j, s), i, 0)))


def _view_sum(sel, operands, out_view, out_shape, out_dtype, grid, name):
    k = len(operands)

    def body(sel_ref, *refs):
        acc = refs[0][...].astype(F32)
        for r in refs[1:k]:
            acc = acc + r[...].astype(F32)
        refs[k][...] = acc.astype(out_dtype)

    spec = pltpu.PrefetchScalarGridSpec(num_scalar_prefetch=1, grid=grid, in_specs=[v[1] for _, v in operands],
                                        out_specs=out_view[1])
    out = _call(body, name=name, grid_spec=spec, out_shape=_sds(out_view[0], out_dtype),
                compiler_params=_params(("parallel", "parallel", "parallel")))(
                    sel, *[a.reshape(v[0]) for a, v in operands])
    return out.reshape(out_shape)


def _sel_core(j, s):
    return s[0]


def _sel_chip(j, s):
    return s[1]


def _grid_j(j, s):
    return j


def _place_shard(lay, sel, w):
    return _view_sum(sel, [(w, lay.shard(_grid_j))], lay.full(_sel_chip, _grid_j), lay.full_shape, BF16,
                     (lay.lead, 2, lay.hr // lay.tr), "place_shard")


SEM_SPEC = pl.BlockSpec(memory_space=pltpu.SEMAPHORE)
ANY_SPEC = pl.BlockSpec(memory_space=pl.ANY)
SPLIT_COPY = pltpu.SideEffectType.DATAFLOW_SIDE_EFFECTING


def _in_hbm(a):
    return pltpu.with_memory_space_constraint(a, pltpu.HBM)


def _gather_copies(lays, bufs, send_sems, recv_sems):
    x, y, c, chips = _place()
    copies = []
    for a, lay in enumerate(lays):
        own = _full_region(bufs[a], lay.axis, _chip(x, y), c, lay.shard_shape)
        for k, chip in enumerate(chips):
            copies.append(pltpu.make_async_remote_copy(
                src_ref=own, dst_ref=own, send_sem=send_sems.at[a * 3 + k], recv_sem=recv_sems.at[a * 3 + k],
                device_id=(*chip, c), device_id_type=MESH))
    return copies


def _gather_start(fulls, after, lays, tag):
    n = len(fulls)

    def body(*refs):
        send_sems, recv_sems = refs[n + 1], refs[n + 2]
        bufs, token = refs[n + 3:2 * n + 3], refs[2 * n + 3]
        for cp in _gather_copies(lays, bufs, send_sems, recv_sems):
            cp.start()
        token[...] = jnp.zeros_like(token)

    outs = _call(body, name="gather_start_" + tag,
                 out_shape=[pltpu.SemaphoreType.DMA((3 * n,)), pltpu.SemaphoreType.DMA((3 * n,))]
                 + [pltpu.HBM(f.shape, f.dtype) for f in fulls] + [_sds((8, 128), F32)],
                 in_specs=[HBM_SPEC] * n + [ANY_SPEC],
                 out_specs=[SEM_SPEC, SEM_SPEC] + [HBM_SPEC] * n + [pl.BlockSpec(memory_space=pltpu.VMEM)],
                 input_output_aliases={a: a + 2 for a in range(n)},
                 compiler_params=pltpu.CompilerParams(has_side_effects=SPLIT_COPY))(*[_in_hbm(f) for f in fulls], after)
    return outs[0], outs[1], outs[2:2 + n], outs[2 + n]


def _gather_wait(send_sems, recv_sems, bufs, after, lays, tag):
    n = len(bufs)

    def body(*refs):
        ss, rs = refs[n], refs[n + 1]
        for cp in _gather_copies(lays, refs[n + 3:], ss, rs):
            cp.wait_send()
            cp.wait_recv()

    return _call(body, name="gather_wait_" + tag,
                 out_shape=[pltpu.HBM(b.shape, b.dtype) for b in bufs],
                 in_specs=[HBM_SPEC] * n + [SEM_SPEC, SEM_SPEC, ANY_SPEC], out_specs=[HBM_SPEC] * n,
                 input_output_aliases={a: a for a in range(n)},
                 compiler_params=pltpu.CompilerParams(has_side_effects=SPLIT_COPY))(*bufs, send_sems, recv_sems, after)


def _gather_forward(bufs, lays):
    n = len(bufs)

    def body(*refs):
        outs = refs[n:2 * n]
        send_sems, recv_sems = refs[2 * n:]
        x, y, c, chips = _place()
        sibling = (x, y, 1 - c)
        sends = []
        for a in range(n):
            for k, chip in enumerate(chips):
                landed = _full_region(outs[a], lays[a].axis, _chip(*chip), c, lays[a].shard_shape)
                cp = pltpu.make_async_remote_copy(
                    src_ref=landed, dst_ref=landed, send_sem=send_sems.at[a * 3 + k], recv_sem=recv_sems.at[a * 3 + k],
                    device_id=sibling, device_id_type=MESH)
                cp.start()
                sends.append(cp)
        for a in range(n):
            for k, chip in enumerate(chips):
                passed = _full_region(outs[a], lays[a].axis, _chip(*chip), 1 - c, lays[a].shard_shape)
                pltpu.make_async_remote_copy(
                    src_ref=passed, dst_ref=passed, send_sem=send_sems.at[a * 3 + k], recv_sem=recv_sems.at[a * 3 + k],
                    device_id=sibling, device_id_type=MESH).wait_recv()
        for cp in sends:
            cp.wait_send()

    return _call(body, name="gather_forward",
                 out_shape=[_sds(b.shape, b.dtype) for b in bufs],
                 in_specs=[HBM_SPEC] * n, out_specs=[HBM_SPEC] * n,
                 input_output_aliases={a: a for a in range(n)},
                 scratch_shapes=[pltpu.SemaphoreType.DMA((3 * n,)), pltpu.SemaphoreType.DMA((3 * n,))],
                 compiler_params=_params())(*bufs)


def _half_rows_shape(full_shape):
    s = list(full_shape)
    s[-2] //= 2
    return tuple(s)


RELATIONS = tuple((r, s) for r in range(N_CHIP) for s in range(2))[1:]


def _peer(x, y, c, rel):
    r, s = rel
    return (1 - x if r in (1, 3) else x, 1 - y if r in (2, 3) else y, 1 - c if s else c)


def _reduce_copies(lays, grads, landing, send_sems, recv_sems):
    x, y, c, _ = _place()
    nr = len(RELATIONS)
    copies = []
    for a, lay in enumerate(lays):
        for k, rel in enumerate(RELATIONS):
            px, py, pc = _peer(x, y, c, rel)
            copies.append(pltpu.make_async_remote_copy(
                src_ref=_full_region(grads[a], lay.axis, _chip(px, py), pc, lay.shard_shape), dst_ref=landing[a * nr + k],
                send_sem=send_sems.at[a * nr + k], recv_sem=recv_sems.at[a * nr + k],
                device_id=(px, py, pc), device_id_type=MESH))
    return copies


def _reduce_start(grads, lays, tag):
    n, nr = len(grads), len(RELATIONS)
    landing = [_in_hbm(lax.empty(lay.half_shard_shape, BF16)) for lay in lays for _ in RELATIONS]
    m = n + n * nr

    def body(*refs):
        send_sems, recv_sems = refs[m], refs[m + 1]
        src, land, token = refs[m + 2:m + 2 + n], refs[m + 2 + n:2 * m + 2], refs[2 * m + 2]
        for cp in _reduce_copies(lays, src, land, send_sems, recv_sems):
            cp.start()
        token[...] = jnp.zeros_like(token)

    ops = [_in_hbm(g) for g in grads] + landing
    outs = _call(body, name="reduce_start_" + tag,
                 out_shape=[pltpu.SemaphoreType.DMA((n * nr,)), pltpu.SemaphoreType.DMA((n * nr,))]
                 + [pltpu.HBM(o.shape, o.dtype) for o in ops] + [_sds((8, 128), F32)],
                 in_specs=[HBM_SPEC] * m,
                 out_specs=[SEM_SPEC, SEM_SPEC] + [HBM_SPEC] * m + [pl.BlockSpec(memory_space=pltpu.VMEM)],
                 input_output_aliases={a: a + 2 for a in range(m)},
                 compiler_params=pltpu.CompilerParams(has_side_effects=SPLIT_COPY))(*ops)
    return outs[0], outs[1], outs[2:2 + n], outs[2 + n:2 + m], outs[2 + m]


def _reduce_wait(send_sems, recv_sems, grads, landing, after, lays, tag):
    n, nr = len(grads), len(RELATIONS)
    m = n + n * nr

    def body(*refs):
        ss, rs = refs[m], refs[m + 1]
        src, land = refs[m + 3:m + 3 + n], refs[m + 3 + n:]
        for cp in _reduce_copies(lays, src, land, ss, rs):
            cp.wait_send()
            cp.wait_recv()

    ops = list(grads) + list(landing)
    outs = _call(body, name="reduce_wait_" + tag,
                 out_shape=[pltpu.HBM(o.shape, o.dtype) for o in ops],
                 in_specs=[HBM_SPEC] * m + [SEM_SPEC, SEM_SPEC, ANY_SPEC], out_specs=[HBM_SPEC] * m,
                 input_output_aliases={a: a for a in range(m)},
                 compiler_params=pltpu.CompilerParams(has_side_effects=SPLIT_COPY))(*ops, send_sems, recv_sems, after)
    return outs[:n], [outs[n + nr * a:n + nr * a + nr] for a in range(n)]


def _share_halves(shards):
    n = len(shards)

    def body(*refs):
        outs = refs[n:2 * n]
        send_sems, recv_sems = refs[2 * n:]
        x, y, c, _ = _place()
        sibling = (x, y, 1 - c)
        started = []
        for a in range(n):
            mine = _shard_half(outs[a], c)
            rc = pltpu.make_async_remote_copy(src_ref=mine, dst_ref=mine, send_sem=send_sems.at[a],
                                              recv_sem=recv_sems.at[a], device_id=sibling, device_id_type=MESH)
            rc.start()
            started.append(rc)
        for rc in started:
            rc.wait_recv()
            rc.wait_send()

    return _call(body, name="share_halves", out_shape=[_sds(s.shape, F32) for s in shards],
                 in_specs=[HBM_SPEC] * n, out_specs=[HBM_SPEC] * n, input_output_aliases={a: a for a in range(n)},
                 scratch_shapes=[pltpu.SemaphoreType.DMA((n,)), pltpu.SemaphoreType.DMA((n,))],
                 compiler_params=_params())(*shards)


def _reduce_end(state, after, lays, sel, tag):
    send_sems, recv_sems, grads, landing, _ = state
    grads, landed = _reduce_wait(send_sems, recv_sems, grads, landing, after, lays, tag)
    halves = [
        _view_sum(sel, [(g, lay.full(_sel_chip, _sel_core))] + [(l, lay.half_shard()) for l in ls], lay.shard(_sel_core),
                  lay.shard_shape, F32, (lay.lead, 1, lay.hr // lay.tr), "shard_half_sum")
        for g, ls, lay in zip(grads, landed, lays)]
    return _share_halves(halves)


def _embed(blocks):
    n, r, c = blocks.shape
    eye = jnp.eye(n, dtype=blocks.dtype)
    return (blocks[:, :, None, :] * eye[:, None, :, None]).reshape(n * r, n * c)


def _unembed(mat, n):
    r, c = mat.shape[0] // n, mat.shape[1] // n
    return jnp.transpose(jnp.diagonal(mat.reshape(n, r, n, c), axis1=0, axis2=2), (2, 0, 1))


def _to_heads(a):
    return jnp.transpose(a.reshape(a.shape[0], HEADS, HD), (1, 0, 2))


def _from_heads(a):
    return jnp.transpose(a, (1, 0, 2)).reshape(a.shape[1], W)


def _row(v):
    return v.reshape(1, -1)


def _ffn_fwd(x, ada, gp, gq, w_in, w_out, s):
    L = x.shape[0]
    h = _norm_mod(x, _row(gp[s]), _row(ada[3 * s]), _row(ada[3 * s + 1]))
    a, b, act = _ffn_in(h, w_in)
    f = _mm(act, w_out, M=L, N=D, K=FF, tm=min(L, 1024), tn=512, name="ffn_out")
    x2 = _post(x, f, _row(gq[s]), _row(ada[3 * s + 2]), 0.5)
    return x2, (x, h, a, b, act, f)


def _ffn_bwd(dx, saved, ada, gp, gq, w_in, w_out, s):
    x, h, a, b, act, f = saved
    L = x.shape[0]
    df, dgate, dgq = _post_bwd(dx, f, _row(gq[s]), _row(ada[3 * s + 2]), 0.5)
    dw_out = _mm(act, df, M=FF, N=D, K=L, tm=256, tn=1024, ta=True, out_dtype=BF16, name="ffn_dw_out")
    da, db = _ffn_mid_bwd(df, w_out, a, b)
    du = jnp.concatenate([da, db], axis=1)
    dw_in = _mm(h, du, M=D, N=2 * FF, K=L, tm=1024, tn=512, ta=True, out_dtype=BF16, name="ffn_dw_in")
    dh = _mm(du, w_in, M=L, N=D, K=2 * FF, tm=min(L, 1024), tn=1024, tk=1408, tb=True, name="ffn_dh")
    dx2, dshift, dscale, dgp = _norm_mod_bwd(dh, x, _row(gp[s]), _row(ada[3 * s + 1]), dx)
    return dx2, dw_in, dw_out, (dshift, dscale, dgate), dgp, dgq


def _mixer_fwd(x, ada, gp, gq, wf, sm):
    L = x.shape[0]
    h = _norm_mod(x, _row(gp[1]), _row(ada[3]), _row(ada[4]))
    p = _mm(h, wf["w_in"], M=L, N=IN_COLS, K=D, tm=min(L, 1024), tn=512, name="mixer_in")
    za = _conv_fwd(p, sm["conv_w"])
    y, zb = _ssm_fwd(p, sm["b_re"], sm["b_im"], sm["c_re"], sm["c_im"], sm["abr"], sm["abi"], sm["fr"], sm["fi"], sm["ssm_d"])
    zc = _pool_fwd(p, sm["w_pool"], sm["pool_scale"])
    q = _to_heads(p[:, 5 * W:6 * W]) * (HD ** -0.5)
    k = _to_heads(p[:, 6 * W:7 * W])
    v = _to_heads(p[:, 7 * W:8 * W])
    o_heads, block_sums = _sb_fwd(q, k, v)
    zd = _from_heads(o_heads).astype(BF16)
    merged = _merge_fwd(za, zb, zc, zd, p, wf["w_conv_out"], wf["w_glu"], wf["w_pool_out"], wf["w_sb_out"])
    m = _mm(merged, wf["w_out"], M=L, N=D, K=D, tm=min(L, 1024), tn=512, name="mixer_out")
    x2 = _post(x, m, _row(gq[1]), _row(ada[5]), 1.0)
    return x2, (x, h, p, za, y, zb, zc, zd, q, k, v, block_sums, merged, m)


def _mixer_bwd(dx, saved, ada, gp, gq, wf, sm):
    x, h, p, za, y, zb, zc, zd, q, k, v, block_sums, merged, m = saved
    L = x.shape[0]
    dmf, dgate, dgq = _post_bwd(dx, m, _row(gq[1]), _row(ada[5]), 1.0)
    dw_out = _mm(merged, dmf, M=D, N=D, K=L, tm=512, tn=512, ta=True, out_dtype=BF16, name="mixer_dw_out")
    dmerged = _mm(dmf, wf["w_out"], M=L, N=D, K=D, tm=min(L, 1024), tn=512, tb=True, name="mixer_dmerged")
    dza, dzb, dzc, dzd, dgates, dwc, dwg, dwp, dws = _merge_bwd(
        dmerged, za, zb, zc, zd, p, wf["w_conv_out"], wf["w_glu"], wf["w_pool_out"], wf["w_sb_out"])
    dconv, dconv_w = _conv_bwd(p, sm["conv_w"], dza)
    (du_ssm, dd, dbr, dbi, dcr, dci, gar, gai, gfr, gfi) = _ssm_bwd(
        p, y, dzb, sm["b_re"], sm["b_im"], sm["c_re"], sm["c_im"], sm["abr"], sm["abi"], sm["fr"], sm["fi"], sm["ssm_d"])
    du_pool, dwpool, dpscale = _pool_bwd(p, sm["w_pool"], sm["pool_scale"], dzc)
    dq, dk, dv = _sb_bwd(q, k, v, _to_heads(dzd), block_sums)
    dqkv = [_from_heads(t).astype(BF16) for t in (dq * (HD ** -0.5), dk, dv)]
    dp = jnp.concatenate([dconv, du_ssm, du_pool] + dqkv + [dgates], axis=1)
    dw_in = _mm(h, dp, M=D, N=IN_COLS, K=L, tm=1024, tn=512, ta=True, out_dtype=BF16, name="mixer_dw_in")
    dh = _mm(dp, wf["w_in"], M=L, N=D, K=IN_COLS, tm=min(L, 1024), tn=1024, tk=1536, tb=True, name="mixer_dh")
    dx2, dshift, dscale, dgp = _norm_mod_bwd(dh, x, _row(gp[1]), _row(ada[4]), dx)
    wgrads = [dw_in, dwc, dwg, dwp, dws, dw_out]
    small = {"conv_w": dconv_w, "ssm_d": dd, "b_re": dbr, "b_im": dbi, "c_re": dcr, "c_im": dci,
             "abr": gar, "abi": gai, "fr": gfr, "fi": gfi, "w_pool": dwpool, "pool_scale": dpscale}
    return dx2, wgrads, small, (dshift, dscale, dgate), dgp, dgq


def _pack(arrays):
    flat = jnp.concatenate([a.reshape(-1) for a in arrays])
    rows = -(-flat.shape[0] // 128)
    rows = -(-rows // 64) * 64
    return jnp.pad(flat, (0, rows * 128 - flat.shape[0])).reshape(rows, 128)


def _unpack(block, shapes):
    flat = block.reshape(-1)
    out, off = [], 0
    for s in shapes:
        n = int(np.prod(s))
        out.append(flat[off:off + n].reshape(s))
        off += n
    return out


def _pad_rows(a, mult):
    rows = -(-a.shape[0] // mult) * mult
    return jnp.concatenate([a] * (-(-rows // a.shape[0])), axis=0)[:rows]


SMALL_ORDER = ("d_ada", "g_pre", "g_post", "conv_w", "lam_re", "lam_im", "log_dt", "ssm_b_re", "ssm_b_im",
               "ssm_c_re", "ssm_c_im", "ssm_d", "w_pool", "pool_scale")
WEIGHTS = ('w_ada', 'b_ada', 'g_pre', 'g_post', 'w_ff_in', 'w_ff_out', 'w_in', 'conv_w', 'w_conv_out', 'lam_re', 'lam_im',
           'log_dt', 'ssm_b_re', 'ssm_b_im', 'ssm_c_re', 'ssm_c_im', 'ssm_d', 'w_glu', 'w_pool', 'pool_scale', 'w_pool_out',
           'w_sb_out', 'w_out')


def _step(a):
    depth = a["w_ada"].shape[0]
    x = a["x"][0]
    target = a["loss_target"][0]
    L = x.shape[0]
    ix, iy, ic = lax.axis_index("x"), lax.axis_index("y"), lax.axis_index("c")
    chip = 2 * ix + iy
    me = 4 * ix + 2 * iy + ic
    sel = jnp.stack([ic, chip]).astype(jnp.int32)
    lays = [_Lay(a[name].shape[(1 if idx is None else 2):], ax) for name, idx, ax, _ in GATHERED]

    first_shapes = [(D,), (depth, 3, W), (depth, 3, W), (depth, 3, W // N_CHIP)]
    gathered = _allgather8(_pack([a["c"], a["g_pre"], a["g_post"], a["conv_w"]]), "gather_small_inputs")
    per_dev = [_unpack(blk, first_shapes) for blk in gathered.reshape(N_DEV, -1, 128)]
    c_all = jnp.stack([d[0] for d in per_dev])
    c_pad = jnp.concatenate([c_all, jnp.zeros_like(c_all)], axis=0)
    g_pre = jnp.concatenate([per_dev[2 * j][1] for j in range(N_CHIP)], axis=-1)
    g_post = jnp.concatenate([per_dev[2 * j][2] for j in range(N_CHIP)], axis=-1)
    conv_w = jnp.concatenate([per_dev[2 * j][3] for j in range(N_CHIP)], axis=-1)

    b_cols = lax.dynamic_slice(a["b_ada"], (0, chip * ADA_SHARD), (depth, ADA_SHARD)).reshape(depth, 1, ADA_SHARD)
    ada_part = _ada_fwd(c_pad, a["w_ada"], b_cols)
    ada_all = _allgather8(ada_part.reshape(depth * 16, ADA_SHARD), "gather_ada").reshape(N_DEV, depth, 16, ADA_SHARD)
    ada_rows = lax.dynamic_slice(ada_all, (0, 0, me, 0), (N_DEV, depth, 1, ADA_SHARD))[:, :, 0]
    ada = jnp.concatenate([ada_rows[2 * j] for j in range(N_CHIP)], axis=-1).reshape(depth, 9, D)

    lam_re = _pad_rows(a["lam_re"].reshape(depth, NST), 8)
    lam_im = _pad_rows(a["lam_im"].reshape(depth, NST), 8)
    log_dt_x = _pad_rows(jnp.repeat(a["log_dt"], GP, axis=1), 8)
    abr, abi, fr, fi = _ssm_prep(lam_re, lam_im, log_dt_x)

    def small_of(l):
        return {"conv_w": conv_w[l], "ssm_d": _row(a["ssm_d"][l]), "pool_scale": _row(a["pool_scale"][l]),
                "b_re": _embed(jnp.transpose(a["ssm_b_re"][l], (0, 2, 1))), "b_im": _embed(jnp.transpose(a["ssm_b_im"][l], (0, 2, 1))),
                "c_re": _embed(jnp.transpose(a["ssm_c_re"][l], (0, 2, 1))), "c_im": _embed(jnp.transpose(a["ssm_c_im"][l], (0, 2, 1))),
                "w_pool": _embed(a["w_pool"][l]),
                "abr": abr[l:l + 1], "abi": abi[l:l + 1], "fr": fr[l:l + 1], "fi": fi[l:l + 1]}

    def entries(g):
        return [i for i, e in enumerate(GATHERED) if e[3] == g]

    def shard_of(i, l):
        name, idx = GATHERED[i][0], GATHERED[i][1]
        return a[name][l] if idx is None else a[name][l, idx]

    stages = [(l, g) for l in range(depth) for g in range(N_SUB)]

    def gather_begin(t, after):
        l, g = stages[t]
        placed = [_place_shard(lays[i], sel, shard_of(i, l)) for i in entries(g)]
        return _gather_start(placed, after, [lays[i] for i in entries(g)], str(t))

    saved, weights, smalls = [], [], [small_of(l) for l in range(depth)]
    pending = {t: gather_begin(t, x) for t in range(min(2, len(stages)))}
    for t, (l, g) in enumerate(stages):
        glays = [lays[i] for i in entries(g)]
        send_sems, recv_sems, bufs, _ = pending.pop(t)
        w = _gather_forward(_gather_wait(send_sems, recv_sems, bufs, x, glays, str(t)), glays)
        weights.append(w)
        ada_l = ada[l]
        if t + 2 < len(stages):
            pending[t + 2] = gather_begin(t + 2, x)
            ada_l = ada_l + pending[t + 2][3][0, 0]
        if g == 1:
            wf = {GATHERED[i][0]: wi for i, wi in zip(entries(1), w)}
            x, sv = _mixer_fwd(x, ada_l, g_pre[l], g_post[l], wf, smalls[l])
        else:
            x, sv = _ffn_fwd(x, ada_l, g_pre[l], g_post[l], w[0], w[1], g)
        saved.append(sv)
    dx, loss_part = _loss_head(x, target)
    loss = lax.psum(loss_part[0, 0], ("x", "y", "c"))

    shard_grads = [[None] * depth for _ in GATHERED]
    small_grads = [{} for _ in range(depth)]
    ada_grads = [[None] * N_SUB for _ in range(depth)]
    gpre_grads = [[None] * N_SUB for _ in range(depth)]
    gpost_grads = [[None] * N_SUB for _ in range(depth)]
    states = {}

    def reduce_finish(t, after):
        l, g = stages[t]
        glays = [lays[i] for i in entries(g)]
        for i, grad in zip(entries(g), _reduce_end(states.pop(t), after, glays, sel, str(t))):
            shard_grads[i][l] = grad

    token = None
    for t in reversed(range(len(stages))):
        l, g = stages[t]
        ada_l = ada[l] if token is None else ada[l] + token[0, 0]
        if g == 1:
            wf = {GATHERED[i][0]: wi for i, wi in zip(entries(1), weights[t])}
            dx, wgrads, small, dada, dgp, dgq = _mixer_bwd(dx, saved[t], ada_l, g_pre[l], g_post[l], wf, smalls[l])
            small_grads[l].update(small)
        else:
            dx, dw_in, dw_out, dada, dgp, dgq = _ffn_bwd(dx, saved[t], ada_l, g_pre[l], g_post[l], weights[t][0], weights[t][1], g)
            wgrads = [dw_in, dw_out]
        ada_grads[l][g], gpre_grads[l][g], gpost_grads[l][g] = dada, dgp, dgq
        states[t] = _reduce_start(wgrads, [lays[i] for i in entries(g)], str(t))
        token = states[t][4]
        if t + 2 in states:
            reduce_finish(t + 2, dx)
    for l in range(depth):
        small_grads[l]["d_ada"] = jnp.concatenate([p for g in range(N_SUB) for p in ada_grads[l][g]], axis=1).reshape(-1)
        small_grads[l]["g_pre"] = jnp.concatenate(gpre_grads[l], axis=0)
        small_grads[l]["g_post"] = jnp.concatenate(gpost_grads[l], axis=0)

    stack = lambda key: _pad_rows(jnp.concatenate([small_grads[l][key] for l in range(depth)], axis=0), 8)
    gs = np.zeros((NST, 128), np.float32)
    gs[np.arange(NST), np.arange(NST) // GP] = 1.0
    dlr, dli, dldt = _ssm_prep_bwd(lam_re, lam_im, log_dt_x, stack("abr"), stack("abi"), stack("fr"), stack("fi"), jnp.asarray(gs))
    part = {
        "d_ada": jnp.stack([small_grads[l]["d_ada"] for l in range(depth)]),
        "g_pre": jnp.stack([small_grads[l]["g_pre"] for l in range(depth)]),
        "g_post": jnp.stack([small_grads[l]["g_post"] for l in range(depth)]),
        "conv_w": jnp.stack([small_grads[l]["conv_w"] for l in range(depth)]),
        "lam_re": dlr[:depth].reshape(depth, G, GP), "lam_im": dli[:depth].reshape(depth, G, GP), "log_dt": dldt[:depth, :G],
        "ssm_b_re": jnp.stack([jnp.transpose(_unembed(small_grads[l]["b_re"], G), (0, 2, 1)) for l in range(depth)]),
        "ssm_b_im": jnp.stack([jnp.transpose(_unembed(small_grads[l]["b_im"], G), (0, 2, 1)) for l in range(depth)]),
        "ssm_c_re": jnp.stack([jnp.transpose(_unembed(small_grads[l]["c_re"], G), (0, 2, 1)) for l in range(depth)]),
        "ssm_c_im": jnp.stack([jnp.transpose(_unembed(small_grads[l]["c_im"], G), (0, 2, 1)) for l in range(depth)]),
        "ssm_d": jnp.stack([small_grads[l]["ssm_d"][0] for l in range(depth)]),
        "w_pool": jnp.stack([_unembed(small_grads[l]["w_pool"], len(POOL_WINDOWS)) for l in range(depth)]),
        "pool_scale": jnp.stack([small_grads[l]["pool_scale"][0] for l in range(depth)]),
    }
    small_shapes = [part[k].shape for k in SMALL_ORDER]
    blocks = _allgather8(_pack([part[k] for k in SMALL_ORDER]), "gather_small_grads").reshape(N_DEV, -1, 128)
    small_sum = _sum_parts([blocks[i] for i in range(N_DEV)], F32, "small_grad_sum")
    total = dict(zip(SMALL_ORDER, _unpack(small_sum, small_shapes)))
    for t in sorted(states, reverse=True):
        reduce_finish(t, small_sum)

    grads = {}
    for name in sorted({e[0] for e in GATHERED}):
        cols = [shard_grads[i] for i, e in enumerate(GATHERED) if e[0] == name]
        grads[name] = jnp.stack(cols[0]) if len(cols) == 1 else jnp.stack([jnp.stack(pair) for pair in zip(*cols)])
    d_ada_all = jnp.stack([_unpack(blocks[i], small_shapes[:1])[0] for i in range(N_DEV)])
    d_cols = lax.dynamic_slice(d_ada_all, (0, 0, chip * ADA_SHARD), (N_DEV, depth, ADA_SHARD))
    d_cols = jnp.transpose(d_cols, (1, 0, 2))
    grads["w_ada"] = _ada_wgrad(c_pad, jnp.concatenate([d_cols, jnp.zeros_like(d_cols)], axis=1))
    grads["b_ada"] = total["d_ada"]
    grads["g_pre"] = lax.dynamic_slice(total["g_pre"], (0, 0, chip * W), (depth, 3, W))
    grads["g_post"] = lax.dynamic_slice(total["g_post"], (0, 0, chip * W), (depth, 3, W))
    grads["conv_w"] = lax.dynamic_slice(total["conv_w"], (0, 0, chip * (W // N_CHIP)), (depth, 3, W // N_CHIP))
    for k in SMALL_ORDER[4:]:
        grads[k] = total[k]

    out = {"loss": loss, "grad_x": dx[None]}
    for name in WEIGHTS:
        out["grad_" + name] = grads[name]
        out["delta_" + name], out["new_m_" + name], out["new_v_" + name] = _adamw(a[name], grads[name], a["m_" + name], a["v_" + name])
    return out


def kernel(x, c, w_ada, b_ada, g_pre, g_post, w_ff_in, w_ff_out, w_in, conv_w, w_conv_out, lam_re, lam_im, log_dt, ssm_b_re, ssm_b_im, ssm_c_re, ssm_c_im, ssm_d, w_glu, w_pool, pool_scale, w_pool_out, w_sb_out, w_out, loss_target, m_w_ada, m_b_ada, m_g_pre, m_g_post, m_w_ff_in, m_w_ff_out, m_w_in, m_conv_w, m_w_conv_out, m_lam_re, m_lam_im, m_log_dt, m_ssm_b_re, m_ssm_b_im, m_ssm_c_re, m_ssm_c_im, m_ssm_d, m_w_glu, m_w_pool, m_pool_scale, m_w_pool_out, m_w_sb_out, m_w_out, v_w_ada, v_b_ada, v_g_pre, v_g_post, v_w_ff_in, v_w_ff_out, v_w_in, v_conv_w, v_w_conv_out, v_lam_re, v_lam_im, v_log_dt, v_ssm_b_re, v_ssm_b_im, v_ssm_c_re, v_ssm_c_im, v_ssm_d, v_w_glu, v_w_pool, v_pool_scale, v_w_pool_out, v_w_sb_out, v_w_out):
    out = _step(dict(locals()))
    names = ["loss", "grad_x"] + [p + n for p in ("grad_", "delta_", "new_m_", "new_v_") for n in WEIGHTS]
    return tuple(out[n] for n in names)
```

```python
import functools
import math

import jax
import jax.numpy as jnp
import numpy as np
from jax import lax
from jax.experimental import pallas as pl
from jax.experimental.pallas import tpu as pltpu

F32 = jnp.float32
BF16 = jnp.bfloat16
MESH = pl.DeviceIdType.MESH

D = 1024
W = 256
FF = 2816
IN_COLS = 6144
G = 16
GH = 16
GP = 64
NST = G * GP
QB = 128
HEADS = 4
HD = 64
EPS = 1e-6
LAMBDA_RE_MAX = -1e-4
POOL_WINDOWS = (2, 4, 8, 16)
N_CHIP = 4
N_DEV = 8
VMEM_LIMIT = 56 * 1024 * 1024
HIGH = lax.Precision.HIGHEST

ADAM_LR, ADAM_B1, ADAM_B2, ADAM_EPS, ADAM_WD, ADAM_STEP = 0.001, 0.9, 0.999, 1e-08, 0.01, 10


def _call(body, **kw):
    return pl.pallas_call(body, **kw)


def _params(dims=None, **kw):
    return pltpu.CompilerParams(dimension_semantics=dims, vmem_limit_bytes=VMEM_LIMIT, **kw)


def _sds(shape, dtype):
    return jax.ShapeDtypeStruct(shape, dtype)


def _dot(a, b, ca=1, cb=0, precision=None):
    return lax.dot_general(a, b, (((ca,), (cb,)), ((), ())), preferred_element_type=F32, precision=precision)


def _bdot(a, b, ca=1, cb=0):
    return _dot(a.astype(BF16), b.astype(BF16), ca, cb)


def _sigmoid(x):
    return 1.0 / (1.0 + jnp.exp(-x))


def _mm(a, b, *, M, N, K, tm, tn, tk=None, ta=False, tb=False, out_dtype=F32, a_off=(0, 0), b_off=(0, 0), name):
    tk = K if tk is None else tk
    nk = K // tk
    assert M % tm == 0 and N % tn == 0 and K % tk == 0

    def body(a_ref, b_ref, o_ref, *acc):
        part = _bdot(a_ref[...], b_ref[...], 0 if ta else 1, 1 if tb else 0)
        if nk == 1:
            o_ref[...] = part.astype(out_dtype)
            return
        acc_ref = acc[0]
        k = pl.program_id(2)

        @pl.when(k == 0)
        def _():
            acc_ref[...] = part

        @pl.when(k > 0)
        def _():
            acc_ref[...] += part

        @pl.when(k == nk - 1)
        def _():
            o_ref[...] = acc_ref[...].astype(out_dtype)

    if ta:
        a_spec = pl.BlockSpec((tk, tm), lambda i, j, k: (k + a_off[0], i + a_off[1]))
    else:
        a_spec = pl.BlockSpec((tm, tk), lambda i, j, k: (i + a_off[0], k + a_off[1]))
    if tb:
        b_spec = pl.BlockSpec((tn, tk), lambda i, j, k: (j + b_off[0], k + b_off[1]))
    else:
        b_spec = pl.BlockSpec((tk, tn), lambda i, j, k: (k + b_off[0], j + b_off[1]))
    return _call(
        body, name=name, grid=(M // tm, N // tn, nk),
        in_specs=[a_spec, b_spec],
        out_specs=pl.BlockSpec((tm, tn), lambda i, j, k: (i, j)),
        out_shape=_sds((M, N), out_dtype),
        scratch_shapes=[] if nk == 1 else [pltpu.VMEM((tm, tn), F32)],
        compiler_params=_params(("parallel", "parallel", "arbitrary")),
    )(a, b)


def _row_tile(L):
    return min(L, 256)


def _norm_mod(x, g, shift, scale):
    L = x.shape[0]
    tr = _row_tile(L)

    def body(x_ref, g_ref, sh_ref, sc_ref, h_ref):
        xv = x_ref[...]
        r = lax.rsqrt(jnp.mean(xv * xv, axis=-1, keepdims=True) + EPS)
        h_ref[...] = (xv * r * g_ref[...] * (1.0 + sc_ref[...]) + sh_ref[...]).astype(BF16)

    row = pl.BlockSpec((tr, D), lambda i: (i, 0))
    vec = pl.BlockSpec((1, D), lambda i: (0, 0))
    return _call(body, name="norm_mod", grid=(L // tr,), in_specs=[row, vec, vec, vec], out_specs=row,
                 out_shape=_sds((L, D), BF16), compiler_params=_params(("parallel",)))(x, g, shift, scale)


STAT_ROWS = 16


def _norm_mod_bwd(dh, x, g, scale, dx_res, stats, s):
    L = x.shape[0]
    tr = _row_tile(L)

    def body(dh_ref, x_ref, g_ref, sc_ref, dxr_ref, stin_ref, dx_ref, st_ref):
        i = pl.program_id(0)
        xv = x_ref[...]
        dhv = dh_ref[...]
        r = lax.rsqrt(jnp.mean(xv * xv, axis=-1, keepdims=True) + EPS)
        y = xv * r
        n = y * g_ref[...]
        dn = dhv * (1.0 + sc_ref[...])
        dy = dn * g_ref[...]
        dx_ref[...] = dxr_ref[...] + r * (dy - y * jnp.mean(dy * y, axis=-1, keepdims=True))

        @pl.when(i == 0)
        def _():
            st_ref[...] = stin_ref[...]

        st_ref[3 * s:3 * s + 1, :] += jnp.sum(dhv, axis=0, keepdims=True)
        st_ref[3 * s + 1:3 * s + 2, :] += jnp.sum(dhv * n, axis=0, keepdims=True)
        st_ref[9 + s:10 + s, :] += jnp.sum(dn * y, axis=0, keepdims=True)

    row = pl.BlockSpec((tr, D), lambda i: (i, 0))
    vec = pl.BlockSpec((1, D), lambda i: (0, 0))
    st = pl.BlockSpec((STAT_ROWS, D), lambda i: (0, 0))
    return _call(body, name="norm_mod_bwd", grid=(L // tr,), in_specs=[row, row, vec, vec, row, st],
                 out_specs=[row, st], out_shape=[_sds((L, D), F32), _sds((STAT_ROWS, D), F32)],
                 input_output_aliases={5: 1},
                 compiler_params=_params(("arbitrary",)))(dh, x, g, scale, dx_res, stats)


def _post(x, f, g, gate, res_weight):
    L = x.shape[0]
    tr = _row_tile(L)

    def body(x_ref, f_ref, g_ref, gt_ref, o_ref):
        fv = f_ref[...]
        r = lax.rsqrt(jnp.mean(fv * fv, axis=-1, keepdims=True) + EPS)
        o_ref[...] = x_ref[...] + (res_weight * (1.0 + gt_ref[...])) * (fv * r * g_ref[...])

    row = pl.BlockSpec((tr, D), lambda i: (i, 0))
    vec = pl.BlockSpec((1, D), lambda i: (0, 0))
    return _call(body, name="post", grid=(L // tr,), in_specs=[row, row, vec, vec], out_specs=row,
                 out_shape=_sds((L, D), F32), compiler_params=_params(("parallel",)))(x, f, g, gate)


def _post_bwd(dx, f, g, gate, res_weight, stats, s):
    L = dx.shape[0]
    tr = _row_tile(L)

    def body(dx_ref, f_ref, g_ref, gt_ref, stin_ref, df_ref, st_ref):
        i = pl.program_id(0)
        fv = f_ref[...]
        dxv = dx_ref[...]
        r = lax.rsqrt(jnp.mean(fv * fv, axis=-1, keepdims=True) + EPS)
        y = fv * r
        dn = dxv * (res_weight * (1.0 + gt_ref[...]))
        dy = dn * g_ref[...]
        df_ref[...] = (r * (dy - y * jnp.mean(dy * y, axis=-1, keepdims=True))).astype(BF16)

        @pl.when(i == 0)
        def _():
            st_ref[...] = stin_ref[...]

        st_ref[3 * s + 2:3 * s + 3, :] += res_weight * jnp.sum(dxv * (y * g_ref[...]), axis=0, keepdims=True)
        st_ref[12 + s:13 + s, :] += jnp.sum(dn * y, axis=0, keepdims=True)

    row = pl.BlockSpec((tr, D), lambda i: (i, 0))
    vec = pl.BlockSpec((1, D), lambda i: (0, 0))
    st = pl.BlockSpec((STAT_ROWS, D), lambda i: (0, 0))
    return _call(body, name="post_bwd", grid=(L // tr,), in_specs=[row, row, vec, vec, st],
                 out_specs=[row, st], out_shape=[_sds((L, D), BF16), _sds((STAT_ROWS, D), F32)],
                 input_output_aliases={4: 1},
                 compiler_params=_params(("arbitrary",)))(dx, f, g, gate, stats)


def _loss_head(x, target):
    L = x.shape[0]
    tr = _row_tile(L)

    def body(x_ref, t_ref, dx_ref, loss_ref):
        i = pl.program_id(0)
        err = x_ref[...] - t_ref[...]
        dx_ref[...] = err * (1.0 / D)

        @pl.when(i == 0)
        def _():
            loss_ref[...] = jnp.zeros_like(loss_ref)

        loss_ref[...] += 0.5 * jnp.sum(jnp.mean(err * err, axis=-1, keepdims=True), axis=0, keepdims=True)

    row = pl.BlockSpec((tr, D), lambda i: (i, 0))
    return _call(body, name="loss_head", grid=(L // tr,), in_specs=[row, row],
                 out_specs=[row, pl.BlockSpec((1, 1), lambda i: (0, 0))],
                 out_shape=[_sds((L, D), F32), _sds((1, 1), F32)],
                 compiler_params=_params(("arbitrary",)))(x, target)


def _ffn_in(h, w_in):
    L = h.shape[0]
    tm, tn = min(L, 1024), 256
    nf = FF // tn

    def body(h_ref, wa_ref, wb_ref, a_ref, b_ref, act_ref):
        hv = h_ref[...]
        a = _dot(hv, wa_ref[...])
        b = _dot(hv, wb_ref[...])
        a_ref[...] = a
        b_ref[...] = b
        act_ref[...] = (a * _sigmoid(a) * b).astype(BF16)

    tile = pl.BlockSpec((tm, tn), lambda i, j: (i, j))
    return _call(body, name="ffn_in", grid=(L // tm, nf),
                 in_specs=[pl.BlockSpec((tm, D), lambda i, j: (i, 0)),
                           pl.BlockSpec((D, tn), lambda i, j: (0, j)),
                           pl.BlockSpec((D, tn), lambda i, j: (0, j + nf))],
                 out_specs=[tile, tile, tile],
                 out_shape=[_sds((L, FF), F32), _sds((L, FF), F32), _sds((L, FF), BF16)],
                 compiler_params=_params(("parallel", "parallel")))(h, w_in, w_in)


def _ffn_mid_bwd(df, w_out, a, b):
    L = df.shape[0]
    tm, tn = min(L, 1024), 256

    def body(df_ref, w_ref, a_ref, b_ref, da_ref, db_ref):
        dact = _dot(df_ref[...], w_ref[...], 1, 1)
        av = a_ref[...]
        sg = _sigmoid(av)
        da_ref[...] = (dact * b_ref[...] * (sg * (1.0 + av * (1.0 - sg)))).astype(BF16)
        db_ref[...] = (dact * (av * sg)).astype(BF16)

    tile = pl.BlockSpec((tm, tn), lambda i, j: (i, j))
    return _call(body, name="ffn_mid_bwd", grid=(L // tm, FF // tn),
                 in_specs=[pl.BlockSpec((tm, D), lambda i, j: (i, 0)),
                           pl.BlockSpec((tn, D), lambda i, j: (j, 0)), tile, tile],
                 out_specs=[tile, tile],
                 out_shape=[_sds((L, FF), BF16), _sds((L, FF), BF16)],
                 compiler_params=_params(("parallel", "parallel")))(df, w_out, a, b)


def _rows_before(ref, i, tr, halo):
    start = pl.multiple_of(jnp.maximum(i * tr - halo, 0), 8)
    return jnp.where(i > 0, ref[pl.ds(start, halo), :], 0.0)


def _rows_after(ref, i, n, tr, halo):
    start = pl.multiple_of(jnp.minimum((i + 1) * tr, (n - 1) * tr), 8)
    return jnp.where(i < n - 1, ref[pl.ds(start, halo), :], 0.0)


def _conv_fwd(p, conv_w):
    L = p.shape[0]
    tr = _row_tile(L)
    n = L // tr

    def body(bg_ref, cg_ref, v_ref, w_ref, za_ref, u_scr):
        i = pl.program_id(0)

        @pl.when(i == 0)
        def _():
            u_scr[...] = cg_ref[...] * v_ref[...]

        r0 = pl.multiple_of(i * tr, 8)
        ext = jnp.concatenate([_rows_before(u_scr, i, tr, 8), u_scr[pl.ds(r0, tr), :]], axis=0)
        w = w_ref[...]
        y = (w[0:1] * pltpu.roll(ext, 2, axis=0) + w[1:2] * pltpu.roll(ext, 1, axis=0) + w[2:3] * ext)[8:, :]
        za_ref[...] = (bg_ref[pl.ds(r0, tr), :] * y).astype(BF16)

    col = lambda c: pl.BlockSpec((L, W), lambda i: (0, c))
    return _call(body, name="conv_fwd", grid=(n,),
                 in_specs=[col(0), col(1), col(2), pl.BlockSpec((3, W), lambda i: (0, 0))],
                 out_specs=pl.BlockSpec((tr, W), lambda i: (i, 0)),
                 out_shape=_sds((L, W), BF16),
                 scratch_shapes=[pltpu.VMEM((L, W), F32)],
                 compiler_params=_params(("arbitrary",)))(p, p, p, conv_w)


def _conv_bwd(p, conv_w, dza):
    L = p.shape[0]
    tr = _row_tile(L)
    n = L // tr

    def body(bg_ref, cg_ref, v_ref, w_ref, dza_ref, dp_ref, dw_ref, u_scr, dy_scr):
        i = pl.program_id(0)

        @pl.when(i == 0)
        def _():
            u_scr[...] = cg_ref[...] * v_ref[...]
            dy_scr[...] = dza_ref[...] * bg_ref[...]
            dw_ref[...] = jnp.zeros_like(dw_ref)

        r0 = pl.multiple_of(i * tr, 8)
        w = w_ref[...]
        ext = jnp.concatenate([_rows_before(u_scr, i, tr, 8), u_scr[pl.ds(r0, tr), :]], axis=0)
        u2 = pltpu.roll(ext, 2, axis=0)[8:, :]
        u1 = pltpu.roll(ext, 1, axis=0)[8:, :]
        u0 = ext[8:, :]
        y = w[0:1] * u2 + w[1:2] * u1 + w[2:3] * u0
        dy = dy_scr[pl.ds(r0, tr), :]
        dext = jnp.concatenate([dy, _rows_after(dy_scr, i, n, tr, 8)], axis=0)
        m = tr + 8
        du = (w[2:3] * dext + w[1:2] * pltpu.roll(dext, m - 1, axis=0) + w[0:1] * pltpu.roll(dext, m - 2, axis=0))[:tr, :]
        dp_ref[:, 0:W] = (dza_ref[pl.ds(r0, tr), :] * y).astype(BF16)
        dp_ref[:, W:2 * W] = (du * v_ref[pl.ds(r0, tr), :]).astype(BF16)
        dp_ref[:, 2 * W:3 * W] = (du * cg_ref[pl.ds(r0, tr), :]).astype(BF16)
        dw_ref[...] += jnp.concatenate([jnp.sum(dy * u2, axis=0, keepdims=True),
                                        jnp.sum(dy * u1, axis=0, keepdims=True),
                                        jnp.sum(dy * u0, axis=0, keepdims=True)], axis=0)

    col = lambda c: pl.BlockSpec((L, W), lambda i: (0, c))
    return _call(body, name="conv_bwd", grid=(n,),
                 in_specs=[col(0), col(1), col(2), pl.BlockSpec((3, W), lambda i: (0, 0)),
                           pl.BlockSpec((L, W), lambda i: (0, 0))],
                 out_specs=[pl.BlockSpec((tr, 3 * W), lambda i: (i, 0)), pl.BlockSpec((3, W), lambda i: (0, 0))],
                 out_shape=[_sds((L, 3 * W), BF16), _sds((3, W), F32)],
                 scratch_shapes=[pltpu.VMEM((L, W), F32), pltpu.VMEM((L, W), F32)],
                 compiler_params=_params(("arbitrary",)))(p, p, p, conv_w, dza)


def _pool_windows(lane):
    wins = jnp.zeros(lane.shape, jnp.int32)
    for gi, w in enumerate(POOL_WINDOWS):
        wins = jnp.where(lane // (W // len(POOL_WINDOWS)) == gi, w, wins)
    return wins


def _pooled_block(u_ref, i, tr):
    r0 = pl.multiple_of(i * tr, 8)
    cur = u_ref[pl.ds(r0, tr), :]
    ext = jnp.concatenate([_rows_before(u_ref, i, tr, 16), cur], axis=0)
    s2 = ext + pltpu.roll(ext, 1, axis=0)
    s4 = s2 + pltpu.roll(s2, 2, axis=0)
    s8 = s4 + pltpu.roll(s4, 4, axis=0)
    s16 = s8 + pltpu.roll(s8, 8, axis=0)
    lane = lax.broadcasted_iota(jnp.int32, (tr, W), 1)
    wins = _pool_windows(lane)
    win_sum = jnp.where(wins == 2, s2[16:], jnp.where(wins == 4, s4[16:], jnp.where(wins == 8, s8[16:], s16[16:])))
    t = lax.broadcasted_iota(jnp.int32, (tr, W), 0) + i * tr
    cnt = jnp.minimum(t + 1, wins).astype(F32)
    return win_sum / cnt - cur, cnt


def _pool_fwd(p, w_pool_bd, pool_scale):
    L = p.shape[0]
    tr = _row_tile(L)

    def body(u_ref, w_ref, sc_ref, zc_ref):
        pooled, _ = _pooled_block(u_ref, pl.program_id(0), tr)
        zc_ref[...] = (_bdot(pooled, w_ref[...]) * sc_ref[...]).astype(BF16)

    return _call(body, name="pool_fwd", grid=(L // tr,),
                 in_specs=[pl.BlockSpec((L, W), lambda i: (0, 4)), pl.BlockSpec((W, W), lambda i: (0, 0)),
                           pl.BlockSpec((1, W), lambda i: (0, 0))],
                 out_specs=pl.BlockSpec((tr, W), lambda i: (i, 0)), out_shape=_sds((L, W), BF16),
                 compiler_params=_params(("arbitrary",)))(p, w_pool_bd, pool_scale)


def _pool_bwd(p, w_pool_bd, pool_scale, dzc):
    L = p.shape[0]
    tr = _row_tile(L)
    n = L // tr

    def body(u_ref, w_ref, sc_ref, dzc_ref, du_ref, dw_ref, dsc_ref, g_scr):
        i = pl.program_id(0)

        @pl.when(i == 0)
        def _():
            dw_ref[...] = jnp.zeros_like(dw_ref)
            dsc_ref[...] = jnp.zeros_like(dsc_ref)

            def rows(k, carry):
                r = pl.multiple_of(k * tr, 8)
                dmix = (dzc_ref[pl.ds(r, tr), :] * sc_ref[...]).astype(BF16)
                dpool = _dot(dmix, w_ref[...].astype(BF16), 1, 1)
                lane = lax.broadcasted_iota(jnp.int32, (tr, W), 1)
                t = lax.broadcasted_iota(jnp.int32, (tr, W), 0) + k * tr
                cnt = jnp.minimum(t + 1, _pool_windows(lane)).astype(F32)
                g_scr[pl.ds(r, tr), :] = dpool / cnt
                return carry

            lax.fori_loop(0, n, rows, 0)

        r0 = pl.multiple_of(i * tr, 8)
        pooled, cnt = _pooled_block(u_ref, i, tr)
        dzc = dzc_ref[pl.ds(r0, tr), :]
        mixed = _bdot(pooled, w_ref[...])
        dsc_ref[...] += jnp.sum(dzc * mixed, axis=0, keepdims=True)
        dmix = (dzc * sc_ref[...]).astype(BF16)
        dw_ref[...] += _dot(pooled.astype(BF16), dmix, 0, 0)
        gcur = g_scr[pl.ds(r0, tr), :]
        ext = jnp.concatenate([gcur, _rows_after(g_scr, i, n, tr, 16)], axis=0)
        m = tr + 16
        s2 = ext + pltpu.roll(ext, m - 1, axis=0)
        s4 = s2 + pltpu.roll(s2, m - 2, axis=0)
        s8 = s4 + pltpu.roll(s4, m - 4, axis=0)
        s16 = s8 + pltpu.roll(s8, m - 8, axis=0)
        lane = lax.broadcasted_iota(jnp.int32, (tr, W), 1)
        wins = _pool_windows(lane)
        ahead = jnp.where(wins == 2, s2[:tr], jnp.where(wins == 4, s4[:tr], jnp.where(wins == 8, s8[:tr], s16[:tr])))
        du_ref[...] = (ahead - gcur * cnt).astype(BF16)

    return _call(body, name="pool_bwd", grid=(n,),
                 in_specs=[pl.BlockSpec((L, W), lambda i: (0, 4)), pl.BlockSpec((W, W), lambda i: (0, 0)),
                           pl.BlockSpec((1, W), lambda i: (0, 0)), pl.BlockSpec((L, W), lambda i: (0, 0))],
                 out_specs=[pl.BlockSpec((tr, W), lambda i: (i, 0)), pl.BlockSpec((W, W), lambda i: (0, 0)),
                            pl.BlockSpec((1, W), lambda i: (0, 0))],
                 out_shape=[_sds((L, W), BF16), _sds((W, W), F32), _sds((1, W), F32)],
                 scratch_shapes=[pltpu.VMEM((L, W), F32)],
                 compiler_params=_params(("arbitrary",)))(p, w_pool_bd, pool_scale, dzc)


SSM_SLAB = 512


def _ssm_prep(lam_re, lam_im, log_dt_x):
    def body(lr_ref, li_ref, ldt_ref, abr_ref, abi_ref, fr_ref, fi_ref):
        lr = jnp.minimum(lr_ref[...], LAMBDA_RE_MAX)
        li = li_ref[...]
        dt = jnp.exp(ldt_ref[...])
        mag = jnp.exp(lr * dt)
        abr = mag * jnp.cos(li * dt)
        abi = mag * jnp.sin(li * dt)
        den = lr * lr + li * li
        nr = abr - 1.0
        abr_ref[...] = abr
        abi_ref[...] = abi
        fr_ref[...] = (nr * lr + abi * li) / den
        fi_ref[...] = (abi * lr - nr * li) / den

    shp = _sds(lam_re.shape, F32)
    return _call(body, name="ssm_prep", out_shape=[shp, shp, shp, shp], compiler_params=_params())(lam_re, lam_im, log_dt_x)


def _ssm_prep_bwd(lam_re, lam_im, log_dt_x, g_abr, g_abi, g_fr, g_fi, group_sum):
    def body(lr_ref, li_ref, ldt_ref, gar_ref, gai_ref, gfr_ref, gfi_ref, gs_ref, dlr_ref, dli_ref, dldt_ref):
        lam = lr_ref[...]
        lr = jnp.minimum(lam, LAMBDA_RE_MAX)
        li = li_ref[...]
        dt = jnp.exp(ldt_ref[...])
        mag = jnp.exp(lr * dt)
        abr = mag * jnp.cos(li * dt)
        abi = mag * jnp.sin(li * dt)
        den = lr * lr + li * li
        nr = abr - 1.0
        fr = (nr * lr + abi * li) / den
        fi = (abi * lr - nr * li) / den
        d_nre = gfr_ref[...] / den
        d_nim = gfi_ref[...] / den
        d_den = -(gfr_ref[...] * fr + gfi_ref[...] * fi) / den
        d_abr = gar_ref[...] + d_nre * lr - d_nim * li
        d_abi = gai_ref[...] + d_nre * li + d_nim * lr
        d_lr = d_nre * nr + d_nim * abi + d_den * 2.0 * lr
        d_li = d_nre * abi - d_nim * nr + d_den * 2.0 * li
        d_mag = d_abr * jnp.cos(li * dt) + d_abi * jnp.sin(li * dt)
        d_th = -d_abr * abi + d_abi * abr
        d_lr = d_lr + d_mag * mag * dt
        d_li = d_li + d_th * dt
        d_dt = d_mag * mag * lr + d_th * li
        passes = jnp.where(lam < LAMBDA_RE_MAX, 1.0, jnp.where(lam == LAMBDA_RE_MAX, 0.5, 0.0))
        dlr_ref[...] = d_lr * passes
        dli_ref[...] = d_li
        dldt_ref[...] = _dot(d_dt * dt, gs_ref[...], precision=HIGH)

    shp = _sds(lam_re.shape, F32)
    return _call(body, name="ssm_prep_bwd", out_shape=[shp, shp, _sds((lam_re.shape[0], 128), F32)],
                 compiler_params=_params())(lam_re, lam_im, log_dt_x, g_abr, g_abi, g_fr, g_fi, group_sum)


def _cmul(ar, ai, br, bi):
    return ar * br - ai * bi, ar * bi + ai * br


def _powers(ar, ai):
    out = [(ar, ai)]
    for _ in range(7):
        out.append(_cmul(out[-1][0], out[-1][1], ar, ai))
    return out


def _scan_rows(s_re, s_im, ar, ai, L, reverse=False, visit=None, visit_init=None):
    n = s_re.shape[1]
    pw = _powers(ar, ai)
    row = lax.broadcasted_iota(jnp.int32, (8, n), 0)
    dist = (8 - row) if reverse else (row + 1)
    pr = jnp.zeros((8, n), F32)
    pi = jnp.zeros((8, n), F32)
    for k in range(8):
        pr = jnp.where(dist == k + 1, pw[k][0], pr)
        pi = jnp.where(dist == k + 1, pw[k][1], pi)
    nb = L // 8

    def blk(t, carry):
        cr, ci, acc = carry
        b = (nb - 1 - t) if reverse else t
        r0 = pl.multiple_of(b * 8, 8)
        xr = s_re[pl.ds(r0, 8), :]
        xi = s_im[pl.ds(r0, 8), :]
        for d in (1, 2, 4):
            if reverse:
                keep = row < 8 - d
                sr, si = pltpu.roll(xr, 8 - d, axis=0), pltpu.roll(xi, 8 - d, axis=0)
            else:
                keep = row >= d
                sr, si = pltpu.roll(xr, d, axis=0), pltpu.roll(xi, d, axis=0)
            sr = jnp.where(keep, sr, 0.0)
            si = jnp.where(keep, si, 0.0)
            mr, mi = _cmul(pw[d - 1][0], pw[d - 1][1], sr, si)
            xr, xi = xr + mr, xi + mi
        mr, mi = _cmul(pr, pi, cr, ci)
        xr, xi = xr + mr, xi + mi
        s_re[pl.ds(r0, 8), :] = xr
        s_im[pl.ds(r0, 8), :] = xi
        if visit is not None:
            acc = visit(b, xr, xi, acc)
        if reverse:
            return xr[0:1, :], xi[0:1, :], acc
        return xr[7:8, :], xi[7:8, :], acc

    zero = jnp.zeros((1, n), F32)
    return lax.fori_loop(0, nb, blk, (zero, zero, visit_init if visit is not None else 0))[2]


def _ssm_project(u_ref, wbr, wbi, s_re, s_im, L):
    ch = min(L, 256)

    def rows(k, carry):
        r = pl.multiple_of(k * ch, 8)
        ub = u_ref[pl.ds(r, ch), :].astype(BF16)
        s_re[pl.ds(r, ch), :] = _dot(ub, wbr)
        s_im[pl.ds(r, ch), :] = _dot(ub, wbi)
        return carry

    lax.fori_loop(0, L // ch, rows, 0)


def _gelu(y):
    c = math.sqrt(2.0 / math.pi)
    return 0.5 * y * (1.0 + jnp.tanh(c * (y + 0.044715 * y * y * y)))


def _gelu_grad(y):
    c = math.sqrt(2.0 / math.pi)
    th = jnp.tanh(c * (y + 0.044715 * y * y * y))
    return 0.5 * (1.0 + th) + 0.5 * y * (1.0 - th * th) * c * (1.0 + 3.0 * 0.044715 * y * y)


def _ssm_fwd(p, b_re_bd, b_im_bd, c_re_bd, c_im_bd, abr, abi, fr, fi, d_skip):
    L = p.shape[0]
    ns = NST // SSM_SLAB
    ch = min(L, 256)

    def body(u_ref, br_ref, bi_ref, cr_ref, ci_ref, abr_ref, abi_ref, fr_ref, fi_ref, d_ref,
             y_ref, zb_ref, s_re, s_im):
        j = pl.program_id(0)
        f_re, f_im = fr_ref[...], fi_ref[...]
        wbr = (f_re * br_ref[...] - f_im * bi_ref[...]).astype(BF16)
        wbi = (f_re * bi_ref[...] + f_im * br_ref[...]).astype(BF16)
        _ssm_project(u_ref, wbr, wbi, s_re, s_im, L)
        _scan_rows(s_re, s_im, abr_ref[...], abi_ref[...], L)
        crb = cr_ref[...].astype(BF16)
        cib = ci_ref[...].astype(BF16)

        def rows(k, carry):
            r = pl.multiple_of(k * ch, 8)
            part = _dot(s_re[pl.ds(r, ch), :].astype(BF16), crb) - _dot(s_im[pl.ds(r, ch), :].astype(BF16), cib)

            @pl.when(j == 0)
            def _():
                y_ref[pl.ds(r, ch), :] = part + d_ref[...] * u_ref[pl.ds(r, ch), :]

            @pl.when(j > 0)
            def _():
                y_ref[pl.ds(r, ch), :] += part

            @pl.when(j == ns - 1)
            def _():
                zb_ref[pl.ds(r, ch), :] = _gelu(y_ref[pl.ds(r, ch), :]).astype(BF16)

            return carry

        lax.fori_loop(0, L // ch, rows, 0)

    full = lambda shape: pl.BlockSpec(shape, lambda j: (0, 0))
    lanes = pl.BlockSpec((1, SSM_SLAB), lambda j: (0, j))
    return _call(body, name="ssm_fwd", grid=(ns,),
                 in_specs=[pl.BlockSpec((L, W), lambda j: (0, 3)),
                           pl.BlockSpec((W, SSM_SLAB), lambda j: (0, j)), pl.BlockSpec((W, SSM_SLAB), lambda j: (0, j)),
                           pl.BlockSpec((SSM_SLAB, W), lambda j: (j, 0)), pl.BlockSpec((SSM_SLAB, W), lambda j: (j, 0)),
                           lanes, lanes, lanes, lanes, full((1, W))],
                 out_specs=[full((L, W)), full((L, W))],
                 out_shape=[_sds((L, W), F32), _sds((L, W), BF16)],
                 scratch_shapes=[pltpu.VMEM((L, SSM_SLAB), F32), pltpu.VMEM((L, SSM_SLAB), F32)],
                 compiler_params=_params(("arbitrary",)))(p, b_re_bd, b_im_bd, c_re_bd, c_im_bd, abr, abi, fr, fi, d_skip)


def _ssm_bwd(p, y, dzb, b_re_bd, b_im_bd, c_re_bd, c_im_bd, abr, abi, fr, fi, d_skip):
    L = p.shape[0]
    ns = NST // SSM_SLAB
    ch = min(L, 256)
    n_ch = L // ch

    def body(u_ref, y_ref, dzb_ref, br_ref, bi_ref, cr_ref, ci_ref, abr_ref, abi_ref, fr_ref, fi_ref, d_ref,
             du_ref, dd_ref, dbr_ref, dbi_ref, dcr_ref, dci_ref, gar_ref, gai_ref, gfr_ref, gfi_ref,
             s_re, s_im, l_re, l_im, dy_scr, du_scr):
        j = pl.program_id(0)
        f_re, f_im = fr_ref[...], fi_ref[...]
        b_re, b_im = br_ref[...], bi_ref[...]
        wbr = (f_re * b_re - f_im * b_im).astype(BF16)
        wbi = (f_re * b_im + f_im * b_re).astype(BF16)
        a_re, a_im = abr_ref[...], abi_ref[...]

        @pl.when(j == 0)
        def _():
            def rows(k, acc):
                r = pl.multiple_of(k * ch, 8)
                dy = dzb_ref[pl.ds(r, ch), :] * _gelu_grad(y_ref[pl.ds(r, ch), :])
                dy_scr[pl.ds(r, ch), :] = dy
                du_scr[pl.ds(r, ch), :] = d_ref[...] * dy
                return acc + jnp.sum(dy * u_ref[pl.ds(r, ch), :], axis=0, keepdims=True)

            dd_ref[...] = lax.fori_loop(0, n_ch, rows, jnp.zeros((1, W), F32))

        _ssm_project(u_ref, wbr, wbi, s_re, s_im, L)
        _scan_rows(s_re, s_im, a_re, a_im, L)
        crb = cr_ref[...].astype(BF16)
        cib = ci_ref[...].astype(BF16)

        def rows_c(k, acc):
            dcr, dci = acc
            r = pl.multiple_of(k * ch, 8)
            dyb = dy_scr[pl.ds(r, ch), :].astype(BF16)
            dcr = dcr + _dot(s_re[pl.ds(r, ch), :].astype(BF16), dyb, 0, 0)
            dci = dci - _dot(s_im[pl.ds(r, ch), :].astype(BF16), dyb, 0, 0)
            l_re[pl.ds(r, ch), :] = _dot(dyb, crb, 1, 1)
            l_im[pl.ds(r, ch), :] = -_dot(dyb, cib, 1, 1)
            return dcr, dci

        zc = jnp.zeros((SSM_SLAB, W), F32)
        dcr, dci = lax.fori_loop(0, n_ch, rows_c, (zc, zc))
        dcr_ref[...] = dcr
        dci_ref[...] = dci

        row8 = lax.broadcasted_iota(jnp.int32, (8, SSM_SLAB), 0)

        def visit(b, lr, li, acc):
            ar_acc, ai_acc = acc
            r0 = pl.multiple_of(b * 8, 8)
            rp = pl.multiple_of(jnp.maximum(b * 8 - 8, 0), 8)
            has_prev = b > 0
            pr = jnp.where(has_prev, s_re[pl.ds(rp, 8), :][7:8, :], 0.0)
            pi = jnp.where(has_prev, s_im[pl.ds(rp, 8), :][7:8, :], 0.0)
            sr = jnp.where(row8 >= 1, pltpu.roll(s_re[pl.ds(r0, 8), :], 1, axis=0), pr)
            si = jnp.where(row8 >= 1, pltpu.roll(s_im[pl.ds(r0, 8), :], 1, axis=0), pi)
            return ar_acc + lr * sr + li * si, ai_acc - lr * si + li * sr

        z8 = jnp.zeros((8, SSM_SLAB), F32)
        ar_acc, ai_acc = _scan_rows(l_re, l_im, a_re, -a_im, L, reverse=True, visit=visit, visit_init=(z8, z8))
        gar_ref[...] = jnp.sum(ar_acc, axis=0, keepdims=True)
        gai_ref[...] = jnp.sum(ai_acc, axis=0, keepdims=True)

        def rows_b(k, acc):
            dwr, dwi = acc
            r = pl.multiple_of(k * ch, 8)
            ub = u_ref[pl.ds(r, ch), :].astype(BF16)
            lrb = l_re[pl.ds(r, ch), :].astype(BF16)
            lib = l_im[pl.ds(r, ch), :].astype(BF16)
            du_scr[pl.ds(r, ch), :] += _dot(lrb, wbr, 1, 1) + _dot(lib, wbi, 1, 1)
            return dwr + _dot(ub, lrb, 0, 0), dwi + _dot(ub, lib, 0, 0)

        zb = jnp.zeros((W, SSM_SLAB), F32)
        dwr, dwi = lax.fori_loop(0, n_ch, rows_b, (zb, zb))
        dbr_ref[...] = dwr * f_re + dwi * f_im
        dbi_ref[...] = -dwr * f_im + dwi * f_re
        gfr_ref[...] = jnp.sum(dwr * b_re + dwi * b_im, axis=0, keepdims=True)
        gfi_ref[...] = jnp.sum(-dwr * b_im + dwi * b_re, axis=0, keepdims=True)

        @pl.when(j == ns - 1)
        def _():
            du_ref[...] = du_scr[...].astype(BF16)

    full = lambda shape: pl.BlockSpec(shape, lambda j: (0, 0))
    lanes = pl.BlockSpec((1, SSM_SLAB), lambda j: (0, j))
    bspec = pl.BlockSpec((W, SSM_SLAB), lambda j: (0, j))
    cspec = pl.BlockSpec((SSM_SLAB, W), lambda j: (j, 0))
    slab = lambda: pltpu.VMEM((L, SSM_SLAB), F32)
    return _call(body, name="ssm_bwd", grid=(ns,),
                 in_specs=[pl.BlockSpec((L, W), lambda j: (0, 3)), full((L, W)), full((L, W)),
                           bspec, bspec, cspec, cspec, lanes, lanes, lanes, lanes, full((1, W))],
                 out_specs=[full((L, W)), full((1, W)), bspec, bspec, cspec, cspec, lanes, lanes, lanes, lanes],
                 out_shape=[_sds((L, W), BF16), _sds((1, W), F32), _sds((W, NST), F32), _sds((W, NST), F32),
                            _sds((NST, W), F32), _sds((NST, W), F32)] + [_sds((1, NST), F32)] * 4,
                 scratch_shapes=[slab(), slab(), slab(), slab(), pltpu.VMEM((L, W), F32), pltpu.VMEM((L, W), F32)],
                 compiler_params=_params(("arbitrary",)))(p, y, dzb, b_re_bd, b_im_bd, c_re_bd, c_im_bd,
                                                          abr, abi, fr, fi, d_skip)


SB_KB = 512


SB_SUB = 256


def _split2(x):
    hi = x.astype(BF16)
    return hi, (x - hi.astype(F32)).astype(BF16)


def _ones_dot(x, ones):
    n = x.shape[0]
    r = _dot(jnp.concatenate(_split2(x), axis=0), ones)
    return r[:n] + r[n:]


def _suffix_sums(x, tri):
    sub = tri.shape[0]
    parts = [_ones_dot(x[:, i:i + sub], tri) for i in range(0, x.shape[1], sub)]
    out, after = [], None
    for p in reversed(parts):
        out.append(p if after is None else p + after)
        after = p[:, 0:1] if after is None else after + p[:, 0:1]
    return jnp.concatenate(out[::-1], axis=1)


def _prefix_sums_exclusive(x, tri_le):
    sub = tri_le.shape[0]
    out, before = [], None
    for i in range(0, x.shape[1], sub):
        xi = x[:, i:i + sub]
        inc = _ones_dot(xi, tri_le)
        out.append(inc - xi if before is None else inc - xi + before)
        before = inc[:, sub - 1:sub] if before is None else before + inc[:, sub - 1:sub]
    return jnp.concatenate(out, axis=1)


def _sb_block(q, kj, i, jb, kb, right, tri):
    z = _bdot(q, kj, 1, 1)
    t_idx = lax.broadcasted_iota(jnp.int32, (QB, kb), 0) + i * QB
    s_idx = lax.broadcasted_iota(jnp.int32, (QB, kb), 1) + jb * kb
    mask = s_idx < t_idx
    lk_all = jnp.minimum(-z, 0.0) - jnp.log1p(jnp.exp(-jnp.abs(z)))
    lk = jnp.where(mask, lk_all, 0.0)
    suf = _suffix_sums(lk, tri)
    a = jnp.where(mask, jnp.exp((lk_all + z) + (suf - lk) + right), 0.0)
    return z, mask, suf, a


def _sb_ones(kb):
    sub = min(SB_SUB, kb)
    r = lax.broadcasted_iota(jnp.int32, (sub, sub), 0)
    c = lax.broadcasted_iota(jnp.int32, (sub, sub), 1)
    return (r >= c).astype(BF16), (r <= c).astype(BF16)


def _sb_fwd(q, k, v):
    L = q.shape[1]
    kb = min(SB_KB, L)
    per = kb // QB

    def body(q_ref, k_ref, v_ref, o_ref, rs_ref):
        i = pl.program_id(0)
        tri, _ = _sb_ones(kb)
        lane = lax.broadcasted_iota(jnp.int32, (QB, 128), 1)
        qs = [q_ref[h] for h in range(HEADS)]

        def step(t, carry):
            accs, rights, sums = carry
            jb = i // per - t
            r = pl.multiple_of(jb * kb, kb)
            out = []
            for h in range(HEADS):
                _, _, suf, a = _sb_block(qs[h], k_ref[h, pl.ds(r, kb), :], i, jb, kb, rights[h], tri)
                tot = suf[:, 0:1]
                out.append((accs[h] + _bdot(a, v_ref[h, pl.ds(r, kb), :]), rights[h] + tot,
                            sums[h] + jnp.where(lane == jb, tot, 0.0)))
            return tuple(o[0] for o in out), tuple(o[1] for o in out), tuple(o[2] for o in out)

        init = (tuple(jnp.zeros((QB, HD), F32) for _ in range(HEADS)), tuple(jnp.zeros((QB, 1), F32) for _ in range(HEADS)),
                tuple(jnp.zeros((QB, 128), F32) for _ in range(HEADS)))
        accs, _, sums = lax.fori_loop(0, i // per + 1, step, init)
        for h in range(HEADS):
            o_ref[h] = accs[h]
            rs_ref[h] = sums[h]

    heads = pl.BlockSpec((HEADS, L, HD), lambda i: (0, 0, 0))
    blk = pl.BlockSpec((HEADS, QB, HD), lambda i: (0, i, 0))
    return _call(body, name="sb_fwd", grid=(L // QB,), in_specs=[blk, heads, heads],
                 out_specs=[blk, pl.BlockSpec((HEADS, QB, 128), lambda i: (0, i, 0))],
                 out_shape=[_sds((HEADS, L, HD), F32), _sds((HEADS, L, 128), F32)],
                 compiler_params=_params(("parallel",)))(q, k, v)


def _sb_bwd(q, k, v, do, block_sums):
    L = q.shape[1]
    kb = min(SB_KB, L)
    per = kb // QB

    def body(q_ref, k_ref, v_ref, do_ref, rs_ref, dq_ref, dk_ref, dv_ref):
        i = pl.program_id(0)
        tri, tri_le = _sb_ones(kb)
        lane = lax.broadcasted_iota(jnp.int32, (QB, 128), 1)

        @pl.when(i == 0)
        def _():
            dk_ref[...] = jnp.zeros_like(dk_ref)
            dv_ref[...] = jnp.zeros_like(dv_ref)

        qs = [q_ref[h] for h in range(HEADS)]
        dos = [do_ref[h] for h in range(HEADS)]
        sums = [rs_ref[h] for h in range(HEADS)]

        def step(jb, carry):
            dqs, lefts = carry
            r = pl.multiple_of(jb * kb, kb)
            out = []
            for h in range(HEADS):
                kj = k_ref[h, pl.ds(r, kb), :]
                vj = v_ref[h, pl.ds(r, kb), :]
                right = jnp.sum(jnp.where(lane > jb, sums[h], 0.0), axis=1, keepdims=True)
                z, mask, _, a = _sb_block(qs[h], kj, i, jb, kb, right, tri)
                e = a * _bdot(dos[h], vj, 1, 1)
                dv_ref[h, pl.ds(r, kb), :] += _dot(a.astype(BF16), dos[h].astype(BF16), 0, 0)
                before = lefts[h] + _prefix_sums_exclusive(e, tri_le)
                sg = _sigmoid(z)
                dz = jnp.where(mask, e * (1.0 - sg) - sg * before, 0.0).astype(BF16)
                dk_ref[h, pl.ds(r, kb), :] += _dot(dz, qs[h].astype(BF16), 0, 0)
                out.append((dqs[h] + _dot(dz, kj.astype(BF16)), lefts[h] + jnp.sum(e, axis=1, keepdims=True)))
            return tuple(o[0] for o in out), tuple(o[1] for o in out)

        init = (tuple(jnp.zeros((QB, HD), F32) for _ in range(HEADS)), tuple(jnp.zeros((QB, 1), F32) for _ in range(HEADS)))
        dqs, _ = lax.fori_loop(0, i // per + 1, step, init)
        for h in range(HEADS):
            dq_ref[h] = dqs[h]

    heads = pl.BlockSpec((HEADS, L, HD), lambda i: (0, 0, 0))
    blk = pl.BlockSpec((HEADS, QB, HD), lambda i: (0, i, 0))
    shp = _sds((HEADS, L, HD), F32)
    return _call(body, name="sb_bwd", grid=(L // QB,),
                 in_specs=[blk, heads, heads, blk, pl.BlockSpec((HEADS, QB, 128), lambda i: (0, i, 0))],
                 out_specs=[blk, heads, heads], out_shape=[shp, shp, shp],
                 compiler_params=_params(("arbitrary",)))(q, k, v, do, block_sums)


def _merge_fwd(za, zb, zc, zd, p, w_conv_out, w_glu, w_pool_out, w_sb_out):
    L = za.shape[0]
    tm = _row_tile(L)

    def body(za_ref, zb_ref, zc_ref, zd_ref, g0, g1, g2, g3, wc_ref, wg_ref, wp_ref, ws_ref, o_ref):
        glu = _dot(zb_ref[...], wg_ref[...])
        yb = glu[:, :D] * _sigmoid(glu[:, D:])
        m = _sigmoid(g0[...]) * _dot(za_ref[...], wc_ref[...])
        m = m + _sigmoid(g1[...]) * yb
        m = m + _sigmoid(g2[...]) * _dot(zc_ref[...], wp_ref[...])
        m = m + _sigmoid(g3[...]) * _dot(zd_ref[...], ws_ref[...])
        o_ref[...] = m.astype(BF16)

    zt = pl.BlockSpec((tm, W), lambda i: (i, 0))
    gate = lambda b: pl.BlockSpec((tm, D), lambda i: (i, 2 + b))
    wfull = lambda n: pl.BlockSpec((W, n), lambda i: (0, 0))
    return _call(body, name="merge_fwd", grid=(L // tm,),
                 in_specs=[zt, zt, zt, zt, gate(0), gate(1), gate(2), gate(3), wfull(D), wfull(2 * D), wfull(D), wfull(D)],
                 out_specs=pl.BlockSpec((tm, D), lambda i: (i, 0)), out_shape=_sds((L, D), BF16),
                 compiler_params=_params(("parallel",)))(za, zb, zc, zd, p, p, p, p, w_conv_out, w_glu, w_pool_out, w_sb_out)


def _merge_bwd(dm, za, zb, zc, zd, p, w_conv_out, w_glu, w_pool_out, w_sb_out):
    L = za.shape[0]
    tm = _row_tile(L)
    n = L // tm

    def body(dm_ref, za_ref, zb_ref, zc_ref, zd_ref, g0, g1, g2, g3, wc_ref, wg_ref, wp_ref, ws_ref,
             dza_ref, dzb_ref, dzc_ref, dzd_ref, dg_ref, dwc_ref, dwg_ref, dwp_ref, dws_ref,
             awc, awg, awp, aws):
        i = pl.program_id(0)

        @pl.when(i == 0)
        def _():
            awc[...] = jnp.zeros_like(awc)
            awg[...] = jnp.zeros_like(awg)
            awp[...] = jnp.zeros_like(awp)
            aws[...] = jnp.zeros_like(aws)

        dmv = dm_ref[...]

        def gated(g_ref, y, col):
            s = _sigmoid(g_ref[...])
            dg_ref[:, col * D:(col + 1) * D] = (dmv * y * s * (1.0 - s)).astype(BF16)
            return (dmv * s)

        def linear(z_ref, w_ref, acc, dz_ref, col, g_ref):
            zv = z_ref[...]
            dy = gated(g_ref, _dot(zv, w_ref[...]), col).astype(BF16)
            dz_ref[...] = _dot(dy, w_ref[...], 1, 1)
            acc[...] += _dot(zv, dy, 0, 0)

        linear(za_ref, wc_ref, awc, dza_ref, 0, g0)
        linear(zc_ref, wp_ref, awp, dzc_ref, 2, g2)
        linear(zd_ref, ws_ref, aws, dzd_ref, 3, g3)
        zbv = zb_ref[...]
        glu = _dot(zbv, wg_ref[...])
        ga = glu[:, :D]
        sg = _sigmoid(glu[:, D:])
        dyb = gated(g1, ga * sg, 1)
        dga = (dyb * sg).astype(BF16)
        dgg = (dyb * ga * sg * (1.0 - sg)).astype(BF16)
        dzb_ref[...] = _dot(dga, wg_ref[:, :D], 1, 1) + _dot(dgg, wg_ref[:, D:], 1, 1)
        awg[:, :D] += _dot(zbv, dga, 0, 0)
        awg[:, D:] += _dot(zbv, dgg, 0, 0)

        @pl.when(i == n - 1)
        def _():
            dwc_ref[...] = awc[...].astype(BF16)
            dwg_ref[...] = awg[...].astype(BF16)
            dwp_ref[...] = awp[...].astype(BF16)
            dws_ref[...] = aws[...].astype(BF16)

    zt = pl.BlockSpec((tm, W), lambda i: (i, 0))
    gate = lambda b: pl.BlockSpec((tm, D), lambda i: (i, 2 + b))
    wfull = lambda n_: pl.BlockSpec((W, n_), lambda i: (0, 0))
    zs = _sds((L, W), F32)
    return _call(body, name="merge_bwd", grid=(n,),
                 in_specs=[pl.BlockSpec((tm, D), lambda i: (i, 0)), zt, zt, zt, zt, gate(0), gate(1), gate(2), gate(3),
                           wfull(D), wfull(2 * D), wfull(D), wfull(D)],
                 out_specs=[zt, zt, zt, zt, pl.BlockSpec((tm, 4 * D), lambda i: (i, 0)),
                            wfull(D), wfull(2 * D), wfull(D), wfull(D)],
                 out_shape=[zs, zs, zs, zs, _sds((L, 4 * D), BF16),
                            _sds((W, D), BF16), _sds((W, 2 * D), BF16), _sds((W, D), BF16), _sds((W, D), BF16)],
                 scratch_shapes=[pltpu.VMEM((W, D), F32), pltpu.VMEM((W, 2 * D), F32), pltpu.VMEM((W, D), F32),
                                 pltpu.VMEM((W, D), F32)],
                 compiler_params=_params(("arbitrary",)))(dm, za, zb, zc, zd, p, p, p, p,
                                                          w_conv_out, w_glu, w_pool_out, w_sb_out)


def _adam_math(w, g, m, v):
    m2 = ADAM_B1 * m + (1.0 - ADAM_B1) * g
    v2 = ADAM_B2 * v + (1.0 - ADAM_B2) * (g * g)
    m_hat = m2 / (1.0 - ADAM_B1 ** ADAM_STEP)
    v_hat = v2 / (1.0 - ADAM_B2 ** ADAM_STEP)
    return -ADAM_LR * (m_hat / (jnp.sqrt(v_hat) + ADAM_EPS) + ADAM_WD * w), m2, v2


def _as_rows(a):
    return a.reshape(-1, a.shape[-1])


def _adamw(w, g, m, v):
    shape = w.shape
    w2, g2, m2, v2 = _as_rows(w), _as_rows(g), _as_rows(m), _as_rows(v)
    R, C = w2.shape
    tr = R
    for cand in (1024, 512, 256, 128, 64, 32, 16, 8):
        if R % cand == 0 and cand * C * 4 <= 2 * 1024 * 1024:
            tr = cand
            break

    def body(w_ref, g_ref, m_ref, v_ref, d_ref, m_out, v_out):
        d, mn, vn = _adam_math(w_ref[...], g_ref[...], m_ref[...], v_ref[...])
        d_ref[...] = d
        m_out[...] = mn
        v_out[...] = vn

    blk = pl.BlockSpec((tr, C), lambda i: (i, 0))
    shp = _sds((R, C), F32)
    outs = _call(body, name="adamw", grid=(R // tr,), in_specs=[blk] * 4, out_specs=[blk] * 3, out_shape=[shp] * 3,
                 compiler_params=_params(("parallel",)))(w2, g2, m2, v2)
    return tuple(o.reshape(shape) for o in outs)


def _sum_parts(parts, out_dtype, name):
    shape = parts[0].shape
    flat = [_as_rows(a) for a in parts]
    R, C = flat[0].shape
    tr = R
    for cand in (1024, 512, 256, 128, 64, 32, 16):
        if R % cand == 0 and cand * C * 4 <= 2 * 1024 * 1024:
            tr = cand
            break
    k = len(parts)

    def body(*refs):
        acc = refs[0][...].astype(F32)
        for r in refs[1:k]:
            acc = acc + r[...].astype(F32)
        refs[k][...] = acc.astype(out_dtype)

    blk = pl.BlockSpec((tr, C), lambda i: (i, 0))
    out = _call(body, name=name, grid=(R // tr,), in_specs=[blk] * k, out_specs=blk, out_shape=_sds((R, C), out_dtype),
                compiler_params=_params(("parallel",)))(*flat)
    return out.reshape(shape)


ADA_SHARD = 9 * D // N_CHIP
ADA_TN = 768


def _ada_fwd(c_pad, w_ada, b_ada_cols):
    depth = w_ada.shape[0]

    def body(c_ref, w_ref, b_ref, o_ref):
        cv = c_ref[...]
        o_ref[...] = _bdot(cv * _sigmoid(cv), w_ref[...]) + b_ref[...]

    return _call(body, name="ada_fwd", grid=(depth, ADA_SHARD // ADA_TN),
                 in_specs=[pl.BlockSpec((16, D), lambda l, j: (0, 0)),
                           pl.BlockSpec((None, D, ADA_TN), lambda l, j: (l, 0, j)),
                           pl.BlockSpec((None, 1, ADA_TN), lambda l, j: (l, 0, j))],
                 out_specs=pl.BlockSpec((None, 16, ADA_TN), lambda l, j: (l, 0, j)),
                 out_shape=_sds((depth, 16, ADA_SHARD), F32),
                 compiler_params=_params(("parallel", "parallel")))(c_pad, w_ada, b_ada_cols)


def _ada_wgrad(c_pad, d_ada):
    depth = d_ada.shape[0]

    def body(c_ref, d_ref, o_ref):
        cv = c_ref[...]
        o_ref[...] = _bdot(cv * _sigmoid(cv), d_ref[...], 0, 0)

    return _call(body, name="ada_wgrad", grid=(depth, ADA_SHARD // ADA_TN),
                 in_specs=[pl.BlockSpec((16, D), lambda l, j: (0, 0)),
                           pl.BlockSpec((None, 16, ADA_TN), lambda l, j: (l, 0, j))],
                 out_specs=pl.BlockSpec((None, D, ADA_TN), lambda l, j: (l, 0, j)),
                 out_shape=_sds((depth, D, ADA_SHARD), F32),
                 compiler_params=_params(("parallel", "parallel")))(c_pad, d_ada)


HBM_SPEC = pl.BlockSpec(memory_space=pltpu.HBM)


def _place():
    x, y, c = lax.axis_index("x"), lax.axis_index("y"), lax.axis_index("c")
    peers = [(1 - x, y), (x, 1 - y), (1 - x, 1 - y)]
    return x, y, c, peers


def _chip(px, py):
    return 2 * px + py


def _allgather8(block, name):
    m_per, n = block.shape

    def body(x_ref, out_ref, send_sems, recv_sems, local_sem):
        x, y, c, chips = _place()
        me, sibling = (x, y, c), (x, y, 1 - c)

        def rows(px, py, pc):
            return out_ref.at[pl.ds(pl.multiple_of((4 * px + 2 * py + pc) * m_per, 8), m_per), :]

        def copy(k, blk, to, src=None):
            return pltpu.make_async_remote_copy(
                src_ref=rows(*blk) if src is None else src, dst_ref=rows(*blk),
                send_sem=send_sems.at[k], recv_sem=recv_sems.at[k], device_id=to, device_id_type=MESH)

        mine = pltpu.make_async_copy(x_ref, rows(*me), local_sem)
        mine.start()
        first = [copy(0, me, sibling, src=x_ref)]
        first += [copy(1 + j, me, (*chip, c), src=x_ref) for j, chip in enumerate(chips)]
        for cp in first:
            cp.start()
        passed = [copy(4 + j, (*chip, c), sibling) for j, chip in enumerate(chips)]
        for j, chip in enumerate(chips):
            copy(1 + j, (*chip, c), me).wait_recv()
            passed[j].start()
        copy(0, sibling, me).wait_recv()
        for j, chip in enumerate(chips):
            copy(4 + j, (*chip, 1 - c), me).wait_recv()
        for cp in first + passed:
            cp.wait_send()
        mine.wait()

    return _call(body, name=name, out_shape=_sds((N_DEV * m_per, n), block.dtype),
                 in_specs=[pl.BlockSpec(memory_space=pltpu.VMEM)], out_specs=pl.BlockSpec(memory_space=pltpu.VMEM),
                 scratch_shapes=[pltpu.SemaphoreType.DMA((7,)), pltpu.SemaphoreType.DMA((7,)), pltpu.SemaphoreType.DMA],
                 compiler_params=_params())(block)


GATHERED = (("w_ff_in", 0, -1, 0), ("w_ff_out", 0, -2, 0),
            ("w_in", None, -1, 1), ("w_conv_out", None, -1, 1), ("w_glu", None, -1, 1), ("w_pool_out", None, -1, 1),
            ("w_sb_out", None, -1, 1), ("w_out", None, -2, 1),
            ("w_ff_in", 1, -1, 2), ("w_ff_out", 1, -2, 2))
N_SUB = 3


def _lead(ref):
    return (slice(None),) * (len(ref.shape) - 2)


def _mo(v, m):
    return v if isinstance(v, int) else pl.multiple_of(v, m)


def _full_region(ref, axis, j, half, shard_shape):
    rs, cs = shard_shape[-2], shard_shape[-1]
    if axis == -1:
        r0, nr = (0, rs) if half is None else (half * (rs // 2), rs // 2)
        return ref.at[_lead(ref) + (pl.ds(_mo(r0, 16), nr), pl.ds(_mo(j * cs, 128), cs))]
    r0, nr = (j * rs, rs) if half is None else (j * rs + half * (rs // 2), rs // 2)
    return ref.at[_lead(ref) + (pl.ds(_mo(r0, 16), nr), slice(None))]


def _shard_half(ref, half):
    rs = ref.shape[-2]
    return ref.at[_lead(ref) + (pl.ds(_mo(half * (rs // 2), 16), rs // 2), slice(None))]


def _full_shape(shard_shape, axis):
    s = list(shard_shape)
    s[axis] *= N_CHIP
    return tuple(s)


class _Lay:
    def __init__(self, shard_shape, axis):
        self.axis = axis
        self.shard_shape = tuple(shard_shape)
        self.full_shape = _full_shape(shard_shape, axis)
        self.lead = int(np.prod(shard_shape[:-2]))
        self.rs, self.cs = shard_shape[-2], shard_shape[-1]
        self.hr = self.rs // 2
        self.tr = next(t for t in (512, 256, 128, 64, 32, 16) if self.hr % t == 0 and t * self.cs * 4 <= (1 << 21))
        self.half_rows_shape = _half_rows_shape(self.full_shape)
        self.half_shard_shape = _half_rows_shape(self.shard_shape)

    def full(self, jf, hf):
        if self.axis == -1:
            return ((self.lead, 2, self.hr, N_CHIP * self.cs),
                    pl.BlockSpec((None, None, self.tr, self.cs), lambda b, j, i, s: (b, hf(j, s), i, jf(j, s))))
        return ((self.lead, N_CHIP, 2, self.hr, self.cs),
                pl.BlockSpec((None, None, None, self.tr, self.cs), lambda b, j, i, s: (b, jf(j, s), hf(j, s), i, 0)))

    def half_shard(self):
        return (self.lead, self.hr, self.cs), pl.BlockSpec((None, self.tr, self.cs), lambda b, j, i, s: (b, i, 0))

    def shard(self, hf):
        return ((self.lead, 2, self.hr, self.cs),
                pl.BlockSpec((None, None, self.tr, self.cs), lambda b, j, i, s: (b, hf(j, s), i, 0)))


def _view_sum(sel, operands, out_view, out_shape, out_dtype, grid, name):
    k = len(operands)

    def body(sel_ref, *refs):
        acc = refs[0][...].astype(F32)
        for r in refs[1:k]:
            acc = acc + r[...].astype(F32)
        refs[k][...] = acc.astype(out_dtype)

    spec = pltpu.PrefetchScalarGridSpec(num_scalar_prefetch=1, grid=grid, in_specs=[v[1] for _, v in operands],
                                        out_specs=out_view[1])
    out = _call(body, name=name, grid_spec=spec, out_shape=_sds(out_view[0], out_dtype),
                compiler_params=_params(("parallel", "parallel", "parallel")))(
                    sel, *[a.reshape(v[0]) for a, v in operands])
    return out.reshape(out_shape)


def _sel_core(j, s):
    return s[0]


def _sel_chip(j, s):
    return s[1]


def _grid_j(j, s):
    return j


def _place_shard(lay, sel, w):
    return _view_sum(sel, [(w, lay.shard(_grid_j))], lay.full(_sel_chip, _grid_j), lay.full_shape, BF16,
                     (lay.lead, 2, lay.hr // lay.tr), "place_shard")


SEM_SPEC = pl.BlockSpec(memory_space=pltpu.SEMAPHORE)
ANY_SPEC = pl.BlockSpec(memory_space=pl.ANY)
SPLIT_COPY = pltpu.SideEffectType.DATAFLOW_SIDE_EFFECTING


def _in_hbm(a):
    return pltpu.with_memory_space_constraint(a, pltpu.HBM)


def _gather_copies(lays, bufs, send_sems, recv_sems):
    x, y, c, chips = _place()
    copies = []
    for a, lay in enumerate(lays):
        own = _full_region(bufs[a], lay.axis, _chip(x, y), c, lay.shard_shape)
        for k, chip in enumerate(chips):
            copies.append(pltpu.make_async_remote_copy(
                src_ref=own, dst_ref=own, send_sem=send_sems.at[a * 3 + k], recv_sem=recv_sems.at[a * 3 + k],
                device_id=(*chip, c), device_id_type=MESH))
    return copies


def _gather_start(fulls, after, lays, tag):
    n = len(fulls)

    def body(*refs):
        send_sems, recv_sems = refs[n + 1], refs[n + 2]
        bufs, token = refs[n + 3:2 * n + 3], refs[2 * n + 3]
        for cp in _gather_copies(lays, bufs, send_sems, recv_sems):
            cp.start()
        token[...] = jnp.zeros_like(token)

    outs = _call(body, name="gather_start_" + tag,
                 out_shape=[pltpu.SemaphoreType.DMA((3 * n,)), pltpu.SemaphoreType.DMA((3 * n,))]
                 + [pltpu.HBM(f.shape, f.dtype) for f in fulls] + [_sds((8, 128), F32)],
                 in_specs=[HBM_SPEC] * n + [ANY_SPEC],
                 out_specs=[SEM_SPEC, SEM_SPEC] + [HBM_SPEC] * n + [pl.BlockSpec(memory_space=pltpu.VMEM)],
                 input_output_aliases={a: a + 2 for a in range(n)},
                 compiler_params=pltpu.CompilerParams(has_side_effects=SPLIT_COPY))(*[_in_hbm(f) for f in fulls], after)
    return outs[0], outs[1], outs[2:2 + n], outs[2 + n]


def _gather_wait(send_sems, recv_sems, bufs, after, lays, tag):
    n = len(bufs)

    def body(*refs):
        ss, rs = refs[n], refs[n + 1]
        for cp in _gather_copies(lays, refs[n + 3:], ss, rs):
            cp.wait_send()
            cp.wait_recv()

    return _call(body, name="gather_wait_" + tag,
                 out_shape=[pltpu.HBM(b.shape, b.dtype) for b in bufs],
                 in_specs=[HBM_SPEC] * n + [SEM_SPEC, SEM_SPEC, ANY_SPEC], out_specs=[HBM_SPEC] * n,
                 input_output_aliases={a: a for a in range(n)},
                 compiler_params=pltpu.CompilerParams(has_side_effects=SPLIT_COPY))(*bufs, send_sems, recv_sems, after)


def _gather_forward(bufs, lays):
    n = len(bufs)

    def body(*refs):
        outs = refs[n:2 * n]
        send_sems, recv_sems = refs[2 * n:]
        x, y, c, chips = _place()
        sibling = (x, y, 1 - c)
        sends = []
        for a in range(n):
            for k, chip in enumerate(chips):
                landed = _full_region(outs[a], lays[a].axis, _chip(*chip), c, lays[a].shard_shape)
                cp = pltpu.make_async_remote_copy(
                    src_ref=landed, dst_ref=landed, send_sem=send_sems.at[a * 3 + k], recv_sem=recv_sems.at[a * 3 + k],
                    device_id=sibling, device_id_type=MESH)
                cp.start()
                sends.append(cp)
        for a in range(n):
            for k, chip in enumerate(chips):
                passed = _full_region(outs[a], lays[a].axis, _chip(*chip), 1 - c, lays[a].shard_shape)
                pltpu.make_async_remote_copy(
                    src_ref=passed, dst_ref=passed, send_sem=send_sems.at[a * 3 + k], recv_sem=recv_sems.at[a * 3 + k],
                    device_id=sibling, device_id_type=MESH).wait_recv()
        for cp in sends:
            cp.wait_send()

    return _call(body, name="gather_forward",
                 out_shape=[_sds(b.shape, b.dtype) for b in bufs],
                 in_specs=[HBM_SPEC] * n, out_specs=[HBM_SPEC] * n,
                 input_output_aliases={a: a for a in range(n)},
                 scratch_shapes=[pltpu.SemaphoreType.DMA((3 * n,)), pltpu.SemaphoreType.DMA((3 * n,))],
                 compiler_params=_params())(*bufs)


def _half_rows_shape(full_shape):
    s = list(full_shape)
    s[-2] //= 2
    return tuple(s)


RELATIONS = tuple((r, s) for r in range(N_CHIP) for s in range(2))[1:]


def _peer(x, y, c, rel):
    r, s = rel
    return (1 - x if r in (1, 3) else x, 1 - y if r in (2, 3) else y, 1 - c if s else c)


def _reduce_copies(lays, grads, landing, send_sems, recv_sems):
    x, y, c, _ = _place()
    nr = len(RELATIONS)
    copies = []
    for a, lay in enumerate(lays):
        for k, rel in enumerate(RELATIONS):
            px, py, pc = _peer(x, y, c, rel)
            copies.append(pltpu.make_async_remote_copy(
                src_ref=_full_region(grads[a], lay.axis, _chip(px, py), pc, lay.shard_shape), dst_ref=landing[a * nr + k],
                send_sem=send_sems.at[a * nr + k], recv_sem=recv_sems.at[a * nr + k],
                device_id=(px, py, pc), device_id_type=MESH))
    return copies


def _reduce_start(grads, lays, tag):
    n, nr = len(grads), len(RELATIONS)
    landing = [_in_hbm(lax.empty(lay.half_shard_shape, BF16)) for lay in lays for _ in RELATIONS]
    m = n + n * nr

    def body(*refs):
        send_sems, recv_sems = refs[m], refs[m + 1]
        src, land, token = refs[m + 2:m + 2 + n], refs[m + 2 + n:2 * m + 2], refs[2 * m + 2]
        for cp in _reduce_copies(lays, src, land, send_sems, recv_sems):
            cp.start()
        token[...] = jnp.zeros_like(token)

    ops = [_in_hbm(g) for g in grads] + landing
    outs = _call(body, name="reduce_start_" + tag,
                 out_shape=[pltpu.SemaphoreType.DMA((n * nr,)), pltpu.SemaphoreType.DMA((n * nr,))]
                 + [pltpu.HBM(o.shape, o.dtype) for o in ops] + [_sds((8, 128), F32)],
                 in_specs=[HBM_SPEC] * m,
                 out_specs=[SEM_SPEC, SEM_SPEC] + [HBM_SPEC] * m + [pl.BlockSpec(memory_space=pltpu.VMEM)],
                 input_output_aliases={a: a + 2 for a in range(m)},
                 compiler_params=pltpu.CompilerParams(has_side_effects=SPLIT_COPY))(*ops)
    return outs[0], outs[1], outs[2:2 + n], outs[2 + n:2 + m], outs[2 + m]


def _reduce_wait(send_sems, recv_sems, grads, landing, after, lays, tag):
    n, nr = len(grads), len(RELATIONS)
    m = n + n * nr

    def body(*refs):
        ss, rs = refs[m], refs[m + 1]
        src, land = refs[m + 3:m + 3 + n], refs[m + 3 + n:]
        for cp in _reduce_copies(lays, src, land, ss, rs):
            cp.wait_send()
            cp.wait_recv()

    ops = list(grads) + list(landing)
    outs = _call(body, name="reduce_wait_" + tag,
                 out_shape=[pltpu.HBM(o.shape, o.dtype) for o in ops],
                 in_specs=[HBM_SPEC] * m + [SEM_SPEC, SEM_SPEC, ANY_SPEC], out_specs=[HBM_SPEC] * m,
                 input_output_aliases={a: a for a in range(m)},
                 compiler_params=pltpu.CompilerParams(has_side_effects=SPLIT_COPY))(*ops, send_sems, recv_sems, after)
    return outs[:n], [outs[n + nr * a:n + nr * a + nr] for a in range(n)]


def _share_halves(shards):
    n = len(shards)

    def body(*refs):
        outs = refs[n:2 * n]
        send_sems, recv_sems = refs[2 * n:]
        x, y, c, _ = _place()
        sibling = (x, y, 1 - c)
        started = []
        for a in range(n):
            mine = _shard_half(outs[a], c)
            rc = pltpu.make_async_remote_copy(src_ref=mine, dst_ref=mine, send_sem=send_sems.at[a],
                                              recv_sem=recv_sems.at[a], device_id=sibling, device_id_type=MESH)
            rc.start()
            started.append(rc)
        for rc in started:
            rc.wait_recv()
            rc.wait_send()

    return _call(body, name="share_halves", out_shape=[_sds(s.shape, F32) for s in shards],
                 in_specs=[HBM_SPEC] * n, out_specs=[HBM_SPEC] * n, input_output_aliases={a: a for a in range(n)},
                 scratch_shapes=[pltpu.SemaphoreType.DMA((n,)), pltpu.SemaphoreType.DMA((n,))],
                 compiler_params=_params())(*shards)


def _reduce_end(state, after, lays, sel, tag):
    send_sems, recv_sems, grads, landing, _ = state
    grads, landed = _reduce_wait(send_sems, recv_sems, grads, landing, after, lays, tag)
    halves = [
        _view_sum(sel, [(g, lay.full(_sel_chip, _sel_core))] + [(l, lay.half_shard()) for l in ls], lay.shard(_sel_core),
                  lay.shard_shape, F32, (lay.lead, 1, lay.hr // lay.tr), "shard_half_sum")
        for g, ls, lay in zip(grads, landed, lays)]
    return _share_halves(halves)


def _embed(blocks):
    n, r, c = blocks.shape
    eye = jnp.eye(n, dtype=blocks.dtype)
    return (blocks[:, :, None, :] * eye[:, None, :, None]).reshape(n * r, n * c)


def _unembed(mat, n):
    r, c = mat.shape[0] // n, mat.shape[1] // n
    return jnp.transpose(jnp.diagonal(mat.reshape(n, r, n, c), axis1=0, axis2=2), (2, 0, 1))


def _to_heads(a):
    return jnp.transpose(a.reshape(a.shape[0], HEADS, HD), (1, 0, 2))


def _from_heads(a):
    return jnp.transpose(a, (1, 0, 2)).reshape(a.shape[1], W)


def _row(v):
    return v.reshape(1, -1)


def _ffn_fwd(x, ada, gp, gq, w_in, w_out, s):
    L = x.shape[0]
    h = _norm_mod(x, _row(gp[s]), _row(ada[3 * s]), _row(ada[3 * s + 1]))
    a, b, act = _ffn_in(h, w_in)
    f = _mm(act, w_out, M=L, N=D, K=FF, tm=min(L, 1024), tn=512, name="ffn_out")
    x2 = _post(x, f, _row(gq[s]), _row(ada[3 * s + 2]), 0.5)
    return x2, (x, h, a, b, act, f)


def _ffn_bwd(dx, saved, ada, gp, gq, w_in, w_out, s, stats):
    x, h, a, b, act, f = saved
    L = x.shape[0]
    df, stats = _post_bwd(dx, f, _row(gq[s]), _row(ada[3 * s + 2]), 0.5, stats, s)
    dw_out = _mm(act, df, M=FF, N=D, K=L, tm=256, tn=1024, ta=True, out_dtype=BF16, name="ffn_dw_out")
    da, db = _ffn_mid_bwd(df, w_out, a, b)
    du = jnp.concatenate([da, db], axis=1)
    dw_in = _mm(h, du, M=D, N=2 * FF, K=L, tm=1024, tn=512, ta=True, out_dtype=BF16, name="ffn_dw_in")
    dh = _mm(du, w_in, M=L, N=D, K=2 * FF, tm=min(L, 1024), tn=1024, tk=1408, tb=True, name="ffn_dh")
    dx2, stats = _norm_mod_bwd(dh, x, _row(gp[s]), _row(ada[3 * s + 1]), dx, stats, s)
    return dx2, dw_in, dw_out, stats


def _mixer_fwd(x, ada, gp, gq, wf, sm):
    L = x.shape[0]
    h = _norm_mod(x, _row(gp[1]), _row(ada[3]), _row(ada[4]))
    p = _mm(h, wf["w_in"], M=L, N=IN_COLS, K=D, tm=min(L, 1024), tn=512, name="mixer_in")
    za = _conv_fwd(p, sm["conv_w"])
    y, zb = _ssm_fwd(p, sm["b_re"], sm["b_im"], sm["c_re"], sm["c_im"], sm["abr"], sm["abi"], sm["fr"], sm["fi"], sm["ssm_d"])
    zc = _pool_fwd(p, sm["w_pool"], sm["pool_scale"])
    q = _to_heads(p[:, 5 * W:6 * W]) * (HD ** -0.5)
    k = _to_heads(p[:, 6 * W:7 * W])
    v = _to_heads(p[:, 7 * W:8 * W])
    o_heads, block_sums = _sb_fwd(q, k, v)
    zd = _from_heads(o_heads).astype(BF16)
    merged = _merge_fwd(za, zb, zc, zd, p, wf["w_conv_out"], wf["w_glu"], wf["w_pool_out"], wf["w_sb_out"])
    m = _mm(merged, wf["w_out"], M=L, N=D, K=D, tm=min(L, 1024), tn=512, name="mixer_out")
    x2 = _post(x, m, _row(gq[1]), _row(ada[5]), 1.0)
    return x2, (x, h, p, za, y, zb, zc, zd, q, k, v, block_sums, merged, m)


def _mixer_bwd(dx, saved, ada, gp, gq, wf, sm, stats):
    x, h, p, za, y, zb, zc, zd, q, k, v, block_sums, merged, m = saved
    L = x.shape[0]
    dmf, stats = _post_bwd(dx, m, _row(gq[1]), _row(ada[5]), 1.0, stats, 1)
    dw_out = _mm(merged, dmf, M=D, N=D, K=L, tm=512, tn=512, ta=True, out_dtype=BF16, name="mixer_dw_out")
    dmerged = _mm(dmf, wf["w_out"], M=L, N=D, K=D, tm=min(L, 1024), tn=512, tb=True, name="mixer_dmerged")
    dza, dzb, dzc, dzd, dgates, dwc, dwg, dwp, dws = _merge_bwd(
        dmerged, za, zb, zc, zd, p, wf["w_conv_out"], wf["w_glu"], wf["w_pool_out"], wf["w_sb_out"])
    dconv, dconv_w = _conv_bwd(p, sm["conv_w"], dza)
    (du_ssm, dd, dbr, dbi, dcr, dci, gar, gai, gfr, gfi) = _ssm_bwd(
        p, y, dzb, sm["b_re"], sm["b_im"], sm["c_re"], sm["c_im"], sm["abr"], sm["abi"], sm["fr"], sm["fi"], sm["ssm_d"])
    du_pool, dwpool, dpscale = _pool_bwd(p, sm["w_pool"], sm["pool_scale"], dzc)
    dq, dk, dv = _sb_bwd(q, k, v, _to_heads(dzd), block_sums)
    dqkv = [_from_heads(t).astype(BF16) for t in (dq * (HD ** -0.5), dk, dv)]
    dp = jnp.concatenate([dconv, du_ssm, du_pool] + dqkv + [dgates], axis=1)
    dw_in = _mm(h, dp, M=D, N=IN_COLS, K=L, tm=1024, tn=512, ta=True, out_dtype=BF16, name="mixer_dw_in")
    dh = _mm(dp, wf["w_in"], M=L, N=D, K=IN_COLS, tm=min(L, 1024), tn=1024, tk=1536, tb=True, name="mixer_dh")
    dx2, stats = _norm_mod_bwd(dh, x, _row(gp[1]), _row(ada[4]), dx, stats, 1)
    wgrads = [dw_in, dwc, dwg, dwp, dws, dw_out]
    small = {"conv_w": dconv_w, "ssm_d": dd, "b_re": dbr, "b_im": dbi, "c_re": dcr, "c_im": dci,
             "abr": gar, "abi": gai, "fr": gfr, "fi": gfi, "w_pool": dwpool, "pool_scale": dpscale}
    return dx2, wgrads, small, stats


def _pack(arrays):
    flat = jnp.concatenate([a.reshape(-1) for a in arrays])
    rows = -(-flat.shape[0] // 128)
    rows = -(-rows // 64) * 64
    return jnp.pad(flat, (0, rows * 128 - flat.shape[0])).reshape(rows, 128)


def _unpack(block, shapes):
    flat = block.reshape(-1)
    out, off = [], 0
    for s in shapes:
        n = int(np.prod(s))
        out.append(flat[off:off + n].reshape(s))
        off += n
    return out


def _pad_rows(a, mult):
    rows = -(-a.shape[0] // mult) * mult
    return jnp.concatenate([a] * (-(-rows // a.shape[0])), axis=0)[:rows]


SMALL_ORDER = ("stats", "conv_w", "lam_re", "lam_im", "log_dt", "ssm_b_re", "ssm_b_im",
               "ssm_c_re", "ssm_c_im", "ssm_d", "w_pool", "pool_scale")
WEIGHTS = ('w_ada', 'b_ada', 'g_pre', 'g_post', 'w_ff_in', 'w_ff_out', 'w_in', 'conv_w', 'w_conv_out', 'lam_re', 'lam_im',
           'log_dt', 'ssm_b_re', 'ssm_b_im', 'ssm_c_re', 'ssm_c_im', 'ssm_d', 'w_glu', 'w_pool', 'pool_scale', 'w_pool_out',
           'w_sb_out', 'w_out')


def _step(a):
    depth = a["w_ada"].shape[0]
    x = a["x"][0]
    target = a["loss_target"][0]
    L = x.shape[0]
    ix, iy, ic = lax.axis_index("x"), lax.axis_index("y"), lax.axis_index("c")
    chip = 2 * ix + iy
    me = 4 * ix + 2 * iy + ic
    sel = jnp.stack([ic, chip]).astype(jnp.int32)
    lays = [_Lay(a[name].shape[(1 if idx is None else 2):], ax) for name, idx, ax, _ in GATHERED]

    first_shapes = [(D,), (depth, 3, W), (depth, 3, W), (depth, 3, W // N_CHIP)]
    gathered = _allgather8(_pack([a["c"], a["g_pre"], a["g_post"], a["conv_w"]]), "gather_small_inputs")
    per_dev = [_unpack(blk, first_shapes) for blk in gathered.reshape(N_DEV, -1, 128)]
    c_all = jnp.stack([d[0] for d in per_dev])
    c_pad = jnp.concatenate([c_all, jnp.zeros_like(c_all)], axis=0)
    g_pre = jnp.concatenate([per_dev[2 * j][1] for j in range(N_CHIP)], axis=-1)
    g_post = jnp.concatenate([per_dev[2 * j][2] for j in range(N_CHIP)], axis=-1)
    conv_w = jnp.concatenate([per_dev[2 * j][3] for j in range(N_CHIP)], axis=-1)

    b_cols = lax.dynamic_slice(a["b_ada"], (0, chip * ADA_SHARD), (depth, ADA_SHARD)).reshape(depth, 1, ADA_SHARD)
    ada_part = _ada_fwd(c_pad, a["w_ada"], b_cols)
    ada_all = _allgather8(ada_part.reshape(depth * 16, ADA_SHARD), "gather_ada").reshape(N_DEV, depth, 16, ADA_SHARD)
    ada_rows = lax.dynamic_slice(ada_all, (0, 0, me, 0), (N_DEV, depth, 1, ADA_SHARD))[:, :, 0]
    ada = jnp.concatenate([ada_rows[2 * j] for j in range(N_CHIP)], axis=-1).reshape(depth, 9, D)

    lam_re = _pad_rows(a["lam_re"].reshape(depth, NST), 8)
    lam_im = _pad_rows(a["lam_im"].reshape(depth, NST), 8)
    log_dt_x = _pad_rows(jnp.repeat(a["log_dt"], GP, axis=1), 8)
    abr, abi, fr, fi = _ssm_prep(lam_re, lam_im, log_dt_x)

    def small_of(l):
        return {"conv_w": conv_w[l], "ssm_d": _row(a["ssm_d"][l]), "pool_scale": _row(a["pool_scale"][l]),
                "b_re": _embed(jnp.transpose(a["ssm_b_re"][l], (0, 2, 1))), "b_im": _embed(jnp.transpose(a["ssm_b_im"][l], (0, 2, 1))),
                "c_re": _embed(jnp.transpose(a["ssm_c_re"][l], (0, 2, 1))), "c_im": _embed(jnp.transpose(a["ssm_c_im"][l], (0, 2, 1))),
                "w_pool": _embed(a["w_pool"][l]),
                "abr": abr[l:l + 1], "abi": abi[l:l + 1], "fr": fr[l:l + 1], "fi": fi[l:l + 1]}

    def entries(g):
        return [i for i, e in enumerate(GATHERED) if e[3] == g]

    def shard_of(i, l):
        name, idx = GATHERED[i][0], GATHERED[i][1]
        return a[name][l] if idx is None else a[name][l, idx]

    stages = [(l, g) for l in range(depth) for g in range(N_SUB)]

    def gather_begin(t, after):
        l, g = stages[t]
        placed = [_place_shard(lays[i], sel, shard_of(i, l)) for i in entries(g)]
        return _gather_start(placed, after, [lays[i] for i in entries(g)], str(t))

    saved, weights, smalls = [], [], [small_of(l) for l in range(depth)]
    pending = {t: gather_begin(t, x) for t in range(min(2, len(stages)))}
    for t, (l, g) in enumerate(stages):
        glays = [lays[i] for i in entries(g)]
        send_sems, recv_sems, bufs, _ = pending.pop(t)
        w = _gather_forward(_gather_wait(send_sems, recv_sems, bufs, x if t else ada, glays, str(t)), glays)
        weights.append(w)
        ada_l = ada[l]
        if t + 2 < len(stages):
            pending[t + 2] = gather_begin(t + 2, x)
            ada_l = ada_l + pending[t + 2][3][0, 0]
        if g == 1:
            wf = {GATHERED[i][0]: wi for i, wi in zip(entries(1), w)}
            x, sv = _mixer_fwd(x, ada_l, g_pre[l], g_post[l], wf, smalls[l])
        else:
            x, sv = _ffn_fwd(x, ada_l, g_pre[l], g_post[l], w[0], w[1], g)
        saved.append(sv)
    dx, loss_part = _loss_head(x, target)
    loss = lax.psum(loss_part[0, 0], ("x", "y", "c"))

    shard_grads = [[None] * depth for _ in GATHERED]
    small_grads = [{} for _ in range(depth)]
    stats = [jnp.zeros((STAT_ROWS, D), F32) for _ in range(depth)]
    states = {}

    def reduce_finish(t, after):
        l, g = stages[t]
        glays = [lays[i] for i in entries(g)]
        for i, grad in zip(entries(g), _reduce_end(states.pop(t), after, glays, sel, str(t))):
            shard_grads[i][l] = grad

    token = None
    for t in reversed(range(len(stages))):
        l, g = stages[t]
        ada_l = ada[l] if token is None else ada[l] + token[0, 0]
        if g == 1:
            wf = {GATHERED[i][0]: wi for i, wi in zip(entries(1), weights[t])}
            dx, wgrads, small, stats[l] = _mixer_bwd(dx, saved[t], ada_l, g_pre[l], g_post[l], wf, smalls[l], stats[l])
            small_grads[l].update(small)
        else:
            dx, dw_in, dw_out, stats[l] = _ffn_bwd(dx, saved[t], ada_l, g_pre[l], g_post[l], weights[t][0], weights[t][1], g,
                                                   stats[l])
            wgrads = [dw_in, dw_out]
        states[t] = _reduce_start(wgrads, [lays[i] for i in entries(g)], str(t))
        token = states[t][4]
        if t + 2 in states:
            reduce_finish(t + 2, dx)
    stack = lambda key: _pad_rows(jnp.concatenate([small_grads[l][key] for l in range(depth)], axis=0), 8)
    gs = np.zeros((NST, 128), np.float32)
    gs[np.arange(NST), np.arange(NST) // GP] = 1.0
    dlr, dli, dldt = _ssm_prep_bwd(lam_re, lam_im, log_dt_x, stack("abr"), stack("abi"), stack("fr"), stack("fi"), jnp.asarray(gs))
    part = {
        "stats": jnp.stack(stats),
        "conv_w": jnp.stack([small_grads[l]["conv_w"] for l in range(depth)]),
        "lam_re": dlr[:depth].reshape(depth, G, GP), "lam_im": dli[:depth].reshape(depth, G, GP), "log_dt": dldt[:depth, :G],
        "ssm_b_re": jnp.stack([jnp.transpose(_unembed(small_grads[l]["b_re"], G), (0, 2, 1)) for l in range(depth)]),
        "ssm_b_im": jnp.stack([jnp.transpose(_unembed(small_grads[l]["b_im"], G), (0, 2, 1)) for l in range(depth)]),
        "ssm_c_re": jnp.stack([jnp.transpose(_unembed(small_grads[l]["c_re"], G), (0, 2, 1)) for l in range(depth)]),
        "ssm_c_im": jnp.stack([jnp.transpose(_unembed(small_grads[l]["c_im"], G), (0, 2, 1)) for l in range(depth)]),
        "ssm_d": jnp.stack([small_grads[l]["ssm_d"][0] for l in range(depth)]),
        "w_pool": jnp.stack([_unembed(small_grads[l]["w_pool"], len(POOL_WINDOWS)) for l in range(depth)]),
        "pool_scale": jnp.stack([small_grads[l]["pool_scale"][0] for l in range(depth)]),
    }
    small_shapes = [part[k].shape for k in SMALL_ORDER]
    blocks = _allgather8(_pack([part[k] for k in SMALL_ORDER]), "gather_small_grads").reshape(N_DEV, -1, 128)
    small_sum = _sum_parts([blocks[i] for i in range(N_DEV)], F32, "small_grad_sum")
    total = dict(zip(SMALL_ORDER, _unpack(small_sum, small_shapes)))
    d_ada_all = jnp.stack([_unpack(blocks[i], small_shapes[:1])[0][:, :9].reshape(depth, 9 * D) for i in range(N_DEV)])
    d_cols = lax.dynamic_slice(d_ada_all, (0, 0, chip * ADA_SHARD), (N_DEV, depth, ADA_SHARD))
    d_cols = jnp.transpose(d_cols, (1, 0, 2))
    grads = {"w_ada": _ada_wgrad(c_pad, jnp.concatenate([d_cols, jnp.zeros_like(d_cols)], axis=1)),
             "b_ada": total["stats"][:, :9].reshape(depth, 9 * D),
             "g_pre": lax.dynamic_slice(total["stats"], (0, 9, chip * W), (depth, 3, W)),
             "g_post": lax.dynamic_slice(total["stats"], (0, 12, chip * W), (depth, 3, W)),
             "conv_w": lax.dynamic_slice(total["conv_w"], (0, 0, chip * (W // N_CHIP)), (depth, 3, W // N_CHIP))}
    for k in SMALL_ORDER[2:]:
        grads[k] = total[k]

    out = {"loss": loss, "grad_x": dx[None]}

    def update(name):
        out["grad_" + name] = grads[name]
        out["delta_" + name], out["new_m_" + name], out["new_v_" + name] = _adamw(a[name], grads[name], a["m_" + name], a["v_" + name])

    for name in WEIGHTS:
        if name in grads:
            update(name)
    for t in sorted(states, reverse=True):
        reduce_finish(t, out["delta_w_ada"])
    for name in sorted({e[0] for e in GATHERED}):
        cols = [shard_grads[i] for i, e in enumerate(GATHERED) if e[0] == name]
        grads[name] = jnp.stack(cols[0]) if len(cols) == 1 else jnp.stack([jnp.stack(pair) for pair in zip(*cols)])
        update(name)
    return out


def kernel(x, c, w_ada, b_ada, g_pre, g_post, w_ff_in, w_ff_out, w_in, conv_w, w_conv_out, lam_re, lam_im, log_dt, ssm_b_re, ssm_b_im, ssm_c_re, ssm_c_im, ssm_d, w_glu, w_pool, pool_scale, w_pool_out, w_sb_out, w_out, loss_target, m_w_ada, m_b_ada, m_g_pre, m_g_post, m_w_ff_in, m_w_ff_out, m_w_in, m_conv_w, m_w_conv_out, m_lam_re, m_lam_im, m_log_dt, m_ssm_b_re, m_ssm_b_im, m_ssm_c_re, m_ssm_c_im, m_ssm_d, m_w_glu, m_w_pool, m_pool_scale, m_w_pool_out, m_w_sb_out, m_w_out, v_w_ada, v_b_ada, v_g_pre, v_g_post, v_w_ff_in, v_w_ff_out, v_w_in, v_conv_w, v_w_conv_out, v_lam_re, v_lam_im, v_log_dt, v_ssm_b_re, v_ssm_b_im, v_ssm_c_re, v_ssm_c_im, v_ssm_d, v_w_glu, v_w_pool, v_pool_scale, v_w_pool_out, v_w_sb_out, v_w_out):
    out = _step(dict(locals()))
    names = ["loss", "grad_x"] + [p + n for p in ("grad_", "delta_", "new_m_", "new_v_") for n in WEIGHTS]
    return tuple(out[n] for n in names)
```

```python
import functools
import math

import jax
import jax.numpy as jnp
import numpy as np
from jax import lax
from jax.experimental import pallas as pl
from jax.experimental.pallas import tpu as pltpu

F32 = jnp.float32
BF16 = jnp.bfloat16
MESH = pl.DeviceIdType.MESH

D = 1024
W = 256
FF = 2816
IN_COLS = 6144
G = 16
GH = 16
GP = 64
NST = G * GP
QB = 128
HEADS = 4
HD = 64
EPS = 1e-6
LAMBDA_RE_MAX = -1e-4
POOL_WINDOWS = (2, 4, 8, 16)
N_CHIP = 4
N_DEV = 8
VMEM_LIMIT = 56 * 1024 * 1024
HIGH = lax.Precision.HIGHEST

ADAM_LR, ADAM_B1, ADAM_B2, ADAM_EPS, ADAM_WD, ADAM_STEP = 0.001, 0.9, 0.999, 1e-08, 0.01, 10


def _call(body, **kw):
    return pl.pallas_call(body, **kw)


def _params(dims=None, **kw):
    return pltpu.CompilerParams(dimension_semantics=dims, vmem_limit_bytes=VMEM_LIMIT, **kw)


def _sds(shape, dtype):
    return jax.ShapeDtypeStruct(shape, dtype)


def _dot(a, b, ca=1, cb=0, precision=None):
    return lax.dot_general(a, b, (((ca,), (cb,)), ((), ())), preferred_element_type=F32, precision=precision)


def _bdot(a, b, ca=1, cb=0):
    return _dot(a.astype(BF16), b.astype(BF16), ca, cb)


def _sigmoid(x):
    return 1.0 / (1.0 + jnp.exp(-x))


def _mm(a, b, *, M, N, K, tm, tn, tk=None, ta=False, tb=False, out_dtype=F32, a_off=(0, 0), b_off=(0, 0), name):
    tk = K if tk is None else tk
    nk = K // tk
    assert M % tm == 0 and N % tn == 0 and K % tk == 0

    def body(a_ref, b_ref, o_ref, *acc):
        part = _bdot(a_ref[...], b_ref[...], 0 if ta else 1, 1 if tb else 0)
        if nk == 1:
            o_ref[...] = part.astype(out_dtype)
            return
        acc_ref = acc[0]
        k = pl.program_id(2)

        @pl.when(k == 0)
        def _():
            acc_ref[...] = part

        @pl.when(k > 0)
        def _():
            acc_ref[...] += part

        @pl.when(k == nk - 1)
        def _():
            o_ref[...] = acc_ref[...].astype(out_dtype)

    if ta:
        a_spec = pl.BlockSpec((tk, tm), lambda i, j, k: (k + a_off[0], i + a_off[1]))
    else:
        a_spec = pl.BlockSpec((tm, tk), lambda i, j, k: (i + a_off[0], k + a_off[1]))
    if tb:
        b_spec = pl.BlockSpec((tn, tk), lambda i, j, k: (j + b_off[0], k + b_off[1]))
    else:
        b_spec = pl.BlockSpec((tk, tn), lambda i, j, k: (k + b_off[0], j + b_off[1]))
    return _call(
        body, name=name, grid=(M // tm, N // tn, nk),
        in_specs=[a_spec, b_spec],
        out_specs=pl.BlockSpec((tm, tn), lambda i, j, k: (i, j)),
        out_shape=_sds((M, N), out_dtype),
        scratch_shapes=[] if nk == 1 else [pltpu.VMEM((tm, tn), F32)],
        compiler_params=_params(("parallel", "parallel", "arbitrary")),
    )(a, b)


def _row_tile(L):
    return min(L, 256)


def _norm_mod(x, g, shift, scale):
    L = x.shape[0]
    tr = _row_tile(L)

    def body(x_ref, g_ref, sh_ref, sc_ref, h_ref):
        xv = x_ref[...]
        r = lax.rsqrt(jnp.mean(xv * xv, axis=-1, keepdims=True) + EPS)
        h_ref[...] = (xv * r * g_ref[...] * (1.0 + sc_ref[...]) + sh_ref[...]).astype(BF16)

    row = pl.BlockSpec((tr, D), lambda i: (i, 0))
    vec = pl.BlockSpec((1, D), lambda i: (0, 0))
    return _call(body, name="norm_mod", grid=(L // tr,), in_specs=[row, vec, vec, vec], out_specs=row,
                 out_shape=_sds((L, D), BF16), compiler_params=_params(("parallel",)))(x, g, shift, scale)


STAT_ROWS = 16


def _norm_mod_bwd(dh, x, g, scale, dx_res, stats, s):
    L = x.shape[0]
    tr = _row_tile(L)

    def body(dh_ref, x_ref, g_ref, sc_ref, dxr_ref, stin_ref, dx_ref, st_ref):
        i = pl.program_id(0)
        xv = x_ref[...]
        dhv = dh_ref[...]
        r = lax.rsqrt(jnp.mean(xv * xv, axis=-1, keepdims=True) + EPS)
        y = xv * r
        n = y * g_ref[...]
        dn = dhv * (1.0 + sc_ref[...])
        dy = dn * g_ref[...]
        dx_ref[...] = dxr_ref[...] + r * (dy - y * jnp.mean(dy * y, axis=-1, keepdims=True))

        @pl.when(i == 0)
        def _():
            st_ref[...] = stin_ref[...]

        st_ref[3 * s:3 * s + 1, :] += jnp.sum(dhv, axis=0, keepdims=True)
        st_ref[3 * s + 1:3 * s + 2, :] += jnp.sum(dhv * n, axis=0, keepdims=True)
        st_ref[9 + s:10 + s, :] += jnp.sum(dn * y, axis=0, keepdims=True)

    row = pl.BlockSpec((tr, D), lambda i: (i, 0))
    vec = pl.BlockSpec((1, D), lambda i: (0, 0))
    st = pl.BlockSpec((STAT_ROWS, D), lambda i: (0, 0))
    return _call(body, name="norm_mod_bwd", grid=(L // tr,), in_specs=[row, row, vec, vec, row, st],
                 out_specs=[row, st], out_shape=[_sds((L, D), F32), _sds((STAT_ROWS, D), F32)],
                 input_output_aliases={5: 1},
                 compiler_params=_params(("arbitrary",)))(dh, x, g, scale, dx_res, stats)


def _post(x, f, g, gate, res_weight):
    L = x.shape[0]
    tr = _row_tile(L)

    def body(x_ref, f_ref, g_ref, gt_ref, o_ref):
        fv = f_ref[...]
        r = lax.rsqrt(jnp.mean(fv * fv, axis=-1, keepdims=True) + EPS)
        o_ref[...] = x_ref[...] + (res_weight * (1.0 + gt_ref[...])) * (fv * r * g_ref[...])

    row = pl.BlockSpec((tr, D), lambda i: (i, 0))
    vec = pl.BlockSpec((1, D), lambda i: (0, 0))
    return _call(body, name="post", grid=(L // tr,), in_specs=[row, row, vec, vec], out_specs=row,
                 out_shape=_sds((L, D), F32), compiler_params=_params(("parallel",)))(x, f, g, gate)


def _post_bwd(dx, f, g, gate, res_weight, stats, s):
    L = dx.shape[0]
    tr = _row_tile(L)

    def body(dx_ref, f_ref, g_ref, gt_ref, stin_ref, df_ref, st_ref):
        i = pl.program_id(0)
        fv = f_ref[...]
        dxv = dx_ref[...]
        r = lax.rsqrt(jnp.mean(fv * fv, axis=-1, keepdims=True) + EPS)
        y = fv * r
        dn = dxv * (res_weight * (1.0 + gt_ref[...]))
        dy = dn * g_ref[...]
        df_ref[...] = (r * (dy - y * jnp.mean(dy * y, axis=-1, keepdims=True))).astype(BF16)

        @pl.when(i == 0)
        def _():
            st_ref[...] = stin_ref[...]

        st_ref[3 * s + 2:3 * s + 3, :] += res_weight * jnp.sum(dxv * (y * g_ref[...]), axis=0, keepdims=True)
        st_ref[12 + s:13 + s, :] += jnp.sum(dn * y, axis=0, keepdims=True)

    row = pl.BlockSpec((tr, D), lambda i: (i, 0))
    vec = pl.BlockSpec((1, D), lambda i: (0, 0))
    st = pl.BlockSpec((STAT_ROWS, D), lambda i: (0, 0))
    return _call(body, name="post_bwd", grid=(L // tr,), in_specs=[row, row, vec, vec, st],
                 out_specs=[row, st], out_shape=[_sds((L, D), BF16), _sds((STAT_ROWS, D), F32)],
                 input_output_aliases={4: 1},
                 compiler_params=_params(("arbitrary",)))(dx, f, g, gate, stats)


def _loss_head(x, target):
    L = x.shape[0]
    tr = _row_tile(L)

    def body(x_ref, t_ref, dx_ref, loss_ref):
        i = pl.program_id(0)
        err = x_ref[...] - t_ref[...]
        dx_ref[...] = err * (1.0 / D)

        @pl.when(i == 0)
        def _():
            loss_ref[...] = jnp.zeros_like(loss_ref)

        loss_ref[...] += 0.5 * jnp.sum(jnp.mean(err * err, axis=-1, keepdims=True), axis=0, keepdims=True)

    row = pl.BlockSpec((tr, D), lambda i: (i, 0))
    return _call(body, name="loss_head", grid=(L // tr,), in_specs=[row, row],
                 out_specs=[row, pl.BlockSpec((1, 1), lambda i: (0, 0))],
                 out_shape=[_sds((L, D), F32), _sds((1, 1), F32)],
                 compiler_params=_params(("arbitrary",)))(x, target)


def _ffn_in(h, w_in):
    L = h.shape[0]
    tm, tn = min(L, 1024), 256
    nf = FF // tn

    def body(h_ref, wa_ref, wb_ref, a_ref, b_ref, act_ref):
        hv = h_ref[...]
        a = _dot(hv, wa_ref[...])
        b = _dot(hv, wb_ref[...])
        a_ref[...] = a
        b_ref[...] = b
        act_ref[...] = (a * _sigmoid(a) * b).astype(BF16)

    tile = pl.BlockSpec((tm, tn), lambda i, j: (i, j))
    return _call(body, name="ffn_in", grid=(L // tm, nf),
                 in_specs=[pl.BlockSpec((tm, D), lambda i, j: (i, 0)),
                           pl.BlockSpec((D, tn), lambda i, j: (0, j)),
                           pl.BlockSpec((D, tn), lambda i, j: (0, j + nf))],
                 out_specs=[tile, tile, tile],
                 out_shape=[_sds((L, FF), F32), _sds((L, FF), F32), _sds((L, FF), BF16)],
                 compiler_params=_params(("parallel", "parallel")))(h, w_in, w_in)


def _ffn_mid_bwd(df, w_out, a, b):
    L = df.shape[0]
    tm, tn = min(L, 1024), 256

    def body(df_ref, w_ref, a_ref, b_ref, da_ref, db_ref):
        dact = _dot(df_ref[...], w_ref[...], 1, 1)
        av = a_ref[...]
        sg = _sigmoid(av)
        da_ref[...] = (dact * b_ref[...] * (sg * (1.0 + av * (1.0 - sg)))).astype(BF16)
        db_ref[...] = (dact * (av * sg)).astype(BF16)

    tile = pl.BlockSpec((tm, tn), lambda i, j: (i, j))
    return _call(body, name="ffn_mid_bwd", grid=(L // tm, FF // tn),
                 in_specs=[pl.BlockSpec((tm, D), lambda i, j: (i, 0)),
                           pl.BlockSpec((tn, D), lambda i, j: (j, 0)), tile, tile],
                 out_specs=[tile, tile],
                 out_shape=[_sds((L, FF), BF16), _sds((L, FF), BF16)],
                 compiler_params=_params(("parallel", "parallel")))(df, w_out, a, b)


def _rows_before(ref, i, tr, halo):
    start = pl.multiple_of(jnp.maximum(i * tr - halo, 0), 8)
    return jnp.where(i > 0, ref[pl.ds(start, halo), :], 0.0)


def _rows_after(ref, i, n, tr, halo):
    start = pl.multiple_of(jnp.minimum((i + 1) * tr, (n - 1) * tr), 8)
    return jnp.where(i < n - 1, ref[pl.ds(start, halo), :], 0.0)


def _conv_fwd(p, conv_w):
    L = p.shape[0]
    tr = _row_tile(L)
    n = L // tr

    def body(bg_ref, cg_ref, v_ref, w_ref, za_ref, u_scr):
        i = pl.program_id(0)

        @pl.when(i == 0)
        def _():
            u_scr[...] = cg_ref[...] * v_ref[...]

        r0 = pl.multiple_of(i * tr, 8)
        ext = jnp.concatenate([_rows_before(u_scr, i, tr, 8), u_scr[pl.ds(r0, tr), :]], axis=0)
        w = w_ref[...]
        y = (w[0:1] * pltpu.roll(ext, 2, axis=0) + w[1:2] * pltpu.roll(ext, 1, axis=0) + w[2:3] * ext)[8:, :]
        za_ref[...] = (bg_ref[pl.ds(r0, tr), :] * y).astype(BF16)

    col = lambda c: pl.BlockSpec((L, W), lambda i: (0, c))
    return _call(body, name="conv_fwd", grid=(n,),
                 in_specs=[col(0), col(1), col(2), pl.BlockSpec((3, W), lambda i: (0, 0))],
                 out_specs=pl.BlockSpec((tr, W), lambda i: (i, 0)),
                 out_shape=_sds((L, W), BF16),
                 scratch_shapes=[pltpu.VMEM((L, W), F32)],
                 compiler_params=_params(("arbitrary",)))(p, p, p, conv_w)


def _conv_bwd(p, conv_w, dza):
    L = p.shape[0]
    tr = _row_tile(L)
    n = L // tr

    def body(bg_ref, cg_ref, v_ref, w_ref, dza_ref, dp_ref, dw_ref, u_scr, dy_scr):
        i = pl.program_id(0)

        @pl.when(i == 0)
        def _():
            u_scr[...] = cg_ref[...] * v_ref[...]
            dy_scr[...] = dza_ref[...] * bg_ref[...]
            dw_ref[...] = jnp.zeros_like(dw_ref)

        r0 = pl.multiple_of(i * tr, 8)
        w = w_ref[...]
        ext = jnp.concatenate([_rows_before(u_scr, i, tr, 8), u_scr[pl.ds(r0, tr), :]], axis=0)
        u2 = pltpu.roll(ext, 2, axis=0)[8:, :]
        u1 = pltpu.roll(ext, 1, axis=0)[8:, :]
        u0 = ext[8:, :]
        y = w[0:1] * u2 + w[1:2] * u1 + w[2:3] * u0
        dy = dy_scr[pl.ds(r0, tr), :]
        dext = jnp.concatenate([dy, _rows_after(dy_scr, i, n, tr, 8)], axis=0)
        m = tr + 8
        du = (w[2:3] * dext + w[1:2] * pltpu.roll(dext, m - 1, axis=0) + w[0:1] * pltpu.roll(dext, m - 2, axis=0))[:tr, :]
        dp_ref[:, 0:W] = (dza_ref[pl.ds(r0, tr), :] * y).astype(BF16)
        dp_ref[:, W:2 * W] = (du * v_ref[pl.ds(r0, tr), :]).astype(BF16)
        dp_ref[:, 2 * W:3 * W] = (du * cg_ref[pl.ds(r0, tr), :]).astype(BF16)
        dw_ref[...] += jnp.concatenate([jnp.sum(dy * u2, axis=0, keepdims=True),
                                        jnp.sum(dy * u1, axis=0, keepdims=True),
                                        jnp.sum(dy * u0, axis=0, keepdims=True)], axis=0)

    col = lambda c: pl.BlockSpec((L, W), lambda i: (0, c))
    return _call(body, name="conv_bwd", grid=(n,),
                 in_specs=[col(0), col(1), col(2), pl.BlockSpec((3, W), lambda i: (0, 0)),
                           pl.BlockSpec((L, W), lambda i: (0, 0))],
                 out_specs=[pl.BlockSpec((tr, 3 * W), lambda i: (i, 0)), pl.BlockSpec((3, W), lambda i: (0, 0))],
                 out_shape=[_sds((L, 3 * W), BF16), _sds((3, W), F32)],
                 scratch_shapes=[pltpu.VMEM((L, W), F32), pltpu.VMEM((L, W), F32)],
                 compiler_params=_params(("arbitrary",)))(p, p, p, conv_w, dza)


def _pool_windows(lane):
    wins = jnp.zeros(lane.shape, jnp.int32)
    for gi, w in enumerate(POOL_WINDOWS):
        wins = jnp.where(lane // (W // len(POOL_WINDOWS)) == gi, w, wins)
    return wins


def _pooled_block(u_ref, i, tr):
    r0 = pl.multiple_of(i * tr, 8)
    cur = u_ref[pl.ds(r0, tr), :]
    ext = jnp.concatenate([_rows_before(u_ref, i, tr, 16), cur], axis=0)
    s2 = ext + pltpu.roll(ext, 1, axis=0)
    s4 = s2 + pltpu.roll(s2, 2, axis=0)
    s8 = s4 + pltpu.roll(s4, 4, axis=0)
    s16 = s8 + pltpu.roll(s8, 8, axis=0)
    lane = lax.broadcasted_iota(jnp.int32, (tr, W), 1)
    wins = _pool_windows(lane)
    win_sum = jnp.where(wins == 2, s2[16:], jnp.where(wins == 4, s4[16:], jnp.where(wins == 8, s8[16:], s16[16:])))
    t = lax.broadcasted_iota(jnp.int32, (tr, W), 0) + i * tr
    cnt = jnp.minimum(t + 1, wins).astype(F32)
    return win_sum / cnt - cur, cnt


def _pool_fwd(p, w_pool_bd, pool_scale):
    L = p.shape[0]
    tr = _row_tile(L)

    def body(u_ref, w_ref, sc_ref, zc_ref):
        pooled, _ = _pooled_block(u_ref, pl.program_id(0), tr)
        zc_ref[...] = (_bdot(pooled, w_ref[...]) * sc_ref[...]).astype(BF16)

    return _call(body, name="pool_fwd", grid=(L // tr,),
                 in_specs=[pl.BlockSpec((L, W), lambda i: (0, 4)), pl.BlockSpec((W, W), lambda i: (0, 0)),
                           pl.BlockSpec((1, W), lambda i: (0, 0))],
                 out_specs=pl.BlockSpec((tr, W), lambda i: (i, 0)), out_shape=_sds((L, W), BF16),
                 compiler_params=_params(("arbitrary",)))(p, w_pool_bd, pool_scale)


def _pool_bwd(p, w_pool_bd, pool_scale, dzc):
    L = p.shape[0]
    tr = _row_tile(L)
    n = L // tr

    def body(u_ref, w_ref, sc_ref, dzc_ref, du_ref, dw_ref, dsc_ref, g_scr):
        i = pl.program_id(0)

        @pl.when(i == 0)
        def _():
            dw_ref[...] = jnp.zeros_like(dw_ref)
            dsc_ref[...] = jnp.zeros_like(dsc_ref)

            def rows(k, carry):
                r = pl.multiple_of(k * tr, 8)
                dmix = (dzc_ref[pl.ds(r, tr), :] * sc_ref[...]).astype(BF16)
                dpool = _dot(dmix, w_ref[...].astype(BF16), 1, 1)
                lane = lax.broadcasted_iota(jnp.int32, (tr, W), 1)
                t = lax.broadcasted_iota(jnp.int32, (tr, W), 0) + k * tr
                cnt = jnp.minimum(t + 1, _pool_windows(lane)).astype(F32)
                g_scr[pl.ds(r, tr), :] = dpool / cnt
                return carry

            lax.fori_loop(0, n, rows, 0)

        r0 = pl.multiple_of(i * tr, 8)
        pooled, cnt = _pooled_block(u_ref, i, tr)
        dzc = dzc_ref[pl.ds(r0, tr), :]
        mixed = _bdot(pooled, w_ref[...])
        dsc_ref[...] += jnp.sum(dzc * mixed, axis=0, keepdims=True)
        dmix = (dzc * sc_ref[...]).astype(BF16)
        dw_ref[...] += _dot(pooled.astype(BF16), dmix, 0, 0)
        gcur = g_scr[pl.ds(r0, tr), :]
        ext = jnp.concatenate([gcur, _rows_after(g_scr, i, n, tr, 16)], axis=0)
        m = tr + 16
        s2 = ext + pltpu.roll(ext, m - 1, axis=0)
        s4 = s2 + pltpu.roll(s2, m - 2, axis=0)
        s8 = s4 + pltpu.roll(s4, m - 4, axis=0)
        s16 = s8 + pltpu.roll(s8, m - 8, axis=0)
        lane = lax.broadcasted_iota(jnp.int32, (tr, W), 1)
        wins = _pool_windows(lane)
        ahead = jnp.where(wins == 2, s2[:tr], jnp.where(wins == 4, s4[:tr], jnp.where(wins == 8, s8[:tr], s16[:tr])))
        du_ref[...] = (ahead - gcur * cnt).astype(BF16)

    return _call(body, name="pool_bwd", grid=(n,),
                 in_specs=[pl.BlockSpec((L, W), lambda i: (0, 4)), pl.BlockSpec((W, W), lambda i: (0, 0)),
                           pl.BlockSpec((1, W), lambda i: (0, 0)), pl.BlockSpec((L, W), lambda i: (0, 0))],
                 out_specs=[pl.BlockSpec((tr, W), lambda i: (i, 0)), pl.BlockSpec((W, W), lambda i: (0, 0)),
                            pl.BlockSpec((1, W), lambda i: (0, 0))],
                 out_shape=[_sds((L, W), BF16), _sds((W, W), F32), _sds((1, W), F32)],
                 scratch_shapes=[pltpu.VMEM((L, W), F32)],
                 compiler_params=_params(("arbitrary",)))(p, w_pool_bd, pool_scale, dzc)


SSM_SLAB = 512


def _ssm_prep(lam_re, lam_im, log_dt_x):
    def body(lr_ref, li_ref, ldt_ref, abr_ref, abi_ref, fr_ref, fi_ref):
        lr = jnp.minimum(lr_ref[...], LAMBDA_RE_MAX)
        li = li_ref[...]
        dt = jnp.exp(ldt_ref[...])
        mag = jnp.exp(lr * dt)
        abr = mag * jnp.cos(li * dt)
        abi = mag * jnp.sin(li * dt)
        den = lr * lr + li * li
        nr = abr - 1.0
        abr_ref[...] = abr
        abi_ref[...] = abi
        fr_ref[...] = (nr * lr + abi * li) / den
        fi_ref[...] = (abi * lr - nr * li) / den

    shp = _sds(lam_re.shape, F32)
    return _call(body, name="ssm_prep", out_shape=[shp, shp, shp, shp], compiler_params=_params())(lam_re, lam_im, log_dt_x)


def _ssm_prep_bwd(lam_re, lam_im, log_dt_x, g_abr, g_abi, g_fr, g_fi, group_sum):
    def body(lr_ref, li_ref, ldt_ref, gar_ref, gai_ref, gfr_ref, gfi_ref, gs_ref, dlr_ref, dli_ref, dldt_ref):
        lam = lr_ref[...]
        lr = jnp.minimum(lam, LAMBDA_RE_MAX)
        li = li_ref[...]
        dt = jnp.exp(ldt_ref[...])
        mag = jnp.exp(lr * dt)
        abr = mag * jnp.cos(li * dt)
        abi = mag * jnp.sin(li * dt)
        den = lr * lr + li * li
        nr = abr - 1.0
        fr = (nr * lr + abi * li) / den
        fi = (abi * lr - nr * li) / den
        d_nre = gfr_ref[...] / den
        d_nim = gfi_ref[...] / den
        d_den = -(gfr_ref[...] * fr + gfi_ref[...] * fi) / den
        d_abr = gar_ref[...] + d_nre * lr - d_nim * li
        d_abi = gai_ref[...] + d_nre * li + d_nim * lr
        d_lr = d_nre * nr + d_nim * abi + d_den * 2.0 * lr
        d_li = d_nre * abi - d_nim * nr + d_den * 2.0 * li
        d_mag = d_abr * jnp.cos(li * dt) + d_abi * jnp.sin(li * dt)
        d_th = -d_abr * abi + d_abi * abr
        d_lr = d_lr + d_mag * mag * dt
        d_li = d_li + d_th * dt
        d_dt = d_mag * mag * lr + d_th * li
        passes = jnp.where(lam < LAMBDA_RE_MAX, 1.0, jnp.where(lam == LAMBDA_RE_MAX, 0.5, 0.0))
        dlr_ref[...] = d_lr * passes
        dli_ref[...] = d_li
        dldt_ref[...] = _dot(d_dt * dt, gs_ref[...], precision=HIGH)

    shp = _sds(lam_re.shape, F32)
    return _call(body, name="ssm_prep_bwd", out_shape=[shp, shp, _sds((lam_re.shape[0], 128), F32)],
                 compiler_params=_params())(lam_re, lam_im, log_dt_x, g_abr, g_abi, g_fr, g_fi, group_sum)


def _cmul(ar, ai, br, bi):
    return ar * br - ai * bi, ar * bi + ai * br


def _powers(ar, ai):
    out = [(ar, ai)]
    for _ in range(7):
        out.append(_cmul(out[-1][0], out[-1][1], ar, ai))
    return out


def _scan_rows(s_re, s_im, ar, ai, L, reverse=False, visit=None, visit_init=None):
    n = s_re.shape[1]
    pw = _powers(ar, ai)
    row = lax.broadcasted_iota(jnp.int32, (8, n), 0)
    dist = (8 - row) if reverse else (row + 1)
    pr = jnp.zeros((8, n), F32)
    pi = jnp.zeros((8, n), F32)
    for k in range(8):
        pr = jnp.where(dist == k + 1, pw[k][0], pr)
        pi = jnp.where(dist == k + 1, pw[k][1], pi)
    nb = L // 8

    def blk(t, carry):
        cr, ci, acc = carry
        b = (nb - 1 - t) if reverse else t
        r0 = pl.multiple_of(b * 8, 8)
        xr = s_re[pl.ds(r0, 8), :]
        xi = s_im[pl.ds(r0, 8), :]
        for d in (1, 2, 4):
            if reverse:
                keep = row < 8 - d
                sr, si = pltpu.roll(xr, 8 - d, axis=0), pltpu.roll(xi, 8 - d, axis=0)
            else:
                keep = row >= d
                sr, si = pltpu.roll(xr, d, axis=0), pltpu.roll(xi, d, axis=0)
            sr = jnp.where(keep, sr, 0.0)
            si = jnp.where(keep, si, 0.0)
            mr, mi = _cmul(pw[d - 1][0], pw[d - 1][1], sr, si)
            xr, xi = xr + mr, xi + mi
        mr, mi = _cmul(pr, pi, cr, ci)
        xr, xi = xr + mr, xi + mi
        s_re[pl.ds(r0, 8), :] = xr
        s_im[pl.ds(r0, 8), :] = xi
        if visit is not None:
            acc = visit(b, xr, xi, acc)
        if reverse:
            return xr[0:1, :], xi[0:1, :], acc
        return xr[7:8, :], xi[7:8, :], acc

    zero = jnp.zeros((1, n), F32)
    return lax.fori_loop(0, nb, blk, (zero, zero, visit_init if visit is not None else 0))[2]


def _ssm_project(u_ref, wbr, wbi, s_re, s_im, L):
    ch = min(L, 256)

    def rows(k, carry):
        r = pl.multiple_of(k * ch, 8)
        ub = u_ref[pl.ds(r, ch), :].astype(BF16)
        s_re[pl.ds(r, ch), :] = _dot(ub, wbr)
        s_im[pl.ds(r, ch), :] = _dot(ub, wbi)
        return carry

    lax.fori_loop(0, L // ch, rows, 0)


def _gelu(y):
    c = math.sqrt(2.0 / math.pi)
    return 0.5 * y * (1.0 + jnp.tanh(c * (y + 0.044715 * y * y * y)))


def _gelu_grad(y):
    c = math.sqrt(2.0 / math.pi)
    th = jnp.tanh(c * (y + 0.044715 * y * y * y))
    return 0.5 * (1.0 + th) + 0.5 * y * (1.0 - th * th) * c * (1.0 + 3.0 * 0.044715 * y * y)


def _ssm_fwd(p, b_re_bd, b_im_bd, c_re_bd, c_im_bd, abr, abi, fr, fi, d_skip):
    L = p.shape[0]
    ns = NST // SSM_SLAB
    ch = min(L, 256)

    def body(u_ref, br_ref, bi_ref, cr_ref, ci_ref, abr_ref, abi_ref, fr_ref, fi_ref, d_ref,
             y_ref, zb_ref, s_re, s_im):
        j = pl.program_id(0)
        f_re, f_im = fr_ref[...], fi_ref[...]
        wbr = (f_re * br_ref[...] - f_im * bi_ref[...]).astype(BF16)
        wbi = (f_re * bi_ref[...] + f_im * br_ref[...]).astype(BF16)
        _ssm_project(u_ref, wbr, wbi, s_re, s_im, L)
        _scan_rows(s_re, s_im, abr_ref[...], abi_ref[...], L)
        crb = cr_ref[...].astype(BF16)
        cib = ci_ref[...].astype(BF16)

        def rows(k, carry):
            r = pl.multiple_of(k * ch, 8)
            part = _dot(s_re[pl.ds(r, ch), :].astype(BF16), crb) - _dot(s_im[pl.ds(r, ch), :].astype(BF16), cib)

            @pl.when(j == 0)
            def _():
                y_ref[pl.ds(r, ch), :] = part + d_ref[...] * u_ref[pl.ds(r, ch), :]

            @pl.when(j > 0)
            def _():
                y_ref[pl.ds(r, ch), :] += part

            @pl.when(j == ns - 1)
            def _():
                zb_ref[pl.ds(r, ch), :] = _gelu(y_ref[pl.ds(r, ch), :]).astype(BF16)

            return carry

        lax.fori_loop(0, L // ch, rows, 0)

    full = lambda shape: pl.BlockSpec(shape, lambda j: (0, 0))
    lanes = pl.BlockSpec((1, SSM_SLAB), lambda j: (0, j))
    return _call(body, name="ssm_fwd", grid=(ns,),
                 in_specs=[pl.BlockSpec((L, W), lambda j: (0, 3)),
                           pl.BlockSpec((W, SSM_SLAB), lambda j: (0, j)), pl.BlockSpec((W, SSM_SLAB), lambda j: (0, j)),
                           pl.BlockSpec((SSM_SLAB, W), lambda j: (j, 0)), pl.BlockSpec((SSM_SLAB, W), lambda j: (j, 0)),
                           lanes, lanes, lanes, lanes, full((1, W))],
                 out_specs=[full((L, W)), full((L, W))],
                 out_shape=[_sds((L, W), F32), _sds((L, W), BF16)],
                 scratch_shapes=[pltpu.VMEM((L, SSM_SLAB), F32), pltpu.VMEM((L, SSM_SLAB), F32)],
                 compiler_params=_params(("arbitrary",)))(p, b_re_bd, b_im_bd, c_re_bd, c_im_bd, abr, abi, fr, fi, d_skip)


def _ssm_bwd(p, y, dzb, b_re_bd, b_im_bd, c_re_bd, c_im_bd, abr, abi, fr, fi, d_skip):
    L = p.shape[0]
    ns = NST // SSM_SLAB
    ch = min(L, 256)
    n_ch = L // ch

    def body(u_ref, y_ref, dzb_ref, br_ref, bi_ref, cr_ref, ci_ref, abr_ref, abi_ref, fr_ref, fi_ref, d_ref,
             du_ref, dd_ref, dbr_ref, dbi_ref, dcr_ref, dci_ref, gar_ref, gai_ref, gfr_ref, gfi_ref,
             s_re, s_im, l_re, l_im, dy_scr, du_scr):
        j = pl.program_id(0)
        f_re, f_im = fr_ref[...], fi_ref[...]
        b_re, b_im = br_ref[...], bi_ref[...]
        wbr = (f_re * b_re - f_im * b_im).astype(BF16)
        wbi = (f_re * b_im + f_im * b_re).astype(BF16)
        a_re, a_im = abr_ref[...], abi_ref[...]

        @pl.when(j == 0)
        def _():
            def rows(k, acc):
                r = pl.multiple_of(k * ch, 8)
                dy = dzb_ref[pl.ds(r, ch), :] * _gelu_grad(y_ref[pl.ds(r, ch), :])
                dy_scr[pl.ds(r, ch), :] = dy
                du_scr[pl.ds(r, ch), :] = d_ref[...] * dy
                return acc + jnp.sum(dy * u_ref[pl.ds(r, ch), :], axis=0, keepdims=True)

            dd_ref[...] = lax.fori_loop(0, n_ch, rows, jnp.zeros((1, W), F32))

        _ssm_project(u_ref, wbr, wbi, s_re, s_im, L)
        _scan_rows(s_re, s_im, a_re, a_im, L)
        crb = cr_ref[...].astype(BF16)
        cib = ci_ref[...].astype(BF16)

        def rows_c(k, acc):
            dcr, dci = acc
            r = pl.multiple_of(k * ch, 8)
            dyb = dy_scr[pl.ds(r, ch), :].astype(BF16)
            dcr = dcr + _dot(s_re[pl.ds(r, ch), :].astype(BF16), dyb, 0, 0)
            dci = dci - _dot(s_im[pl.ds(r, ch), :].astype(BF16), dyb, 0, 0)
            l_re[pl.ds(r, ch), :] = _dot(dyb, crb, 1, 1)
            l_im[pl.ds(r, ch), :] = -_dot(dyb, cib, 1, 1)
            return dcr, dci

        zc = jnp.zeros((SSM_SLAB, W), F32)
        dcr, dci = lax.fori_loop(0, n_ch, rows_c, (zc, zc))
        dcr_ref[...] = dcr
        dci_ref[...] = dci

        row8 = lax.broadcasted_iota(jnp.int32, (8, SSM_SLAB), 0)

        def visit(b, lr, li, acc):
            ar_acc, ai_acc = acc
            r0 = pl.multiple_of(b * 8, 8)
            rp = pl.multiple_of(jnp.maximum(b * 8 - 8, 0), 8)
            has_prev = b > 0
            pr = jnp.where(has_prev, s_re[pl.ds(rp, 8), :][7:8, :], 0.0)
            pi = jnp.where(has_prev, s_im[pl.ds(rp, 8), :][7:8, :], 0.0)
            sr = jnp.where(row8 >= 1, pltpu.roll(s_re[pl.ds(r0, 8), :], 1, axis=0), pr)
            si = jnp.where(row8 >= 1, pltpu.roll(s_im[pl.ds(r0, 8), :], 1, axis=0), pi)
            return ar_acc + lr * sr + li * si, ai_acc - lr * si + li * sr

        z8 = jnp.zeros((8, SSM_SLAB), F32)
        ar_acc, ai_acc = _scan_rows(l_re, l_im, a_re, -a_im, L, reverse=True, visit=visit, visit_init=(z8, z8))
        gar_ref[...] = jnp.sum(ar_acc, axis=0, keepdims=True)
        gai_ref[...] = jnp.sum(ai_acc, axis=0, keepdims=True)

        def rows_b(k, acc):
            dwr, dwi = acc
            r = pl.multiple_of(k * ch, 8)
            ub = u_ref[pl.ds(r, ch), :].astype(BF16)
            lrb = l_re[pl.ds(r, ch), :].astype(BF16)
            lib = l_im[pl.ds(r, ch), :].astype(BF16)
            du_scr[pl.ds(r, ch), :] += _dot(lrb, wbr, 1, 1) + _dot(lib, wbi, 1, 1)
            return dwr + _dot(ub, lrb, 0, 0), dwi + _dot(ub, lib, 0, 0)

        zb = jnp.zeros((W, SSM_SLAB), F32)
        dwr, dwi = lax.fori_loop(0, n_ch, rows_b, (zb, zb))
        dbr_ref[...] = dwr * f_re + dwi * f_im
        dbi_ref[...] = -dwr * f_im + dwi * f_re
        gfr_ref[...] = jnp.sum(dwr * b_re + dwi * b_im, axis=0, keepdims=True)
        gfi_ref[...] = jnp.sum(-dwr * b_im + dwi * b_re, axis=0, keepdims=True)

        @pl.when(j == ns - 1)
        def _():
            du_ref[...] = du_scr[...].astype(BF16)

    full = lambda shape: pl.BlockSpec(shape, lambda j: (0, 0))
    lanes = pl.BlockSpec((1, SSM_SLAB), lambda j: (0, j))
    bspec = pl.BlockSpec((W, SSM_SLAB), lambda j: (0, j))
    cspec = pl.BlockSpec((SSM_SLAB, W), lambda j: (j, 0))
    slab = lambda: pltpu.VMEM((L, SSM_SLAB), F32)
    return _call(body, name="ssm_bwd", grid=(ns,),
                 in_specs=[pl.BlockSpec((L, W), lambda j: (0, 3)), full((L, W)), full((L, W)),
                           bspec, bspec, cspec, cspec, lanes, lanes, lanes, lanes, full((1, W))],
                 out_specs=[full((L, W)), full((1, W)), bspec, bspec, cspec, cspec, lanes, lanes, lanes, lanes],
                 out_shape=[_sds((L, W), BF16), _sds((1, W), F32), _sds((W, NST), F32), _sds((W, NST), F32),
                            _sds((NST, W), F32), _sds((NST, W), F32)] + [_sds((1, NST), F32)] * 4,
                 scratch_shapes=[slab(), slab(), slab(), slab(), pltpu.VMEM((L, W), F32), pltpu.VMEM((L, W), F32)],
                 compiler_params=_params(("arbitrary",)))(p, y, dzb, b_re_bd, b_im_bd, c_re_bd, c_im_bd,
                                                          abr, abi, fr, fi, d_skip)


SB_KB = 512


SB_SUB = 256


def _split2(x):
    hi = x.astype(BF16)
    return hi, (x - hi.astype(F32)).astype(BF16)


def _ones_dot(x, ones):
    n = x.shape[0]
    r = _dot(jnp.concatenate(_split2(x), axis=0), ones)
    return r[:n] + r[n:]


def _suffix_sums(x, tri):
    sub = tri.shape[0]
    parts = [_ones_dot(x[:, i:i + sub], tri) for i in range(0, x.shape[1], sub)]
    out, after = [], None
    for p in reversed(parts):
        out.append(p if after is None else p + after)
        after = p[:, 0:1] if after is None else after + p[:, 0:1]
    return jnp.concatenate(out[::-1], axis=1)


def _prefix_sums_exclusive(x, tri_le):
    sub = tri_le.shape[0]
    out, before = [], None
    for i in range(0, x.shape[1], sub):
        xi = x[:, i:i + sub]
        inc = _ones_dot(xi, tri_le)
        out.append(inc - xi if before is None else inc - xi + before)
        before = inc[:, sub - 1:sub] if before is None else before + inc[:, sub - 1:sub]
    return jnp.concatenate(out, axis=1)


def _sb_block(q, kj, i, jb, kb, right, tri):
    z = _bdot(q, kj, 1, 1)
    t_idx = lax.broadcasted_iota(jnp.int32, (QB, kb), 0) + i * QB
    s_idx = lax.broadcasted_iota(jnp.int32, (QB, kb), 1) + jb * kb
    mask = s_idx < t_idx
    lk_all = jnp.minimum(-z, 0.0) - jnp.log1p(jnp.exp(-jnp.abs(z)))
    lk = jnp.where(mask, lk_all, 0.0)
    suf = _suffix_sums(lk, tri)
    a = jnp.where(mask, jnp.exp((lk_all + z) + (suf - lk) + right), 0.0)
    return z, mask, suf, a


def _sb_ones(kb):
    sub = min(SB_SUB, kb)
    r = lax.broadcasted_iota(jnp.int32, (sub, sub), 0)
    c = lax.broadcasted_iota(jnp.int32, (sub, sub), 1)
    return (r >= c).astype(BF16), (r <= c).astype(BF16)


def _sb_fwd(q, k, v):
    L = q.shape[1]
    kb = min(SB_KB, L)
    per = kb // QB

    def body(q_ref, k_ref, v_ref, o_ref, rs_ref):
        i = pl.program_id(0)
        tri, _ = _sb_ones(kb)
        lane = lax.broadcasted_iota(jnp.int32, (QB, 128), 1)
        qs = [q_ref[h] for h in range(HEADS)]

        def step(t, carry):
            accs, rights, sums = carry
            jb = i // per - t
            r = pl.multiple_of(jb * kb, kb)
            out = []
            for h in range(HEADS):
                _, _, suf, a = _sb_block(qs[h], k_ref[h, pl.ds(r, kb), :], i, jb, kb, rights[h], tri)
                tot = suf[:, 0:1]
                out.append((accs[h] + _bdot(a, v_ref[h, pl.ds(r, kb), :]), rights[h] + tot,
                            sums[h] + jnp.where(lane == jb, tot, 0.0)))
            return tuple(o[0] for o in out), tuple(o[1] for o in out), tuple(o[2] for o in out)

        init = (tuple(jnp.zeros((QB, HD), F32) for _ in range(HEADS)), tuple(jnp.zeros((QB, 1), F32) for _ in range(HEADS)),
                tuple(jnp.zeros((QB, 128), F32) for _ in range(HEADS)))
        accs, _, sums = lax.fori_loop(0, i // per + 1, step, init)
        for h in range(HEADS):
            o_ref[h] = accs[h]
            rs_ref[h] = sums[h]

    heads = pl.BlockSpec((HEADS, L, HD), lambda i: (0, 0, 0))
    blk = pl.BlockSpec((HEADS, QB, HD), lambda i: (0, i, 0))
    return _call(body, name="sb_fwd", grid=(L // QB,), in_specs=[blk, heads, heads],
                 out_specs=[blk, pl.BlockSpec((HEADS, QB, 128), lambda i: (0, i, 0))],
                 out_shape=[_sds((HEADS, L, HD), F32), _sds((HEADS, L, 128), F32)],
                 compiler_params=_params(("parallel",)))(q, k, v)


def _sb_bwd(q, k, v, do, block_sums):
    L = q.shape[1]
    kb = min(SB_KB, L)
    per = kb // QB

    def body(q_ref, k_ref, v_ref, do_ref, rs_ref, dq_ref, dk_ref, dv_ref):
        i = pl.program_id(0)
        tri, tri_le = _sb_ones(kb)
        lane = lax.broadcasted_iota(jnp.int32, (QB, 128), 1)

        @pl.when(i == 0)
        def _():
            dk_ref[...] = jnp.zeros_like(dk_ref)
            dv_ref[...] = jnp.zeros_like(dv_ref)

        qs = [q_ref[h] for h in range(HEADS)]
        dos = [do_ref[h] for h in range(HEADS)]
        sums = [rs_ref[h] for h in range(HEADS)]

        def step(jb, carry):
            dqs, lefts = carry
            r = pl.multiple_of(jb * kb, kb)
            out = []
            for h in range(HEADS):
                kj = k_ref[h, pl.ds(r, kb), :]
                vj = v_ref[h, pl.ds(r, kb), :]
                right = jnp.sum(jnp.where(lane > jb, sums[h], 0.0), axis=1, keepdims=True)
                z, mask, _, a = _sb_block(qs[h], kj, i, jb, kb, right, tri)
                e = a * _bdot(dos[h], vj, 1, 1)
                dv_ref[h, pl.ds(r, kb), :] += _dot(a.astype(BF16), dos[h].astype(BF16), 0, 0)
                before = lefts[h] + _prefix_sums_exclusive(e, tri_le)
                sg = _sigmoid(z)
                dz = jnp.where(mask, e * (1.0 - sg) - sg * before, 0.0).astype(BF16)
                dk_ref[h, pl.ds(r, kb), :] += _dot(dz, qs[h].astype(BF16), 0, 0)
                out.append((dqs[h] + _dot(dz, kj.astype(BF16)), lefts[h] + jnp.sum(e, axis=1, keepdims=True)))
            return tuple(o[0] for o in out), tuple(o[1] for o in out)

        init = (tuple(jnp.zeros((QB, HD), F32) for _ in range(HEADS)), tuple(jnp.zeros((QB, 1), F32) for _ in range(HEADS)))
        dqs, _ = lax.fori_loop(0, i // per + 1, step, init)
        for h in range(HEADS):
            dq_ref[h] = dqs[h]

    heads = pl.BlockSpec((HEADS, L, HD), lambda i: (0, 0, 0))
    blk = pl.BlockSpec((HEADS, QB, HD), lambda i: (0, i, 0))
    shp = _sds((HEADS, L, HD), F32)
    return _call(body, name="sb_bwd", grid=(L // QB,),
                 in_specs=[blk, heads, heads, blk, pl.BlockSpec((HEADS, QB, 128), lambda i: (0, i, 0))],
                 out_specs=[blk, heads, heads], out_shape=[shp, shp, shp],
                 compiler_params=_params(("arbitrary",)))(q, k, v, do, block_sums)


def _merge_fwd(za, zb, zc, zd, p, w_conv_out, w_glu, w_pool_out, w_sb_out):
    L = za.shape[0]
    tm = _row_tile(L)

    def body(za_ref, zb_ref, zc_ref, zd_ref, g0, g1, g2, g3, wc_ref, wg_ref, wp_ref, ws_ref, o_ref):
        glu = _dot(zb_ref[...], wg_ref[...])
        yb = glu[:, :D] * _sigmoid(glu[:, D:])
        m = _sigmoid(g0[...]) * _dot(za_ref[...], wc_ref[...])
        m = m + _sigmoid(g1[...]) * yb
        m = m + _sigmoid(g2[...]) * _dot(zc_ref[...], wp_ref[...])
        m = m + _sigmoid(g3[...]) * _dot(zd_ref[...], ws_ref[...])
        o_ref[...] = m.astype(BF16)

    zt = pl.BlockSpec((tm, W), lambda i: (i, 0))
    gate = lambda b: pl.BlockSpec((tm, D), lambda i: (i, 2 + b))
    wfull = lambda n: pl.BlockSpec((W, n), lambda i: (0, 0))
    return _call(body, name="merge_fwd", grid=(L // tm,),
                 in_specs=[zt, zt, zt, zt, gate(0), gate(1), gate(2), gate(3), wfull(D), wfull(2 * D), wfull(D), wfull(D)],
                 out_specs=pl.BlockSpec((tm, D), lambda i: (i, 0)), out_shape=_sds((L, D), BF16),
                 compiler_params=_params(("parallel",)))(za, zb, zc, zd, p, p, p, p, w_conv_out, w_glu, w_pool_out, w_sb_out)


def _merge_bwd(dm, za, zb, zc, zd, p, w_conv_out, w_glu, w_pool_out, w_sb_out):
    L = za.shape[0]
    tm = _row_tile(L)
    n = L // tm

    def body(dm_ref, za_ref, zb_ref, zc_ref, zd_ref, g0, g1, g2, g3, wc_ref, wg_ref, wp_ref, ws_ref,
             dza_ref, dzb_ref, dzc_ref, dzd_ref, dg_ref, dwc_ref, dwg_ref, dwp_ref, dws_ref,
             awc, awg, awp, aws):
        i = pl.program_id(0)

        @pl.when(i == 0)
        def _():
            awc[...] = jnp.zeros_like(awc)
            awg[...] = jnp.zeros_like(awg)
            awp[...] = jnp.zeros_like(awp)
            aws[...] = jnp.zeros_like(aws)

        dmv = dm_ref[...]

        def gated(g_ref, y, col):
            s = _sigmoid(g_ref[...])
            dg_ref[:, col * D:(col + 1) * D] = (dmv * y * s * (1.0 - s)).astype(BF16)
            return (dmv * s)

        def linear(z_ref, w_ref, acc, dz_ref, col, g_ref):
            zv = z_ref[...]
            dy = gated(g_ref, _dot(zv, w_ref[...]), col).astype(BF16)
            dz_ref[...] = _dot(dy, w_ref[...], 1, 1)
            acc[...] += _dot(zv, dy, 0, 0)

        linear(za_ref, wc_ref, awc, dza_ref, 0, g0)
        linear(zc_ref, wp_ref, awp, dzc_ref, 2, g2)
        linear(zd_ref, ws_ref, aws, dzd_ref, 3, g3)
        zbv = zb_ref[...]
        glu = _dot(zbv, wg_ref[...])
        ga = glu[:, :D]
        sg = _sigmoid(glu[:, D:])
        dyb = gated(g1, ga * sg, 1)
        dga = (dyb * sg).astype(BF16)
        dgg = (dyb * ga * sg * (1.0 - sg)).astype(BF16)
        dzb_ref[...] = _dot(dga, wg_ref[:, :D], 1, 1) + _dot(dgg, wg_ref[:, D:], 1, 1)
        awg[:, :D] += _dot(zbv, dga, 0, 0)
        awg[:, D:] += _dot(zbv, dgg, 0, 0)

        @pl.when(i == n - 1)
        def _():
            dwc_ref[...] = awc[...].astype(BF16)
            dwg_ref[...] = awg[...].astype(BF16)
            dwp_ref[...] = awp[...].astype(BF16)
            dws_ref[...] = aws[...].astype(BF16)

    zt = pl.BlockSpec((tm, W), lambda i: (i, 0))
    gate = lambda b: pl.BlockSpec((tm, D), lambda i: (i, 2 + b))
    wfull = lambda n_: pl.BlockSpec((W, n_), lambda i: (0, 0))
    zs = _sds((L, W), F32)
    return _call(body, name="merge_bwd", grid=(n,),
                 in_specs=[pl.BlockSpec((tm, D), lambda i: (i, 0)), zt, zt, zt, zt, gate(0), gate(1), gate(2), gate(3),
                           wfull(D), wfull(2 * D), wfull(D), wfull(D)],
                 out_specs=[zt, zt, zt, zt, pl.BlockSpec((tm, 4 * D), lambda i: (i, 0)),
                            wfull(D), wfull(2 * D), wfull(D), wfull(D)],
                 out_shape=[zs, zs, zs, zs, _sds((L, 4 * D), BF16),
                            _sds((W, D), BF16), _sds((W, 2 * D), BF16), _sds((W, D), BF16), _sds((W, D), BF16)],
                 scratch_shapes=[pltpu.VMEM((W, D), F32), pltpu.VMEM((W, 2 * D), F32), pltpu.VMEM((W, D), F32),
                                 pltpu.VMEM((W, D), F32)],
                 compiler_params=_params(("arbitrary",)))(dm, za, zb, zc, zd, p, p, p, p,
                                                          w_conv_out, w_glu, w_pool_out, w_sb_out)


def _adam_math(w, g, m, v):
    m2 = ADAM_B1 * m + (1.0 - ADAM_B1) * g
    v2 = ADAM_B2 * v + (1.0 - ADAM_B2) * (g * g)
    m_hat = m2 / (1.0 - ADAM_B1 ** ADAM_STEP)
    v_hat = v2 / (1.0 - ADAM_B2 ** ADAM_STEP)
    return -ADAM_LR * (m_hat / (jnp.sqrt(v_hat) + ADAM_EPS) + ADAM_WD * w), m2, v2


def _as_rows(a):
    return a.reshape(-1, a.shape[-1])


def _adamw(w, g, m, v):
    shape = w.shape
    w2, g2, m2, v2 = _as_rows(w), _as_rows(g), _as_rows(m), _as_rows(v)
    R, C = w2.shape
    tr = R
    for cand in (1024, 512, 256, 128, 64, 32, 16, 8):
        if R % cand == 0 and cand * C * 4 <= 2 * 1024 * 1024:
            tr = cand
            break

    def body(w_ref, g_ref, m_ref, v_ref, d_ref, m_out, v_out):
        d, mn, vn = _adam_math(w_ref[...], g_ref[...], m_ref[...], v_ref[...])
        d_ref[...] = d
        m_out[...] = mn
        v_out[...] = vn

    blk = pl.BlockSpec((tr, C), lambda i: (i, 0))
    shp = _sds((R, C), F32)
    outs = _call(body, name="adamw", grid=(R // tr,), in_specs=[blk] * 4, out_specs=[blk] * 3, out_shape=[shp] * 3,
                 compiler_params=_params(("parallel",)))(w2, g2, m2, v2)
    return tuple(o.reshape(shape) for o in outs)


def _sum_parts(parts, out_dtype, name):
    shape = parts[0].shape
    flat = [_as_rows(a) for a in parts]
    R, C = flat[0].shape
    tr = R
    for cand in (1024, 512, 256, 128, 64, 32, 16):
        if R % cand == 0 and cand * C * 4 <= 2 * 1024 * 1024:
            tr = cand
            break
    k = len(parts)

    def body(*refs):
        acc = refs[0][...].astype(F32)
        for r in refs[1:k]:
            acc = acc + r[...].astype(F32)
        refs[k][...] = acc.astype(out_dtype)

    blk = pl.BlockSpec((tr, C), lambda i: (i, 0))
    out = _call(body, name=name, grid=(R // tr,), in_specs=[blk] * k, out_specs=blk, out_shape=_sds((R, C), out_dtype),
                compiler_params=_params(("parallel",)))(*flat)
    return out.reshape(shape)


ADA_SHARD = 9 * D // N_CHIP
ADA_TN = 768


def _ada_fwd(c_pad, w_ada, b_ada_cols):
    depth = w_ada.shape[0]

    def body(c_ref, w_ref, b_ref, o_ref):
        cv = c_ref[...]
        o_ref[...] = _bdot(cv * _sigmoid(cv), w_ref[...]) + b_ref[...]

    return _call(body, name="ada_fwd", grid=(depth, ADA_SHARD // ADA_TN),
                 in_specs=[pl.BlockSpec((16, D), lambda l, j: (0, 0)),
                           pl.BlockSpec((None, D, ADA_TN), lambda l, j: (l, 0, j)),
                           pl.BlockSpec((None, 1, ADA_TN), lambda l, j: (l, 0, j))],
                 out_specs=pl.BlockSpec((None, 16, ADA_TN), lambda l, j: (l, 0, j)),
                 out_shape=_sds((depth, 16, ADA_SHARD), F32),
                 compiler_params=_params(("parallel", "parallel")))(c_pad, w_ada, b_ada_cols)


def _ada_wgrad(c_pad, d_ada):
    depth = d_ada.shape[0]

    def body(c_ref, d_ref, o_ref):
        cv = c_ref[...]
        o_ref[...] = _bdot(cv * _sigmoid(cv), d_ref[...], 0, 0)

    return _call(body, name="ada_wgrad", grid=(depth, ADA_SHARD // ADA_TN),
                 in_specs=[pl.BlockSpec((16, D), lambda l, j: (0, 0)),
                           pl.BlockSpec((None, 16, ADA_TN), lambda l, j: (l, 0, j))],
                 out_specs=pl.BlockSpec((None, D, ADA_TN), lambda l, j: (l, 0, j)),
                 out_shape=_sds((depth, D, ADA_SHARD), F32),
                 compiler_params=_params(("parallel", "parallel")))(c_pad, d_ada)


HBM_SPEC = pl.BlockSpec(memory_space=pltpu.HBM)


def _place():
    x, y, c = lax.axis_index("x"), lax.axis_index("y"), lax.axis_index("c")
    peers = [(1 - x, y), (x, 1 - y), (1 - x, 1 - y)]
    return x, y, c, peers


def _chip(px, py):
    return 2 * px + py


def _allgather8(block, name):
    m_per, n = block.shape

    def body(x_ref, out_ref, send_sems, recv_sems, local_sem):
        x, y, c, chips = _place()
        me, sibling = (x, y, c), (x, y, 1 - c)

        def rows(px, py, pc):
            return out_ref.at[pl.ds(pl.multiple_of((4 * px + 2 * py + pc) * m_per, 8), m_per), :]

        def copy(k, blk, to, src=None):
            return pltpu.make_async_remote_copy(
                src_ref=rows(*blk) if src is None else src, dst_ref=rows(*blk),
                send_sem=send_sems.at[k], recv_sem=recv_sems.at[k], device_id=to, device_id_type=MESH)

        mine = pltpu.make_async_copy(x_ref, rows(*me), local_sem)
        mine.start()
        first = [copy(0, me, sibling, src=x_ref)]
        first += [copy(1 + j, me, (*chip, c), src=x_ref) for j, chip in enumerate(chips)]
        for cp in first:
            cp.start()
        passed = [copy(4 + j, (*chip, c), sibling) for j, chip in enumerate(chips)]
        for j, chip in enumerate(chips):
            copy(1 + j, (*chip, c), me).wait_recv()
            passed[j].start()
        copy(0, sibling, me).wait_recv()
        for j, chip in enumerate(chips):
            copy(4 + j, (*chip, 1 - c), me).wait_recv()
        for cp in first + passed:
            cp.wait_send()
        mine.wait()

    return _call(body, name=name, out_shape=_sds((N_DEV * m_per, n), block.dtype),
                 in_specs=[pl.BlockSpec(memory_space=pltpu.VMEM)], out_specs=pl.BlockSpec(memory_space=pltpu.VMEM),
                 scratch_shapes=[pltpu.SemaphoreType.DMA((7,)), pltpu.SemaphoreType.DMA((7,)), pltpu.SemaphoreType.DMA],
                 compiler_params=_params())(block)


GATHERED = (("w_ff_in", 0, -1, 0), ("w_ff_out", 0, -2, 0),
            ("w_in", None, -1, 1), ("w_conv_out", None, -1, 1), ("w_glu", None, -1, 1), ("w_pool_out", None, -1, 1),
            ("w_sb_out", None, -1, 1), ("w_out", None, -2, 1),
            ("w_ff_in", 1, -1, 2), ("w_ff_out", 1, -2, 2))
N_SUB = 3


def _lead(ref):
    return (slice(None),) * (len(ref.shape) - 2)


def _mo(v, m):
    return v if isinstance(v, int) else pl.multiple_of(v, m)


def _full_region(ref, axis, j, half, shard_shape):
    rs, cs = shard_shape[-2], shard_shape[-1]
    if axis == -1:
        r0, nr = (0, rs) if half is None else (half * (rs // 2), rs // 2)
        return ref.at[_lead(ref) + (pl.ds(_mo(r0, 16), nr), pl.ds(_mo(j * cs, 128), cs))]
    r0, nr = (j * rs, rs) if half is None else (j * rs + half * (rs // 2), rs // 2)
    return ref.at[_lead(ref) + (pl.ds(_mo(r0, 16), nr), slice(None))]


def _shard_half(ref, half):
    rs = ref.shape[-2]
    return ref.at[_lead(ref) + (pl.ds(_mo(half * (rs // 2), 16), rs // 2), slice(None))]


def _full_shape(shard_shape, axis):
    s = list(shard_shape)
    s[axis] *= N_CHIP
    return tuple(s)


class _Lay:
    def __init__(self, shard_shape, axis):
        self.axis = axis
        self.shard_shape = tuple(shard_shape)
        self.full_shape = _full_shape(shard_shape, axis)
        self.lead = int(np.prod(shard_shape[:-2]))
        self.rs, self.cs = shard_shape[-2], shard_shape[-1]
        self.hr = self.rs // 2
        self.tr = next(t for t in (512, 256, 128, 64, 32, 16) if self.hr % t == 0 and t * self.cs * 4 <= (1 << 21))
        self.half_rows_shape = _half_rows_shape(self.full_shape)
        self.half_shard_shape = _half_rows_shape(self.shard_shape)

    def full(self, jf, hf):
        if self.axis == -1:
            return ((self.lead, 2, self.hr, N_CHIP * self.cs),
                    pl.BlockSpec((None, None, self.tr, self.cs), lambda b, j, i, s: (b, hf(j, s), i, jf(j, s))))
        return ((self.lead, N_CHIP, 2, self.hr, self.cs),
                pl.BlockSpec((None, None, None, self.tr, self.cs), lambda b, j, i, s: (b, jf(j, s), hf(j, s), i, 0)))

    def half_shard(self):
        return (self.lead, self.hr, self.cs), pl.BlockSpec((None, self.tr, self.cs), lambda b, j, i, s: (b, i, 0))

    def shard(self, hf):
        return ((self.lead, 2, self.hr, self.cs),
                pl.BlockSpec((None, None, self.tr, self.cs), lambda b, j, i, s: (b, hf(j, s), i, 0)))


def _view_sum(sel, operands, out_view, out_shape, out_dtype, grid, name):
    k = len(operands)

    def body(sel_ref, *refs):
        acc = refs[0][...].astype(F32)
        for r in refs[1:k]:
            acc = acc + r[...].astype(F32)
        refs[k][...] = acc.astype(out_dtype)

    spec = pltpu.PrefetchScalarGridSpec(num_scalar_prefetch=1, grid=grid, in_specs=[v[1] for _, v in operands],
                                        out_specs=out_view[1])
    out = _call(body, name=name, grid_spec=spec, out_shape=_sds(out_view[0], out_dtype),
                compiler_params=_params(("parallel", "parallel", "parallel")))(
                    sel, *[a.reshape(v[0]) for a, v in operands])
    return out.reshape(out_shape)


def _sel_core(j, s):
    return s[0]


def _sel_chip(j, s):
    return s[1]


def _grid_j(j, s):
    return j


def _place_shard(lay, sel, w):
    return _view_sum(sel, [(w, lay.shard(_grid_j))], lay.full(_sel_chip, _grid_j), lay.full_shape, BF16,
                     (lay.lead, 2, lay.hr // lay.tr), "place_shard")


SEM_SPEC = pl.BlockSpec(memory_space=pltpu.SEMAPHORE)
ANY_SPEC = pl.BlockSpec(memory_space=pl.ANY)
SPLIT_COPY = pltpu.SideEffectType.DATAFLOW_SIDE_EFFECTING


def _in_hbm(a):
    return pltpu.with_memory_space_constraint(a, pltpu.HBM)


def _gather_copies(lays, bufs, send_sems, recv_sems):
    x, y, c, chips = _place()
    copies = []
    for a, lay in enumerate(lays):
        own = _full_region(bufs[a], lay.axis, _chip(x, y), c, lay.shard_shape)
        for k, chip in enumerate(chips):
            copies.append(pltpu.make_async_remote_copy(
                src_ref=own, dst_ref=own, send_sem=send_sems.at[a * 3 + k], recv_sem=recv_sems.at[a * 3 + k],
                device_id=(*chip, c), device_id_type=MESH))
    return copies


def _gather_start(fulls, after, lays, tag):
    n = len(fulls)

    def body(*refs):
        send_sems, recv_sems = refs[n + 1], refs[n + 2]
        bufs, token = refs[n + 3:2 * n + 3], refs[2 * n + 3]
        for cp in _gather_copies(lays, bufs, send_sems, recv_sems):
            cp.start()
        token[...] = jnp.zeros_like(token)

    outs = _call(body, name="gather_start_" + tag,
                 out_shape=[pltpu.SemaphoreType.DMA((3 * n,)), pltpu.SemaphoreType.DMA((3 * n,))]
                 + [pltpu.HBM(f.shape, f.dtype) for f in fulls] + [_sds((8, 128), F32)],
                 in_specs=[HBM_SPEC] * n + [ANY_SPEC],
                 out_specs=[SEM_SPEC, SEM_SPEC] + [HBM_SPEC] * n + [pl.BlockSpec(memory_space=pltpu.VMEM)],
                 input_output_aliases={a: a + 2 for a in range(n)},
                 compiler_params=pltpu.CompilerParams(has_side_effects=SPLIT_COPY))(*[_in_hbm(f) for f in fulls], after)
    return outs[0], outs[1], outs[2:2 + n], outs[2 + n]


def _gather_wait(send_sems, recv_sems, bufs, after, lays, tag):
    n = len(bufs)

    def body(*refs):
        ss, rs = refs[n], refs[n + 1]
        for cp in _gather_copies(lays, refs[n + 3:], ss, rs):
            cp.wait_send()
            cp.wait_recv()

    return _call(body, name="gather_wait_" + tag,
                 out_shape=[pltpu.HBM(b.shape, b.dtype) for b in bufs],
                 in_specs=[HBM_SPEC] * n + [SEM_SPEC, SEM_SPEC, ANY_SPEC], out_specs=[HBM_SPEC] * n,
                 input_output_aliases={a: a for a in range(n)},
                 compiler_params=pltpu.CompilerParams(has_side_effects=SPLIT_COPY))(*bufs, send_sems, recv_sems, after)


def _gather_forward(bufs, lays):
    n = len(bufs)

    def body(*refs):
        outs = refs[n:2 * n]
        send_sems, recv_sems = refs[2 * n:]
        x, y, c, chips = _place()
        sibling = (x, y, 1 - c)
        sends = []
        for a in range(n):
            for k, chip in enumerate(chips):
                landed = _full_region(outs[a], lays[a].axis, _chip(*chip), c, lays[a].shard_shape)
                cp = pltpu.make_async_remote_copy(
                    src_ref=landed, dst_ref=landed, send_sem=send_sems.at[a * 3 + k], recv_sem=recv_sems.at[a * 3 + k],
                    device_id=sibling, device_id_type=MESH)
                cp.start()
                sends.append(cp)
        for a in range(n):
            for k, chip in enumerate(chips):
                passed = _full_region(outs[a], lays[a].axis, _chip(*chip), 1 - c, lays[a].shard_shape)
                pltpu.make_async_remote_copy(
                    src_ref=passed, dst_ref=passed, send_sem=send_sems.at[a * 3 + k], recv_sem=recv_sems.at[a * 3 + k],
                    device_id=sibling, device_id_type=MESH).wait_recv()
        for cp in sends:
            cp.wait_send()

    return _call(body, name="gather_forward",
                 out_shape=[_sds(b.shape, b.dtype) for b in bufs],
                 in_specs=[HBM_SPEC] * n, out_specs=[HBM_SPEC] * n,
                 input_output_aliases={a: a for a in range(n)},
                 scratch_shapes=[pltpu.SemaphoreType.DMA((3 * n,)), pltpu.SemaphoreType.DMA((3 * n,))],
                 compiler_params=_params())(*bufs)


def _half_rows_shape(full_shape):
    s = list(full_shape)
    s[-2] //= 2
    return tuple(s)


RELATIONS = tuple((r, s) for r in range(N_CHIP) for s in range(2))[1:]


def _peer(x, y, c, rel):
    r, s = rel
    return (1 - x if r in (1, 3) else x, 1 - y if r in (2, 3) else y, 1 - c if s else c)


def _reduce_copies(lays, grads, landing, send_sems, recv_sems):
    x, y, c, _ = _place()
    nr = len(RELATIONS)
    copies = []
    for a, lay in enumerate(lays):
        for k, rel in enumerate(RELATIONS):
            px, py, pc = _peer(x, y, c, rel)
            copies.append(pltpu.make_async_remote_copy(
                src_ref=_full_region(grads[a], lay.axis, _chip(px, py), pc, lay.shard_shape), dst_ref=landing[a * nr + k],
                send_sem=send_sems.at[a * nr + k], recv_sem=recv_sems.at[a * nr + k],
                device_id=(px, py, pc), device_id_type=MESH))
    return copies


def _reduce_start(grads, lays, tag):
    n, nr = len(grads), len(RELATIONS)
    landing = [_in_hbm(lax.empty(lay.half_shard_shape, BF16)) for lay in lays for _ in RELATIONS]
    m = n + n * nr

    def body(*refs):
        send_sems, recv_sems = refs[m], refs[m + 1]
        src, land, token = refs[m + 2:m + 2 + n], refs[m + 2 + n:2 * m + 2], refs[2 * m + 2]
        for cp in _reduce_copies(lays, src, land, send_sems, recv_sems):
            cp.start()
        token[...] = jnp.zeros_like(token)

    ops = [_in_hbm(g) for g in grads] + landing
    outs = _call(body, name="reduce_start_" + tag,
                 out_shape=[pltpu.SemaphoreType.DMA((n * nr,)), pltpu.SemaphoreType.DMA((n * nr,))]
                 + [pltpu.HBM(o.shape, o.dtype) for o in ops] + [_sds((8, 128), F32)],
                 in_specs=[HBM_SPEC] * m,
                 out_specs=[SEM_SPEC, SEM_SPEC] + [HBM_SPEC] * m + [pl.BlockSpec(memory_space=pltpu.VMEM)],
                 input_output_aliases={a: a + 2 for a in range(m)},
                 compiler_params=pltpu.CompilerParams(has_side_effects=SPLIT_COPY))(*ops)
    return outs[0], outs[1], outs[2:2 + n], outs[2 + n:2 + m], outs[2 + m]


def _reduce_wait(send_sems, recv_sems, grads, landing, after, lays, tag):
    n, nr = len(grads), len(RELATIONS)
    m = n + n * nr

    def body(*refs):
        ss, rs = refs[m], refs[m + 1]
        src, land = refs[m + 3:m + 3 + n], refs[m + 3 + n:]
        for cp in _reduce_copies(lays, src, land, ss, rs):
            cp.wait_send()
            cp.wait_recv()

    ops = list(grads) + list(landing)
    outs = _call(body, name="reduce_wait_" + tag,
                 out_shape=[pltpu.HBM(o.shape, o.dtype) for o in ops],
                 in_specs=[HBM_SPEC] * m + [SEM_SPEC, SEM_SPEC, ANY_SPEC], out_specs=[HBM_SPEC] * m,
                 input_output_aliases={a: a for a in range(m)},
                 compiler_params=pltpu.CompilerParams(has_side_effects=SPLIT_COPY))(*ops, send_sems, recv_sems, after)
    return outs[:n], [outs[n + nr * a:n + nr * a + nr] for a in range(n)]


def _share_halves(shards):
    n = len(shards)

    def body(*refs):
        outs = refs[n:2 * n]
        send_sems, recv_sems = refs[2 * n:]
        x, y, c, _ = _place()
        sibling = (x, y, 1 - c)
        started = []
        for a in range(n):
            mine = _shard_half(outs[a], c)
            rc = pltpu.make_async_remote_copy(src_ref=mine, dst_ref=mine, send_sem=send_sems.at[a],
                                              recv_sem=recv_sems.at[a], device_id=sibling, device_id_type=MESH)
            rc.start()
            started.append(rc)
        for rc in started:
            rc.wait_recv()
            rc.wait_send()

    return _call(body, name="share_halves", out_shape=[_sds(s.shape, F32) for s in shards],
                 in_specs=[HBM_SPEC] * n, out_specs=[HBM_SPEC] * n, input_output_aliases={a: a for a in range(n)},
                 scratch_shapes=[pltpu.SemaphoreType.DMA((n,)), pltpu.SemaphoreType.DMA((n,))],
                 compiler_params=_params())(*shards)


def _reduce_end(state, after, lays, sel, tag):
    send_sems, recv_sems, grads, landing, _ = state
    grads, landed = _reduce_wait(send_sems, recv_sems, grads, landing, after, lays, tag)
    halves = [
        _view_sum(sel, [(g, lay.full(_sel_chip, _sel_core))] + [(l, lay.half_shard()) for l in ls], lay.shard(_sel_core),
                  lay.shard_shape, F32, (lay.lead, 1, lay.hr // lay.tr), "shard_half_sum")
        for g, ls, lay in zip(grads, landed, lays)]
    return _share_halves(halves)


def _embed(blocks):
    n, r, c = blocks.shape
    eye = jnp.eye(n, dtype=blocks.dtype)
    return (blocks[:, :, None, :] * eye[:, None, :, None]).reshape(n * r, n * c)


def _unembed(mat, n):
    r, c = mat.shape[0] // n, mat.shape[1] // n
    return jnp.transpose(jnp.diagonal(mat.reshape(n, r, n, c), axis1=0, axis2=2), (2, 0, 1))


def _to_heads(a):
    return jnp.transpose(a.reshape(a.shape[0], HEADS, HD), (1, 0, 2))


def _from_heads(a):
    return jnp.transpose(a, (1, 0, 2)).reshape(a.shape[1], W)


def _row(v):
    return v.reshape(1, -1)


def _concat_cols(pieces):
    L = pieces[0].shape[0]
    widths = [p.shape[1] for p in pieces]
    tr = _row_tile(L)

    def body(*refs):
        off = 0
        for r, w in zip(refs[:-1], widths):
            refs[-1][:, off:off + w] = r[...].astype(BF16)
            off += w

    return _call(body, name="concat_cols", grid=(L // tr,),
                 in_specs=[pl.BlockSpec((tr, w), lambda i: (i, 0)) for w in widths],
                 out_specs=pl.BlockSpec((tr, sum(widths)), lambda i: (i, 0)), out_shape=_sds((L, sum(widths)), BF16),
                 compiler_params=_params(("parallel",)))(*pieces)


def _ffn_fwd(x, ada, gp, gq, w_in, w_out, s):
    L = x.shape[0]
    h = _norm_mod(x, _row(gp[s]), _row(ada[3 * s]), _row(ada[3 * s + 1]))
    a, b, act = _ffn_in(h, w_in)
    f = _mm(act, w_out, M=L, N=D, K=FF, tm=min(L, 1024), tn=512, name="ffn_out")
    x2 = _post(x, f, _row(gq[s]), _row(ada[3 * s + 2]), 0.5)
    return x2, (x, h, a, b, act, f)


def _ffn_bwd(dx, saved, ada, gp, gq, w_in, w_out, s, stats):
    x, h, a, b, act, f = saved
    L = x.shape[0]
    df, stats = _post_bwd(dx, f, _row(gq[s]), _row(ada[3 * s + 2]), 0.5, stats, s)
    dw_out = _mm(act, df, M=FF, N=D, K=L, tm=256, tn=1024, ta=True, out_dtype=BF16, name="ffn_dw_out")
    da, db = _ffn_mid_bwd(df, w_out, a, b)
    du = _concat_cols([da, db])
    dw_in = _mm(h, du, M=D, N=2 * FF, K=L, tm=1024, tn=512, ta=True, out_dtype=BF16, name="ffn_dw_in")
    dh = _mm(du, w_in, M=L, N=D, K=2 * FF, tm=min(L, 1024), tn=1024, tk=1408, tb=True, name="ffn_dh")
    dx2, stats = _norm_mod_bwd(dh, x, _row(gp[s]), _row(ada[3 * s + 1]), dx, stats, s)
    return dx2, dw_in, dw_out, stats


def _mixer_fwd(x, ada, gp, gq, wf, sm):
    L = x.shape[0]
    h = _norm_mod(x, _row(gp[1]), _row(ada[3]), _row(ada[4]))
    p = _mm(h, wf["w_in"], M=L, N=IN_COLS, K=D, tm=min(L, 1024), tn=512, name="mixer_in")
    za = _conv_fwd(p, sm["conv_w"])
    y, zb = _ssm_fwd(p, sm["b_re"], sm["b_im"], sm["c_re"], sm["c_im"], sm["abr"], sm["abi"], sm["fr"], sm["fi"], sm["ssm_d"])
    zc = _pool_fwd(p, sm["w_pool"], sm["pool_scale"])
    q = _to_heads(p[:, 5 * W:6 * W]) * (HD ** -0.5)
    k = _to_heads(p[:, 6 * W:7 * W])
    v = _to_heads(p[:, 7 * W:8 * W])
    o_heads, block_sums = _sb_fwd(q, k, v)
    zd = _from_heads(o_heads).astype(BF16)
    merged = _merge_fwd(za, zb, zc, zd, p, wf["w_conv_out"], wf["w_glu"], wf["w_pool_out"], wf["w_sb_out"])
    m = _mm(merged, wf["w_out"], M=L, N=D, K=D, tm=min(L, 1024), tn=512, name="mixer_out")
    x2 = _post(x, m, _row(gq[1]), _row(ada[5]), 1.0)
    return x2, (x, h, p, za, y, zb, zc, zd, q, k, v, block_sums, merged, m)


def _mixer_bwd(dx, saved, ada, gp, gq, wf, sm, stats):
    x, h, p, za, y, zb, zc, zd, q, k, v, block_sums, merged, m = saved
    L = x.shape[0]
    dmf, stats = _post_bwd(dx, m, _row(gq[1]), _row(ada[5]), 1.0, stats, 1)
    dw_out = _mm(merged, dmf, M=D, N=D, K=L, tm=512, tn=512, ta=True, out_dtype=BF16, name="mixer_dw_out")
    dmerged = _mm(dmf, wf["w_out"], M=L, N=D, K=D, tm=min(L, 1024), tn=512, tb=True, name="mixer_dmerged")
    dza, dzb, dzc, dzd, dgates, dwc, dwg, dwp, dws = _merge_bwd(
        dmerged, za, zb, zc, zd, p, wf["w_conv_out"], wf["w_glu"], wf["w_pool_out"], wf["w_sb_out"])
    dconv, dconv_w = _conv_bwd(p, sm["conv_w"], dza)
    (du_ssm, dd, dbr, dbi, dcr, dci, gar, gai, gfr, gfi) = _ssm_bwd(
        p, y, dzb, sm["b_re"], sm["b_im"], sm["c_re"], sm["c_im"], sm["abr"], sm["abi"], sm["fr"], sm["fi"], sm["ssm_d"])
    du_pool, dwpool, dpscale = _pool_bwd(p, sm["w_pool"], sm["pool_scale"], dzc)
    dq, dk, dv = _sb_bwd(q, k, v, _to_heads(dzd), block_sums)
    dqkv = [_from_heads(t) for t in (dq * (HD ** -0.5), dk, dv)]
    dp = _concat_cols([dconv, du_ssm, du_pool] + dqkv + [dgates])
    dw_in = _mm(h, dp, M=D, N=IN_COLS, K=L, tm=1024, tn=512, ta=True, out_dtype=BF16, name="mixer_dw_in")
    dh = _mm(dp, wf["w_in"], M=L, N=D, K=IN_COLS, tm=min(L, 1024), tn=1024, tk=1536, tb=True, name="mixer_dh")
    dx2, stats = _norm_mod_bwd(dh, x, _row(gp[1]), _row(ada[4]), dx, stats, 1)
    wgrads = [dw_in, dwc, dwg, dwp, dws, dw_out]
    small = {"conv_w": dconv_w, "ssm_d": dd, "b_re": dbr, "b_im": dbi, "c_re": dcr, "c_im": dci,
             "abr": gar, "abi": gai, "fr": gfr, "fi": gfi, "w_pool": dwpool, "pool_scale": dpscale}
    return dx2, wgrads, small, stats


def _pack(arrays):
    flat = jnp.concatenate([a.reshape(-1) for a in arrays])
    rows = -(-flat.shape[0] // 128)
    rows = -(-rows // 64) * 64
    return jnp.pad(flat, (0, rows * 128 - flat.shape[0])).reshape(rows, 128)


def _unpack(block, shapes):
    flat = block.reshape(-1)
    out, off = [], 0
    for s in shapes:
        n = int(np.prod(s))
        out.append(flat[off:off + n].reshape(s))
        off += n
    return out


def _pad_rows(a, mult):
    rows = -(-a.shape[0] // mult) * mult
    return jnp.concatenate([a] * (-(-rows // a.shape[0])), axis=0)[:rows]


SMALL_ORDER = ("stats", "conv_w", "lam_re", "lam_im", "log_dt", "ssm_b_re", "ssm_b_im",
               "ssm_c_re", "ssm_c_im", "ssm_d", "w_pool", "pool_scale")
WEIGHTS = ('w_ada', 'b_ada', 'g_pre', 'g_post', 'w_ff_in', 'w_ff_out', 'w_in', 'conv_w', 'w_conv_out', 'lam_re', 'lam_im',
           'log_dt', 'ssm_b_re', 'ssm_b_im', 'ssm_c_re', 'ssm_c_im', 'ssm_d', 'w_glu', 'w_pool', 'pool_scale', 'w_pool_out',
           'w_sb_out', 'w_out')


def _step(a):
    depth = a["w_ada"].shape[0]
    x = a["x"][0]
    target = a["loss_target"][0]
    L = x.shape[0]
    ix, iy, ic = lax.axis_index("x"), lax.axis_index("y"), lax.axis_index("c")
    chip = 2 * ix + iy
    me = 4 * ix + 2 * iy + ic
    sel = jnp.stack([ic, chip]).astype(jnp.int32)
    lays = [_Lay(a[name].shape[(1 if idx is None else 2):], ax) for name, idx, ax, _ in GATHERED]

    def entries(g):
        return [i for i, e in enumerate(GATHERED) if e[3] == g]

    def shard_of(i, l):
        name, idx = GATHERED[i][0], GATHERED[i][1]
        return a[name][l] if idx is None else a[name][l, idx]

    stages = [(l, g) for l in range(depth) for g in range(N_SUB)]

    def gather_begin(t, after):
        l, g = stages[t]
        placed = [_place_shard(lays[i], sel, shard_of(i, l)) for i in entries(g)]
        return _gather_start(placed, after, [lays[i] for i in entries(g)], str(t))

    pending = {t: gather_begin(t, x) for t in range(min(2, len(stages)))}
    started = sum(p[3][0, 0] for p in pending.values())

    first_shapes = [(D,), (depth, 3, W), (depth, 3, W), (depth, 3, W // N_CHIP)]
    gathered = _allgather8(_pack([a["c"] + started, a["g_pre"], a["g_post"], a["conv_w"]]), "gather_small_inputs")
    per_dev = [_unpack(blk, first_shapes) for blk in gathered.reshape(N_DEV, -1, 128)]
    c_all = jnp.stack([d[0] for d in per_dev])
    c_pad = jnp.concatenate([c_all, jnp.zeros_like(c_all)], axis=0)
    g_pre = jnp.concatenate([per_dev[2 * j][1] for j in range(N_CHIP)], axis=-1)
    g_post = jnp.concatenate([per_dev[2 * j][2] for j in range(N_CHIP)], axis=-1)
    conv_w = jnp.concatenate([per_dev[2 * j][3] for j in range(N_CHIP)], axis=-1)

    b_cols = lax.dynamic_slice(a["b_ada"], (0, chip * ADA_SHARD), (depth, ADA_SHARD)).reshape(depth, 1, ADA_SHARD)
    ada_part = _ada_fwd(c_pad, a["w_ada"], b_cols)
    ada_all = _allgather8(ada_part.reshape(depth * 16, ADA_SHARD), "gather_ada").reshape(N_DEV, depth, 16, ADA_SHARD)
    ada_rows = lax.dynamic_slice(ada_all, (0, 0, me, 0), (N_DEV, depth, 1, ADA_SHARD))[:, :, 0]
    ada = jnp.concatenate([ada_rows[2 * j] for j in range(N_CHIP)], axis=-1).reshape(depth, 9, D)

    lam_re = _pad_rows(a["lam_re"].reshape(depth, NST), 8)
    lam_im = _pad_rows(a["lam_im"].reshape(depth, NST), 8)
    log_dt_x = _pad_rows(jnp.repeat(a["log_dt"], GP, axis=1), 8)
    abr, abi, fr, fi = _ssm_prep(lam_re, lam_im, log_dt_x)

    def small_of(l):
        return {"conv_w": conv_w[l], "ssm_d": _row(a["ssm_d"][l]), "pool_scale": _row(a["pool_scale"][l]),
                "b_re": _embed(jnp.transpose(a["ssm_b_re"][l], (0, 2, 1))), "b_im": _embed(jnp.transpose(a["ssm_b_im"][l], (0, 2, 1))),
                "c_re": _embed(jnp.transpose(a["ssm_c_re"][l], (0, 2, 1))), "c_im": _embed(jnp.transpose(a["ssm_c_im"][l], (0, 2, 1))),
                "w_pool": _embed(a["w_pool"][l]),
                "abr": abr[l:l + 1], "abi": abi[l:l + 1], "fr": fr[l:l + 1], "fi": fi[l:l + 1]}

    saved, weights, smalls = [], [], [small_of(l) for l in range(depth)]
    for t, (l, g) in enumerate(stages):
        glays = [lays[i] for i in entries(g)]
        send_sems, recv_sems, bufs, _ = pending.pop(t)
        w = _gather_forward(_gather_wait(send_sems, recv_sems, bufs, x if t else ada, glays, str(t)), glays)
        weights.append(w)
        ada_l = ada[l]
        if t + 2 < len(stages):
            pending[t + 2] = gather_begin(t + 2, x)
            ada_l = ada_l + pending[t + 2][3][0, 0]
        if g == 1:
            wf = {GATHERED[i][0]: wi for i, wi in zip(entries(1), w)}
            x, sv = _mixer_fwd(x, ada_l, g_pre[l], g_post[l], wf, smalls[l])
        else:
            x, sv = _ffn_fwd(x, ada_l, g_pre[l], g_post[l], w[0], w[1], g)
        saved.append(sv)
    dx, loss_part = _loss_head(x, target)
    loss = lax.psum(loss_part[0, 0], ("x", "y", "c"))

    shard_grads = [[None] * depth for _ in GATHERED]
    small_grads = [{} for _ in range(depth)]
    stats = [jnp.zeros((STAT_ROWS, D), F32) for _ in range(depth)]
    states = {}

    def reduce_finish(t, after):
        l, g = stages[t]
        glays = [lays[i] for i in entries(g)]
        for i, grad in zip(entries(g), _reduce_end(states.pop(t), after, glays, sel, str(t))):
            shard_grads[i][l] = grad

    token = None
    for t in reversed(range(len(stages))):
        l, g = stages[t]
        ada_l = ada[l] if token is None else ada[l] + token[0, 0]
        if g == 1:
            wf = {GATHERED[i][0]: wi for i, wi in zip(entries(1), weights[t])}
            dx, wgrads, small, stats[l] = _mixer_bwd(dx, saved[t], ada_l, g_pre[l], g_post[l], wf, smalls[l], stats[l])
            small_grads[l].update(small)
        else:
            dx, dw_in, dw_out, stats[l] = _ffn_bwd(dx, saved[t], ada_l, g_pre[l], g_post[l], weights[t][0], weights[t][1], g,
                                                   stats[l])
            wgrads = [dw_in, dw_out]
        states[t] = _reduce_start(wgrads, [lays[i] for i in entries(g)], str(t))
        token = states[t][4]
        if t + 2 in states:
            reduce_finish(t + 2, dx)
    stack = lambda key: _pad_rows(jnp.concatenate([small_grads[l][key] for l in range(depth)], axis=0), 8)
    gs = np.zeros((NST, 128), np.float32)
    gs[np.arange(NST), np.arange(NST) // GP] = 1.0
    dlr, dli, dldt = _ssm_prep_bwd(lam_re, lam_im, log_dt_x, stack("abr"), stack("abi"), stack("fr"), stack("fi"), jnp.asarray(gs))
    part = {
        "stats": jnp.stack(stats) + sum(st[4][0, 0] for st in states.values()),
        "conv_w": jnp.stack([small_grads[l]["conv_w"] for l in range(depth)]),
        "lam_re": dlr[:depth].reshape(depth, G, GP), "lam_im": dli[:depth].reshape(depth, G, GP), "log_dt": dldt[:depth, :G],
        "ssm_b_re": jnp.stack([jnp.transpose(_unembed(small_grads[l]["b_re"], G), (0, 2, 1)) for l in range(depth)]),
        "ssm_b_im": jnp.stack([jnp.transpose(_unembed(small_grads[l]["b_im"], G), (0, 2, 1)) for l in range(depth)]),
        "ssm_c_re": jnp.stack([jnp.transpose(_unembed(small_grads[l]["c_re"], G), (0, 2, 1)) for l in range(depth)]),
        "ssm_c_im": jnp.stack([jnp.transpose(_unembed(small_grads[l]["c_im"], G), (0, 2, 1)) for l in range(depth)]),
        "ssm_d": jnp.stack([small_grads[l]["ssm_d"][0] for l in range(depth)]),
        "w_pool": jnp.stack([_unembed(small_grads[l]["w_pool"], len(POOL_WINDOWS)) for l in range(depth)]),
        "pool_scale": jnp.stack([small_grads[l]["pool_scale"][0] for l in range(depth)]),
    }
    small_shapes = [part[k].shape for k in SMALL_ORDER]
    blocks = _allgather8(_pack([part[k] for k in SMALL_ORDER]), "gather_small_grads").reshape(N_DEV, -1, 128)
    small_sum = _sum_parts([blocks[i] for i in range(N_DEV)], F32, "small_grad_sum")
    total = dict(zip(SMALL_ORDER, _unpack(small_sum, small_shapes)))
    d_ada_all = jnp.stack([_unpack(blocks[i], small_shapes[:1])[0][:, :9].reshape(depth, 9 * D) for i in range(N_DEV)])
    d_cols = lax.dynamic_slice(d_ada_all, (0, 0, chip * ADA_SHARD), (N_DEV, depth, ADA_SHARD))
    d_cols = jnp.transpose(d_cols, (1, 0, 2))
    grads = {"w_ada": _ada_wgrad(c_pad, jnp.concatenate([d_cols, jnp.zeros_like(d_cols)], axis=1)),
             "b_ada": total["stats"][:, :9].reshape(depth, 9 * D),
             "g_pre": lax.dynamic_slice(total["stats"], (0, 9, chip * W), (depth, 3, W)),
             "g_post": lax.dynamic_slice(total["stats"], (0, 12, chip * W), (depth, 3, W)),
             "conv_w": lax.dynamic_slice(total["conv_w"], (0, 0, chip * (W // N_CHIP)), (depth, 3, W // N_CHIP))}
    for k in SMALL_ORDER[2:]:
        grads[k] = total[k]

    out = {"loss": loss, "grad_x": dx[None]}

    def update(name):
        out["grad_" + name] = grads[name]
        out["delta_" + name], out["new_m_" + name], out["new_v_" + name] = _adamw(a[name], grads[name], a["m_" + name], a["v_" + name])

    for name in WEIGHTS:
        if name in grads:
            update(name)
    for t in sorted(states, reverse=True):
        reduce_finish(t, out["delta_w_ada"])
    for name in sorted({e[0] for e in GATHERED}):
        cols = [shard_grads[i] for i, e in enumerate(GATHERED) if e[0] == name]
        grads[name] = jnp.stack(cols[0]) if len(cols) == 1 else jnp.stack([jnp.stack(pair) for pair in zip(*cols)])
        update(name)
    return out


def kernel(x, c, w_ada, b_ada, g_pre, g_post, w_ff_in, w_ff_out, w_in, conv_w, w_conv_out, lam_re, lam_im, log_dt, ssm_b_re, ssm_b_im, ssm_c_re, ssm_c_im, ssm_d, w_glu, w_pool, pool_scale, w_pool_out, w_sb_out, w_out, loss_target, m_w_ada, m_b_ada, m_g_pre, m_g_post, m_w_ff_in, m_w_ff_out, m_w_in, m_conv_w, m_w_conv_out, m_lam_re, m_lam_im, m_log_dt, m_ssm_b_re, m_ssm_b_im, m_ssm_c_re, m_ssm_c_im, m_ssm_d, m_w_glu, m_w_pool, m_pool_scale, m_w_pool_out, m_w_sb_out, m_w_out, v_w_ada, v_b_ada, v_g_pre, v_g_post, v_w_ff_in, v_w_ff_out, v_w_in, v_conv_w, v_w_conv_out, v_lam_re, v_lam_im, v_log_dt, v_ssm_b_re, v_ssm_b_im, v_ssm_c_re, v_ssm_c_im, v_ssm_d, v_w_glu, v_w_pool, v_pool_scale, v_w_pool_out, v_w_sb_out, v_w_out):
    out = _step(dict(locals()))
    names = ["loss", "grad_x"] + [p + n for p in ("grad_", "delta_", "new_m_", "new_v_") for n in WEIGHTS]
    return tuple(out[n] for n in names)
```

```python
import functools
import math

import jax
import jax.numpy as jnp
import numpy as np
from jax import lax
from jax.experimental import pallas as pl
from jax.experimental.pallas import tpu as pltpu

F32 = jnp.float32
BF16 = jnp.bfloat16
MESH = pl.DeviceIdType.MESH

D = 1024
W = 256
FF = 2816
IN_COLS = 6144
G = 16
GH = 16
GP = 64
NST = G * GP
QB = 128
HEADS = 4
HD = 64
EPS = 1e-6
LAMBDA_RE_MAX = -1e-4
POOL_WINDOWS = (2, 4, 8, 16)
N_CHIP = 4
N_DEV = 8
VMEM_LIMIT = 56 * 1024 * 1024
HIGH = lax.Precision.HIGHEST

ADAM_LR, ADAM_B1, ADAM_B2, ADAM_EPS, ADAM_WD, ADAM_STEP = 0.001, 0.9, 0.999, 1e-08, 0.01, 10


def _call(body, **kw):
    return pl.pallas_call(body, **kw)


def _params(dims=None, **kw):
    return pltpu.CompilerParams(dimension_semantics=dims, vmem_limit_bytes=VMEM_LIMIT, **kw)


def _sds(shape, dtype):
    return jax.ShapeDtypeStruct(shape, dtype)


def _dot(a, b, ca=1, cb=0, precision=None):
    return lax.dot_general(a, b, (((ca,), (cb,)), ((), ())), preferred_element_type=F32, precision=precision)


def _bdot(a, b, ca=1, cb=0):
    return _dot(a.astype(BF16), b.astype(BF16), ca, cb)


def _sigmoid(x):
    return 1.0 / (1.0 + jnp.exp(-x))


def _mm(a, b, *, M, N, K, tm, tn, tk=None, ta=False, tb=False, out_dtype=F32, a_off=(0, 0), b_off=(0, 0), name):
    tk = K if tk is None else tk
    nk = K // tk
    assert M % tm == 0 and N % tn == 0 and K % tk == 0

    def body(a_ref, b_ref, o_ref, *acc):
        part = _bdot(a_ref[...], b_ref[...], 0 if ta else 1, 1 if tb else 0)
        if nk == 1:
            o_ref[...] = part.astype(out_dtype)
            return
        acc_ref = acc[0]
        k = pl.program_id(2)

        @pl.when(k == 0)
        def _():
            acc_ref[...] = part

        @pl.when(k > 0)
        def _():
            acc_ref[...] += part

        @pl.when(k == nk - 1)
        def _():
            o_ref[...] = acc_ref[...].astype(out_dtype)

    if ta:
        a_spec = pl.BlockSpec((tk, tm), lambda i, j, k: (k + a_off[0], i + a_off[1]))
    else:
        a_spec = pl.BlockSpec((tm, tk), lambda i, j, k: (i + a_off[0], k + a_off[1]))
    if tb:
        b_spec = pl.BlockSpec((tn, tk), lambda i, j, k: (j + b_off[0], k + b_off[1]))
    else:
        b_spec = pl.BlockSpec((tk, tn), lambda i, j, k: (k + b_off[0], j + b_off[1]))
    return _call(
        body, name=name, grid=(M // tm, N // tn, nk),
        in_specs=[a_spec, b_spec],
        out_specs=pl.BlockSpec((tm, tn), lambda i, j, k: (i, j)),
        out_shape=_sds((M, N), out_dtype),
        scratch_shapes=[] if nk == 1 else [pltpu.VMEM((tm, tn), F32)],
        compiler_params=_params(("parallel", "parallel", "arbitrary")),
    )(a, b)


def _row_tile(L):
    return min(L, 256)


def _wide_tile(L):
    return min(L, 512)


def _norm_mod(x, g, shift, scale):
    L = x.shape[0]
    tr = _wide_tile(L)

    def body(x_ref, g_ref, sh_ref, sc_ref, h_ref):
        xv = x_ref[...]
        r = lax.rsqrt(jnp.mean(xv * xv, axis=-1, keepdims=True) + EPS)
        h_ref[...] = (xv * r * g_ref[...] * (1.0 + sc_ref[...]) + sh_ref[...]).astype(BF16)

    row = pl.BlockSpec((tr, D), lambda i: (i, 0))
    vec = pl.BlockSpec((1, D), lambda i: (0, 0))
    return _call(body, name="norm_mod", grid=(L // tr,), in_specs=[row, vec, vec, vec], out_specs=row,
                 out_shape=_sds((L, D), BF16), compiler_params=_params(("parallel",)))(x, g, shift, scale)


STAT_ROWS = 16


def _norm_mod_bwd(dh, x, g, scale, dx_res, stats, s):
    L = x.shape[0]
    tr = _wide_tile(L)

    def body(dh_ref, x_ref, g_ref, sc_ref, dxr_ref, stin_ref, dx_ref, st_ref):
        i = pl.program_id(0)
        xv = x_ref[...]
        dhv = dh_ref[...]
        r = lax.rsqrt(jnp.mean(xv * xv, axis=-1, keepdims=True) + EPS)
        y = xv * r
        n = y * g_ref[...]
        dn = dhv * (1.0 + sc_ref[...])
        dy = dn * g_ref[...]
        dx_ref[...] = dxr_ref[...] + r * (dy - y * jnp.mean(dy * y, axis=-1, keepdims=True))

        @pl.when(i == 0)
        def _():
            st_ref[...] = stin_ref[...]

        st_ref[3 * s:3 * s + 1, :] += jnp.sum(dhv, axis=0, keepdims=True)
        st_ref[3 * s + 1:3 * s + 2, :] += jnp.sum(dhv * n, axis=0, keepdims=True)
        st_ref[9 + s:10 + s, :] += jnp.sum(dn * y, axis=0, keepdims=True)

    row = pl.BlockSpec((tr, D), lambda i: (i, 0))
    vec = pl.BlockSpec((1, D), lambda i: (0, 0))
    st = pl.BlockSpec((STAT_ROWS, D), lambda i: (0, 0))
    return _call(body, name="norm_mod_bwd", grid=(L // tr,), in_specs=[row, row, vec, vec, row, st],
                 out_specs=[row, st], out_shape=[_sds((L, D), F32), _sds((STAT_ROWS, D), F32)],
                 input_output_aliases={5: 1},
                 compiler_params=_params(("arbitrary",)))(dh, x, g, scale, dx_res, stats)


def _post(x, f, g, gate, res_weight):
    L = x.shape[0]
    tr = _wide_tile(L)

    def body(x_ref, f_ref, g_ref, gt_ref, o_ref):
        fv = f_ref[...]
        r = lax.rsqrt(jnp.mean(fv * fv, axis=-1, keepdims=True) + EPS)
        o_ref[...] = x_ref[...] + (res_weight * (1.0 + gt_ref[...])) * (fv * r * g_ref[...])

    row = pl.BlockSpec((tr, D), lambda i: (i, 0))
    vec = pl.BlockSpec((1, D), lambda i: (0, 0))
    return _call(body, name="post", grid=(L // tr,), in_specs=[row, row, vec, vec], out_specs=row,
                 out_shape=_sds((L, D), F32), compiler_params=_params(("parallel",)))(x, f, g, gate)


def _post_bwd(dx, f, g, gate, res_weight, stats, s):
    L = dx.shape[0]
    tr = _wide_tile(L)

    def body(dx_ref, f_ref, g_ref, gt_ref, stin_ref, df_ref, st_ref):
        i = pl.program_id(0)
        fv = f_ref[...]
        dxv = dx_ref[...]
        r = lax.rsqrt(jnp.mean(fv * fv, axis=-1, keepdims=True) + EPS)
        y = fv * r
        dn = dxv * (res_weight * (1.0 + gt_ref[...]))
        dy = dn * g_ref[...]
        df_ref[...] = (r * (dy - y * jnp.mean(dy * y, axis=-1, keepdims=True))).astype(BF16)

        @pl.when(i == 0)
        def _():
            st_ref[...] = stin_ref[...]

        st_ref[3 * s + 2:3 * s + 3, :] += res_weight * jnp.sum(dxv * (y * g_ref[...]), axis=0, keepdims=True)
        st_ref[12 + s:13 + s, :] += jnp.sum(dn * y, axis=0, keepdims=True)

    row = pl.BlockSpec((tr, D), lambda i: (i, 0))
    vec = pl.BlockSpec((1, D), lambda i: (0, 0))
    st = pl.BlockSpec((STAT_ROWS, D), lambda i: (0, 0))
    return _call(body, name="post_bwd", grid=(L // tr,), in_specs=[row, row, vec, vec, st],
                 out_specs=[row, st], out_shape=[_sds((L, D), BF16), _sds((STAT_ROWS, D), F32)],
                 input_output_aliases={4: 1},
                 compiler_params=_params(("arbitrary",)))(dx, f, g, gate, stats)


def _loss_head(x, target):
    L = x.shape[0]
    tr = _wide_tile(L)

    def body(x_ref, t_ref, dx_ref, loss_ref):
        i = pl.program_id(0)
        err = x_ref[...] - t_ref[...]
        dx_ref[...] = err * (1.0 / D)

        @pl.when(i == 0)
        def _():
            loss_ref[...] = jnp.zeros_like(loss_ref)

        loss_ref[...] += 0.5 * jnp.sum(jnp.mean(err * err, axis=-1, keepdims=True), axis=0, keepdims=True)

    row = pl.BlockSpec((tr, D), lambda i: (i, 0))
    return _call(body, name="loss_head", grid=(L // tr,), in_specs=[row, row],
                 out_specs=[row, pl.BlockSpec((1, 1), lambda i: (0, 0))],
                 out_shape=[_sds((L, D), F32), _sds((1, 1), F32)],
                 compiler_params=_params(("arbitrary",)))(x, target)


def _ffn_in(h, w_in):
    L = h.shape[0]
    tm, tn = min(L, 2048), 256
    nf = FF // tn

    def body(h_ref, wa_ref, wb_ref, a_ref, b_ref, act_ref):
        hv = h_ref[...]
        a = _dot(hv, wa_ref[...])
        b = _dot(hv, wb_ref[...])
        a_ref[...] = a
        b_ref[...] = b
        act_ref[...] = (a * _sigmoid(a) * b).astype(BF16)

    tile = pl.BlockSpec((tm, tn), lambda i, j: (i, j))
    return _call(body, name="ffn_in", grid=(L // tm, nf),
                 in_specs=[pl.BlockSpec((tm, D), lambda i, j: (i, 0)),
                           pl.BlockSpec((D, tn), lambda i, j: (0, j)),
                           pl.BlockSpec((D, tn), lambda i, j: (0, j + nf))],
                 out_specs=[tile, tile, tile],
                 out_shape=[_sds((L, FF), F32), _sds((L, FF), F32), _sds((L, FF), BF16)],
                 compiler_params=_params(("parallel", "parallel")))(h, w_in, w_in)


def _ffn_mid_bwd(df, w_out, a, b):
    L = df.shape[0]
    tm, tn = min(L, 2048), 256

    def body(df_ref, w_ref, a_ref, b_ref, da_ref, db_ref):
        dact = _dot(df_ref[...], w_ref[...], 1, 1)
        av = a_ref[...]
        sg = _sigmoid(av)
        da_ref[...] = (dact * b_ref[...] * (sg * (1.0 + av * (1.0 - sg)))).astype(BF16)
        db_ref[...] = (dact * (av * sg)).astype(BF16)

    tile = pl.BlockSpec((tm, tn), lambda i, j: (i, j))
    return _call(body, name="ffn_mid_bwd", grid=(L // tm, FF // tn),
                 in_specs=[pl.BlockSpec((tm, D), lambda i, j: (i, 0)),
                           pl.BlockSpec((tn, D), lambda i, j: (j, 0)), tile, tile],
                 out_specs=[tile, tile],
                 out_shape=[_sds((L, FF), BF16), _sds((L, FF), BF16)],
                 compiler_params=_params(("parallel", "parallel")))(df, w_out, a, b)


def _rows_before(ref, i, tr, halo):
    start = pl.multiple_of(jnp.maximum(i * tr - halo, 0), 8)
    return jnp.where(i > 0, ref[pl.ds(start, halo), :], 0.0)


def _rows_after(ref, i, n, tr, halo):
    start = pl.multiple_of(jnp.minimum((i + 1) * tr, (n - 1) * tr), 8)
    return jnp.where(i < n - 1, ref[pl.ds(start, halo), :], 0.0)


def _conv_fwd(p, conv_w):
    L = p.shape[0]
    tr = _row_tile(L)
    n = L // tr

    def body(bg_ref, cg_ref, v_ref, w_ref, za_ref, u_scr):
        i = pl.program_id(0)

        @pl.when(i == 0)
        def _():
            u_scr[...] = cg_ref[...] * v_ref[...]

        r0 = pl.multiple_of(i * tr, 8)
        ext = jnp.concatenate([_rows_before(u_scr, i, tr, 8), u_scr[pl.ds(r0, tr), :]], axis=0)
        w = w_ref[...]
        y = (w[0:1] * pltpu.roll(ext, 2, axis=0) + w[1:2] * pltpu.roll(ext, 1, axis=0) + w[2:3] * ext)[8:, :]
        za_ref[...] = (bg_ref[pl.ds(r0, tr), :] * y).astype(BF16)

    col = lambda c: pl.BlockSpec((L, W), lambda i: (0, c))
    return _call(body, name="conv_fwd", grid=(n,),
                 in_specs=[col(0), col(1), col(2), pl.BlockSpec((3, W), lambda i: (0, 0))],
                 out_specs=pl.BlockSpec((tr, W), lambda i: (i, 0)),
                 out_shape=_sds((L, W), BF16),
                 scratch_shapes=[pltpu.VMEM((L, W), F32)],
                 compiler_params=_params(("arbitrary",)))(p, p, p, conv_w)


def _conv_bwd(p, conv_w, dza):
    L = p.shape[0]
    tr = _row_tile(L)
    n = L // tr

    def body(bg_ref, cg_ref, v_ref, w_ref, dza_ref, dp_ref, dw_ref, u_scr, dy_scr):
        i = pl.program_id(0)

        @pl.when(i == 0)
        def _():
            u_scr[...] = cg_ref[...] * v_ref[...]
            dy_scr[...] = dza_ref[...] * bg_ref[...]
            dw_ref[...] = jnp.zeros_like(dw_ref)

        r0 = pl.multiple_of(i * tr, 8)
        w = w_ref[...]
        ext = jnp.concatenate([_rows_before(u_scr, i, tr, 8), u_scr[pl.ds(r0, tr), :]], axis=0)
        u2 = pltpu.roll(ext, 2, axis=0)[8:, :]
        u1 = pltpu.roll(ext, 1, axis=0)[8:, :]
        u0 = ext[8:, :]
        y = w[0:1] * u2 + w[1:2] * u1 + w[2:3] * u0
        dy = dy_scr[pl.ds(r0, tr), :]
        dext = jnp.concatenate([dy, _rows_after(dy_scr, i, n, tr, 8)], axis=0)
        m = tr + 8
        du = (w[2:3] * dext + w[1:2] * pltpu.roll(dext, m - 1, axis=0) + w[0:1] * pltpu.roll(dext, m - 2, axis=0))[:tr, :]
        dp_ref[:, 0:W] = (dza_ref[pl.ds(r0, tr), :] * y).astype(BF16)
        dp_ref[:, W:2 * W] = (du * v_ref[pl.ds(r0, tr), :]).astype(BF16)
        dp_ref[:, 2 * W:3 * W] = (du * cg_ref[pl.ds(r0, tr), :]).astype(BF16)
        dw_ref[...] += jnp.concatenate([jnp.sum(dy * u2, axis=0, keepdims=True),
                                        jnp.sum(dy * u1, axis=0, keepdims=True),
                                        jnp.sum(dy * u0, axis=0, keepdims=True)], axis=0)

    col = lambda c: pl.BlockSpec((L, W), lambda i: (0, c))
    return _call(body, name="conv_bwd", grid=(n,),
                 in_specs=[col(0), col(1), col(2), pl.BlockSpec((3, W), lambda i: (0, 0)),
                           pl.BlockSpec((L, W), lambda i: (0, 0))],
                 out_specs=[pl.BlockSpec((tr, 3 * W), lambda i: (i, 0)), pl.BlockSpec((3, W), lambda i: (0, 0))],
                 out_shape=[_sds((L, 3 * W), BF16), _sds((3, W), F32)],
                 scratch_shapes=[pltpu.VMEM((L, W), F32), pltpu.VMEM((L, W), F32)],
                 compiler_params=_params(("arbitrary",)))(p, p, p, conv_w, dza)


def _pool_windows(lane):
    wins = jnp.zeros(lane.shape, jnp.int32)
    for gi, w in enumerate(POOL_WINDOWS):
        wins = jnp.where(lane // (W // len(POOL_WINDOWS)) == gi, w, wins)
    return wins


def _pooled_block(u_ref, i, tr):
    r0 = pl.multiple_of(i * tr, 8)
    cur = u_ref[pl.ds(r0, tr), :]
    ext = jnp.concatenate([_rows_before(u_ref, i, tr, 16), cur], axis=0)
    s2 = ext + pltpu.roll(ext, 1, axis=0)
    s4 = s2 + pltpu.roll(s2, 2, axis=0)
    s8 = s4 + pltpu.roll(s4, 4, axis=0)
    s16 = s8 + pltpu.roll(s8, 8, axis=0)
    lane = lax.broadcasted_iota(jnp.int32, (tr, W), 1)
    wins = _pool_windows(lane)
    win_sum = jnp.where(wins == 2, s2[16:], jnp.where(wins == 4, s4[16:], jnp.where(wins == 8, s8[16:], s16[16:])))
    t = lax.broadcasted_iota(jnp.int32, (tr, W), 0) + i * tr
    cnt = jnp.minimum(t + 1, wins).astype(F32)
    return win_sum / cnt - cur, cnt


def _pool_fwd(p, w_pool_bd, pool_scale):
    L = p.shape[0]
    tr = _row_tile(L)

    def body(u_ref, w_ref, sc_ref, zc_ref):
        pooled, _ = _pooled_block(u_ref, pl.program_id(0), tr)
        zc_ref[...] = (_bdot(pooled, w_ref[...]) * sc_ref[...]).astype(BF16)

    return _call(body, name="pool_fwd", grid=(L // tr,),
                 in_specs=[pl.BlockSpec((L, W), lambda i: (0, 4)), pl.BlockSpec((W, W), lambda i: (0, 0)),
                           pl.BlockSpec((1, W), lambda i: (0, 0))],
                 out_specs=pl.BlockSpec((tr, W), lambda i: (i, 0)), out_shape=_sds((L, W), BF16),
                 compiler_params=_params(("arbitrary",)))(p, w_pool_bd, pool_scale)


def _pool_bwd(p, w_pool_bd, pool_scale, dzc):
    L = p.shape[0]
    tr = _row_tile(L)
    n = L // tr

    def body(u_ref, w_ref, sc_ref, dzc_ref, du_ref, dw_ref, dsc_ref, g_scr):
        i = pl.program_id(0)

        @pl.when(i == 0)
        def _():
            dw_ref[...] = jnp.zeros_like(dw_ref)
            dsc_ref[...] = jnp.zeros_like(dsc_ref)

            def rows(k, carry):
                r = pl.multiple_of(k * tr, 8)
                dmix = (dzc_ref[pl.ds(r, tr), :] * sc_ref[...]).astype(BF16)
                dpool = _dot(dmix, w_ref[...].astype(BF16), 1, 1)
                lane = lax.broadcasted_iota(jnp.int32, (tr, W), 1)
                t = lax.broadcasted_iota(jnp.int32, (tr, W), 0) + k * tr
                cnt = jnp.minimum(t + 1, _pool_windows(lane)).astype(F32)
                g_scr[pl.ds(r, tr), :] = dpool / cnt
                return carry

            lax.fori_loop(0, n, rows, 0)

        r0 = pl.multiple_of(i * tr, 8)
        pooled, cnt = _pooled_block(u_ref, i, tr)
        dzc = dzc_ref[pl.ds(r0, tr), :]
        mixed = _bdot(pooled, w_ref[...])
        dsc_ref[...] += jnp.sum(dzc * mixed, axis=0, keepdims=True)
        dmix = (dzc * sc_ref[...]).astype(BF16)
        dw_ref[...] += _dot(pooled.astype(BF16), dmix, 0, 0)
        gcur = g_scr[pl.ds(r0, tr), :]
        ext = jnp.concatenate([gcur, _rows_after(g_scr, i, n, tr, 16)], axis=0)
        m = tr + 16
        s2 = ext + pltpu.roll(ext, m - 1, axis=0)
        s4 = s2 + pltpu.roll(s2, m - 2, axis=0)
        s8 = s4 + pltpu.roll(s4, m - 4, axis=0)
        s16 = s8 + pltpu.roll(s8, m - 8, axis=0)
        lane = lax.broadcasted_iota(jnp.int32, (tr, W), 1)
        wins = _pool_windows(lane)
        ahead = jnp.where(wins == 2, s2[:tr], jnp.where(wins == 4, s4[:tr], jnp.where(wins == 8, s8[:tr], s16[:tr])))
        du_ref[...] = (ahead - gcur * cnt).astype(BF16)

    return _call(body, name="pool_bwd", grid=(n,),
                 in_specs=[pl.BlockSpec((L, W), lambda i: (0, 4)), pl.BlockSpec((W, W), lambda i: (0, 0)),
                           pl.BlockSpec((1, W), lambda i: (0, 0)), pl.BlockSpec((L, W), lambda i: (0, 0))],
                 out_specs=[pl.BlockSpec((tr, W), lambda i: (i, 0)), pl.BlockSpec((W, W), lambda i: (0, 0)),
                            pl.BlockSpec((1, W), lambda i: (0, 0))],
                 out_shape=[_sds((L, W), BF16), _sds((W, W), F32), _sds((1, W), F32)],
                 scratch_shapes=[pltpu.VMEM((L, W), F32)],
                 compiler_params=_params(("arbitrary",)))(p, w_pool_bd, pool_scale, dzc)


SSM_SLAB = 512


def _ssm_prep(lam_re, lam_im, log_dt_x):
    def body(lr_ref, li_ref, ldt_ref, abr_ref, abi_ref, fr_ref, fi_ref):
        lr = jnp.minimum(lr_ref[...], LAMBDA_RE_MAX)
        li = li_ref[...]
        dt = jnp.exp(ldt_ref[...])
        mag = jnp.exp(lr * dt)
        abr = mag * jnp.cos(li * dt)
        abi = mag * jnp.sin(li * dt)
        den = lr * lr + li * li
        nr = abr - 1.0
        abr_ref[...] = abr
        abi_ref[...] = abi
        fr_ref[...] = (nr * lr + abi * li) / den
        fi_ref[...] = (abi * lr - nr * li) / den

    shp = _sds(lam_re.shape, F32)
    return _call(body, name="ssm_prep", out_shape=[shp, shp, shp, shp], compiler_params=_params())(lam_re, lam_im, log_dt_x)


def _ssm_prep_bwd(lam_re, lam_im, log_dt_x, g_abr, g_abi, g_fr, g_fi, group_sum):
    def body(lr_ref, li_ref, ldt_ref, gar_ref, gai_ref, gfr_ref, gfi_ref, gs_ref, dlr_ref, dli_ref, dldt_ref):
        lam = lr_ref[...]
        lr = jnp.minimum(lam, LAMBDA_RE_MAX)
        li = li_ref[...]
        dt = jnp.exp(ldt_ref[...])
        mag = jnp.exp(lr * dt)
        abr = mag * jnp.cos(li * dt)
        abi = mag * jnp.sin(li * dt)
        den = lr * lr + li * li
        nr = abr - 1.0
        fr = (nr * lr + abi * li) / den
        fi = (abi * lr - nr * li) / den
        d_nre = gfr_ref[...] / den
        d_nim = gfi_ref[...] / den
        d_den = -(gfr_ref[...] * fr + gfi_ref[...] * fi) / den
        d_abr = gar_ref[...] + d_nre * lr - d_nim * li
        d_abi = gai_ref[...] + d_nre * li + d_nim * lr
        d_lr = d_nre * nr + d_nim * abi + d_den * 2.0 * lr
        d_li = d_nre * abi - d_nim * nr + d_den * 2.0 * li
        d_mag = d_abr * jnp.cos(li * dt) + d_abi * jnp.sin(li * dt)
        d_th = -d_abr * abi + d_abi * abr
        d_lr = d_lr + d_mag * mag * dt
        d_li = d_li + d_th * dt
        d_dt = d_mag * mag * lr + d_th * li
        passes = jnp.where(lam < LAMBDA_RE_MAX, 1.0, jnp.where(lam == LAMBDA_RE_MAX, 0.5, 0.0))
        dlr_ref[...] = d_lr * passes
        dli_ref[...] = d_li
        dldt_ref[...] = _dot(d_dt * dt, gs_ref[...], precision=HIGH)

    shp = _sds(lam_re.shape, F32)
    return _call(body, name="ssm_prep_bwd", out_shape=[shp, shp, _sds((lam_re.shape[0], 128), F32)],
                 compiler_params=_params())(lam_re, lam_im, log_dt_x, g_abr, g_abi, g_fr, g_fi, group_sum)


def _cmul(ar, ai, br, bi):
    return ar * br - ai * bi, ar * bi + ai * br


def _powers(ar, ai):
    out = [(ar, ai)]
    for _ in range(7):
        out.append(_cmul(out[-1][0], out[-1][1], ar, ai))
    return out


def _scan_rows(s_re, s_im, ar, ai, L, reverse=False, visit=None, visit_init=None):
    n = s_re.shape[1]
    pw = _powers(ar, ai)
    row = lax.broadcasted_iota(jnp.int32, (8, n), 0)
    dist = (8 - row) if reverse else (row + 1)
    pr = jnp.zeros((8, n), F32)
    pi = jnp.zeros((8, n), F32)
    for k in range(8):
        pr = jnp.where(dist == k + 1, pw[k][0], pr)
        pi = jnp.where(dist == k + 1, pw[k][1], pi)
    nb = L // 8

    def blk(t, carry):
        cr, ci, acc = carry
        b = (nb - 1 - t) if reverse else t
        r0 = pl.multiple_of(b * 8, 8)
        xr = s_re[pl.ds(r0, 8), :]
        xi = s_im[pl.ds(r0, 8), :]
        for d in (1, 2, 4):
            if reverse:
                keep = row < 8 - d
                sr, si = pltpu.roll(xr, 8 - d, axis=0), pltpu.roll(xi, 8 - d, axis=0)
            else:
                keep = row >= d
                sr, si = pltpu.roll(xr, d, axis=0), pltpu.roll(xi, d, axis=0)
            sr = jnp.where(keep, sr, 0.0)
            si = jnp.where(keep, si, 0.0)
            mr, mi = _cmul(pw[d - 1][0], pw[d - 1][1], sr, si)
            xr, xi = xr + mr, xi + mi
        mr, mi = _cmul(pr, pi, cr, ci)
        xr, xi = xr + mr, xi + mi
        s_re[pl.ds(r0, 8), :] = xr
        s_im[pl.ds(r0, 8), :] = xi
        if visit is not None:
            acc = visit(b, xr, xi, acc)
        if reverse:
            return xr[0:1, :], xi[0:1, :], acc
        return xr[7:8, :], xi[7:8, :], acc

    zero = jnp.zeros((1, n), F32)
    return lax.fori_loop(0, nb, blk, (zero, zero, visit_init if visit is not None else 0))[2]


def _ssm_project(u_ref, wbr, wbi, s_re, s_im, L):
    ch = min(L, 256)

    def rows(k, carry):
        r = pl.multiple_of(k * ch, 8)
        ub = u_ref[pl.ds(r, ch), :].astype(BF16)
        s_re[pl.ds(r, ch), :] = _dot(ub, wbr)
        s_im[pl.ds(r, ch), :] = _dot(ub, wbi)
        return carry

    lax.fori_loop(0, L // ch, rows, 0)


def _gelu(y):
    c = math.sqrt(2.0 / math.pi)
    return 0.5 * y * (1.0 + jnp.tanh(c * (y + 0.044715 * y * y * y)))


def _gelu_grad(y):
    c = math.sqrt(2.0 / math.pi)
    th = jnp.tanh(c * (y + 0.044715 * y * y * y))
    return 0.5 * (1.0 + th) + 0.5 * y * (1.0 - th * th) * c * (1.0 + 3.0 * 0.044715 * y * y)


def _ssm_fwd(p, b_re_bd, b_im_bd, c_re_bd, c_im_bd, abr, abi, fr, fi, d_skip):
    L = p.shape[0]
    ns = NST // SSM_SLAB
    ch = min(L, 256)

    def body(u_ref, br_ref, bi_ref, cr_ref, ci_ref, abr_ref, abi_ref, fr_ref, fi_ref, d_ref,
             y_ref, zb_ref, s_re, s_im):
        j = pl.program_id(0)
        f_re, f_im = fr_ref[...], fi_ref[...]
        wbr = (f_re * br_ref[...] - f_im * bi_ref[...]).astype(BF16)
        wbi = (f_re * bi_ref[...] + f_im * br_ref[...]).astype(BF16)
        _ssm_project(u_ref, wbr, wbi, s_re, s_im, L)
        _scan_rows(s_re, s_im, abr_ref[...], abi_ref[...], L)
        crb = cr_ref[...].astype(BF16)
        cib = ci_ref[...].astype(BF16)

        def rows(k, carry):
            r = pl.multiple_of(k * ch, 8)
            part = _dot(s_re[pl.ds(r, ch), :].astype(BF16), crb) - _dot(s_im[pl.ds(r, ch), :].astype(BF16), cib)

            @pl.when(j == 0)
            def _():
                y_ref[pl.ds(r, ch), :] = part + d_ref[...] * u_ref[pl.ds(r, ch), :]

            @pl.when(j > 0)
            def _():
                y_ref[pl.ds(r, ch), :] += part

            @pl.when(j == ns - 1)
            def _():
                zb_ref[pl.ds(r, ch), :] = _gelu(y_ref[pl.ds(r, ch), :]).astype(BF16)

            return carry

        lax.fori_loop(0, L // ch, rows, 0)

    full = lambda shape: pl.BlockSpec(shape, lambda j: (0, 0))
    lanes = pl.BlockSpec((1, SSM_SLAB), lambda j: (0, j))
    return _call(body, name="ssm_fwd", grid=(ns,),
                 in_specs=[pl.BlockSpec((L, W), lambda j: (0, 3)),
                           pl.BlockSpec((W, SSM_SLAB), lambda j: (0, j)), pl.BlockSpec((W, SSM_SLAB), lambda j: (0, j)),
                           pl.BlockSpec((SSM_SLAB, W), lambda j: (j, 0)), pl.BlockSpec((SSM_SLAB, W), lambda j: (j, 0)),
                           lanes, lanes, lanes, lanes, full((1, W))],
                 out_specs=[full((L, W)), full((L, W))],
                 out_shape=[_sds((L, W), F32), _sds((L, W), BF16)],
                 scratch_shapes=[pltpu.VMEM((L, SSM_SLAB), F32), pltpu.VMEM((L, SSM_SLAB), F32)],
                 compiler_params=_params(("arbitrary",)))(p, b_re_bd, b_im_bd, c_re_bd, c_im_bd, abr, abi, fr, fi, d_skip)


def _ssm_bwd(p, y, dzb, b_re_bd, b_im_bd, c_re_bd, c_im_bd, abr, abi, fr, fi, d_skip):
    L = p.shape[0]
    ns = NST // SSM_SLAB
    ch = min(L, 256)
    n_ch = L // ch

    def body(u_ref, y_ref, dzb_ref, br_ref, bi_ref, cr_ref, ci_ref, abr_ref, abi_ref, fr_ref, fi_ref, d_ref,
             du_ref, dd_ref, dbr_ref, dbi_ref, dcr_ref, dci_ref, gar_ref, gai_ref, gfr_ref, gfi_ref,
             s_re, s_im, l_re, l_im, dy_scr, du_scr):
        j = pl.program_id(0)
        f_re, f_im = fr_ref[...], fi_ref[...]
        b_re, b_im = br_ref[...], bi_ref[...]
        wbr = (f_re * b_re - f_im * b_im).astype(BF16)
        wbi = (f_re * b_im + f_im * b_re).astype(BF16)
        a_re, a_im = abr_ref[...], abi_ref[...]

        @pl.when(j == 0)
        def _():
            def rows(k, acc):
                r = pl.multiple_of(k * ch, 8)
                dy = dzb_ref[pl.ds(r, ch), :] * _gelu_grad(y_ref[pl.ds(r, ch), :])
                dy_scr[pl.ds(r, ch), :] = dy
                du_scr[pl.ds(r, ch), :] = d_ref[...] * dy
                return acc + jnp.sum(dy * u_ref[pl.ds(r, ch), :], axis=0, keepdims=True)

            dd_ref[...] = lax.fori_loop(0, n_ch, rows, jnp.zeros((1, W), F32))

        _ssm_project(u_ref, wbr, wbi, s_re, s_im, L)
        _scan_rows(s_re, s_im, a_re, a_im, L)
        crb = cr_ref[...].astype(BF16)
        cib = ci_ref[...].astype(BF16)

        def rows_c(k, acc):
            dcr, dci = acc
            r = pl.multiple_of(k * ch, 8)
            dyb = dy_scr[pl.ds(r, ch), :].astype(BF16)
            dcr = dcr + _dot(s_re[pl.ds(r, ch), :].astype(BF16), dyb, 0, 0)
            dci = dci - _dot(s_im[pl.ds(r, ch), :].astype(BF16), dyb, 0, 0)
            l_re[pl.ds(r, ch), :] = _dot(dyb, crb, 1, 1)
            l_im[pl.ds(r, ch), :] = -_dot(dyb, cib, 1, 1)
            return dcr, dci

        zc = jnp.zeros((SSM_SLAB, W), F32)
        dcr, dci = lax.fori_loop(0, n_ch, rows_c, (zc, zc))
        dcr_ref[...] = dcr
        dci_ref[...] = dci

        row8 = lax.broadcasted_iota(jnp.int32, (8, SSM_SLAB), 0)

        def visit(b, lr, li, acc):
            ar_acc, ai_acc = acc
            r0 = pl.multiple_of(b * 8, 8)
            rp = pl.multiple_of(jnp.maximum(b * 8 - 8, 0), 8)
            has_prev = b > 0
            pr = jnp.where(has_prev, s_re[pl.ds(rp, 8), :][7:8, :], 0.0)
            pi = jnp.where(has_prev, s_im[pl.ds(rp, 8), :][7:8, :], 0.0)
            sr = jnp.where(row8 >= 1, pltpu.roll(s_re[pl.ds(r0, 8), :], 1, axis=0), pr)
            si = jnp.where(row8 >= 1, pltpu.roll(s_im[pl.ds(r0, 8), :], 1, axis=0), pi)
            return ar_acc + lr * sr + li * si, ai_acc - lr * si + li * sr

        z8 = jnp.zeros((8, SSM_SLAB), F32)
        ar_acc, ai_acc = _scan_rows(l_re, l_im, a_re, -a_im, L, reverse=True, visit=visit, visit_init=(z8, z8))
        gar_ref[...] = jnp.sum(ar_acc, axis=0, keepdims=True)
        gai_ref[...] = jnp.sum(ai_acc, axis=0, keepdims=True)

        def rows_b(k, acc):
            dwr, dwi = acc
            r = pl.multiple_of(k * ch, 8)
            ub = u_ref[pl.ds(r, ch), :].astype(BF16)
            lrb = l_re[pl.ds(r, ch), :].astype(BF16)
            lib = l_im[pl.ds(r, ch), :].astype(BF16)
            du_scr[pl.ds(r, ch), :] += _dot(lrb, wbr, 1, 1) + _dot(lib, wbi, 1, 1)
            return dwr + _dot(ub, lrb, 0, 0), dwi + _dot(ub, lib, 0, 0)

        zb = jnp.zeros((W, SSM_SLAB), F32)
        dwr, dwi = lax.fori_loop(0, n_ch, rows_b, (zb, zb))
        dbr_ref[...] = dwr * f_re + dwi * f_im
        dbi_ref[...] = -dwr * f_im + dwi * f_re
        gfr_ref[...] = jnp.sum(dwr * b_re + dwi * b_im, axis=0, keepdims=True)
        gfi_ref[...] = jnp.sum(-dwr * b_im + dwi * b_re, axis=0, keepdims=True)

        @pl.when(j == ns - 1)
        def _():
            du_ref[...] = du_scr[...].astype(BF16)

    full = lambda shape: pl.BlockSpec(shape, lambda j: (0, 0))
    lanes = pl.BlockSpec((1, SSM_SLAB), lambda j: (0, j))
    bspec = pl.BlockSpec((W, SSM_SLAB), lambda j: (0, j))
    cspec = pl.BlockSpec((SSM_SLAB, W), lambda j: (j, 0))
    slab = lambda: pltpu.VMEM((L, SSM_SLAB), F32)
    return _call(body, name="ssm_bwd", grid=(ns,),
                 in_specs=[pl.BlockSpec((L, W), lambda j: (0, 3)), full((L, W)), full((L, W)),
                           bspec, bspec, cspec, cspec, lanes, lanes, lanes, lanes, full((1, W))],
                 out_specs=[full((L, W)), full((1, W)), bspec, bspec, cspec, cspec, lanes, lanes, lanes, lanes],
                 out_shape=[_sds((L, W), BF16), _sds((1, W), F32), _sds((W, NST), F32), _sds((W, NST), F32),
                            _sds((NST, W), F32), _sds((NST, W), F32)] + [_sds((1, NST), F32)] * 4,
                 scratch_shapes=[slab(), slab(), slab(), slab(), pltpu.VMEM((L, W), F32), pltpu.VMEM((L, W), F32)],
                 compiler_params=_params(("arbitrary",)))(p, y, dzb, b_re_bd, b_im_bd, c_re_bd, c_im_bd,
                                                          abr, abi, fr, fi, d_skip)


SB_KB = 512


SB_SUB = 256


def _split2(x):
    hi = x.astype(BF16)
    return hi, (x - hi.astype(F32)).astype(BF16)


def _ones_dot(x, ones):
    n = x.shape[0]
    r = _dot(jnp.concatenate(_split2(x), axis=0), ones)
    return r[:n] + r[n:]


def _suffix_sums(x, tri):
    sub = tri.shape[0]
    parts = [_ones_dot(x[:, i:i + sub], tri) for i in range(0, x.shape[1], sub)]
    out, after = [], None
    for p in reversed(parts):
        out.append(p if after is None else p + after)
        after = p[:, 0:1] if after is None else after + p[:, 0:1]
    return jnp.concatenate(out[::-1], axis=1)


def _prefix_sums_exclusive(x, tri_le):
    sub = tri_le.shape[0]
    out, before = [], None
    for i in range(0, x.shape[1], sub):
        xi = x[:, i:i + sub]
        inc = _ones_dot(xi, tri_le)
        out.append(inc - xi if before is None else inc - xi + before)
        before = inc[:, sub - 1:sub] if before is None else before + inc[:, sub - 1:sub]
    return jnp.concatenate(out, axis=1)


def _sb_block(q, kj, i, jb, kb, right, tri):
    z = _bdot(q, kj, 1, 1)
    t_idx = lax.broadcasted_iota(jnp.int32, (QB, kb), 0) + i * QB
    s_idx = lax.broadcasted_iota(jnp.int32, (QB, kb), 1) + jb * kb
    mask = s_idx < t_idx
    lk_all = jnp.minimum(-z, 0.0) - jnp.log1p(jnp.exp(-jnp.abs(z)))
    lk = jnp.where(mask, lk_all, 0.0)
    suf = _suffix_sums(lk, tri)
    a = jnp.where(mask, jnp.exp((lk_all + z) + (suf - lk) + right), 0.0)
    return z, mask, suf, a


def _sb_ones(kb):
    sub = min(SB_SUB, kb)
    r = lax.broadcasted_iota(jnp.int32, (sub, sub), 0)
    c = lax.broadcasted_iota(jnp.int32, (sub, sub), 1)
    return (r >= c).astype(BF16), (r <= c).astype(BF16)


def _sb_fwd(q, k, v):
    L = q.shape[1]
    kb = min(SB_KB, L)
    per = kb // QB

    def body(q_ref, k_ref, v_ref, o_ref, rs_ref):
        i = pl.program_id(0)
        tri, _ = _sb_ones(kb)
        lane = lax.broadcasted_iota(jnp.int32, (QB, 128), 1)
        qs = [q_ref[h] for h in range(HEADS)]

        def step(t, carry):
            accs, rights, sums = carry
            jb = i // per - t
            r = pl.multiple_of(jb * kb, kb)
            out = []
            for h in range(HEADS):
                _, _, suf, a = _sb_block(qs[h], k_ref[h, pl.ds(r, kb), :], i, jb, kb, rights[h], tri)
                tot = suf[:, 0:1]
                out.append((accs[h] + _bdot(a, v_ref[h, pl.ds(r, kb), :]), rights[h] + tot,
                            sums[h] + jnp.where(lane == jb, tot, 0.0)))
            return tuple(o[0] for o in out), tuple(o[1] for o in out), tuple(o[2] for o in out)

        init = (tuple(jnp.zeros((QB, HD), F32) for _ in range(HEADS)), tuple(jnp.zeros((QB, 1), F32) for _ in range(HEADS)),
                tuple(jnp.zeros((QB, 128), F32) for _ in range(HEADS)))
        accs, _, sums = lax.fori_loop(0, i // per + 1, step, init)
        for h in range(HEADS):
            o_ref[h] = accs[h]
            rs_ref[h] = sums[h]

    heads = pl.BlockSpec((HEADS, L, HD), lambda i: (0, 0, 0))
    blk = pl.BlockSpec((HEADS, QB, HD), lambda i: (0, i, 0))
    return _call(body, name="sb_fwd", grid=(L // QB,), in_specs=[blk, heads, heads],
                 out_specs=[blk, pl.BlockSpec((HEADS, QB, 128), lambda i: (0, i, 0))],
                 out_shape=[_sds((HEADS, L, HD), F32), _sds((HEADS, L, 128), F32)],
                 compiler_params=_params(("parallel",)))(q, k, v)


def _sb_bwd(q, k, v, do, block_sums):
    L = q.shape[1]
    kb = min(SB_KB, L)
    per = kb // QB

    def body(q_ref, k_ref, v_ref, do_ref, rs_ref, dq_ref, dk_ref, dv_ref):
        i = pl.program_id(0)
        tri, tri_le = _sb_ones(kb)
        lane = lax.broadcasted_iota(jnp.int32, (QB, 128), 1)

        @pl.when(i == 0)
        def _():
            dk_ref[...] = jnp.zeros_like(dk_ref)
            dv_ref[...] = jnp.zeros_like(dv_ref)

        qs = [q_ref[h] for h in range(HEADS)]
        dos = [do_ref[h] for h in range(HEADS)]
        sums = [rs_ref[h] for h in range(HEADS)]

        def step(jb, carry):
            dqs, lefts = carry
            r = pl.multiple_of(jb * kb, kb)
            out = []
            for h in range(HEADS):
                kj = k_ref[h, pl.ds(r, kb), :]
                vj = v_ref[h, pl.ds(r, kb), :]
                right = jnp.sum(jnp.where(lane > jb, sums[h], 0.0), axis=1, keepdims=True)
                z, mask, _, a = _sb_block(qs[h], kj, i, jb, kb, right, tri)
                e = a * _bdot(dos[h], vj, 1, 1)
                dv_ref[h, pl.ds(r, kb), :] += _dot(a.astype(BF16), dos[h].astype(BF16), 0, 0)
                before = lefts[h] + _prefix_sums_exclusive(e, tri_le)
                sg = _sigmoid(z)
                dz = jnp.where(mask, e * (1.0 - sg) - sg * before, 0.0).astype(BF16)
                dk_ref[h, pl.ds(r, kb), :] += _dot(dz, qs[h].astype(BF16), 0, 0)
                out.append((dqs[h] + _dot(dz, kj.astype(BF16)), lefts[h] + jnp.sum(e, axis=1, keepdims=True)))
            return tuple(o[0] for o in out), tuple(o[1] for o in out)

        init = (tuple(jnp.zeros((QB, HD), F32) for _ in range(HEADS)), tuple(jnp.zeros((QB, 1), F32) for _ in range(HEADS)))
        dqs, _ = lax.fori_loop(0, i // per + 1, step, init)
        for h in range(HEADS):
            dq_ref[h] = dqs[h]

    heads = pl.BlockSpec((HEADS, L, HD), lambda i: (0, 0, 0))
    blk = pl.BlockSpec((HEADS, QB, HD), lambda i: (0, i, 0))
    shp = _sds((HEADS, L, HD), F32)
    return _call(body, name="sb_bwd", grid=(L // QB,),
                 in_specs=[blk, heads, heads, blk, pl.BlockSpec((HEADS, QB, 128), lambda i: (0, i, 0))],
                 out_specs=[blk, heads, heads], out_shape=[shp, shp, shp],
                 compiler_params=_params(("arbitrary",)))(q, k, v, do, block_sums)


def _merge_fwd(za, zb, zc, zd, p, w_conv_out, w_glu, w_pool_out, w_sb_out):
    L = za.shape[0]
    tm = _row_tile(L)

    def body(za_ref, zb_ref, zc_ref, zd_ref, g0, g1, g2, g3, wc_ref, wg_ref, wp_ref, ws_ref, o_ref):
        glu = _dot(zb_ref[...], wg_ref[...])
        yb = glu[:, :D] * _sigmoid(glu[:, D:])
        m = _sigmoid(g0[...]) * _dot(za_ref[...], wc_ref[...])
        m = m + _sigmoid(g1[...]) * yb
        m = m + _sigmoid(g2[...]) * _dot(zc_ref[...], wp_ref[...])
        m = m + _sigmoid(g3[...]) * _dot(zd_ref[...], ws_ref[...])
        o_ref[...] = m.astype(BF16)

    zt = pl.BlockSpec((tm, W), lambda i: (i, 0))
    gate = lambda b: pl.BlockSpec((tm, D), lambda i: (i, 2 + b))
    wfull = lambda n: pl.BlockSpec((W, n), lambda i: (0, 0))
    return _call(body, name="merge_fwd", grid=(L // tm,),
                 in_specs=[zt, zt, zt, zt, gate(0), gate(1), gate(2), gate(3), wfull(D), wfull(2 * D), wfull(D), wfull(D)],
                 out_specs=pl.BlockSpec((tm, D), lambda i: (i, 0)), out_shape=_sds((L, D), BF16),
                 compiler_params=_params(("parallel",)))(za, zb, zc, zd, p, p, p, p, w_conv_out, w_glu, w_pool_out, w_sb_out)


def _merge_bwd(dm, za, zb, zc, zd, p, w_conv_out, w_glu, w_pool_out, w_sb_out):
    L = za.shape[0]
    tm = _row_tile(L)
    n = L // tm

    def body(dm_ref, za_ref, zb_ref, zc_ref, zd_ref, g0, g1, g2, g3, wc_ref, wg_ref, wp_ref, ws_ref,
             dza_ref, dzb_ref, dzc_ref, dzd_ref, dg_ref, dwc_ref, dwg_ref, dwp_ref, dws_ref,
             awc, awg, awp, aws):
        i = pl.program_id(0)

        @pl.when(i == 0)
        def _():
            awc[...] = jnp.zeros_like(awc)
            awg[...] = jnp.zeros_like(awg)
            awp[...] = jnp.zeros_like(awp)
            aws[...] = jnp.zeros_like(aws)

        dmv = dm_ref[...]

        def gated(g_ref, y, col):
            s = _sigmoid(g_ref[...])
            dg_ref[:, col * D:(col + 1) * D] = (dmv * y * s * (1.0 - s)).astype(BF16)
            return (dmv * s)

        def linear(z_ref, w_ref, acc, dz_ref, col, g_ref):
            zv = z_ref[...]
            dy = gated(g_ref, _dot(zv, w_ref[...]), col).astype(BF16)
            dz_ref[...] = _dot(dy, w_ref[...], 1, 1)
            acc[...] += _dot(zv, dy, 0, 0)

        linear(za_ref, wc_ref, awc, dza_ref, 0, g0)
        linear(zc_ref, wp_ref, awp, dzc_ref, 2, g2)
        linear(zd_ref, ws_ref, aws, dzd_ref, 3, g3)
        zbv = zb_ref[...]
        glu = _dot(zbv, wg_ref[...])
        ga = glu[:, :D]
        sg = _sigmoid(glu[:, D:])
        dyb = gated(g1, ga * sg, 1)
        dga = (dyb * sg).astype(BF16)
        dgg = (dyb * ga * sg * (1.0 - sg)).astype(BF16)
        dzb_ref[...] = _dot(dga, wg_ref[:, :D], 1, 1) + _dot(dgg, wg_ref[:, D:], 1, 1)
        awg[:, :D] += _dot(zbv, dga, 0, 0)
        awg[:, D:] += _dot(zbv, dgg, 0, 0)

        @pl.when(i == n - 1)
        def _():
            dwc_ref[...] = awc[...].astype(BF16)
            dwg_ref[...] = awg[...].astype(BF16)
            dwp_ref[...] = awp[...].astype(BF16)
            dws_ref[...] = aws[...].astype(BF16)

    zt = pl.BlockSpec((tm, W), lambda i: (i, 0))
    gate = lambda b: pl.BlockSpec((tm, D), lambda i: (i, 2 + b))
    wfull = lambda n_: pl.BlockSpec((W, n_), lambda i: (0, 0))
    zs = _sds((L, W), F32)
    return _call(body, name="merge_bwd", grid=(n,),
                 in_specs=[pl.BlockSpec((tm, D), lambda i: (i, 0)), zt, zt, zt, zt, gate(0), gate(1), gate(2), gate(3),
                           wfull(D), wfull(2 * D), wfull(D), wfull(D)],
                 out_specs=[zt, zt, zt, zt, pl.BlockSpec((tm, 4 * D), lambda i: (i, 0)),
                            wfull(D), wfull(2 * D), wfull(D), wfull(D)],
                 out_shape=[zs, zs, zs, zs, _sds((L, 4 * D), BF16),
                            _sds((W, D), BF16), _sds((W, 2 * D), BF16), _sds((W, D), BF16), _sds((W, D), BF16)],
                 scratch_shapes=[pltpu.VMEM((W, D), F32), pltpu.VMEM((W, 2 * D), F32), pltpu.VMEM((W, D), F32),
                                 pltpu.VMEM((W, D), F32)],
                 compiler_params=_params(("arbitrary",)))(dm, za, zb, zc, zd, p, p, p, p,
                                                          w_conv_out, w_glu, w_pool_out, w_sb_out)


def _adam_math(w, g, m, v):
    m2 = ADAM_B1 * m + (1.0 - ADAM_B1) * g
    v2 = ADAM_B2 * v + (1.0 - ADAM_B2) * (g * g)
    m_hat = m2 / (1.0 - ADAM_B1 ** ADAM_STEP)
    v_hat = v2 / (1.0 - ADAM_B2 ** ADAM_STEP)
    return -ADAM_LR * (m_hat / (jnp.sqrt(v_hat) + ADAM_EPS) + ADAM_WD * w), m2, v2


def _as_rows(a):
    return a.reshape(-1, a.shape[-1])


def _adamw(w, g, m, v):
    shape = w.shape
    w2, g2, m2, v2 = _as_rows(w), _as_rows(g), _as_rows(m), _as_rows(v)
    R, C = w2.shape
    tr = R
    for cand in (1024, 512, 256, 128, 64, 32, 16, 8):
        if R % cand == 0 and cand * C * 4 <= 2 * 1024 * 1024:
            tr = cand
            break

    def body(w_ref, g_ref, m_ref, v_ref, d_ref, m_out, v_out):
        d, mn, vn = _adam_math(w_ref[...], g_ref[...], m_ref[...], v_ref[...])
        d_ref[...] = d
        m_out[...] = mn
        v_out[...] = vn

    blk = pl.BlockSpec((tr, C), lambda i: (i, 0))
    shp = _sds((R, C), F32)
    outs = _call(body, name="adamw", grid=(R // tr,), in_specs=[blk] * 4, out_specs=[blk] * 3, out_shape=[shp] * 3,
                 compiler_params=_params(("parallel",)))(w2, g2, m2, v2)
    return tuple(o.reshape(shape) for o in outs)


def _sum_parts(parts, out_dtype, name):
    shape = parts[0].shape
    flat = [_as_rows(a) for a in parts]
    R, C = flat[0].shape
    tr = R
    for cand in (1024, 512, 256, 128, 64, 32, 16):
        if R % cand == 0 and cand * C * 4 <= 2 * 1024 * 1024:
            tr = cand
            break
    k = len(parts)

    def body(*refs):
        acc = refs[0][...].astype(F32)
        for r in refs[1:k]:
            acc = acc + r[...].astype(F32)
        refs[k][...] = acc.astype(out_dtype)

    blk = pl.BlockSpec((tr, C), lambda i: (i, 0))
    out = _call(body, name=name, grid=(R // tr,), in_specs=[blk] * k, out_specs=blk, out_shape=_sds((R, C), out_dtype),
                compiler_params=_params(("parallel",)))(*flat)
    return out.reshape(shape)


ADA_SHARD = 9 * D // N_CHIP
ADA_TN = 768


def _ada_fwd(c_pad, w_ada, b_ada_cols):
    depth = w_ada.shape[0]

    def body(c_ref, w_ref, b_ref, o_ref):
        cv = c_ref[...]
        o_ref[...] = _bdot(cv * _sigmoid(cv), w_ref[...]) + b_ref[...]

    return _call(body, name="ada_fwd", grid=(depth, ADA_SHARD // ADA_TN),
                 in_specs=[pl.BlockSpec((16, D), lambda l, j: (0, 0)),
                           pl.BlockSpec((None, D, ADA_TN), lambda l, j: (l, 0, j)),
                           pl.BlockSpec((None, 1, ADA_TN), lambda l, j: (l, 0, j))],
                 out_specs=pl.BlockSpec((None, 16, ADA_TN), lambda l, j: (l, 0, j)),
                 out_shape=_sds((depth, 16, ADA_SHARD), F32),
                 compiler_params=_params(("parallel", "parallel")))(c_pad, w_ada, b_ada_cols)


def _ada_wgrad(c_pad, d_ada):
    depth = d_ada.shape[0]

    def body(c_ref, d_ref, o_ref):
        cv = c_ref[...]
        o_ref[...] = _bdot(cv * _sigmoid(cv), d_ref[...], 0, 0)

    return _call(body, name="ada_wgrad", grid=(depth, ADA_SHARD // ADA_TN),
                 in_specs=[pl.BlockSpec((16, D), lambda l, j: (0, 0)),
                           pl.BlockSpec((None, 16, ADA_TN), lambda l, j: (l, 0, j))],
                 out_specs=pl.BlockSpec((None, D, ADA_TN), lambda l, j: (l, 0, j)),
                 out_shape=_sds((depth, D, ADA_SHARD), F32),
                 compiler_params=_params(("parallel", "parallel")))(c_pad, d_ada)


HBM_SPEC = pl.BlockSpec(memory_space=pltpu.HBM)


def _place():
    x, y, c = lax.axis_index("x"), lax.axis_index("y"), lax.axis_index("c")
    peers = [(1 - x, y), (x, 1 - y), (1 - x, 1 - y)]
    return x, y, c, peers


def _chip(px, py):
    return 2 * px + py


def _allgather8(block, name):
    m_per, n = block.shape

    def body(x_ref, out_ref, send_sems, recv_sems, local_sem):
        x, y, c, chips = _place()
        me, sibling = (x, y, c), (x, y, 1 - c)

        def rows(px, py, pc):
            return out_ref.at[pl.ds(pl.multiple_of((4 * px + 2 * py + pc) * m_per, 8), m_per), :]

        def copy(k, blk, to, src=None):
            return pltpu.make_async_remote_copy(
                src_ref=rows(*blk) if src is None else src, dst_ref=rows(*blk),
                send_sem=send_sems.at[k], recv_sem=recv_sems.at[k], device_id=to, device_id_type=MESH)

        mine = pltpu.make_async_copy(x_ref, rows(*me), local_sem)
        mine.start()
        first = [copy(0, me, sibling, src=x_ref)]
        first += [copy(1 + j, me, (*chip, c), src=x_ref) for j, chip in enumerate(chips)]
        for cp in first:
            cp.start()
        passed = [copy(4 + j, (*chip, c), sibling) for j, chip in enumerate(chips)]
        for j, chip in enumerate(chips):
            copy(1 + j, (*chip, c), me).wait_recv()
            passed[j].start()
        copy(0, sibling, me).wait_recv()
        for j, chip in enumerate(chips):
            copy(4 + j, (*chip, 1 - c), me).wait_recv()
        for cp in first + passed:
            cp.wait_send()
        mine.wait()

    return _call(body, name=name, out_shape=_sds((N_DEV * m_per, n), block.dtype),
                 in_specs=[pl.BlockSpec(memory_space=pltpu.VMEM)], out_specs=pl.BlockSpec(memory_space=pltpu.VMEM),
                 scratch_shapes=[pltpu.SemaphoreType.DMA((7,)), pltpu.SemaphoreType.DMA((7,)), pltpu.SemaphoreType.DMA],
                 compiler_params=_params())(block)


GATHERED = (("w_ff_in", 0, -1, 0), ("w_ff_out", 0, -2, 0),
            ("w_in", None, -1, 1), ("w_conv_out", None, -1, 1), ("w_glu", None, -1, 1), ("w_pool_out", None, -1, 1),
            ("w_sb_out", None, -1, 1), ("w_out", None, -2, 1),
            ("w_ff_in", 1, -1, 2), ("w_ff_out", 1, -2, 2))
N_SUB = 3


def _lead(ref):
    return (slice(None),) * (len(ref.shape) - 2)


def _mo(v, m):
    return v if isinstance(v, int) else pl.multiple_of(v, m)


def _full_region(ref, axis, j, half, shard_shape):
    rs, cs = shard_shape[-2], shard_shape[-1]
    if axis == -1:
        r0, nr = (0, rs) if half is None else (half * (rs // 2), rs // 2)
        return ref.at[_lead(ref) + (pl.ds(_mo(r0, 16), nr), pl.ds(_mo(j * cs, 128), cs))]
    r0, nr = (j * rs, rs) if half is None else (j * rs + half * (rs // 2), rs // 2)
    return ref.at[_lead(ref) + (pl.ds(_mo(r0, 16), nr), slice(None))]


def _shard_half(ref, half):
    rs = ref.shape[-2]
    return ref.at[_lead(ref) + (pl.ds(_mo(half * (rs // 2), 16), rs // 2), slice(None))]


def _full_shape(shard_shape, axis):
    s = list(shard_shape)
    s[axis] *= N_CHIP
    return tuple(s)


class _Lay:
    def __init__(self, shard_shape, axis):
        self.axis = axis
        self.shard_shape = tuple(shard_shape)
        self.full_shape = _full_shape(shard_shape, axis)
        self.lead = int(np.prod(shard_shape[:-2]))
        self.rs, self.cs = shard_shape[-2], shard_shape[-1]
        self.hr = self.rs // 2
        self.tr = next(t for t in (512, 256, 128, 64, 32, 16) if self.hr % t == 0 and t * self.cs * 4 <= (1 << 21))
        self.half_rows_shape = _half_rows_shape(self.full_shape)
        self.half_shard_shape = _half_rows_shape(self.shard_shape)

    def full(self, jf, hf):
        if self.axis == -1:
            return ((self.lead, 2, self.hr, N_CHIP * self.cs),
                    pl.BlockSpec((None, None, self.tr, self.cs), lambda b, j, i, s: (b, hf(j, s), i, jf(j, s))))
        return ((self.lead, N_CHIP, 2, self.hr, self.cs),
                pl.BlockSpec((None, None, None, self.tr, self.cs), lambda b, j, i, s: (b, jf(j, s), hf(j, s), i, 0)))

    def half_shard(self):
        return (self.lead, self.hr, self.cs), pl.BlockSpec((None, self.tr, self.cs), lambda b, j, i, s: (b, i, 0))

    def shard(self, hf):
        return ((self.lead, 2, self.hr, self.cs),
                pl.BlockSpec((None, None, self.tr, self.cs), lambda b, j, i, s: (b, hf(j, s), i, 0)))


def _view_sum(sel, operands, out_view, out_shape, out_dtype, grid, name):
    k = len(operands)

    def body(sel_ref, *refs):
        acc = refs[0][...].astype(F32)
        for r in refs[1:k]:
            acc = acc + r[...].astype(F32)
        refs[k][...] = acc.astype(out_dtype)

    spec = pltpu.PrefetchScalarGridSpec(num_scalar_prefetch=1, grid=grid, in_specs=[v[1] for _, v in operands],
                                        out_specs=out_view[1])
    out = _call(body, name=name, grid_spec=spec, out_shape=_sds(out_view[0], out_dtype),
                compiler_params=_params(("parallel", "parallel", "parallel")))(
                    sel, *[a.reshape(v[0]) for a, v in operands])
    return out.reshape(out_shape)


def _sel_core(j, s):
    return s[0]


def _sel_chip(j, s):
    return s[1]


def _grid_j(j, s):
    return j


def _place_shard(lay, sel, w):
    return _view_sum(sel, [(w, lay.shard(_grid_j))], lay.full(_sel_chip, _grid_j), lay.full_shape, BF16,
                     (lay.lead, 2, lay.hr // lay.tr), "place_shard")


SEM_SPEC = pl.BlockSpec(memory_space=pltpu.SEMAPHORE)
ANY_SPEC = pl.BlockSpec(memory_space=pl.ANY)
SPLIT_COPY = pltpu.SideEffectType.DATAFLOW_SIDE_EFFECTING


def _in_hbm(a):
    return pltpu.with_memory_space_constraint(a, pltpu.HBM)


def _gather_copies(lays, bufs, send_sems, recv_sems):
    x, y, c, chips = _place()
    copies = []
    for a, lay in enumerate(lays):
        own = _full_region(bufs[a], lay.axis, _chip(x, y), c, lay.shard_shape)
        for k, chip in enumerate(chips):
            copies.append(pltpu.make_async_remote_copy(
                src_ref=own, dst_ref=own, send_sem=send_sems.at[a * 3 + k], recv_sem=recv_sems.at[a * 3 + k],
                device_id=(*chip, c), device_id_type=MESH))
    return copies


def _gather_start(fulls, after, lays, tag):
    n = len(fulls)

    def body(*refs):
        send_sems, recv_sems = refs[n + 1], refs[n + 2]
        bufs, token = refs[n + 3:2 * n + 3], refs[2 * n + 3]
        for cp in _gather_copies(lays, bufs, send_sems, recv_sems):
            cp.start()
        token[...] = jnp.zeros_like(token)

    outs = _call(body, name="gather_start_" + tag,
                 out_shape=[pltpu.SemaphoreType.DMA((3 * n,)), pltpu.SemaphoreType.DMA((3 * n,))]
                 + [pltpu.HBM(f.shape, f.dtype) for f in fulls] + [_sds((8, 128), F32)],
                 in_specs=[HBM_SPEC] * n + [ANY_SPEC],
                 out_specs=[SEM_SPEC, SEM_SPEC] + [HBM_SPEC] * n + [pl.BlockSpec(memory_space=pltpu.VMEM)],
                 input_output_aliases={a: a + 2 for a in range(n)},
                 compiler_params=pltpu.CompilerParams(has_side_effects=SPLIT_COPY))(*[_in_hbm(f) for f in fulls], after)
    return outs[0], outs[1], outs[2:2 + n], outs[2 + n]


def _gather_wait(send_sems, recv_sems, bufs, after, lays, tag):
    n = len(bufs)

    def body(*refs):
        ss, rs = refs[n], refs[n + 1]
        for cp in _gather_copies(lays, refs[n + 3:], ss, rs):
            cp.wait_send()
            cp.wait_recv()

    return _call(body, name="gather_wait_" + tag,
                 out_shape=[pltpu.HBM(b.shape, b.dtype) for b in bufs],
                 in_specs=[HBM_SPEC] * n + [SEM_SPEC, SEM_SPEC, ANY_SPEC], out_specs=[HBM_SPEC] * n,
                 input_output_aliases={a: a for a in range(n)},
                 compiler_params=pltpu.CompilerParams(has_side_effects=SPLIT_COPY))(*bufs, send_sems, recv_sems, after)


def _gather_forward(bufs, lays):
    n = len(bufs)

    def body(*refs):
        outs = refs[n:2 * n]
        send_sems, recv_sems = refs[2 * n:]
        x, y, c, chips = _place()
        sibling = (x, y, 1 - c)
        sends = []
        for a in range(n):
            for k, chip in enumerate(chips):
                landed = _full_region(outs[a], lays[a].axis, _chip(*chip), c, lays[a].shard_shape)
                cp = pltpu.make_async_remote_copy(
                    src_ref=landed, dst_ref=landed, send_sem=send_sems.at[a * 3 + k], recv_sem=recv_sems.at[a * 3 + k],
                    device_id=sibling, device_id_type=MESH)
                cp.start()
                sends.append(cp)
        for a in range(n):
            for k, chip in enumerate(chips):
                passed = _full_region(outs[a], lays[a].axis, _chip(*chip), 1 - c, lays[a].shard_shape)
                pltpu.make_async_remote_copy(
                    src_ref=passed, dst_ref=passed, send_sem=send_sems.at[a * 3 + k], recv_sem=recv_sems.at[a * 3 + k],
                    device_id=sibling, device_id_type=MESH).wait_recv()
        for cp in sends:
            cp.wait_send()

    return _call(body, name="gather_forward",
                 out_shape=[_sds(b.shape, b.dtype) for b in bufs],
                 in_specs=[HBM_SPEC] * n, out_specs=[HBM_SPEC] * n,
                 input_output_aliases={a: a for a in range(n)},
                 scratch_shapes=[pltpu.SemaphoreType.DMA((3 * n,)), pltpu.SemaphoreType.DMA((3 * n,))],
                 compiler_params=_params())(*bufs)


def _half_rows_shape(full_shape):
    s = list(full_shape)
    s[-2] //= 2
    return tuple(s)


RELATIONS = tuple((r, s) for r in range(N_CHIP) for s in range(2))[1:]


def _peer(x, y, c, rel):
    r, s = rel
    return (1 - x if r in (1, 3) else x, 1 - y if r in (2, 3) else y, 1 - c if s else c)


def _reduce_copies(lays, grads, landing, send_sems, recv_sems):
    x, y, c, _ = _place()
    nr = len(RELATIONS)
    copies = []
    for a, lay in enumerate(lays):
        for k, rel in enumerate(RELATIONS):
            px, py, pc = _peer(x, y, c, rel)
            copies.append(pltpu.make_async_remote_copy(
                src_ref=_full_region(grads[a], lay.axis, _chip(px, py), pc, lay.shard_shape), dst_ref=landing[a * nr + k],
                send_sem=send_sems.at[a * nr + k], recv_sem=recv_sems.at[a * nr + k],
                device_id=(px, py, pc), device_id_type=MESH))
    return copies


def _reduce_start(grads, lays, tag):
    n, nr = len(grads), len(RELATIONS)
    landing = [_in_hbm(lax.empty(lay.half_shard_shape, BF16)) for lay in lays for _ in RELATIONS]
    m = n + n * nr

    def body(*refs):
        send_sems, recv_sems = refs[m], refs[m + 1]
        src, land, token = refs[m + 2:m + 2 + n], refs[m + 2 + n:2 * m + 2], refs[2 * m + 2]
        for cp in _reduce_copies(lays, src, land, send_sems, recv_sems):
            cp.start()
        token[...] = jnp.zeros_like(token)

    ops = [_in_hbm(g) for g in grads] + landing
    outs = _call(body, name="reduce_start_" + tag,
                 out_shape=[pltpu.SemaphoreType.DMA((n * nr,)), pltpu.SemaphoreType.DMA((n * nr,))]
                 + [pltpu.HBM(o.shape, o.dtype) for o in ops] + [_sds((8, 128), F32)],
                 in_specs=[HBM_SPEC] * m,
                 out_specs=[SEM_SPEC, SEM_SPEC] + [HBM_SPEC] * m + [pl.BlockSpec(memory_space=pltpu.VMEM)],
                 input_output_aliases={a: a + 2 for a in range(m)},
                 compiler_params=pltpu.CompilerParams(has_side_effects=SPLIT_COPY))(*ops)
    return outs[0], outs[1], outs[2:2 + n], outs[2 + n:2 + m], outs[2 + m]


def _reduce_wait(send_sems, recv_sems, grads, landing, after, lays, tag):
    n, nr = len(grads), len(RELATIONS)
    m = n + n * nr

    def body(*refs):
        ss, rs = refs[m], refs[m + 1]
        src, land = refs[m + 3:m + 3 + n], refs[m + 3 + n:]
        for cp in _reduce_copies(lays, src, land, ss, rs):
            cp.wait_send()
            cp.wait_recv()

    ops = list(grads) + list(landing)
    outs = _call(body, name="reduce_wait_" + tag,
                 out_shape=[pltpu.HBM(o.shape, o.dtype) for o in ops],
                 in_specs=[HBM_SPEC] * m + [SEM_SPEC, SEM_SPEC, ANY_SPEC], out_specs=[HBM_SPEC] * m,
                 input_output_aliases={a: a for a in range(m)},
                 compiler_params=pltpu.CompilerParams(has_side_effects=SPLIT_COPY))(*ops, send_sems, recv_sems, after)
    return outs[:n], [outs[n + nr * a:n + nr * a + nr] for a in range(n)]


def _share_halves(shards):
    n = len(shards)

    def body(*refs):
        outs = refs[n:2 * n]
        send_sems, recv_sems = refs[2 * n:]
        x, y, c, _ = _place()
        sibling = (x, y, 1 - c)
        started = []
        for a in range(n):
            mine = _shard_half(outs[a], c)
            rc = pltpu.make_async_remote_copy(src_ref=mine, dst_ref=mine, send_sem=send_sems.at[a],
                                              recv_sem=recv_sems.at[a], device_id=sibling, device_id_type=MESH)
            rc.start()
            started.append(rc)
        for rc in started:
            rc.wait_recv()
            rc.wait_send()

    return _call(body, name="share_halves", out_shape=[_sds(s.shape, F32) for s in shards],
                 in_specs=[HBM_SPEC] * n, out_specs=[HBM_SPEC] * n, input_output_aliases={a: a for a in range(n)},
                 scratch_shapes=[pltpu.SemaphoreType.DMA((n,)), pltpu.SemaphoreType.DMA((n,))],
                 compiler_params=_params())(*shards)


def _reduce_end(state, after, lays, sel, tag):
    send_sems, recv_sems, grads, landing, _ = state
    grads, landed = _reduce_wait(send_sems, recv_sems, grads, landing, after, lays, tag)
    halves = [
        _view_sum(sel, [(g, lay.full(_sel_chip, _sel_core))] + [(l, lay.half_shard()) for l in ls], lay.shard(_sel_core),
                  lay.shard_shape, F32, (lay.lead, 1, lay.hr // lay.tr), "shard_half_sum")
        for g, ls, lay in zip(grads, landed, lays)]
    return _share_halves(halves)


def _embed(blocks):
    n, r, c = blocks.shape
    eye = jnp.eye(n, dtype=blocks.dtype)
    return (blocks[:, :, None, :] * eye[:, None, :, None]).reshape(n * r, n * c)


def _unembed(mat, n):
    r, c = mat.shape[0] // n, mat.shape[1] // n
    return jnp.transpose(jnp.diagonal(mat.reshape(n, r, n, c), axis1=0, axis2=2), (2, 0, 1))


def _to_heads(a):
    return jnp.transpose(a.reshape(a.shape[0], HEADS, HD), (1, 0, 2))


def _from_heads(a):
    return jnp.transpose(a, (1, 0, 2)).reshape(a.shape[1], W)


def _row(v):
    return v.reshape(1, -1)


def _concat_cols(pieces):
    L = pieces[0].shape[0]
    widths = [p.shape[1] for p in pieces]
    tr = _row_tile(L)

    def body(*refs):
        off = 0
        for r, w in zip(refs[:-1], widths):
            refs[-1][:, off:off + w] = r[...].astype(BF16)
            off += w

    return _call(body, name="concat_cols", grid=(L // tr,),
                 in_specs=[pl.BlockSpec((tr, w), lambda i: (i, 0)) for w in widths],
                 out_specs=pl.BlockSpec((tr, sum(widths)), lambda i: (i, 0)), out_shape=_sds((L, sum(widths)), BF16),
                 compiler_params=_params(("parallel",)))(*pieces)


def _ffn_fwd(x, ada, gp, gq, w_in, w_out, s):
    L = x.shape[0]
    h = _norm_mod(x, _row(gp[s]), _row(ada[3 * s]), _row(ada[3 * s + 1]))
    a, b, act = _ffn_in(h, w_in)
    f = _mm(act, w_out, M=L, N=D, K=FF, tm=min(L, 1024), tn=512, name="ffn_out")
    x2 = _post(x, f, _row(gq[s]), _row(ada[3 * s + 2]), 0.5)
    return x2, (x, h, a, b, act, f)


def _ffn_bwd(dx, saved, ada, gp, gq, w_in, w_out, s, stats):
    x, h, a, b, act, f = saved
    L = x.shape[0]
    df, stats = _post_bwd(dx, f, _row(gq[s]), _row(ada[3 * s + 2]), 0.5, stats, s)
    dw_out = _mm(act, df, M=FF, N=D, K=L, tm=256, tn=1024, ta=True, out_dtype=BF16, name="ffn_dw_out")
    da, db = _ffn_mid_bwd(df, w_out, a, b)
    du = _concat_cols([da, db])
    dw_in = _mm(h, du, M=D, N=2 * FF, K=L, tm=1024, tn=512, ta=True, out_dtype=BF16, name="ffn_dw_in")
    dh = _mm(du, w_in, M=L, N=D, K=2 * FF, tm=min(L, 1024), tn=1024, tk=1408, tb=True, name="ffn_dh")
    dx2, stats = _norm_mod_bwd(dh, x, _row(gp[s]), _row(ada[3 * s + 1]), dx, stats, s)
    return dx2, dw_in, dw_out, stats


def _mixer_fwd(x, ada, gp, gq, wf, sm):
    L = x.shape[0]
    h = _norm_mod(x, _row(gp[1]), _row(ada[3]), _row(ada[4]))
    p = _mm(h, wf["w_in"], M=L, N=IN_COLS, K=D, tm=min(L, 2048), tn=512, name="mixer_in")
    za = _conv_fwd(p, sm["conv_w"])
    y, zb = _ssm_fwd(p, sm["b_re"], sm["b_im"], sm["c_re"], sm["c_im"], sm["abr"], sm["abi"], sm["fr"], sm["fi"], sm["ssm_d"])
    zc = _pool_fwd(p, sm["w_pool"], sm["pool_scale"])
    q = _to_heads(p[:, 5 * W:6 * W]) * (HD ** -0.5)
    k = _to_heads(p[:, 6 * W:7 * W])
    v = _to_heads(p[:, 7 * W:8 * W])
    o_heads, block_sums = _sb_fwd(q, k, v)
    zd = _from_heads(o_heads).astype(BF16)
    merged = _merge_fwd(za, zb, zc, zd, p, wf["w_conv_out"], wf["w_glu"], wf["w_pool_out"], wf["w_sb_out"])
    m = _mm(merged, wf["w_out"], M=L, N=D, K=D, tm=min(L, 1024), tn=512, name="mixer_out")
    x2 = _post(x, m, _row(gq[1]), _row(ada[5]), 1.0)
    return x2, (x, h, p, za, y, zb, zc, zd, q, k, v, block_sums, merged, m)


def _mixer_bwd(dx, saved, ada, gp, gq, wf, sm, stats):
    x, h, p, za, y, zb, zc, zd, q, k, v, block_sums, merged, m = saved
    L = x.shape[0]
    dmf, stats = _post_bwd(dx, m, _row(gq[1]), _row(ada[5]), 1.0, stats, 1)
    dw_out = _mm(merged, dmf, M=D, N=D, K=L, tm=512, tn=512, ta=True, out_dtype=BF16, name="mixer_dw_out")
    dmerged = _mm(dmf, wf["w_out"], M=L, N=D, K=D, tm=min(L, 1024), tn=512, tb=True, name="mixer_dmerged")
    dza, dzb, dzc, dzd, dgates, dwc, dwg, dwp, dws = _merge_bwd(
        dmerged, za, zb, zc, zd, p, wf["w_conv_out"], wf["w_glu"], wf["w_pool_out"], wf["w_sb_out"])
    dconv, dconv_w = _conv_bwd(p, sm["conv_w"], dza)
    (du_ssm, dd, dbr, dbi, dcr, dci, gar, gai, gfr, gfi) = _ssm_bwd(
        p, y, dzb, sm["b_re"], sm["b_im"], sm["c_re"], sm["c_im"], sm["abr"], sm["abi"], sm["fr"], sm["fi"], sm["ssm_d"])
    du_pool, dwpool, dpscale = _pool_bwd(p, sm["w_pool"], sm["pool_scale"], dzc)
    dq, dk, dv = _sb_bwd(q, k, v, _to_heads(dzd), block_sums)
    dqkv = [_from_heads(t) for t in (dq * (HD ** -0.5), dk, dv)]
    dp = _concat_cols([dconv, du_ssm, du_pool] + dqkv + [dgates])
    dw_in = _mm(h, dp, M=D, N=IN_COLS, K=L, tm=1024, tn=512, ta=True, out_dtype=BF16, name="mixer_dw_in")
    dh = _mm(dp, wf["w_in"], M=L, N=D, K=IN_COLS, tm=min(L, 1024), tn=1024, tk=1536, tb=True, name="mixer_dh")
    dx2, stats = _norm_mod_bwd(dh, x, _row(gp[1]), _row(ada[4]), dx, stats, 1)
    wgrads = [dw_in, dwc, dwg, dwp, dws, dw_out]
    small = {"conv_w": dconv_w, "ssm_d": dd, "b_re": dbr, "b_im": dbi, "c_re": dcr, "c_im": dci,
             "abr": gar, "abi": gai, "fr": gfr, "fi": gfi, "w_pool": dwpool, "pool_scale": dpscale}
    return dx2, wgrads, small, stats


def _pack(arrays):
    flat = jnp.concatenate([a.reshape(-1) for a in arrays])
    rows = -(-flat.shape[0] // 128)
    rows = -(-rows // 64) * 64
    return jnp.pad(flat, (0, rows * 128 - flat.shape[0])).reshape(rows, 128)


def _unpack(block, shapes):
    flat = block.reshape(-1)
    out, off = [], 0
    for s in shapes:
        n = int(np.prod(s))
        out.append(flat[off:off + n].reshape(s))
        off += n
    return out


def _pad_rows(a, mult):
    rows = -(-a.shape[0] // mult) * mult
    return jnp.concatenate([a] * (-(-rows // a.shape[0])), axis=0)[:rows]


SMALL_ORDER = ("stats", "conv_w", "lam_re", "lam_im", "log_dt", "ssm_b_re", "ssm_b_im",
               "ssm_c_re", "ssm_c_im", "ssm_d", "w_pool", "pool_scale")
WEIGHTS = ('w_ada', 'b_ada', 'g_pre', 'g_post', 'w_ff_in', 'w_ff_out', 'w_in', 'conv_w', 'w_conv_out', 'lam_re', 'lam_im',
           'log_dt', 'ssm_b_re', 'ssm_b_im', 'ssm_c_re', 'ssm_c_im', 'ssm_d', 'w_glu', 'w_pool', 'pool_scale', 'w_pool_out',
           'w_sb_out', 'w_out')


def _step(a):
    depth = a["w_ada"].shape[0]
    x = a["x"][0]
    target = a["loss_target"][0]
    L = x.shape[0]
    ix, iy, ic = lax.axis_index("x"), lax.axis_index("y"), lax.axis_index("c")
    chip = 2 * ix + iy
    me = 4 * ix + 2 * iy + ic
    sel = jnp.stack([ic, chip]).astype(jnp.int32)
    lays = [_Lay(a[name].shape[(1 if idx is None else 2):], ax) for name, idx, ax, _ in GATHERED]

    def entries(g):
        return [i for i, e in enumerate(GATHERED) if e[3] == g]

    def shard_of(i, l):
        name, idx = GATHERED[i][0], GATHERED[i][1]
        return a[name][l] if idx is None else a[name][l, idx]

    stages = [(l, g) for l in range(depth) for g in range(N_SUB)]

    def gather_begin(t, after):
        l, g = stages[t]
        placed = [_place_shard(lays[i], sel, shard_of(i, l)) for i in entries(g)]
        return _gather_start(placed, after, [lays[i] for i in entries(g)], str(t))

    pending = {t: gather_begin(t, x) for t in range(min(2, len(stages)))}
    started = sum(p[3][0, 0] for p in pending.values())

    first_shapes = [(D,), (depth, 3, W), (depth, 3, W), (depth, 3, W // N_CHIP)]
    gathered = _allgather8(_pack([a["c"] + started, a["g_pre"], a["g_post"], a["conv_w"]]), "gather_small_inputs")
    per_dev = [_unpack(blk, first_shapes) for blk in gathered.reshape(N_DEV, -1, 128)]
    c_all = jnp.stack([d[0] for d in per_dev])
    c_pad = jnp.concatenate([c_all, jnp.zeros_like(c_all)], axis=0)
    g_pre = jnp.concatenate([per_dev[2 * j][1] for j in range(N_CHIP)], axis=-1)
    g_post = jnp.concatenate([per_dev[2 * j][2] for j in range(N_CHIP)], axis=-1)
    conv_w = jnp.concatenate([per_dev[2 * j][3] for j in range(N_CHIP)], axis=-1)

    b_cols = lax.dynamic_slice(a["b_ada"], (0, chip * ADA_SHARD), (depth, ADA_SHARD)).reshape(depth, 1, ADA_SHARD)
    ada_part = _ada_fwd(c_pad, a["w_ada"], b_cols)
    ada_all = _allgather8(ada_part.reshape(depth * 16, ADA_SHARD), "gather_ada").reshape(N_DEV, depth, 16, ADA_SHARD)
    ada_rows = lax.dynamic_slice(ada_all, (0, 0, me, 0), (N_DEV, depth, 1, ADA_SHARD))[:, :, 0]
    ada = jnp.concatenate([ada_rows[2 * j] for j in range(N_CHIP)], axis=-1).reshape(depth, 9, D)

    lam_re = _pad_rows(a["lam_re"].reshape(depth, NST), 8)
    lam_im = _pad_rows(a["lam_im"].reshape(depth, NST), 8)
    log_dt_x = _pad_rows(jnp.repeat(a["log_dt"], GP, axis=1), 8)
    abr, abi, fr, fi = _ssm_prep(lam_re, lam_im, log_dt_x)

    def small_of(l):
        return {"conv_w": conv_w[l], "ssm_d": _row(a["ssm_d"][l]), "pool_scale": _row(a["pool_scale"][l]),
                "b_re": _embed(jnp.transpose(a["ssm_b_re"][l], (0, 2, 1))), "b_im": _embed(jnp.transpose(a["ssm_b_im"][l], (0, 2, 1))),
                "c_re": _embed(jnp.transpose(a["ssm_c_re"][l], (0, 2, 1))), "c_im": _embed(jnp.transpose(a["ssm_c_im"][l], (0, 2, 1))),
                "w_pool": _embed(a["w_pool"][l]),
                "abr": abr[l:l + 1], "abi": abi[l:l + 1], "fr": fr[l:l + 1], "fi": fi[l:l + 1]}

    saved, weights, smalls = [], [], [small_of(l) for l in range(depth)]
    for t, (l, g) in enumerate(stages):
        glays = [lays[i] for i in entries(g)]
        send_sems, recv_sems, bufs, _ = pending.pop(t)
        w = _gather_forward(_gather_wait(send_sems, recv_sems, bufs, x if t else ada, glays, str(t)), glays)
        weights.append(w)
        ada_l = ada[l]
        if t + 2 < len(stages):
            pending[t + 2] = gather_begin(t + 2, x)
            ada_l = ada_l + pending[t + 2][3][0, 0]
        if g == 1:
            wf = {GATHERED[i][0]: wi for i, wi in zip(entries(1), w)}
            x, sv = _mixer_fwd(x, ada_l, g_pre[l], g_post[l], wf, smalls[l])
        else:
            x, sv = _ffn_fwd(x, ada_l, g_pre[l], g_post[l], w[0], w[1], g)
        saved.append(sv)
    dx, loss_part = _loss_head(x, target)
    loss = lax.psum(loss_part[0, 0], ("x", "y", "c"))

    shard_grads = [[None] * depth for _ in GATHERED]
    small_grads = [{} for _ in range(depth)]
    stats = [jnp.zeros((STAT_ROWS, D), F32) for _ in range(depth)]
    states = {}

    def reduce_finish(t, after):
        l, g = stages[t]
        glays = [lays[i] for i in entries(g)]
        for i, grad in zip(entries(g), _reduce_end(states.pop(t), after, glays, sel, str(t))):
            shard_grads[i][l] = grad

    token = None
    for t in reversed(range(len(stages))):
        l, g = stages[t]
        ada_l = ada[l] if token is None else ada[l] + token[0, 0]
        if g == 1:
            wf = {GATHERED[i][0]: wi for i, wi in zip(entries(1), weights[t])}
            dx, wgrads, small, stats[l] = _mixer_bwd(dx, saved[t], ada_l, g_pre[l], g_post[l], wf, smalls[l], stats[l])
            small_grads[l].update(small)
        else:
            dx, dw_in, dw_out, stats[l] = _ffn_bwd(dx, saved[t], ada_l, g_pre[l], g_post[l], weights[t][0], weights[t][1], g,
                                                   stats[l])
            wgrads = [dw_in, dw_out]
        states[t] = _reduce_start(wgrads, [lays[i] for i in entries(g)], str(t))
        token = states[t][4]
        if t + 2 in states:
            reduce_finish(t + 2, dx)
    stack = lambda key: _pad_rows(jnp.concatenate([small_grads[l][key] for l in range(depth)], axis=0), 8)
    gs = np.zeros((NST, 128), np.float32)
    gs[np.arange(NST), np.arange(NST) // GP] = 1.0
    dlr, dli, dldt = _ssm_prep_bwd(lam_re, lam_im, log_dt_x, stack("abr"), stack("abi"), stack("fr"), stack("fi"), jnp.asarray(gs))
    part = {
        "stats": jnp.stack(stats) + sum(st[4][0, 0] for st in states.values()),
        "conv_w": jnp.stack([small_grads[l]["conv_w"] for l in range(depth)]),
        "lam_re": dlr[:depth].reshape(depth, G, GP), "lam_im": dli[:depth].reshape(depth, G, GP), "log_dt": dldt[:depth, :G],
        "ssm_b_re": jnp.stack([jnp.transpose(_unembed(small_grads[l]["b_re"], G), (0, 2, 1)) for l in range(depth)]),
        "ssm_b_im": jnp.stack([jnp.transpose(_unembed(small_grads[l]["b_im"], G), (0, 2, 1)) for l in range(depth)]),
        "ssm_c_re": jnp.stack([jnp.transpose(_unembed(small_grads[l]["c_re"], G), (0, 2, 1)) for l in range(depth)]),
        "ssm_c_im": jnp.stack([jnp.transpose(_unembed(small_grads[l]["c_im"], G), (0, 2, 1)) for l in range(depth)]),
        "ssm_d": jnp.stack([small_grads[l]["ssm_d"][0] for l in range(depth)]),
        "w_pool": jnp.stack([_unembed(small_grads[l]["w_pool"], len(POOL_WINDOWS)) for l in range(depth)]),
        "pool_scale": jnp.stack([small_grads[l]["pool_scale"][0] for l in range(depth)]),
    }
    small_shapes = [part[k].shape for k in SMALL_ORDER]
    blocks = _allgather8(_pack([part[k] for k in SMALL_ORDER]), "gather_small_grads").reshape(N_DEV, -1, 128)
    small_sum = _sum_parts([blocks[i] for i in range(N_DEV)], F32, "small_grad_sum")
    total = dict(zip(SMALL_ORDER, _unpack(small_sum, small_shapes)))
    d_ada_all = jnp.stack([_unpack(blocks[i], small_shapes[:1])[0][:, :9].reshape(depth, 9 * D) for i in range(N_DEV)])
    d_cols = lax.dynamic_slice(d_ada_all, (0, 0, chip * ADA_SHARD), (N_DEV, depth, ADA_SHARD))
    d_cols = jnp.transpose(d_cols, (1, 0, 2))
    grads = {"w_ada": _ada_wgrad(c_pad, jnp.concatenate([d_cols, jnp.zeros_like(d_cols)], axis=1)),
             "b_ada": total["stats"][:, :9].reshape(depth, 9 * D),
             "g_pre": lax.dynamic_slice(total["stats"], (0, 9, chip * W), (depth, 3, W)),
             "g_post": lax.dynamic_slice(total["stats"], (0, 12, chip * W), (depth, 3, W)),
             "conv_w": lax.dynamic_slice(total["conv_w"], (0, 0, chip * (W // N_CHIP)), (depth, 3, W // N_CHIP))}
    for k in SMALL_ORDER[2:]:
        grads[k] = total[k]

    out = {"loss": loss, "grad_x": dx[None]}

    def update(name):
        out["grad_" + name] = grads[name]
        out["delta_" + name], out["new_m_" + name], out["new_v_" + name] = _adamw(a[name], grads[name], a["m_" + name], a["v_" + name])

    for name in WEIGHTS:
        if name in grads:
            update(name)
    for t in sorted(states, reverse=True):
        reduce_finish(t, out["delta_w_ada"])
    for name in sorted({e[0] for e in GATHERED}):
        cols = [shard_grads[i] for i, e in enumerate(GATHERED) if e[0] == name]
        grads[name] = jnp.stack(cols[0]) if len(cols) == 1 else jnp.stack([jnp.stack(pair) for pair in zip(*cols)])
        update(name)
    return out


def kernel(x, c, w_ada, b_ada, g_pre, g_post, w_ff_in, w_ff_out, w_in, conv_w, w_conv_out, lam_re, lam_im, log_dt, ssm_b_re, ssm_b_im, ssm_c_re, ssm_c_im, ssm_d, w_glu, w_pool, pool_scale, w_pool_out, w_sb_out, w_out, loss_target, m_w_ada, m_b_ada, m_g_pre, m_g_post, m_w_ff_in, m_w_ff_out, m_w_in, m_conv_w, m_w_conv_out, m_lam_re, m_lam_im, m_log_dt, m_ssm_b_re, m_ssm_b_im, m_ssm_c_re, m_ssm_c_im, m_ssm_d, m_w_glu, m_w_pool, m_pool_scale, m_w_pool_out, m_w_sb_out, m_w_out, v_w_ada, v_b_ada, v_g_pre, v_g_post, v_w_ff_in, v_w_ff_out, v_w_in, v_conv_w, v_w_conv_out, v_lam_re, v_lam_im, v_log_dt, v_ssm_b_re, v_ssm_b_im, v_ssm_c_re, v_ssm_c_im, v_ssm_d, v_w_glu, v_w_pool, v_pool_scale, v_w_pool_out, v_w_sb_out, v_w_out):
    out = _step(dict(locals()))
    names = ["loss", "grad_x"] + [p + n for p in ("grad_", "delta_", "new_m_", "new_v_") for n in WEIGHTS]
    return tuple(out[n] for n in names)
```

```python
import functools
import math

import jax
import jax.numpy as jnp
import numpy as np
from jax import lax
from jax.experimental import pallas as pl
from jax.experimental.pallas import tpu as pltpu

F32 = jnp.float32
BF16 = jnp.bfloat16
MESH = pl.DeviceIdType.MESH

D = 1024
W = 256
FF = 2816
IN_COLS = 6144
G = 16
GH = 16
GP = 64
NST = G * GP
QB = 128
HEADS = 4
HD = 64
EPS = 1e-6
LAMBDA_RE_MAX = -1e-4
POOL_WINDOWS = (2, 4, 8, 16)
N_CHIP = 4
N_DEV = 8
VMEM_LIMIT = 56 * 1024 * 1024
HIGH = lax.Precision.HIGHEST

ADAM_LR, ADAM_B1, ADAM_B2, ADAM_EPS, ADAM_WD, ADAM_STEP = 0.001, 0.9, 0.999, 1e-08, 0.01, 10


def _call(body, **kw):
    return pl.pallas_call(body, **kw)


def _params(dims=None, **kw):
    return pltpu.CompilerParams(dimension_semantics=dims, vmem_limit_bytes=VMEM_LIMIT, **kw)


def _sds(shape, dtype):
    return jax.ShapeDtypeStruct(shape, dtype)


def _dot(a, b, ca=1, cb=0, precision=None):
    return lax.dot_general(a, b, (((ca,), (cb,)), ((), ())), preferred_element_type=F32, precision=precision)


def _bdot(a, b, ca=1, cb=0):
    return _dot(a.astype(BF16), b.astype(BF16), ca, cb)


def _sigmoid(x):
    return 1.0 / (1.0 + jnp.exp(-x))


def _mm(a, b, *, M, N, K, tm, tn, tk=None, ta=False, tb=False, out_dtype=F32, a_off=(0, 0), b_off=(0, 0), name):
    tk = K if tk is None else tk
    nk = K // tk
    assert M % tm == 0 and N % tn == 0 and K % tk == 0

    def body(a_ref, b_ref, o_ref, *acc):
        part = _bdot(a_ref[...], b_ref[...], 0 if ta else 1, 1 if tb else 0)
        if nk == 1:
            o_ref[...] = part.astype(out_dtype)
            return
        acc_ref = acc[0]
        k = pl.program_id(2)

        @pl.when(k == 0)
        def _():
            acc_ref[...] = part

        @pl.when(k > 0)
        def _():
            acc_ref[...] += part

        @pl.when(k == nk - 1)
        def _():
            o_ref[...] = acc_ref[...].astype(out_dtype)

    if ta:
        a_spec = pl.BlockSpec((tk, tm), lambda i, j, k: (k + a_off[0], i + a_off[1]))
    else:
        a_spec = pl.BlockSpec((tm, tk), lambda i, j, k: (i + a_off[0], k + a_off[1]))
    if tb:
        b_spec = pl.BlockSpec((tn, tk), lambda i, j, k: (j + b_off[0], k + b_off[1]))
    else:
        b_spec = pl.BlockSpec((tk, tn), lambda i, j, k: (k + b_off[0], j + b_off[1]))
    return _call(
        body, name=name, grid=(M // tm, N // tn, nk),
        in_specs=[a_spec, b_spec],
        out_specs=pl.BlockSpec((tm, tn), lambda i, j, k: (i, j)),
        out_shape=_sds((M, N), out_dtype),
        scratch_shapes=[] if nk == 1 else [pltpu.VMEM((tm, tn), F32)],
        compiler_params=_params(("parallel", "parallel", "arbitrary")),
    )(a, b)


def _row_tile(L):
    return min(L, 256)


def _wide_tile(L):
    return min(L, 512)


def _norm_mod(x, g, shift, scale):
    L = x.shape[0]
    tr = _wide_tile(L)

    def body(x_ref, g_ref, sh_ref, sc_ref, h_ref):
        xv = x_ref[...]
        r = lax.rsqrt(jnp.mean(xv * xv, axis=-1, keepdims=True) + EPS)
        h_ref[...] = (xv * r * g_ref[...] * (1.0 + sc_ref[...]) + sh_ref[...]).astype(BF16)

    row = pl.BlockSpec((tr, D), lambda i: (i, 0))
    vec = pl.BlockSpec((1, D), lambda i: (0, 0))
    return _call(body, name="norm_mod", grid=(L // tr,), in_specs=[row, vec, vec, vec], out_specs=row,
                 out_shape=_sds((L, D), BF16), compiler_params=_params(("parallel",)))(x, g, shift, scale)


STAT_ROWS = 16


def _norm_mod_bwd(dh, x, g, scale, dx_res, stats, s):
    L = x.shape[0]
    tr = _wide_tile(L)

    def body(dh_ref, x_ref, g_ref, sc_ref, dxr_ref, stin_ref, dx_ref, st_ref):
        i = pl.program_id(0)
        xv = x_ref[...]
        dhv = dh_ref[...]
        r = lax.rsqrt(jnp.mean(xv * xv, axis=-1, keepdims=True) + EPS)
        y = xv * r
        n = y * g_ref[...]
        dn = dhv * (1.0 + sc_ref[...])
        dy = dn * g_ref[...]
        dx_ref[...] = dxr_ref[...] + r * (dy - y * jnp.mean(dy * y, axis=-1, keepdims=True))

        @pl.when(i == 0)
        def _():
            st_ref[...] = stin_ref[...]

        st_ref[3 * s:3 * s + 1, :] += jnp.sum(dhv, axis=0, keepdims=True)
        st_ref[3 * s + 1:3 * s + 2, :] += jnp.sum(dhv * n, axis=0, keepdims=True)
        st_ref[9 + s:10 + s, :] += jnp.sum(dn * y, axis=0, keepdims=True)

    row = pl.BlockSpec((tr, D), lambda i: (i, 0))
    vec = pl.BlockSpec((1, D), lambda i: (0, 0))
    st = pl.BlockSpec((STAT_ROWS, D), lambda i: (0, 0))
    return _call(body, name="norm_mod_bwd", grid=(L // tr,), in_specs=[row, row, vec, vec, row, st],
                 out_specs=[row, st], out_shape=[_sds((L, D), F32), _sds((STAT_ROWS, D), F32)],
                 input_output_aliases={5: 1},
                 compiler_params=_params(("arbitrary",)))(dh, x, g, scale, dx_res, stats)


def _post(x, f, g, gate, res_weight):
    L = x.shape[0]
    tr = _wide_tile(L)

    def body(x_ref, f_ref, g_ref, gt_ref, o_ref):
        fv = f_ref[...]
        r = lax.rsqrt(jnp.mean(fv * fv, axis=-1, keepdims=True) + EPS)
        o_ref[...] = x_ref[...] + (res_weight * (1.0 + gt_ref[...])) * (fv * r * g_ref[...])

    row = pl.BlockSpec((tr, D), lambda i: (i, 0))
    vec = pl.BlockSpec((1, D), lambda i: (0, 0))
    return _call(body, name="post", grid=(L // tr,), in_specs=[row, row, vec, vec], out_specs=row,
                 out_shape=_sds((L, D), F32), compiler_params=_params(("parallel",)))(x, f, g, gate)


def _post_bwd(dx, f, g, gate, res_weight, stats, s):
    L = dx.shape[0]
    tr = _wide_tile(L)

    def body(dx_ref, f_ref, g_ref, gt_ref, stin_ref, df_ref, st_ref):
        i = pl.program_id(0)
        fv = f_ref[...]
        dxv = dx_ref[...]
        r = lax.rsqrt(jnp.mean(fv * fv, axis=-1, keepdims=True) + EPS)
        y = fv * r
        dn = dxv * (res_weight * (1.0 + gt_ref[...]))
        dy = dn * g_ref[...]
        df_ref[...] = (r * (dy - y * jnp.mean(dy * y, axis=-1, keepdims=True))).astype(BF16)

        @pl.when(i == 0)
        def _():
            st_ref[...] = stin_ref[...]

        st_ref[3 * s + 2:3 * s + 3, :] += res_weight * jnp.sum(dxv * (y * g_ref[...]), axis=0, keepdims=True)
        st_ref[12 + s:13 + s, :] += jnp.sum(dn * y, axis=0, keepdims=True)

    row = pl.BlockSpec((tr, D), lambda i: (i, 0))
    vec = pl.BlockSpec((1, D), lambda i: (0, 0))
    st = pl.BlockSpec((STAT_ROWS, D), lambda i: (0, 0))
    return _call(body, name="post_bwd", grid=(L // tr,), in_specs=[row, row, vec, vec, st],
                 out_specs=[row, st], out_shape=[_sds((L, D), BF16), _sds((STAT_ROWS, D), F32)],
                 input_output_aliases={4: 1},
                 compiler_params=_params(("arbitrary",)))(dx, f, g, gate, stats)


def _loss_head(x, target):
    L = x.shape[0]
    tr = _wide_tile(L)

    def body(x_ref, t_ref, dx_ref, loss_ref):
        i = pl.program_id(0)
        err = x_ref[...] - t_ref[...]
        dx_ref[...] = err * (1.0 / D)

        @pl.when(i == 0)
        def _():
            loss_ref[...] = jnp.zeros_like(loss_ref)

        loss_ref[...] += 0.5 * jnp.sum(jnp.mean(err * err, axis=-1, keepdims=True), axis=0, keepdims=True)

    row = pl.BlockSpec((tr, D), lambda i: (i, 0))
    return _call(body, name="loss_head", grid=(L // tr,), in_specs=[row, row],
                 out_specs=[row, pl.BlockSpec((1, 1), lambda i: (0, 0))],
                 out_shape=[_sds((L, D), F32), _sds((1, 1), F32)],
                 compiler_params=_params(("arbitrary",)))(x, target)


def _ffn_in(h, w_in):
    L = h.shape[0]
    tm, tn = min(L, 2048), 256
    nf = FF // tn

    def body(h_ref, wa_ref, wb_ref, a_ref, b_ref, act_ref):
        hv = h_ref[...]
        a = _dot(hv, wa_ref[...])
        b = _dot(hv, wb_ref[...])
        a_ref[...] = a
        b_ref[...] = b
        act_ref[...] = (a * _sigmoid(a) * b).astype(BF16)

    tile = pl.BlockSpec((tm, tn), lambda i, j: (i, j))
    return _call(body, name="ffn_in", grid=(L // tm, nf),
                 in_specs=[pl.BlockSpec((tm, D), lambda i, j: (i, 0)),
                           pl.BlockSpec((D, tn), lambda i, j: (0, j)),
                           pl.BlockSpec((D, tn), lambda i, j: (0, j + nf))],
                 out_specs=[tile, tile, tile],
                 out_shape=[_sds((L, FF), F32), _sds((L, FF), F32), _sds((L, FF), BF16)],
                 compiler_params=_params(("parallel", "parallel")))(h, w_in, w_in)


def _ffn_mid_bwd(df, w_out, a, b):
    L = df.shape[0]
    tm, tn = min(L, 2048), 256

    def body(df_ref, w_ref, a_ref, b_ref, da_ref, db_ref):
        dact = _dot(df_ref[...], w_ref[...], 1, 1)
        av = a_ref[...]
        sg = _sigmoid(av)
        da_ref[...] = (dact * b_ref[...] * (sg * (1.0 + av * (1.0 - sg)))).astype(BF16)
        db_ref[...] = (dact * (av * sg)).astype(BF16)

    tile = pl.BlockSpec((tm, tn), lambda i, j: (i, j))
    return _call(body, name="ffn_mid_bwd", grid=(L // tm, FF // tn),
                 in_specs=[pl.BlockSpec((tm, D), lambda i, j: (i, 0)),
                           pl.BlockSpec((tn, D), lambda i, j: (j, 0)), tile, tile],
                 out_specs=[tile, tile],
                 out_shape=[_sds((L, FF), BF16), _sds((L, FF), BF16)],
                 compiler_params=_params(("parallel", "parallel")))(df, w_out, a, b)


def _rows_before(ref, i, tr, halo):
    start = pl.multiple_of(jnp.maximum(i * tr - halo, 0), 8)
    return jnp.where(i > 0, ref[pl.ds(start, halo), :], 0.0)


def _rows_after(ref, i, n, tr, halo):
    start = pl.multiple_of(jnp.minimum((i + 1) * tr, (n - 1) * tr), 8)
    return jnp.where(i < n - 1, ref[pl.ds(start, halo), :], 0.0)


def _conv_fwd(p, conv_w):
    L = p.shape[0]
    tr = _row_tile(L)
    n = L // tr

    def body(bg_ref, cg_ref, v_ref, w_ref, za_ref, u_scr):
        i = pl.program_id(0)

        @pl.when(i == 0)
        def _():
            u_scr[...] = cg_ref[...] * v_ref[...]

        r0 = pl.multiple_of(i * tr, 8)
        ext = jnp.concatenate([_rows_before(u_scr, i, tr, 8), u_scr[pl.ds(r0, tr), :]], axis=0)
        w = w_ref[...]
        y = (w[0:1] * pltpu.roll(ext, 2, axis=0) + w[1:2] * pltpu.roll(ext, 1, axis=0) + w[2:3] * ext)[8:, :]
        za_ref[...] = (bg_ref[pl.ds(r0, tr), :] * y).astype(BF16)

    col = lambda c: pl.BlockSpec((L, W), lambda i: (0, c))
    return _call(body, name="conv_fwd", grid=(n,),
                 in_specs=[col(0), col(1), col(2), pl.BlockSpec((3, W), lambda i: (0, 0))],
                 out_specs=pl.BlockSpec((tr, W), lambda i: (i, 0)),
                 out_shape=_sds((L, W), BF16),
                 scratch_shapes=[pltpu.VMEM((L, W), F32)],
                 compiler_params=_params(("arbitrary",)))(p, p, p, conv_w)


def _conv_bwd(p, conv_w, dza):
    L = p.shape[0]
    tr = _row_tile(L)
    n = L // tr

    def body(bg_ref, cg_ref, v_ref, w_ref, dza_ref, dp_ref, dw_ref, u_scr, dy_scr):
        i = pl.program_id(0)

        @pl.when(i == 0)
        def _():
            u_scr[...] = cg_ref[...] * v_ref[...]
            dy_scr[...] = dza_ref[...] * bg_ref[...]
            dw_ref[...] = jnp.zeros_like(dw_ref)

        r0 = pl.multiple_of(i * tr, 8)
        w = w_ref[...]
        ext = jnp.concatenate([_rows_before(u_scr, i, tr, 8), u_scr[pl.ds(r0, tr), :]], axis=0)
        u2 = pltpu.roll(ext, 2, axis=0)[8:, :]
        u1 = pltpu.roll(ext, 1, axis=0)[8:, :]
        u0 = ext[8:, :]
        y = w[0:1] * u2 + w[1:2] * u1 + w[2:3] * u0
        dy = dy_scr[pl.ds(r0, tr), :]
        dext = jnp.concatenate([dy, _rows_after(dy_scr, i, n, tr, 8)], axis=0)
        m = tr + 8
        du = (w[2:3] * dext + w[1:2] * pltpu.roll(dext, m - 1, axis=0) + w[0:1] * pltpu.roll(dext, m - 2, axis=0))[:tr, :]
        dp_ref[:, 0:W] = (dza_ref[pl.ds(r0, tr), :] * y).astype(BF16)
        dp_ref[:, W:2 * W] = (du * v_ref[pl.ds(r0, tr), :]).astype(BF16)
        dp_ref[:, 2 * W:3 * W] = (du * cg_ref[pl.ds(r0, tr), :]).astype(BF16)
        dw_ref[...] += jnp.concatenate([jnp.sum(dy * u2, axis=0, keepdims=True),
                                        jnp.sum(dy * u1, axis=0, keepdims=True),
                                        jnp.sum(dy * u0, axis=0, keepdims=True)], axis=0)

    col = lambda c: pl.BlockSpec((L, W), lambda i: (0, c))
    return _call(body, name="conv_bwd", grid=(n,),
                 in_specs=[col(0), col(1), col(2), pl.BlockSpec((3, W), lambda i: (0, 0)),
                           pl.BlockSpec((L, W), lambda i: (0, 0))],
                 out_specs=[pl.BlockSpec((tr, 3 * W), lambda i: (i, 0)), pl.BlockSpec((3, W), lambda i: (0, 0))],
                 out_shape=[_sds((L, 3 * W), BF16), _sds((3, W), F32)],
                 scratch_shapes=[pltpu.VMEM((L, W), F32), pltpu.VMEM((L, W), F32)],
                 compiler_params=_params(("arbitrary",)))(p, p, p, conv_w, dza)


def _pool_windows(lane):
    wins = jnp.zeros(lane.shape, jnp.int32)
    for gi, w in enumerate(POOL_WINDOWS):
        wins = jnp.where(lane // (W // len(POOL_WINDOWS)) == gi, w, wins)
    return wins


def _pooled_block(u_ref, i, tr):
    r0 = pl.multiple_of(i * tr, 8)
    cur = u_ref[pl.ds(r0, tr), :]
    ext = jnp.concatenate([_rows_before(u_ref, i, tr, 16), cur], axis=0)
    s2 = ext + pltpu.roll(ext, 1, axis=0)
    s4 = s2 + pltpu.roll(s2, 2, axis=0)
    s8 = s4 + pltpu.roll(s4, 4, axis=0)
    s16 = s8 + pltpu.roll(s8, 8, axis=0)
    lane = lax.broadcasted_iota(jnp.int32, (tr, W), 1)
    wins = _pool_windows(lane)
    win_sum = jnp.where(wins == 2, s2[16:], jnp.where(wins == 4, s4[16:], jnp.where(wins == 8, s8[16:], s16[16:])))
    t = lax.broadcasted_iota(jnp.int32, (tr, W), 0) + i * tr
    cnt = jnp.minimum(t + 1, wins).astype(F32)
    return win_sum / cnt - cur, cnt


def _pool_fwd(p, w_pool_bd, pool_scale):
    L = p.shape[0]
    tr = _row_tile(L)

    def body(u_ref, w_ref, sc_ref, zc_ref):
        pooled, _ = _pooled_block(u_ref, pl.program_id(0), tr)
        zc_ref[...] = (_bdot(pooled, w_ref[...]) * sc_ref[...]).astype(BF16)

    return _call(body, name="pool_fwd", grid=(L // tr,),
                 in_specs=[pl.BlockSpec((L, W), lambda i: (0, 4)), pl.BlockSpec((W, W), lambda i: (0, 0)),
                           pl.BlockSpec((1, W), lambda i: (0, 0))],
                 out_specs=pl.BlockSpec((tr, W), lambda i: (i, 0)), out_shape=_sds((L, W), BF16),
                 compiler_params=_params(("arbitrary",)))(p, w_pool_bd, pool_scale)


def _pool_bwd(p, w_pool_bd, pool_scale, dzc):
    L = p.shape[0]
    tr = _row_tile(L)
    n = L // tr

    def body(u_ref, w_ref, sc_ref, dzc_ref, du_ref, dw_ref, dsc_ref, g_scr):
        i = pl.program_id(0)

        @pl.when(i == 0)
        def _():
            dw_ref[...] = jnp.zeros_like(dw_ref)
            dsc_ref[...] = jnp.zeros_like(dsc_ref)

            def rows(k, carry):
                r = pl.multiple_of(k * tr, 8)
                dmix = (dzc_ref[pl.ds(r, tr), :] * sc_ref[...]).astype(BF16)
                dpool = _dot(dmix, w_ref[...].astype(BF16), 1, 1)
                lane = lax.broadcasted_iota(jnp.int32, (tr, W), 1)
                t = lax.broadcasted_iota(jnp.int32, (tr, W), 0) + k * tr
                cnt = jnp.minimum(t + 1, _pool_windows(lane)).astype(F32)
                g_scr[pl.ds(r, tr), :] = dpool / cnt
                return carry

            lax.fori_loop(0, n, rows, 0)

        r0 = pl.multiple_of(i * tr, 8)
        pooled, cnt = _pooled_block(u_ref, i, tr)
        dzc = dzc_ref[pl.ds(r0, tr), :]
        mixed = _bdot(pooled, w_ref[...])
        dsc_ref[...] += jnp.sum(dzc * mixed, axis=0, keepdims=True)
        dmix = (dzc * sc_ref[...]).astype(BF16)
        dw_ref[...] += _dot(pooled.astype(BF16), dmix, 0, 0)
        gcur = g_scr[pl.ds(r0, tr), :]
        ext = jnp.concatenate([gcur, _rows_after(g_scr, i, n, tr, 16)], axis=0)
        m = tr + 16
        s2 = ext + pltpu.roll(ext, m - 1, axis=0)
        s4 = s2 + pltpu.roll(s2, m - 2, axis=0)
        s8 = s4 + pltpu.roll(s4, m - 4, axis=0)
        s16 = s8 + pltpu.roll(s8, m - 8, axis=0)
        lane = lax.broadcasted_iota(jnp.int32, (tr, W), 1)
        wins = _pool_windows(lane)
        ahead = jnp.where(wins == 2, s2[:tr], jnp.where(wins == 4, s4[:tr], jnp.where(wins == 8, s8[:tr], s16[:tr])))
        du_ref[...] = (ahead - gcur * cnt).astype(BF16)

    return _call(body, name="pool_bwd", grid=(n,),
                 in_specs=[pl.BlockSpec((L, W), lambda i: (0, 4)), pl.BlockSpec((W, W), lambda i: (0, 0)),
                           pl.BlockSpec((1, W), lambda i: (0, 0)), pl.BlockSpec((L, W), lambda i: (0, 0))],
                 out_specs=[pl.BlockSpec((tr, W), lambda i: (i, 0)), pl.BlockSpec((W, W), lambda i: (0, 0)),
                            pl.BlockSpec((1, W), lambda i: (0, 0))],
                 out_shape=[_sds((L, W), BF16), _sds((W, W), F32), _sds((1, W), F32)],
                 scratch_shapes=[pltpu.VMEM((L, W), F32)],
                 compiler_params=_params(("arbitrary",)))(p, w_pool_bd, pool_scale, dzc)


SSM_SLAB = 512


def _ssm_prep(lam_re, lam_im, log_dt_x):
    def body(lr_ref, li_ref, ldt_ref, abr_ref, abi_ref, fr_ref, fi_ref):
        lr = jnp.minimum(lr_ref[...], LAMBDA_RE_MAX)
        li = li_ref[...]
        dt = jnp.exp(ldt_ref[...])
        mag = jnp.exp(lr * dt)
        abr = mag * jnp.cos(li * dt)
        abi = mag * jnp.sin(li * dt)
        den = lr * lr + li * li
        nr = abr - 1.0
        abr_ref[...] = abr
        abi_ref[...] = abi
        fr_ref[...] = (nr * lr + abi * li) / den
        fi_ref[...] = (abi * lr - nr * li) / den

    shp = _sds(lam_re.shape, F32)
    return _call(body, name="ssm_prep", out_shape=[shp, shp, shp, shp], compiler_params=_params())(lam_re, lam_im, log_dt_x)


def _ssm_prep_bwd(lam_re, lam_im, log_dt_x, g_abr, g_abi, g_fr, g_fi, group_sum):
    def body(lr_ref, li_ref, ldt_ref, gar_ref, gai_ref, gfr_ref, gfi_ref, gs_ref, dlr_ref, dli_ref, dldt_ref):
        lam = lr_ref[...]
        lr = jnp.minimum(lam, LAMBDA_RE_MAX)
        li = li_ref[...]
        dt = jnp.exp(ldt_ref[...])
        mag = jnp.exp(lr * dt)
        abr = mag * jnp.cos(li * dt)
        abi = mag * jnp.sin(li * dt)
        den = lr * lr + li * li
        nr = abr - 1.0
        fr = (nr * lr + abi * li) / den
        fi = (abi * lr - nr * li) / den
        d_nre = gfr_ref[...] / den
        d_nim = gfi_ref[...] / den
        d_den = -(gfr_ref[...] * fr + gfi_ref[...] * fi) / den
        d_abr = gar_ref[...] + d_nre * lr - d_nim * li
        d_abi = gai_ref[...] + d_nre * li + d_nim * lr
        d_lr = d_nre * nr + d_nim * abi + d_den * 2.0 * lr
        d_li = d_nre * abi - d_nim * nr + d_den * 2.0 * li
        d_mag = d_abr * jnp.cos(li * dt) + d_abi * jnp.sin(li * dt)
        d_th = -d_abr * abi + d_abi * abr
        d_lr = d_lr + d_mag * mag * dt
        d_li = d_li + d_th * dt
        d_dt = d_mag * mag * lr + d_th * li
        passes = jnp.where(lam < LAMBDA_RE_MAX, 1.0, jnp.where(lam == LAMBDA_RE_MAX, 0.5, 0.0))
        dlr_ref[...] = d_lr * passes
        dli_ref[...] = d_li
        dldt_ref[...] = _dot(d_dt * dt, gs_ref[...], precision=HIGH)

    shp = _sds(lam_re.shape, F32)
    return _call(body, name="ssm_prep_bwd", out_shape=[shp, shp, _sds((lam_re.shape[0], 128), F32)],
                 compiler_params=_params())(lam_re, lam_im, log_dt_x, g_abr, g_abi, g_fr, g_fi, group_sum)


def _cmul(ar, ai, br, bi):
    return ar * br - ai * bi, ar * bi + ai * br


def _powers(ar, ai):
    out = [(ar, ai)]
    for _ in range(7):
        out.append(_cmul(out[-1][0], out[-1][1], ar, ai))
    return out


def _scan_rows(s_re, s_im, ar, ai, L, reverse=False, visit=None, visit_init=None):
    n = s_re.shape[1]
    pw = _powers(ar, ai)
    row = lax.broadcasted_iota(jnp.int32, (8, n), 0)
    dist = (8 - row) if reverse else (row + 1)
    pr = jnp.zeros((8, n), F32)
    pi = jnp.zeros((8, n), F32)
    for k in range(8):
        pr = jnp.where(dist == k + 1, pw[k][0], pr)
        pi = jnp.where(dist == k + 1, pw[k][1], pi)
    nb = L // 8

    def blk(t, carry):
        cr, ci, acc = carry
        b = (nb - 1 - t) if reverse else t
        r0 = pl.multiple_of(b * 8, 8)
        xr = s_re[pl.ds(r0, 8), :]
        xi = s_im[pl.ds(r0, 8), :]
        for d in (1, 2, 4):
            if reverse:
                keep = row < 8 - d
                sr, si = pltpu.roll(xr, 8 - d, axis=0), pltpu.roll(xi, 8 - d, axis=0)
            else:
                keep = row >= d
                sr, si = pltpu.roll(xr, d, axis=0), pltpu.roll(xi, d, axis=0)
            sr = jnp.where(keep, sr, 0.0)
            si = jnp.where(keep, si, 0.0)
            mr, mi = _cmul(pw[d - 1][0], pw[d - 1][1], sr, si)
            xr, xi = xr + mr, xi + mi
        mr, mi = _cmul(pr, pi, cr, ci)
        xr, xi = xr + mr, xi + mi
        s_re[pl.ds(r0, 8), :] = xr
        s_im[pl.ds(r0, 8), :] = xi
        if visit is not None:
            acc = visit(b, xr, xi, acc)
        if reverse:
            return xr[0:1, :], xi[0:1, :], acc
        return xr[7:8, :], xi[7:8, :], acc

    zero = jnp.zeros((1, n), F32)
    return lax.fori_loop(0, nb, blk, (zero, zero, visit_init if visit is not None else 0))[2]


def _ssm_project(u_ref, wbr, wbi, s_re, s_im, L):
    ch = min(L, 256)

    def rows(k, carry):
        r = pl.multiple_of(k * ch, 8)
        ub = u_ref[pl.ds(r, ch), :].astype(BF16)
        s_re[pl.ds(r, ch), :] = _dot(ub, wbr)
        s_im[pl.ds(r, ch), :] = _dot(ub, wbi)
        return carry

    lax.fori_loop(0, L // ch, rows, 0)


def _gelu(y):
    c = math.sqrt(2.0 / math.pi)
    return 0.5 * y * (1.0 + jnp.tanh(c * (y + 0.044715 * y * y * y)))


def _gelu_grad(y):
    c = math.sqrt(2.0 / math.pi)
    th = jnp.tanh(c * (y + 0.044715 * y * y * y))
    return 0.5 * (1.0 + th) + 0.5 * y * (1.0 - th * th) * c * (1.0 + 3.0 * 0.044715 * y * y)


def _ssm_fwd(p, b_re_bd, b_im_bd, c_re_bd, c_im_bd, abr, abi, fr, fi, d_skip):
    L = p.shape[0]
    ns = NST // SSM_SLAB
    ch = min(L, 256)

    def body(u_ref, br_ref, bi_ref, cr_ref, ci_ref, abr_ref, abi_ref, fr_ref, fi_ref, d_ref,
             y_ref, zb_ref, s_re, s_im):
        j = pl.program_id(0)
        f_re, f_im = fr_ref[...], fi_ref[...]
        wbr = (f_re * br_ref[...] - f_im * bi_ref[...]).astype(BF16)
        wbi = (f_re * bi_ref[...] + f_im * br_ref[...]).astype(BF16)
        _ssm_project(u_ref, wbr, wbi, s_re, s_im, L)
        _scan_rows(s_re, s_im, abr_ref[...], abi_ref[...], L)
        crb = cr_ref[...].astype(BF16)
        cib = ci_ref[...].astype(BF16)

        def rows(k, carry):
            r = pl.multiple_of(k * ch, 8)
            part = _dot(s_re[pl.ds(r, ch), :].astype(BF16), crb) - _dot(s_im[pl.ds(r, ch), :].astype(BF16), cib)

            @pl.when(j == 0)
            def _():
                y_ref[pl.ds(r, ch), :] = part + d_ref[...] * u_ref[pl.ds(r, ch), :]

            @pl.when(j > 0)
            def _():
                y_ref[pl.ds(r, ch), :] += part

            @pl.when(j == ns - 1)
            def _():
                zb_ref[pl.ds(r, ch), :] = _gelu(y_ref[pl.ds(r, ch), :]).astype(BF16)

            return carry

        lax.fori_loop(0, L // ch, rows, 0)

    full = lambda shape: pl.BlockSpec(shape, lambda j: (0, 0))
    lanes = pl.BlockSpec((1, SSM_SLAB), lambda j: (0, j))
    return _call(body, name="ssm_fwd", grid=(ns,),
                 in_specs=[pl.BlockSpec((L, W), lambda j: (0, 3)),
                           pl.BlockSpec((W, SSM_SLAB), lambda j: (0, j)), pl.BlockSpec((W, SSM_SLAB), lambda j: (0, j)),
                           pl.BlockSpec((SSM_SLAB, W), lambda j: (j, 0)), pl.BlockSpec((SSM_SLAB, W), lambda j: (j, 0)),
                           lanes, lanes, lanes, lanes, full((1, W))],
                 out_specs=[full((L, W)), full((L, W))],
                 out_shape=[_sds((L, W), F32), _sds((L, W), BF16)],
                 scratch_shapes=[pltpu.VMEM((L, SSM_SLAB), F32), pltpu.VMEM((L, SSM_SLAB), F32)],
                 compiler_params=_params(("arbitrary",)))(p, b_re_bd, b_im_bd, c_re_bd, c_im_bd, abr, abi, fr, fi, d_skip)


def _ssm_bwd(p, y, dzb, b_re_bd, b_im_bd, c_re_bd, c_im_bd, abr, abi, fr, fi, d_skip):
    L = p.shape[0]
    ns = NST // SSM_SLAB
    ch = min(L, 256)
    n_ch = L // ch

    def body(u_ref, y_ref, dzb_ref, br_ref, bi_ref, cr_ref, ci_ref, abr_ref, abi_ref, fr_ref, fi_ref, d_ref,
             du_ref, dd_ref, dbr_ref, dbi_ref, dcr_ref, dci_ref, gar_ref, gai_ref, gfr_ref, gfi_ref,
             s_re, s_im, l_re, l_im, dy_scr, du_scr):
        j = pl.program_id(0)
        f_re, f_im = fr_ref[...], fi_ref[...]
        b_re, b_im = br_ref[...], bi_ref[...]
        wbr = (f_re * b_re - f_im * b_im).astype(BF16)
        wbi = (f_re * b_im + f_im * b_re).astype(BF16)
        a_re, a_im = abr_ref[...], abi_ref[...]

        @pl.when(j == 0)
        def _():
            def rows(k, acc):
                r = pl.multiple_of(k * ch, 8)
                dy = dzb_ref[pl.ds(r, ch), :] * _gelu_grad(y_ref[pl.ds(r, ch), :])
                dy_scr[pl.ds(r, ch), :] = dy
                du_scr[pl.ds(r, ch), :] = d_ref[...] * dy
                return acc + jnp.sum(dy * u_ref[pl.ds(r, ch), :], axis=0, keepdims=True)

            dd_ref[...] = lax.fori_loop(0, n_ch, rows, jnp.zeros((1, W), F32))

        _ssm_project(u_ref, wbr, wbi, s_re, s_im, L)
        _scan_rows(s_re, s_im, a_re, a_im, L)
        crb = cr_ref[...].astype(BF16)
        cib = ci_ref[...].astype(BF16)

        def rows_c(k, acc):
            dcr, dci = acc
            r = pl.multiple_of(k * ch, 8)
            dyb = dy_scr[pl.ds(r, ch), :].astype(BF16)
            dcr = dcr + _dot(s_re[pl.ds(r, ch), :].astype(BF16), dyb, 0, 0)
            dci = dci - _dot(s_im[pl.ds(r, ch), :].astype(BF16), dyb, 0, 0)
            l_re[pl.ds(r, ch), :] = _dot(dyb, crb, 1, 1)
            l_im[pl.ds(r, ch), :] = -_dot(dyb, cib, 1, 1)
            return dcr, dci

        zc = jnp.zeros((SSM_SLAB, W), F32)
        dcr, dci = lax.fori_loop(0, n_ch, rows_c, (zc, zc))
        dcr_ref[...] = dcr
        dci_ref[...] = dci

        row8 = lax.broadcasted_iota(jnp.int32, (8, SSM_SLAB), 0)

        def visit(b, lr, li, acc):
            ar_acc, ai_acc = acc
            r0 = pl.multiple_of(b * 8, 8)
            rp = pl.multiple_of(jnp.maximum(b * 8 - 8, 0), 8)
            has_prev = b > 0
            pr = jnp.where(has_prev, s_re[pl.ds(rp, 8), :][7:8, :], 0.0)
            pi = jnp.where(has_prev, s_im[pl.ds(rp, 8), :][7:8, :], 0.0)
            sr = jnp.where(row8 >= 1, pltpu.roll(s_re[pl.ds(r0, 8), :], 1, axis=0), pr)
            si = jnp.where(row8 >= 1, pltpu.roll(s_im[pl.ds(r0, 8), :], 1, axis=0), pi)
            return ar_acc + lr * sr + li * si, ai_acc - lr * si + li * sr

        z8 = jnp.zeros((8, SSM_SLAB), F32)
        ar_acc, ai_acc = _scan_rows(l_re, l_im, a_re, -a_im, L, reverse=True, visit=visit, visit_init=(z8, z8))
        gar_ref[...] = jnp.sum(ar_acc, axis=0, keepdims=True)
        gai_ref[...] = jnp.sum(ai_acc, axis=0, keepdims=True)

        def rows_b(k, acc):
            dwr, dwi = acc
            r = pl.multiple_of(k * ch, 8)
            ub = u_ref[pl.ds(r, ch), :].astype(BF16)
            lrb = l_re[pl.ds(r, ch), :].astype(BF16)
            lib = l_im[pl.ds(r, ch), :].astype(BF16)
            du_scr[pl.ds(r, ch), :] += _dot(lrb, wbr, 1, 1) + _dot(lib, wbi, 1, 1)
            return dwr + _dot(ub, lrb, 0, 0), dwi + _dot(ub, lib, 0, 0)

        zb = jnp.zeros((W, SSM_SLAB), F32)
        dwr, dwi = lax.fori_loop(0, n_ch, rows_b, (zb, zb))
        dbr_ref[...] = dwr * f_re + dwi * f_im
        dbi_ref[...] = -dwr * f_im + dwi * f_re
        gfr_ref[...] = jnp.sum(dwr * b_re + dwi * b_im, axis=0, keepdims=True)
        gfi_ref[...] = jnp.sum(-dwr * b_im + dwi * b_re, axis=0, keepdims=True)

        @pl.when(j == ns - 1)
        def _():
            du_ref[...] = du_scr[...].astype(BF16)

    full = lambda shape: pl.BlockSpec(shape, lambda j: (0, 0))
    lanes = pl.BlockSpec((1, SSM_SLAB), lambda j: (0, j))
    bspec = pl.BlockSpec((W, SSM_SLAB), lambda j: (0, j))
    cspec = pl.BlockSpec((SSM_SLAB, W), lambda j: (j, 0))
    slab = lambda: pltpu.VMEM((L, SSM_SLAB), F32)
    return _call(body, name="ssm_bwd", grid=(ns,),
                 in_specs=[pl.BlockSpec((L, W), lambda j: (0, 3)), full((L, W)), full((L, W)),
                           bspec, bspec, cspec, cspec, lanes, lanes, lanes, lanes, full((1, W))],
                 out_specs=[full((L, W)), full((1, W)), bspec, bspec, cspec, cspec, lanes, lanes, lanes, lanes],
                 out_shape=[_sds((L, W), BF16), _sds((1, W), F32), _sds((W, NST), F32), _sds((W, NST), F32),
                            _sds((NST, W), F32), _sds((NST, W), F32)] + [_sds((1, NST), F32)] * 4,
                 scratch_shapes=[slab(), slab(), slab(), slab(), pltpu.VMEM((L, W), F32), pltpu.VMEM((L, W), F32)],
                 compiler_params=_params(("arbitrary",)))(p, y, dzb, b_re_bd, b_im_bd, c_re_bd, c_im_bd,
                                                          abr, abi, fr, fi, d_skip)


SB_KB = 512


SB_SUB = 256


def _split2(x):
    hi = x.astype(BF16)
    return hi, (x - hi.astype(F32)).astype(BF16)


def _ones_dot(x, ones):
    n = x.shape[0]
    r = _dot(jnp.concatenate(_split2(x), axis=0), ones)
    return r[:n] + r[n:]


def _suffix_sums(x, tri):
    sub = tri.shape[0]
    parts = [_ones_dot(x[:, i:i + sub], tri) for i in range(0, x.shape[1], sub)]
    out, after = [], None
    for p in reversed(parts):
        out.append(p if after is None else p + after)
        after = p[:, 0:1] if after is None else after + p[:, 0:1]
    return jnp.concatenate(out[::-1], axis=1)


def _prefix_sums_exclusive(x, tri_le):
    sub = tri_le.shape[0]
    out, before = [], None
    for i in range(0, x.shape[1], sub):
        xi = x[:, i:i + sub]
        inc = _ones_dot(xi, tri_le)
        out.append(inc - xi if before is None else inc - xi + before)
        before = inc[:, sub - 1:sub] if before is None else before + inc[:, sub - 1:sub]
    return jnp.concatenate(out, axis=1)


def _sb_block(q, kj, i, jb, kb, right, tri):
    z = _bdot(q, kj, 1, 1)
    t_idx = lax.broadcasted_iota(jnp.int32, (QB, kb), 0) + i * QB
    s_idx = lax.broadcasted_iota(jnp.int32, (QB, kb), 1) + jb * kb
    mask = s_idx < t_idx
    lk_all = jnp.minimum(-z, 0.0) - jnp.log1p(jnp.exp(-jnp.abs(z)))
    lk = jnp.where(mask, lk_all, 0.0)
    suf = _suffix_sums(lk, tri)
    a = jnp.where(mask, jnp.exp((lk_all + z) + (suf - lk) + right), 0.0)
    return z, mask, suf, a


def _sb_ones(kb):
    sub = min(SB_SUB, kb)
    r = lax.broadcasted_iota(jnp.int32, (sub, sub), 0)
    c = lax.broadcasted_iota(jnp.int32, (sub, sub), 1)
    return (r >= c).astype(BF16), (r <= c).astype(BF16)


def _sb_fwd(q, k, v):
    L = q.shape[1]
    kb = min(SB_KB, L)
    per = kb // QB

    def body(q_ref, k_ref, v_ref, o_ref, rs_ref):
        i = pl.program_id(0)
        tri, _ = _sb_ones(kb)
        lane = lax.broadcasted_iota(jnp.int32, (QB, 128), 1)
        qs = [q_ref[h] for h in range(HEADS)]

        def step(t, carry):
            accs, rights, sums = carry
            jb = i // per - t
            r = pl.multiple_of(jb * kb, kb)
            out = []
            for h in range(HEADS):
                _, _, suf, a = _sb_block(qs[h], k_ref[h, pl.ds(r, kb), :], i, jb, kb, rights[h], tri)
                tot = suf[:, 0:1]
                out.append((accs[h] + _bdot(a, v_ref[h, pl.ds(r, kb), :]), rights[h] + tot,
                            sums[h] + jnp.where(lane == jb, tot, 0.0)))
            return tuple(o[0] for o in out), tuple(o[1] for o in out), tuple(o[2] for o in out)

        init = (tuple(jnp.zeros((QB, HD), F32) for _ in range(HEADS)), tuple(jnp.zeros((QB, 1), F32) for _ in range(HEADS)),
                tuple(jnp.zeros((QB, 128), F32) for _ in range(HEADS)))
        accs, _, sums = lax.fori_loop(0, i // per + 1, step, init)
        for h in range(HEADS):
            o_ref[h] = accs[h]
            rs_ref[h] = sums[h]

    heads = pl.BlockSpec((HEADS, L, HD), lambda i: (0, 0, 0))
    blk = pl.BlockSpec((HEADS, QB, HD), lambda i: (0, i, 0))
    return _call(body, name="sb_fwd", grid=(L // QB,), in_specs=[blk, heads, heads],
                 out_specs=[blk, pl.BlockSpec((HEADS, QB, 128), lambda i: (0, i, 0))],
                 out_shape=[_sds((HEADS, L, HD), F32), _sds((HEADS, L, 128), F32)],
                 compiler_params=_params(("parallel",)))(q, k, v)


def _sb_bwd(q, k, v, do, block_sums):
    L = q.shape[1]
    kb = min(SB_KB, L)
    per = kb // QB

    def body(q_ref, k_ref, v_ref, do_ref, rs_ref, dq_ref, dk_ref, dv_ref):
        i = pl.program_id(0)
        tri, tri_le = _sb_ones(kb)
        lane = lax.broadcasted_iota(jnp.int32, (QB, 128), 1)

        @pl.when(i == 0)
        def _():
            dk_ref[...] = jnp.zeros_like(dk_ref)
            dv_ref[...] = jnp.zeros_like(dv_ref)

        qs = [q_ref[h] for h in range(HEADS)]
        dos = [do_ref[h] for h in range(HEADS)]
        sums = [rs_ref[h] for h in range(HEADS)]

        def step(jb, carry):
            dqs, lefts = carry
            r = pl.multiple_of(jb * kb, kb)
            out = []
            for h in range(HEADS):
                kj = k_ref[h, pl.ds(r, kb), :]
                vj = v_ref[h, pl.ds(r, kb), :]
                right = jnp.sum(jnp.where(lane > jb, sums[h], 0.0), axis=1, keepdims=True)
                z, mask, _, a = _sb_block(qs[h], kj, i, jb, kb, right, tri)
                e = a * _bdot(dos[h], vj, 1, 1)
                dv_ref[h, pl.ds(r, kb), :] += _dot(a.astype(BF16), dos[h].astype(BF16), 0, 0)
                before = lefts[h] + _prefix_sums_exclusive(e, tri_le)
                sg = _sigmoid(z)
                dz = jnp.where(mask, e * (1.0 - sg) - sg * before, 0.0).astype(BF16)
                dk_ref[h, pl.ds(r, kb), :] += _dot(dz, qs[h].astype(BF16), 0, 0)
                out.append((dqs[h] + _dot(dz, kj.astype(BF16)), lefts[h] + jnp.sum(e, axis=1, keepdims=True)))
            return tuple(o[0] for o in out), tuple(o[1] for o in out)

        init = (tuple(jnp.zeros((QB, HD), F32) for _ in range(HEADS)), tuple(jnp.zeros((QB, 1), F32) for _ in range(HEADS)))
        dqs, _ = lax.fori_loop(0, i // per + 1, step, init)
        for h in range(HEADS):
            dq_ref[h] = dqs[h]

    heads = pl.BlockSpec((HEADS, L, HD), lambda i: (0, 0, 0))
    blk = pl.BlockSpec((HEADS, QB, HD), lambda i: (0, i, 0))
    shp = _sds((HEADS, L, HD), F32)
    return _call(body, name="sb_bwd", grid=(L // QB,),
                 in_specs=[blk, heads, heads, blk, pl.BlockSpec((HEADS, QB, 128), lambda i: (0, i, 0))],
                 out_specs=[blk, heads, heads], out_shape=[shp, shp, shp],
                 compiler_params=_params(("arbitrary",)))(q, k, v, do, block_sums)


def _merge_fwd(za, zb, zc, zd, p, w_conv_out, w_glu, w_pool_out, w_sb_out):
    L = za.shape[0]
    tm = _row_tile(L)

    def body(za_ref, zb_ref, zc_ref, zd_ref, g0, g1, g2, g3, wc_ref, wg_ref, wp_ref, ws_ref, o_ref):
        glu = _dot(zb_ref[...], wg_ref[...])
        yb = glu[:, :D] * _sigmoid(glu[:, D:])
        m = _sigmoid(g0[...]) * _dot(za_ref[...], wc_ref[...])
        m = m + _sigmoid(g1[...]) * yb
        m = m + _sigmoid(g2[...]) * _dot(zc_ref[...], wp_ref[...])
        m = m + _sigmoid(g3[...]) * _dot(zd_ref[...], ws_ref[...])
        o_ref[...] = m.astype(BF16)

    zt = pl.BlockSpec((tm, W), lambda i: (i, 0))
    gate = lambda b: pl.BlockSpec((tm, D), lambda i: (i, 2 + b))
    wfull = lambda n: pl.BlockSpec((W, n), lambda i: (0, 0))
    return _call(body, name="merge_fwd", grid=(L // tm,),
                 in_specs=[zt, zt, zt, zt, gate(0), gate(1), gate(2), gate(3), wfull(D), wfull(2 * D), wfull(D), wfull(D)],
                 out_specs=pl.BlockSpec((tm, D), lambda i: (i, 0)), out_shape=_sds((L, D), BF16),
                 compiler_params=_params(("parallel",)))(za, zb, zc, zd, p, p, p, p, w_conv_out, w_glu, w_pool_out, w_sb_out)


def _merge_bwd(dm, za, zb, zc, zd, p, w_conv_out, w_glu, w_pool_out, w_sb_out):
    L = za.shape[0]
    tm = _row_tile(L)
    n = L // tm

    def body(dm_ref, za_ref, zb_ref, zc_ref, zd_ref, g0, g1, g2, g3, wc_ref, wg_ref, wp_ref, ws_ref,
             dza_ref, dzb_ref, dzc_ref, dzd_ref, dg_ref, dwc_ref, dwg_ref, dwp_ref, dws_ref,
             awc, awg, awp, aws):
        i = pl.program_id(0)

        @pl.when(i == 0)
        def _():
            awc[...] = jnp.zeros_like(awc)
            awg[...] = jnp.zeros_like(awg)
            awp[...] = jnp.zeros_like(awp)
            aws[...] = jnp.zeros_like(aws)

        dmv = dm_ref[...]

        def gated(g_ref, y, col):
            s = _sigmoid(g_ref[...])
            dg_ref[:, col * D:(col + 1) * D] = (dmv * y * s * (1.0 - s)).astype(BF16)
            return (dmv * s)

        def linear(z_ref, w_ref, acc, dz_ref, col, g_ref):
            zv = z_ref[...]
            dy = gated(g_ref, _dot(zv, w_ref[...]), col).astype(BF16)
            dz_ref[...] = _dot(dy, w_ref[...], 1, 1)
            acc[...] += _dot(zv, dy, 0, 0)

        linear(za_ref, wc_ref, awc, dza_ref, 0, g0)
        linear(zc_ref, wp_ref, awp, dzc_ref, 2, g2)
        linear(zd_ref, ws_ref, aws, dzd_ref, 3, g3)
        zbv = zb_ref[...]
        glu = _dot(zbv, wg_ref[...])
        ga = glu[:, :D]
        sg = _sigmoid(glu[:, D:])
        dyb = gated(g1, ga * sg, 1)
        dga = (dyb * sg).astype(BF16)
        dgg = (dyb * ga * sg * (1.0 - sg)).astype(BF16)
        dzb_ref[...] = _dot(dga, wg_ref[:, :D], 1, 1) + _dot(dgg, wg_ref[:, D:], 1, 1)
        awg[:, :D] += _dot(zbv, dga, 0, 0)
        awg[:, D:] += _dot(zbv, dgg, 0, 0)

        @pl.when(i == n - 1)
        def _():
            dwc_ref[...] = awc[...].astype(BF16)
            dwg_ref[...] = awg[...].astype(BF16)
            dwp_ref[...] = awp[...].astype(BF16)
            dws_ref[...] = aws[...].astype(BF16)

    zt = pl.BlockSpec((tm, W), lambda i: (i, 0))
    gate = lambda b: pl.BlockSpec((tm, D), lambda i: (i, 2 + b))
    wfull = lambda n_: pl.BlockSpec((W, n_), lambda i: (0, 0))
    zs = _sds((L, W), F32)
    return _call(body, name="merge_bwd", grid=(n,),
                 in_specs=[pl.BlockSpec((tm, D), lambda i: (i, 0)), zt, zt, zt, zt, gate(0), gate(1), gate(2), gate(3),
                           wfull(D), wfull(2 * D), wfull(D), wfull(D)],
                 out_specs=[zt, zt, zt, zt, pl.BlockSpec((tm, 4 * D), lambda i: (i, 0)),
                            wfull(D), wfull(2 * D), wfull(D), wfull(D)],
                 out_shape=[zs, zs, zs, zs, _sds((L, 4 * D), BF16),
                            _sds((W, D), BF16), _sds((W, 2 * D), BF16), _sds((W, D), BF16), _sds((W, D), BF16)],
                 scratch_shapes=[pltpu.VMEM((W, D), F32), pltpu.VMEM((W, 2 * D), F32), pltpu.VMEM((W, D), F32),
                                 pltpu.VMEM((W, D), F32)],
                 compiler_params=_params(("arbitrary",)))(dm, za, zb, zc, zd, p, p, p, p,
                                                          w_conv_out, w_glu, w_pool_out, w_sb_out)


def _adam_math(w, g, m, v):
    m2 = ADAM_B1 * m + (1.0 - ADAM_B1) * g
    v2 = ADAM_B2 * v + (1.0 - ADAM_B2) * (g * g)
    m_hat = m2 / (1.0 - ADAM_B1 ** ADAM_STEP)
    v_hat = v2 / (1.0 - ADAM_B2 ** ADAM_STEP)
    return -ADAM_LR * (m_hat / (jnp.sqrt(v_hat) + ADAM_EPS) + ADAM_WD * w), m2, v2


def _as_rows(a):
    return a.reshape(-1, a.shape[-1])


def _adamw(w, g, m, v):
    shape = w.shape
    w2, g2, m2, v2 = _as_rows(w), _as_rows(g), _as_rows(m), _as_rows(v)
    R, C = w2.shape
    tr = R
    for cand in (1024, 512, 256, 128, 64, 32, 16, 8):
        if R % cand == 0 and cand * C * 4 <= 2 * 1024 * 1024:
            tr = cand
            break

    def body(w_ref, g_ref, m_ref, v_ref, d_ref, m_out, v_out):
        d, mn, vn = _adam_math(w_ref[...], g_ref[...], m_ref[...], v_ref[...])
        d_ref[...] = d
        m_out[...] = mn
        v_out[...] = vn

    blk = pl.BlockSpec((tr, C), lambda i: (i, 0))
    shp = _sds((R, C), F32)
    outs = _call(body, name="adamw", grid=(R // tr,), in_specs=[blk] * 4, out_specs=[blk] * 3, out_shape=[shp] * 3,
                 compiler_params=_params(("parallel",)))(w2, g2, m2, v2)
    return tuple(o.reshape(shape) for o in outs)


def _sum_parts(parts, out_dtype, name):
    shape = parts[0].shape
    flat = [_as_rows(a) for a in parts]
    R, C = flat[0].shape
    tr = R
    for cand in (1024, 512, 256, 128, 64, 32, 16):
        if R % cand == 0 and cand * C * 4 <= 2 * 1024 * 1024:
            tr = cand
            break
    k = len(parts)

    def body(*refs):
        acc = refs[0][...].astype(F32)
        for r in refs[1:k]:
            acc = acc + r[...].astype(F32)
        refs[k][...] = acc.astype(out_dtype)

    blk = pl.BlockSpec((tr, C), lambda i: (i, 0))
    out = _call(body, name=name, grid=(R // tr,), in_specs=[blk] * k, out_specs=blk, out_shape=_sds((R, C), out_dtype),
                compiler_params=_params(("parallel",)))(*flat)
    return out.reshape(shape)


ADA_SHARD = 9 * D // N_CHIP
ADA_TN = 768


def _ada_fwd(c_pad, w_ada, b_ada_cols):
    depth = w_ada.shape[0]

    def body(c_ref, w_ref, b_ref, o_ref):
        cv = c_ref[...]
        o_ref[...] = _bdot(cv * _sigmoid(cv), w_ref[...]) + b_ref[...]

    return _call(body, name="ada_fwd", grid=(depth, ADA_SHARD // ADA_TN),
                 in_specs=[pl.BlockSpec((16, D), lambda l, j: (0, 0)),
                           pl.BlockSpec((None, D, ADA_TN), lambda l, j: (l, 0, j)),
                           pl.BlockSpec((None, 1, ADA_TN), lambda l, j: (l, 0, j))],
                 out_specs=pl.BlockSpec((None, 16, ADA_TN), lambda l, j: (l, 0, j)),
                 out_shape=_sds((depth, 16, ADA_SHARD), F32),
                 compiler_params=_params(("parallel", "parallel")))(c_pad, w_ada, b_ada_cols)


def _ada_wgrad(c_pad, d_ada):
    depth = d_ada.shape[0]

    def body(c_ref, d_ref, o_ref):
        cv = c_ref[...]
        o_ref[...] = _bdot(cv * _sigmoid(cv), d_ref[...], 0, 0)

    return _call(body, name="ada_wgrad", grid=(depth, ADA_SHARD // ADA_TN),
                 in_specs=[pl.BlockSpec((16, D), lambda l, j: (0, 0)),
                           pl.BlockSpec((None, 16, ADA_TN), lambda l, j: (l, 0, j))],
                 out_specs=pl.BlockSpec((None, D, ADA_TN), lambda l, j: (l, 0, j)),
                 out_shape=_sds((depth, D, ADA_SHARD), F32),
                 compiler_params=_params(("parallel", "parallel")))(c_pad, d_ada)


HBM_SPEC = pl.BlockSpec(memory_space=pltpu.HBM)


def _place():
    x, y, c = lax.axis_index("x"), lax.axis_index("y"), lax.axis_index("c")
    peers = [(1 - x, y), (x, 1 - y), (1 - x, 1 - y)]
    return x, y, c, peers


def _chip(px, py):
    return 2 * px + py


def _allgather8(block, name):
    m_per, n = block.shape

    def body(x_ref, out_ref, send_sems, recv_sems, local_sem):
        x, y, c, chips = _place()
        me, sibling = (x, y, c), (x, y, 1 - c)

        def rows(px, py, pc):
            return out_ref.at[pl.ds(pl.multiple_of((4 * px + 2 * py + pc) * m_per, 8), m_per), :]

        def copy(k, blk, to, src=None):
            return pltpu.make_async_remote_copy(
                src_ref=rows(*blk) if src is None else src, dst_ref=rows(*blk),
                send_sem=send_sems.at[k], recv_sem=recv_sems.at[k], device_id=to, device_id_type=MESH)

        mine = pltpu.make_async_copy(x_ref, rows(*me), local_sem)
        mine.start()
        first = [copy(0, me, sibling, src=x_ref)]
        first += [copy(1 + j, me, (*chip, c), src=x_ref) for j, chip in enumerate(chips)]
        for cp in first:
            cp.start()
        passed = [copy(4 + j, (*chip, c), sibling) for j, chip in enumerate(chips)]
        for j, chip in enumerate(chips):
            copy(1 + j, (*chip, c), me).wait_recv()
            passed[j].start()
        copy(0, sibling, me).wait_recv()
        for j, chip in enumerate(chips):
            copy(4 + j, (*chip, 1 - c), me).wait_recv()
        for cp in first + passed:
            cp.wait_send()
        mine.wait()

    return _call(body, name=name, out_shape=_sds((N_DEV * m_per, n), block.dtype),
                 in_specs=[pl.BlockSpec(memory_space=pltpu.VMEM)], out_specs=pl.BlockSpec(memory_space=pltpu.VMEM),
                 scratch_shapes=[pltpu.SemaphoreType.DMA((7,)), pltpu.SemaphoreType.DMA((7,)), pltpu.SemaphoreType.DMA],
                 compiler_params=_params())(block)


GATHERED = (("w_ff_in", 0, -1, 0), ("w_ff_out", 0, -2, 0),
            ("w_in", None, -1, 1), ("w_conv_out", None, -1, 1), ("w_glu", None, -1, 1), ("w_pool_out", None, -1, 1),
            ("w_sb_out", None, -1, 1), ("w_out", None, -2, 1),
            ("w_ff_in", 1, -1, 2), ("w_ff_out", 1, -2, 2))
N_SUB = 3


def _lead(ref):
    return (slice(None),) * (len(ref.shape) - 2)


def _mo(v, m):
    return v if isinstance(v, int) else pl.multiple_of(v, m)


def _full_region(ref, axis, j, half, shard_shape):
    rs, cs = shard_shape[-2], shard_shape[-1]
    if axis == -1:
        r0, nr = (0, rs) if half is None else (half * (rs // 2), rs // 2)
        return ref.at[_lead(ref) + (pl.ds(_mo(r0, 16), nr), pl.ds(_mo(j * cs, 128), cs))]
    r0, nr = (j * rs, rs) if half is None else (j * rs + half * (rs // 2), rs // 2)
    return ref.at[_lead(ref) + (pl.ds(_mo(r0, 16), nr), slice(None))]


def _shard_half(ref, half):
    rs = ref.shape[-2]
    return ref.at[_lead(ref) + (pl.ds(_mo(half * (rs // 2), 16), rs // 2), slice(None))]


def _full_shape(shard_shape, axis):
    s = list(shard_shape)
    s[axis] *= N_CHIP
    return tuple(s)


class _Lay:
    def __init__(self, shard_shape, axis):
        self.axis = axis
        self.shard_shape = tuple(shard_shape)
        self.full_shape = _full_shape(shard_shape, axis)
        self.lead = int(np.prod(shard_shape[:-2]))
        self.rs, self.cs = shard_shape[-2], shard_shape[-1]
        self.hr = self.rs // 2
        self.tr = next(t for t in (512, 256, 128, 64, 32, 16) if self.hr % t == 0 and t * self.cs * 4 <= (1 << 21))
        self.half_rows_shape = _half_rows_shape(self.full_shape)
        self.half_shard_shape = _half_rows_shape(self.shard_shape)

    def full(self, jf, hf):
        if self.axis == -1:
            return ((self.lead, 2, self.hr, N_CHIP * self.cs),
                    pl.BlockSpec((None, None, self.tr, self.cs), lambda b, j, i, s: (b, hf(j, s), i, jf(j, s))))
        return ((self.lead, N_CHIP, 2, self.hr, self.cs),
                pl.BlockSpec((None, None, None, self.tr, self.cs), lambda b, j, i, s: (b, jf(j, s), hf(j, s), i, 0)))

    def half_shard(self):
        return (self.lead, self.hr, self.cs), pl.BlockSpec((None, self.tr, self.cs), lambda b, j, i, s: (b, i, 0))

    def shard(self, hf):
        return ((self.lead, 2, self.hr, self.cs),
                pl.BlockSpec((None, None, self.tr, self.cs), lambda b, j, i, s: (b, hf(j, s), i, 0)))


def _view_sum(sel, operands, out_view, out_shape, out_dtype, grid, name):
    k = len(operands)

    def body(sel_ref, *refs):
        acc = refs[0][...].astype(F32)
        for r in refs[1:k]:
            acc = acc + r[...].astype(F32)
        refs[k][...] = acc.astype(out_dtype)

    spec = pltpu.PrefetchScalarGridSpec(num_scalar_prefetch=1, grid=grid, in_specs=[v[1] for _, v in operands],
                                        out_specs=out_view[1])
    out = _call(body, name=name, grid_spec=spec, out_shape=_sds(out_view[0], out_dtype),
                compiler_params=_params(("parallel", "parallel", "parallel")))(
                    sel, *[a.reshape(v[0]) for a, v in operands])
    return out.reshape(out_shape)


def _sel_core(j, s):
    return s[0]


def _sel_chip(j, s):
    return s[1]


def _grid_j(j, s):
    return j


def _place_shard(lay, sel, w):
    return _view_sum(sel, [(w, lay.shard(_grid_j))], lay.full(_sel_chip, _grid_j), lay.full_shape, BF16,
                     (lay.lead, 2, lay.hr // lay.tr), "place_shard")


SEM_SPEC = pl.BlockSpec(memory_space=pltpu.SEMAPHORE)
ANY_SPEC = pl.BlockSpec(memory_space=pl.ANY)
SPLIT_COPY = pltpu.SideEffectType.DATAFLOW_SIDE_EFFECTING


def _in_hbm(a):
    return pltpu.with_memory_space_constraint(a, pltpu.HBM)


def _gather_copies(lays, bufs, send_sems, recv_sems):
    x, y, c, chips = _place()
    copies = []
    for a, lay in enumerate(lays):
        own = _full_region(bufs[a], lay.axis, _chip(x, y), c, lay.shard_shape)
        for k, chip in enumerate(chips):
            copies.append(pltpu.make_async_remote_copy(
                src_ref=own, dst_ref=own, send_sem=send_sems.at[a * 3 + k], recv_sem=recv_sems.at[a * 3 + k],
                device_id=(*chip, c), device_id_type=MESH))
    return copies


def _gather_start(fulls, after, lays, tag):
    n = len(fulls)

    def body(*refs):
        send_sems, recv_sems = refs[n + 1], refs[n + 2]
        bufs, token = refs[n + 3:2 * n + 3], refs[2 * n + 3]
        for cp in _gather_copies(lays, bufs, send_sems, recv_sems):
            cp.start()
        token[...] = jnp.zeros_like(token)

    outs = _call(body, name="gather_start_" + tag,
                 out_shape=[pltpu.SemaphoreType.DMA((3 * n,)), pltpu.SemaphoreType.DMA((3 * n,))]
                 + [pltpu.HBM(f.shape, f.dtype) for f in fulls] + [_sds((8, 128), F32)],
                 in_specs=[HBM_SPEC] * n + [ANY_SPEC],
                 out_specs=[SEM_SPEC, SEM_SPEC] + [HBM_SPEC] * n + [pl.BlockSpec(memory_space=pltpu.VMEM)],
                 input_output_aliases={a: a + 2 for a in range(n)},
                 compiler_params=pltpu.CompilerParams(has_side_effects=SPLIT_COPY))(*[_in_hbm(f) for f in fulls], after)
    return outs[0], outs[1], outs[2:2 + n], outs[2 + n]


def _gather_wait(send_sems, recv_sems, bufs, after, lays, tag):
    n = len(bufs)

    def body(*refs):
        ss, rs = refs[n], refs[n + 1]
        for cp in _gather_copies(lays, refs[n + 3:], ss, rs):
            cp.wait_send()
            cp.wait_recv()

    return _call(body, name="gather_wait_" + tag,
                 out_shape=[pltpu.HBM(b.shape, b.dtype) for b in bufs],
                 in_specs=[HBM_SPEC] * n + [SEM_SPEC, SEM_SPEC, ANY_SPEC], out_specs=[HBM_SPEC] * n,
                 input_output_aliases={a: a for a in range(n)},
                 compiler_params=pltpu.CompilerParams(has_side_effects=SPLIT_COPY))(*bufs, send_sems, recv_sems, after)


def _gather_forward(bufs, lays):
    n = len(bufs)

    def body(*refs):
        outs = refs[n:2 * n]
        send_sems, recv_sems = refs[2 * n:]
        x, y, c, chips = _place()
        sibling = (x, y, 1 - c)
        sends = []
        for a in range(n):
            for k, chip in enumerate(chips):
                landed = _full_region(outs[a], lays[a].axis, _chip(*chip), c, lays[a].shard_shape)
                cp = pltpu.make_async_remote_copy(
                    src_ref=landed, dst_ref=landed, send_sem=send_sems.at[a * 3 + k], recv_sem=recv_sems.at[a * 3 + k],
                    device_id=sibling, device_id_type=MESH)
                cp.start()
                sends.append(cp)
        for a in range(n):
            for k, chip in enumerate(chips):
                passed = _full_region(outs[a], lays[a].axis, _chip(*chip), 1 - c, lays[a].shard_shape)
                pltpu.make_async_remote_copy(
                    src_ref=passed, dst_ref=passed, send_sem=send_sems.at[a * 3 + k], recv_sem=recv_sems.at[a * 3 + k],
                    device_id=sibling, device_id_type=MESH).wait_recv()
        for cp in sends:
            cp.wait_send()

    return _call(body, name="gather_forward",
                 out_shape=[_sds(b.shape, b.dtype) for b in bufs],
                 in_specs=[HBM_SPEC] * n, out_specs=[HBM_SPEC] * n,
                 input_output_aliases={a: a for a in range(n)},
                 scratch_shapes=[pltpu.SemaphoreType.DMA((3 * n,)), pltpu.SemaphoreType.DMA((3 * n,))],
                 compiler_params=_params())(*bufs)


def _half_rows_shape(full_shape):
    s = list(full_shape)
    s[-2] //= 2
    return tuple(s)


RELATIONS = tuple((r, s) for r in range(N_CHIP) for s in range(2))[1:]


def _peer(x, y, c, rel):
    r, s = rel
    return (1 - x if r in (1, 3) else x, 1 - y if r in (2, 3) else y, 1 - c if s else c)


def _reduce_copies(lays, grads, landing, send_sems, recv_sems):
    x, y, c, _ = _place()
    nr = len(RELATIONS)
    copies = []
    for a, lay in enumerate(lays):
        for k, rel in enumerate(RELATIONS):
            px, py, pc = _peer(x, y, c, rel)
            copies.append(pltpu.make_async_remote_copy(
                src_ref=_full_region(grads[a], lay.axis, _chip(px, py), pc, lay.shard_shape), dst_ref=landing[a * nr + k],
                send_sem=send_sems.at[a * nr + k], recv_sem=recv_sems.at[a * nr + k],
                device_id=(px, py, pc), device_id_type=MESH))
    return copies


def _reduce_start(grads, lays, tag):
    n, nr = len(grads), len(RELATIONS)
    landing = [_in_hbm(lax.empty(lay.half_shard_shape, BF16)) for lay in lays for _ in RELATIONS]
    m = n + n * nr

    def body(*refs):
        send_sems, recv_sems = refs[m], refs[m + 1]
        src, land, token = refs[m + 2:m + 2 + n], refs[m + 2 + n:2 * m + 2], refs[2 * m + 2]
        for cp in _reduce_copies(lays, src, land, send_sems, recv_sems):
            cp.start()
        token[...] = jnp.zeros_like(token)

    ops = [_in_hbm(g) for g in grads] + landing
    outs = _call(body, name="reduce_start_" + tag,
                 out_shape=[pltpu.SemaphoreType.DMA((n * nr,)), pltpu.SemaphoreType.DMA((n * nr,))]
                 + [pltpu.HBM(o.shape, o.dtype) for o in ops] + [_sds((8, 128), F32)],
                 in_specs=[HBM_SPEC] * m,
                 out_specs=[SEM_SPEC, SEM_SPEC] + [HBM_SPEC] * m + [pl.BlockSpec(memory_space=pltpu.VMEM)],
                 input_output_aliases={a: a + 2 for a in range(m)},
                 compiler_params=pltpu.CompilerParams(has_side_effects=SPLIT_COPY))(*ops)
    return outs[0], outs[1], outs[2:2 + n], outs[2 + n:2 + m], outs[2 + m]


def _reduce_wait(send_sems, recv_sems, grads, landing, after, lays, tag):
    n, nr = len(grads), len(RELATIONS)
    m = n + n * nr

    def body(*refs):
        ss, rs = refs[m], refs[m + 1]
        src, land = refs[m + 3:m + 3 + n], refs[m + 3 + n:]
        for cp in _reduce_copies(lays, src, land, ss, rs):
            cp.wait_send()
            cp.wait_recv()

    ops = list(grads) + list(landing)
    outs = _call(body, name="reduce_wait_" + tag,
                 out_shape=[pltpu.HBM(o.shape, o.dtype) for o in ops],
                 in_specs=[HBM_SPEC] * m + [SEM_SPEC, SEM_SPEC, ANY_SPEC], out_specs=[HBM_SPEC] * m,
                 input_output_aliases={a: a for a in range(m)},
                 compiler_params=pltpu.CompilerParams(has_side_effects=SPLIT_COPY))(*ops, send_sems, recv_sems, after)
    return outs[:n], [outs[n + nr * a:n + nr * a + nr] for a in range(n)]


def _share_halves(shards):
    n = len(shards)

    def body(*refs):
        outs = refs[n:2 * n]
        send_sems, recv_sems = refs[2 * n:]
        x, y, c, _ = _place()
        sibling = (x, y, 1 - c)
        started = []
        for a in range(n):
            mine = _shard_half(outs[a], c)
            rc = pltpu.make_async_remote_copy(src_ref=mine, dst_ref=mine, send_sem=send_sems.at[a],
                                              recv_sem=recv_sems.at[a], device_id=sibling, device_id_type=MESH)
            rc.start()
            started.append(rc)
        for rc in started:
            rc.wait_recv()
            rc.wait_send()

    return _call(body, name="share_halves", out_shape=[_sds(s.shape, F32) for s in shards],
                 in_specs=[HBM_SPEC] * n, out_specs=[HBM_SPEC] * n, input_output_aliases={a: a for a in range(n)},
                 scratch_shapes=[pltpu.SemaphoreType.DMA((n,)), pltpu.SemaphoreType.DMA((n,))],
                 compiler_params=_params())(*shards)


def _reduce_end(state, after, lays, sel, tag):
    send_sems, recv_sems, grads, landing, _ = state
    grads, landed = _reduce_wait(send_sems, recv_sems, grads, landing, after, lays, tag)
    halves = [
        _view_sum(sel, [(g, lay.full(_sel_chip, _sel_core))] + [(l, lay.half_shard()) for l in ls], lay.shard(_sel_core),
                  lay.shard_shape, F32, (lay.lead, 1, lay.hr // lay.tr), "shard_half_sum")
        for g, ls, lay in zip(grads, landed, lays)]
    return _share_halves(halves)


def _embed(blocks):
    n, r, c = blocks.shape
    eye = jnp.eye(n, dtype=blocks.dtype)
    return (blocks[:, :, None, :] * eye[:, None, :, None]).reshape(n * r, n * c)


def _unembed(mat, n):
    r, c = mat.shape[0] // n, mat.shape[1] // n
    return jnp.transpose(jnp.diagonal(mat.reshape(n, r, n, c), axis1=0, axis2=2), (2, 0, 1))


def _to_heads(a):
    return jnp.transpose(a.reshape(a.shape[0], HEADS, HD), (1, 0, 2))


def _from_heads(a):
    return jnp.transpose(a, (1, 0, 2)).reshape(a.shape[1], W)


def _row(v):
    return v.reshape(1, -1)


def _concat_cols(pieces):
    L = pieces[0].shape[0]
    widths = [p.shape[1] for p in pieces]
    tr = _wide_tile(L)

    def body(*refs):
        off = 0
        for r, w in zip(refs[:-1], widths):
            refs[-1][:, off:off + w] = r[...].astype(BF16)
            off += w

    return _call(body, name="concat_cols", grid=(L // tr,),
                 in_specs=[pl.BlockSpec((tr, w), lambda i: (i, 0)) for w in widths],
                 out_specs=pl.BlockSpec((tr, sum(widths)), lambda i: (i, 0)), out_shape=_sds((L, sum(widths)), BF16),
                 compiler_params=_params(("parallel",)))(*pieces)


def _ffn_fwd(x, ada, gp, gq, w_in, w_out, s):
    L = x.shape[0]
    h = _norm_mod(x, _row(gp[s]), _row(ada[3 * s]), _row(ada[3 * s + 1]))
    a, b, act = _ffn_in(h, w_in)
    f = _mm(act, w_out, M=L, N=D, K=FF, tm=min(L, 1024), tn=512, name="ffn_out")
    x2 = _post(x, f, _row(gq[s]), _row(ada[3 * s + 2]), 0.5)
    return x2, (x, h, a, b, act, f)


def _ffn_bwd(dx, saved, ada, gp, gq, w_in, w_out, s, stats):
    x, h, a, b, act, f = saved
    L = x.shape[0]
    df, stats = _post_bwd(dx, f, _row(gq[s]), _row(ada[3 * s + 2]), 0.5, stats, s)
    dw_out = _mm(act, df, M=FF, N=D, K=L, tm=1408, tn=1024, ta=True, out_dtype=BF16, name="ffn_dw_out")
    da, db = _ffn_mid_bwd(df, w_out, a, b)
    du = _concat_cols([da, db])
    dw_in = _mm(h, du, M=D, N=2 * FF, K=L, tm=1024, tn=512, ta=True, out_dtype=BF16, name="ffn_dw_in")
    dh = _mm(du, w_in, M=L, N=D, K=2 * FF, tm=min(L, 1024), tn=1024, tk=2816, tb=True, name="ffn_dh")
    dx2, stats = _norm_mod_bwd(dh, x, _row(gp[s]), _row(ada[3 * s + 1]), dx, stats, s)
    return dx2, dw_in, dw_out, stats


def _mixer_fwd(x, ada, gp, gq, wf, sm):
    L = x.shape[0]
    h = _norm_mod(x, _row(gp[1]), _row(ada[3]), _row(ada[4]))
    p = _mm(h, wf["w_in"], M=L, N=IN_COLS, K=D, tm=min(L, 2048), tn=512, name="mixer_in")
    za = _conv_fwd(p, sm["conv_w"])
    y, zb = _ssm_fwd(p, sm["b_re"], sm["b_im"], sm["c_re"], sm["c_im"], sm["abr"], sm["abi"], sm["fr"], sm["fi"], sm["ssm_d"])
    zc = _pool_fwd(p, sm["w_pool"], sm["pool_scale"])
    q = _to_heads(p[:, 5 * W:6 * W]) * (HD ** -0.5)
    k = _to_heads(p[:, 6 * W:7 * W])
    v = _to_heads(p[:, 7 * W:8 * W])
    o_heads, block_sums = _sb_fwd(q, k, v)
    zd = _from_heads(o_heads).astype(BF16)
    merged = _merge_fwd(za, zb, zc, zd, p, wf["w_conv_out"], wf["w_glu"], wf["w_pool_out"], wf["w_sb_out"])
    m = _mm(merged, wf["w_out"], M=L, N=D, K=D, tm=min(L, 1024), tn=512, name="mixer_out")
    x2 = _post(x, m, _row(gq[1]), _row(ada[5]), 1.0)
    return x2, (x, h, p, za, y, zb, zc, zd, q, k, v, block_sums, merged, m)


def _mixer_bwd(dx, saved, ada, gp, gq, wf, sm, stats):
    x, h, p, za, y, zb, zc, zd, q, k, v, block_sums, merged, m = saved
    L = x.shape[0]
    dmf, stats = _post_bwd(dx, m, _row(gq[1]), _row(ada[5]), 1.0, stats, 1)
    dw_out = _mm(merged, dmf, M=D, N=D, K=L, tm=512, tn=512, ta=True, out_dtype=BF16, name="mixer_dw_out")
    dmerged = _mm(dmf, wf["w_out"], M=L, N=D, K=D, tm=min(L, 1024), tn=512, tb=True, name="mixer_dmerged")
    dza, dzb, dzc, dzd, dgates, dwc, dwg, dwp, dws = _merge_bwd(
        dmerged, za, zb, zc, zd, p, wf["w_conv_out"], wf["w_glu"], wf["w_pool_out"], wf["w_sb_out"])
    dconv, dconv_w = _conv_bwd(p, sm["conv_w"], dza)
    (du_ssm, dd, dbr, dbi, dcr, dci, gar, gai, gfr, gfi) = _ssm_bwd(
        p, y, dzb, sm["b_re"], sm["b_im"], sm["c_re"], sm["c_im"], sm["abr"], sm["abi"], sm["fr"], sm["fi"], sm["ssm_d"])
    du_pool, dwpool, dpscale = _pool_bwd(p, sm["w_pool"], sm["pool_scale"], dzc)
    dq, dk, dv = _sb_bwd(q, k, v, _to_heads(dzd), block_sums)
    dqkv = [_from_heads(t) for t in (dq * (HD ** -0.5), dk, dv)]
    dp = _concat_cols([dconv, du_ssm, du_pool] + dqkv + [dgates])
    dw_in = _mm(h, dp, M=D, N=IN_COLS, K=L, tm=1024, tn=512, ta=True, out_dtype=BF16, name="mixer_dw_in")
    dh = _mm(dp, wf["w_in"], M=L, N=D, K=IN_COLS, tm=min(L, 1024), tn=1024, tk=3072, tb=True, name="mixer_dh")
    dx2, stats = _norm_mod_bwd(dh, x, _row(gp[1]), _row(ada[4]), dx, stats, 1)
    wgrads = [dw_in, dwc, dwg, dwp, dws, dw_out]
    small = {"conv_w": dconv_w, "ssm_d": dd, "b_re": dbr, "b_im": dbi, "c_re": dcr, "c_im": dci,
             "abr": gar, "abi": gai, "fr": gfr, "fi": gfi, "w_pool": dwpool, "pool_scale": dpscale}
    return dx2, wgrads, small, stats


def _pack(arrays):
    flat = jnp.concatenate([a.reshape(-1) for a in arrays])
    rows = -(-flat.shape[0] // 128)
    rows = -(-rows // 64) * 64
    return jnp.pad(flat, (0, rows * 128 - flat.shape[0])).reshape(rows, 128)


def _unpack(block, shapes):
    flat = block.reshape(-1)
    out, off = [], 0
    for s in shapes:
        n = int(np.prod(s))
        out.append(flat[off:off + n].reshape(s))
        off += n
    return out


def _pad_rows(a, mult):
    rows = -(-a.shape[0] // mult) * mult
    return jnp.concatenate([a] * (-(-rows // a.shape[0])), axis=0)[:rows]


SMALL_ORDER = ("stats", "conv_w", "lam_re", "lam_im", "log_dt", "ssm_b_re", "ssm_b_im",
               "ssm_c_re", "ssm_c_im", "ssm_d", "w_pool", "pool_scale")
WEIGHTS = ('w_ada', 'b_ada', 'g_pre', 'g_post', 'w_ff_in', 'w_ff_out', 'w_in', 'conv_w', 'w_conv_out', 'lam_re', 'lam_im',
           'log_dt', 'ssm_b_re', 'ssm_b_im', 'ssm_c_re', 'ssm_c_im', 'ssm_d', 'w_glu', 'w_pool', 'pool_scale', 'w_pool_out',
           'w_sb_out', 'w_out')


def _step(a):
    depth = a["w_ada"].shape[0]
    x = a["x"][0]
    target = a["loss_target"][0]
    L = x.shape[0]
    ix, iy, ic = lax.axis_index("x"), lax.axis_index("y"), lax.axis_index("c")
    chip = 2 * ix + iy
    me = 4 * ix + 2 * iy + ic
    sel = jnp.stack([ic, chip]).astype(jnp.int32)
    lays = [_Lay(a[name].shape[(1 if idx is None else 2):], ax) for name, idx, ax, _ in GATHERED]

    def entries(g):
        return [i for i, e in enumerate(GATHERED) if e[3] == g]

    def shard_of(i, l):
        name, idx = GATHERED[i][0], GATHERED[i][1]
        return a[name][l] if idx is None else a[name][l, idx]

    stages = [(l, g) for l in range(depth) for g in range(N_SUB)]

    def gather_begin(t, after):
        l, g = stages[t]
        placed = [_place_shard(lays[i], sel, shard_of(i, l)) for i in entries(g)]
        return _gather_start(placed, after, [lays[i] for i in entries(g)], str(t))

    pending = {t: gather_begin(t, x) for t in range(min(2, len(stages)))}
    started = sum(p[3][0, 0] for p in pending.values())

    first_shapes = [(D,), (depth, 3, W), (depth, 3, W), (depth, 3, W // N_CHIP)]
    gathered = _allgather8(_pack([a["c"] + started, a["g_pre"], a["g_post"], a["conv_w"]]), "gather_small_inputs")
    per_dev = [_unpack(blk, first_shapes) for blk in gathered.reshape(N_DEV, -1, 128)]
    c_all = jnp.stack([d[0] for d in per_dev])
    c_pad = jnp.concatenate([c_all, jnp.zeros_like(c_all)], axis=0)
    g_pre = jnp.concatenate([per_dev[2 * j][1] for j in range(N_CHIP)], axis=-1)
    g_post = jnp.concatenate([per_dev[2 * j][2] for j in range(N_CHIP)], axis=-1)
    conv_w = jnp.concatenate([per_dev[2 * j][3] for j in range(N_CHIP)], axis=-1)

    b_cols = lax.dynamic_slice(a["b_ada"], (0, chip * ADA_SHARD), (depth, ADA_SHARD)).reshape(depth, 1, ADA_SHARD)
    ada_part = _ada_fwd(c_pad, a["w_ada"], b_cols)
    ada_all = _allgather8(ada_part.reshape(depth * 16, ADA_SHARD), "gather_ada").reshape(N_DEV, depth, 16, ADA_SHARD)
    ada_rows = lax.dynamic_slice(ada_all, (0, 0, me, 0), (N_DEV, depth, 1, ADA_SHARD))[:, :, 0]
    ada = jnp.concatenate([ada_rows[2 * j] for j in range(N_CHIP)], axis=-1).reshape(depth, 9, D)

    lam_re = _pad_rows(a["lam_re"].reshape(depth, NST), 8)
    lam_im = _pad_rows(a["lam_im"].reshape(depth, NST), 8)
    log_dt_x = _pad_rows(jnp.repeat(a["log_dt"], GP, axis=1), 8)
    abr, abi, fr, fi = _ssm_prep(lam_re, lam_im, log_dt_x)

    def small_of(l):
        return {"conv_w": conv_w[l], "ssm_d": _row(a["ssm_d"][l]), "pool_scale": _row(a["pool_scale"][l]),
                "b_re": _embed(jnp.transpose(a["ssm_b_re"][l], (0, 2, 1))), "b_im": _embed(jnp.transpose(a["ssm_b_im"][l], (0, 2, 1))),
                "c_re": _embed(jnp.transpose(a["ssm_c_re"][l], (0, 2, 1))), "c_im": _embed(jnp.transpose(a["ssm_c_im"][l], (0, 2, 1))),
                "w_pool": _embed(a["w_pool"][l]),
                "abr": abr[l:l + 1], "abi": abi[l:l + 1], "fr": fr[l:l + 1], "fi": fi[l:l + 1]}

    saved, weights, smalls = [], [], [small_of(l) for l in range(depth)]
    for t, (l, g) in enumerate(stages):
        glays = [lays[i] for i in entries(g)]
        send_sems, recv_sems, bufs, _ = pending.pop(t)
        w = _gather_forward(_gather_wait(send_sems, recv_sems, bufs, x if t else ada, glays, str(t)), glays)
        weights.append(w)
        ada_l = ada[l]
        if t + 2 < len(stages):
            pending[t + 2] = gather_begin(t + 2, x)
            ada_l = ada_l + pending[t + 2][3][0, 0]
        if g == 1:
            wf = {GATHERED[i][0]: wi for i, wi in zip(entries(1), w)}
            x, sv = _mixer_fwd(x, ada_l, g_pre[l], g_post[l], wf, smalls[l])
        else:
            x, sv = _ffn_fwd(x, ada_l, g_pre[l], g_post[l], w[0], w[1], g)
        saved.append(sv)
    dx, loss_part = _loss_head(x, target)
    loss = lax.psum(loss_part[0, 0], ("x", "y", "c"))

    shard_grads = [[None] * depth for _ in GATHERED]
    small_grads = [{} for _ in range(depth)]
    stats = [jnp.zeros((STAT_ROWS, D), F32) for _ in range(depth)]
    states = {}

    def reduce_finish(t, after):
        l, g = stages[t]
        glays = [lays[i] for i in entries(g)]
        for i, grad in zip(entries(g), _reduce_end(states.pop(t), after, glays, sel, str(t))):
            shard_grads[i][l] = grad

    token = None
    for t in reversed(range(len(stages))):
        l, g = stages[t]
        ada_l = ada[l] if token is None else ada[l] + token[0, 0]
        if g == 1:
            wf = {GATHERED[i][0]: wi for i, wi in zip(entries(1), weights[t])}
            dx, wgrads, small, stats[l] = _mixer_bwd(dx, saved[t], ada_l, g_pre[l], g_post[l], wf, smalls[l], stats[l])
            small_grads[l].update(small)
        else:
            dx, dw_in, dw_out, stats[l] = _ffn_bwd(dx, saved[t], ada_l, g_pre[l], g_post[l], weights[t][0], weights[t][1], g,
                                                   stats[l])
            wgrads = [dw_in, dw_out]
        states[t] = _reduce_start(wgrads, [lays[i] for i in entries(g)], str(t))
        token = states[t][4]
        if t + 2 in states:
            reduce_finish(t + 2, dx)
    stack = lambda key: _pad_rows(jnp.concatenate([small_grads[l][key] for l in range(depth)], axis=0), 8)
    gs = np.zeros((NST, 128), np.float32)
    gs[np.arange(NST), np.arange(NST) // GP] = 1.0
    dlr, dli, dldt = _ssm_prep_bwd(lam_re, lam_im, log_dt_x, stack("abr"), stack("abi"), stack("fr"), stack("fi"), jnp.asarray(gs))
    part = {
        "stats": jnp.stack(stats) + sum(st[4][0, 0] for st in states.values()),
        "conv_w": jnp.stack([small_grads[l]["conv_w"] for l in range(depth)]),
        "lam_re": dlr[:depth].reshape(depth, G, GP), "lam_im": dli[:depth].reshape(depth, G, GP), "log_dt": dldt[:depth, :G],
        "ssm_b_re": jnp.stack([jnp.transpose(_unembed(small_grads[l]["b_re"], G), (0, 2, 1)) for l in range(depth)]),
        "ssm_b_im": jnp.stack([jnp.transpose(_unembed(small_grads[l]["b_im"], G), (0, 2, 1)) for l in range(depth)]),
        "ssm_c_re": jnp.stack([jnp.transpose(_unembed(small_grads[l]["c_re"], G), (0, 2, 1)) for l in range(depth)]),
        "ssm_c_im": jnp.stack([jnp.transpose(_unembed(small_grads[l]["c_im"], G), (0, 2, 1)) for l in range(depth)]),
        "ssm_d": jnp.stack([small_grads[l]["ssm_d"][0] for l in range(depth)]),
        "w_pool": jnp.stack([_unembed(small_grads[l]["w_pool"], len(POOL_WINDOWS)) for l in range(depth)]),
        "pool_scale": jnp.stack([small_grads[l]["pool_scale"][0] for l in range(depth)]),
    }
    small_shapes = [part[k].shape for k in SMALL_ORDER]
    blocks = _allgather8(_pack([part[k] for k in SMALL_ORDER]), "gather_small_grads").reshape(N_DEV, -1, 128)
    small_sum = _sum_parts([blocks[i] for i in range(N_DEV)], F32, "small_grad_sum")
    total = dict(zip(SMALL_ORDER, _unpack(small_sum, small_shapes)))
    d_ada_all = jnp.stack([_unpack(blocks[i], small_shapes[:1])[0][:, :9].reshape(depth, 9 * D) for i in range(N_DEV)])
    d_cols = lax.dynamic_slice(d_ada_all, (0, 0, chip * ADA_SHARD), (N_DEV, depth, ADA_SHARD))
    d_cols = jnp.transpose(d_cols, (1, 0, 2))
    grads = {"w_ada": _ada_wgrad(c_pad, jnp.concatenate([d_cols, jnp.zeros_like(d_cols)], axis=1)),
             "b_ada": total["stats"][:, :9].reshape(depth, 9 * D),
             "g_pre": lax.dynamic_slice(total["stats"], (0, 9, chip * W), (depth, 3, W)),
             "g_post": lax.dynamic_slice(total["stats"], (0, 12, chip * W), (depth, 3, W)),
             "conv_w": lax.dynamic_slice(total["conv_w"], (0, 0, chip * (W // N_CHIP)), (depth, 3, W // N_CHIP))}
    for k in SMALL_ORDER[2:]:
        grads[k] = total[k]

    out = {"loss": loss, "grad_x": dx[None]}

    def update(name):
        out["grad_" + name] = grads[name]
        out["delta_" + name], out["new_m_" + name], out["new_v_" + name] = _adamw(a[name], grads[name], a["m_" + name], a["v_" + name])

    for name in WEIGHTS:
        if name in grads:
            update(name)
    for t in sorted(states, reverse=True):
        reduce_finish(t, out["delta_w_ada"])
    for name in sorted({e[0] for e in GATHERED}):
        cols = [shard_grads[i] for i, e in enumerate(GATHERED) if e[0] == name]
        grads[name] = jnp.stack(cols[0]) if len(cols) == 1 else jnp.stack([jnp.stack(pair) for pair in zip(*cols)])
        update(name)
    return out


def kernel(x, c, w_ada, b_ada, g_pre, g_post, w_ff_in, w_ff_out, w_in, conv_w, w_conv_out, lam_re, lam_im, log_dt, ssm_b_re, ssm_b_im, ssm_c_re, ssm_c_im, ssm_d, w_glu, w_pool, pool_scale, w_pool_out, w_sb_out, w_out, loss_target, m_w_ada, m_b_ada, m_g_pre, m_g_post, m_w_ff_in, m_w_ff_out, m_w_in, m_conv_w, m_w_conv_out, m_lam_re, m_lam_im, m_log_dt, m_ssm_b_re, m_ssm_b_im, m_ssm_c_re, m_ssm_c_im, m_ssm_d, m_w_glu, m_w_pool, m_pool_scale, m_w_pool_out, m_w_sb_out, m_w_out, v_w_ada, v_b_ada, v_g_pre, v_g_post, v_w_ff_in, v_w_ff_out, v_w_in, v_conv_w, v_w_conv_out, v_lam_re, v_lam_im, v_log_dt, v_ssm_b_re, v_ssm_b_im, v_ssm_c_re, v_ssm_c_im, v_ssm_d, v_w_glu, v_w_pool, v_pool_scale, v_w_pool_out, v_w_sb_out, v_w_out):
    out = _step(dict(locals()))
    names = ["loss", "grad_x"] + [p + n for p in ("grad_", "delta_", "new_m_", "new_v_") for n in WEIGHTS]
    return tuple(out[n] for n in names)
```

```python
import functools
import math

import jax
import jax.numpy as jnp
import numpy as np
from jax import lax
from jax.experimental import pallas as pl
from jax.experimental.pallas import tpu as pltpu

F32 = jnp.float32
BF16 = jnp.bfloat16
MESH = pl.DeviceIdType.MESH

D = 1024
W = 256
FF = 2816
IN_COLS = 6144
G = 16
GH = 16
GP = 64
NST = G * GP
QB = 128
HEADS = 4
HD = 64
EPS = 1e-6
LAMBDA_RE_MAX = -1e-4
POOL_WINDOWS = (2, 4, 8, 16)
N_CHIP = 4
N_DEV = 8
VMEM_LIMIT = 56 * 1024 * 1024
HIGH = lax.Precision.HIGHEST

ADAM_LR, ADAM_B1, ADAM_B2, ADAM_EPS, ADAM_WD, ADAM_STEP = 0.001, 0.9, 0.999, 1e-08, 0.01, 10


def _call(body, **kw):
    return pl.pallas_call(body, **kw)


def _params(dims=None, **kw):
    return pltpu.CompilerParams(dimension_semantics=dims, vmem_limit_bytes=VMEM_LIMIT, **kw)


def _sds(shape, dtype):
    return jax.ShapeDtypeStruct(shape, dtype)


def _dot(a, b, ca=1, cb=0, precision=None):
    return lax.dot_general(a, b, (((ca,), (cb,)), ((), ())), preferred_element_type=F32, precision=precision)


def _bdot(a, b, ca=1, cb=0):
    return _dot(a.astype(BF16), b.astype(BF16), ca, cb)


def _sigmoid(x):
    return 1.0 / (1.0 + jnp.exp(-x))


def _mm(a, b, *, M, N, K, tm, tn, tk=None, ta=False, tb=False, out_dtype=F32, a_off=(0, 0), b_off=(0, 0), name):
    tk = K if tk is None else tk
    nk = K // tk
    assert M % tm == 0 and N % tn == 0 and K % tk == 0

    def body(a_ref, b_ref, o_ref, *acc):
        part = _bdot(a_ref[...], b_ref[...], 0 if ta else 1, 1 if tb else 0)
        if nk == 1:
            o_ref[...] = part.astype(out_dtype)
            return
        acc_ref = acc[0]
        k = pl.program_id(2)

        @pl.when(k == 0)
        def _():
            acc_ref[...] = part

        @pl.when(k > 0)
        def _():
            acc_ref[...] += part

        @pl.when(k == nk - 1)
        def _():
            o_ref[...] = acc_ref[...].astype(out_dtype)

    if ta:
        a_spec = pl.BlockSpec((tk, tm), lambda i, j, k: (k + a_off[0], i + a_off[1]))
    else:
        a_spec = pl.BlockSpec((tm, tk), lambda i, j, k: (i + a_off[0], k + a_off[1]))
    if tb:
        b_spec = pl.BlockSpec((tn, tk), lambda i, j, k: (j + b_off[0], k + b_off[1]))
    else:
        b_spec = pl.BlockSpec((tk, tn), lambda i, j, k: (k + b_off[0], j + b_off[1]))
    return _call(
        body, name=name, grid=(M // tm, N // tn, nk),
        in_specs=[a_spec, b_spec],
        out_specs=pl.BlockSpec((tm, tn), lambda i, j, k: (i, j)),
        out_shape=_sds((M, N), out_dtype),
        scratch_shapes=[] if nk == 1 else [pltpu.VMEM((tm, tn), F32)],
        compiler_params=_params(("parallel", "parallel", "arbitrary")),
    )(a, b)


def _row_tile(L):
    return min(L, 256)


def _wide_tile(L):
    return min(L, 512)


def _norm_mod(x, g, shift, scale):
    L = x.shape[0]
    tr = _wide_tile(L)

    def body(x_ref, g_ref, sh_ref, sc_ref, h_ref):
        xv = x_ref[...]
        r = lax.rsqrt(jnp.mean(xv * xv, axis=-1, keepdims=True) + EPS)
        h_ref[...] = (xv * r * g_ref[...] * (1.0 + sc_ref[...]) + sh_ref[...]).astype(BF16)

    row = pl.BlockSpec((tr, D), lambda i: (i, 0))
    vec = pl.BlockSpec((1, D), lambda i: (0, 0))
    return _call(body, name="norm_mod", grid=(L // tr,), in_specs=[row, vec, vec, vec], out_specs=row,
                 out_shape=_sds((L, D), BF16), compiler_params=_params(("parallel",)))(x, g, shift, scale)


STAT_ROWS = 16


def _norm_mod_bwd(dh, x, g, scale, dx_res, stats, s):
    L = x.shape[0]
    tr = _wide_tile(L)

    def body(dh_ref, x_ref, g_ref, sc_ref, dxr_ref, stin_ref, dx_ref, st_ref):
        i = pl.program_id(0)
        xv = x_ref[...]
        dhv = dh_ref[...]
        r = lax.rsqrt(jnp.mean(xv * xv, axis=-1, keepdims=True) + EPS)
        y = xv * r
        n = y * g_ref[...]
        dn = dhv * (1.0 + sc_ref[...])
        dy = dn * g_ref[...]
        dx_ref[...] = dxr_ref[...] + r * (dy - y * jnp.mean(dy * y, axis=-1, keepdims=True))

        @pl.when(i == 0)
        def _():
            st_ref[...] = stin_ref[...]

        st_ref[3 * s:3 * s + 1, :] += jnp.sum(dhv, axis=0, keepdims=True)
        st_ref[3 * s + 1:3 * s + 2, :] += jnp.sum(dhv * n, axis=0, keepdims=True)
        st_ref[9 + s:10 + s, :] += jnp.sum(dn * y, axis=0, keepdims=True)

    row = pl.BlockSpec((tr, D), lambda i: (i, 0))
    vec = pl.BlockSpec((1, D), lambda i: (0, 0))
    st = pl.BlockSpec((STAT_ROWS, D), lambda i: (0, 0))
    return _call(body, name="norm_mod_bwd", grid=(L // tr,), in_specs=[row, row, vec, vec, row, st],
                 out_specs=[row, st], out_shape=[_sds((L, D), F32), _sds((STAT_ROWS, D), F32)],
                 input_output_aliases={5: 1},
                 compiler_params=_params(("arbitrary",)))(dh, x, g, scale, dx_res, stats)


def _post(x, f, g, gate, res_weight):
    L = x.shape[0]
    tr = _wide_tile(L)

    def body(x_ref, f_ref, g_ref, gt_ref, o_ref):
        fv = f_ref[...]
        r = lax.rsqrt(jnp.mean(fv * fv, axis=-1, keepdims=True) + EPS)
        o_ref[...] = x_ref[...] + (res_weight * (1.0 + gt_ref[...])) * (fv * r * g_ref[...])

    row = pl.BlockSpec((tr, D), lambda i: (i, 0))
    vec = pl.BlockSpec((1, D), lambda i: (0, 0))
    return _call(body, name="post", grid=(L // tr,), in_specs=[row, row, vec, vec], out_specs=row,
                 out_shape=_sds((L, D), F32), compiler_params=_params(("parallel",)))(x, f, g, gate)


def _post_bwd(dx, f, g, gate, res_weight, stats, s):
    L = dx.shape[0]
    tr = _wide_tile(L)

    def body(dx_ref, f_ref, g_ref, gt_ref, stin_ref, df_ref, st_ref):
        i = pl.program_id(0)
        fv = f_ref[...]
        dxv = dx_ref[...]
        r = lax.rsqrt(jnp.mean(fv * fv, axis=-1, keepdims=True) + EPS)
        y = fv * r
        dn = dxv * (res_weight * (1.0 + gt_ref[...]))
        dy = dn * g_ref[...]
        df_ref[...] = (r * (dy - y * jnp.mean(dy * y, axis=-1, keepdims=True))).astype(BF16)

        @pl.when(i == 0)
        def _():
            st_ref[...] = stin_ref[...]

        st_ref[3 * s + 2:3 * s + 3, :] += res_weight * jnp.sum(dxv * (y * g_ref[...]), axis=0, keepdims=True)
        st_ref[12 + s:13 + s, :] += jnp.sum(dn * y, axis=0, keepdims=True)

    row = pl.BlockSpec((tr, D), lambda i: (i, 0))
    vec = pl.BlockSpec((1, D), lambda i: (0, 0))
    st = pl.BlockSpec((STAT_ROWS, D), lambda i: (0, 0))
    return _call(body, name="post_bwd", grid=(L // tr,), in_specs=[row, row, vec, vec, st],
                 out_specs=[row, st], out_shape=[_sds((L, D), BF16), _sds((STAT_ROWS, D), F32)],
                 input_output_aliases={4: 1},
                 compiler_params=_params(("arbitrary",)))(dx, f, g, gate, stats)


def _loss_head(x, target):
    L = x.shape[0]
    tr = _wide_tile(L)

    def body(x_ref, t_ref, dx_ref, loss_ref):
        i = pl.program_id(0)
        err = x_ref[...] - t_ref[...]
        dx_ref[...] = err * (1.0 / D)

        @pl.when(i == 0)
        def _():
            loss_ref[...] = jnp.zeros_like(loss_ref)

        loss_ref[...] += 0.5 * jnp.sum(jnp.mean(err * err, axis=-1, keepdims=True), axis=0, keepdims=True)

    row = pl.BlockSpec((tr, D), lambda i: (i, 0))
    return _call(body, name="loss_head", grid=(L // tr,), in_specs=[row, row],
                 out_specs=[row, pl.BlockSpec((1, 1), lambda i: (0, 0))],
                 out_shape=[_sds((L, D), F32), _sds((1, 1), F32)],
                 compiler_params=_params(("arbitrary",)))(x, target)


def _ffn_in(h, w_in):
    L = h.shape[0]
    tm, tn = min(L, 2048), 256
    nf = FF // tn

    def body(h_ref, wa_ref, wb_ref, a_ref, b_ref, act_ref):
        hv = h_ref[...]
        a = _dot(hv, wa_ref[...])
        b = _dot(hv, wb_ref[...])
        a_ref[...] = a
        b_ref[...] = b
        act_ref[...] = (a * _sigmoid(a) * b).astype(BF16)

    tile = pl.BlockSpec((tm, tn), lambda i, j: (i, j))
    return _call(body, name="ffn_in", grid=(L // tm, nf),
                 in_specs=[pl.BlockSpec((tm, D), lambda i, j: (i, 0)),
                           pl.BlockSpec((D, tn), lambda i, j: (0, j)),
                           pl.BlockSpec((D, tn), lambda i, j: (0, j + nf))],
                 out_specs=[tile, tile, tile],
                 out_shape=[_sds((L, FF), F32), _sds((L, FF), F32), _sds((L, FF), BF16)],
                 compiler_params=_params(("parallel", "parallel")))(h, w_in, w_in)


def _ffn_mid_bwd(df, w_out, a, b):
    L = df.shape[0]
    tm, tn = min(L, 2048), 256

    def body(df_ref, w_ref, a_ref, b_ref, da_ref, db_ref):
        dact = _dot(df_ref[...], w_ref[...], 1, 1)
        av = a_ref[...]
        sg = _sigmoid(av)
        da_ref[...] = (dact * b_ref[...] * (sg * (1.0 + av * (1.0 - sg)))).astype(BF16)
        db_ref[...] = (dact * (av * sg)).astype(BF16)

    tile = pl.BlockSpec((tm, tn), lambda i, j: (i, j))
    return _call(body, name="ffn_mid_bwd", grid=(L // tm, FF // tn),
                 in_specs=[pl.BlockSpec((tm, D), lambda i, j: (i, 0)),
                           pl.BlockSpec((tn, D), lambda i, j: (j, 0)), tile, tile],
                 out_specs=[tile, tile],
                 out_shape=[_sds((L, FF), BF16), _sds((L, FF), BF16)],
                 compiler_params=_params(("parallel", "parallel")))(df, w_out, a, b)


def _rows_before(ref, i, tr, halo):
    start = pl.multiple_of(jnp.maximum(i * tr - halo, 0), 8)
    return jnp.where(i > 0, ref[pl.ds(start, halo), :], 0.0)


def _rows_after(ref, i, n, tr, halo):
    start = pl.multiple_of(jnp.minimum((i + 1) * tr, (n - 1) * tr), 8)
    return jnp.where(i < n - 1, ref[pl.ds(start, halo), :], 0.0)


def _conv_fwd(p, conv_w):
    L = p.shape[0]
    tr = _row_tile(L)
    n = L // tr

    def body(bg_ref, cg_ref, v_ref, w_ref, za_ref, u_scr):
        i = pl.program_id(0)

        @pl.when(i == 0)
        def _():
            u_scr[...] = cg_ref[...] * v_ref[...]

        r0 = pl.multiple_of(i * tr, 8)
        ext = jnp.concatenate([_rows_before(u_scr, i, tr, 8), u_scr[pl.ds(r0, tr), :]], axis=0)
        w = w_ref[...]
        y = (w[0:1] * pltpu.roll(ext, 2, axis=0) + w[1:2] * pltpu.roll(ext, 1, axis=0) + w[2:3] * ext)[8:, :]
        za_ref[...] = (bg_ref[pl.ds(r0, tr), :] * y).astype(BF16)

    col = lambda c: pl.BlockSpec((L, W), lambda i: (0, c))
    return _call(body, name="conv_fwd", grid=(n,),
                 in_specs=[col(0), col(1), col(2), pl.BlockSpec((3, W), lambda i: (0, 0))],
                 out_specs=pl.BlockSpec((tr, W), lambda i: (i, 0)),
                 out_shape=_sds((L, W), BF16),
                 scratch_shapes=[pltpu.VMEM((L, W), F32)],
                 compiler_params=_params(("arbitrary",)))(p, p, p, conv_w)


def _conv_bwd(p, conv_w, dza):
    L = p.shape[0]
    tr = _row_tile(L)
    n = L // tr

    def body(bg_ref, cg_ref, v_ref, w_ref, dza_ref, dp_ref, dw_ref, u_scr, dy_scr):
        i = pl.program_id(0)

        @pl.when(i == 0)
        def _():
            u_scr[...] = cg_ref[...] * v_ref[...]
            dy_scr[...] = dza_ref[...] * bg_ref[...]
            dw_ref[...] = jnp.zeros_like(dw_ref)

        r0 = pl.multiple_of(i * tr, 8)
        w = w_ref[...]
        ext = jnp.concatenate([_rows_before(u_scr, i, tr, 8), u_scr[pl.ds(r0, tr), :]], axis=0)
        u2 = pltpu.roll(ext, 2, axis=0)[8:, :]
        u1 = pltpu.roll(ext, 1, axis=0)[8:, :]
        u0 = ext[8:, :]
        y = w[0:1] * u2 + w[1:2] * u1 + w[2:3] * u0
        dy = dy_scr[pl.ds(r0, tr), :]
        dext = jnp.concatenate([dy, _rows_after(dy_scr, i, n, tr, 8)], axis=0)
        m = tr + 8
        du = (w[2:3] * dext + w[1:2] * pltpu.roll(dext, m - 1, axis=0) + w[0:1] * pltpu.roll(dext, m - 2, axis=0))[:tr, :]
        dp_ref[:, 0:W] = (dza_ref[pl.ds(r0, tr), :] * y).astype(BF16)
        dp_ref[:, W:2 * W] = (du * v_ref[pl.ds(r0, tr), :]).astype(BF16)
        dp_ref[:, 2 * W:3 * W] = (du * cg_ref[pl.ds(r0, tr), :]).astype(BF16)
        dw_ref[...] += jnp.concatenate([jnp.sum(dy * u2, axis=0, keepdims=True),
                                        jnp.sum(dy * u1, axis=0, keepdims=True),
                                        jnp.sum(dy * u0, axis=0, keepdims=True)], axis=0)

    col = lambda c: pl.BlockSpec((L, W), lambda i: (0, c))
    return _call(body, name="conv_bwd", grid=(n,),
                 in_specs=[col(0), col(1), col(2), pl.BlockSpec((3, W), lambda i: (0, 0)),
                           pl.BlockSpec((L, W), lambda i: (0, 0))],
                 out_specs=[pl.BlockSpec((tr, 3 * W), lambda i: (i, 0)), pl.BlockSpec((3, W), lambda i: (0, 0))],
                 out_shape=[_sds((L, 3 * W), BF16), _sds((3, W), F32)],
                 scratch_shapes=[pltpu.VMEM((L, W), F32), pltpu.VMEM((L, W), F32)],
                 compiler_params=_params(("arbitrary",)))(p, p, p, conv_w, dza)


def _pool_windows(lane):
    wins = jnp.zeros(lane.shape, jnp.int32)
    for gi, w in enumerate(POOL_WINDOWS):
        wins = jnp.where(lane // (W // len(POOL_WINDOWS)) == gi, w, wins)
    return wins


def _pooled_block(u_ref, i, tr):
    r0 = pl.multiple_of(i * tr, 8)
    cur = u_ref[pl.ds(r0, tr), :]
    ext = jnp.concatenate([_rows_before(u_ref, i, tr, 16), cur], axis=0)
    s2 = ext + pltpu.roll(ext, 1, axis=0)
    s4 = s2 + pltpu.roll(s2, 2, axis=0)
    s8 = s4 + pltpu.roll(s4, 4, axis=0)
    s16 = s8 + pltpu.roll(s8, 8, axis=0)
    lane = lax.broadcasted_iota(jnp.int32, (tr, W), 1)
    wins = _pool_windows(lane)
    win_sum = jnp.where(wins == 2, s2[16:], jnp.where(wins == 4, s4[16:], jnp.where(wins == 8, s8[16:], s16[16:])))
    t = lax.broadcasted_iota(jnp.int32, (tr, W), 0) + i * tr
    cnt = jnp.minimum(t + 1, wins).astype(F32)
    return win_sum / cnt - cur, cnt


def _pool_fwd(p, w_pool_bd, pool_scale):
    L = p.shape[0]
    tr = _row_tile(L)

    def body(u_ref, w_ref, sc_ref, zc_ref):
        pooled, _ = _pooled_block(u_ref, pl.program_id(0), tr)
        zc_ref[...] = (_bdot(pooled, w_ref[...]) * sc_ref[...]).astype(BF16)

    return _call(body, name="pool_fwd", grid=(L // tr,),
                 in_specs=[pl.BlockSpec((L, W), lambda i: (0, 4)), pl.BlockSpec((W, W), lambda i: (0, 0)),
                           pl.BlockSpec((1, W), lambda i: (0, 0))],
                 out_specs=pl.BlockSpec((tr, W), lambda i: (i, 0)), out_shape=_sds((L, W), BF16),
                 compiler_params=_params(("arbitrary",)))(p, w_pool_bd, pool_scale)


def _pool_bwd(p, w_pool_bd, pool_scale, dzc):
    L = p.shape[0]
    tr = _row_tile(L)
    n = L // tr

    def body(u_ref, w_ref, sc_ref, dzc_ref, du_ref, dw_ref, dsc_ref, g_scr):
        i = pl.program_id(0)

        @pl.when(i == 0)
        def _():
            dw_ref[...] = jnp.zeros_like(dw_ref)
            dsc_ref[...] = jnp.zeros_like(dsc_ref)

            def rows(k, carry):
                r = pl.multiple_of(k * tr, 8)
                dmix = (dzc_ref[pl.ds(r, tr), :] * sc_ref[...]).astype(BF16)
                dpool = _dot(dmix, w_ref[...].astype(BF16), 1, 1)
                lane = lax.broadcasted_iota(jnp.int32, (tr, W), 1)
                t = lax.broadcasted_iota(jnp.int32, (tr, W), 0) + k * tr
                cnt = jnp.minimum(t + 1, _pool_windows(lane)).astype(F32)
                g_scr[pl.ds(r, tr), :] = dpool / cnt
                return carry

            lax.fori_loop(0, n, rows, 0)

        r0 = pl.multiple_of(i * tr, 8)
        pooled, cnt = _pooled_block(u_ref, i, tr)
        dzc = dzc_ref[pl.ds(r0, tr), :]
        mixed = _bdot(pooled, w_ref[...])
        dsc_ref[...] += jnp.sum(dzc * mixed, axis=0, keepdims=True)
        dmix = (dzc * sc_ref[...]).astype(BF16)
        dw_ref[...] += _dot(pooled.astype(BF16), dmix, 0, 0)
        gcur = g_scr[pl.ds(r0, tr), :]
        ext = jnp.concatenate([gcur, _rows_after(g_scr, i, n, tr, 16)], axis=0)
        m = tr + 16
        s2 = ext + pltpu.roll(ext, m - 1, axis=0)
        s4 = s2 + pltpu.roll(s2, m - 2, axis=0)
        s8 = s4 + pltpu.roll(s4, m - 4, axis=0)
        s16 = s8 + pltpu.roll(s8, m - 8, axis=0)
        lane = lax.broadcasted_iota(jnp.int32, (tr, W), 1)
        wins = _pool_windows(lane)
        ahead = jnp.where(wins == 2, s2[:tr], jnp.where(wins == 4, s4[:tr], jnp.where(wins == 8, s8[:tr], s16[:tr])))
        du_ref[...] = (ahead - gcur * cnt).astype(BF16)

    return _call(body, name="pool_bwd", grid=(n,),
                 in_specs=[pl.BlockSpec((L, W), lambda i: (0, 4)), pl.BlockSpec((W, W), lambda i: (0, 0)),
                           pl.BlockSpec((1, W), lambda i: (0, 0)), pl.BlockSpec((L, W), lambda i: (0, 0))],
                 out_specs=[pl.BlockSpec((tr, W), lambda i: (i, 0)), pl.BlockSpec((W, W), lambda i: (0, 0)),
                            pl.BlockSpec((1, W), lambda i: (0, 0))],
                 out_shape=[_sds((L, W), BF16), _sds((W, W), F32), _sds((1, W), F32)],
                 scratch_shapes=[pltpu.VMEM((L, W), F32)],
                 compiler_params=_params(("arbitrary",)))(p, w_pool_bd, pool_scale, dzc)


SSM_SLAB = 512


def _ssm_prep(lam_re, lam_im, log_dt_x):
    def body(lr_ref, li_ref, ldt_ref, abr_ref, abi_ref, fr_ref, fi_ref):
        lr = jnp.minimum(lr_ref[...], LAMBDA_RE_MAX)
        li = li_ref[...]
        dt = jnp.exp(ldt_ref[...])
        mag = jnp.exp(lr * dt)
        abr = mag * jnp.cos(li * dt)
        abi = mag * jnp.sin(li * dt)
        den = lr * lr + li * li
        nr = abr - 1.0
        abr_ref[...] = abr
        abi_ref[...] = abi
        fr_ref[...] = (nr * lr + abi * li) / den
        fi_ref[...] = (abi * lr - nr * li) / den

    shp = _sds(lam_re.shape, F32)
    return _call(body, name="ssm_prep", out_shape=[shp, shp, shp, shp], compiler_params=_params())(lam_re, lam_im, log_dt_x)


def _ssm_prep_bwd(lam_re, lam_im, log_dt_x, g_abr, g_abi, g_fr, g_fi, group_sum):
    def body(lr_ref, li_ref, ldt_ref, gar_ref, gai_ref, gfr_ref, gfi_ref, gs_ref, dlr_ref, dli_ref, dldt_ref):
        lam = lr_ref[...]
        lr = jnp.minimum(lam, LAMBDA_RE_MAX)
        li = li_ref[...]
        dt = jnp.exp(ldt_ref[...])
        mag = jnp.exp(lr * dt)
        abr = mag * jnp.cos(li * dt)
        abi = mag * jnp.sin(li * dt)
        den = lr * lr + li * li
        nr = abr - 1.0
        fr = (nr * lr + abi * li) / den
        fi = (abi * lr - nr * li) / den
        d_nre = gfr_ref[...] / den
        d_nim = gfi_ref[...] / den
        d_den = -(gfr_ref[...] * fr + gfi_ref[...] * fi) / den
        d_abr = gar_ref[...] + d_nre * lr - d_nim * li
        d_abi = gai_ref[...] + d_nre * li + d_nim * lr
        d_lr = d_nre * nr + d_nim * abi + d_den * 2.0 * lr
        d_li = d_nre * abi - d_nim * nr + d_den * 2.0 * li
        d_mag = d_abr * jnp.cos(li * dt) + d_abi * jnp.sin(li * dt)
        d_th = -d_abr * abi + d_abi * abr
        d_lr = d_lr + d_mag * mag * dt
        d_li = d_li + d_th * dt
        d_dt = d_mag * mag * lr + d_th * li
        passes = jnp.where(lam < LAMBDA_RE_MAX, 1.0, jnp.where(lam == LAMBDA_RE_MAX, 0.5, 0.0))
        dlr_ref[...] = d_lr * passes
        dli_ref[...] = d_li
        dldt_ref[...] = _dot(d_dt * dt, gs_ref[...], precision=HIGH)

    shp = _sds(lam_re.shape, F32)
    return _call(body, name="ssm_prep_bwd", out_shape=[shp, shp, _sds((lam_re.shape[0], 128), F32)],
                 compiler_params=_params())(lam_re, lam_im, log_dt_x, g_abr, g_abi, g_fr, g_fi, group_sum)


def _cmul(ar, ai, br, bi):
    return ar * br - ai * bi, ar * bi + ai * br


def _powers(ar, ai):
    out = [(ar, ai)]
    for _ in range(7):
        out.append(_cmul(out[-1][0], out[-1][1], ar, ai))
    return out


def _scan_rows(s_re, s_im, ar, ai, L, reverse=False, visit=None, visit_init=None):
    n = s_re.shape[1]
    pw = _powers(ar, ai)
    row = lax.broadcasted_iota(jnp.int32, (8, n), 0)
    dist = (8 - row) if reverse else (row + 1)
    pr = jnp.zeros((8, n), F32)
    pi = jnp.zeros((8, n), F32)
    for k in range(8):
        pr = jnp.where(dist == k + 1, pw[k][0], pr)
        pi = jnp.where(dist == k + 1, pw[k][1], pi)
    nb = L // 8

    def blk(t, carry):
        cr, ci, acc = carry
        b = (nb - 1 - t) if reverse else t
        r0 = pl.multiple_of(b * 8, 8)
        xr = s_re[pl.ds(r0, 8), :]
        xi = s_im[pl.ds(r0, 8), :]
        for d in (1, 2, 4):
            if reverse:
                keep = row < 8 - d
                sr, si = pltpu.roll(xr, 8 - d, axis=0), pltpu.roll(xi, 8 - d, axis=0)
            else:
                keep = row >= d
                sr, si = pltpu.roll(xr, d, axis=0), pltpu.roll(xi, d, axis=0)
            sr = jnp.where(keep, sr, 0.0)
            si = jnp.where(keep, si, 0.0)
            mr, mi = _cmul(pw[d - 1][0], pw[d - 1][1], sr, si)
            xr, xi = xr + mr, xi + mi
        mr, mi = _cmul(pr, pi, cr, ci)
        xr, xi = xr + mr, xi + mi
        s_re[pl.ds(r0, 8), :] = xr
        s_im[pl.ds(r0, 8), :] = xi
        if visit is not None:
            acc = visit(b, xr, xi, acc)
        if reverse:
            return xr[0:1, :], xi[0:1, :], acc
        return xr[7:8, :], xi[7:8, :], acc

    zero = jnp.zeros((1, n), F32)
    return lax.fori_loop(0, nb, blk, (zero, zero, visit_init if visit is not None else 0))[2]


def _ssm_project(u_ref, wbr, wbi, s_re, s_im, L):
    ch = min(L, 256)

    def rows(k, carry):
        r = pl.multiple_of(k * ch, 8)
        ub = u_ref[pl.ds(r, ch), :].astype(BF16)
        s_re[pl.ds(r, ch), :] = _dot(ub, wbr)
        s_im[pl.ds(r, ch), :] = _dot(ub, wbi)
        return carry

    lax.fori_loop(0, L // ch, rows, 0)


def _gelu(y):
    c = math.sqrt(2.0 / math.pi)
    return 0.5 * y * (1.0 + jnp.tanh(c * (y + 0.044715 * y * y * y)))


def _gelu_grad(y):
    c = math.sqrt(2.0 / math.pi)
    th = jnp.tanh(c * (y + 0.044715 * y * y * y))
    return 0.5 * (1.0 + th) + 0.5 * y * (1.0 - th * th) * c * (1.0 + 3.0 * 0.044715 * y * y)


def _ssm_fwd(p, b_re_bd, b_im_bd, c_re_bd, c_im_bd, abr, abi, fr, fi, d_skip):
    L = p.shape[0]
    ns = NST // SSM_SLAB
    ch = min(L, 256)

    def body(u_ref, br_ref, bi_ref, cr_ref, ci_ref, abr_ref, abi_ref, fr_ref, fi_ref, d_ref,
             y_ref, zb_ref, s_re, s_im):
        j = pl.program_id(0)
        f_re, f_im = fr_ref[...], fi_ref[...]
        wbr = (f_re * br_ref[...] - f_im * bi_ref[...]).astype(BF16)
        wbi = (f_re * bi_ref[...] + f_im * br_ref[...]).astype(BF16)
        _ssm_project(u_ref, wbr, wbi, s_re, s_im, L)
        _scan_rows(s_re, s_im, abr_ref[...], abi_ref[...], L)
        crb = cr_ref[...].astype(BF16)
        cib = ci_ref[...].astype(BF16)

        def rows(k, carry):
            r = pl.multiple_of(k * ch, 8)
            part = _dot(s_re[pl.ds(r, ch), :].astype(BF16), crb) - _dot(s_im[pl.ds(r, ch), :].astype(BF16), cib)

            @pl.when(j == 0)
            def _():
                y_ref[pl.ds(r, ch), :] = part + d_ref[...] * u_ref[pl.ds(r, ch), :]

            @pl.when(j > 0)
            def _():
                y_ref[pl.ds(r, ch), :] += part

            @pl.when(j == ns - 1)
            def _():
                zb_ref[pl.ds(r, ch), :] = _gelu(y_ref[pl.ds(r, ch), :]).astype(BF16)

            return carry

        lax.fori_loop(0, L // ch, rows, 0)

    full = lambda shape: pl.BlockSpec(shape, lambda j: (0, 0))
    lanes = pl.BlockSpec((1, SSM_SLAB), lambda j: (0, j))
    return _call(body, name="ssm_fwd", grid=(ns,),
                 in_specs=[pl.BlockSpec((L, W), lambda j: (0, 3)),
                           pl.BlockSpec((W, SSM_SLAB), lambda j: (0, j)), pl.BlockSpec((W, SSM_SLAB), lambda j: (0, j)),
                           pl.BlockSpec((SSM_SLAB, W), lambda j: (j, 0)), pl.BlockSpec((SSM_SLAB, W), lambda j: (j, 0)),
                           lanes, lanes, lanes, lanes, full((1, W))],
                 out_specs=[full((L, W)), full((L, W))],
                 out_shape=[_sds((L, W), F32), _sds((L, W), BF16)],
                 scratch_shapes=[pltpu.VMEM((L, SSM_SLAB), F32), pltpu.VMEM((L, SSM_SLAB), F32)],
                 compiler_params=_params(("arbitrary",)))(p, b_re_bd, b_im_bd, c_re_bd, c_im_bd, abr, abi, fr, fi, d_skip)


def _ssm_bwd(p, y, dzb, b_re_bd, b_im_bd, c_re_bd, c_im_bd, abr, abi, fr, fi, d_skip):
    L = p.shape[0]
    ns = NST // SSM_SLAB
    ch = min(L, 256)
    n_ch = L // ch

    def body(u_ref, y_ref, dzb_ref, br_ref, bi_ref, cr_ref, ci_ref, abr_ref, abi_ref, fr_ref, fi_ref, d_ref,
             du_ref, dd_ref, dbr_ref, dbi_ref, dcr_ref, dci_ref, gar_ref, gai_ref, gfr_ref, gfi_ref,
             s_re, s_im, l_re, l_im, dy_scr, du_scr):
        j = pl.program_id(0)
        f_re, f_im = fr_ref[...], fi_ref[...]
        b_re, b_im = br_ref[...], bi_ref[...]
        wbr = (f_re * b_re - f_im * b_im).astype(BF16)
        wbi = (f_re * b_im + f_im * b_re).astype(BF16)
        a_re, a_im = abr_ref[...], abi_ref[...]

        @pl.when(j == 0)
        def _():
            def rows(k, acc):
                r = pl.multiple_of(k * ch, 8)
                dy = dzb_ref[pl.ds(r, ch), :] * _gelu_grad(y_ref[pl.ds(r, ch), :])
                dy_scr[pl.ds(r, ch), :] = dy
                du_scr[pl.ds(r, ch), :] = d_ref[...] * dy
                return acc + jnp.sum(dy * u_ref[pl.ds(r, ch), :], axis=0, keepdims=True)

            dd_ref[...] = lax.fori_loop(0, n_ch, rows, jnp.zeros((1, W), F32))

        _ssm_project(u_ref, wbr, wbi, s_re, s_im, L)
        _scan_rows(s_re, s_im, a_re, a_im, L)
        crb = cr_ref[...].astype(BF16)
        cib = ci_ref[...].astype(BF16)

        def rows_c(k, acc):
            dcr, dci = acc
            r = pl.multiple_of(k * ch, 8)
            dyb = dy_scr[pl.ds(r, ch), :].astype(BF16)
            dcr = dcr + _dot(s_re[pl.ds(r, ch), :].astype(BF16), dyb, 0, 0)
            dci = dci - _dot(s_im[pl.ds(r, ch), :].astype(BF16), dyb, 0, 0)
            l_re[pl.ds(r, ch), :] = _dot(dyb, crb, 1, 1)
            l_im[pl.ds(r, ch), :] = -_dot(dyb, cib, 1, 1)
            return dcr, dci

        zc = jnp.zeros((SSM_SLAB, W), F32)
        dcr, dci = lax.fori_loop(0, n_ch, rows_c, (zc, zc))
        dcr_ref[...] = dcr
        dci_ref[...] = dci

        row8 = lax.broadcasted_iota(jnp.int32, (8, SSM_SLAB), 0)

        def visit(b, lr, li, acc):
            ar_acc, ai_acc = acc
            r0 = pl.multiple_of(b * 8, 8)
            rp = pl.multiple_of(jnp.maximum(b * 8 - 8, 0), 8)
            has_prev = b > 0
            pr = jnp.where(has_prev, s_re[pl.ds(rp, 8), :][7:8, :], 0.0)
            pi = jnp.where(has_prev, s_im[pl.ds(rp, 8), :][7:8, :], 0.0)
            sr = jnp.where(row8 >= 1, pltpu.roll(s_re[pl.ds(r0, 8), :], 1, axis=0), pr)
            si = jnp.where(row8 >= 1, pltpu.roll(s_im[pl.ds(r0, 8), :], 1, axis=0), pi)
            return ar_acc + lr * sr + li * si, ai_acc - lr * si + li * sr

        z8 = jnp.zeros((8, SSM_SLAB), F32)
        ar_acc, ai_acc = _scan_rows(l_re, l_im, a_re, -a_im, L, reverse=True, visit=visit, visit_init=(z8, z8))
        gar_ref[...] = jnp.sum(ar_acc, axis=0, keepdims=True)
        gai_ref[...] = jnp.sum(ai_acc, axis=0, keepdims=True)

        def rows_b(k, acc):
            dwr, dwi = acc
            r = pl.multiple_of(k * ch, 8)
            ub = u_ref[pl.ds(r, ch), :].astype(BF16)
            lrb = l_re[pl.ds(r, ch), :].astype(BF16)
            lib = l_im[pl.ds(r, ch), :].astype(BF16)
            du_scr[pl.ds(r, ch), :] += _dot(lrb, wbr, 1, 1) + _dot(lib, wbi, 1, 1)
            return dwr + _dot(ub, lrb, 0, 0), dwi + _dot(ub, lib, 0, 0)

        zb = jnp.zeros((W, SSM_SLAB), F32)
        dwr, dwi = lax.fori_loop(0, n_ch, rows_b, (zb, zb))
        dbr_ref[...] = dwr * f_re + dwi * f_im
        dbi_ref[...] = -dwr * f_im + dwi * f_re
        gfr_ref[...] = jnp.sum(dwr * b_re + dwi * b_im, axis=0, keepdims=True)
        gfi_ref[...] = jnp.sum(-dwr * b_im + dwi * b_re, axis=0, keepdims=True)

        @pl.when(j == ns - 1)
        def _():
            du_ref[...] = du_scr[...].astype(BF16)

    full = lambda shape: pl.BlockSpec(shape, lambda j: (0, 0))
    lanes = pl.BlockSpec((1, SSM_SLAB), lambda j: (0, j))
    bspec = pl.BlockSpec((W, SSM_SLAB), lambda j: (0, j))
    cspec = pl.BlockSpec((SSM_SLAB, W), lambda j: (j, 0))
    slab = lambda: pltpu.VMEM((L, SSM_SLAB), F32)
    return _call(body, name="ssm_bwd", grid=(ns,),
                 in_specs=[pl.BlockSpec((L, W), lambda j: (0, 3)), full((L, W)), full((L, W)),
                           bspec, bspec, cspec, cspec, lanes, lanes, lanes, lanes, full((1, W))],
                 out_specs=[full((L, W)), full((1, W)), bspec, bspec, cspec, cspec, lanes, lanes, lanes, lanes],
                 out_shape=[_sds((L, W), BF16), _sds((1, W), F32), _sds((W, NST), F32), _sds((W, NST), F32),
                            _sds((NST, W), F32), _sds((NST, W), F32)] + [_sds((1, NST), F32)] * 4,
                 scratch_shapes=[slab(), slab(), slab(), slab(), pltpu.VMEM((L, W), F32), pltpu.VMEM((L, W), F32)],
                 compiler_params=_params(("arbitrary",)))(p, y, dzb, b_re_bd, b_im_bd, c_re_bd, c_im_bd,
                                                          abr, abi, fr, fi, d_skip)


SB_KB = 512


SB_SUB = 256


def _split2(x):
    hi = x.astype(BF16)
    return hi, (x - hi.astype(F32)).astype(BF16)


def _ones_dot(x, ones):
    n = x.shape[0]
    r = _dot(jnp.concatenate(_split2(x), axis=0), ones)
    return r[:n] + r[n:]


def _suffix_sums(x, tri):
    sub = tri.shape[0]
    parts = [_ones_dot(x[:, i:i + sub], tri) for i in range(0, x.shape[1], sub)]
    out, after = [], None
    for p in reversed(parts):
        out.append(p if after is None else p + after)
        after = p[:, 0:1] if after is None else after + p[:, 0:1]
    return jnp.concatenate(out[::-1], axis=1)


def _prefix_sums_exclusive(x, tri_le):
    sub = tri_le.shape[0]
    out, before = [], None
    for i in range(0, x.shape[1], sub):
        xi = x[:, i:i + sub]
        inc = _ones_dot(xi, tri_le)
        out.append(inc - xi if before is None else inc - xi + before)
        before = inc[:, sub - 1:sub] if before is None else before + inc[:, sub - 1:sub]
    return jnp.concatenate(out, axis=1)


def _sb_block(q, kj, i, jb, kb, right, tri):
    z = _bdot(q, kj, 1, 1)
    t_idx = lax.broadcasted_iota(jnp.int32, (QB, kb), 0) + i * QB
    s_idx = lax.broadcasted_iota(jnp.int32, (QB, kb), 1) + jb * kb
    mask = s_idx < t_idx
    lk_all = jnp.minimum(-z, 0.0) - jnp.log1p(jnp.exp(-jnp.abs(z)))
    lk = jnp.where(mask, lk_all, 0.0)
    suf = _suffix_sums(lk, tri)
    a = jnp.where(mask, jnp.exp((lk_all + z) + (suf - lk) + right), 0.0)
    return z, mask, suf, a


def _sb_ones(kb):
    sub = min(SB_SUB, kb)
    r = lax.broadcasted_iota(jnp.int32, (sub, sub), 0)
    c = lax.broadcasted_iota(jnp.int32, (sub, sub), 1)
    return (r >= c).astype(BF16), (r <= c).astype(BF16)


def _sb_fwd(q, k, v):
    L = q.shape[1]
    kb = min(SB_KB, L)
    per = kb // QB

    def body(q_ref, k_ref, v_ref, o_ref, rs_ref):
        i = pl.program_id(0)
        tri, _ = _sb_ones(kb)
        lane = lax.broadcasted_iota(jnp.int32, (QB, 128), 1)
        qs = [q_ref[h] for h in range(HEADS)]

        def step(t, carry):
            accs, rights, sums = carry
            jb = i // per - t
            r = pl.multiple_of(jb * kb, kb)
            out = []
            for h in range(HEADS):
                _, _, suf, a = _sb_block(qs[h], k_ref[h, pl.ds(r, kb), :], i, jb, kb, rights[h], tri)
                tot = suf[:, 0:1]
                out.append((accs[h] + _bdot(a, v_ref[h, pl.ds(r, kb), :]), rights[h] + tot,
                            sums[h] + jnp.where(lane == jb, tot, 0.0)))
            return tuple(o[0] for o in out), tuple(o[1] for o in out), tuple(o[2] for o in out)

        init = (tuple(jnp.zeros((QB, HD), F32) for _ in range(HEADS)), tuple(jnp.zeros((QB, 1), F32) for _ in range(HEADS)),
                tuple(jnp.zeros((QB, 128), F32) for _ in range(HEADS)))
        accs, _, sums = lax.fori_loop(0, i // per + 1, step, init)
        for h in range(HEADS):
            o_ref[h] = accs[h]
            rs_ref[h] = sums[h]

    heads = pl.BlockSpec((HEADS, L, HD), lambda i: (0, 0, 0))
    blk = pl.BlockSpec((HEADS, QB, HD), lambda i: (0, i, 0))
    return _call(body, name="sb_fwd", grid=(L // QB,), in_specs=[blk, heads, heads],
                 out_specs=[blk, pl.BlockSpec((HEADS, QB, 128), lambda i: (0, i, 0))],
                 out_shape=[_sds((HEADS, L, HD), F32), _sds((HEADS, L, 128), F32)],
                 compiler_params=_params(("parallel",)))(q, k, v)


def _sb_bwd(q, k, v, do, block_sums):
    L = q.shape[1]
    kb = min(SB_KB, L)
    per = kb // QB

    def body(q_ref, k_ref, v_ref, do_ref, rs_ref, dq_ref, dk_ref, dv_ref):
        i = pl.program_id(0)
        tri, tri_le = _sb_ones(kb)
        lane = lax.broadcasted_iota(jnp.int32, (QB, 128), 1)

        @pl.when(i == 0)
        def _():
            dk_ref[...] = jnp.zeros_like(dk_ref)
            dv_ref[...] = jnp.zeros_like(dv_ref)

        qs = [q_ref[h] for h in range(HEADS)]
        dos = [do_ref[h] for h in range(HEADS)]
        sums = [rs_ref[h] for h in range(HEADS)]

        def step(jb, carry):
            dqs, lefts = carry
            r = pl.multiple_of(jb * kb, kb)
            out = []
            for h in range(HEADS):
                kj = k_ref[h, pl.ds(r, kb), :]
                vj = v_ref[h, pl.ds(r, kb), :]
                right = jnp.sum(jnp.where(lane > jb, sums[h], 0.0), axis=1, keepdims=True)
                z, mask, _, a = _sb_block(qs[h], kj, i, jb, kb, right, tri)
                e = a * _bdot(dos[h], vj, 1, 1)
                dv_ref[h, pl.ds(r, kb), :] += _dot(a.astype(BF16), dos[h].astype(BF16), 0, 0)
                before = lefts[h] + _prefix_sums_exclusive(e, tri_le)
                sg = _sigmoid(z)
                dz = jnp.where(mask, e * (1.0 - sg) - sg * before, 0.0).astype(BF16)
                dk_ref[h, pl.ds(r, kb), :] += _dot(dz, qs[h].astype(BF16), 0, 0)
                out.append((dqs[h] + _dot(dz, kj.astype(BF16)), lefts[h] + jnp.sum(e, axis=1, keepdims=True)))
            return tuple(o[0] for o in out), tuple(o[1] for o in out)

        init = (tuple(jnp.zeros((QB, HD), F32) for _ in range(HEADS)), tuple(jnp.zeros((QB, 1), F32) for _ in range(HEADS)))
        dqs, _ = lax.fori_loop(0, i // per + 1, step, init)
        for h in range(HEADS):
            dq_ref[h] = dqs[h]

    heads = pl.BlockSpec((HEADS, L, HD), lambda i: (0, 0, 0))
    blk = pl.BlockSpec((HEADS, QB, HD), lambda i: (0, i, 0))
    shp = _sds((HEADS, L, HD), F32)
    return _call(body, name="sb_bwd", grid=(L // QB,),
                 in_specs=[blk, heads, heads, blk, pl.BlockSpec((HEADS, QB, 128), lambda i: (0, i, 0))],
                 out_specs=[blk, heads, heads], out_shape=[shp, shp, shp],
                 compiler_params=_params(("arbitrary",)))(q, k, v, do, block_sums)


def _merge_fwd(za, zb, zc, zd, p, w_conv_out, w_glu, w_pool_out, w_sb_out):
    L = za.shape[0]
    tm = _row_tile(L)

    def body(za_ref, zb_ref, zc_ref, zd_ref, g0, g1, g2, g3, wc_ref, wg_ref, wp_ref, ws_ref, o_ref):
        glu = _dot(zb_ref[...], wg_ref[...])
        yb = glu[:, :D] * _sigmoid(glu[:, D:])
        m = _sigmoid(g0[...]) * _dot(za_ref[...], wc_ref[...])
        m = m + _sigmoid(g1[...]) * yb
        m = m + _sigmoid(g2[...]) * _dot(zc_ref[...], wp_ref[...])
        m = m + _sigmoid(g3[...]) * _dot(zd_ref[...], ws_ref[...])
        o_ref[...] = m.astype(BF16)

    zt = pl.BlockSpec((tm, W), lambda i: (i, 0))
    gate = lambda b: pl.BlockSpec((tm, D), lambda i: (i, 2 + b))
    wfull = lambda n: pl.BlockSpec((W, n), lambda i: (0, 0))
    return _call(body, name="merge_fwd", grid=(L // tm,),
                 in_specs=[zt, zt, zt, zt, gate(0), gate(1), gate(2), gate(3), wfull(D), wfull(2 * D), wfull(D), wfull(D)],
                 out_specs=pl.BlockSpec((tm, D), lambda i: (i, 0)), out_shape=_sds((L, D), BF16),
                 compiler_params=_params(("parallel",)))(za, zb, zc, zd, p, p, p, p, w_conv_out, w_glu, w_pool_out, w_sb_out)


def _merge_bwd(dm, za, zb, zc, zd, p, w_conv_out, w_glu, w_pool_out, w_sb_out):
    L = za.shape[0]
    tm = _row_tile(L)
    n = L // tm

    def body(dm_ref, za_ref, zb_ref, zc_ref, zd_ref, g0, g1, g2, g3, wc_ref, wg_ref, wp_ref, ws_ref,
             dza_ref, dzb_ref, dzc_ref, dzd_ref, dg_ref, dwc_ref, dwg_ref, dwp_ref, dws_ref,
             awc, awg, awp, aws):
        i = pl.program_id(0)

        @pl.when(i == 0)
        def _():
            awc[...] = jnp.zeros_like(awc)
            awg[...] = jnp.zeros_like(awg)
            awp[...] = jnp.zeros_like(awp)
            aws[...] = jnp.zeros_like(aws)

        dmv = dm_ref[...]

        def gated(g_ref, y, col):
            s = _sigmoid(g_ref[...])
            dg_ref[:, col * D:(col + 1) * D] = (dmv * y * s * (1.0 - s)).astype(BF16)
            return (dmv * s)

        def linear(z_ref, w_ref, acc, dz_ref, col, g_ref):
            zv = z_ref[...]
            dy = gated(g_ref, _dot(zv, w_ref[...]), col).astype(BF16)
            dz_ref[...] = _dot(dy, w_ref[...], 1, 1)
            acc[...] += _dot(zv, dy, 0, 0)

        linear(za_ref, wc_ref, awc, dza_ref, 0, g0)
        linear(zc_ref, wp_ref, awp, dzc_ref, 2, g2)
        linear(zd_ref, ws_ref, aws, dzd_ref, 3, g3)
        zbv = zb_ref[...]
        glu = _dot(zbv, wg_ref[...])
        ga = glu[:, :D]
        sg = _sigmoid(glu[:, D:])
        dyb = gated(g1, ga * sg, 1)
        dga = (dyb * sg).astype(BF16)
        dgg = (dyb * ga * sg * (1.0 - sg)).astype(BF16)
        dzb_ref[...] = _dot(dga, wg_ref[:, :D], 1, 1) + _dot(dgg, wg_ref[:, D:], 1, 1)
        awg[:, :D] += _dot(zbv, dga, 0, 0)
        awg[:, D:] += _dot(zbv, dgg, 0, 0)

        @pl.when(i == n - 1)
        def _():
            dwc_ref[...] = awc[...].astype(BF16)
            dwg_ref[...] = awg[...].astype(BF16)
            dwp_ref[...] = awp[...].astype(BF16)
            dws_ref[...] = aws[...].astype(BF16)

    zt = pl.BlockSpec((tm, W), lambda i: (i, 0))
    gate = lambda b: pl.BlockSpec((tm, D), lambda i: (i, 2 + b))
    wfull = lambda n_: pl.BlockSpec((W, n_), lambda i: (0, 0))
    zs = _sds((L, W), F32)
    return _call(body, name="merge_bwd", grid=(n,),
                 in_specs=[pl.BlockSpec((tm, D), lambda i: (i, 0)), zt, zt, zt, zt, gate(0), gate(1), gate(2), gate(3),
                           wfull(D), wfull(2 * D), wfull(D), wfull(D)],
                 out_specs=[zt, zt, zt, zt, pl.BlockSpec((tm, 4 * D), lambda i: (i, 0)),
                            wfull(D), wfull(2 * D), wfull(D), wfull(D)],
                 out_shape=[zs, zs, zs, zs, _sds((L, 4 * D), BF16),
                            _sds((W, D), BF16), _sds((W, 2 * D), BF16), _sds((W, D), BF16), _sds((W, D), BF16)],
                 scratch_shapes=[pltpu.VMEM((W, D), F32), pltpu.VMEM((W, 2 * D), F32), pltpu.VMEM((W, D), F32),
                                 pltpu.VMEM((W, D), F32)],
                 compiler_params=_params(("arbitrary",)))(dm, za, zb, zc, zd, p, p, p, p,
                                                          w_conv_out, w_glu, w_pool_out, w_sb_out)


def _adam_math(w, g, m, v):
    m2 = ADAM_B1 * m + (1.0 - ADAM_B1) * g
    v2 = ADAM_B2 * v + (1.0 - ADAM_B2) * (g * g)
    m_hat = m2 / (1.0 - ADAM_B1 ** ADAM_STEP)
    v_hat = v2 / (1.0 - ADAM_B2 ** ADAM_STEP)
    return -ADAM_LR * (m_hat / (jnp.sqrt(v_hat) + ADAM_EPS) + ADAM_WD * w), m2, v2


def _as_rows(a):
    return a.reshape(-1, a.shape[-1])


def _adamw(w, g, m, v):
    shape = w.shape
    w2, g2, m2, v2 = _as_rows(w), _as_rows(g), _as_rows(m), _as_rows(v)
    R, C = w2.shape
    tr = R
    for cand in (1024, 512, 256, 128, 64, 32, 16, 8):
        if R % cand == 0 and cand * C * 4 <= 2 * 1024 * 1024:
            tr = cand
            break

    def body(w_ref, g_ref, m_ref, v_ref, d_ref, m_out, v_out):
        d, mn, vn = _adam_math(w_ref[...], g_ref[...], m_ref[...], v_ref[...])
        d_ref[...] = d
        m_out[...] = mn
        v_out[...] = vn

    blk = pl.BlockSpec((tr, C), lambda i: (i, 0))
    shp = _sds((R, C), F32)
    outs = _call(body, name="adamw", grid=(R // tr,), in_specs=[blk] * 4, out_specs=[blk] * 3, out_shape=[shp] * 3,
                 compiler_params=_params(("parallel",)))(w2, g2, m2, v2)
    return tuple(o.reshape(shape) for o in outs)


def _sum_parts(parts, out_dtype, name):
    shape = parts[0].shape
    flat = [_as_rows(a) for a in parts]
    R, C = flat[0].shape
    tr = R
    for cand in (1024, 512, 256, 128, 64, 32, 16):
        if R % cand == 0 and cand * C * 4 <= 2 * 1024 * 1024:
            tr = cand
            break
    k = len(parts)

    def body(*refs):
        acc = refs[0][...].astype(F32)
        for r in refs[1:k]:
            acc = acc + r[...].astype(F32)
        refs[k][...] = acc.astype(out_dtype)

    blk = pl.BlockSpec((tr, C), lambda i: (i, 0))
    out = _call(body, name=name, grid=(R // tr,), in_specs=[blk] * k, out_specs=blk, out_shape=_sds((R, C), out_dtype),
                compiler_params=_params(("parallel",)))(*flat)
    return out.reshape(shape)


ADA_SHARD = 9 * D // N_CHIP
ADA_TN = 768


def _ada_fwd(c_pad, w_ada, b_ada_cols):
    depth = w_ada.shape[0]

    def body(c_ref, w_ref, b_ref, o_ref):
        cv = c_ref[...]
        o_ref[...] = _bdot(cv * _sigmoid(cv), w_ref[...]) + b_ref[...]

    return _call(body, name="ada_fwd", grid=(depth, ADA_SHARD // ADA_TN),
                 in_specs=[pl.BlockSpec((16, D), lambda l, j: (0, 0)),
                           pl.BlockSpec((None, D, ADA_TN), lambda l, j: (l, 0, j)),
                           pl.BlockSpec((None, 1, ADA_TN), lambda l, j: (l, 0, j))],
                 out_specs=pl.BlockSpec((None, 16, ADA_TN), lambda l, j: (l, 0, j)),
                 out_shape=_sds((depth, 16, ADA_SHARD), F32),
                 compiler_params=_params(("parallel", "parallel")))(c_pad, w_ada, b_ada_cols)


def _ada_wgrad(c_pad, d_ada):
    depth = d_ada.shape[0]

    def body(c_ref, d_ref, o_ref):
        cv = c_ref[...]
        o_ref[...] = _bdot(cv * _sigmoid(cv), d_ref[...], 0, 0)

    return _call(body, name="ada_wgrad", grid=(depth, ADA_SHARD // ADA_TN),
                 in_specs=[pl.BlockSpec((16, D), lambda l, j: (0, 0)),
                           pl.BlockSpec((None, 16, ADA_TN), lambda l, j: (l, 0, j))],
                 out_specs=pl.BlockSpec((None, D, ADA_TN), lambda l, j: (l, 0, j)),
                 out_shape=_sds((depth, D, ADA_SHARD), F32),
                 compiler_params=_params(("parallel", "parallel")))(c_pad, d_ada)


HBM_SPEC = pl.BlockSpec(memory_space=pltpu.HBM)


def _place():
    x, y, c = lax.axis_index("x"), lax.axis_index("y"), lax.axis_index("c")
    peers = [(1 - x, y), (x, 1 - y), (1 - x, 1 - y)]
    return x, y, c, peers


def _chip(px, py):
    return 2 * px + py


def _allgather8(block, name):
    m_per, n = block.shape

    def body(x_ref, out_ref, send_sems, recv_sems, local_sem):
        x, y, c, chips = _place()
        me, sibling = (x, y, c), (x, y, 1 - c)

        def rows(px, py, pc):
            return out_ref.at[pl.ds(pl.multiple_of((4 * px + 2 * py + pc) * m_per, 8), m_per), :]

        def copy(k, blk, to, src=None):
            return pltpu.make_async_remote_copy(
                src_ref=rows(*blk) if src is None else src, dst_ref=rows(*blk),
                send_sem=send_sems.at[k], recv_sem=recv_sems.at[k], device_id=to, device_id_type=MESH)

        mine = pltpu.make_async_copy(x_ref, rows(*me), local_sem)
        mine.start()
        first = [copy(0, me, sibling, src=x_ref)]
        first += [copy(1 + j, me, (*chip, c), src=x_ref) for j, chip in enumerate(chips)]
        for cp in first:
            cp.start()
        passed = [copy(4 + j, (*chip, c), sibling) for j, chip in enumerate(chips)]
        for j, chip in enumerate(chips):
            copy(1 + j, (*chip, c), me).wait_recv()
            passed[j].start()
        copy(0, sibling, me).wait_recv()
        for j, chip in enumerate(chips):
            copy(4 + j, (*chip, 1 - c), me).wait_recv()
        for cp in first + passed:
            cp.wait_send()
        mine.wait()

    return _call(body, name=name, out_shape=_sds((N_DEV * m_per, n), block.dtype),
                 in_specs=[pl.BlockSpec(memory_space=pltpu.VMEM)], out_specs=pl.BlockSpec(memory_space=pltpu.VMEM),
                 scratch_shapes=[pltpu.SemaphoreType.DMA((7,)), pltpu.SemaphoreType.DMA((7,)), pltpu.SemaphoreType.DMA],
                 compiler_params=_params())(block)


GATHERED = (("w_ff_in", 0, -1, 0), ("w_ff_out", 0, -2, 0),
            ("w_in", None, -1, 1), ("w_conv_out", None, -1, 1), ("w_glu", None, -1, 1), ("w_pool_out", None, -1, 1),
            ("w_sb_out", None, -1, 1), ("w_out", None, -2, 1),
            ("w_ff_in", 1, -1, 2), ("w_ff_out", 1, -2, 2))
N_SUB = 3


def _lead(ref):
    return (slice(None),) * (len(ref.shape) - 2)


def _mo(v, m):
    return v if isinstance(v, int) else pl.multiple_of(v, m)


def _full_region(ref, axis, j, half, shard_shape):
    rs, cs = shard_shape[-2], shard_shape[-1]
    if axis == -1:
        r0, nr = (0, rs) if half is None else (half * (rs // 2), rs // 2)
        return ref.at[_lead(ref) + (pl.ds(_mo(r0, 16), nr), pl.ds(_mo(j * cs, 128), cs))]
    r0, nr = (j * rs, rs) if half is None else (j * rs + half * (rs // 2), rs // 2)
    return ref.at[_lead(ref) + (pl.ds(_mo(r0, 16), nr), slice(None))]


def _shard_half(ref, half):
    rs = ref.shape[-2]
    return ref.at[_lead(ref) + (pl.ds(_mo(half * (rs // 2), 16), rs // 2), slice(None))]


def _full_shape(shard_shape, axis):
    s = list(shard_shape)
    s[axis] *= N_CHIP
    return tuple(s)


class _Lay:
    def __init__(self, shard_shape, axis):
        self.axis = axis
        self.shard_shape = tuple(shard_shape)
        self.full_shape = _full_shape(shard_shape, axis)
        self.lead = int(np.prod(shard_shape[:-2]))
        self.rs, self.cs = shard_shape[-2], shard_shape[-1]
        self.hr = self.rs // 2
        self.tr = next(t for t in (512, 256, 128, 64, 32, 16) if self.hr % t == 0 and t * self.cs * 4 <= (1 << 21))
        self.half_rows_shape = _half_rows_shape(self.full_shape)
        self.half_shard_shape = _half_rows_shape(self.shard_shape)

    def full(self, jf, hf):
        if self.axis == -1:
            return ((self.lead, 2, self.hr, N_CHIP * self.cs),
                    pl.BlockSpec((None, None, self.tr, self.cs), lambda b, j, i, s: (b, hf(j, s), i, jf(j, s))))
        return ((self.lead, N_CHIP, 2, self.hr, self.cs),
                pl.BlockSpec((None, None, None, self.tr, self.cs), lambda b, j, i, s: (b, jf(j, s), hf(j, s), i, 0)))

    def half_shard(self):
        return (self.lead, self.hr, self.cs), pl.BlockSpec((None, self.tr, self.cs), lambda b, j, i, s: (b, i, 0))

    def shard(self, hf):
        return ((self.lead, 2, self.hr, self.cs),
                pl.BlockSpec((None, None, self.tr, self.cs), lambda b, j, i, s: (b, hf(j, s), i, 0)))


def _view_sum(sel, operands, out_view, out_shape, out_dtype, grid, name):
    k = len(operands)

    def body(sel_ref, *refs):
        acc = refs[0][...].astype(F32)
        for r in refs[1:k]:
            acc = acc + r[...].astype(F32)
        refs[k][...] = acc.astype(out_dtype)

    spec = pltpu.PrefetchScalarGridSpec(num_scalar_prefetch=1, grid=grid, in_specs=[v[1] for _, v in operands],
                                        out_specs=out_view[1])
    out = _call(body, name=name, grid_spec=spec, out_shape=_sds(out_view[0], out_dtype),
                compiler_params=_params(("parallel", "parallel", "parallel")))(
                    sel, *[a.reshape(v[0]) for a, v in operands])
    return out.reshape(out_shape)


def _sel_core(j, s):
    return s[0]


def _sel_chip(j, s):
    return s[1]


def _grid_j(j, s):
    return j


def _place_shard(lay, sel, w):
    return _view_sum(sel, [(w, lay.shard(_grid_j))], lay.full(_sel_chip, _grid_j), lay.full_shape, BF16,
                     (lay.lead, 2, lay.hr // lay.tr), "place_shard")


SEM_SPEC = pl.BlockSpec(memory_space=pltpu.SEMAPHORE)
ANY_SPEC = pl.BlockSpec(memory_space=pl.ANY)
SPLIT_COPY = pltpu.SideEffectType.DATAFLOW_SIDE_EFFECTING


def _in_hbm(a):
    return pltpu.with_memory_space_constraint(a, pltpu.HBM)


def _gather_copies(lays, bufs, send_sems, recv_sems):
    x, y, c, chips = _place()
    copies = []
    for a, lay in enumerate(lays):
        own = _full_region(bufs[a], lay.axis, _chip(x, y), c, lay.shard_shape)
        for k, chip in enumerate(chips):
            copies.append(pltpu.make_async_remote_copy(
                src_ref=own, dst_ref=own, send_sem=send_sems.at[a * 3 + k], recv_sem=recv_sems.at[a * 3 + k],
                device_id=(*chip, c), device_id_type=MESH))
    return copies


def _gather_start(fulls, after, lays, tag):
    n = len(fulls)

    def body(*refs):
        send_sems, recv_sems = refs[n + 1], refs[n + 2]
        bufs, token = refs[n + 3:2 * n + 3], refs[2 * n + 3]
        for cp in _gather_copies(lays, bufs, send_sems, recv_sems):
            cp.start()
        token[...] = jnp.zeros_like(token)

    outs = _call(body, name="gather_start_" + tag,
                 out_shape=[pltpu.SemaphoreType.DMA((3 * n,)), pltpu.SemaphoreType.DMA((3 * n,))]
                 + [pltpu.HBM(f.shape, f.dtype) for f in fulls] + [_sds((8, 128), F32)],
                 in_specs=[HBM_SPEC] * n + [ANY_SPEC],
                 out_specs=[SEM_SPEC, SEM_SPEC] + [HBM_SPEC] * n + [pl.BlockSpec(memory_space=pltpu.VMEM)],
                 input_output_aliases={a: a + 2 for a in range(n)},
                 compiler_params=pltpu.CompilerParams(has_side_effects=SPLIT_COPY))(*[_in_hbm(f) for f in fulls], after)
    return outs[0], outs[1], outs[2:2 + n], outs[2 + n]


def _gather_wait(send_sems, recv_sems, bufs, after, lays, tag):
    n = len(bufs)

    def body(*refs):
        ss, rs = refs[n], refs[n + 1]
        for cp in _gather_copies(lays, refs[n + 3:], ss, rs):
            cp.wait_send()
            cp.wait_recv()

    return _call(body, name="gather_wait_" + tag,
                 out_shape=[pltpu.HBM(b.shape, b.dtype) for b in bufs],
                 in_specs=[HBM_SPEC] * n + [SEM_SPEC, SEM_SPEC, ANY_SPEC], out_specs=[HBM_SPEC] * n,
                 input_output_aliases={a: a for a in range(n)},
                 compiler_params=pltpu.CompilerParams(has_side_effects=SPLIT_COPY))(*bufs, send_sems, recv_sems, after)


def _gather_forward(bufs, lays):
    n = len(bufs)

    def body(*refs):
        outs = refs[n:2 * n]
        send_sems, recv_sems = refs[2 * n:]
        x, y, c, chips = _place()
        sibling = (x, y, 1 - c)
        sends = []
        for a in range(n):
            for k, chip in enumerate(chips):
                landed = _full_region(outs[a], lays[a].axis, _chip(*chip), c, lays[a].shard_shape)
                cp = pltpu.make_async_remote_copy(
                    src_ref=landed, dst_ref=landed, send_sem=send_sems.at[a * 3 + k], recv_sem=recv_sems.at[a * 3 + k],
                    device_id=sibling, device_id_type=MESH)
                cp.start()
                sends.append(cp)
        for a in range(n):
            for k, chip in enumerate(chips):
                passed = _full_region(outs[a], lays[a].axis, _chip(*chip), 1 - c, lays[a].shard_shape)
                pltpu.make_async_remote_copy(
                    src_ref=passed, dst_ref=passed, send_sem=send_sems.at[a * 3 + k], recv_sem=recv_sems.at[a * 3 + k],
                    device_id=sibling, device_id_type=MESH).wait_recv()
        for cp in sends:
            cp.wait_send()

    return _call(body, name="gather_forward",
                 out_shape=[_sds(b.shape, b.dtype) for b in bufs],
                 in_specs=[HBM_SPEC] * n, out_specs=[HBM_SPEC] * n,
                 input_output_aliases={a: a for a in range(n)},
                 scratch_shapes=[pltpu.SemaphoreType.DMA((3 * n,)), pltpu.SemaphoreType.DMA((3 * n,))],
                 compiler_params=_params())(*bufs)


def _half_rows_shape(full_shape):
    s = list(full_shape)
    s[-2] //= 2
    return tuple(s)


RELATIONS = tuple((r, s) for r in range(N_CHIP) for s in range(2))[1:]


def _peer(x, y, c, rel):
    r, s = rel
    return (1 - x if r in (1, 3) else x, 1 - y if r in (2, 3) else y, 1 - c if s else c)


def _reduce_copies(lays, grads, landing, send_sems, recv_sems):
    x, y, c, _ = _place()
    nr = len(RELATIONS)
    copies = []
    for a, lay in enumerate(lays):
        for k, rel in enumerate(RELATIONS):
            px, py, pc = _peer(x, y, c, rel)
            copies.append(pltpu.make_async_remote_copy(
                src_ref=_full_region(grads[a], lay.axis, _chip(px, py), pc, lay.shard_shape), dst_ref=landing[a * nr + k],
                send_sem=send_sems.at[a * nr + k], recv_sem=recv_sems.at[a * nr + k],
                device_id=(px, py, pc), device_id_type=MESH))
    return copies


def _reduce_start(grads, lays, tag):
    n, nr = len(grads), len(RELATIONS)
    landing = [_in_hbm(lax.empty(lay.half_shard_shape, BF16)) for lay in lays for _ in RELATIONS]
    m = n + n * nr

    def body(*refs):
        send_sems, recv_sems = refs[m], refs[m + 1]
        src, land, token = refs[m + 2:m + 2 + n], refs[m + 2 + n:2 * m + 2], refs[2 * m + 2]
        for cp in _reduce_copies(lays, src, land, send_sems, recv_sems):
            cp.start()
        token[...] = jnp.zeros_like(token)

    ops = [_in_hbm(g) for g in grads] + landing
    outs = _call(body, name="reduce_start_" + tag,
                 out_shape=[pltpu.SemaphoreType.DMA((n * nr,)), pltpu.SemaphoreType.DMA((n * nr,))]
                 + [pltpu.HBM(o.shape, o.dtype) for o in ops] + [_sds((8, 128), F32)],
                 in_specs=[HBM_SPEC] * m,
                 out_specs=[SEM_SPEC, SEM_SPEC] + [HBM_SPEC] * m + [pl.BlockSpec(memory_space=pltpu.VMEM)],
                 input_output_aliases={a: a + 2 for a in range(m)},
                 compiler_params=pltpu.CompilerParams(has_side_effects=SPLIT_COPY))(*ops)
    return outs[0], outs[1], outs[2:2 + n], outs[2 + n:2 + m], outs[2 + m]


def _reduce_wait(send_sems, recv_sems, grads, landing, after, lays, tag):
    n, nr = len(grads), len(RELATIONS)
    m = n + n * nr

    def body(*refs):
        ss, rs = refs[m], refs[m + 1]
        src, land = refs[m + 3:m + 3 + n], refs[m + 3 + n:]
        for cp in _reduce_copies(lays, src, land, ss, rs):
            cp.wait_send()
            cp.wait_recv()

    ops = list(grads) + list(landing)
    outs = _call(body, name="reduce_wait_" + tag,
                 out_shape=[pltpu.HBM(o.shape, o.dtype) for o in ops],
                 in_specs=[HBM_SPEC] * m + [SEM_SPEC, SEM_SPEC, ANY_SPEC], out_specs=[HBM_SPEC] * m,
                 input_output_aliases={a: a for a in range(m)},
                 compiler_params=pltpu.CompilerParams(has_side_effects=SPLIT_COPY))(*ops, send_sems, recv_sems, after)
    return outs[:n], [outs[n + nr * a:n + nr * a + nr] for a in range(n)]


def _share_halves(shards):
    n = len(shards)

    def body(*refs):
        outs = refs[n:2 * n]
        send_sems, recv_sems = refs[2 * n:]
        x, y, c, _ = _place()
        sibling = (x, y, 1 - c)
        started = []
        for a in range(n):
            mine = _shard_half(outs[a], c)
            rc = pltpu.make_async_remote_copy(src_ref=mine, dst_ref=mine, send_sem=send_sems.at[a],
                                              recv_sem=recv_sems.at[a], device_id=sibling, device_id_type=MESH)
            rc.start()
            started.append(rc)
        for rc in started:
            rc.wait_recv()
            rc.wait_send()

    return _call(body, name="share_halves", out_shape=[_sds(s.shape, F32) for s in shards],
                 in_specs=[HBM_SPEC] * n, out_specs=[HBM_SPEC] * n, input_output_aliases={a: a for a in range(n)},
                 scratch_shapes=[pltpu.SemaphoreType.DMA((n,)), pltpu.SemaphoreType.DMA((n,))],
                 compiler_params=_params())(*shards)


def _reduce_end(state, after, lays, sel, tag):
    send_sems, recv_sems, grads, landing, _ = state
    grads, landed = _reduce_wait(send_sems, recv_sems, grads, landing, after, lays, tag)
    halves = [
        _view_sum(sel, [(g, lay.full(_sel_chip, _sel_core))] + [(l, lay.half_shard()) for l in ls], lay.shard(_sel_core),
                  lay.shard_shape, F32, (lay.lead, 1, lay.hr // lay.tr), "shard_half_sum")
        for g, ls, lay in zip(grads, landed, lays)]
    return _share_halves(halves)


def _embed(blocks):
    n, r, c = blocks.shape
    eye = jnp.eye(n, dtype=blocks.dtype)
    return (blocks[:, :, None, :] * eye[:, None, :, None]).reshape(n * r, n * c)


def _unembed(mat, n):
    r, c = mat.shape[0] // n, mat.shape[1] // n
    return jnp.transpose(jnp.diagonal(mat.reshape(n, r, n, c), axis1=0, axis2=2), (2, 0, 1))


def _to_heads(a):
    return jnp.transpose(a.reshape(a.shape[0], HEADS, HD), (1, 0, 2))


def _from_heads(a):
    return jnp.transpose(a, (1, 0, 2)).reshape(a.shape[1], W)


def _row(v):
    return v.reshape(1, -1)


def _concat_cols(pieces):
    L = pieces[0].shape[0]
    widths = [p.shape[1] for p in pieces]
    tr = _wide_tile(L)

    def body(*refs):
        off = 0
        for r, w in zip(refs[:-1], widths):
            refs[-1][:, off:off + w] = r[...].astype(BF16)
            off += w

    return _call(body, name="concat_cols", grid=(L // tr,),
                 in_specs=[pl.BlockSpec((tr, w), lambda i: (i, 0)) for w in widths],
                 out_specs=pl.BlockSpec((tr, sum(widths)), lambda i: (i, 0)), out_shape=_sds((L, sum(widths)), BF16),
                 compiler_params=_params(("parallel",)))(*pieces)


def _ffn_fwd(x, ada, gp, gq, w_in, w_out, s):
    L = x.shape[0]
    h = _norm_mod(x, _row(gp[s]), _row(ada[3 * s]), _row(ada[3 * s + 1]))
    a, b, act = _ffn_in(h, w_in)
    f = _mm(act, w_out, M=L, N=D, K=FF, tm=min(L, 1024), tn=512, name="ffn_out")
    x2 = _post(x, f, _row(gq[s]), _row(ada[3 * s + 2]), 0.5)
    return x2, (x, h, a, b, act, f)


def _ffn_bwd(dx, saved, ada, gp, gq, w_in, w_out, s, stats):
    x, h, a, b, act, f = saved
    L = x.shape[0]
    df, stats = _post_bwd(dx, f, _row(gq[s]), _row(ada[3 * s + 2]), 0.5, stats, s)
    dw_out = _mm(act, df, M=FF, N=D, K=L, tm=1408, tn=1024, ta=True, out_dtype=BF16, name="ffn_dw_out")
    da, db = _ffn_mid_bwd(df, w_out, a, b)
    du = _concat_cols([da, db])
    dw_in = _mm(h, du, M=D, N=2 * FF, K=L, tm=1024, tn=1408, ta=True, out_dtype=BF16, name="ffn_dw_in")
    dh = _mm(du, w_in, M=L, N=D, K=2 * FF, tm=min(L, 1024), tn=1024, tk=2816, tb=True, name="ffn_dh")
    dx2, stats = _norm_mod_bwd(dh, x, _row(gp[s]), _row(ada[3 * s + 1]), dx, stats, s)
    return dx2, dw_in, dw_out, stats


def _mixer_fwd(x, ada, gp, gq, wf, sm):
    L = x.shape[0]
    h = _norm_mod(x, _row(gp[1]), _row(ada[3]), _row(ada[4]))
    p = _mm(h, wf["w_in"], M=L, N=IN_COLS, K=D, tm=min(L, 2048), tn=1024, name="mixer_in")
    za = _conv_fwd(p, sm["conv_w"])
    y, zb = _ssm_fwd(p, sm["b_re"], sm["b_im"], sm["c_re"], sm["c_im"], sm["abr"], sm["abi"], sm["fr"], sm["fi"], sm["ssm_d"])
    zc = _pool_fwd(p, sm["w_pool"], sm["pool_scale"])
    q = _to_heads(p[:, 5 * W:6 * W]) * (HD ** -0.5)
    k = _to_heads(p[:, 6 * W:7 * W])
    v = _to_heads(p[:, 7 * W:8 * W])
    o_heads, block_sums = _sb_fwd(q, k, v)
    zd = _from_heads(o_heads).astype(BF16)
    merged = _merge_fwd(za, zb, zc, zd, p, wf["w_conv_out"], wf["w_glu"], wf["w_pool_out"], wf["w_sb_out"])
    m = _mm(merged, wf["w_out"], M=L, N=D, K=D, tm=min(L, 1024), tn=512, name="mixer_out")
    x2 = _post(x, m, _row(gq[1]), _row(ada[5]), 1.0)
    return x2, (x, h, p, za, y, zb, zc, zd, q, k, v, block_sums, merged, m)


def _mixer_bwd(dx, saved, ada, gp, gq, wf, sm, stats):
    x, h, p, za, y, zb, zc, zd, q, k, v, block_sums, merged, m = saved
    L = x.shape[0]
    dmf, stats = _post_bwd(dx, m, _row(gq[1]), _row(ada[5]), 1.0, stats, 1)
    dw_out = _mm(merged, dmf, M=D, N=D, K=L, tm=512, tn=512, ta=True, out_dtype=BF16, name="mixer_dw_out")
    dmerged = _mm(dmf, wf["w_out"], M=L, N=D, K=D, tm=min(L, 1024), tn=512, tb=True, name="mixer_dmerged")
    dza, dzb, dzc, dzd, dgates, dwc, dwg, dwp, dws = _merge_bwd(
        dmerged, za, zb, zc, zd, p, wf["w_conv_out"], wf["w_glu"], wf["w_pool_out"], wf["w_sb_out"])
    dconv, dconv_w = _conv_bwd(p, sm["conv_w"], dza)
    (du_ssm, dd, dbr, dbi, dcr, dci, gar, gai, gfr, gfi) = _ssm_bwd(
        p, y, dzb, sm["b_re"], sm["b_im"], sm["c_re"], sm["c_im"], sm["abr"], sm["abi"], sm["fr"], sm["fi"], sm["ssm_d"])
    du_pool, dwpool, dpscale = _pool_bwd(p, sm["w_pool"], sm["pool_scale"], dzc)
    dq, dk, dv = _sb_bwd(q, k, v, _to_heads(dzd), block_sums)
    dqkv = [_from_heads(t) for t in (dq * (HD ** -0.5), dk, dv)]
    dp = _concat_cols([dconv, du_ssm, du_pool] + dqkv + [dgates])
    dw_in = _mm(h, dp, M=D, N=IN_COLS, K=L, tm=1024, tn=1536, ta=True, out_dtype=BF16, name="mixer_dw_in")
    dh = _mm(dp, wf["w_in"], M=L, N=D, K=IN_COLS, tm=min(L, 1024), tn=1024, tk=3072, tb=True, name="mixer_dh")
    dx2, stats = _norm_mod_bwd(dh, x, _row(gp[1]), _row(ada[4]), dx, stats, 1)
    wgrads = [dw_in, dwc, dwg, dwp, dws, dw_out]
    small = {"conv_w": dconv_w, "ssm_d": dd, "b_re": dbr, "b_im": dbi, "c_re": dcr, "c_im": dci,
             "abr": gar, "abi": gai, "fr": gfr, "fi": gfi, "w_pool": dwpool, "pool_scale": dpscale}
    return dx2, wgrads, small, stats


def _pack(arrays):
    flat = jnp.concatenate([a.reshape(-1) for a in arrays])
    rows = -(-flat.shape[0] // 128)
    rows = -(-rows // 64) * 64
    return jnp.pad(flat, (0, rows * 128 - flat.shape[0])).reshape(rows, 128)


def _unpack(block, shapes):
    flat = block.reshape(-1)
    out, off = [], 0
    for s in shapes:
        n = int(np.prod(s))
        out.append(flat[off:off + n].reshape(s))
        off += n
    return out


def _pad_rows(a, mult):
    rows = -(-a.shape[0] // mult) * mult
    return jnp.concatenate([a] * (-(-rows // a.shape[0])), axis=0)[:rows]


SMALL_ORDER = ("stats", "conv_w", "lam_re", "lam_im", "log_dt", "ssm_b_re", "ssm_b_im",
               "ssm_c_re", "ssm_c_im", "ssm_d", "w_pool", "pool_scale")
WEIGHTS = ('w_ada', 'b_ada', 'g_pre', 'g_post', 'w_ff_in', 'w_ff_out', 'w_in', 'conv_w', 'w_conv_out', 'lam_re', 'lam_im',
           'log_dt', 'ssm_b_re', 'ssm_b_im', 'ssm_c_re', 'ssm_c_im', 'ssm_d', 'w_glu', 'w_pool', 'pool_scale', 'w_pool_out',
           'w_sb_out', 'w_out')


def _step(a):
    depth = a["w_ada"].shape[0]
    x = a["x"][0]
    target = a["loss_target"][0]
    L = x.shape[0]
    ix, iy, ic = lax.axis_index("x"), lax.axis_index("y"), lax.axis_index("c")
    chip = 2 * ix + iy
    me = 4 * ix + 2 * iy + ic
    sel = jnp.stack([ic, chip]).astype(jnp.int32)
    lays = [_Lay(a[name].shape[(1 if idx is None else 2):], ax) for name, idx, ax, _ in GATHERED]

    def entries(g):
        return [i for i, e in enumerate(GATHERED) if e[3] == g]

    def shard_of(i, l):
        name, idx = GATHERED[i][0], GATHERED[i][1]
        return a[name][l] if idx is None else a[name][l, idx]

    stages = [(l, g) for l in range(depth) for g in range(N_SUB)]

    def gather_begin(t, after):
        l, g = stages[t]
        placed = [_place_shard(lays[i], sel, shard_of(i, l)) for i in entries(g)]
        return _gather_start(placed, after, [lays[i] for i in entries(g)], str(t))

    pending = {t: gather_begin(t, x) for t in range(min(2, len(stages)))}
    started = sum(p[3][0, 0] for p in pending.values())

    first_shapes = [(D,), (depth, 3, W), (depth, 3, W), (depth, 3, W // N_CHIP)]
    gathered = _allgather8(_pack([a["c"] + started, a["g_pre"], a["g_post"], a["conv_w"]]), "gather_small_inputs")
    per_dev = [_unpack(blk, first_shapes) for blk in gathered.reshape(N_DEV, -1, 128)]
    c_all = jnp.stack([d[0] for d in per_dev])
    c_pad = jnp.concatenate([c_all, jnp.zeros_like(c_all)], axis=0)
    g_pre = jnp.concatenate([per_dev[2 * j][1] for j in range(N_CHIP)], axis=-1)
    g_post = jnp.concatenate([per_dev[2 * j][2] for j in range(N_CHIP)], axis=-1)
    conv_w = jnp.concatenate([per_dev[2 * j][3] for j in range(N_CHIP)], axis=-1)

    b_cols = lax.dynamic_slice(a["b_ada"], (0, chip * ADA_SHARD), (depth, ADA_SHARD)).reshape(depth, 1, ADA_SHARD)
    ada_part = _ada_fwd(c_pad, a["w_ada"], b_cols)
    ada_all = _allgather8(ada_part.reshape(depth * 16, ADA_SHARD), "gather_ada").reshape(N_DEV, depth, 16, ADA_SHARD)
    ada_rows = lax.dynamic_slice(ada_all, (0, 0, me, 0), (N_DEV, depth, 1, ADA_SHARD))[:, :, 0]
    ada = jnp.concatenate([ada_rows[2 * j] for j in range(N_CHIP)], axis=-1).reshape(depth, 9, D)

    lam_re = _pad_rows(a["lam_re"].reshape(depth, NST), 8)
    lam_im = _pad_rows(a["lam_im"].reshape(depth, NST), 8)
    log_dt_x = _pad_rows(jnp.repeat(a["log_dt"], GP, axis=1), 8)
    abr, abi, fr, fi = _ssm_prep(lam_re, lam_im, log_dt_x)

    def small_of(l):
        return {"conv_w": conv_w[l], "ssm_d": _row(a["ssm_d"][l]), "pool_scale": _row(a["pool_scale"][l]),
                "b_re": _embed(jnp.transpose(a["ssm_b_re"][l], (0, 2, 1))), "b_im": _embed(jnp.transpose(a["ssm_b_im"][l], (0, 2, 1))),
                "c_re": _embed(jnp.transpose(a["ssm_c_re"][l], (0, 2, 1))), "c_im": _embed(jnp.transpose(a["ssm_c_im"][l], (0, 2, 1))),
                "w_pool": _embed(a["w_pool"][l]),
                "abr": abr[l:l + 1], "abi": abi[l:l + 1], "fr": fr[l:l + 1], "fi": fi[l:l + 1]}

    saved, weights, smalls = [], [], [small_of(l) for l in range(depth)]
    for t, (l, g) in enumerate(stages):
        glays = [lays[i] for i in entries(g)]
        send_sems, recv_sems, bufs, _ = pending.pop(t)
        w = _gather_forward(_gather_wait(send_sems, recv_sems, bufs, x if t else ada, glays, str(t)), glays)
        weights.append(w)
        ada_l = ada[l]
        if t + 2 < len(stages):
            pending[t + 2] = gather_begin(t + 2, x)
            ada_l = ada_l + pending[t + 2][3][0, 0]
        if g == 1:
            wf = {GATHERED[i][0]: wi for i, wi in zip(entries(1), w)}
            x, sv = _mixer_fwd(x, ada_l, g_pre[l], g_post[l], wf, smalls[l])
        else:
            x, sv = _ffn_fwd(x, ada_l, g_pre[l], g_post[l], w[0], w[1], g)
        saved.append(sv)
    dx, loss_part = _loss_head(x, target)
    loss = lax.psum(loss_part[0, 0], ("x", "y", "c"))

    shard_grads = [[None] * depth for _ in GATHERED]
    small_grads = [{} for _ in range(depth)]
    stats = [jnp.zeros((STAT_ROWS, D), F32) for _ in range(depth)]
    states = {}

    def reduce_finish(t, after):
        l, g = stages[t]
        glays = [lays[i] for i in entries(g)]
        for i, grad in zip(entries(g), _reduce_end(states.pop(t), after, glays, sel, str(t))):
            shard_grads[i][l] = grad

    token = None
    for t in reversed(range(len(stages))):
        l, g = stages[t]
        ada_l = ada[l] if token is None else ada[l] + token[0, 0]
        if g == 1:
            wf = {GATHERED[i][0]: wi for i, wi in zip(entries(1), weights[t])}
            dx, wgrads, small, stats[l] = _mixer_bwd(dx, saved[t], ada_l, g_pre[l], g_post[l], wf, smalls[l], stats[l])
            small_grads[l].update(small)
        else:
            dx, dw_in, dw_out, stats[l] = _ffn_bwd(dx, saved[t], ada_l, g_pre[l], g_post[l], weights[t][0], weights[t][1], g,
                                                   stats[l])
            wgrads = [dw_in, dw_out]
        states[t] = _reduce_start(wgrads, [lays[i] for i in entries(g)], str(t))
        token = states[t][4]
        if t + 2 in states:
            reduce_finish(t + 2, dx)
    stack = lambda key: _pad_rows(jnp.concatenate([small_grads[l][key] for l in range(depth)], axis=0), 8)
    gs = np.zeros((NST, 128), np.float32)
    gs[np.arange(NST), np.arange(NST) // GP] = 1.0
    dlr, dli, dldt = _ssm_prep_bwd(lam_re, lam_im, log_dt_x, stack("abr"), stack("abi"), stack("fr"), stack("fi"), jnp.asarray(gs))
    part = {
        "stats": jnp.stack(stats) + sum(st[4][0, 0] for st in states.values()),
        "conv_w": jnp.stack([small_grads[l]["conv_w"] for l in range(depth)]),
        "lam_re": dlr[:depth].reshape(depth, G, GP), "lam_im": dli[:depth].reshape(depth, G, GP), "log_dt": dldt[:depth, :G],
        "ssm_b_re": jnp.stack([jnp.transpose(_unembed(small_grads[l]["b_re"], G), (0, 2, 1)) for l in range(depth)]),
        "ssm_b_im": jnp.stack([jnp.transpose(_unembed(small_grads[l]["b_im"], G), (0, 2, 1)) for l in range(depth)]),
        "ssm_c_re": jnp.stack([jnp.transpose(_unembed(small_grads[l]["c_re"], G), (0, 2, 1)) for l in range(depth)]),
        "ssm_c_im": jnp.stack([jnp.transpose(_unembed(small_grads[l]["c_im"], G), (0, 2, 1)) for l in range(depth)]),
        "ssm_d": jnp.stack([small_grads[l]["ssm_d"][0] for l in range(depth)]),
        "w_pool": jnp.stack([_unembed(small_grads[l]["w_pool"], len(POOL_WINDOWS)) for l in range(depth)]),
        "pool_scale": jnp.stack([small_grads[l]["pool_scale"][0] for l in range(depth)]),
    }
    small_shapes = [part[k].shape for k in SMALL_ORDER]
    blocks = _allgather8(_pack([part[k] for k in SMALL_ORDER]), "gather_small_grads").reshape(N_DEV, -1, 128)
    small_sum = _sum_parts([blocks[i] for i in range(N_DEV)], F32, "small_grad_sum")
    total = dict(zip(SMALL_ORDER, _unpack(small_sum, small_shapes)))
    d_ada_all = jnp.stack([_unpack(blocks[i], small_shapes[:1])[0][:, :9].reshape(depth, 9 * D) for i in range(N_DEV)])
    d_cols = lax.dynamic_slice(d_ada_all, (0, 0, chip * ADA_SHARD), (N_DEV, depth, ADA_SHARD))
    d_cols = jnp.transpose(d_cols, (1, 0, 2))
    grads = {"w_ada": _ada_wgrad(c_pad, jnp.concatenate([d_cols, jnp.zeros_like(d_cols)], axis=1)),
             "b_ada": total["stats"][:, :9].reshape(depth, 9 * D),
             "g_pre": lax.dynamic_slice(total["stats"], (0, 9, chip * W), (depth, 3, W)),
             "g_post": lax.dynamic_slice(total["stats"], (0, 12, chip * W), (depth, 3, W)),
             "conv_w": lax.dynamic_slice(total["conv_w"], (0, 0, chip * (W // N_CHIP)), (depth, 3, W // N_CHIP))}
    for k in SMALL_ORDER[2:]:
        grads[k] = total[k]

    out = {"loss": loss, "grad_x": dx[None]}

    def update(name):
        out["grad_" + name] = grads[name]
        out["delta_" + name], out["new_m_" + name], out["new_v_" + name] = _adamw(a[name], grads[name], a["m_" + name], a["v_" + name])

    for name in WEIGHTS:
        if name in grads:
            update(name)
    for t in sorted(states, reverse=True):
        reduce_finish(t, out["delta_w_ada"])
    for name in sorted({e[0] for e in GATHERED}):
        cols = [shard_grads[i] for i, e in enumerate(GATHERED) if e[0] == name]
        grads[name] = jnp.stack(cols[0]) if len(cols) == 1 else jnp.stack([jnp.stack(pair) for pair in zip(*cols)])
        update(name)
    return out


def kernel(x, c, w_ada, b_ada, g_pre, g_post, w_ff_in, w_ff_out, w_in, conv_w, w_conv_out, lam_re, lam_im, log_dt, ssm_b_re, ssm_b_im, ssm_c_re, ssm_c_im, ssm_d, w_glu, w_pool, pool_scale, w_pool_out, w_sb_out, w_out, loss_target, m_w_ada, m_b_ada, m_g_pre, m_g_post, m_w_ff_in, m_w_ff_out, m_w_in, m_conv_w, m_w_conv_out, m_lam_re, m_lam_im, m_log_dt, m_ssm_b_re, m_ssm_b_im, m_ssm_c_re, m_ssm_c_im, m_ssm_d, m_w_glu, m_w_pool, m_pool_scale, m_w_pool_out, m_w_sb_out, m_w_out, v_w_ada, v_b_ada, v_g_pre, v_g_post, v_w_ff_in, v_w_ff_out, v_w_in, v_conv_w, v_w_conv_out, v_lam_re, v_lam_im, v_log_dt, v_ssm_b_re, v_ssm_b_im, v_ssm_c_re, v_ssm_c_im, v_ssm_d, v_w_glu, v_w_pool, v_pool_scale, v_w_pool_out, v_w_sb_out, v_w_out):
    out = _step(dict(locals()))
    names = ["loss", "grad_x"] + [p + n for p in ("grad_", "delta_", "new_m_", "new_v_") for n in WEIGHTS]
    return tuple(out[n] for n in names)
```

```python
import functools
import math

import jax
import jax.numpy as jnp
import numpy as np
from jax import lax
from jax.experimental import pallas as pl
from jax.experimental.pallas import tpu as pltpu

F32 = jnp.float32
BF16 = jnp.bfloat16
MESH = pl.DeviceIdType.MESH

D = 1024
W = 256
FF = 2816
IN_COLS = 6144
G = 16
GH = 16
GP = 64
NST = G * GP
QB = 128
HEADS = 4
HD = 64
EPS = 1e-6
LAMBDA_RE_MAX = -1e-4
POOL_WINDOWS = (2, 4, 8, 16)
N_CHIP = 4
N_DEV = 8
VMEM_LIMIT = 56 * 1024 * 1024
HIGH = lax.Precision.HIGHEST

ADAM_LR, ADAM_B1, ADAM_B2, ADAM_EPS, ADAM_WD, ADAM_STEP = 0.001, 0.9, 0.999, 1e-08, 0.01, 10


def _call(body, **kw):
    return pl.pallas_call(body, **kw)


def _params(dims=None, **kw):
    return pltpu.CompilerParams(dimension_semantics=dims, vmem_limit_bytes=VMEM_LIMIT, **kw)


def _sds(shape, dtype):
    return jax.ShapeDtypeStruct(shape, dtype)


def _dot(a, b, ca=1, cb=0, precision=None):
    return lax.dot_general(a, b, (((ca,), (cb,)), ((), ())), preferred_element_type=F32, precision=precision)


def _bdot(a, b, ca=1, cb=0):
    return _dot(a.astype(BF16), b.astype(BF16), ca, cb)


def _sigmoid(x):
    return 1.0 / (1.0 + jnp.exp(-x))


def _mm(a, b, *, M, N, K, tm, tn, tk=None, ta=False, tb=False, out_dtype=F32, a_off=(0, 0), b_off=(0, 0), name):
    tk = K if tk is None else tk
    nk = K // tk
    assert M % tm == 0 and N % tn == 0 and K % tk == 0

    def body(a_ref, b_ref, o_ref, *acc):
        part = _bdot(a_ref[...], b_ref[...], 0 if ta else 1, 1 if tb else 0)
        if nk == 1:
            o_ref[...] = part.astype(out_dtype)
            return
        acc_ref = acc[0]
        k = pl.program_id(2)

        @pl.when(k == 0)
        def _():
            acc_ref[...] = part

        @pl.when(k > 0)
        def _():
            acc_ref[...] += part

        @pl.when(k == nk - 1)
        def _():
            o_ref[...] = acc_ref[...].astype(out_dtype)

    if ta:
        a_spec = pl.BlockSpec((tk, tm), lambda i, j, k: (k + a_off[0], i + a_off[1]))
    else:
        a_spec = pl.BlockSpec((tm, tk), lambda i, j, k: (i + a_off[0], k + a_off[1]))
    if tb:
        b_spec = pl.BlockSpec((tn, tk), lambda i, j, k: (j + b_off[0], k + b_off[1]))
    else:
        b_spec = pl.BlockSpec((tk, tn), lambda i, j, k: (k + b_off[0], j + b_off[1]))
    return _call(
        body, name=name, grid=(M // tm, N // tn, nk),
        in_specs=[a_spec, b_spec],
        out_specs=pl.BlockSpec((tm, tn), lambda i, j, k: (i, j)),
        out_shape=_sds((M, N), out_dtype),
        scratch_shapes=[] if nk == 1 else [pltpu.VMEM((tm, tn), F32)],
        compiler_params=_params(("parallel", "parallel", "arbitrary")),
    )(a, b)


def _row_tile(L):
    return min(L, 256)


def _wide_tile(L):
    return min(L, 512)


def _norm_mod(x, g, shift, scale):
    L = x.shape[0]
    tr = _wide_tile(L)

    def body(x_ref, g_ref, sh_ref, sc_ref, h_ref):
        xv = x_ref[...]
        r = lax.rsqrt(jnp.mean(xv * xv, axis=-1, keepdims=True) + EPS)
        h_ref[...] = (xv * r * g_ref[...] * (1.0 + sc_ref[...]) + sh_ref[...]).astype(BF16)

    row = pl.BlockSpec((tr, D), lambda i: (i, 0))
    vec = pl.BlockSpec((1, D), lambda i: (0, 0))
    return _call(body, name="norm_mod", grid=(L // tr,), in_specs=[row, vec, vec, vec], out_specs=row,
                 out_shape=_sds((L, D), BF16), compiler_params=_params(("parallel",)))(x, g, shift, scale)


STAT_ROWS = 16


def _norm_mod_bwd(dh, x, g, scale, dx_res, stats, s):
    L = x.shape[0]
    tr = _wide_tile(L)

    def body(dh_ref, x_ref, g_ref, sc_ref, dxr_ref, stin_ref, dx_ref, st_ref):
        i = pl.program_id(0)
        xv = x_ref[...]
        dhv = dh_ref[...]
        r = lax.rsqrt(jnp.mean(xv * xv, axis=-1, keepdims=True) + EPS)
        y = xv * r
        n = y * g_ref[...]
        dn = dhv * (1.0 + sc_ref[...])
        dy = dn * g_ref[...]
        dx_ref[...] = dxr_ref[...] + r * (dy - y * jnp.mean(dy * y, axis=-1, keepdims=True))

        @pl.when(i == 0)
        def _():
            st_ref[...] = stin_ref[...]

        st_ref[3 * s:3 * s + 1, :] += jnp.sum(dhv, axis=0, keepdims=True)
        st_ref[3 * s + 1:3 * s + 2, :] += jnp.sum(dhv * n, axis=0, keepdims=True)
        st_ref[9 + s:10 + s, :] += jnp.sum(dn * y, axis=0, keepdims=True)

    row = pl.BlockSpec((tr, D), lambda i: (i, 0))
    vec = pl.BlockSpec((1, D), lambda i: (0, 0))
    st = pl.BlockSpec((STAT_ROWS, D), lambda i: (0, 0))
    return _call(body, name="norm_mod_bwd", grid=(L // tr,), in_specs=[row, row, vec, vec, row, st],
                 out_specs=[row, st], out_shape=[_sds((L, D), F32), _sds((STAT_ROWS, D), F32)],
                 input_output_aliases={5: 1},
                 compiler_params=_params(("arbitrary",)))(dh, x, g, scale, dx_res, stats)


def _post(x, f, g, gate, res_weight):
    L = x.shape[0]
    tr = _wide_tile(L)

    def body(x_ref, f_ref, g_ref, gt_ref, o_ref):
        fv = f_ref[...]
        r = lax.rsqrt(jnp.mean(fv * fv, axis=-1, keepdims=True) + EPS)
        o_ref[...] = x_ref[...] + (res_weight * (1.0 + gt_ref[...])) * (fv * r * g_ref[...])

    row = pl.BlockSpec((tr, D), lambda i: (i, 0))
    vec = pl.BlockSpec((1, D), lambda i: (0, 0))
    return _call(body, name="post", grid=(L // tr,), in_specs=[row, row, vec, vec], out_specs=row,
                 out_shape=_sds((L, D), F32), compiler_params=_params(("parallel",)))(x, f, g, gate)


def _post_bwd(dx, f, g, gate, res_weight, stats, s):
    L = dx.shape[0]
    tr = _wide_tile(L)

    def body(dx_ref, f_ref, g_ref, gt_ref, stin_ref, df_ref, st_ref):
        i = pl.program_id(0)
        fv = f_ref[...]
        dxv = dx_ref[...]
        r = lax.rsqrt(jnp.mean(fv * fv, axis=-1, keepdims=True) + EPS)
        y = fv * r
        dn = dxv * (res_weight * (1.0 + gt_ref[...]))
        dy = dn * g_ref[...]
        df_ref[...] = (r * (dy - y * jnp.mean(dy * y, axis=-1, keepdims=True))).astype(BF16)

        @pl.when(i == 0)
        def _():
            st_ref[...] = stin_ref[...]

        st_ref[3 * s + 2:3 * s + 3, :] += res_weight * jnp.sum(dxv * (y * g_ref[...]), axis=0, keepdims=True)
        st_ref[12 + s:13 + s, :] += jnp.sum(dn * y, axis=0, keepdims=True)

    row = pl.BlockSpec((tr, D), lambda i: (i, 0))
    vec = pl.BlockSpec((1, D), lambda i: (0, 0))
    st = pl.BlockSpec((STAT_ROWS, D), lambda i: (0, 0))
    return _call(body, name="post_bwd", grid=(L // tr,), in_specs=[row, row, vec, vec, st],
                 out_specs=[row, st], out_shape=[_sds((L, D), BF16), _sds((STAT_ROWS, D), F32)],
                 input_output_aliases={4: 1},
                 compiler_params=_params(("arbitrary",)))(dx, f, g, gate, stats)


def _loss_head(x, target):
    L = x.shape[0]
    tr = _wide_tile(L)

    def body(x_ref, t_ref, dx_ref, loss_ref):
        i = pl.program_id(0)
        err = x_ref[...] - t_ref[...]
        dx_ref[...] = err * (1.0 / D)

        @pl.when(i == 0)
        def _():
            loss_ref[...] = jnp.zeros_like(loss_ref)

        loss_ref[...] += 0.5 * jnp.sum(jnp.mean(err * err, axis=-1, keepdims=True), axis=0, keepdims=True)

    row = pl.BlockSpec((tr, D), lambda i: (i, 0))
    return _call(body, name="loss_head", grid=(L // tr,), in_specs=[row, row],
                 out_specs=[row, pl.BlockSpec((1, 1), lambda i: (0, 0))],
                 out_shape=[_sds((L, D), F32), _sds((1, 1), F32)],
                 compiler_params=_params(("arbitrary",)))(x, target)


def _ffn_in(h, w_in):
    L = h.shape[0]
    tm, tn = min(L, 2048), 256
    nf = FF // tn

    def body(h_ref, wa_ref, wb_ref, a_ref, b_ref, act_ref):
        hv = h_ref[...]
        a = _dot(hv, wa_ref[...])
        b = _dot(hv, wb_ref[...])
        a_ref[...] = a
        b_ref[...] = b
        act_ref[...] = (a * _sigmoid(a) * b).astype(BF16)

    tile = pl.BlockSpec((tm, tn), lambda i, j: (i, j))
    return _call(body, name="ffn_in", grid=(L // tm, nf),
                 in_specs=[pl.BlockSpec((tm, D), lambda i, j: (i, 0)),
                           pl.BlockSpec((D, tn), lambda i, j: (0, j)),
                           pl.BlockSpec((D, tn), lambda i, j: (0, j + nf))],
                 out_specs=[tile, tile, tile],
                 out_shape=[_sds((L, FF), F32), _sds((L, FF), F32), _sds((L, FF), BF16)],
                 compiler_params=_params(("parallel", "parallel")))(h, w_in, w_in)


def _ffn_mid_bwd(df, w_out, a, b):
    L = df.shape[0]
    tm, tn = min(L, 2048), 256

    def body(df_ref, w_ref, a_ref, b_ref, da_ref, db_ref):
        dact = _dot(df_ref[...], w_ref[...], 1, 1)
        av = a_ref[...]
        sg = _sigmoid(av)
        da_ref[...] = (dact * b_ref[...] * (sg * (1.0 + av * (1.0 - sg)))).astype(BF16)
        db_ref[...] = (dact * (av * sg)).astype(BF16)

    tile = pl.BlockSpec((tm, tn), lambda i, j: (i, j))
    return _call(body, name="ffn_mid_bwd", grid=(L // tm, FF // tn),
                 in_specs=[pl.BlockSpec((tm, D), lambda i, j: (i, 0)),
                           pl.BlockSpec((tn, D), lambda i, j: (j, 0)), tile, tile],
                 out_specs=[tile, tile],
                 out_shape=[_sds((L, FF), BF16), _sds((L, FF), BF16)],
                 compiler_params=_params(("parallel", "parallel")))(df, w_out, a, b)


def _rows_before(ref, i, tr, halo):
    start = pl.multiple_of(jnp.maximum(i * tr - halo, 0), 8)
    return jnp.where(i > 0, ref[pl.ds(start, halo), :], 0.0)


def _rows_after(ref, i, n, tr, halo):
    start = pl.multiple_of(jnp.minimum((i + 1) * tr, (n - 1) * tr), 8)
    return jnp.where(i < n - 1, ref[pl.ds(start, halo), :], 0.0)


def _conv_fwd(p, conv_w):
    L = p.shape[0]
    tr = _row_tile(L)
    n = L // tr

    def body(bg_ref, cg_ref, v_ref, w_ref, za_ref, u_scr):
        i = pl.program_id(0)

        @pl.when(i == 0)
        def _():
            u_scr[...] = cg_ref[...] * v_ref[...]

        r0 = pl.multiple_of(i * tr, 8)
        ext = jnp.concatenate([_rows_before(u_scr, i, tr, 8), u_scr[pl.ds(r0, tr), :]], axis=0)
        w = w_ref[...]
        y = (w[0:1] * pltpu.roll(ext, 2, axis=0) + w[1:2] * pltpu.roll(ext, 1, axis=0) + w[2:3] * ext)[8:, :]
        za_ref[...] = (bg_ref[pl.ds(r0, tr), :] * y).astype(BF16)

    col = lambda c: pl.BlockSpec((L, W), lambda i: (0, c))
    return _call(body, name="conv_fwd", grid=(n,),
                 in_specs=[col(0), col(1), col(2), pl.BlockSpec((3, W), lambda i: (0, 0))],
                 out_specs=pl.BlockSpec((tr, W), lambda i: (i, 0)),
                 out_shape=_sds((L, W), BF16),
                 scratch_shapes=[pltpu.VMEM((L, W), F32)],
                 compiler_params=_params(("arbitrary",)))(p, p, p, conv_w)


def _conv_bwd(p, conv_w, dza):
    L = p.shape[0]
    tr = _row_tile(L)
    n = L // tr

    def body(bg_ref, cg_ref, v_ref, w_ref, dza_ref, dp_ref, dw_ref, u_scr, dy_scr):
        i = pl.program_id(0)

        @pl.when(i == 0)
        def _():
            u_scr[...] = cg_ref[...] * v_ref[...]
            dy_scr[...] = dza_ref[...] * bg_ref[...]
            dw_ref[...] = jnp.zeros_like(dw_ref)

        r0 = pl.multiple_of(i * tr, 8)
        w = w_ref[...]
        ext = jnp.concatenate([_rows_before(u_scr, i, tr, 8), u_scr[pl.ds(r0, tr), :]], axis=0)
        u2 = pltpu.roll(ext, 2, axis=0)[8:, :]
        u1 = pltpu.roll(ext, 1, axis=0)[8:, :]
        u0 = ext[8:, :]
        y = w[0:1] * u2 + w[1:2] * u1 + w[2:3] * u0
        dy = dy_scr[pl.ds(r0, tr), :]
        dext = jnp.concatenate([dy, _rows_after(dy_scr, i, n, tr, 8)], axis=0)
        m = tr + 8
        du = (w[2:3] * dext + w[1:2] * pltpu.roll(dext, m - 1, axis=0) + w[0:1] * pltpu.roll(dext, m - 2, axis=0))[:tr, :]
        dp_ref[:, 0:W] = (dza_ref[pl.ds(r0, tr), :] * y).astype(BF16)
        dp_ref[:, W:2 * W] = (du * v_ref[pl.ds(r0, tr), :]).astype(BF16)
        dp_ref[:, 2 * W:3 * W] = (du * cg_ref[pl.ds(r0, tr), :]).astype(BF16)
        dw_ref[...] += jnp.concatenate([jnp.sum(dy * u2, axis=0, keepdims=True),
                                        jnp.sum(dy * u1, axis=0, keepdims=True),
                                        jnp.sum(dy * u0, axis=0, keepdims=True)], axis=0)

    col = lambda c: pl.BlockSpec((L, W), lambda i: (0, c))
    return _call(body, name="conv_bwd", grid=(n,),
                 in_specs=[col(0), col(1), col(2), pl.BlockSpec((3, W), lambda i: (0, 0)),
                           pl.BlockSpec((L, W), lambda i: (0, 0))],
                 out_specs=[pl.BlockSpec((tr, 3 * W), lambda i: (i, 0)), pl.BlockSpec((3, W), lambda i: (0, 0))],
                 out_shape=[_sds((L, 3 * W), BF16), _sds((3, W), F32)],
                 scratch_shapes=[pltpu.VMEM((L, W), F32), pltpu.VMEM((L, W), F32)],
                 compiler_params=_params(("arbitrary",)))(p, p, p, conv_w, dza)


def _pool_windows(lane):
    wins = jnp.zeros(lane.shape, jnp.int32)
    for gi, w in enumerate(POOL_WINDOWS):
        wins = jnp.where(lane // (W // len(POOL_WINDOWS)) == gi, w, wins)
    return wins


def _pooled_block(u_ref, i, tr):
    r0 = pl.multiple_of(i * tr, 8)
    cur = u_ref[pl.ds(r0, tr), :]
    ext = jnp.concatenate([_rows_before(u_ref, i, tr, 16), cur], axis=0)
    s2 = ext + pltpu.roll(ext, 1, axis=0)
    s4 = s2 + pltpu.roll(s2, 2, axis=0)
    s8 = s4 + pltpu.roll(s4, 4, axis=0)
    s16 = s8 + pltpu.roll(s8, 8, axis=0)
    lane = lax.broadcasted_iota(jnp.int32, (tr, W), 1)
    wins = _pool_windows(lane)
    win_sum = jnp.where(wins == 2, s2[16:], jnp.where(wins == 4, s4[16:], jnp.where(wins == 8, s8[16:], s16[16:])))
    t = lax.broadcasted_iota(jnp.int32, (tr, W), 0) + i * tr
    cnt = jnp.minimum(t + 1, wins).astype(F32)
    return win_sum / cnt - cur, cnt


def _pool_fwd(p, w_pool_bd, pool_scale):
    L = p.shape[0]
    tr = _row_tile(L)

    def body(u_ref, w_ref, sc_ref, zc_ref):
        pooled, _ = _pooled_block(u_ref, pl.program_id(0), tr)
        zc_ref[...] = (_bdot(pooled, w_ref[...]) * sc_ref[...]).astype(BF16)

    return _call(body, name="pool_fwd", grid=(L // tr,),
                 in_specs=[pl.BlockSpec((L, W), lambda i: (0, 4)), pl.BlockSpec((W, W), lambda i: (0, 0)),
                           pl.BlockSpec((1, W), lambda i: (0, 0))],
                 out_specs=pl.BlockSpec((tr, W), lambda i: (i, 0)), out_shape=_sds((L, W), BF16),
                 compiler_params=_params(("arbitrary",)))(p, w_pool_bd, pool_scale)


def _pool_bwd(p, w_pool_bd, pool_scale, dzc):
    L = p.shape[0]
    tr = _row_tile(L)
    n = L // tr

    def body(u_ref, w_ref, sc_ref, dzc_ref, du_ref, dw_ref, dsc_ref, g_scr):
        i = pl.program_id(0)

        @pl.when(i == 0)
        def _():
            dw_ref[...] = jnp.zeros_like(dw_ref)
            dsc_ref[...] = jnp.zeros_like(dsc_ref)

            def rows(k, carry):
                r = pl.multiple_of(k * tr, 8)
                dmix = (dzc_ref[pl.ds(r, tr), :] * sc_ref[...]).astype(BF16)
                dpool = _dot(dmix, w_ref[...].astype(BF16), 1, 1)
                lane = lax.broadcasted_iota(jnp.int32, (tr, W), 1)
                t = lax.broadcasted_iota(jnp.int32, (tr, W), 0) + k * tr
                cnt = jnp.minimum(t + 1, _pool_windows(lane)).astype(F32)
                g_scr[pl.ds(r, tr), :] = dpool / cnt
                return carry

            lax.fori_loop(0, n, rows, 0)

        r0 = pl.multiple_of(i * tr, 8)
        pooled, cnt = _pooled_block(u_ref, i, tr)
        dzc = dzc_ref[pl.ds(r0, tr), :]
        mixed = _bdot(pooled, w_ref[...])
        dsc_ref[...] += jnp.sum(dzc * mixed, axis=0, keepdims=True)
        dmix = (dzc * sc_ref[...]).astype(BF16)
        dw_ref[...] += _dot(pooled.astype(BF16), dmix, 0, 0)
        gcur = g_scr[pl.ds(r0, tr), :]
        ext = jnp.concatenate([gcur, _rows_after(g_scr, i, n, tr, 16)], axis=0)
        m = tr + 16
        s2 = ext + pltpu.roll(ext, m - 1, axis=0)
        s4 = s2 + pltpu.roll(s2, m - 2, axis=0)
        s8 = s4 + pltpu.roll(s4, m - 4, axis=0)
        s16 = s8 + pltpu.roll(s8, m - 8, axis=0)
        lane = lax.broadcasted_iota(jnp.int32, (tr, W), 1)
        wins = _pool_windows(lane)
        ahead = jnp.where(wins == 2, s2[:tr], jnp.where(wins == 4, s4[:tr], jnp.where(wins == 8, s8[:tr], s16[:tr])))
        du_ref[...] = (ahead - gcur * cnt).astype(BF16)

    return _call(body, name="pool_bwd", grid=(n,),
                 in_specs=[pl.BlockSpec((L, W), lambda i: (0, 4)), pl.BlockSpec((W, W), lambda i: (0, 0)),
                           pl.BlockSpec((1, W), lambda i: (0, 0)), pl.BlockSpec((L, W), lambda i: (0, 0))],
                 out_specs=[pl.BlockSpec((tr, W), lambda i: (i, 0)), pl.BlockSpec((W, W), lambda i: (0, 0)),
                            pl.BlockSpec((1, W), lambda i: (0, 0))],
                 out_shape=[_sds((L, W), BF16), _sds((W, W), F32), _sds((1, W), F32)],
                 scratch_shapes=[pltpu.VMEM((L, W), F32)],
                 compiler_params=_params(("arbitrary",)))(p, w_pool_bd, pool_scale, dzc)


SSM_SLAB = 512


def _ssm_prep(lam_re, lam_im, log_dt_x):
    def body(lr_ref, li_ref, ldt_ref, abr_ref, abi_ref, fr_ref, fi_ref):
        lr = jnp.minimum(lr_ref[...], LAMBDA_RE_MAX)
        li = li_ref[...]
        dt = jnp.exp(ldt_ref[...])
        mag = jnp.exp(lr * dt)
        abr = mag * jnp.cos(li * dt)
        abi = mag * jnp.sin(li * dt)
        den = lr * lr + li * li
        nr = abr - 1.0
        abr_ref[...] = abr
        abi_ref[...] = abi
        fr_ref[...] = (nr * lr + abi * li) / den
        fi_ref[...] = (abi * lr - nr * li) / den

    shp = _sds(lam_re.shape, F32)
    return _call(body, name="ssm_prep", out_shape=[shp, shp, shp, shp], compiler_params=_params())(lam_re, lam_im, log_dt_x)


def _ssm_prep_bwd(lam_re, lam_im, log_dt_x, g_abr, g_abi, g_fr, g_fi, group_sum):
    def body(lr_ref, li_ref, ldt_ref, gar_ref, gai_ref, gfr_ref, gfi_ref, gs_ref, dlr_ref, dli_ref, dldt_ref):
        lam = lr_ref[...]
        lr = jnp.minimum(lam, LAMBDA_RE_MAX)
        li = li_ref[...]
        dt = jnp.exp(ldt_ref[...])
        mag = jnp.exp(lr * dt)
        abr = mag * jnp.cos(li * dt)
        abi = mag * jnp.sin(li * dt)
        den = lr * lr + li * li
        nr = abr - 1.0
        fr = (nr * lr + abi * li) / den
        fi = (abi * lr - nr * li) / den
        d_nre = gfr_ref[...] / den
        d_nim = gfi_ref[...] / den
        d_den = -(gfr_ref[...] * fr + gfi_ref[...] * fi) / den
        d_abr = gar_ref[...] + d_nre * lr - d_nim * li
        d_abi = gai_ref[...] + d_nre * li + d_nim * lr
        d_lr = d_nre * nr + d_nim * abi + d_den * 2.0 * lr
        d_li = d_nre * abi - d_nim * nr + d_den * 2.0 * li
        d_mag = d_abr * jnp.cos(li * dt) + d_abi * jnp.sin(li * dt)
        d_th = -d_abr * abi + d_abi * abr
        d_lr = d_lr + d_mag * mag * dt
        d_li = d_li + d_th * dt
        d_dt = d_mag * mag * lr + d_th * li
        passes = jnp.where(lam < LAMBDA_RE_MAX, 1.0, jnp.where(lam == LAMBDA_RE_MAX, 0.5, 0.0))
        dlr_ref[...] = d_lr * passes
        dli_ref[...] = d_li
        dldt_ref[...] = _dot(d_dt * dt, gs_ref[...], precision=HIGH)

    shp = _sds(lam_re.shape, F32)
    return _call(body, name="ssm_prep_bwd", out_shape=[shp, shp, _sds((lam_re.shape[0], 128), F32)],
                 compiler_params=_params())(lam_re, lam_im, log_dt_x, g_abr, g_abi, g_fr, g_fi, group_sum)


def _cmul(ar, ai, br, bi):
    return ar * br - ai * bi, ar * bi + ai * br


def _powers(ar, ai):
    out = [(ar, ai)]
    for _ in range(7):
        out.append(_cmul(out[-1][0], out[-1][1], ar, ai))
    return out


def _scan_rows(s_re, s_im, ar, ai, L, reverse=False, visit=None, visit_init=None):
    n = s_re.shape[1]
    pw = _powers(ar, ai)
    row = lax.broadcasted_iota(jnp.int32, (8, n), 0)
    dist = (8 - row) if reverse else (row + 1)
    pr = jnp.zeros((8, n), F32)
    pi = jnp.zeros((8, n), F32)
    for k in range(8):
        pr = jnp.where(dist == k + 1, pw[k][0], pr)
        pi = jnp.where(dist == k + 1, pw[k][1], pi)
    nb = L // 8

    def blk(t, carry):
        cr, ci, acc = carry
        b = (nb - 1 - t) if reverse else t
        r0 = pl.multiple_of(b * 8, 8)
        xr = s_re[pl.ds(r0, 8), :]
        xi = s_im[pl.ds(r0, 8), :]
        for d in (1, 2, 4):
            if reverse:
                keep = row < 8 - d
                sr, si = pltpu.roll(xr, 8 - d, axis=0), pltpu.roll(xi, 8 - d, axis=0)
            else:
                keep = row >= d
                sr, si = pltpu.roll(xr, d, axis=0), pltpu.roll(xi, d, axis=0)
            sr = jnp.where(keep, sr, 0.0)
            si = jnp.where(keep, si, 0.0)
            mr, mi = _cmul(pw[d - 1][0], pw[d - 1][1], sr, si)
            xr, xi = xr + mr, xi + mi
        mr, mi = _cmul(pr, pi, cr, ci)
        xr, xi = xr + mr, xi + mi
        s_re[pl.ds(r0, 8), :] = xr
        s_im[pl.ds(r0, 8), :] = xi
        if visit is not None:
            acc = visit(b, xr, xi, acc)
        if reverse:
            return xr[0:1, :], xi[0:1, :], acc
        return xr[7:8, :], xi[7:8, :], acc

    zero = jnp.zeros((1, n), F32)
    return lax.fori_loop(0, nb, blk, (zero, zero, visit_init if visit is not None else 0))[2]


def _ssm_project(u_ref, wbr, wbi, s_re, s_im, L):
    ch = min(L, 256)

    def rows(k, carry):
        r = pl.multiple_of(k * ch, 8)
        ub = u_ref[pl.ds(r, ch), :].astype(BF16)
        s_re[pl.ds(r, ch), :] = _dot(ub, wbr)
        s_im[pl.ds(r, ch), :] = _dot(ub, wbi)
        return carry

    lax.fori_loop(0, L // ch, rows, 0)


def _gelu(y):
    c = math.sqrt(2.0 / math.pi)
    return 0.5 * y * (1.0 + jnp.tanh(c * (y + 0.044715 * y * y * y)))


def _gelu_grad(y):
    c = math.sqrt(2.0 / math.pi)
    th = jnp.tanh(c * (y + 0.044715 * y * y * y))
    return 0.5 * (1.0 + th) + 0.5 * y * (1.0 - th * th) * c * (1.0 + 3.0 * 0.044715 * y * y)


def _ssm_fwd(p, b_re_bd, b_im_bd, c_re_bd, c_im_bd, abr, abi, fr, fi, d_skip):
    L = p.shape[0]
    ns = NST // SSM_SLAB
    ch = min(L, 256)

    def body(u_ref, br_ref, bi_ref, cr_ref, ci_ref, abr_ref, abi_ref, fr_ref, fi_ref, d_ref,
             y_ref, zb_ref, s_re, s_im):
        j = pl.program_id(0)
        f_re, f_im = fr_ref[...], fi_ref[...]
        wbr = (f_re * br_ref[...] - f_im * bi_ref[...]).astype(BF16)
        wbi = (f_re * bi_ref[...] + f_im * br_ref[...]).astype(BF16)
        _ssm_project(u_ref, wbr, wbi, s_re, s_im, L)
        _scan_rows(s_re, s_im, abr_ref[...], abi_ref[...], L)
        crb = cr_ref[...].astype(BF16)
        cib = ci_ref[...].astype(BF16)

        def rows(k, carry):
            r = pl.multiple_of(k * ch, 8)
            part = _dot(s_re[pl.ds(r, ch), :].astype(BF16), crb) - _dot(s_im[pl.ds(r, ch), :].astype(BF16), cib)

            @pl.when(j == 0)
            def _():
                y_ref[pl.ds(r, ch), :] = part + d_ref[...] * u_ref[pl.ds(r, ch), :]

            @pl.when(j > 0)
            def _():
                y_ref[pl.ds(r, ch), :] += part

            @pl.when(j == ns - 1)
            def _():
                zb_ref[pl.ds(r, ch), :] = _gelu(y_ref[pl.ds(r, ch), :]).astype(BF16)

            return carry

        lax.fori_loop(0, L // ch, rows, 0)

    full = lambda shape: pl.BlockSpec(shape, lambda j: (0, 0))
    lanes = pl.BlockSpec((1, SSM_SLAB), lambda j: (0, j))
    return _call(body, name="ssm_fwd", grid=(ns,),
                 in_specs=[pl.BlockSpec((L, W), lambda j: (0, 3)),
                           pl.BlockSpec((W, SSM_SLAB), lambda j: (0, j)), pl.BlockSpec((W, SSM_SLAB), lambda j: (0, j)),
                           pl.BlockSpec((SSM_SLAB, W), lambda j: (j, 0)), pl.BlockSpec((SSM_SLAB, W), lambda j: (j, 0)),
                           lanes, lanes, lanes, lanes, full((1, W))],
                 out_specs=[full((L, W)), full((L, W))],
                 out_shape=[_sds((L, W), F32), _sds((L, W), BF16)],
                 scratch_shapes=[pltpu.VMEM((L, SSM_SLAB), F32), pltpu.VMEM((L, SSM_SLAB), F32)],
                 compiler_params=_params(("arbitrary",)))(p, b_re_bd, b_im_bd, c_re_bd, c_im_bd, abr, abi, fr, fi, d_skip)


def _ssm_bwd(p, y, dzb, b_re_bd, b_im_bd, c_re_bd, c_im_bd, abr, abi, fr, fi, d_skip):
    L = p.shape[0]
    ns = NST // SSM_SLAB
    ch = min(L, 256)
    n_ch = L // ch

    def body(u_ref, y_ref, dzb_ref, br_ref, bi_ref, cr_ref, ci_ref, abr_ref, abi_ref, fr_ref, fi_ref, d_ref,
             du_ref, dd_ref, dbr_ref, dbi_ref, dcr_ref, dci_ref, gar_ref, gai_ref, gfr_ref, gfi_ref,
             s_re, s_im, l_re, l_im, dy_scr, du_scr):
        j = pl.program_id(0)
        f_re, f_im = fr_ref[...], fi_ref[...]
        b_re, b_im = br_ref[...], bi_ref[...]
        wbr = (f_re * b_re - f_im * b_im).astype(BF16)
        wbi = (f_re * b_im + f_im * b_re).astype(BF16)
        a_re, a_im = abr_ref[...], abi_ref[...]

        @pl.when(j == 0)
        def _():
            def rows(k, acc):
                r = pl.multiple_of(k * ch, 8)
                dy = dzb_ref[pl.ds(r, ch), :] * _gelu_grad(y_ref[pl.ds(r, ch), :])
                dy_scr[pl.ds(r, ch), :] = dy
                du_scr[pl.ds(r, ch), :] = d_ref[...] * dy
                return acc + jnp.sum(dy * u_ref[pl.ds(r, ch), :], axis=0, keepdims=True)

            dd_ref[...] = lax.fori_loop(0, n_ch, rows, jnp.zeros((1, W), F32))

        _ssm_project(u_ref, wbr, wbi, s_re, s_im, L)
        _scan_rows(s_re, s_im, a_re, a_im, L)
        crb = cr_ref[...].astype(BF16)
        cib = ci_ref[...].astype(BF16)

        def rows_c(k, acc):
            dcr, dci = acc
            r = pl.multiple_of(k * ch, 8)
            dyb = dy_scr[pl.ds(r, ch), :].astype(BF16)
            dcr = dcr + _dot(s_re[pl.ds(r, ch), :].astype(BF16), dyb, 0, 0)
            dci = dci - _dot(s_im[pl.ds(r, ch), :].astype(BF16), dyb, 0, 0)
            l_re[pl.ds(r, ch), :] = _dot(dyb, crb, 1, 1)
            l_im[pl.ds(r, ch), :] = -_dot(dyb, cib, 1, 1)
            return dcr, dci

        zc = jnp.zeros((SSM_SLAB, W), F32)
        dcr, dci = lax.fori_loop(0, n_ch, rows_c, (zc, zc))
        dcr_ref[...] = dcr
        dci_ref[...] = dci

        row8 = lax.broadcasted_iota(jnp.int32, (8, SSM_SLAB), 0)

        def visit(b, lr, li, acc):
            ar_acc, ai_acc = acc
            r0 = pl.multiple_of(b * 8, 8)
            rp = pl.multiple_of(jnp.maximum(b * 8 - 8, 0), 8)
            has_prev = b > 0
            pr = jnp.where(has_prev, s_re[pl.ds(rp, 8), :][7:8, :], 0.0)
            pi = jnp.where(has_prev, s_im[pl.ds(rp, 8), :][7:8, :], 0.0)
            sr = jnp.where(row8 >= 1, pltpu.roll(s_re[pl.ds(r0, 8), :], 1, axis=0), pr)
            si = jnp.where(row8 >= 1, pltpu.roll(s_im[pl.ds(r0, 8), :], 1, axis=0), pi)
            return ar_acc + lr * sr + li * si, ai_acc - lr * si + li * sr

        z8 = jnp.zeros((8, SSM_SLAB), F32)
        ar_acc, ai_acc = _scan_rows(l_re, l_im, a_re, -a_im, L, reverse=True, visit=visit, visit_init=(z8, z8))
        gar_ref[...] = jnp.sum(ar_acc, axis=0, keepdims=True)
        gai_ref[...] = jnp.sum(ai_acc, axis=0, keepdims=True)

        def rows_b(k, acc):
            dwr, dwi = acc
            r = pl.multiple_of(k * ch, 8)
            ub = u_ref[pl.ds(r, ch), :].astype(BF16)
            lrb = l_re[pl.ds(r, ch), :].astype(BF16)
            lib = l_im[pl.ds(r, ch), :].astype(BF16)
            du_scr[pl.ds(r, ch), :] += _dot(lrb, wbr, 1, 1) + _dot(lib, wbi, 1, 1)
            return dwr + _dot(ub, lrb, 0, 0), dwi + _dot(ub, lib, 0, 0)

        zb = jnp.zeros((W, SSM_SLAB), F32)
        dwr, dwi = lax.fori_loop(0, n_ch, rows_b, (zb, zb))
        dbr_ref[...] = dwr * f_re + dwi * f_im
        dbi_ref[...] = -dwr * f_im + dwi * f_re
        gfr_ref[...] = jnp.sum(dwr * b_re + dwi * b_im, axis=0, keepdims=True)
        gfi_ref[...] = jnp.sum(-dwr * b_im + dwi * b_re, axis=0, keepdims=True)

        @pl.when(j == ns - 1)
        def _():
            du_ref[...] = du_scr[...].astype(BF16)

    full = lambda shape: pl.BlockSpec(shape, lambda j: (0, 0))
    lanes = pl.BlockSpec((1, SSM_SLAB), lambda j: (0, j))
    bspec = pl.BlockSpec((W, SSM_SLAB), lambda j: (0, j))
    cspec = pl.BlockSpec((SSM_SLAB, W), lambda j: (j, 0))
    slab = lambda: pltpu.VMEM((L, SSM_SLAB), F32)
    return _call(body, name="ssm_bwd", grid=(ns,),
                 in_specs=[pl.BlockSpec((L, W), lambda j: (0, 3)), full((L, W)), full((L, W)),
                           bspec, bspec, cspec, cspec, lanes, lanes, lanes, lanes, full((1, W))],
                 out_specs=[full((L, W)), full((1, W)), bspec, bspec, cspec, cspec, lanes, lanes, lanes, lanes],
                 out_shape=[_sds((L, W), BF16), _sds((1, W), F32), _sds((W, NST), F32), _sds((W, NST), F32),
                            _sds((NST, W), F32), _sds((NST, W), F32)] + [_sds((1, NST), F32)] * 4,
                 scratch_shapes=[slab(), slab(), slab(), slab(), pltpu.VMEM((L, W), F32), pltpu.VMEM((L, W), F32)],
                 compiler_params=_params(("arbitrary",)))(p, y, dzb, b_re_bd, b_im_bd, c_re_bd, c_im_bd,
                                                          abr, abi, fr, fi, d_skip)


SB_KB = 512


SB_SUB = 256


def _split2(x):
    hi = x.astype(BF16)
    return hi, (x - hi.astype(F32)).astype(BF16)


def _ones_dot(x, ones):
    n = x.shape[0]
    r = _dot(jnp.concatenate(_split2(x), axis=0), ones)
    return r[:n] + r[n:]


def _suffix_sums(x, tri):
    sub = tri.shape[0]
    parts = [_ones_dot(x[:, i:i + sub], tri) for i in range(0, x.shape[1], sub)]
    out, after = [], None
    for p in reversed(parts):
        out.append(p if after is None else p + after)
        after = p[:, 0:1] if after is None else after + p[:, 0:1]
    return jnp.concatenate(out[::-1], axis=1)


def _prefix_sums_exclusive(x, tri_le):
    sub = tri_le.shape[0]
    out, before = [], None
    for i in range(0, x.shape[1], sub):
        xi = x[:, i:i + sub]
        inc = _ones_dot(xi, tri_le)
        out.append(inc - xi if before is None else inc - xi + before)
        before = inc[:, sub - 1:sub] if before is None else before + inc[:, sub - 1:sub]
    return jnp.concatenate(out, axis=1)


def _sb_block(q, kj, i, jb, kb, right, tri):
    z = _bdot(q, kj, 1, 1)
    t_idx = lax.broadcasted_iota(jnp.int32, (QB, kb), 0) + i * QB
    s_idx = lax.broadcasted_iota(jnp.int32, (QB, kb), 1) + jb * kb
    mask = s_idx < t_idx
    lk_all = jnp.minimum(-z, 0.0) - jnp.log1p(jnp.exp(-jnp.abs(z)))
    lk = jnp.where(mask, lk_all, 0.0)
    suf = _suffix_sums(lk, tri)
    a = jnp.where(mask, jnp.exp((lk_all + z) + (suf - lk) + right), 0.0)
    return z, mask, suf, a


def _sb_ones(kb):
    sub = min(SB_SUB, kb)
    r = lax.broadcasted_iota(jnp.int32, (sub, sub), 0)
    c = lax.broadcasted_iota(jnp.int32, (sub, sub), 1)
    return (r >= c).astype(BF16), (r <= c).astype(BF16)


def _sb_fwd(q, k, v):
    L = q.shape[1]
    kb = min(SB_KB, L)
    per = kb // QB

    def body(q_ref, k_ref, v_ref, o_ref, rs_ref):
        i = pl.program_id(0)
        tri, _ = _sb_ones(kb)
        lane = lax.broadcasted_iota(jnp.int32, (QB, 128), 1)
        qs = [q_ref[h] for h in range(HEADS)]

        def step(t, carry):
            accs, rights, sums = carry
            jb = i // per - t
            r = pl.multiple_of(jb * kb, kb)
            out = []
            for h in range(HEADS):
                _, _, suf, a = _sb_block(qs[h], k_ref[h, pl.ds(r, kb), :], i, jb, kb, rights[h], tri)
                tot = suf[:, 0:1]
                out.append((accs[h] + _bdot(a, v_ref[h, pl.ds(r, kb), :]), rights[h] + tot,
                            sums[h] + jnp.where(lane == jb, tot, 0.0)))
            return tuple(o[0] for o in out), tuple(o[1] for o in out), tuple(o[2] for o in out)

        init = (tuple(jnp.zeros((QB, HD), F32) for _ in range(HEADS)), tuple(jnp.zeros((QB, 1), F32) for _ in range(HEADS)),
                tuple(jnp.zeros((QB, 128), F32) for _ in range(HEADS)))
        accs, _, sums = lax.fori_loop(0, i // per + 1, step, init)
        for h in range(HEADS):
            o_ref[h] = accs[h]
            rs_ref[h] = sums[h]

    heads = pl.BlockSpec((HEADS, L, HD), lambda i: (0, 0, 0))
    blk = pl.BlockSpec((HEADS, QB, HD), lambda i: (0, i, 0))
    return _call(body, name="sb_fwd", grid=(L // QB,), in_specs=[blk, heads, heads],
                 out_specs=[blk, pl.BlockSpec((HEADS, QB, 128), lambda i: (0, i, 0))],
                 out_shape=[_sds((HEADS, L, HD), F32), _sds((HEADS, L, 128), F32)],
                 compiler_params=_params(("parallel",)))(q, k, v)


def _sb_bwd(q, k, v, do, block_sums):
    L = q.shape[1]
    kb = min(SB_KB, L)
    per = kb // QB

    def body(q_ref, k_ref, v_ref, do_ref, rs_ref, dq_ref, dk_ref, dv_ref):
        i = pl.program_id(0)
        tri, tri_le = _sb_ones(kb)
        lane = lax.broadcasted_iota(jnp.int32, (QB, 128), 1)

        @pl.when(i == 0)
        def _():
            dk_ref[...] = jnp.zeros_like(dk_ref)
            dv_ref[...] = jnp.zeros_like(dv_ref)

        qs = [q_ref[h] for h in range(HEADS)]
        dos = [do_ref[h] for h in range(HEADS)]
        sums = [rs_ref[h] for h in range(HEADS)]

        def step(jb, carry):
            dqs, lefts = carry
            r = pl.multiple_of(jb * kb, kb)
            out = []
            for h in range(HEADS):
                kj = k_ref[h, pl.ds(r, kb), :]
                vj = v_ref[h, pl.ds(r, kb), :]
                right = jnp.sum(jnp.where(lane > jb, sums[h], 0.0), axis=1, keepdims=True)
                z, mask, _, a = _sb_block(qs[h], kj, i, jb, kb, right, tri)
                e = a * _bdot(dos[h], vj, 1, 1)
                dv_ref[h, pl.ds(r, kb), :] += _dot(a.astype(BF16), dos[h].astype(BF16), 0, 0)
                before = lefts[h] + _prefix_sums_exclusive(e, tri_le)
                sg = _sigmoid(z)
                dz = jnp.where(mask, e * (1.0 - sg) - sg * before, 0.0).astype(BF16)
                dk_ref[h, pl.ds(r, kb), :] += _dot(dz, qs[h].astype(BF16), 0, 0)
                out.append((dqs[h] + _dot(dz, kj.astype(BF16)), lefts[h] + jnp.sum(e, axis=1, keepdims=True)))
            return tuple(o[0] for o in out), tuple(o[1] for o in out)

        init = (tuple(jnp.zeros((QB, HD), F32) for _ in range(HEADS)), tuple(jnp.zeros((QB, 1), F32) for _ in range(HEADS)))
        dqs, _ = lax.fori_loop(0, i // per + 1, step, init)
        for h in range(HEADS):
            dq_ref[h] = dqs[h]

    heads = pl.BlockSpec((HEADS, L, HD), lambda i: (0, 0, 0))
    blk = pl.BlockSpec((HEADS, QB, HD), lambda i: (0, i, 0))
    shp = _sds((HEADS, L, HD), F32)
    return _call(body, name="sb_bwd", grid=(L // QB,),
                 in_specs=[blk, heads, heads, blk, pl.BlockSpec((HEADS, QB, 128), lambda i: (0, i, 0))],
                 out_specs=[blk, heads, heads], out_shape=[shp, shp, shp],
                 compiler_params=_params(("arbitrary",)))(q, k, v, do, block_sums)


def _merge_fwd(za, zb, zc, zd, p, w_conv_out, w_glu, w_pool_out, w_sb_out):
    L = za.shape[0]
    tm = _row_tile(L)

    def body(za_ref, zb_ref, zc_ref, zd_ref, g0, g1, g2, g3, wc_ref, wg_ref, wp_ref, ws_ref, o_ref):
        glu = _dot(zb_ref[...], wg_ref[...])
        yb = glu[:, :D] * _sigmoid(glu[:, D:])
        m = _sigmoid(g0[...]) * _dot(za_ref[...], wc_ref[...])
        m = m + _sigmoid(g1[...]) * yb
        m = m + _sigmoid(g2[...]) * _dot(zc_ref[...], wp_ref[...])
        m = m + _sigmoid(g3[...]) * _dot(zd_ref[...], ws_ref[...])
        o_ref[...] = m.astype(BF16)

    zt = pl.BlockSpec((tm, W), lambda i: (i, 0))
    gate = lambda b: pl.BlockSpec((tm, D), lambda i: (i, 2 + b))
    wfull = lambda n: pl.BlockSpec((W, n), lambda i: (0, 0))
    return _call(body, name="merge_fwd", grid=(L // tm,),
                 in_specs=[zt, zt, zt, zt, gate(0), gate(1), gate(2), gate(3), wfull(D), wfull(2 * D), wfull(D), wfull(D)],
                 out_specs=pl.BlockSpec((tm, D), lambda i: (i, 0)), out_shape=_sds((L, D), BF16),
                 compiler_params=_params(("parallel",)))(za, zb, zc, zd, p, p, p, p, w_conv_out, w_glu, w_pool_out, w_sb_out)


def _merge_bwd(dm, za, zb, zc, zd, p, w_conv_out, w_glu, w_pool_out, w_sb_out):
    L = za.shape[0]
    tm = _row_tile(L)
    n = L // tm

    def body(dm_ref, za_ref, zb_ref, zc_ref, zd_ref, g0, g1, g2, g3, wc_ref, wg_ref, wp_ref, ws_ref,
             dza_ref, dzb_ref, dzc_ref, dzd_ref, dg_ref, dwc_ref, dwg_ref, dwp_ref, dws_ref,
             awc, awg, awp, aws):
        i = pl.program_id(0)

        @pl.when(i == 0)
        def _():
            awc[...] = jnp.zeros_like(awc)
            awg[...] = jnp.zeros_like(awg)
            awp[...] = jnp.zeros_like(awp)
            aws[...] = jnp.zeros_like(aws)

        dmv = dm_ref[...]

        def gated(g_ref, y, col):
            s = _sigmoid(g_ref[...])
            dg_ref[:, col * D:(col + 1) * D] = (dmv * y * s * (1.0 - s)).astype(BF16)
            return (dmv * s)

        def linear(z_ref, w_ref, acc, dz_ref, col, g_ref):
            zv = z_ref[...]
            dy = gated(g_ref, _dot(zv, w_ref[...]), col).astype(BF16)
            dz_ref[...] = _dot(dy, w_ref[...], 1, 1)
            acc[...] += _dot(zv, dy, 0, 0)

        linear(za_ref, wc_ref, awc, dza_ref, 0, g0)
        linear(zc_ref, wp_ref, awp, dzc_ref, 2, g2)
        linear(zd_ref, ws_ref, aws, dzd_ref, 3, g3)
        zbv = zb_ref[...]
        glu = _dot(zbv, wg_ref[...])
        ga = glu[:, :D]
        sg = _sigmoid(glu[:, D:])
        dyb = gated(g1, ga * sg, 1)
        dga = (dyb * sg).astype(BF16)
        dgg = (dyb * ga * sg * (1.0 - sg)).astype(BF16)
        dzb_ref[...] = _dot(dga, wg_ref[:, :D], 1, 1) + _dot(dgg, wg_ref[:, D:], 1, 1)
        awg[:, :D] += _dot(zbv, dga, 0, 0)
        awg[:, D:] += _dot(zbv, dgg, 0, 0)

        @pl.when(i == n - 1)
        def _():
            dwc_ref[...] = awc[...].astype(BF16)
            dwg_ref[...] = awg[...].astype(BF16)
            dwp_ref[...] = awp[...].astype(BF16)
            dws_ref[...] = aws[...].astype(BF16)

    zt = pl.BlockSpec((tm, W), lambda i: (i, 0))
    gate = lambda b: pl.BlockSpec((tm, D), lambda i: (i, 2 + b))
    wfull = lambda n_: pl.BlockSpec((W, n_), lambda i: (0, 0))
    zs = _sds((L, W), F32)
    return _call(body, name="merge_bwd", grid=(n,),
                 in_specs=[pl.BlockSpec((tm, D), lambda i: (i, 0)), zt, zt, zt, zt, gate(0), gate(1), gate(2), gate(3),
                           wfull(D), wfull(2 * D), wfull(D), wfull(D)],
                 out_specs=[zt, zt, zt, zt, pl.BlockSpec((tm, 4 * D), lambda i: (i, 0)),
                            wfull(D), wfull(2 * D), wfull(D), wfull(D)],
                 out_shape=[zs, zs, zs, zs, _sds((L, 4 * D), BF16),
                            _sds((W, D), BF16), _sds((W, 2 * D), BF16), _sds((W, D), BF16), _sds((W, D), BF16)],
                 scratch_shapes=[pltpu.VMEM((W, D), F32), pltpu.VMEM((W, 2 * D), F32), pltpu.VMEM((W, D), F32),
                                 pltpu.VMEM((W, D), F32)],
                 compiler_params=_params(("arbitrary",)))(dm, za, zb, zc, zd, p, p, p, p,
                                                          w_conv_out, w_glu, w_pool_out, w_sb_out)


def _adam_math(w, g, m, v):
    m2 = ADAM_B1 * m + (1.0 - ADAM_B1) * g
    v2 = ADAM_B2 * v + (1.0 - ADAM_B2) * (g * g)
    m_hat = m2 / (1.0 - ADAM_B1 ** ADAM_STEP)
    v_hat = v2 / (1.0 - ADAM_B2 ** ADAM_STEP)
    return -ADAM_LR * (m_hat / (jnp.sqrt(v_hat) + ADAM_EPS) + ADAM_WD * w), m2, v2


def _as_rows(a):
    return a.reshape(-1, a.shape[-1])


def _adamw(w, g, m, v):
    shape = w.shape
    w2, g2, m2, v2 = _as_rows(w), _as_rows(g), _as_rows(m), _as_rows(v)
    R, C = w2.shape
    tr = R
    for cand in (1024, 512, 256, 128, 64, 32, 16, 8):
        if R % cand == 0 and cand * C * 4 <= 2 * 1024 * 1024:
            tr = cand
            break

    def body(w_ref, g_ref, m_ref, v_ref, d_ref, m_out, v_out):
        d, mn, vn = _adam_math(w_ref[...], g_ref[...], m_ref[...], v_ref[...])
        d_ref[...] = d
        m_out[...] = mn
        v_out[...] = vn

    blk = pl.BlockSpec((tr, C), lambda i: (i, 0))
    shp = _sds((R, C), F32)
    outs = _call(body, name="adamw", grid=(R // tr,), in_specs=[blk] * 4, out_specs=[blk] * 3, out_shape=[shp] * 3,
                 compiler_params=_params(("parallel",)))(w2, g2, m2, v2)
    return tuple(o.reshape(shape) for o in outs)


def _sum_parts(parts, out_dtype, name):
    shape = parts[0].shape
    flat = [_as_rows(a) for a in parts]
    R, C = flat[0].shape
    tr = R
    for cand in (1024, 512, 256, 128, 64, 32, 16):
        if R % cand == 0 and cand * C * 4 <= 2 * 1024 * 1024:
            tr = cand
            break
    k = len(parts)

    def body(*refs):
        acc = refs[0][...].astype(F32)
        for r in refs[1:k]:
            acc = acc + r[...].astype(F32)
        refs[k][...] = acc.astype(out_dtype)

    blk = pl.BlockSpec((tr, C), lambda i: (i, 0))
    out = _call(body, name=name, grid=(R // tr,), in_specs=[blk] * k, out_specs=blk, out_shape=_sds((R, C), out_dtype),
                compiler_params=_params(("parallel",)))(*flat)
    return out.reshape(shape)


ADA_SHARD = 9 * D // N_CHIP
ADA_TN = 768


def _ada_fwd(c_pad, w_ada, b_ada_cols):
    depth = w_ada.shape[0]

    def body(c_ref, w_ref, b_ref, o_ref):
        cv = c_ref[...]
        o_ref[...] = _bdot(cv * _sigmoid(cv), w_ref[...]) + b_ref[...]

    return _call(body, name="ada_fwd", grid=(depth, ADA_SHARD // ADA_TN),
                 in_specs=[pl.BlockSpec((16, D), lambda l, j: (0, 0)),
                           pl.BlockSpec((None, D, ADA_TN), lambda l, j: (l, 0, j)),
                           pl.BlockSpec((None, 1, ADA_TN), lambda l, j: (l, 0, j))],
                 out_specs=pl.BlockSpec((None, 16, ADA_TN), lambda l, j: (l, 0, j)),
                 out_shape=_sds((depth, 16, ADA_SHARD), F32),
                 compiler_params=_params(("parallel", "parallel")))(c_pad, w_ada, b_ada_cols)


def _ada_wgrad(c_pad, d_ada):
    depth = d_ada.shape[0]

    def body(c_ref, d_ref, o_ref):
        cv = c_ref[...]
        o_ref[...] = _bdot(cv * _sigmoid(cv), d_ref[...], 0, 0)

    return _call(body, name="ada_wgrad", grid=(depth, ADA_SHARD // ADA_TN),
                 in_specs=[pl.BlockSpec((16, D), lambda l, j: (0, 0)),
                           pl.BlockSpec((None, 16, ADA_TN), lambda l, j: (l, 0, j))],
                 out_specs=pl.BlockSpec((None, D, ADA_TN), lambda l, j: (l, 0, j)),
                 out_shape=_sds((depth, D, ADA_SHARD), F32),
                 compiler_params=_params(("parallel", "parallel")))(c_pad, d_ada)


HBM_SPEC = pl.BlockSpec(memory_space=pltpu.HBM)


def _place():
    x, y, c = lax.axis_index("x"), lax.axis_index("y"), lax.axis_index("c")
    peers = [(1 - x, y), (x, 1 - y), (1 - x, 1 - y)]
    return x, y, c, peers


def _chip(px, py):
    return 2 * px + py


def _allgather8(block, name):
    m_per, n = block.shape

    def body(x_ref, out_ref, send_sems, recv_sems, local_sem):
        x, y, c, chips = _place()
        me, sibling = (x, y, c), (x, y, 1 - c)

        def rows(px, py, pc):
            return out_ref.at[pl.ds(pl.multiple_of((4 * px + 2 * py + pc) * m_per, 8), m_per), :]

        def copy(k, blk, to, src=None):
            return pltpu.make_async_remote_copy(
                src_ref=rows(*blk) if src is None else src, dst_ref=rows(*blk),
                send_sem=send_sems.at[k], recv_sem=recv_sems.at[k], device_id=to, device_id_type=MESH)

        mine = pltpu.make_async_copy(x_ref, rows(*me), local_sem)
        mine.start()
        first = [copy(0, me, sibling, src=x_ref)]
        first += [copy(1 + j, me, (*chip, c), src=x_ref) for j, chip in enumerate(chips)]
        for cp in first:
            cp.start()
        passed = [copy(4 + j, (*chip, c), sibling) for j, chip in enumerate(chips)]
        for j, chip in enumerate(chips):
            copy(1 + j, (*chip, c), me).wait_recv()
            passed[j].start()
        copy(0, sibling, me).wait_recv()
        for j, chip in enumerate(chips):
            copy(4 + j, (*chip, 1 - c), me).wait_recv()
        for cp in first + passed:
            cp.wait_send()
        mine.wait()

    return _call(body, name=name, out_shape=_sds((N_DEV * m_per, n), block.dtype),
                 in_specs=[pl.BlockSpec(memory_space=pltpu.VMEM)], out_specs=pl.BlockSpec(memory_space=pltpu.VMEM),
                 scratch_shapes=[pltpu.SemaphoreType.DMA((7,)), pltpu.SemaphoreType.DMA((7,)), pltpu.SemaphoreType.DMA],
                 compiler_params=_params())(block)


GATHERED = (("w_ff_in", 0, -1, 0), ("w_ff_out", 0, -2, 0),
            ("w_in", None, -1, 1), ("w_conv_out", None, -1, 1), ("w_glu", None, -1, 1), ("w_pool_out", None, -1, 1),
            ("w_sb_out", None, -1, 1), ("w_out", None, -2, 1),
            ("w_ff_in", 1, -1, 2), ("w_ff_out", 1, -2, 2))
N_SUB = 3


def _lead(ref):
    return (slice(None),) * (len(ref.shape) - 2)


def _mo(v, m):
    return v if isinstance(v, int) else pl.multiple_of(v, m)


def _full_region(ref, axis, j, half, shard_shape):
    rs, cs = shard_shape[-2], shard_shape[-1]
    if axis == -1:
        r0, nr = (0, rs) if half is None else (half * (rs // 2), rs // 2)
        return ref.at[_lead(ref) + (pl.ds(_mo(r0, 16), nr), pl.ds(_mo(j * cs, 128), cs))]
    r0, nr = (j * rs, rs) if half is None else (j * rs + half * (rs // 2), rs // 2)
    return ref.at[_lead(ref) + (pl.ds(_mo(r0, 16), nr), slice(None))]


def _shard_half(ref, half):
    rs = ref.shape[-2]
    return ref.at[_lead(ref) + (pl.ds(_mo(half * (rs // 2), 16), rs // 2), slice(None))]


def _full_shape(shard_shape, axis):
    s = list(shard_shape)
    s[axis] *= N_CHIP
    return tuple(s)


class _Lay:
    def __init__(self, shard_shape, axis):
        self.axis = axis
        self.shard_shape = tuple(shard_shape)
        self.full_shape = _full_shape(shard_shape, axis)
        self.lead = int(np.prod(shard_shape[:-2]))
        self.rs, self.cs = shard_shape[-2], shard_shape[-1]
        self.hr = self.rs // 2
        self.tr = next(t for t in (512, 256, 128, 64, 32, 16) if self.hr % t == 0 and t * self.cs * 4 <= (1 << 21))
        self.half_rows_shape = _half_rows_shape(self.full_shape)
        self.half_shard_shape = _half_rows_shape(self.shard_shape)

    def full(self, jf, hf):
        if self.axis == -1:
            return ((self.lead, 2, self.hr, N_CHIP * self.cs),
                    pl.BlockSpec((None, None, self.tr, self.cs), lambda b, j, i, s: (b, hf(j, s), i, jf(j, s))))
        return ((self.lead, N_CHIP, 2, self.hr, self.cs),
                pl.BlockSpec((None, None, None, self.tr, self.cs), lambda b, j, i, s: (b, jf(j, s), hf(j, s), i, 0)))

    def half_shard(self):
        return (self.lead, self.hr, self.cs), pl.BlockSpec((None, self.tr, self.cs), lambda b, j, i, s: (b, i, 0))

    def shard(self, hf):
        return ((self.lead, 2, self.hr, self.cs),
                pl.BlockSpec((None, None, self.tr, self.cs), lambda b, j, i, s: (b, hf(j, s), i, 0)))


def _view_sum(sel, operands, out_view, out_shape, out_dtype, grid, name):
    k = len(operands)

    def body(sel_ref, *refs):
        acc = refs[0][...].astype(F32)
        for r in refs[1:k]:
            acc = acc + r[...].astype(F32)
        refs[k][...] = acc.astype(out_dtype)

    spec = pltpu.PrefetchScalarGridSpec(num_scalar_prefetch=1, grid=grid, in_specs=[v[1] for _, v in operands],
                                        out_specs=out_view[1])
    out = _call(body, name=name, grid_spec=spec, out_shape=_sds(out_view[0], out_dtype),
                compiler_params=_params(("parallel", "parallel", "parallel")))(
                    sel, *[a.reshape(v[0]) for a, v in operands])
    return out.reshape(out_shape)


def _sel_core(j, s):
    return s[0]


def _sel_chip(j, s):
    return s[1]


def _grid_j(j, s):
    return j


def _place_shard(lay, sel, w):
    return _view_sum(sel, [(w, lay.shard(_grid_j))], lay.full(_sel_chip, _grid_j), lay.full_shape, BF16,
                     (lay.lead, 2, lay.hr // lay.tr), "place_shard")


SEM_SPEC = pl.BlockSpec(memory_space=pltpu.SEMAPHORE)
ANY_SPEC = pl.BlockSpec(memory_space=pl.ANY)
SPLIT_COPY = pltpu.SideEffectType.DATAFLOW_SIDE_EFFECTING


def _in_hbm(a):
    return pltpu.with_memory_space_constraint(a, pltpu.HBM)


def _gather_copies(lays, bufs, send_sems, recv_sems):
    x, y, c, chips = _place()
    copies = []
    for a, lay in enumerate(lays):
        own = _full_region(bufs[a], lay.axis, _chip(x, y), c, lay.shard_shape)
        for k, chip in enumerate(chips):
            copies.append(pltpu.make_async_remote_copy(
                src_ref=own, dst_ref=own, send_sem=send_sems.at[a * 3 + k], recv_sem=recv_sems.at[a * 3 + k],
                device_id=(*chip, c), device_id_type=MESH))
    return copies


def _gather_start(fulls, after, lays, tag):
    n = len(fulls)

    def body(*refs):
        send_sems, recv_sems = refs[n + 1], refs[n + 2]
        bufs, token = refs[n + 3:2 * n + 3], refs[2 * n + 3]
        for cp in _gather_copies(lays, bufs, send_sems, recv_sems):
            cp.start()
        token[...] = jnp.zeros_like(token)

    outs = _call(body, name="gather_start_" + tag,
                 out_shape=[pltpu.SemaphoreType.DMA((3 * n,)), pltpu.SemaphoreType.DMA((3 * n,))]
                 + [pltpu.HBM(f.shape, f.dtype) for f in fulls] + [_sds((8, 128), F32)],
                 in_specs=[HBM_SPEC] * n + [ANY_SPEC],
                 out_specs=[SEM_SPEC, SEM_SPEC] + [HBM_SPEC] * n + [pl.BlockSpec(memory_space=pltpu.VMEM)],
                 input_output_aliases={a: a + 2 for a in range(n)},
                 compiler_params=pltpu.CompilerParams(has_side_effects=SPLIT_COPY))(*[_in_hbm(f) for f in fulls], after)
    return outs[0], outs[1], outs[2:2 + n], outs[2 + n]


def _gather_wait(send_sems, recv_sems, bufs, after, lays, tag):
    n = len(bufs)

    def body(*refs):
        ss, rs = refs[n], refs[n + 1]
        for cp in _gather_copies(lays, refs[n + 3:], ss, rs):
            cp.wait_send()
            cp.wait_recv()

    return _call(body, name="gather_wait_" + tag,
                 out_shape=[pltpu.HBM(b.shape, b.dtype) for b in bufs],
                 in_specs=[HBM_SPEC] * n + [SEM_SPEC, SEM_SPEC, ANY_SPEC], out_specs=[HBM_SPEC] * n,
                 input_output_aliases={a: a for a in range(n)},
                 compiler_params=pltpu.CompilerParams(has_side_effects=SPLIT_COPY))(*bufs, send_sems, recv_sems, after)


def _gather_forward(bufs, lays):
    n = len(bufs)

    def body(*refs):
        outs = refs[n:2 * n]
        send_sems, recv_sems = refs[2 * n:]
        x, y, c, chips = _place()
        sibling = (x, y, 1 - c)
        sends = []
        for a in range(n):
            for k, chip in enumerate(chips):
                landed = _full_region(outs[a], lays[a].axis, _chip(*chip), c, lays[a].shard_shape)
                cp = pltpu.make_async_remote_copy(
                    src_ref=landed, dst_ref=landed, send_sem=send_sems.at[a * 3 + k], recv_sem=recv_sems.at[a * 3 + k],
                    device_id=sibling, device_id_type=MESH)
                cp.start()
                sends.append(cp)
        for a in range(n):
            for k, chip in enumerate(chips):
                passed = _full_region(outs[a], lays[a].axis, _chip(*chip), 1 - c, lays[a].shard_shape)
                pltpu.make_async_remote_copy(
                    src_ref=passed, dst_ref=passed, send_sem=send_sems.at[a * 3 + k], recv_sem=recv_sems.at[a * 3 + k],
                    device_id=sibling, device_id_type=MESH).wait_recv()
        for cp in sends:
            cp.wait_send()

    return _call(body, name="gather_forward",
                 out_shape=[_sds(b.shape, b.dtype) for b in bufs],
                 in_specs=[HBM_SPEC] * n, out_specs=[HBM_SPEC] * n,
                 input_output_aliases={a: a for a in range(n)},
                 scratch_shapes=[pltpu.SemaphoreType.DMA((3 * n,)), pltpu.SemaphoreType.DMA((3 * n,))],
                 compiler_params=_params())(*bufs)


def _half_rows_shape(full_shape):
    s = list(full_shape)
    s[-2] //= 2
    return tuple(s)


RELATIONS = tuple((r, s) for r in range(N_CHIP) for s in range(2))[1:]


def _peer(x, y, c, rel):
    r, s = rel
    return (1 - x if r in (1, 3) else x, 1 - y if r in (2, 3) else y, 1 - c if s else c)


def _reduce_copies(lays, grads, landing, send_sems, recv_sems):
    x, y, c, _ = _place()
    nr = len(RELATIONS)
    copies = []
    for a, lay in enumerate(lays):
        for k, rel in enumerate(RELATIONS):
            px, py, pc = _peer(x, y, c, rel)
            copies.append(pltpu.make_async_remote_copy(
                src_ref=_full_region(grads[a], lay.axis, _chip(px, py), pc, lay.shard_shape), dst_ref=landing[a * nr + k],
                send_sem=send_sems.at[a * nr + k], recv_sem=recv_sems.at[a * nr + k],
                device_id=(px, py, pc), device_id_type=MESH))
    return copies


def _reduce_start(grads, after, lays, tag):
    n, nr = len(grads), len(RELATIONS)
    landing = [_in_hbm(lax.empty(lay.half_shard_shape, BF16)) for lay in lays for _ in RELATIONS]
    m = n + n * nr

    def body(*refs):
        send_sems, recv_sems = refs[m + 1], refs[m + 2]
        src, land, token = refs[m + 3:m + 3 + n], refs[m + 3 + n:2 * m + 3], refs[2 * m + 3]
        for cp in _reduce_copies(lays, src, land, send_sems, recv_sems):
            cp.start()
        token[...] = jnp.zeros_like(token)

    ops = [_in_hbm(g) for g in grads] + landing
    outs = _call(body, name="reduce_start_" + tag,
                 out_shape=[pltpu.SemaphoreType.DMA((n * nr,)), pltpu.SemaphoreType.DMA((n * nr,))]
                 + [pltpu.HBM(o.shape, o.dtype) for o in ops] + [_sds((8, 128), F32)],
                 in_specs=[HBM_SPEC] * m + [ANY_SPEC],
                 out_specs=[SEM_SPEC, SEM_SPEC] + [HBM_SPEC] * m + [pl.BlockSpec(memory_space=pltpu.VMEM)],
                 input_output_aliases={a: a + 2 for a in range(m)},
                 compiler_params=pltpu.CompilerParams(has_side_effects=SPLIT_COPY))(*ops, after)
    return outs[0], outs[1], outs[2:2 + n], outs[2 + n:2 + m], outs[2 + m]


def _reduce_wait(send_sems, recv_sems, grads, landing, after, lays, tag):
    n, nr = len(grads), len(RELATIONS)
    m = n + n * nr

    def body(*refs):
        ss, rs = refs[m], refs[m + 1]
        src, land = refs[m + 3:m + 3 + n], refs[m + 3 + n:]
        for cp in _reduce_copies(lays, src, land, ss, rs):
            cp.wait_send()
            cp.wait_recv()

    ops = list(grads) + list(landing)
    outs = _call(body, name="reduce_wait_" + tag,
                 out_shape=[pltpu.HBM(o.shape, o.dtype) for o in ops],
                 in_specs=[HBM_SPEC] * m + [SEM_SPEC, SEM_SPEC, ANY_SPEC], out_specs=[HBM_SPEC] * m,
                 input_output_aliases={a: a for a in range(m)},
                 compiler_params=pltpu.CompilerParams(has_side_effects=SPLIT_COPY))(*ops, send_sems, recv_sems, after)
    return outs[:n], [outs[n + nr * a:n + nr * a + nr] for a in range(n)]


def _share_halves(shards):
    n = len(shards)

    def body(*refs):
        outs = refs[n:2 * n]
        send_sems, recv_sems = refs[2 * n:]
        x, y, c, _ = _place()
        sibling = (x, y, 1 - c)
        started = []
        for a in range(n):
            mine = _shard_half(outs[a], c)
            rc = pltpu.make_async_remote_copy(src_ref=mine, dst_ref=mine, send_sem=send_sems.at[a],
                                              recv_sem=recv_sems.at[a], device_id=sibling, device_id_type=MESH)
            rc.start()
            started.append(rc)
        for rc in started:
            rc.wait_recv()
            rc.wait_send()

    return _call(body, name="share_halves", out_shape=[_sds(s.shape, F32) for s in shards],
                 in_specs=[HBM_SPEC] * n, out_specs=[HBM_SPEC] * n, input_output_aliases={a: a for a in range(n)},
                 scratch_shapes=[pltpu.SemaphoreType.DMA((n,)), pltpu.SemaphoreType.DMA((n,))],
                 compiler_params=_params())(*shards)


def _reduce_end(state, after, lays, sel, tag):
    send_sems, recv_sems, grads, landing, _ = state
    grads, landed = _reduce_wait(send_sems, recv_sems, grads, landing, after, lays, tag)
    halves = [
        _view_sum(sel, [(g, lay.full(_sel_chip, _sel_core))] + [(l, lay.half_shard()) for l in ls], lay.shard(_sel_core),
                  lay.shard_shape, F32, (lay.lead, 1, lay.hr // lay.tr), "shard_half_sum")
        for g, ls, lay in zip(grads, landed, lays)]
    return _share_halves(halves)


def _embed(blocks):
    n, r, c = blocks.shape
    eye = jnp.eye(n, dtype=blocks.dtype)
    return (blocks[:, :, None, :] * eye[:, None, :, None]).reshape(n * r, n * c)


def _unembed(mat, n):
    r, c = mat.shape[0] // n, mat.shape[1] // n
    return jnp.transpose(jnp.diagonal(mat.reshape(n, r, n, c), axis1=0, axis2=2), (2, 0, 1))


def _to_heads(a):
    return jnp.transpose(a.reshape(a.shape[0], HEADS, HD), (1, 0, 2))


def _from_heads(a):
    return jnp.transpose(a, (1, 0, 2)).reshape(a.shape[1], W)


def _row(v):
    return v.reshape(1, -1)


def _concat_cols(pieces):
    L = pieces[0].shape[0]
    widths = [p.shape[1] for p in pieces]
    tr = _wide_tile(L)

    def body(*refs):
        off = 0
        for r, w in zip(refs[:-1], widths):
            refs[-1][:, off:off + w] = r[...].astype(BF16)
            off += w

    return _call(body, name="concat_cols", grid=(L // tr,),
                 in_specs=[pl.BlockSpec((tr, w), lambda i: (i, 0)) for w in widths],
                 out_specs=pl.BlockSpec((tr, sum(widths)), lambda i: (i, 0)), out_shape=_sds((L, sum(widths)), BF16),
                 compiler_params=_params(("parallel",)))(*pieces)


def _ffn_fwd(x, ada, gp, gq, w_in, w_out, s):
    L = x.shape[0]
    h = _norm_mod(x, _row(gp[s]), _row(ada[3 * s]), _row(ada[3 * s + 1]))
    a, b, act = _ffn_in(h, w_in)
    f = _mm(act, w_out, M=L, N=D, K=FF, tm=min(L, 1024), tn=512, name="ffn_out")
    x2 = _post(x, f, _row(gq[s]), _row(ada[3 * s + 2]), 0.5)
    return x2, (x, h, a, b, act, f)


def _ffn_bwd(dx, saved, ada, gp, gq, w_in, w_out, s, stats):
    x, h, a, b, act, f = saved
    L = x.shape[0]
    df, stats = _post_bwd(dx, f, _row(gq[s]), _row(ada[3 * s + 2]), 0.5, stats, s)
    dw_out = _mm(act, df, M=FF, N=D, K=L, tm=1408, tn=1024, ta=True, out_dtype=BF16, name="ffn_dw_out")
    da, db = _ffn_mid_bwd(df, w_out, a, b)
    du = _concat_cols([da, db])
    dw_in = _mm(h, du, M=D, N=2 * FF, K=L, tm=1024, tn=512, ta=True, out_dtype=BF16, name="ffn_dw_in")
    dh = _mm(du, w_in, M=L, N=D, K=2 * FF, tm=min(L, 1024), tn=1024, tk=2816, tb=True, name="ffn_dh")
    dx2, stats = _norm_mod_bwd(dh, x, _row(gp[s]), _row(ada[3 * s + 1]), dx, stats, s)
    return dx2, dw_in, dw_out, stats


def _mixer_fwd(x, ada, gp, gq, wf, sm):
    L = x.shape[0]
    h = _norm_mod(x, _row(gp[1]), _row(ada[3]), _row(ada[4]))
    p = _mm(h, wf["w_in"], M=L, N=IN_COLS, K=D, tm=min(L, 2048), tn=512, name="mixer_in")
    za = _conv_fwd(p, sm["conv_w"])
    y, zb = _ssm_fwd(p, sm["b_re"], sm["b_im"], sm["c_re"], sm["c_im"], sm["abr"], sm["abi"], sm["fr"], sm["fi"], sm["ssm_d"])
    zc = _pool_fwd(p, sm["w_pool"], sm["pool_scale"])
    q = _to_heads(p[:, 5 * W:6 * W]) * (HD ** -0.5)
    k = _to_heads(p[:, 6 * W:7 * W])
    v = _to_heads(p[:, 7 * W:8 * W])
    o_heads, block_sums = _sb_fwd(q, k, v)
    zd = _from_heads(o_heads).astype(BF16)
    merged = _merge_fwd(za, zb, zc, zd, p, wf["w_conv_out"], wf["w_glu"], wf["w_pool_out"], wf["w_sb_out"])
    m = _mm(merged, wf["w_out"], M=L, N=D, K=D, tm=min(L, 1024), tn=512, name="mixer_out")
    x2 = _post(x, m, _row(gq[1]), _row(ada[5]), 1.0)
    return x2, (x, h, p, za, y, zb, zc, zd, q, k, v, block_sums, merged, m)


def _mixer_bwd(dx, saved, ada, gp, gq, wf, sm, stats):
    x, h, p, za, y, zb, zc, zd, q, k, v, block_sums, merged, m = saved
    L = x.shape[0]
    dmf, stats = _post_bwd(dx, m, _row(gq[1]), _row(ada[5]), 1.0, stats, 1)
    dw_out = _mm(merged, dmf, M=D, N=D, K=L, tm=512, tn=512, ta=True, out_dtype=BF16, name="mixer_dw_out")
    dmerged = _mm(dmf, wf["w_out"], M=L, N=D, K=D, tm=min(L, 1024), tn=512, tb=True, name="mixer_dmerged")
    dza, dzb, dzc, dzd, dgates, dwc, dwg, dwp, dws = _merge_bwd(
        dmerged, za, zb, zc, zd, p, wf["w_conv_out"], wf["w_glu"], wf["w_pool_out"], wf["w_sb_out"])
    dconv, dconv_w = _conv_bwd(p, sm["conv_w"], dza)
    (du_ssm, dd, dbr, dbi, dcr, dci, gar, gai, gfr, gfi) = _ssm_bwd(
        p, y, dzb, sm["b_re"], sm["b_im"], sm["c_re"], sm["c_im"], sm["abr"], sm["abi"], sm["fr"], sm["fi"], sm["ssm_d"])
    du_pool, dwpool, dpscale = _pool_bwd(p, sm["w_pool"], sm["pool_scale"], dzc)
    dq, dk, dv = _sb_bwd(q, k, v, _to_heads(dzd), block_sums)
    dqkv = [_from_heads(t) for t in (dq * (HD ** -0.5), dk, dv)]
    dp = _concat_cols([dconv, du_ssm, du_pool] + dqkv + [dgates])
    dw_in = _mm(h, dp, M=D, N=IN_COLS, K=L, tm=1024, tn=512, ta=True, out_dtype=BF16, name="mixer_dw_in")
    dh = _mm(dp, wf["w_in"], M=L, N=D, K=IN_COLS, tm=min(L, 1024), tn=1024, tk=3072, tb=True, name="mixer_dh")
    dx2, stats = _norm_mod_bwd(dh, x, _row(gp[1]), _row(ada[4]), dx, stats, 1)
    wgrads = [dw_in, dwc, dwg, dwp, dws, dw_out]
    small = {"conv_w": dconv_w, "ssm_d": dd, "b_re": dbr, "b_im": dbi, "c_re": dcr, "c_im": dci,
             "abr": gar, "abi": gai, "fr": gfr, "fi": gfi, "w_pool": dwpool, "pool_scale": dpscale}
    return dx2, wgrads, small, stats


def _pack(arrays):
    flat = jnp.concatenate([a.reshape(-1) for a in arrays])
    rows = -(-flat.shape[0] // 128)
    rows = -(-rows // 64) * 64
    return jnp.pad(flat, (0, rows * 128 - flat.shape[0])).reshape(rows, 128)


def _unpack(block, shapes):
    flat = block.reshape(-1)
    out, off = [], 0
    for s in shapes:
        n = int(np.prod(s))
        out.append(flat[off:off + n].reshape(s))
        off += n
    return out


def _pad_rows(a, mult):
    rows = -(-a.shape[0] // mult) * mult
    return jnp.concatenate([a] * (-(-rows // a.shape[0])), axis=0)[:rows]


SMALL_ORDER = ("stats", "conv_w", "lam_re", "lam_im", "log_dt", "ssm_b_re", "ssm_b_im",
               "ssm_c_re", "ssm_c_im", "ssm_d", "w_pool", "pool_scale")
WEIGHTS = ('w_ada', 'b_ada', 'g_pre', 'g_post', 'w_ff_in', 'w_ff_out', 'w_in', 'conv_w', 'w_conv_out', 'lam_re', 'lam_im',
           'log_dt', 'ssm_b_re', 'ssm_b_im', 'ssm_c_re', 'ssm_c_im', 'ssm_d', 'w_glu', 'w_pool', 'pool_scale', 'w_pool_out',
           'w_sb_out', 'w_out')


def _step(a):
    depth = a["w_ada"].shape[0]
    x = a["x"][0]
    target = a["loss_target"][0]
    L = x.shape[0]
    ix, iy, ic = lax.axis_index("x"), lax.axis_index("y"), lax.axis_index("c")
    chip = 2 * ix + iy
    me = 4 * ix + 2 * iy + ic
    sel = jnp.stack([ic, chip]).astype(jnp.int32)
    lays = [_Lay(a[name].shape[(1 if idx is None else 2):], ax) for name, idx, ax, _ in GATHERED]

    def entries(g):
        return [i for i, e in enumerate(GATHERED) if e[3] == g]

    def shard_of(i, l):
        name, idx = GATHERED[i][0], GATHERED[i][1]
        return a[name][l] if idx is None else a[name][l, idx]

    stages = [(l, g) for l in range(depth) for g in range(N_SUB)]

    def gather_begin(t, after):
        l, g = stages[t]
        placed = [_place_shard(lays[i], sel, shard_of(i, l)) for i in entries(g)]
        return _gather_start(placed, after, [lays[i] for i in entries(g)], str(t))

    pending = {t: gather_begin(t, x) for t in range(min(2, len(stages)))}
    started = sum(p[3][0, 0] for p in pending.values())

    first_shapes = [(D,), (depth, 3, W), (depth, 3, W), (depth, 3, W // N_CHIP)]
    gathered = _allgather8(_pack([a["c"] + started, a["g_pre"], a["g_post"], a["conv_w"]]), "gather_small_inputs")
    per_dev = [_unpack(blk, first_shapes) for blk in gathered.reshape(N_DEV, -1, 128)]
    c_all = jnp.stack([d[0] for d in per_dev])
    c_pad = jnp.concatenate([c_all, jnp.zeros_like(c_all)], axis=0)
    g_pre = jnp.concatenate([per_dev[2 * j][1] for j in range(N_CHIP)], axis=-1)
    g_post = jnp.concatenate([per_dev[2 * j][2] for j in range(N_CHIP)], axis=-1)
    conv_w = jnp.concatenate([per_dev[2 * j][3] for j in range(N_CHIP)], axis=-1)

    b_cols = lax.dynamic_slice(a["b_ada"], (0, chip * ADA_SHARD), (depth, ADA_SHARD)).reshape(depth, 1, ADA_SHARD)
    ada_part = _ada_fwd(c_pad, a["w_ada"], b_cols)
    ada_all = _allgather8(ada_part.reshape(depth * 16, ADA_SHARD), "gather_ada").reshape(N_DEV, depth, 16, ADA_SHARD)
    ada_rows = lax.dynamic_slice(ada_all, (0, 0, me, 0), (N_DEV, depth, 1, ADA_SHARD))[:, :, 0]
    ada = jnp.concatenate([ada_rows[2 * j] for j in range(N_CHIP)], axis=-1).reshape(depth, 9, D)

    lam_re = _pad_rows(a["lam_re"].reshape(depth, NST), 8)
    lam_im = _pad_rows(a["lam_im"].reshape(depth, NST), 8)
    log_dt_x = _pad_rows(jnp.repeat(a["log_dt"], GP, axis=1), 8)
    abr, abi, fr, fi = _ssm_prep(lam_re, lam_im, log_dt_x)

    def small_of(l):
        return {"conv_w": conv_w[l], "ssm_d": _row(a["ssm_d"][l]), "pool_scale": _row(a["pool_scale"][l]),
                "b_re": _embed(jnp.transpose(a["ssm_b_re"][l], (0, 2, 1))), "b_im": _embed(jnp.transpose(a["ssm_b_im"][l], (0, 2, 1))),
                "c_re": _embed(jnp.transpose(a["ssm_c_re"][l], (0, 2, 1))), "c_im": _embed(jnp.transpose(a["ssm_c_im"][l], (0, 2, 1))),
                "w_pool": _embed(a["w_pool"][l]),
                "abr": abr[l:l + 1], "abi": abi[l:l + 1], "fr": fr[l:l + 1], "fi": fi[l:l + 1]}

    saved, weights, smalls = [], [], [small_of(l) for l in range(depth)]
    for t, (l, g) in enumerate(stages):
        glays = [lays[i] for i in entries(g)]
        send_sems, recv_sems, bufs, _ = pending.pop(t)
        w = _gather_forward(_gather_wait(send_sems, recv_sems, bufs, x if t else ada, glays, str(t)), glays)
        weights.append(w)
        ada_l = ada[l]
        if t + 2 < len(stages):
            pending[t + 2] = gather_begin(t + 2, x)
            ada_l = ada_l + pending[t + 2][3][0, 0]
        if g == 1:
            wf = {GATHERED[i][0]: wi for i, wi in zip(entries(1), w)}
            x, sv = _mixer_fwd(x, ada_l, g_pre[l], g_post[l], wf, smalls[l])
        else:
            x, sv = _ffn_fwd(x, ada_l, g_pre[l], g_post[l], w[0], w[1], g)
        saved.append(sv)
    dx, loss_part = _loss_head(x, target)
    loss = lax.psum(loss_part[0, 0], ("x", "y", "c"))

    shard_grads = [[None] * depth for _ in GATHERED]
    small_grads = [{} for _ in range(depth)]
    stats = [jnp.zeros((STAT_ROWS, D), F32) for _ in range(depth)]
    states = {}

    def reduce_finish(t, after):
        l, g = stages[t]
        glays = [lays[i] for i in entries(g)]
        for i, grad in zip(entries(g), _reduce_end(states.pop(t), after, glays, sel, str(t))):
            shard_grads[i][l] = grad

    token = None
    for t in reversed(range(len(stages))):
        l, g = stages[t]
        ada_l = ada[l] if token is None else ada[l] + token[0, 0]
        if g == 1:
            wf = {GATHERED[i][0]: wi for i, wi in zip(entries(1), weights[t])}
            dx, wgrads, small, stats[l] = _mixer_bwd(dx, saved[t], ada_l, g_pre[l], g_post[l], wf, smalls[l], stats[l])
            small_grads[l].update(small)
        else:
            dx, dw_in, dw_out, stats[l] = _ffn_bwd(dx, saved[t], ada_l, g_pre[l], g_post[l], weights[t][0], weights[t][1], g,
                                                   stats[l])
            wgrads = [dw_in, dw_out]
        if t:
            states[t] = _reduce_start(wgrads, dx, [lays[i] for i in entries(g)], str(t))
            token = states[t][4]
        else:
            last = (wgrads, [lays[i] for i in entries(g)])
        if t + 2 in states:
            reduce_finish(t + 2, dx)
    stack = lambda key: _pad_rows(jnp.concatenate([small_grads[l][key] for l in range(depth)], axis=0), 8)
    gs = np.zeros((NST, 128), np.float32)
    gs[np.arange(NST), np.arange(NST) // GP] = 1.0
    dlr, dli, dldt = _ssm_prep_bwd(lam_re, lam_im, log_dt_x, stack("abr"), stack("abi"), stack("fr"), stack("fi"), jnp.asarray(gs))
    part = {
        "stats": jnp.stack(stats) + sum(st[4][0, 0] for st in states.values()),
        "conv_w": jnp.stack([small_grads[l]["conv_w"] for l in range(depth)]),
        "lam_re": dlr[:depth].reshape(depth, G, GP), "lam_im": dli[:depth].reshape(depth, G, GP), "log_dt": dldt[:depth, :G],
        "ssm_b_re": jnp.stack([jnp.transpose(_unembed(small_grads[l]["b_re"], G), (0, 2, 1)) for l in range(depth)]),
        "ssm_b_im": jnp.stack([jnp.transpose(_unembed(small_grads[l]["b_im"], G), (0, 2, 1)) for l in range(depth)]),
        "ssm_c_re": jnp.stack([jnp.transpose(_unembed(small_grads[l]["c_re"], G), (0, 2, 1)) for l in range(depth)]),
        "ssm_c_im": jnp.stack([jnp.transpose(_unembed(small_grads[l]["c_im"], G), (0, 2, 1)) for l in range(depth)]),
        "ssm_d": jnp.stack([small_grads[l]["ssm_d"][0] for l in range(depth)]),
        "w_pool": jnp.stack([_unembed(small_grads[l]["w_pool"], len(POOL_WINDOWS)) for l in range(depth)]),
        "pool_scale": jnp.stack([small_grads[l]["pool_scale"][0] for l in range(depth)]),
    }
    small_shapes = [part[k].shape for k in SMALL_ORDER]
    blocks = _allgather8(_pack([part[k] for k in SMALL_ORDER]), "gather_small_grads").reshape(N_DEV, -1, 128)
    states[0] = _reduce_start(last[0], blocks, last[1], "0")
    behind = states[0][4][0, 0]
    small_sum = _sum_parts([blocks[i] for i in range(N_DEV)], F32, "small_grad_sum") + behind
    total = dict(zip(SMALL_ORDER, _unpack(small_sum, small_shapes)))
    d_ada_all = jnp.stack([_unpack(blocks[i], small_shapes[:1])[0][:, :9].reshape(depth, 9 * D) for i in range(N_DEV)])
    d_cols = lax.dynamic_slice(d_ada_all, (0, 0, chip * ADA_SHARD), (N_DEV, depth, ADA_SHARD))
    d_cols = jnp.transpose(d_cols, (1, 0, 2)) + behind
    grads = {"w_ada": _ada_wgrad(c_pad, jnp.concatenate([d_cols, jnp.zeros_like(d_cols)], axis=1)),
             "b_ada": total["stats"][:, :9].reshape(depth, 9 * D),
             "g_pre": lax.dynamic_slice(total["stats"], (0, 9, chip * W), (depth, 3, W)),
             "g_post": lax.dynamic_slice(total["stats"], (0, 12, chip * W), (depth, 3, W)),
             "conv_w": lax.dynamic_slice(total["conv_w"], (0, 0, chip * (W // N_CHIP)), (depth, 3, W // N_CHIP))}
    for k in SMALL_ORDER[2:]:
        grads[k] = total[k]

    out = {"loss": loss, "grad_x": dx[None]}

    def update(name):
        out["grad_" + name] = grads[name]
        out["delta_" + name], out["new_m_" + name], out["new_v_" + name] = _adamw(a[name], grads[name], a["m_" + name], a["v_" + name])

    for name in WEIGHTS:
        if name in grads:
            update(name)
    for t in sorted(states, reverse=True):
        reduce_finish(t, out["delta_w_ada"])
    for name in sorted({e[0] for e in GATHERED}):
        cols = [shard_grads[i] for i, e in enumerate(GATHERED) if e[0] == name]
        grads[name] = jnp.stack(cols[0]) if len(cols) == 1 else jnp.stack([jnp.stack(pair) for pair in zip(*cols)])
        update(name)
    return out


def kernel(x, c, w_ada, b_ada, g_pre, g_post, w_ff_in, w_ff_out, w_in, conv_w, w_conv_out, lam_re, lam_im, log_dt, ssm_b_re, ssm_b_im, ssm_c_re, ssm_c_im, ssm_d, w_glu, w_pool, pool_scale, w_pool_out, w_sb_out, w_out, loss_target, m_w_ada, m_b_ada, m_g_pre, m_g_post, m_w_ff_in, m_w_ff_out, m_w_in, m_conv_w, m_w_conv_out, m_lam_re, m_lam_im, m_log_dt, m_ssm_b_re, m_ssm_b_im, m_ssm_c_re, m_ssm_c_im, m_ssm_d, m_w_glu, m_w_pool, m_pool_scale, m_w_pool_out, m_w_sb_out, m_w_out, v_w_ada, v_b_ada, v_g_pre, v_g_post, v_w_ff_in, v_w_ff_out, v_w_in, v_conv_w, v_w_conv_out, v_lam_re, v_lam_im, v_log_dt, v_ssm_b_re, v_ssm_b_im, v_ssm_c_re, v_ssm_c_im, v_ssm_d, v_w_glu, v_w_pool, v_pool_scale, v_w_pool_out, v_w_sb_out, v_w_out):
    out = _step(dict(locals()))
    names = ["loss", "grad_x"] + [p + n for p in ("grad_", "delta_", "new_m_", "new_v_") for n in WEIGHTS]
    return tuple(out[n] for n in names)
```
